```python
import math
import jax, jax.numpy as jnp
from jax import lax
import numpy as np

D_MODEL = 2048
BATCH = 1
SEQ = 8192
DEPTH = 2
DEC_BATCH = 32
DEC_SEQ = 32
PAST_LEN = 1024

CHUNK = 64
Q_BLOCK = 128
HEAD_DIM = 64
ROPE_DIM = HEAD_DIM // 4
ROPE_THETA = 500000.0
N_BRANCH = 4
BRANCH_WIDTH = D_MODEL // N_BRANCH

A_HEADS = BRANCH_WIDTH // HEAD_DIM
A_KV_HEADS = 2
A_IDX_HEADS = 4
A_IDX_DIM = 64
A_TOPK_MAX = 256

B_HEADS = BRANCH_WIDTH // HEAD_DIM
B_PAST_CHUNKS = 8
B_WINDOW = B_PAST_CHUNKS * CHUNK
B_REL_CLIP = 128

C_HEADS = BRANCH_WIDTH // (2 * HEAD_DIM)
C_VDIM = 2 * HEAD_DIM

D_HEADS = BRANCH_WIDTH // HEAD_DIM
D_W_LORA = 64
D_A_LORA = 64
D_G_LORA = 128
D_GN_EPS = 64e-5
D_WIDTHS = (BRANCH_WIDTH, BRANCH_WIDTH, BRANCH_WIDTH, D_W_LORA, D_A_LORA, D_G_LORA)
D_COLS = 3 * BRANCH_WIDTH + D_W_LORA + D_A_LORA + D_G_LORA

FFN_HIDDEN = 4 * D_MODEL
PLE_DIM = 256
NORM_EPS = 1e-6

IN_WIDTHS = (A_HEADS * HEAD_DIM, A_KV_HEADS * HEAD_DIM, A_KV_HEADS * HEAD_DIM,
             A_IDX_HEADS * A_IDX_DIM, A_IDX_DIM, A_IDX_HEADS,
             B_HEADS * HEAD_DIM, B_HEADS * HEAD_DIM, B_HEADS * HEAD_DIM,
             C_HEADS * 2 * HEAD_DIM, C_HEADS * 2 * HEAD_DIM, C_HEADS * C_VDIM,
             D_COLS, N_BRANCH * D_MODEL)
N_IN = sum(IN_WIDTHS)

kernel_name = 'hybrid_streaming_encoder_step'


def rmsnorm(x, g):
    xf = x.astype(jnp.float32)
    y = xf * lax.rsqrt(jnp.mean(xf * xf, axis=-1, keepdims=True) + NORM_EPS)
    return (y * g.astype(jnp.float32)).astype(x.dtype)


def partial_rope(x, pos):
    half = ROPE_DIM // 2
    inv_freq = ROPE_THETA ** (-jnp.arange(0, ROPE_DIM, 2, dtype=jnp.float32) / ROPE_DIM)
    ang = pos.astype(jnp.float32)[:, None] * inv_freq[None, :]
    shape = (pos.shape[0],) + (1,) * (x.ndim - 3) + (half,)
    cos = jnp.cos(ang).reshape(shape)
    sin = jnp.sin(ang).reshape(shape)
    xf = x.astype(jnp.float32)
    x1, x2 = xf[..., :half], xf[..., half:ROPE_DIM]
    out = jnp.concatenate([x1 * cos - x2 * sin, x1 * sin + x2 * cos, xf[..., ROPE_DIM:]], axis=-1)
    return out.astype(x.dtype)


def split_cols(z, widths):
    return jnp.split(z, [int(c) for c in np.cumsum(widths)[:-1]], axis=-1)


def map_query_blocks(fn, qs, pos):
    B, T = qs[0].shape[:2]
    nb = T // Q_BLOCK
    blk = lambda a: jnp.swapaxes(a.reshape((B, nb, Q_BLOCK) + a.shape[2:]), 0, 1)
    out = lax.map(lambda args: fn(*args), tuple(blk(a) for a in qs) + (pos.reshape(nb, Q_BLOCK),))
    return jnp.swapaxes(out, 0, 1).reshape((B, T) + out.shape[3:])


def dsa_core(q, iq, iw, q_pos, k, v, ik, k_pos, topk):
    f32 = jnp.float32
    logits = jnp.einsum('bqhd,bsd->bqhs', iq.astype(f32), ik.astype(f32)) * (A_IDX_DIM ** -0.5)
    score = jnp.einsum('bqh,bqhs->bqs', iw.astype(f32) * (A_IDX_HEADS ** -0.5), jax.nn.relu(logits))
    admissible = (k_pos[None, :] // CHUNK) <= (q_pos[:, None] // CHUNK)
    score = jnp.where(admissible[None], score, -jnp.inf)
    _, sel = lax.top_k(score, topk)
    sel_ok = (k_pos[sel] // CHUNK) <= (q_pos[None, :, None] // CHUNK)
    gather = jax.vmap(lambda arr, ix: arr[ix])
    ks = gather(k, sel).astype(f32)
    vs = gather(v, sel).astype(f32)
    B, Tq = q.shape[:2]
    qg = q.reshape(B, Tq, A_KV_HEADS, A_HEADS // A_KV_HEADS, HEAD_DIM).astype(f32)
    s = jnp.einsum('bqgrd,bqsgd->bqgrs', qg, ks) * (HEAD_DIM ** -0.5)
    s = jnp.where(sel_ok[:, :, None, None, :], s, -jnp.inf)
    p = jax.nn.softmax(s, axis=-1)
    o = jnp.einsum('bqgrs,bqsgd->bqgrd', p, vs)
    return o.reshape(B, Tq, A_HEADS * HEAD_DIM).astype(q.dtype)


def band_core(q, k, v, q_pos, k_pos, table):
    f32 = jnp.float32
    s = jnp.einsum('...qhd,...khd->...hqk', q.astype(f32), k.astype(f32)) * (HEAD_DIM ** -0.5)
    rel = k_pos[..., None, :] - q_pos[..., :, None]
    bias = jnp.moveaxis(table.astype(f32)[:, jnp.clip(rel, -B_REL_CLIP, B_REL_CLIP) + B_REL_CLIP], 0, -3)
    qc = (q_pos // CHUNK)[..., :, None]
    kc = (k_pos // CHUNK)[..., None, :]
    ok = (k_pos[..., None, :] >= 0) & (kc <= qc) & (kc >= qc - B_PAST_CHUNKS)
    s = jnp.where(ok[..., None, :, :], s + bias, -jnp.inf)
    p = jax.nn.softmax(s, axis=-1)
    return jnp.einsum('...hqk,...khd->...qhd', p, v.astype(f32))


def diff_core(q, k, v, q_pos, k_pos, lam):
    f32 = jnp.float32
    s = jnp.einsum('bqhcd,bkhcd->bhcqk', q.astype(f32), k.astype(f32)) * (HEAD_DIM ** -0.5)
    ok = (k_pos[None, :] // CHUNK) <= (q_pos[:, None] // CHUNK)
    s = jnp.where(ok, s, -jnp.inf)
    p = jax.nn.softmax(s, axis=-1)
    attn = p[:, :, 0] - lam * p[:, :, 1]
    return jnp.einsum('bhqk,bkhe->bqhe', attn, v.astype(f32))


def rwkv7_time_mix(zd, zprev, wkv0, lw):
    B, T, _ = zd.shape
    f32 = jnp.float32
    zf = zd.astype(f32)
    zsh = jnp.concatenate([zprev.astype(f32), zf[:, :-1]], axis=1)
    zm = zf + (zsh - zf) * lw['d_mu'].astype(f32)
    r, k, v, wl, al, gl = split_cols(zm, D_WIDTHS)
    w = -jax.nn.softplus(-(lw['d_w0'].astype(f32) + jnp.tanh(wl) @ lw['d_w2'].astype(f32))) - 0.5
    decay = jnp.exp(-jnp.exp(w))
    a = jax.nn.sigmoid(lw['d_a0'].astype(f32) + al @ lw['d_a2'].astype(f32))
    g = jax.nn.sigmoid(gl) @ lw['d_g2'].astype(f32)
    heads = lambda t: t.reshape(B, T, D_HEADS, HEAD_DIM)
    kk = heads(k * lw['d_kk'].astype(f32))
    kk = kk / jnp.maximum(jnp.linalg.norm(kk, axis=-1, keepdims=True), 1e-12)
    k = k * (1.0 + (a - 1.0) * lw['d_ka'].astype(f32))
    r, k, v, decay, a = heads(r), heads(k), heads(v), heads(decay), heads(a)

    def step(S, inp):
        r_t, w_t, k_t, v_t, kk_t, a_t = inp
        sa = jnp.einsum('bhvk,bhk->bhv', S, -kk_t)
        S = S * w_t[:, :, None, :] + sa[..., None] * (kk_t * a_t)[:, :, None, :] + v_t[..., None] * k_t[:, :, None, :]
        return S, jnp.einsum('bhvk,bhk->bhv', S, r_t)

    tm = lambda t: jnp.moveaxis(t, 1, 0)
    S_fin, ys = lax.scan(step, wkv0.astype(f32), (tm(r), tm(decay), tm(k), tm(v), tm(kk), tm(a)))
    y = jnp.moveaxis(ys, 0, 1)
    mean = jnp.mean(y, axis=-1, keepdims=True)
    var = jnp.mean(jnp.square(y - mean), axis=-1, keepdims=True)
    yn = ((y - mean) * lax.rsqrt(var + D_GN_EPS)).reshape(B, T, BRANCH_WIDTH)
    yn = yn * lw['d_ln_w'].astype(f32) + lw['d_ln_b'].astype(f32)
    bonus = (jnp.sum(r * k * lw['d_rk'].astype(f32), axis=-1, keepdims=True) * v).reshape(B, T, BRANCH_WIDTH)
    out = (yn + bonus) * g
    return out.astype(zd.dtype), S_fin.astype(wkv0.dtype), zd[:, -1:]


def trunk_layer(l, x, pe, pos, cache, lw):
    B, T, _ = x.shape
    f32 = jnp.float32
    h = rmsnorm(x, lw['norm1'])
    z = h @ lw['w_in']
    (aq, ak, av, aiq, aik, aiw, bq, bk, bv, cq, ck, cv, zd, zg) = split_cols(z, IN_WIDTHS)
    aq = partial_rope(rmsnorm(aq.reshape(B, T, A_HEADS, HEAD_DIM), lw['a_qn']), pos)
    ak = partial_rope(rmsnorm(ak.reshape(B, T, A_KV_HEADS, HEAD_DIM), lw['a_kn']), pos)
    av = av.reshape(B, T, A_KV_HEADS, HEAD_DIM)
    aiq = partial_rope(aiq.reshape(B, T, A_IDX_HEADS, A_IDX_DIM), pos)
    aik = partial_rope(aik, pos)
    bq = rmsnorm(bq.reshape(B, T, B_HEADS, HEAD_DIM), lw['b_qn'])
    bk = rmsnorm(bk.reshape(B, T, B_HEADS, HEAD_DIM), lw['b_kn'])
    bv = bv.reshape(B, T, B_HEADS, HEAD_DIM)
    cq = partial_rope(rmsnorm(cq.reshape(B, T, C_HEADS, 2, HEAD_DIM), lw['c_qn']), pos)
    ck = partial_rope(rmsnorm(ck.reshape(B, T, C_HEADS, 2, HEAD_DIM), lw['c_kn']), pos)
    cv = cv.reshape(B, T, C_HEADS, C_VDIM)
    lam_init = 0.8 - 0.6 * math.exp(-0.3 * l)
    lv = lw['c_lam'].astype(f32)
    lam = jnp.exp(jnp.sum(lv[0] * lv[1])) - jnp.exp(jnp.sum(lv[2] * lv[3])) + lam_init

    if cache is None:
        topk = min(A_TOPK_MAX, T // 4)
        oa = map_query_blocks(lambda qb, iqb, iwb, pb: dsa_core(qb, iqb, iwb, pb, ak, av, aik, pos, topk),
                              (aq, aiq, aiw), pos)
        nc = T // CHUNK
        band = jnp.arange(nc)[:, None] + jnp.arange(B_PAST_CHUNKS + 1)[None, :]

        def banded(a):
            ac = a.reshape((B, nc, CHUNK) + a.shape[2:])
            ap = jnp.pad(ac, ((0, 0), (B_PAST_CHUNKS, 0)) + ((0, 0),) * (ac.ndim - 2))
            return ap[:, band].reshape((B, nc, (B_PAST_CHUNKS + 1) * CHUNK) + a.shape[2:])

        kpos_band = (((band - B_PAST_CHUNKS) * CHUNK)[:, :, None]
                     + jnp.arange(CHUNK)[None, None, :]).reshape(nc, (B_PAST_CHUNKS + 1) * CHUNK)
        ob = band_core(bq.reshape(B, nc, CHUNK, B_HEADS, HEAD_DIM), banded(bk), banded(bv),
                       pos.reshape(nc, CHUNK), kpos_band, lw['b_rel']).reshape(B, T, B_HEADS, HEAD_DIM)
        oc = map_query_blocks(lambda qb, pb: diff_core(qb, ck, cv, pb, pos, lam), (cq,), pos)
        wkv0 = jnp.zeros((B, D_HEADS, HEAD_DIM, HEAD_DIM), x.dtype)
        zprev = jnp.zeros((B, 1, D_COLS), x.dtype)
        keep = min(B_WINDOW, T)
        rows = (ak, av, aik, bk[:, T - keep:], bv[:, T - keep:], ck, cv)
    else:
        (ca_k, ca_v, ca_ik, cb_k, cb_v, cc_k, cc_v, wkv0, zprev) = cache
        past = ca_k.shape[1]
        kpos = jnp.arange(past + T, dtype=jnp.int32)
        topk = min(A_TOPK_MAX, (past + T) // 4)
        oa = dsa_core(aq, aiq, aiw, pos, jnp.concatenate([ca_k, ak], axis=1), jnp.concatenate([ca_v, av], axis=1),
                      jnp.concatenate([ca_ik, aik], axis=1), kpos, topk)
        pb = cb_k.shape[1]
        kpos_b = jnp.arange(past - pb, past + T, dtype=jnp.int32)
        ob = band_core(bq, jnp.concatenate([cb_k, bk], axis=1), jnp.concatenate([cb_v, bv], axis=1),
                       pos, kpos_b, lw['b_rel'])
        oc = diff_core(cq, jnp.concatenate([cc_k, ck], axis=1), jnp.concatenate([cc_v, cv], axis=1),
                       pos, kpos, lam)
        rows = (ak, av, aik, bk, bv, ck, cv)

    od, wkv_new, shift_new = rwkv7_time_mix(zd, zprev, wkv0, lw)
    oc = rmsnorm(oc, lw['c_on']) * (1.0 - lam_init)
    o = jnp.stack([oa.reshape(B, T, BRANCH_WIDTH).astype(x.dtype),
                   ob.reshape(B, T, BRANCH_WIDTH).astype(x.dtype),
                   oc.reshape(B, T, BRANCH_WIDTH).astype(x.dtype),
                   od], axis=2)
    u = jnp.einsum('btnc,ncd->btnd', o, lw['w_br'])
    gate = jax.nn.sigmoid(zg.reshape(B, T, N_BRANCH, D_MODEL))
    x = x + jnp.sum(u * gate, axis=2) @ lw['w_out']
    h2 = rmsnorm(x, lw['norm2'])
    x = x + jnp.square(jax.nn.relu(h2 @ lw['w_up'])) @ lw['w_down']
    x = x + (pe @ lw['w_ple']) * jax.nn.sigmoid(rmsnorm(x, lw['norm3']) @ lw['w_ple_gate'])
    return x, rows + (wkv_new, shift_new)


def setup_inputs(seed: int = 0) -> dict:
    key = jax.random.key(seed)
    ks = iter(jax.random.split(key, 64))
    nrm = lambda shape, scale=1.0: scale * jax.random.normal(next(ks), shape, jnp.float32)
    gain = lambda shape: 1.0 + 0.02 * jax.random.normal(next(ks), shape, jnp.float32)
    unif = lambda shape, lo, hi: jax.random.uniform(next(ks), shape, jnp.float32, lo, hi)
    b_past = min(B_WINDOW, PAST_LEN)
    L = DEPTH
    return {
        'x_prompt': nrm((BATCH, SEQ, D_MODEL)),
        'x_sample': nrm((DEC_BATCH, DEC_SEQ, D_MODEL)),
        'cache_a_k': nrm((L, DEC_BATCH, PAST_LEN, A_KV_HEADS, HEAD_DIM)),
        'cache_a_v': nrm((L, DEC_BATCH, PAST_LEN, A_KV_HEADS, HEAD_DIM)),
        'cache_a_kidx': nrm((L, DEC_BATCH, PAST_LEN, A_IDX_DIM)),
        'cache_b_k': nrm((L, DEC_BATCH, b_past, B_HEADS, HEAD_DIM)),
        'cache_b_v': nrm((L, DEC_BATCH, b_past, B_HEADS, HEAD_DIM)),
        'cache_c_k': nrm((L, DEC_BATCH, PAST_LEN, C_HEADS, 2, HEAD_DIM)),
        'cache_c_v': nrm((L, DEC_BATCH, PAST_LEN, C_HEADS, C_VDIM)),
        'state_d_wkv': nrm((L, DEC_BATCH, D_HEADS, HEAD_DIM, HEAD_DIM), 0.1),
        'state_d_shift': nrm((L, DEC_BATCH, 1, D_COLS)),
        'p_prompt': nrm((L, BATCH, SEQ, PLE_DIM)),
        'p_sample': nrm((L, DEC_BATCH, DEC_SEQ, PLE_DIM)),
        'norm1_g': gain((L, D_MODEL)),
        'w_in': nrm((L, D_MODEL, N_IN), D_MODEL ** -0.5),
        'a_q_norm': gain((L, HEAD_DIM)),
        'a_k_norm': gain((L, HEAD_DIM)),
        'b_q_norm': gain((L, HEAD_DIM)),
        'b_k_norm': gain((L, HEAD_DIM)),
        'b_rel_bias': nrm((L, B_HEADS, 2 * B_REL_CLIP + 1), 0.5),
        'c_q_norm': gain((L, 2, HEAD_DIM)),
        'c_k_norm': gain((L, 2, HEAD_DIM)),
        'c_lambda': nrm((L, 4, HEAD_DIM), 0.1),
        'c_out_norm': gain((L, C_VDIM)),
        'd_mu': unif((L, D_COLS), 0.0, 1.0),
        'd_w0': unif((L, BRANCH_WIDTH), -6.0, -1.0),
        'd_w2': nrm((L, D_W_LORA, BRANCH_WIDTH), 0.1 * D_W_LORA ** -0.5),
        'd_a0': nrm((L, BRANCH_WIDTH), 0.1),
        'd_a2': nrm((L, D_A_LORA, BRANCH_WIDTH), 0.1 * D_A_LORA ** -0.5),
        'd_g2': nrm((L, D_G_LORA, BRANCH_WIDTH), D_G_LORA ** -0.5),
        'd_k_k': 1.0 + nrm((L, BRANCH_WIDTH), 0.1),
        'd_k_a': 1.0 + nrm((L, BRANCH_WIDTH), 0.1),
        'd_r_k': nrm((L, D_HEADS, HEAD_DIM), 0.1),
        'd_ln_w': gain((L, BRANCH_WIDTH)),
        'd_ln_b': nrm((L, BRANCH_WIDTH), 0.02),
        'w_branch': nrm((L, N_BRANCH, BRANCH_WIDTH, D_MODEL), BRANCH_WIDTH ** -0.5),
        'w_out': nrm((L, D_MODEL, D_MODEL), D_MODEL ** -0.5),
        'norm2_g': gain((L, D_MODEL)),
        'w_up': nrm((L, D_MODEL, FFN_HIDDEN), D_MODEL ** -0.5),
        'w_down': nrm((L, FFN_HIDDEN, D_MODEL), FFN_HIDDEN ** -0.5),
        'norm3_g': gain((L, D_MODEL)),
        'w_ple': nrm((L, PLE_DIM, D_MODEL), PLE_DIM ** -0.5),
        'w_ple_gate': nrm((L, D_MODEL, D_MODEL), D_MODEL ** -0.5),
    }


def reference(x_prompt, x_sample, cache_a_k, cache_a_v, cache_a_kidx, cache_b_k, cache_b_v, cache_c_k, cache_c_v,
              state_d_wkv, state_d_shift, p_prompt, p_sample, norm1_g, w_in, a_q_norm, a_k_norm, b_q_norm, b_k_norm,
              b_rel_bias, c_q_norm, c_k_norm, c_lambda, c_out_norm, d_mu, d_w0, d_w2, d_a0, d_a2, d_g2, d_k_k, d_k_a,
              d_r_k, d_ln_w, d_ln_b, w_branch, w_out, norm2_g, w_up, w_down, norm3_g, w_ple, w_ple_gate):
    seq, dec_seq, past = x_prompt.shape[1], x_sample.shape[1], cache_a_k.shape[2]
    pos_p = jnp.arange(seq, dtype=jnp.int32)
    pos_s = past + jnp.arange(dec_seq, dtype=jnp.int32)
    yp, ys = x_prompt, x_sample
    st_p = [[] for _ in range(9)]
    st_s = [[] for _ in range(9)]
    for l in range(DEPTH):
        lw = {'norm1': norm1_g[l], 'w_in': w_in[l], 'a_qn': a_q_norm[l], 'a_kn': a_k_norm[l],
              'b_qn': b_q_norm[l], 'b_kn': b_k_norm[l], 'b_rel': b_rel_bias[l],
              'c_qn': c_q_norm[l], 'c_kn': c_k_norm[l], 'c_lam': c_lambda[l], 'c_on': c_out_norm[l],
              'd_mu': d_mu[l], 'd_w0': d_w0[l], 'd_w2': d_w2[l], 'd_a0': d_a0[l], 'd_a2': d_a2[l],
              'd_g2': d_g2[l], 'd_kk': d_k_k[l], 'd_ka': d_k_a[l], 'd_rk': d_r_k[l],
              'd_ln_w': d_ln_w[l], 'd_ln_b': d_ln_b[l], 'w_br': w_branch[l], 'w_out': w_out[l],
              'norm2': norm2_g[l], 'w_up': w_up[l], 'w_down': w_down[l],
              'norm3': norm3_g[l], 'w_ple': w_ple[l], 'w_ple_gate': w_ple_gate[l]}
        yp, new_p = trunk_layer(l, yp, p_prompt[l], pos_p, None, lw)
        cache = (cache_a_k[l], cache_a_v[l], cache_a_kidx[l], cache_b_k[l], cache_b_v[l],
                 cache_c_k[l], cache_c_v[l], state_d_wkv[l], state_d_shift[l])
        ys, new_s = trunk_layer(l, ys, p_sample[l], pos_s, cache, lw)
        for lst, arr in zip(st_p, new_p):
            lst.append(arr)
        for lst, arr in zip(st_s, new_s):
            lst.append(arr)
    (pa_k, pa_v, pa_kidx, pb_k, pb_v, pc_k, pc_v, pd_wkv, pd_shift) = [jnp.stack(s, axis=0) for s in st_p]
    (sa_k, sa_v, sa_kidx, sb_k, sb_v, sc_k, sc_v, sd_wkv, sd_shift) = [jnp.stack(s, axis=0) for s in st_s]
    return (yp, ys, pa_k, pa_v, pa_kidx, pb_k, pb_v, pc_k, pc_v, pd_wkv, pd_shift,
            sa_k, sa_v, sa_kidx, sb_k, sb_v, sc_k, sc_v, sd_wkv, sd_shift)
```

```python
import functools
import math

import numpy as np
import jax
import jax.numpy as jnp
from jax import lax
from jax.experimental import pallas as pl
from jax.experimental.pallas import tpu as pltpu

F32 = jnp.float32
BF16 = jnp.bfloat16

CHUNK = 64
HEAD_DIM = 64
ROPE_DIM = 16
ROPE_THETA = 500000.0
N_BRANCH = 4
BRANCH_WIDTH = 512
A_HEADS, A_KV_HEADS, A_IDX_HEADS = 8, 2, 4
A_TOPK_MAX = 256
B_HEADS = 8
B_PAST_CHUNKS = 8
B_WINDOW = B_PAST_CHUNKS * CHUNK
B_REL_CLIP = 128
C_HEADS = 4
D_HEADS = 8
D_GN_EPS = 64e-5
NORM_EPS = 1e-6

LANE = 128
VMEM_LIMIT = 48 * 1024 * 1024

S_AQ, S_AIQ, S_AK, S_AV, S_AIK = 0, 8, 12, 13, 14
S_BQ, S_BK, S_BV = 16, 24, 28
S_CQ, S_CK, S_CV = 32, 40, 44
N_SLOTS = 48
ZW = N_SLOTS * LANE

NEG_KEY = -2139095041
INT_MIN = -2147483648
M_FLOOR = -1e30


def _cparams(*sem):
    return pltpu.CompilerParams(dimension_semantics=sem, vmem_limit_bytes=VMEM_LIMIT)


def _tile(n, pref):
    t = min(n, pref)
    while n % t:
        t -= 8
    return t


def _split2(x):
    hi = x.astype(BF16)
    lo = (x - hi.astype(F32)).astype(BF16)
    return hi, lo


def _seg_sum(x, ones_bd):
    hi, lo = _split2(x)
    return (jnp.dot(hi, ones_bd, preferred_element_type=F32)
            + jnp.dot(lo, ones_bd, preferred_element_type=F32))


def _dot_nt(a, b):
    return lax.dot_general(a, b, (((1,), (1,)), ((), ())), preferred_element_type=F32)


def _dot_nt3(a, b):
    ah, al = _split2(a)
    bh, bl = _split2(b)
    return _dot_nt(ah, bh) + _dot_nt(ah, bl) + _dot_nt(al, bh)


def _ones_blockdiag(n):
    i = np.arange(n)
    return jnp.asarray((i[:, None] // HEAD_DIM) == (i[None, :] // HEAD_DIM), dtype=BF16)


def _rms_kernel(x_ref, g_ref, o_ref):
    x = x_ref[...]
    ms = jnp.mean(x * x, axis=-1, keepdims=True)
    o_ref[...] = (x * lax.rsqrt(ms + NORM_EPS) * g_ref[...]).astype(o_ref.dtype)


def _rmsnorm(x, g):
    m, d = x.shape
    tm = _tile(m, 512)
    return pl.pallas_call(
        _rms_kernel,
        out_shape=jax.ShapeDtypeStruct((m, d), BF16),
        grid=(m // tm,),
        in_specs=[pl.BlockSpec((tm, d), lambda i: (i, 0)), pl.BlockSpec((1, d), lambda i: (0, 0))],
        out_specs=pl.BlockSpec((tm, d), lambda i: (i, 0)),
        compiler_params=_cparams("parallel"),
        name="rmsnorm",
    )(x, g.reshape(1, d))


def _mm_kernel(*refs, n_extra, nk, epilogue):
    a_ref, w_ref = refs[0], refs[1]
    extra = refs[2:2 + n_extra]
    o_ref = refs[2 + n_extra]
    acc_ref = refs[3 + n_extra]
    k = pl.program_id(2)

    @pl.when(k == 0)
    def _():
        acc_ref[...] = jnp.zeros_like(acc_ref)

    acc_ref[...] += jnp.dot(a_ref[...], w_ref[...], preferred_element_type=F32)

    @pl.when(k == nk - 1)
    def _():
        o_ref[...] = epilogue(acc_ref[...], *[e[...] for e in extra]).astype(o_ref.dtype)


def _matmul(a, w, epilogue, out_dtype, *, extras=(), tm=1024, tn=512, tk=2048, name="matmul"):
    m, kdim = a.shape
    n = w.shape[1]
    tm, tn, tk = _tile(m, tm), _tile(n, tn), _tile(kdim, tk)
    nk = kdim // tk
    specs = [pl.BlockSpec((tm, tk), lambda i, j, k: (i, k)), pl.BlockSpec((tk, tn), lambda i, j, k: (k, j))]
    for arr, kind in extras:
        if kind == "tile":
            specs.append(pl.BlockSpec((tm, tn), lambda i, j, k: (i, j)))
        elif kind == "row":
            specs.append(pl.BlockSpec((1, tn), lambda i, j, k: (0, j)))
        else:
            specs.append(pl.BlockSpec((tm, arr.shape[1]), lambda i, j, k: (i, 0)))
    return pl.pallas_call(
        functools.partial(_mm_kernel, n_extra=len(extras), nk=nk, epilogue=epilogue),
        out_shape=jax.ShapeDtypeStruct((m, n), out_dtype),
        grid=(m // tm, n // tn, nk),
        in_specs=specs,
        out_specs=pl.BlockSpec((tm, tn), lambda i, j, k: (i, j)),
        scratch_shapes=[pltpu.VMEM((tm, tn), F32)],
        compiler_params=_cparams("parallel", "parallel", "arbitrary"),
        name=name,
    )(a, w, *[e[0] for e in extras])


def _proj_kernel(h_ref, w_ref, gain_ref, nf_ref, rf_ref, cos_ref, sa_ref, sb_ref, bd_ref, o_ref):
    x = jnp.dot(h_ref[...], w_ref[...], preferred_element_type=F32)
    ms = _seg_sum(x * x, bd_ref[...]) * (1.0 / HEAD_DIM)
    scale = jnp.where(nf_ref[...] > 0.5, lax.rsqrt(ms + NORM_EPS) * gain_ref[...], 1.0)
    y = x * scale
    rf = rf_ref[...]
    cosv, sav, sbv = cos_ref[...], sa_ref[...], sb_ref[...]
    for s in range(2):
        ys = y[:, s * LANE:(s + 1) * LANE]
        f = rf[:, s * LANE:(s + 1) * LANE]
        roped = ys * cosv + pltpu.roll(ys, 8, 1) * sav + pltpu.roll(ys, LANE - 8, 1) * sbv
        o_ref[:, s * LANE:(s + 1) * LANE] = jnp.where(f > 0.5, roped, ys)


def _qkv_projection(h, w_pack, gain, nf, rf, cos_t, sin_a, sin_b):
    m, d = h.shape
    tm, tn = _tile(m, 1024), 2 * LANE
    row = lambda i, j: (0, j)
    tab = lambda i, j: (i, 0)
    return pl.pallas_call(
        _proj_kernel,
        out_shape=jax.ShapeDtypeStruct((m, ZW), F32),
        grid=(m // tm, ZW // tn),
        in_specs=[pl.BlockSpec((tm, d), lambda i, j: (i, 0)), pl.BlockSpec((d, tn), lambda i, j: (0, j)),
                  pl.BlockSpec((1, tn), row), pl.BlockSpec((1, tn), row), pl.BlockSpec((1, tn), row),
                  pl.BlockSpec((tm, LANE), tab), pl.BlockSpec((tm, LANE), tab), pl.BlockSpec((tm, LANE), tab),
                  pl.BlockSpec((tn, tn), lambda i, j: (0, 0))],
        out_specs=pl.BlockSpec((tm, tn), lambda i, j: (i, j)),
        compiler_params=_cparams("parallel", "arbitrary"),
        name="qkv_projection",
    )(h, w_pack, gain, nf, rf, cos_t, sin_a, sin_b, _ones_blockdiag(tn))


def _pack_w_in(w_in, lw):
    d = w_in.shape[0]
    zero = jnp.zeros((d, HEAD_DIM), w_in.dtype)
    ones64, zeros64 = jnp.ones((HEAD_DIM,), F32), jnp.zeros((HEAD_DIM,), F32)
    off = np.cumsum([0, 512, 128, 128, 256, 64, 4, 512, 512, 512, 512, 512, 512])
    (o_aq, o_ak, o_av, o_aiq, o_aik, o_aiw, o_bq, o_bk, o_bv, o_cq, o_ck, o_cv) = [int(v) for v in off[:12]]
    col = lambda a, n=HEAD_DIM: w_in[:, a:a + n]
    w, gain, nf, rf = [], [], [], []

    def half(wc, g, n_flag, r_flag):
        w.append(wc)
        gain.append(g)
        nf.append(ones64 * n_flag)
        rf.append(ones64 * r_flag)

    def empty():
        half(zero, ones64, 0.0, 0.0)

    def head_in_half(wc, which, g, n_flag, r_flag):
        for hf in range(2):
            if hf == which:
                half(wc, g, n_flag, r_flag)
            else:
                empty()

    for hd in range(A_HEADS):
        head_in_half(col(o_aq + 64 * hd), hd // (A_HEADS // A_KV_HEADS), lw["a_qn"], 1.0, 1.0)
    for hd in range(A_IDX_HEADS):
        head_in_half(col(o_aiq + 64 * hd), 0, ones64, 0.0, 1.0)
    for g in range(2):
        half(col(o_ak + 64 * g), lw["a_kn"], 1.0, 1.0)
    for g in range(2):
        half(col(o_av + 64 * g), ones64, 0.0, 0.0)
    half(col(o_aik), ones64, 0.0, 1.0)
    half(jnp.concatenate([col(o_aiw, 4), zero[:, :60]], axis=1), ones64, 0.0, 0.0)
    empty(); empty()
    for hd in range(B_HEADS):
        head_in_half(col(o_bq + 64 * hd), hd % 2, lw["b_qn"], 1.0, 0.0)
    for hd in range(B_HEADS):
        half(col(o_bk + 64 * hd), lw["b_kn"], 1.0, 0.0)
    for hd in range(B_HEADS):
        half(col(o_bv + 64 * hd), ones64, 0.0, 0.0)
    for hd in range(C_HEADS):
        for c in range(2):
            head_in_half(col(o_cq + 64 * (2 * hd + c)), c, lw["c_qn"][c], 1.0, 1.0)
    for hd in range(C_HEADS):
        for c in range(2):
            half(col(o_ck + 64 * (2 * hd + c)), lw["c_kn"][c], 1.0, 1.0)
    for hd in range(2 * C_HEADS):
        half(col(o_cv + 64 * hd), ones64, 0.0, 0.0)
    del zeros64
    w_pack = jnp.concatenate(w, axis=1).astype(BF16)
    vec = lambda parts: jnp.concatenate([p.astype(F32) for p in parts]).reshape(1, ZW)
    return w_pack, vec(gain), vec(nf), vec(rf)


def _rope_tables(pos):
    half = ROPE_DIM // 2
    inv_freq = ROPE_THETA ** (-jnp.arange(0, ROPE_DIM, 2, dtype=F32) / ROPE_DIM)
    ang = pos.astype(F32)[:, None] * inv_freq[None, :]
    cos, sin = jnp.cos(ang), jnp.sin(ang)
    rows = pos.shape[0]
    one = jnp.ones((rows, HEAD_DIM - ROPE_DIM), F32)
    zero = jnp.zeros((rows, HEAD_DIM - ROPE_DIM), F32)
    z8 = jnp.zeros((rows, half), F32)
    cos_h = jnp.concatenate([cos, cos, one], axis=1)
    sa_h = jnp.concatenate([z8, sin, zero], axis=1)
    sb_h = jnp.concatenate([-sin, z8, zero], axis=1)
    dup = lambda t: jnp.concatenate([t, t], axis=1)
    return dup(cos_h), dup(sa_h), dup(sb_h)


def _to_key(score):
    bits = pltpu.bitcast(score + 0.0, jnp.int32)
    return jnp.where(bits < 0, bits ^ 0x7FFFFFFF, bits)


def _indexer_scores(iq, iw, ik):
    kk = ik[:, :HEAD_DIM]
    sc = None
    for hd in range(A_IDX_HEADS):
        logit = _dot_nt3(iq[:, hd * LANE:hd * LANE + HEAD_DIM], kk)
        wgt = iw[:, HEAD_DIM + hd:HEAD_DIM + hd + 1] * (A_IDX_HEADS ** -0.5 * HEAD_DIM ** -0.5)
        term = jnp.maximum(logit, 0.0) * wgt
        sc = term if sc is None else sc + term
    return sc


def _count(key_ref, n_blk, blk, pred):
    rows = key_ref.shape[0]

    def body(b, acc):
        start = pl.multiple_of(b * blk, blk)
        kb = key_ref[:, pl.ds(start, blk)]
        hit = jnp.where(pred(kb, start), 1.0, 0.0)
        for s in range(blk // LANE):
            acc = acc + hit[:, s * LANE:(s + 1) * LANE]
        return acc

    acc = lax.fori_loop(0, n_blk, body, jnp.zeros((rows, LANE), F32))
    return jnp.sum(acc, axis=-1, keepdims=True)


def _topk_threshold(key_ref, n_blk, blk, topk):
    rows = key_ref.shape[0]
    kf = float(topk)
    c0 = _count(key_ref, n_blk, blk, lambda kb, st: kb >= 0)
    ans = jnp.where(c0 >= kf, 0, INT_MIN).astype(jnp.int32)

    def bit_step(it, ans):
        cand = ans + jnp.left_shift(jnp.int32(1), 30 - it)
        c = _count(key_ref, n_blk, blk, lambda kb, st: kb >= cand)
        return jnp.where(c >= kf, cand, ans)

    thr = lax.fori_loop(0, 31, bit_step, ans)
    n_gt = _count(key_ref, n_blk, blk, lambda kb, st: kb > thr)
    n_eq = _count(key_ref, n_blk, blk, lambda kb, st: kb == thr)
    need = kf - n_gt
    nbits = int(n_blk * blk).bit_length() if isinstance(n_blk, int) else 14
    cut_all = jnp.full((rows, 1), 1 << nbits, jnp.int32)
    tie_overflow = jnp.max(jnp.where((n_eq > need) & (thr != NEG_KEY), 1.0, 0.0)) > 0.5

    def search_cut():
        def cut_step(it, cut):
            cand = cut + jnp.left_shift(jnp.int32(1), nbits - 1 - it)

            def pred(kb, st):
                idx = st + lax.broadcasted_iota(jnp.int32, kb.shape, 1)
                return (kb == thr) & (idx < cand)

            c = _count(key_ref, n_blk, blk, pred)
            return jnp.where(c <= need, cand, cut)

        return lax.fori_loop(0, nbits, cut_step, jnp.zeros((rows, 1), jnp.int32))

    cut = lax.cond(tie_overflow, search_cut, lambda: cut_all)
    return thr, cut


def _selected(kb, first_idx, thr, cut):
    idx = first_idx + lax.broadcasted_iota(jnp.int32, kb.shape, 1)
    return ((kb > thr) | ((kb == thr) & (idx < cut))) & (kb > NEG_KEY)


def _pair_heads(o_even, o_odd, group_half):
    lane = lax.broadcasted_iota(jnp.int32, o_even.shape, 1)
    if group_half == 0:
        return jnp.where(lane < HEAD_DIM, o_even, pltpu.roll(o_odd, HEAD_DIM, 1))
    return jnp.where(lane < HEAD_DIM, pltpu.roll(o_even, HEAD_DIM, 1), o_odd)


def _dsa_prompt_kernel(q_ref, iq_ref, iw_ref, k_ref, v_ref, ik_ref, o_ref,
                       key_ref, m_ref, l_ref, acc_ref, *, tq, kb, topk):
    i = pl.program_id(0)
    q0 = i * tq
    n_blk = (q0 + tq + kb - 1) // kb
    row = lax.broadcasted_iota(jnp.int32, (tq, kb), 0)
    limit = (((q0 + row) >> 6) + 1) << 6
    iq, iw = iq_ref[...], iw_ref[...]

    def score_blk(b, carry):
        start = pl.multiple_of(b * kb, kb)
        sc = _indexer_scores(iq, iw, ik_ref[pl.ds(start, kb), :])
        kpos = start + lax.broadcasted_iota(jnp.int32, (tq, kb), 1)
        key_ref[:, pl.ds(start, kb)] = jnp.where(kpos < limit, _to_key(sc), NEG_KEY)
        return carry

    lax.fori_loop(0, n_blk, score_blk, 0)
    thr, cut = _topk_threshold(key_ref, n_blk, kb, topk)

    m_ref[...] = jnp.full(m_ref.shape, M_FLOOR, F32)
    l_ref[...] = jnp.zeros(l_ref.shape, F32)
    acc_ref[...] = jnp.zeros(acc_ref.shape, F32)

    def attn_blk(b, carry):
        start = pl.multiple_of(b * kb, kb)
        sel = _selected(key_ref[:, pl.ds(start, kb)], start, thr, cut)
        kblk = k_ref[pl.ds(start, kb), :].astype(BF16)
        vblk = v_ref[pl.ds(start, kb), :].astype(BF16)
        for hd in range(A_HEADS):
            qh = q_ref[:, hd * LANE:(hd + 1) * LANE].astype(BF16)
            s = jnp.where(sel, _dot_nt(qh, kblk) * (HEAD_DIM ** -0.5), -jnp.inf)
            m_prev = m_ref[hd]
            m_new = jnp.maximum(m_prev, jnp.max(s, axis=-1, keepdims=True))
            alpha = jnp.exp(m_prev - m_new)
            p = jnp.exp(s - m_new)
            l_ref[hd] = alpha * l_ref[hd] + jnp.sum(p, axis=-1, keepdims=True)
            acc_ref[hd] = alpha * acc_ref[hd] + jnp.dot(p.astype(BF16), vblk, preferred_element_type=F32)
            m_ref[hd] = m_new
        return carry

    lax.fori_loop(0, n_blk, attn_blk, 0)
    for j in range(A_HEADS // 2):
        o0 = acc_ref[2 * j] / l_ref[2 * j]
        o1 = acc_ref[2 * j + 1] / l_ref[2 * j + 1]
        o_ref[:, j * LANE:(j + 1) * LANE] = _pair_heads(o0, o1, (2 * j) // (A_HEADS // A_KV_HEADS)).astype(o_ref.dtype)


def _dsa_prompt(z, t):
    tq, kb = 128, 512
    topk = min(A_TOPK_MAX, t // 4)
    full = lambda s: pl.BlockSpec((t, LANE), lambda i: (0, s))
    return pl.pallas_call(
        functools.partial(_dsa_prompt_kernel, tq=tq, kb=kb, topk=topk),
        out_shape=jax.ShapeDtypeStruct((t, BRANCH_WIDTH), BF16),
        grid=(t // tq,),
        in_specs=[pl.BlockSpec((tq, 8 * LANE), lambda i: (i, S_AQ // 8)),
                  pl.BlockSpec((tq, 4 * LANE), lambda i: (i, S_AIQ // 4)),
                  pl.BlockSpec((tq, LANE), lambda i: (i, S_AIK)),
                  full(S_AK), full(S_AV), full(S_AIK)],
        out_specs=pl.BlockSpec((tq, BRANCH_WIDTH), lambda i: (i, 0)),
        scratch_shapes=[pltpu.VMEM((tq, t), jnp.int32), pltpu.VMEM((A_HEADS, tq, 1), F32),
                        pltpu.VMEM((A_HEADS, tq, 1), F32), pltpu.VMEM((A_HEADS, tq, LANE), F32)],
        compiler_params=_cparams("arbitrary"),
        name="dsa_prompt",
    )(z, z, z, z, z, z)


def _dsa_sample_kernel(q_ref, iq_ref, new_ik_ref, new_k_ref, new_v_ref, ck_ref, cv_ref, cik_ref, o_ref,
                       key_ref, *, ts, past, topk, q_pos0):
    width = key_ref.shape[1]
    iq, iw = iq_ref[...], new_ik_ref[...]
    row_c = lax.broadcasted_iota(jnp.int32, (ts, past), 0)
    limit_c = (((q_pos0 + row_c) >> 6) + 1) << 6
    kpos_c = lax.broadcasted_iota(jnp.int32, (ts, past), 1)
    sc_c = _indexer_scores(iq, iw, cik_ref[...])
    key_ref[:, :past] = jnp.where(kpos_c < limit_c, _to_key(sc_c), NEG_KEY)
    key_ref[:, past:] = jnp.full((ts, width - past), NEG_KEY, jnp.int32)
    sc_n = _indexer_scores(iq, iw, new_ik_ref[...])
    kpos_n = past + lax.broadcasted_iota(jnp.int32, (ts, ts), 1)
    limit_n = (((q_pos0 + lax.broadcasted_iota(jnp.int32, (ts, ts), 0)) >> 6) + 1) << 6
    key_ref[:, past:past + ts] = jnp.where(kpos_n < limit_n, _to_key(sc_n), NEG_KEY)

    thr, cut = _topk_threshold(key_ref, width // LANE, LANE, topk)
    sel_c = _selected(key_ref[:, :past], 0, thr, cut)
    sel_n = _selected(key_ref[:, past:past + ts], past, thr, cut)
    kc, vc = ck_ref[...].astype(BF16), cv_ref[...].astype(BF16)
    kn, vn = new_k_ref[...].astype(BF16), new_v_ref[...].astype(BF16)
    outs = []
    for hd in range(A_HEADS):
        qh = q_ref[:, hd * LANE:(hd + 1) * LANE].astype(BF16)
        s_c = jnp.where(sel_c, _dot_nt(qh, kc) * (HEAD_DIM ** -0.5), -jnp.inf)
        s_n = jnp.where(sel_n, _dot_nt(qh, kn) * (HEAD_DIM ** -0.5), -jnp.inf)
        m = jnp.maximum(jnp.max(s_c, axis=-1, keepdims=True), jnp.max(s_n, axis=-1, keepdims=True))
        p_c, p_n = jnp.exp(s_c - m), jnp.exp(s_n - m)
        l = jnp.sum(p_c, axis=-1, keepdims=True) + jnp.sum(p_n, axis=-1, keepdims=True)
        o = (jnp.dot(p_c.astype(BF16), vc, preferred_element_type=F32)
             + jnp.dot(p_n.astype(BF16), vn, preferred_element_type=F32))
        outs.append(o / l)
    for j in range(A_HEADS // 2):
        o_ref[:, j * LANE:(j + 1) * LANE] = _pair_heads(
            outs[2 * j], outs[2 * j + 1], (2 * j) // (A_HEADS // A_KV_HEADS)).astype(o_ref.dtype)


def _dsa_sample(z, cache_k, cache_v, cache_ik, t, nb, ts):
    past = cache_k.shape[1]
    topk = min(A_TOPK_MAX, (past + ts) // 4)
    width = ((past + ts + LANE - 1) // LANE) * LANE
    rb = t // ts
    new = lambda s: pl.BlockSpec((ts, LANE), lambda b: (rb + b, s))
    return pl.pallas_call(
        functools.partial(_dsa_sample_kernel, ts=ts, past=past, topk=topk, q_pos0=past),
        out_shape=jax.ShapeDtypeStruct((nb * ts, BRANCH_WIDTH), BF16),
        grid=(nb,),
        in_specs=[pl.BlockSpec((ts, 8 * LANE), lambda b: (rb + b, S_AQ // 8)),
                  pl.BlockSpec((ts, 4 * LANE), lambda b: (rb + b, S_AIQ // 4)),
                  new(S_AIK), new(S_AK), new(S_AV),
                  pl.BlockSpec((None, past, LANE), lambda b: (b, 0, 0)),
                  pl.BlockSpec((None, past, LANE), lambda b: (b, 0, 0)),
                  pl.BlockSpec((None, past, HEAD_DIM), lambda b: (b, 0, 0))],
        out_specs=pl.BlockSpec((ts, BRANCH_WIDTH), lambda b: (b, 0)),
        scratch_shapes=[pltpu.VMEM((ts, width), jnp.int32)],
        compiler_params=_cparams("arbitrary"),
        name="dsa_sample",
    )(z, z, z, z, z, cache_k.reshape(nb, past, LANE), cache_v.reshape(nb, past, LANE), cache_ik)


def _band_prompt_kernel(q_ref, k_ref, v_ref, bias_ref, o_ref, *, tq, win):
    w0 = pl.multiple_of(pl.program_id(1) * tq, tq)
    kw = k_ref[pl.ds(w0, win), :].astype(BF16)
    vw = v_ref[pl.ds(w0, win), :].astype(BF16)
    kpos = w0 - B_WINDOW + lax.broadcasted_iota(jnp.int32, (tq, win), 1)
    qc = (w0 + lax.broadcasted_iota(jnp.int32, (tq, win), 0)) >> 6
    kc = kpos >> 6
    ok = (kpos >= 0) & (kc <= qc) & (kc >= qc - B_PAST_CHUNKS)
    outs = []
    for e in range(2):
        qe = q_ref[:, e * LANE:(e + 1) * LANE].astype(BF16)
        s = _dot_nt(qe, kw) * (HEAD_DIM ** -0.5) + bias_ref[e]
        s = jnp.where(ok, s, -jnp.inf)
        p = jnp.exp(s - jnp.max(s, axis=-1, keepdims=True))
        l = jnp.sum(p, axis=-1, keepdims=True)
        outs.append(jnp.dot(p.astype(BF16), vw, preferred_element_type=F32) / l)
    lane = lax.broadcasted_iota(jnp.int32, (tq, LANE), 1)
    o_ref[...] = jnp.where(lane < HEAD_DIM, outs[0], outs[1]).astype(o_ref.dtype)


def _band_prompt(z, kpad, vpad, bias, t):
    tq = 128
    win = B_WINDOW + tq
    tp = kpad.shape[0]
    return pl.pallas_call(
        functools.partial(_band_prompt_kernel, tq=tq, win=win),
        out_shape=jax.ShapeDtypeStruct((t, BRANCH_WIDTH), BF16),
        grid=(B_HEADS // 2, t // tq),
        in_specs=[pl.BlockSpec((tq, 2 * LANE), lambda j, i: (i, S_BQ // 2 + j)),
                  pl.BlockSpec((tp, LANE), lambda j, i: (0, j)),
                  pl.BlockSpec((tp, LANE), lambda j, i: (0, j)),
                  pl.BlockSpec((2, tq, win), lambda j, i: (j, 0, 0))],
        out_specs=pl.BlockSpec((tq, LANE), lambda j, i: (i, j)),
        compiler_params=_cparams("parallel", "arbitrary"),
        name="band_prompt",
    )(z, kpad, vpad, bias)


def _band_sample_kernel(q_ref, kn_ref, vn_ref, kc_ref, vc_ref, bc_ref, bn_ref, o_ref):
    kc, vc = kc_ref[...].astype(BF16), vc_ref[...].astype(BF16)
    kn, vn = kn_ref[...].astype(BF16), vn_ref[...].astype(BF16)
    outs = []
    for e in range(2):
        qe = q_ref[:, e * LANE:(e + 1) * LANE].astype(BF16)
        s_c = _dot_nt(qe, kc) * (HEAD_DIM ** -0.5) + bc_ref[e]
        s_n = _dot_nt(qe, kn) * (HEAD_DIM ** -0.5) + bn_ref[e]
        m = jnp.maximum(jnp.max(s_c, axis=-1, keepdims=True), jnp.max(s_n, axis=-1, keepdims=True))
        p_c, p_n = jnp.exp(s_c - m), jnp.exp(s_n - m)
        l = jnp.sum(p_c, axis=-1, keepdims=True) + jnp.sum(p_n, axis=-1, keepdims=True)
        o = (jnp.dot(p_c.astype(BF16), vc, preferred_element_type=F32)
             + jnp.dot(p_n.astype(BF16), vn, preferred_element_type=F32))
        outs.append(o / l)
    lane = lax.broadcasted_iota(jnp.int32, outs[0].shape, 1)
    o_ref[...] = jnp.where(lane < HEAD_DIM, outs[0], outs[1]).astype(o_ref.dtype)


def _band_sample(z, cache_k, cache_v, bias_c, bias_n, t, nb, ts):
    pb = cache_k.shape[1]
    rb = t // ts
    cache = pl.BlockSpec((None, pb, LANE), lambda b, j: (b, 0, j))
    return pl.pallas_call(
        _band_sample_kernel,
        out_shape=jax.ShapeDtypeStruct((nb * ts, BRANCH_WIDTH), BF16),
        grid=(nb, B_HEADS // 2),
        in_specs=[pl.BlockSpec((ts, 2 * LANE), lambda b, j: (rb + b, S_BQ // 2 + j)),
                  pl.BlockSpec((ts, LANE), lambda b, j: (rb + b, S_BK + j)),
                  pl.BlockSpec((ts, LANE), lambda b, j: (rb + b, S_BV + j)),
                  cache, cache,
                  pl.BlockSpec((2, ts, pb), lambda b, j: (j, 0, 0)),
                  pl.BlockSpec((2, ts, ts), lambda b, j: (j, 0, 0))],
        out_specs=pl.BlockSpec((ts, LANE), lambda b, j: (b, j)),
        compiler_params=_cparams("parallel", "arbitrary"),
        name="band_sample",
    )(z, z, z, cache_k.reshape(nb, pb, BRANCH_WIDTH), cache_v.reshape(nb, pb, BRANCH_WIDTH), bias_c, bias_n)


def _lambda(lam_ref, lam_init):
    lv = lam_ref[...]
    return (jnp.exp(jnp.sum(lv[0:1] * lv[1:2], axis=-1, keepdims=True))
            - jnp.exp(jnp.sum(lv[2:3] * lv[3:4], axis=-1, keepdims=True)) + lam_init)


def _diff_finish(o0, o1, lam, on_ref, lam_init):
    attn = o0 - lam * o1
    ms = jnp.mean(attn * attn, axis=-1, keepdims=True)
    return (attn * lax.rsqrt(ms + NORM_EPS) * on_ref[...]) * (1.0 - lam_init)


def _diff_prompt_kernel(q_ref, k_ref, v_ref, lam_ref, on_ref, o_ref, m_ref, l_ref, acc_ref, *, tq, kb, lam_init):
    i = pl.program_id(1)
    q0 = i * tq
    n_blk = (q0 + tq + kb - 1) // kb
    row = lax.broadcasted_iota(jnp.int32, (tq, kb), 0)
    limit = (((q0 + row) >> 6) + 1) << 6
    m_ref[...] = jnp.full(m_ref.shape, M_FLOOR, F32)
    l_ref[...] = jnp.zeros(l_ref.shape, F32)
    acc_ref[...] = jnp.zeros(acc_ref.shape, F32)

    def blk(b, carry):
        start = pl.multiple_of(b * kb, kb)
        ok = start + lax.broadcasted_iota(jnp.int32, (tq, kb), 1) < limit
        kblk = k_ref[pl.ds(start, kb), :].astype(BF16)
        vblk = v_ref[pl.ds(start, kb), :].astype(BF16)
        for c in range(2):
            qc = q_ref[:, c * LANE:(c + 1) * LANE].astype(BF16)
            s = jnp.where(ok, _dot_nt(qc, kblk) * (HEAD_DIM ** -0.5), -jnp.inf)
            m_prev = m_ref[c]
            m_new = jnp.maximum(m_prev, jnp.max(s, axis=-1, keepdims=True))
            alpha = jnp.exp(m_prev - m_new)
            p = jnp.exp(s - m_new)
            l_ref[c] = alpha * l_ref[c] + jnp.sum(p, axis=-1, keepdims=True)
            acc_ref[c] = alpha * acc_ref[c] + jnp.dot(p.astype(BF16), vblk, preferred_element_type=F32)
            m_ref[c] = m_new
        return carry

    lax.fori_loop(0, n_blk, blk, 0)
    lam = _lambda(lam_ref, lam_init)
    o_ref[...] = _diff_finish(acc_ref[0] / l_ref[0], acc_ref[1] / l_ref[1], lam, on_ref, lam_init).astype(o_ref.dtype)


def _diff_prompt(z, c_lam, c_on, t, lam_init):
    tq, kb = 256, 512
    tq = _tile(t, tq)
    return pl.pallas_call(
        functools.partial(_diff_prompt_kernel, tq=tq, kb=kb, lam_init=lam_init),
        out_shape=jax.ShapeDtypeStruct((t, BRANCH_WIDTH), BF16),
        grid=(C_HEADS, t // tq),
        in_specs=[pl.BlockSpec((tq, 2 * LANE), lambda h, i: (i, S_CQ // 2 + h)),
                  pl.BlockSpec((t, LANE), lambda h, i: (0, S_CK + h)),
                  pl.BlockSpec((t, LANE), lambda h, i: (0, S_CV + h)),
                  pl.BlockSpec((4, HEAD_DIM), lambda h, i: (0, 0)),
                  pl.BlockSpec((1, LANE), lambda h, i: (0, 0))],
        out_specs=pl.BlockSpec((tq, LANE), lambda h, i: (i, h)),
        scratch_shapes=[pltpu.VMEM((2, tq, 1), F32), pltpu.VMEM((2, tq, 1), F32), pltpu.VMEM((2, tq, LANE), F32)],
        compiler_params=_cparams("parallel", "arbitrary"),
        name="diff_prompt",
    )(z, z, z, c_lam, c_on.reshape(1, LANE))


def _diff_sample_kernel(q_ref, kn_ref, vn_ref, kc_ref, vc_ref, lam_ref, on_ref, o_ref, *, lam_init):
    kc, vc = kc_ref[...].astype(BF16), vc_ref[...].astype(BF16)
    kn, vn = kn_ref[...].astype(BF16), vn_ref[...].astype(BF16)
    outs = []
    for c in range(2):
        qc = q_ref[:, c * LANE:(c + 1) * LANE].astype(BF16)
        s_c = _dot_nt(qc, kc) * (HEAD_DIM ** -0.5)
        s_n = _dot_nt(qc, kn) * (HEAD_DIM ** -0.5)
        m = jnp.maximum(jnp.max(s_c, axis=-1, keepdims=True), jnp.max(s_n, axis=-1, keepdims=True))
        p_c, p_n = jnp.exp(s_c - m), jnp.exp(s_n - m)
        l = jnp.sum(p_c, axis=-1, keepdims=True) + jnp.sum(p_n, axis=-1, keepdims=True)
        o = (jnp.dot(p_c.astype(BF16), vc, preferred_element_type=F32)
             + jnp.dot(p_n.astype(BF16), vn, preferred_element_type=F32))
        outs.append(o / l)
    lam = _lambda(lam_ref, lam_init)
    o_ref[...] = _diff_finish(outs[0], outs[1], lam, on_ref, lam_init).astype(o_ref.dtype)


def _diff_sample(z, cache_k, cache_v, c_lam, c_on, t, nb, ts, lam_init):
    past = cache_k.shape[1]
    rb = t // ts
    cache = pl.BlockSpec((None, past, LANE), lambda b, h: (b, 0, h))
    return pl.pallas_call(
        functools.partial(_diff_sample_kernel, lam_init=lam_init),
        out_shape=jax.ShapeDtypeStruct((nb * ts, BRANCH_WIDTH), BF16),
        grid=(nb, C_HEADS),
        in_specs=[pl.BlockSpec((ts, 2 * LANE), lambda b, h: (rb + b, S_CQ // 2 + h)),
                  pl.BlockSpec((ts, LANE), lambda b, h: (rb + b, S_CK + h)),
                  pl.BlockSpec((ts, LANE), lambda b, h: (rb + b, S_CV + h)),
                  cache, cache,
                  pl.BlockSpec((4, HEAD_DIM), lambda b, h: (0, 0)),
                  pl.BlockSpec((1, LANE), lambda b, h: (0, 0))],
        out_specs=pl.BlockSpec((ts, LANE), lambda b, h: (b, h)),
        compiler_params=_cparams("parallel", "arbitrary"),
        name="diff_sample",
    )(z, z, z, cache_k.reshape(nb, past, BRANCH_WIDTH), cache_v.reshape(nb, past, BRANCH_WIDTH),
      c_lam, c_on.reshape(1, LANE))


def _rwkv_pre_kernel(zd_ref, zs_ref, mu_ref, w0_ref, w2_ref, a0_ref, a2_ref, g2_ref, kkw_ref, ka_ref, bd_ref,
                     r_ref, w_ref, k_ref, v_ref, kk_ref, b_ref, g_ref):
    zf = zd_ref[...]
    zm = zf + (zs_ref[...] - zf) * mu_ref[...]
    bw = BRANCH_WIDTH
    r, k, v = zm[:, :bw], zm[:, bw:2 * bw], zm[:, 2 * bw:3 * bw]
    wl, al, gl = zm[:, 3 * bw:3 * bw + 64], zm[:, 3 * bw + 64:3 * bw + 128], zm[:, 3 * bw + 128:]
    dot = lambda a, b: jnp.dot(a.astype(BF16), b.astype(BF16), preferred_element_type=F32)
    u = -(w0_ref[...] + dot(jnp.tanh(wl), w2_ref[...]))
    softplus = jnp.maximum(u, 0.0) + jnp.log(1.0 + jnp.exp(-jnp.abs(u)))
    w = -softplus - 0.5
    a = jax.nn.sigmoid(a0_ref[...] + dot(al, a2_ref[...]))
    kk = k * kkw_ref[...]
    nrm = jnp.sqrt(_seg_sum(kk * kk, bd_ref[...]))
    kk = kk / jnp.maximum(nrm, 1e-12)
    r_ref[...] = r
    w_ref[...] = jnp.exp(-jnp.exp(w))
    k_ref[...] = k * (1.0 + (a - 1.0) * ka_ref[...])
    v_ref[...] = v
    kk_ref[...] = kk
    b_ref[...] = kk * a
    g_ref[...] = dot(jax.nn.sigmoid(gl), g2_ref[...])


def _rwkv_pre(zd, zsh, lw):
    m, dc = zd.shape
    tm = _tile(m, 512)
    bw = BRANCH_WIDTH
    rows = pl.BlockSpec((tm, dc), lambda i: (i, 0))
    const = lambda a: pl.BlockSpec(a.shape, lambda i: (0,) * a.ndim)
    params = [lw["d_mu"].reshape(1, dc), lw["d_w0"].reshape(1, bw), lw["d_w2"], lw["d_a0"].reshape(1, bw),
              lw["d_a2"], lw["d_g2"], lw["d_kk"].reshape(1, bw), lw["d_ka"].reshape(1, bw), _ones_blockdiag(bw)]
    out = jax.ShapeDtypeStruct((m, bw), F32)
    return pl.pallas_call(
        _rwkv_pre_kernel,
        out_shape=[out] * 7,
        grid=(m // tm,),
        in_specs=[rows, rows] + [const(p) for p in params],
        out_specs=[pl.BlockSpec((tm, bw), lambda i: (i, 0))] * 7,
        compiler_params=_cparams("parallel"),
        name="rwkv_pre",
    )(zd, zsh, *params)


def _rwkv_scan_kernel(r_ref, w_ref, k_ref, kk_ref, b_ref, v_ref, s0_ref, y_ref, sf_ref, s_ref, yt_ref, *, tb):
    @pl.when(pl.program_id(1) == 0)
    def _():
        s_ref[...] = s0_ref[...]

    vt = v_ref[...].T
    lane_t = lax.broadcasted_iota(jnp.int32, (BRANCH_WIDTH, tb), 1)

    def bcast(ref, t):
        x8 = ref[pl.ds(pl.multiple_of(t * D_HEADS, D_HEADS), D_HEADS), :]
        return jnp.concatenate(
            [jnp.broadcast_to(x8[hd:hd + 1, :], (HEAD_DIM, HEAD_DIM)) for hd in range(D_HEADS)], axis=0)

    def step(t, carry):
        s = s_ref[...]
        sa = jnp.sum(s * bcast(kk_ref, t), axis=-1, keepdims=True) * -1.0
        vcol = jnp.sum(jnp.where(lane_t == t, vt, 0.0), axis=-1, keepdims=True)
        s = s * bcast(w_ref, t) + sa * bcast(b_ref, t) + vcol * bcast(k_ref, t)
        s_ref[...] = s
        y = jnp.sum(s * bcast(r_ref, t), axis=-1, keepdims=True)
        yt_ref[...] = jnp.where(lane_t == t, y, yt_ref[...])
        return carry

    lax.fori_loop(0, tb, step, 0)
    y_ref[...] = yt_ref[...].T

    @pl.when(pl.program_id(1) == pl.num_programs(1) - 1)
    def _():
        sf_ref[...] = s_ref[...]


def _rwkv_scan(ops8, v, s0, row0, nb, t):
    tb = _tile(t, 128)
    ntb = t // tb
    rb = row0 // tb
    op_spec = pl.BlockSpec((tb * D_HEADS, HEAD_DIM), lambda b, i: (rb + b * ntb + i, 0))
    state = pl.BlockSpec((None, BRANCH_WIDTH, HEAD_DIM), lambda b, i: (b, 0, 0))
    y, sf = pl.pallas_call(
        functools.partial(_rwkv_scan_kernel, tb=tb),
        out_shape=[jax.ShapeDtypeStruct((nb * t, BRANCH_WIDTH), F32),
                   jax.ShapeDtypeStruct((nb, BRANCH_WIDTH, HEAD_DIM), F32)],
        grid=(nb, ntb),
        in_specs=[op_spec] * 5 + [pl.BlockSpec((tb, BRANCH_WIDTH), lambda b, i: (rb + b * ntb + i, 0)), state],
        out_specs=[pl.BlockSpec((tb, BRANCH_WIDTH), lambda b, i: (b * ntb + i, 0)), state],
        scratch_shapes=[pltpu.VMEM((BRANCH_WIDTH, HEAD_DIM), F32), pltpu.VMEM((BRANCH_WIDTH, tb), F32)],
        compiler_params=_cparams("parallel", "arbitrary"),
        name="rwkv_scan",
    )(*ops8, v, s0.reshape(nb, BRANCH_WIDTH, HEAD_DIM))
    return y, sf.reshape(nb, D_HEADS, HEAD_DIM, HEAD_DIM)


def _rwkv_post_kernel(y_ref, r_ref, k_ref, v_ref, g_ref, lnw_ref, lnb_ref, rk_ref, bd_ref, o_ref):
    bd = bd_ref[...]
    y = y_ref[...]
    mean = _seg_sum(y, bd) * (1.0 / HEAD_DIM)
    yc = y - mean
    var = _seg_sum(yc * yc, bd) * (1.0 / HEAD_DIM)
    yn = yc * lax.rsqrt(var + D_GN_EPS) * lnw_ref[...] + lnb_ref[...]
    v = v_ref[...]
    rkk = r_ref[...] * k_ref[...] * rk_ref[...]
    hi, lo = _split2(rkk)
    mid, lo2 = _split2(rkk - hi.astype(F32) - lo.astype(F32))
    del lo2
    dsum = lambda p: jnp.dot(p, bd, preferred_element_type=F32)
    bonus = (dsum(hi) + dsum(lo) + dsum(mid)) * v
    o_ref[...] = ((yn + bonus) * g_ref[...]).astype(o_ref.dtype)


def _rwkv_post(y, r, k, v, g, lw):
    m, bw = y.shape
    tm = _tile(m, 512)
    rows = pl.BlockSpec((tm, bw), lambda i: (i, 0))
    vec = pl.BlockSpec((1, bw), lambda i: (0, 0))
    return pl.pallas_call(
        _rwkv_post_kernel,
        out_shape=jax.ShapeDtypeStruct((m, bw), BF16),
        grid=(m // tm,),
        in_specs=[rows] * 5 + [vec] * 3 + [pl.BlockSpec((bw, bw), lambda i: (0, 0))],
        out_specs=rows,
        compiler_params=_cparams("parallel"),
        name="rwkv_post",
    )(y, r, k, v, g, lw["d_ln_w"].reshape(1, bw), lw["d_ln_b"].reshape(1, bw), lw["d_rk"].reshape(1, bw),
      _ones_blockdiag(bw))


def _merge_kernel(o_ref, wbr_ref, h_ref, wg_ref, out_ref, acc_ref):
    n = pl.program_id(2)

    @pl.when(n == 0)
    def _():
        acc_ref[...] = jnp.zeros_like(acc_ref)

    u = jnp.dot(o_ref[...], wbr_ref[...], preferred_element_type=F32)
    gate = jax.nn.sigmoid(jnp.dot(h_ref[...], wg_ref[...], preferred_element_type=F32))
    acc_ref[...] += u * gate

    @pl.when(n == N_BRANCH - 1)
    def _():
        out_ref[...] = acc_ref[...].astype(out_ref.dtype)


def _merge(o_all, w_br, h, w_gate):
    m, d = h.shape
    tm, tn = _tile(m, 1024), 512
    nj = d // tn
    return pl.pallas_call(
        _merge_kernel,
        out_shape=jax.ShapeDtypeStruct((m, d), BF16),
        grid=(m // tm, nj, N_BRANCH),
        in_specs=[pl.BlockSpec((tm, BRANCH_WIDTH), lambda i, j, n: (i, n)),
                  pl.BlockSpec((None, BRANCH_WIDTH, tn), lambda i, j, n: (n, 0, j)),
                  pl.BlockSpec((tm, d), lambda i, j, n: (i, 0)),
                  pl.BlockSpec((d, tn), lambda i, j, n: (0, n * nj + j))],
        out_specs=pl.BlockSpec((tm, tn), lambda i, j, n: (i, j)),
        scratch_shapes=[pltpu.VMEM((tm, tn), F32)],
        compiler_params=_cparams("parallel", "parallel", "arbitrary"),
        name="branch_merge",
    )(o_all, w_br, h, w_gate)


def _ple_kernel(x_ref, pe_ref, wple_ref, h_ref, wg_ref, o_ref):
    emb = jnp.dot(pe_ref[...].astype(BF16), wple_ref[...], preferred_element_type=F32)
    gate = jax.nn.sigmoid(jnp.dot(h_ref[...], wg_ref[...], preferred_element_type=F32))
    o_ref[...] = x_ref[...] + emb * gate


def _ple(x, pe, w_ple, h, w_gate):
    m, d = x.shape
    pd = pe.shape[1]
    tm, tn = _tile(m, 1024), 512
    return pl.pallas_call(
        _ple_kernel,
        out_shape=jax.ShapeDtypeStruct((m, d), F32),
        grid=(m // tm, d // tn),
        in_specs=[pl.BlockSpec((tm, tn), lambda i, j: (i, j)),
                  pl.BlockSpec((tm, pd), lambda i, j: (i, 0)),
                  pl.BlockSpec((pd, tn), lambda i, j: (0, j)),
                  pl.BlockSpec((tm, d), lambda i, j: (i, 0)),
                  pl.BlockSpec((d, tn), lambda i, j: (0, j))],
        out_specs=pl.BlockSpec((tm, tn), lambda i, j: (i, j)),
        compiler_params=_cparams("parallel", "parallel"),
        name="ple",
    )(x, pe, w_ple, h, w_gate)


def _band_bias(table, tq):
    rel = np.arange(B_WINDOW + tq)[None, :] - np.arange(tq)[:, None] - B_WINDOW
    return table.astype(F32)[:, np.clip(rel, -B_REL_CLIP, B_REL_CLIP) + B_REL_CLIP]


def _layer(l, x, pe, caches, lw, t, nb, ts, tabs):
    m = x.shape[0]
    bw = BRANCH_WIDTH
    (ca_k, ca_v, ca_ik, cb_k, cb_v, cc_k, cc_v, wkv0, zprev) = caches
    lam_init = 0.8 - 0.6 * math.exp(-0.3 * l)
    w_in = lw["w_in"]
    n_qkv = 1092 + 3 * 512 + 3 * 512
    dc = lw["d_mu"].shape[0]

    h = _rmsnorm(x, lw["norm1"])
    w_pack, gain, nf, rf = _pack_w_in(w_in, lw)
    z = _qkv_projection(h, w_pack, gain, nf, rf, *tabs)
    zd = _matmul(h, w_in[:, n_qkv:n_qkv + dc].astype(BF16), lambda acc: acc, F32, tn=256, name="rwkv_projection")

    oa = jnp.concatenate([_dsa_prompt(z, t), _dsa_sample(z, ca_k, ca_v, ca_ik, t, nb, ts)], axis=0)
    slot = lambda rows, s, n=1: z[rows, s * LANE:(s + n) * LANE]
    prompt, sample = slice(0, t), slice(t, m)
    pad = jnp.zeros((B_WINDOW, bw), F32)
    bias = _band_bias(lw["b_rel"], 128)
    pb = cb_k.shape[1]
    ob = jnp.concatenate([
        _band_prompt(z, jnp.concatenate([pad, slot(prompt, S_BK, 4)], axis=0),
                     jnp.concatenate([pad, slot(prompt, S_BV, 4)], axis=0), bias, t),
        _band_sample(z, cb_k, cb_v, bias[:, :ts, B_WINDOW - pb:B_WINDOW], bias[:, :ts, B_WINDOW:B_WINDOW + ts],
                     t, nb, ts)], axis=0)
    oc = jnp.concatenate([_diff_prompt(z, lw["c_lam"], lw["c_on"], t, lam_init),
                          _diff_sample(z, cc_k, cc_v, lw["c_lam"], lw["c_on"], t, nb, ts, lam_init)], axis=0)
    zd_s = zd[t:].reshape(nb, ts, dc)
    zsh = jnp.concatenate([jnp.zeros((1, dc), F32), zd[:t - 1],
                           jnp.concatenate([zprev, zd_s[:, :-1]], axis=1).reshape(nb * ts, dc)], axis=0)
    r, w, k, v, kk, b, g = _rwkv_pre(zd, zsh, lw)
    ops8 = [a.reshape(m * D_HEADS, HEAD_DIM) for a in (r, w, k, kk, b)]
    y_p, wkv_p = _rwkv_scan(ops8, v, jnp.zeros((1, D_HEADS, HEAD_DIM, HEAD_DIM), F32), 0, 1, t)
    y_s, wkv_s = _rwkv_scan(ops8, v, wkv0, t, nb, ts)
    od = _rwkv_post(jnp.concatenate([y_p, y_s], axis=0), r, k, v, g, lw)

    n_gate0 = n_qkv + dc
    ug = _merge(jnp.concatenate([oa, ob, oc, od], axis=1), lw["w_br"].astype(BF16), h,
                w_in[:, n_gate0:].astype(BF16))
    x = _matmul(ug, lw["w_out"].astype(BF16), lambda acc, res: res + acc, F32, extras=[(x, "tile")], name="out_proj")
    h2 = _rmsnorm(x, lw["norm2"])
    up = _matmul(h2, lw["w_up"].astype(BF16), lambda acc: jnp.square(jnp.maximum(acc, 0.0)), BF16, name="mlp_up")
    x = _matmul(up, lw["w_down"].astype(BF16), lambda acc, res: res + acc, F32, extras=[(x, "tile")], name="mlp_down")
    h3 = _rmsnorm(x, lw["norm3"])
    x = _ple(x, pe, lw["w_ple"].astype(BF16), h3, lw["w_ple_gate"].astype(BF16))

    def rows_of(sl, lead):
        a = lambda s, n, shape: slot(sl, s, n).reshape(lead + shape)
        ak = a(S_AK, 1, (A_KV_HEADS, HEAD_DIM))
        av = a(S_AV, 1, (A_KV_HEADS, HEAD_DIM))
        aik = slot(sl, S_AIK)[:, :HEAD_DIM].reshape(lead + (HEAD_DIM,))
        bk = a(S_BK, 4, (B_HEADS, HEAD_DIM))
        bv = a(S_BV, 4, (B_HEADS, HEAD_DIM))
        ck = a(S_CK, 4, (C_HEADS, 2, HEAD_DIM))
        cv = a(S_CV, 4, (C_HEADS, 2 * HEAD_DIM))
        return ak, av, aik, bk, bv, ck, cv

    keep = min(B_WINDOW, t)
    pak, pav, paik, pbk, pbv, pck, pcv = rows_of(prompt, (1, t))
    new_p = (pak, pav, paik, pbk[:, t - keep:], pbv[:, t - keep:], pck, pcv, wkv_p, zd[t - 1:t].reshape(1, 1, dc))
    new_s = rows_of(sample, (nb, ts)) + (wkv_s, zd_s[:, -1:])
    return x, new_p, new_s


def kernel(x_prompt, x_sample, cache_a_k, cache_a_v, cache_a_kidx, cache_b_k, cache_b_v, cache_c_k, cache_c_v, state_d_wkv, state_d_shift, p_prompt, p_sample, norm1_g, w_in, a_q_norm, a_k_norm, b_q_norm, b_k_norm, b_rel_bias, c_q_norm, c_k_norm, c_lambda, c_out_norm, d_mu, d_w0, d_w2, d_a0, d_a2, d_g2, d_k_k, d_k_a, d_r_k, d_ln_w, d_ln_b, w_branch, w_out, norm2_g, w_up, w_down, norm3_g, w_ple, w_ple_gate):
    batch, t, d = x_prompt.shape
    nb, ts, _ = x_sample.shape
    past = cache_a_k.shape[2]
    depth = w_in.shape[0]
    assert batch == 1 and t % 512 == 0 and past % CHUNK == 0 and ts <= CHUNK and (nb * ts) % 8 == 0
    x = jnp.concatenate([x_prompt[0], x_sample.reshape(nb * ts, d)], axis=0)
    pos = jnp.concatenate([jnp.arange(t, dtype=jnp.int32),
                           jnp.tile(past + jnp.arange(ts, dtype=jnp.int32), nb)])
    tabs = _rope_tables(pos)
    st_p = [[] for _ in range(9)]
    st_s = [[] for _ in range(9)]
    for l in range(depth):
        lw = {'norm1': norm1_g[l], 'w_in': w_in[l], 'a_qn': a_q_norm[l], 'a_kn': a_k_norm[l],
              'b_qn': b_q_norm[l], 'b_kn': b_k_norm[l], 'b_rel': b_rel_bias[l],
              'c_qn': c_q_norm[l], 'c_kn': c_k_norm[l], 'c_lam': c_lambda[l], 'c_on': c_out_norm[l],
              'd_mu': d_mu[l], 'd_w0': d_w0[l], 'd_w2': d_w2[l], 'd_a0': d_a0[l], 'd_a2': d_a2[l],
              'd_g2': d_g2[l], 'd_kk': d_k_k[l], 'd_ka': d_k_a[l], 'd_rk': d_r_k[l],
              'd_ln_w': d_ln_w[l], 'd_ln_b': d_ln_b[l], 'w_br': w_branch[l], 'w_out': w_out[l],
              'norm2': norm2_g[l], 'w_up': w_up[l], 'w_down': w_down[l],
              'norm3': norm3_g[l], 'w_ple': w_ple[l], 'w_ple_gate': w_ple_gate[l]}
        pe = jnp.concatenate([p_prompt[l, 0], p_sample[l].reshape(nb * ts, -1)], axis=0)
        caches = (cache_a_k[l], cache_a_v[l], cache_a_kidx[l], cache_b_k[l], cache_b_v[l],
                  cache_c_k[l], cache_c_v[l], state_d_wkv[l], state_d_shift[l])
        x, new_p, new_s = _layer(l, x, pe, caches, lw, t, nb, ts, tabs)
        for lst, arr in zip(st_p, new_p):
            lst.append(arr)
        for lst, arr in zip(st_s, new_s):
            lst.append(arr)
    outs_p = [jnp.stack(s, axis=0) for s in st_p]
    outs_s = [jnp.stack(s, axis=0) for s in st_s]
    return (x[:t].reshape(1, t, d), x[t:].reshape(nb, ts, d), *outs_p, *outs_s)
```

```python
import functools
import math

import numpy as np
import jax
import jax.numpy as jnp
from jax import lax
from jax.experimental import pallas as pl
from jax.experimental.pallas import tpu as pltpu

F32 = jnp.float32
BF16 = jnp.bfloat16

CHUNK = 64
HEAD_DIM = 64
ROPE_DIM = 16
ROPE_THETA = 500000.0
N_BRANCH = 4
BRANCH_WIDTH = 512
A_HEADS, A_KV_HEADS, A_IDX_HEADS = 8, 2, 4
A_TOPK_MAX = 256
B_HEADS = 8
B_PAST_CHUNKS = 8
B_WINDOW = B_PAST_CHUNKS * CHUNK
B_REL_CLIP = 128
C_HEADS = 4
D_HEADS = 8
D_GN_EPS = 64e-5
NORM_EPS = 1e-6

LANE = 128
VMEM_LIMIT = 48 * 1024 * 1024

S_AQ, S_AK, S_AV, S_AIQ, S_AIK = 0, 4, 5, 6, 8
S_BQ, S_BK, S_BV = 10, 14, 18
S_CQ, S_CK, S_CV = 22, 26, 30
S_D, S_GATE = 34, 48
N_QKV_SLOTS = 34
N_D_SLOTS = 14
A_COLS = 1092
A_SLOTS = 9
SHIFT = A_COLS - (A_SLOTS - 1) * LANE

NEG_KEY = -2139095041
INT_MIN = -2147483648
M_FLOOR = -1e30


def _cparams(*sem):
    return pltpu.CompilerParams(dimension_semantics=sem, vmem_limit_bytes=VMEM_LIMIT)


def _tile(n, pref):
    t = min(n, pref)
    while n % t:
        t -= 8
    return t


def _split2(x):
    hi = x.astype(BF16)
    lo = (x - hi.astype(F32)).astype(BF16)
    return hi, lo


def _seg_sum(x, ones_bd):
    hi, lo = _split2(x)
    return (jnp.dot(hi, ones_bd, preferred_element_type=F32)
            + jnp.dot(lo, ones_bd, preferred_element_type=F32))


def _dot_nt(a, b):
    return lax.dot_general(a, b, (((1,), (1,)), ((), ())), preferred_element_type=F32)


def _dot_nt3(a, b):
    ah, al = _split2(a)
    bh, bl = _split2(b)
    return _dot_nt(ah, bh) + _dot_nt(ah, bl) + _dot_nt(al, bh)


def _ones_blockdiag(n):
    i = np.arange(n)
    return jnp.asarray((i[:, None] // HEAD_DIM) == (i[None, :] // HEAD_DIM), dtype=BF16)


def _vec3(a):
    return a.reshape(a.shape[0], 1, -1)


def _layer_vec(l, n):
    return pl.BlockSpec((None, 1, n), lambda *_: (l, 0, 0))


def _half_mask(shape, half):
    lane = lax.broadcasted_iota(jnp.int32, shape, len(shape) - 1)
    return (lane < HEAD_DIM) if half == 0 else (lane >= HEAD_DIM)


def _rms_kernel(x_ref, g_ref, o_ref):
    x = x_ref[...]
    ms = jnp.mean(x * x, axis=-1, keepdims=True)
    o_ref[...] = (x * lax.rsqrt(ms + NORM_EPS) * g_ref[...]).astype(o_ref.dtype)


def _rmsnorm(x, g_all, l):
    m, d = x.shape
    tm = _tile(m, 512)
    return pl.pallas_call(
        _rms_kernel,
        out_shape=jax.ShapeDtypeStruct((m, d), BF16),
        grid=(m // tm,),
        in_specs=[pl.BlockSpec((tm, d), lambda i: (i, 0)), _layer_vec(l, d)],
        out_specs=pl.BlockSpec((tm, d), lambda i: (i, 0)),
        compiler_params=_cparams("parallel"),
        name="rmsnorm",
    )(x, _vec3(g_all))


def _pack_kernel(a_ref, b_ref, o_ref):
    j = pl.program_id(0)
    a, b = a_ref[...], b_ref[...]
    lane = lax.broadcasted_iota(jnp.int32, a.shape, 1)
    shifted = jnp.where(lane < LANE - SHIFT, pltpu.roll(a, LANE - SHIFT, 1), pltpu.roll(b, LANE - SHIFT, 1))
    out = jnp.where(j < A_SLOTS, a, jnp.where(j == A_SLOTS, 0.0, shifted))
    o_ref[...] = out.astype(o_ref.dtype)


def _pack_w_in(w_in_all, l):
    _, d, n_in = w_in_all.shape
    n_slots = A_SLOTS + 1 + (n_in - A_COLS) // LANE
    assert (n_in - A_COLS) % LANE == 0 and 0 < SHIFT < LANE
    src_a = lambda j: jnp.where(j < A_SLOTS, j, j - 2)
    src_b = lambda j: jnp.where(j < A_SLOTS, j, j - 1)
    return pl.pallas_call(
        _pack_kernel,
        out_shape=jax.ShapeDtypeStruct((d, n_slots * LANE), BF16),
        grid=(n_slots,),
        in_specs=[pl.BlockSpec((None, d, LANE), lambda j: (l, 0, src_a(j))),
                  pl.BlockSpec((None, d, LANE), lambda j: (l, 0, src_b(j)))],
        out_specs=pl.BlockSpec((d, LANE), lambda j: (0, j)),
        compiler_params=_cparams("parallel"),
        name="pack_w_in",
    )(w_in_all, w_in_all)


def _mm_kernel(*refs, n_extra, nk, epilogue):
    a_ref, w_ref = refs[0], refs[1]
    extra = refs[2:2 + n_extra]
    o_ref = refs[2 + n_extra]
    acc_ref = refs[3 + n_extra]
    k = pl.program_id(2)

    @pl.when(k == 0)
    def _():
        acc_ref[...] = jnp.zeros_like(acc_ref)

    acc_ref[...] += jnp.dot(a_ref[...], w_ref[...].astype(BF16), preferred_element_type=F32)

    @pl.when(k == nk - 1)
    def _():
        o_ref[...] = epilogue(acc_ref[...], *[e[...] for e in extra]).astype(o_ref.dtype)


def _matmul(a, w, epilogue, out_dtype, *, n, layer=None, col0=0, residual=None,
            tm=1024, tn=512, tk=2048, name="matmul"):
    m, kdim = a.shape
    tm, tn, tk = _tile(m, tm), _tile(n, tn), _tile(kdim, tk)
    nk = kdim // tk
    cb = col0 // tn
    assert col0 % tn == 0
    if layer is None:
        w_spec = pl.BlockSpec((tk, tn), lambda i, j, k: (k, cb + j))
    else:
        w_spec = pl.BlockSpec((None, tk, tn), lambda i, j, k: (layer, k, cb + j))
    specs = [pl.BlockSpec((tm, tk), lambda i, j, k: (i, k)), w_spec]
    extras = []
    if residual is not None:
        specs.append(pl.BlockSpec((tm, tn), lambda i, j, k: (i, j)))
        extras.append(residual)
    return pl.pallas_call(
        functools.partial(_mm_kernel, n_extra=len(extras), nk=nk, epilogue=epilogue),
        out_shape=jax.ShapeDtypeStruct((m, n), out_dtype),
        grid=(m // tm, n // tn, nk),
        in_specs=specs,
        out_specs=pl.BlockSpec((tm, tn), lambda i, j, k: (i, j)),
        scratch_shapes=[pltpu.VMEM((tm, tn), F32)],
        compiler_params=_cparams("parallel", "parallel", "arbitrary"),
        name=name,
    )(a, w, *extras)


def _proj_kernel(h_ref, w_ref, gain_ref, nf_ref, rf_ref, cos_ref, sa_ref, sb_ref, bd_ref, o_ref, ob_ref):
    x = jnp.dot(h_ref[...], w_ref[...], preferred_element_type=F32)
    ms = _seg_sum(x * x, bd_ref[...]) * (1.0 / HEAD_DIM)
    scale = jnp.where(nf_ref[...] > 0.5, lax.rsqrt(ms + NORM_EPS) * gain_ref[...], 1.0)
    y = x * scale
    rf = rf_ref[...]
    cosv, sav, sbv = cos_ref[...], sa_ref[...], sb_ref[...]
    for s in range(2):
        ys = y[:, s * LANE:(s + 1) * LANE]
        f = rf[:, s * LANE:(s + 1) * LANE]
        roped = ys * cosv + pltpu.roll(ys, 8, 1) * sav + pltpu.roll(ys, LANE - 8, 1) * sbv
        out = jnp.where(f > 0.5, roped, ys)
        o_ref[:, s * LANE:(s + 1) * LANE] = out
        ob_ref[:, s * LANE:(s + 1) * LANE] = out.astype(BF16)


def _qkv_projection(h, w_pack, gain, nf, rf, cos_t, sin_a, sin_b):
    m, d = h.shape
    tm, tn = _tile(m, 1024), 2 * LANE
    zw = N_QKV_SLOTS * LANE
    row = lambda i, j: (0, j)
    tab = lambda i, j: (i, 0)
    return pl.pallas_call(
        _proj_kernel,
        out_shape=[jax.ShapeDtypeStruct((m, zw), F32), jax.ShapeDtypeStruct((m, zw), BF16)],
        grid=(m // tm, zw // tn),
        in_specs=[pl.BlockSpec((tm, d), lambda i, j: (i, 0)), pl.BlockSpec((d, tn), lambda i, j: (0, j)),
                  pl.BlockSpec((1, tn), row), pl.BlockSpec((1, tn), row), pl.BlockSpec((1, tn), row),
                  pl.BlockSpec((tm, LANE), tab), pl.BlockSpec((tm, LANE), tab), pl.BlockSpec((tm, LANE), tab),
                  pl.BlockSpec((tn, tn), lambda i, j: (0, 0))],
        out_specs=[pl.BlockSpec((tm, tn), lambda i, j: (i, j))] * 2,
        compiler_params=_cparams("parallel", "arbitrary"),
        name="qkv_projection",
    )(h, w_pack, gain, nf, rf, cos_t, sin_a, sin_b, _ones_blockdiag(tn))


def _column_vectors(lw):
    f = lambda v: jnp.asarray(v, F32).reshape(-1)
    ones = lambda n: jnp.ones((n,), F32)
    zeros = lambda n: jnp.zeros((n,), F32)
    rep = lambda v, n: jnp.tile(f(v), n)
    groups = [
        (rep(lw["a_qn"], 8), 1.0, 1.0), (rep(lw["a_kn"], 2), 1.0, 1.0), (ones(128), 0.0, 0.0),
        (ones(256), 0.0, 1.0), (ones(64), 0.0, 1.0), (ones(64 + LANE), 0.0, 0.0),
        (rep(lw["b_qn"], 8), 1.0, 0.0), (rep(lw["b_kn"], 8), 1.0, 0.0), (ones(512), 0.0, 0.0),
        (rep(lw["c_qn"], 4), 1.0, 1.0), (rep(lw["c_kn"], 4), 1.0, 1.0), (ones(512), 0.0, 0.0)]
    gain = jnp.concatenate([g for g, _, _ in groups]).reshape(1, -1)
    nf = jnp.concatenate([ones(g.shape[0]) * a for g, a, _ in groups]).reshape(1, -1)
    rf = jnp.concatenate([ones(g.shape[0]) * b for g, _, b in groups]).reshape(1, -1)
    del zeros
    return gain, nf, rf


def _rope_tables(pos):
    half = ROPE_DIM // 2
    inv_freq = ROPE_THETA ** (-jnp.arange(0, ROPE_DIM, 2, dtype=F32) / ROPE_DIM)
    ang = pos.astype(F32)[:, None] * inv_freq[None, :]
    cos, sin = jnp.cos(ang), jnp.sin(ang)
    rows = pos.shape[0]
    one = jnp.ones((rows, HEAD_DIM - ROPE_DIM), F32)
    zero = jnp.zeros((rows, HEAD_DIM - ROPE_DIM), F32)
    z8 = jnp.zeros((rows, half), F32)
    cos_h = jnp.concatenate([cos, cos, one], axis=1)
    sa_h = jnp.concatenate([z8, sin, zero], axis=1)
    sb_h = jnp.concatenate([-sin, z8, zero], axis=1)
    dup = lambda t: jnp.concatenate([t, t], axis=1)
    return dup(cos_h), dup(sa_h), dup(sb_h)


def _to_key(score):
    bits = pltpu.bitcast(score + 0.0, jnp.int32)
    return jnp.where(bits < 0, bits ^ 0x7FFFFFFF, bits)


def _indexer_scores(iq, iw, ik):
    kk = ik[:, :HEAD_DIM]
    sc = None
    for hd in range(A_IDX_HEADS):
        logit = _dot_nt3(iq[:, hd * HEAD_DIM:(hd + 1) * HEAD_DIM], kk)
        wgt = iw[:, HEAD_DIM + hd:HEAD_DIM + hd + 1] * (A_IDX_HEADS ** -0.5 * HEAD_DIM ** -0.5)
        term = jnp.maximum(logit, 0.0) * wgt
        sc = term if sc is None else sc + term
    return sc


def _lane_fold(x, op):
    out = x[:, :LANE]
    for s in range(1, x.shape[1] // LANE):
        out = op(out, x[:, s * LANE:(s + 1) * LANE])
    return out


def _count(key_ref, n_blk, blk, pred):
    rows = key_ref.shape[0]

    def body(b, acc):
        start = pl.multiple_of(b * blk, blk)
        kb = key_ref[:, pl.ds(start, blk)]
        return acc + _lane_fold(jnp.where(pred(kb, start), 1.0, 0.0), jnp.add)

    acc = lax.fori_loop(0, n_blk, body, jnp.zeros((rows, LANE), F32))
    return jnp.sum(acc, axis=-1, keepdims=True)


def _topk_threshold(key_ref, n_blk, blk, topk):
    rows = key_ref.shape[0]
    kf = float(topk)
    c0 = _count(key_ref, n_blk, blk, lambda kb, st: kb >= 0)
    ans = jnp.where(c0 >= kf, 0, INT_MIN).astype(jnp.int32)

    def bit_step(it, ans):
        cand = ans + jnp.left_shift(jnp.int32(1), 30 - it)
        c = _count(key_ref, n_blk, blk, lambda kb, st: kb >= cand)
        return jnp.where(c >= kf, cand, ans)

    thr = lax.fori_loop(0, 31, bit_step, ans)
    n_gt = _count(key_ref, n_blk, blk, lambda kb, st: kb > thr)
    n_eq = _count(key_ref, n_blk, blk, lambda kb, st: kb == thr)
    need = kf - n_gt
    nbits = int(n_blk * blk).bit_length() if isinstance(n_blk, int) else 14
    cut_all = jnp.full((rows, 1), 1 << nbits, jnp.int32)
    tie_overflow = jnp.max(jnp.where((n_eq > need) & (thr != NEG_KEY), 1.0, 0.0)) > 0.5

    def search_cut():
        def cut_step(it, cut):
            cand = cut + jnp.left_shift(jnp.int32(1), nbits - 1 - it)

            def pred(kb, st):
                idx = st + lax.broadcasted_iota(jnp.int32, kb.shape, 1)
                return (kb == thr) & (idx < cand)

            c = _count(key_ref, n_blk, blk, pred)
            return jnp.where(c <= need, cand, cut)

        return lax.fori_loop(0, nbits, cut_step, jnp.zeros((rows, 1), jnp.int32))

    cut = lax.cond(tie_overflow, search_cut, lambda: cut_all)
    return thr, cut


def _selected(kb, first_idx, thr, cut):
    idx = first_idx + lax.broadcasted_iota(jnp.int32, kb.shape, 1)
    return ((kb > thr) | ((kb == thr) & (idx < cut))) & (kb > NEG_KEY)


def _pair_heads(o_even, o_odd, group_half):
    lane = lax.broadcasted_iota(jnp.int32, o_even.shape, 1)
    if group_half == 0:
        return jnp.where(lane < HEAD_DIM, o_even, pltpu.roll(o_odd, HEAD_DIM, 1))
    return jnp.where(lane < HEAD_DIM, pltpu.roll(o_even, HEAD_DIM, 1), o_odd)


def _dsa_query_heads(q_ref, qh_ref):
    for hd in range(A_HEADS):
        slot, half = hd // 2, hd % 2
        group = hd // (A_HEADS // A_KV_HEADS)
        qs = q_ref[:, slot * LANE:(slot + 1) * LANE] * (HEAD_DIM ** -0.5)
        if half != group:
            qs = pltpu.roll(qs, HEAD_DIM, 1)
        qh_ref[hd] = jnp.where(_half_mask(qs.shape, group), qs, 0.0).astype(BF16)


def _softmax_block(sel, kblk, vblk, q_heads, m_ref, l_ref, acc_ref, s_ref, p_ref):
    n = len(q_heads)
    reps = kblk.shape[0] // LANE
    for hd in range(n):
        s_ref[hd] = _dot_nt(q_heads[hd], kblk)
    alphas = []
    for hd in range(n):
        s = jnp.where(sel, s_ref[hd], -jnp.inf)
        s_ref[hd] = s
        blk_max = jnp.max(_lane_fold(s, jnp.maximum), axis=-1, keepdims=True)
        m_prev = m_ref[hd]
        m_new = jnp.maximum(m_prev, blk_max)
        alphas.append(jnp.exp(m_prev - m_new))
        m_ref[hd] = m_new
    for hd in range(n):
        p = jnp.exp(s_ref[hd] - jnp.tile(m_ref[hd], (1, reps)))
        l_ref[hd] = alphas[hd] * l_ref[hd] + _lane_fold(p, jnp.add)
        p_ref[hd] = p.astype(BF16)
    for hd in range(n):
        acc_ref[hd] = alphas[hd] * acc_ref[hd] + jnp.dot(p_ref[hd], vblk, preferred_element_type=F32)


def _softmax_init(m_ref, l_ref, acc_ref):
    m_ref[...] = jnp.full(m_ref.shape, M_FLOOR, F32)
    l_ref[...] = jnp.zeros(l_ref.shape, F32)
    acc_ref[...] = jnp.zeros(acc_ref.shape, F32)


def _indexer_query3(iq_ref, iq3_ref):
    for hd in range(A_IDX_HEADS):
        slot, half = hd // 2, hd % 2
        x = iq_ref[:, slot * LANE:(slot + 1) * LANE]
        xl = jnp.where(_half_mask(x.shape, half), x, pltpu.roll(x, HEAD_DIM, 1))
        hi = xl.astype(BF16).astype(F32)
        lower = _half_mask(x.shape, 0)
        iq3_ref[hd, :, :LANE] = jnp.where(lower, hi, xl - hi).astype(BF16)
        iq3_ref[hd, :, LANE:] = jnp.where(lower, hi, 0.0).astype(BF16)


def _ik3_kernel(x_ref, o_ref):
    x = x_ref[...]
    lower = _half_mask(x.shape, 0)
    xl = jnp.where(lower, x, pltpu.roll(x, HEAD_DIM, 1))
    hi = xl.astype(BF16).astype(F32)
    o_ref[:, :LANE] = hi.astype(BF16)
    o_ref[:, LANE:] = jnp.where(lower, xl - hi, 0.0).astype(BF16)


def _indexer_keys3(z, t):
    tm = _tile(t, 1024)
    return pl.pallas_call(
        _ik3_kernel,
        out_shape=jax.ShapeDtypeStruct((t, 2 * LANE), BF16),
        grid=(t // tm,),
        in_specs=[pl.BlockSpec((tm, LANE), lambda i: (i, S_AIK))],
        out_specs=pl.BlockSpec((tm, 2 * LANE), lambda i: (i, 0)),
        compiler_params=_cparams("parallel"),
        name="indexer_keys",
    )(z)


def _dsa_prompt_kernel(q_ref, iq_ref, iw_ref, k_ref, v_ref, ik3_ref, o_ref,
                       key_ref, qh_ref, iq3_ref, m_ref, l_ref, acc_ref, s_ref, p_ref, *, tq, kb, topk):
    i = pl.program_id(0)
    q0 = i * tq
    n_blk = (q0 + tq + kb - 1) // kb
    row = lax.broadcasted_iota(jnp.int32, (tq, kb), 0)
    limit = (((q0 + row) >> 6) + 1) << 6
    _indexer_query3(iq_ref, iq3_ref)
    iw = iw_ref[...]
    wgt = [iw[:, HEAD_DIM + hd:HEAD_DIM + hd + 1] * (A_IDX_HEADS ** -0.5 * HEAD_DIM ** -0.5)
           for hd in range(A_IDX_HEADS)]

    def score_blk(b, carry):
        start = pl.multiple_of(b * kb, kb)
        ikb = ik3_ref[pl.ds(start, kb), :]
        sc = None
        for hd in range(A_IDX_HEADS):
            term = jnp.maximum(_dot_nt(iq3_ref[hd], ikb), 0.0) * wgt[hd]
            sc = term if sc is None else sc + term
        kpos = start + lax.broadcasted_iota(jnp.int32, (tq, kb), 1)
        key_ref[:, pl.ds(start, kb)] = jnp.where(kpos < limit, _to_key(sc), NEG_KEY)
        return carry

    lax.fori_loop(0, n_blk, score_blk, 0)
    thr, cut = _topk_threshold(key_ref, n_blk, kb, topk)
    _dsa_query_heads(q_ref, qh_ref)
    _softmax_init(m_ref, l_ref, acc_ref)

    def attn_blk(b, carry):
        start = pl.multiple_of(b * kb, kb)
        sel = _selected(key_ref[:, pl.ds(start, kb)], start, thr, cut)
        _softmax_block(sel, k_ref[pl.ds(start, kb), :], v_ref[pl.ds(start, kb), :],
                       [qh_ref[hd] for hd in range(A_HEADS)], m_ref, l_ref, acc_ref, s_ref, p_ref)
        return carry

    lax.fori_loop(0, n_blk, attn_blk, 0)
    for j in range(A_HEADS // 2):
        outs = [acc_ref[2 * j + e] / jnp.sum(l_ref[2 * j + e], axis=-1, keepdims=True) for e in range(2)]
        o_ref[:, j * LANE:(j + 1) * LANE] = _pair_heads(outs[0], outs[1], (2 * j) // (A_HEADS // A_KV_HEADS)).astype(o_ref.dtype)


def _dsa_prompt(z, zb, t):
    tq, kb = 128, 512
    topk = min(A_TOPK_MAX, t // 4)
    full = lambda s: pl.BlockSpec((t, LANE), lambda i: (0, s))
    heads = lambda dt, w=LANE: pltpu.VMEM((A_HEADS, tq, w), dt)
    return pl.pallas_call(
        functools.partial(_dsa_prompt_kernel, tq=tq, kb=kb, topk=topk),
        out_shape=jax.ShapeDtypeStruct((t, BRANCH_WIDTH), BF16),
        grid=(t // tq,),
        in_specs=[pl.BlockSpec((tq, 4 * LANE), lambda i: (i, S_AQ // 4)),
                  pl.BlockSpec((tq, 2 * LANE), lambda i: (i, S_AIQ // 2)),
                  pl.BlockSpec((tq, LANE), lambda i: (i, S_AIK)),
                  full(S_AK), full(S_AV), pl.BlockSpec((t, 2 * LANE), lambda i: (0, 0))],
        out_specs=pl.BlockSpec((tq, BRANCH_WIDTH), lambda i: (i, 0)),
        scratch_shapes=[pltpu.VMEM((tq, t), jnp.int32), heads(BF16),
                        pltpu.VMEM((A_IDX_HEADS, tq, 2 * LANE), BF16),
                        heads(F32), heads(F32), heads(F32), heads(F32, kb), heads(BF16, kb)],
        compiler_params=_cparams("arbitrary"),
        name="dsa_prompt",
    )(z, z, z, zb, zb, _indexer_keys3(z, t))


def _dsa_sample_kernel(q_ref, iq_ref, new_ik_ref, new_k_ref, new_v_ref, ck_ref, cv_ref, cik_ref, o_ref,
                       key_ref, qh_ref, *, ts, past, topk, q_pos0):
    width = key_ref.shape[1]
    iq, iw = iq_ref[...], new_ik_ref[...]
    row_c = lax.broadcasted_iota(jnp.int32, (ts, past), 0)
    limit_c = (((q_pos0 + row_c) >> 6) + 1) << 6
    kpos_c = lax.broadcasted_iota(jnp.int32, (ts, past), 1)
    sc_c = _indexer_scores(iq, iw, cik_ref[...])
    key_ref[:, :past] = jnp.where(kpos_c < limit_c, _to_key(sc_c), NEG_KEY)
    key_ref[:, past:] = jnp.full((ts, width - past), NEG_KEY, jnp.int32)
    sc_n = _indexer_scores(iq, iw, new_ik_ref[...])
    kpos_n = past + lax.broadcasted_iota(jnp.int32, (ts, ts), 1)
    limit_n = (((q_pos0 + lax.broadcasted_iota(jnp.int32, (ts, ts), 0)) >> 6) + 1) << 6
    key_ref[:, past:past + ts] = jnp.where(kpos_n < limit_n, _to_key(sc_n), NEG_KEY)

    thr, cut = _topk_threshold(key_ref, width // LANE, LANE, topk)
    sel_c = _selected(key_ref[:, :past], 0, thr, cut)
    sel_n = _selected(key_ref[:, past:past + ts], past, thr, cut)
    _dsa_query_heads(q_ref, qh_ref)
    kc, vc = ck_ref[...].astype(BF16), cv_ref[...].astype(BF16)
    kn, vn = new_k_ref[...].astype(BF16), new_v_ref[...].astype(BF16)
    outs = []
    for hd in range(A_HEADS):
        s_c = jnp.where(sel_c, _dot_nt(qh_ref[hd], kc), -jnp.inf)
        s_n = jnp.where(sel_n, _dot_nt(qh_ref[hd], kn), -jnp.inf)
        m = jnp.maximum(jnp.max(s_c, axis=-1, keepdims=True), jnp.max(s_n, axis=-1, keepdims=True))
        p_c, p_n = jnp.exp(s_c - m), jnp.exp(s_n - m)
        l = jnp.sum(p_c, axis=-1, keepdims=True) + jnp.sum(p_n, axis=-1, keepdims=True)
        o = (jnp.dot(p_c.astype(BF16), vc, preferred_element_type=F32)
             + jnp.dot(p_n.astype(BF16), vn, preferred_element_type=F32))
        outs.append(o / l)
    for j in range(A_HEADS // 2):
        o_ref[:, j * LANE:(j + 1) * LANE] = _pair_heads(
            outs[2 * j], outs[2 * j + 1], (2 * j) // (A_HEADS // A_KV_HEADS)).astype(o_ref.dtype)


def _dsa_sample(z, cache_k, cache_v, cache_ik, l, t, nb, ts):
    past = cache_k.shape[2]
    topk = min(A_TOPK_MAX, (past + ts) // 4)
    width = ((past + ts + LANE - 1) // LANE) * LANE
    rb = t // ts
    new = lambda s: pl.BlockSpec((ts, LANE), lambda b: (rb + b, s))
    cache = lambda w: pl.BlockSpec((None, None, past, w), lambda b: (l, b, 0, 0))
    depth = cache_k.shape[0]
    return pl.pallas_call(
        functools.partial(_dsa_sample_kernel, ts=ts, past=past, topk=topk, q_pos0=past),
        out_shape=jax.ShapeDtypeStruct((nb * ts, BRANCH_WIDTH), BF16),
        grid=(nb,),
        in_specs=[pl.BlockSpec((ts, 4 * LANE), lambda b: (rb + b, S_AQ // 4)),
                  pl.BlockSpec((ts, 2 * LANE), lambda b: (rb + b, S_AIQ // 2)),
                  new(S_AIK), new(S_AK), new(S_AV), cache(LANE), cache(LANE), cache(HEAD_DIM)],
        out_specs=pl.BlockSpec((ts, BRANCH_WIDTH), lambda b: (b, 0)),
        scratch_shapes=[pltpu.VMEM((ts, width), jnp.int32), pltpu.VMEM((A_HEADS, ts, LANE), BF16)],
        compiler_params=_cparams("arbitrary"),
        name="dsa_sample",
    )(z, z, z, z, z, cache_k.reshape(depth, nb, past, LANE), cache_v.reshape(depth, nb, past, LANE), cache_ik)


def _band_prompt_kernel(q_ref, k_ref, v_ref, bias_ref, o_ref, *, tq, win):
    i = pl.program_id(1)
    w0 = pl.multiple_of(jnp.maximum(i * tq - B_WINDOW, 0), tq)
    kw = k_ref[pl.ds(w0, win), :]
    vw = v_ref[pl.ds(w0, win), :]
    kc = (w0 + lax.broadcasted_iota(jnp.int32, (tq, win), 1)) >> 6
    qc = (i * tq + lax.broadcasted_iota(jnp.int32, (tq, win), 0)) >> 6
    ok = (kc <= qc) & (kc >= qc - B_PAST_CHUNKS)
    qs = q_ref[...] * (HEAD_DIM ** -0.5)
    outs = []
    for e in range(2):
        qe = jnp.where(_half_mask(qs.shape, e), qs, 0.0).astype(BF16)
        s = jnp.where(ok, _dot_nt(qe, kw) + bias_ref[e], -jnp.inf)
        p = jnp.exp(s - jnp.max(s, axis=-1, keepdims=True))
        l = jnp.sum(p, axis=-1, keepdims=True)
        outs.append(jnp.dot(p.astype(BF16), vw, preferred_element_type=F32) / l)
    o_ref[...] = jnp.where(_half_mask(outs[0].shape, 0), outs[0], outs[1]).astype(o_ref.dtype)


def _band_prompt(z, zb, bias, t, tq):
    win = B_WINDOW + tq
    n_case = B_WINDOW // tq
    return pl.pallas_call(
        functools.partial(_band_prompt_kernel, tq=tq, win=win),
        out_shape=jax.ShapeDtypeStruct((t, BRANCH_WIDTH), BF16),
        grid=(B_HEADS // 2, t // tq),
        in_specs=[pl.BlockSpec((tq, LANE), lambda j, i: (i, S_BQ + j)),
                  pl.BlockSpec((t, LANE), lambda j, i: (0, S_BK + j)),
                  pl.BlockSpec((t, LANE), lambda j, i: (0, S_BV + j)),
                  pl.BlockSpec((None, 2, tq, win), lambda j, i: (jnp.minimum(i, n_case), j, 0, 0))],
        out_specs=pl.BlockSpec((tq, LANE), lambda j, i: (i, j)),
        compiler_params=_cparams("parallel", "arbitrary"),
        name="band_prompt",
    )(z, zb, zb, bias)


def _band_sample_kernel(q_ref, kn_ref, vn_ref, kc_ref, vc_ref, bc_ref, bn_ref, o_ref):
    kc, vc = kc_ref[...].astype(BF16), vc_ref[...].astype(BF16)
    kn, vn = kn_ref[...].astype(BF16), vn_ref[...].astype(BF16)
    qs = q_ref[...] * (HEAD_DIM ** -0.5)
    outs = []
    for e in range(2):
        qe = jnp.where(_half_mask(qs.shape, e), qs, 0.0).astype(BF16)
        s_c = _dot_nt(qe, kc) + bc_ref[e]
        s_n = _dot_nt(qe, kn) + bn_ref[e]
        m = jnp.maximum(jnp.max(s_c, axis=-1, keepdims=True), jnp.max(s_n, axis=-1, keepdims=True))
        p_c, p_n = jnp.exp(s_c - m), jnp.exp(s_n - m)
        l = jnp.sum(p_c, axis=-1, keepdims=True) + jnp.sum(p_n, axis=-1, keepdims=True)
        o = (jnp.dot(p_c.astype(BF16), vc, preferred_element_type=F32)
             + jnp.dot(p_n.astype(BF16), vn, preferred_element_type=F32))
        outs.append(o / l)
    o_ref[...] = jnp.where(_half_mask(outs[0].shape, 0), outs[0], outs[1]).astype(o_ref.dtype)


def _band_sample(z, cache_k, cache_v, bias_c, bias_n, l, t, nb, ts):
    depth, _, pb = cache_k.shape[:3]
    rb = t // ts
    cache = pl.BlockSpec((None, None, pb, LANE), lambda b, j: (l, b, 0, j))
    return pl.pallas_call(
        _band_sample_kernel,
        out_shape=jax.ShapeDtypeStruct((nb * ts, BRANCH_WIDTH), BF16),
        grid=(nb, B_HEADS // 2),
        in_specs=[pl.BlockSpec((ts, LANE), lambda b, j: (rb + b, S_BQ + j)),
                  pl.BlockSpec((ts, LANE), lambda b, j: (rb + b, S_BK + j)),
                  pl.BlockSpec((ts, LANE), lambda b, j: (rb + b, S_BV + j)),
                  cache, cache,
                  pl.BlockSpec((2, ts, pb), lambda b, j: (j, 0, 0)),
                  pl.BlockSpec((2, ts, ts), lambda b, j: (j, 0, 0))],
        out_specs=pl.BlockSpec((ts, LANE), lambda b, j: (b, j)),
        compiler_params=_cparams("parallel", "arbitrary"),
        name="band_sample",
    )(z, z, z, cache_k.reshape(depth, nb, pb, BRANCH_WIDTH), cache_v.reshape(depth, nb, pb, BRANCH_WIDTH),
      bias_c, bias_n)


def _band_bias(table, tq):
    n_case = B_WINDOW // tq
    win = B_WINDOW + tq
    width = win + B_WINDOW
    j = np.arange(width + tq - 1) - (tq - 1) - B_WINDOW
    ext = table.astype(F32)[:, np.clip(j, -B_REL_CLIP, B_REL_CLIP) + B_REL_CLIP]
    toe = jnp.stack([ext[:, tq - 1 - r:tq - 1 - r + width] for r in range(tq)], axis=1)
    return jnp.stack([toe[:, :, B_WINDOW - c * tq:B_WINDOW - c * tq + win] for c in range(n_case + 1)], axis=0)


def _lambda(lam_ref, lam_init):
    lv = lam_ref[...]
    return (jnp.exp(jnp.sum(lv[0:1] * lv[1:2], axis=-1, keepdims=True))
            - jnp.exp(jnp.sum(lv[2:3] * lv[3:4], axis=-1, keepdims=True)) + lam_init)


def _diff_finish(o0, o1, lam, on_ref, lam_init):
    attn = o0 - lam * o1
    ms = jnp.mean(attn * attn, axis=-1, keepdims=True)
    return (attn * lax.rsqrt(ms + NORM_EPS) * on_ref[...]) * (1.0 - lam_init)


def _diff_prompt_kernel(q_ref, k_ref, v_ref, lam_ref, on_ref, o_ref, qh_ref, m_ref, l_ref, acc_ref, s_ref, p_ref,
                        *, tq, kb, lam_init):
    i = pl.program_id(1)
    q0 = i * tq
    n_blk = (q0 + tq + kb - 1) // kb
    row = lax.broadcasted_iota(jnp.int32, (tq, kb), 0)
    limit = (((q0 + row) >> 6) + 1) << 6
    qs = q_ref[...] * (HEAD_DIM ** -0.5)
    for c in range(2):
        qh_ref[c] = jnp.where(_half_mask(qs.shape, c), qs, 0.0).astype(BF16)
    _softmax_init(m_ref, l_ref, acc_ref)

    def attn_blk(b, carry):
        start = pl.multiple_of(b * kb, kb)
        ok = start + lax.broadcasted_iota(jnp.int32, (tq, kb), 1) < limit
        _softmax_block(ok, k_ref[pl.ds(start, kb), :], v_ref[pl.ds(start, kb), :],
                       [qh_ref[c] for c in range(2)], m_ref, l_ref, acc_ref, s_ref, p_ref)
        return carry

    lax.fori_loop(0, n_blk, attn_blk, 0)
    lam = _lambda(lam_ref, lam_init)
    outs = [acc_ref[c] / jnp.sum(l_ref[c], axis=-1, keepdims=True) for c in range(2)]
    o_ref[...] = _diff_finish(outs[0], outs[1], lam, on_ref, lam_init).astype(o_ref.dtype)


def _diff_prompt(z, zb, c_lam_all, c_on_all, l, t, lam_init):
    tq, kb = _tile(t, 256), 512
    maps = lambda dt, w=LANE: pltpu.VMEM((2, tq, w), dt)
    return pl.pallas_call(
        functools.partial(_diff_prompt_kernel, tq=tq, kb=kb, lam_init=lam_init),
        out_shape=jax.ShapeDtypeStruct((t, BRANCH_WIDTH), BF16),
        grid=(C_HEADS, t // tq),
        in_specs=[pl.BlockSpec((tq, LANE), lambda h, i: (i, S_CQ + h)),
                  pl.BlockSpec((t, LANE), lambda h, i: (0, S_CK + h)),
                  pl.BlockSpec((t, LANE), lambda h, i: (0, S_CV + h)),
                  pl.BlockSpec((None, 4, HEAD_DIM), lambda h, i: (l, 0, 0)),
                  _layer_vec(l, LANE)],
        out_specs=pl.BlockSpec((tq, LANE), lambda h, i: (i, h)),
        scratch_shapes=[maps(BF16), maps(F32), maps(F32), maps(F32), maps(F32, kb), maps(BF16, kb)],
        compiler_params=_cparams("parallel", "arbitrary"),
        name="diff_prompt",
    )(z, zb, zb, c_lam_all, _vec3(c_on_all))


def _diff_sample_kernel(q_ref, kn_ref, vn_ref, kc_ref, vc_ref, lam_ref, on_ref, o_ref, *, lam_init):
    kc, vc = kc_ref[...].astype(BF16), vc_ref[...].astype(BF16)
    kn, vn = kn_ref[...].astype(BF16), vn_ref[...].astype(BF16)
    qs = q_ref[...] * (HEAD_DIM ** -0.5)
    outs = []
    for c in range(2):
        qc = jnp.where(_half_mask(qs.shape, c), qs, 0.0).astype(BF16)
        s_c = _dot_nt(qc, kc)
        s_n = _dot_nt(qc, kn)
        m = jnp.maximum(jnp.max(s_c, axis=-1, keepdims=True), jnp.max(s_n, axis=-1, keepdims=True))
        p_c, p_n = jnp.exp(s_c - m), jnp.exp(s_n - m)
        l = jnp.sum(p_c, axis=-1, keepdims=True) + jnp.sum(p_n, axis=-1, keepdims=True)
        o = (jnp.dot(p_c.astype(BF16), vc, preferred_element_type=F32)
             + jnp.dot(p_n.astype(BF16), vn, preferred_element_type=F32))
        outs.append(o / l)
    lam = _lambda(lam_ref, lam_init)
    o_ref[...] = _diff_finish(outs[0], outs[1], lam, on_ref, lam_init).astype(o_ref.dtype)


def _diff_sample(z, cache_k, cache_v, c_lam_all, c_on_all, l, t, nb, ts, lam_init):
    depth, _, past = cache_k.shape[:3]
    rb = t // ts
    cache = pl.BlockSpec((None, None, past, LANE), lambda b, h: (l, b, 0, h))
    return pl.pallas_call(
        functools.partial(_diff_sample_kernel, lam_init=lam_init),
        out_shape=jax.ShapeDtypeStruct((nb * ts, BRANCH_WIDTH), BF16),
        grid=(nb, C_HEADS),
        in_specs=[pl.BlockSpec((ts, LANE), lambda b, h: (rb + b, S_CQ + h)),
                  pl.BlockSpec((ts, LANE), lambda b, h: (rb + b, S_CK + h)),
                  pl.BlockSpec((ts, LANE), lambda b, h: (rb + b, S_CV + h)),
                  cache, cache,
                  pl.BlockSpec((None, 4, HEAD_DIM), lambda b, h: (l, 0, 0)),
                  _layer_vec(l, LANE)],
        out_specs=pl.BlockSpec((ts, LANE), lambda b, h: (b, h)),
        compiler_params=_cparams("parallel", "arbitrary"),
        name="diff_sample",
    )(z, z, z, cache_k.reshape(depth, nb, past, BRANCH_WIDTH), cache_v.reshape(depth, nb, past, BRANCH_WIDTH),
      c_lam_all, _vec3(c_on_all))


def _rwkv_pre_kernel(zd_ref, prev_ref, first_ref, mu_ref, w0_ref, w2_ref, a0_ref, a2_ref, g2_ref, kkw_ref,
                     ka_ref, bd_ref, r_ref, w_ref, k_ref, v_ref, kk_ref, b_ref, g_ref, *, tm, t, ts):
    zf = zd_ref[...]
    grow = pl.program_id(0) * tm + lax.broadcasted_iota(jnp.int32, (tm, 1), 0)
    zs = jnp.where(grow % tm == 0, prev_ref[7:8, :], pltpu.roll(zf, 1, 0))
    seq_start = (grow >= t) & ((grow - t) % ts == 0)
    zs = jnp.where(seq_start, first_ref[...], zs)
    zs = jnp.where(grow == 0, 0.0, zs)
    zm = zf + (zs - zf) * mu_ref[...]
    bw = BRANCH_WIDTH
    r, k, v = zm[:, :bw], zm[:, bw:2 * bw], zm[:, 2 * bw:3 * bw]
    wl, al, gl = zm[:, 3 * bw:3 * bw + 64], zm[:, 3 * bw + 64:3 * bw + 128], zm[:, 3 * bw + 128:]
    dot = lambda a, b: jnp.dot(a.astype(BF16), b.astype(BF16), preferred_element_type=F32)
    u = -(w0_ref[...] + dot(jnp.tanh(wl), w2_ref[...]))
    softplus = jnp.maximum(u, 0.0) + jnp.log(1.0 + jnp.exp(-jnp.abs(u)))
    w = -softplus - 0.5
    a = jax.nn.sigmoid(a0_ref[...] + dot(al, a2_ref[...]))
    kk = k * kkw_ref[...]
    nrm = jnp.sqrt(_seg_sum(kk * kk, bd_ref[...]))
    kk = kk / jnp.maximum(nrm, 1e-12)
    r_ref[...] = r
    w_ref[...] = jnp.exp(-jnp.exp(w))
    k_ref[...] = k * (1.0 + (a - 1.0) * ka_ref[...])
    v_ref[...] = v
    kk_ref[...] = kk
    b_ref[...] = kk * a
    g_ref[...] = dot(jax.nn.sigmoid(gl), g2_ref[...])


def _rwkv_pre(zd, zfirst, p, l, t, ts):
    m, dc = zd.shape
    tm = _tile(math.gcd(t, m - t), 512)
    bw = BRANCH_WIDTH
    npt = t // tm
    rows = pl.BlockSpec((tm, dc), lambda i: (i, 0))
    vec = lambda n: _layer_vec(l, n)
    mat = lambda k: pl.BlockSpec((None, k, bw), lambda i: (l, 0, 0))
    out = jax.ShapeDtypeStruct((m, bw), F32)
    return pl.pallas_call(
        functools.partial(_rwkv_pre_kernel, tm=tm, t=t, ts=ts),
        out_shape=[out] * 7,
        grid=(m // tm,),
        in_specs=[rows,
                  pl.BlockSpec((8, dc), lambda i: (jnp.maximum(i * (tm // 8) - 1, 0), 0)),
                  pl.BlockSpec((tm, dc), lambda i: (jnp.maximum(i - npt, 0), 0)),
                  vec(dc), vec(bw), mat(64), vec(bw), mat(64), mat(128), vec(bw), vec(bw),
                  pl.BlockSpec((bw, bw), lambda i: (0, 0))],
        out_specs=[pl.BlockSpec((tm, bw), lambda i: (i, 0))] * 7,
        compiler_params=_cparams("parallel"),
        name="rwkv_pre",
    )(zd, zd, zfirst, _vec3(p["d_mu"]), _vec3(p["d_w0"]), p["d_w2"], _vec3(p["d_a0"]), p["d_a2"], p["d_g2"],
      _vec3(p["d_k_k"]), _vec3(p["d_k_a"]), _ones_blockdiag(bw))


def _rwkv_scan_kernel(r_ref, w_ref, k_ref, kk_ref, b_ref, v_ref, s0_ref, y_ref, sf_ref, s_ref, yt_ref, *, tb):
    @pl.when(pl.program_id(1) == 0)
    def _():
        s_ref[...] = s0_ref[...]

    vt = v_ref[...].T
    lane_t = lax.broadcasted_iota(jnp.int32, (BRANCH_WIDTH, tb), 1)

    def bcast(ref, t):
        x8 = ref[pl.ds(pl.multiple_of(t * D_HEADS, D_HEADS), D_HEADS), :]
        return jnp.concatenate(
            [jnp.broadcast_to(x8[hd:hd + 1, :], (HEAD_DIM, HEAD_DIM)) for hd in range(D_HEADS)], axis=0)

    def step(t, carry):
        s = s_ref[...]
        sa = jnp.sum(s * bcast(kk_ref, t), axis=-1, keepdims=True) * -1.0
        vcol = jnp.sum(jnp.where(lane_t == t, vt, 0.0), axis=-1, keepdims=True)
        s = s * bcast(w_ref, t) + sa * bcast(b_ref, t) + vcol * bcast(k_ref, t)
        s_ref[...] = s
        y = jnp.sum(s * bcast(r_ref, t), axis=-1, keepdims=True)
        yt_ref[...] = jnp.where(lane_t == t, y, yt_ref[...])
        return carry

    lax.fori_loop(0, tb, step, 0)
    y_ref[...] = yt_ref[...].T

    @pl.when(pl.program_id(1) == pl.num_programs(1) - 1)
    def _():
        sf_ref[...] = s_ref[...]


def _rwkv_scan(ops8, v, s0, row0, nb, t):
    tb = _tile(t, 128)
    ntb = t // tb
    rb = row0 // tb
    op_spec = pl.BlockSpec((tb * D_HEADS, HEAD_DIM), lambda b, i: (rb + b * ntb + i, 0))
    state = pl.BlockSpec((None, BRANCH_WIDTH, HEAD_DIM), lambda b, i: (b, 0, 0))
    y, sf = pl.pallas_call(
        functools.partial(_rwkv_scan_kernel, tb=tb),
        out_shape=[jax.ShapeDtypeStruct((nb * t, BRANCH_WIDTH), F32),
                   jax.ShapeDtypeStruct((nb, BRANCH_WIDTH, HEAD_DIM), F32)],
        grid=(nb, ntb),
        in_specs=[op_spec] * 5 + [pl.BlockSpec((tb, BRANCH_WIDTH), lambda b, i: (rb + b * ntb + i, 0)), state],
        out_specs=[pl.BlockSpec((tb, BRANCH_WIDTH), lambda b, i: (b * ntb + i, 0)), state],
        scratch_shapes=[pltpu.VMEM((BRANCH_WIDTH, HEAD_DIM), F32), pltpu.VMEM((BRANCH_WIDTH, tb), F32)],
        compiler_params=_cparams("parallel", "arbitrary"),
        name="rwkv_scan",
    )(*ops8, v, s0.reshape(nb, BRANCH_WIDTH, HEAD_DIM))
    return y, sf.reshape(nb, D_HEADS, HEAD_DIM, HEAD_DIM)


def _rwkv_post_kernel(y_ref, r_ref, k_ref, v_ref, g_ref, lnw_ref, lnb_ref, rk_ref, bd_ref, o_ref):
    bd = bd_ref[...]
    y = y_ref[...]
    mean = _seg_sum(y, bd) * (1.0 / HEAD_DIM)
    yc = y - mean
    var = _seg_sum(yc * yc, bd) * (1.0 / HEAD_DIM)
    yn = yc * lax.rsqrt(var + D_GN_EPS) * lnw_ref[...] + lnb_ref[...]
    bonus = _seg_sum(r_ref[...] * k_ref[...] * rk_ref[...], bd) * v_ref[...]
    o_ref[...] = ((yn + bonus) * g_ref[...]).astype(o_ref.dtype)


def _rwkv_post(y, r, k, v, g, p, l):
    m, bw = y.shape
    tm = _tile(m, 512)
    rows = pl.BlockSpec((tm, bw), lambda i: (i, 0))
    vec = _layer_vec(l, bw)
    return pl.pallas_call(
        _rwkv_post_kernel,
        out_shape=jax.ShapeDtypeStruct((m, bw), BF16),
        grid=(m // tm,),
        in_specs=[rows] * 5 + [vec] * 3 + [pl.BlockSpec((bw, bw), lambda i: (0, 0))],
        out_specs=rows,
        compiler_params=_cparams("parallel"),
        name="rwkv_post",
    )(y, r, k, v, g, _vec3(p["d_ln_w"]), _vec3(p["d_ln_b"]), _vec3(p["d_r_k"]), _ones_blockdiag(bw))


def _merge_kernel(o_ref, wbr_ref, h_ref, wg_ref, out_ref, acc_ref):
    n = pl.program_id(2)

    @pl.when(n == 0)
    def _():
        acc_ref[...] = jnp.zeros_like(acc_ref)

    u = jnp.dot(o_ref[...], wbr_ref[...].astype(BF16), preferred_element_type=F32)
    gate = jax.nn.sigmoid(jnp.dot(h_ref[...], wg_ref[...], preferred_element_type=F32))
    acc_ref[...] += u * gate

    @pl.when(n == N_BRANCH - 1)
    def _():
        out_ref[...] = acc_ref[...].astype(out_ref.dtype)


def _merge(o_all, w_br_all, h, w_pack, l):
    m, d = h.shape
    tm, tn = _tile(m, 1024), 512
    nj = d // tn
    g0 = S_GATE * LANE // tn
    return pl.pallas_call(
        _merge_kernel,
        out_shape=jax.ShapeDtypeStruct((m, d), BF16),
        grid=(m // tm, nj, N_BRANCH),
        in_specs=[pl.BlockSpec((tm, BRANCH_WIDTH), lambda i, j, n: (i, n)),
                  pl.BlockSpec((None, None, BRANCH_WIDTH, tn), lambda i, j, n: (l, n, 0, j)),
                  pl.BlockSpec((tm, d), lambda i, j, n: (i, 0)),
                  pl.BlockSpec((d, tn), lambda i, j, n: (0, g0 + n * nj + j))],
        out_specs=pl.BlockSpec((tm, tn), lambda i, j, n: (i, j)),
        scratch_shapes=[pltpu.VMEM((tm, tn), F32)],
        compiler_params=_cparams("parallel", "parallel", "arbitrary"),
        name="branch_merge",
    )(o_all, w_br_all, h, w_pack)


def _ple_kernel(x_ref, pe_ref, wple_ref, h_ref, wg_ref, o_ref):
    emb = jnp.dot(pe_ref[...].astype(BF16), wple_ref[...].astype(BF16), preferred_element_type=F32)
    gate = jax.nn.sigmoid(jnp.dot(h_ref[...], wg_ref[...].astype(BF16), preferred_element_type=F32))
    o_ref[...] = x_ref[...] + emb * gate


def _ple(x, pe, w_ple_all, h, w_gate_all, l):
    m, d = x.shape
    pd = pe.shape[1]
    tm, tn = _tile(m, 1024), 512
    return pl.pallas_call(
        _ple_kernel,
        out_shape=jax.ShapeDtypeStruct((m, d), F32),
        grid=(m // tm, d // tn),
        in_specs=[pl.BlockSpec((tm, tn), lambda i, j: (i, j)),
                  pl.BlockSpec((tm, pd), lambda i, j: (i, 0)),
                  pl.BlockSpec((None, pd, tn), lambda i, j: (l, 0, j)),
                  pl.BlockSpec((tm, d), lambda i, j: (i, 0)),
                  pl.BlockSpec((None, d, tn), lambda i, j: (l, 0, j))],
        out_specs=pl.BlockSpec((tm, tn), lambda i, j: (i, j)),
        compiler_params=_cparams("parallel", "parallel"),
        name="ple",
    )(x, pe, w_ple_all, h, w_gate_all)


def _layer(l, x, pe, p, t, nb, ts, tabs):
    m = x.shape[0]
    lam_init = 0.8 - 0.6 * math.exp(-0.3 * l)
    dc = p["d_mu"].shape[1]
    d = x.shape[1]
    lw = {"a_qn": p["a_q_norm"][l], "a_kn": p["a_k_norm"][l], "b_qn": p["b_q_norm"][l], "b_kn": p["b_k_norm"][l],
          "c_qn": p["c_q_norm"][l], "c_kn": p["c_k_norm"][l]}

    h = _rmsnorm(x, p["norm1_g"], l)
    w_pack = _pack_w_in(p["w_in"], l)
    z, zb = _qkv_projection(h, w_pack, *_column_vectors(lw), *tabs)
    zd = _matmul(h, w_pack, lambda acc: acc, F32, n=dc, col0=S_D * LANE, tn=256, name="rwkv_projection")

    oa = jnp.concatenate([_dsa_prompt(z, zb, t),
                          _dsa_sample(z, p["cache_a_k"], p["cache_a_v"], p["cache_a_kidx"], l, t, nb, ts)], axis=0)
    tq_b = 128
    bias = _band_bias(p["b_rel_bias"][l], tq_b)
    pb = p["cache_b_k"].shape[2]
    generic = bias[B_WINDOW // tq_b]
    ob = jnp.concatenate([
        _band_prompt(z, zb, bias, t, tq_b),
        _band_sample(z, p["cache_b_k"], p["cache_b_v"], generic[:, :ts, B_WINDOW - pb:B_WINDOW],
                     generic[:, :ts, B_WINDOW:B_WINDOW + ts], l, t, nb, ts)], axis=0)
    c_on = p["c_out_norm"]
    oc = jnp.concatenate([_diff_prompt(z, zb, p["c_lambda"], c_on, l, t, lam_init),
                          _diff_sample(z, p["cache_c_k"], p["cache_c_v"], p["c_lambda"], c_on, l, t, nb, ts, lam_init)],
                         axis=0)

    zfirst = jnp.broadcast_to(p["state_d_shift"][l], (nb, ts, dc)).reshape(nb * ts, dc)
    r, w, k, v, kk, b, g = _rwkv_pre(zd, zfirst, p, l, t, ts)
    ops8 = [a.reshape(m * D_HEADS, HEAD_DIM) for a in (r, w, k, kk, b)]
    y_p, wkv_p = _rwkv_scan(ops8, v, jnp.zeros((1, D_HEADS, HEAD_DIM, HEAD_DIM), F32), 0, 1, t)
    y_s, wkv_s = _rwkv_scan(ops8, v, p["state_d_wkv"][l], t, nb, ts)
    od = _rwkv_post(jnp.concatenate([y_p, y_s], axis=0), r, k, v, g, p, l)

    ug = _merge(jnp.concatenate([oa, ob, oc, od], axis=1), p["w_branch"], h, w_pack, l)
    res = lambda acc, r_: r_ + acc
    x = _matmul(ug, p["w_out"], res, F32, n=d, layer=l, residual=x, name="out_proj")
    h2 = _rmsnorm(x, p["norm2_g"], l)
    ffn = p["w_up"].shape[2]
    up = _matmul(h2, p["w_up"], lambda acc: jnp.square(jnp.maximum(acc, 0.0)), BF16, n=ffn, layer=l, name="mlp_up")
    x = _matmul(up, p["w_down"], res, F32, n=d, layer=l, residual=x, name="mlp_down")
    h3 = _rmsnorm(x, p["norm3_g"], l)
    x = _ple(x, pe, p["w_ple"], h3, p["w_ple_gate"], l)

    slot = lambda rows, s, n=1: z[rows, s * LANE:(s + n) * LANE]

    def rows_of(sl, lead):
        a = lambda s, n, shape: slot(sl, s, n).reshape(lead + shape)
        ak = a(S_AK, 1, (A_KV_HEADS, HEAD_DIM))
        av = a(S_AV, 1, (A_KV_HEADS, HEAD_DIM))
        aik = slot(sl, S_AIK)[:, :HEAD_DIM].reshape(lead + (HEAD_DIM,))
        bk = a(S_BK, 4, (B_HEADS, HEAD_DIM))
        bv = a(S_BV, 4, (B_HEADS, HEAD_DIM))
        ck = a(S_CK, 4, (C_HEADS, 2, HEAD_DIM))
        cv = a(S_CV, 4, (C_HEADS, 2 * HEAD_DIM))
        return ak, av, aik, bk, bv, ck, cv

    keep = min(B_WINDOW, t)
    pak, pav, paik, pbk, pbv, pck, pcv = rows_of(slice(0, t), (1, t))
    new_p = (pak, pav, paik, pbk[:, t - keep:], pbv[:, t - keep:], pck, pcv, wkv_p, zd[t - 1:t].reshape(1, 1, dc))
    new_s = rows_of(slice(t, m), (nb, ts)) + (wkv_s, zd[t:].reshape(nb, ts, dc)[:, -1:])
    return x, new_p, new_s


def kernel(x_prompt, x_sample, cache_a_k, cache_a_v, cache_a_kidx, cache_b_k, cache_b_v, cache_c_k, cache_c_v, state_d_wkv, state_d_shift, p_prompt, p_sample, norm1_g, w_in, a_q_norm, a_k_norm, b_q_norm, b_k_norm, b_rel_bias, c_q_norm, c_k_norm, c_lambda, c_out_norm, d_mu, d_w0, d_w2, d_a0, d_a2, d_g2, d_k_k, d_k_a, d_r_k, d_ln_w, d_ln_b, w_branch, w_out, norm2_g, w_up, w_down, norm3_g, w_ple, w_ple_gate):
    batch, t, d = x_prompt.shape
    nb, ts, _ = x_sample.shape
    past = cache_a_k.shape[2]
    depth = w_in.shape[0]
    assert batch == 1 and t % 512 == 0 and past % CHUNK == 0 and ts <= CHUNK and (nb * ts) % 8 == 0
    p = dict(cache_a_k=cache_a_k, cache_a_v=cache_a_v, cache_a_kidx=cache_a_kidx, cache_b_k=cache_b_k,
             cache_b_v=cache_b_v, cache_c_k=cache_c_k, cache_c_v=cache_c_v, state_d_wkv=state_d_wkv,
             state_d_shift=state_d_shift, norm1_g=norm1_g, w_in=w_in, a_q_norm=a_q_norm, a_k_norm=a_k_norm,
             b_q_norm=b_q_norm, b_k_norm=b_k_norm, b_rel_bias=b_rel_bias, c_q_norm=c_q_norm, c_k_norm=c_k_norm,
             c_lambda=c_lambda, c_out_norm=c_out_norm, d_mu=d_mu, d_w0=d_w0, d_w2=d_w2, d_a0=d_a0, d_a2=d_a2,
             d_g2=d_g2, d_k_k=d_k_k, d_k_a=d_k_a, d_r_k=d_r_k, d_ln_w=d_ln_w, d_ln_b=d_ln_b, w_branch=w_branch,
             w_out=w_out, norm2_g=norm2_g, w_up=w_up, w_down=w_down, norm3_g=norm3_g, w_ple=w_ple,
             w_ple_gate=w_ple_gate)
    x = jnp.concatenate([x_prompt[0], x_sample.reshape(nb * ts, d)], axis=0)
    pos = jnp.concatenate([jnp.arange(t, dtype=jnp.int32),
                           jnp.tile(past + jnp.arange(ts, dtype=jnp.int32), nb)])
    tabs = _rope_tables(pos)
    st_p = [[] for _ in range(9)]
    st_s = [[] for _ in range(9)]
    for l in range(depth):
        pe = jnp.concatenate([p_prompt[l, 0], p_sample[l].reshape(nb * ts, -1)], axis=0)
        x, new_p, new_s = _layer(l, x, pe, p, t, nb, ts, tabs)
        for lst, arr in zip(st_p, new_p):
            lst.append(arr)
        for lst, arr in zip(st_s, new_s):
            lst.append(arr)
    outs_p = [jnp.stack(s, axis=0) for s in st_p]
    outs_s = [jnp.stack(s, axis=0) for s in st_s]
    return (x[:t].reshape(1, t, d), x[t:].reshape(nb, ts, d), *outs_p, *outs_s)
```

```python
import functools
import math

import numpy as np
import jax
import jax.numpy as jnp
from jax import lax
from jax.experimental import pallas as pl
from jax.experimental.pallas import tpu as pltpu

F32 = jnp.float32
BF16 = jnp.bfloat16

CHUNK = 64
HEAD_DIM = 64
ROPE_DIM = 16
ROPE_THETA = 500000.0
N_BRANCH = 4
BRANCH_WIDTH = 512
A_HEADS, A_KV_HEADS, A_IDX_HEADS = 8, 2, 4
A_TOPK_MAX = 256
B_HEADS = 8
B_PAST_CHUNKS = 8
B_WINDOW = B_PAST_CHUNKS * CHUNK
B_REL_CLIP = 128
C_HEADS = 4
D_HEADS = 8
D_GN_EPS = 64e-5
NORM_EPS = 1e-6

LANE = 128
VMEM_LIMIT = 48 * 1024 * 1024

S_AQ, S_AK, S_AV, S_AIQ, S_AIK = 0, 4, 5, 6, 8
S_BQ, S_BK, S_BV = 10, 14, 18
S_CQ, S_CK, S_CV = 22, 26, 30
S_D, S_GATE = 34, 48
N_QKV_SLOTS = 34
N_D_SLOTS = 14
A_COLS = 1092
A_SLOTS = 9
SHIFT = A_COLS - (A_SLOTS - 1) * LANE

NEG_KEY = -2139095041
INT_MIN = -2147483648
M_FLOOR = -1e30


def _cparams(*sem):
    return pltpu.CompilerParams(dimension_semantics=sem, vmem_limit_bytes=VMEM_LIMIT)


def _tile(n, pref):
    t = min(n, pref)
    while n % t:
        t -= 8
    return t


def _split2(x):
    hi = x.astype(BF16)
    lo = (x - hi.astype(F32)).astype(BF16)
    return hi, lo


def _seg_sum(x, ones_bd):
    hi, lo = _split2(x)
    return (jnp.dot(hi, ones_bd, preferred_element_type=F32)
            + jnp.dot(lo, ones_bd, preferred_element_type=F32))


def _dot_nt(a, b):
    return lax.dot_general(a, b, (((1,), (1,)), ((), ())), preferred_element_type=F32)


def _dot_nt3(a, b):
    ah, al = _split2(a)
    bh, bl = _split2(b)
    return _dot_nt(ah, bh) + _dot_nt(ah, bl) + _dot_nt(al, bh)


def _ones_blockdiag(n):
    i = np.arange(n)
    return jnp.asarray((i[:, None] // HEAD_DIM) == (i[None, :] // HEAD_DIM), dtype=BF16)


def _vec3(a):
    return a.reshape(a.shape[0], 1, -1)


def _layer_vec(l, n):
    return pl.BlockSpec((None, 1, n), lambda *_: (l, 0, 0))


def _half_mask(shape, half):
    lane = lax.broadcasted_iota(jnp.int32, shape, len(shape) - 1)
    return (lane < HEAD_DIM) if half == 0 else (lane >= HEAD_DIM)


def _rms_kernel(x_ref, g_ref, o_ref):
    x = x_ref[...]
    ms = jnp.mean(x * x, axis=-1, keepdims=True)
    o_ref[...] = (x * lax.rsqrt(ms + NORM_EPS) * g_ref[...]).astype(o_ref.dtype)


def _rmsnorm(x, g_all, l):
    m, d = x.shape
    tm = _tile(m, 512)
    return pl.pallas_call(
        _rms_kernel,
        out_shape=jax.ShapeDtypeStruct((m, d), BF16),
        grid=(m // tm,),
        in_specs=[pl.BlockSpec((tm, d), lambda i: (i, 0)), _layer_vec(l, d)],
        out_specs=pl.BlockSpec((tm, d), lambda i: (i, 0)),
        compiler_params=_cparams("parallel"),
        name="rmsnorm",
    )(x, _vec3(g_all))


def _pack_kernel(a_ref, b_ref, o_ref):
    j = pl.program_id(0)
    a, b = a_ref[...], b_ref[...]
    lane = lax.broadcasted_iota(jnp.int32, a.shape, 1)
    shifted = jnp.where(lane < LANE - SHIFT, pltpu.roll(a, LANE - SHIFT, 1), pltpu.roll(b, LANE - SHIFT, 1))
    out = jnp.where(j < A_SLOTS, a, jnp.where(j == A_SLOTS, 0.0, shifted))
    o_ref[...] = out.astype(o_ref.dtype)


def _pack_w_in(w_in_all, l):
    _, d, n_in = w_in_all.shape
    n_slots = A_SLOTS + 1 + (n_in - A_COLS) // LANE
    assert (n_in - A_COLS) % LANE == 0 and 0 < SHIFT < LANE
    src_a = lambda j: jnp.where(j < A_SLOTS, j, j - 2)
    src_b = lambda j: jnp.where(j < A_SLOTS, j, j - 1)
    return pl.pallas_call(
        _pack_kernel,
        out_shape=jax.ShapeDtypeStruct((d, n_slots * LANE), BF16),
        grid=(n_slots,),
        in_specs=[pl.BlockSpec((None, d, LANE), lambda j: (l, 0, src_a(j))),
                  pl.BlockSpec((None, d, LANE), lambda j: (l, 0, src_b(j)))],
        out_specs=pl.BlockSpec((d, LANE), lambda j: (0, j)),
        compiler_params=_cparams("parallel"),
        name="pack_w_in",
    )(w_in_all, w_in_all)


def _mm_kernel(*refs, n_extra, nk, epilogue):
    a_ref, w_ref = refs[0], refs[1]
    extra = refs[2:2 + n_extra]
    o_ref = refs[2 + n_extra]
    acc_ref = refs[3 + n_extra]
    k = pl.program_id(2)

    @pl.when(k == 0)
    def _():
        acc_ref[...] = jnp.zeros_like(acc_ref)

    acc_ref[...] += jnp.dot(a_ref[...], w_ref[...].astype(BF16), preferred_element_type=F32)

    @pl.when(k == nk - 1)
    def _():
        o_ref[...] = epilogue(acc_ref[...], *[e[...] for e in extra]).astype(o_ref.dtype)


def _matmul(a, w, epilogue, out_dtype, *, n, layer=None, col0=0, residual=None,
            tm=1024, tn=512, tk=2048, name="matmul"):
    m, kdim = a.shape
    tm, tn, tk = _tile(m, tm), _tile(n, tn), _tile(kdim, tk)
    nk = kdim // tk
    cb = col0 // tn
    assert col0 % tn == 0
    if layer is None:
        w_spec = pl.BlockSpec((tk, tn), lambda i, j, k: (k, cb + j))
    else:
        w_spec = pl.BlockSpec((None, tk, tn), lambda i, j, k: (layer, k, cb + j))
    specs = [pl.BlockSpec((tm, tk), lambda i, j, k: (i, k)), w_spec]
    extras = []
    if residual is not None:
        specs.append(pl.BlockSpec((tm, tn), lambda i, j, k: (i, j)))
        extras.append(residual)
    return pl.pallas_call(
        functools.partial(_mm_kernel, n_extra=len(extras), nk=nk, epilogue=epilogue),
        out_shape=jax.ShapeDtypeStruct((m, n), out_dtype),
        grid=(m // tm, n // tn, nk),
        in_specs=specs,
        out_specs=pl.BlockSpec((tm, tn), lambda i, j, k: (i, j)),
        scratch_shapes=[pltpu.VMEM((tm, tn), F32)],
        compiler_params=_cparams("parallel", "parallel", "arbitrary"),
        name=name,
    )(a, w, *extras)


def _proj_kernel(h_ref, w_ref, gain_ref, nf_ref, rf_ref, cos_ref, sa_ref, sb_ref, bd_ref, o_ref, ob_ref):
    x = jnp.dot(h_ref[...], w_ref[...], preferred_element_type=F32)
    ms = _seg_sum(x * x, bd_ref[...]) * (1.0 / HEAD_DIM)
    scale = jnp.where(nf_ref[...] > 0.5, lax.rsqrt(ms + NORM_EPS) * gain_ref[...], 1.0)
    y = x * scale
    rf = rf_ref[...]
    cosv, sav, sbv = cos_ref[...], sa_ref[...], sb_ref[...]
    for s in range(2):
        ys = y[:, s * LANE:(s + 1) * LANE]
        f = rf[:, s * LANE:(s + 1) * LANE]
        roped = ys * cosv + pltpu.roll(ys, 8, 1) * sav + pltpu.roll(ys, LANE - 8, 1) * sbv
        out = jnp.where(f > 0.5, roped, ys)
        o_ref[:, s * LANE:(s + 1) * LANE] = out
        ob_ref[:, s * LANE:(s + 1) * LANE] = out.astype(BF16)


def _qkv_projection(h, w_pack, gain, nf, rf, cos_t, sin_a, sin_b):
    m, d = h.shape
    tm, tn = _tile(m, 1024), 2 * LANE
    zw = N_QKV_SLOTS * LANE
    row = lambda i, j: (0, j)
    tab = lambda i, j: (i, 0)
    return pl.pallas_call(
        _proj_kernel,
        out_shape=[jax.ShapeDtypeStruct((m, zw), F32), jax.ShapeDtypeStruct((m, zw), BF16)],
        grid=(m // tm, zw // tn),
        in_specs=[pl.BlockSpec((tm, d), lambda i, j: (i, 0)), pl.BlockSpec((d, tn), lambda i, j: (0, j)),
                  pl.BlockSpec((1, tn), row), pl.BlockSpec((1, tn), row), pl.BlockSpec((1, tn), row),
                  pl.BlockSpec((tm, LANE), tab), pl.BlockSpec((tm, LANE), tab), pl.BlockSpec((tm, LANE), tab),
                  pl.BlockSpec((tn, tn), lambda i, j: (0, 0))],
        out_specs=[pl.BlockSpec((tm, tn), lambda i, j: (i, j))] * 2,
        compiler_params=_cparams("parallel", "arbitrary"),
        name="qkv_projection",
    )(h, w_pack, gain, nf, rf, cos_t, sin_a, sin_b, _ones_blockdiag(tn))


def _column_vectors(lw):
    f = lambda v: jnp.asarray(v, F32).reshape(-1)
    ones = lambda n: jnp.ones((n,), F32)
    zeros = lambda n: jnp.zeros((n,), F32)
    rep = lambda v, n: jnp.tile(f(v), n)
    groups = [
        (rep(lw["a_qn"], 8), 1.0, 1.0), (rep(lw["a_kn"], 2), 1.0, 1.0), (ones(128), 0.0, 0.0),
        (ones(256), 0.0, 1.0), (ones(64), 0.0, 1.0), (ones(64 + LANE), 0.0, 0.0),
        (rep(lw["b_qn"], 8), 1.0, 0.0), (rep(lw["b_kn"], 8), 1.0, 0.0), (ones(512), 0.0, 0.0),
        (rep(lw["c_qn"], 4), 1.0, 1.0), (rep(lw["c_kn"], 4), 1.0, 1.0), (ones(512), 0.0, 0.0)]
    gain = jnp.concatenate([g for g, _, _ in groups]).reshape(1, -1)
    nf = jnp.concatenate([ones(g.shape[0]) * a for g, a, _ in groups]).reshape(1, -1)
    rf = jnp.concatenate([ones(g.shape[0]) * b for g, _, b in groups]).reshape(1, -1)
    del zeros
    return gain, nf, rf


def _rope_tables(pos):
    half = ROPE_DIM // 2
    inv_freq = ROPE_THETA ** (-jnp.arange(0, ROPE_DIM, 2, dtype=F32) / ROPE_DIM)
    ang = pos.astype(F32)[:, None] * inv_freq[None, :]
    cos, sin = jnp.cos(ang), jnp.sin(ang)
    rows = pos.shape[0]
    one = jnp.ones((rows, HEAD_DIM - ROPE_DIM), F32)
    zero = jnp.zeros((rows, HEAD_DIM - ROPE_DIM), F32)
    z8 = jnp.zeros((rows, half), F32)
    cos_h = jnp.concatenate([cos, cos, one], axis=1)
    sa_h = jnp.concatenate([z8, sin, zero], axis=1)
    sb_h = jnp.concatenate([-sin, z8, zero], axis=1)
    dup = lambda t: jnp.concatenate([t, t], axis=1)
    return dup(cos_h), dup(sa_h), dup(sb_h)


def _to_key(score):
    bits = pltpu.bitcast(score + 0.0, jnp.int32)
    return jnp.where(bits < 0, bits ^ 0x7FFFFFFF, bits)


def _indexer_scores(iq, iw, ik):
    kk = ik[:, :HEAD_DIM]
    sc = None
    for hd in range(A_IDX_HEADS):
        logit = _dot_nt3(iq[:, hd * HEAD_DIM:(hd + 1) * HEAD_DIM], kk)
        wgt = iw[:, HEAD_DIM + hd:HEAD_DIM + hd + 1] * (A_IDX_HEADS ** -0.5 * HEAD_DIM ** -0.5)
        term = jnp.maximum(logit, 0.0) * wgt
        sc = term if sc is None else sc + term
    return sc


def _lane_fold(x, op):
    out = x[:, :LANE]
    for s in range(1, x.shape[1] // LANE):
        out = op(out, x[:, s * LANE:(s + 1) * LANE])
    return out


def _count(key_ref, n_blk, blk, pred):
    rows = key_ref.shape[0]

    def body(b, acc):
        start = pl.multiple_of(b * blk, blk)
        kb = key_ref[:, pl.ds(start, blk)]
        return acc + _lane_fold(jnp.where(pred(kb, start), 1.0, 0.0), jnp.add)

    acc = lax.fori_loop(0, n_blk, body, jnp.zeros((rows, LANE), F32))
    return jnp.sum(acc, axis=-1, keepdims=True)


def _topk_threshold(key_ref, n_blk, blk, topk):
    rows = key_ref.shape[0]
    kf = float(topk)
    c0 = _count(key_ref, n_blk, blk, lambda kb, st: kb >= 0)
    ans = jnp.where(c0 >= kf, 0, INT_MIN).astype(jnp.int32)

    def bit_step(it, ans):
        cand = ans + jnp.left_shift(jnp.int32(1), 30 - it)
        c = _count(key_ref, n_blk, blk, lambda kb, st: kb >= cand)
        return jnp.where(c >= kf, cand, ans)

    thr = lax.fori_loop(0, 31, bit_step, ans)
    n_gt = _count(key_ref, n_blk, blk, lambda kb, st: kb > thr)
    n_eq = _count(key_ref, n_blk, blk, lambda kb, st: kb == thr)
    need = kf - n_gt
    nbits = int(n_blk * blk).bit_length() if isinstance(n_blk, int) else 14
    cut_all = jnp.full((rows, 1), 1 << nbits, jnp.int32)
    tie_overflow = jnp.max(jnp.where((n_eq > need) & (thr != NEG_KEY), 1.0, 0.0)) > 0.5

    def search_cut():
        def cut_step(it, cut):
            cand = cut + jnp.left_shift(jnp.int32(1), nbits - 1 - it)

            def pred(kb, st):
                idx = st + lax.broadcasted_iota(jnp.int32, kb.shape, 1)
                return (kb == thr) & (idx < cand)

            c = _count(key_ref, n_blk, blk, pred)
            return jnp.where(c <= need, cand, cut)

        return lax.fori_loop(0, nbits, cut_step, jnp.zeros((rows, 1), jnp.int32))

    cut = lax.cond(tie_overflow, search_cut, lambda: cut_all)
    return thr, cut


def _selected(kb, first_idx, thr, cut):
    idx = first_idx + lax.broadcasted_iota(jnp.int32, kb.shape, 1)
    return ((kb > thr) | ((kb == thr) & (idx < cut))) & (kb > NEG_KEY)


def _pair_heads(o_even, o_odd, group_half):
    lane = lax.broadcasted_iota(jnp.int32, o_even.shape, 1)
    if group_half == 0:
        return jnp.where(lane < HEAD_DIM, o_even, pltpu.roll(o_odd, HEAD_DIM, 1))
    return jnp.where(lane < HEAD_DIM, pltpu.roll(o_even, HEAD_DIM, 1), o_odd)


def _dsa_query_heads(q_ref, qh_ref):
    for hd in range(A_HEADS):
        slot, half = hd // 2, hd % 2
        group = hd // (A_HEADS // A_KV_HEADS)
        qs = q_ref[:, slot * LANE:(slot + 1) * LANE] * (HEAD_DIM ** -0.5)
        if half != group:
            qs = pltpu.roll(qs, HEAD_DIM, 1)
        qh_ref[hd] = jnp.where(_half_mask(qs.shape, group), qs, 0.0).astype(BF16)


def _softmax_block(sel, kblk, vblk, q_heads, m_ref, l_ref, acc_ref, s_ref, p_ref):
    n = len(q_heads)
    reps = kblk.shape[0] // LANE
    for hd in range(n):
        s_ref[hd] = _dot_nt(q_heads[hd], kblk)
    alphas = []
    for hd in range(n):
        s = jnp.where(sel, s_ref[hd], -jnp.inf)
        s_ref[hd] = s
        blk_max = jnp.max(_lane_fold(s, jnp.maximum), axis=-1, keepdims=True)
        m_prev = m_ref[hd]
        m_new = jnp.maximum(m_prev, blk_max)
        alphas.append(jnp.exp(m_prev - m_new))
        m_ref[hd] = m_new
    for hd in range(n):
        p = jnp.exp(s_ref[hd] - jnp.tile(m_ref[hd], (1, reps)))
        l_ref[hd] = alphas[hd] * l_ref[hd] + _lane_fold(p, jnp.add)
        p_ref[hd] = p.astype(BF16)
    for hd in range(n):
        acc_ref[hd] = alphas[hd] * acc_ref[hd] + jnp.dot(p_ref[hd], vblk, preferred_element_type=F32)


def _softmax_init(m_ref, l_ref, acc_ref):
    m_ref[...] = jnp.full(m_ref.shape, M_FLOOR, F32)
    l_ref[...] = jnp.zeros(l_ref.shape, F32)
    acc_ref[...] = jnp.zeros(acc_ref.shape, F32)


def _indexer_query3(iq_ref, iq3_ref):
    for hd in range(A_IDX_HEADS):
        slot, half = hd // 2, hd % 2
        x = iq_ref[:, slot * LANE:(slot + 1) * LANE]
        xl = jnp.where(_half_mask(x.shape, half), x, pltpu.roll(x, HEAD_DIM, 1))
        hi = xl.astype(BF16).astype(F32)
        lower = _half_mask(x.shape, 0)
        iq3_ref[hd, :, :LANE] = jnp.where(lower, hi, xl - hi).astype(BF16)
        iq3_ref[hd, :, LANE:] = jnp.where(lower, hi, 0.0).astype(BF16)


def _ik3_kernel(x_ref, o_ref):
    x = x_ref[...]
    lower = _half_mask(x.shape, 0)
    xl = jnp.where(lower, x, pltpu.roll(x, HEAD_DIM, 1))
    hi = xl.astype(BF16).astype(F32)
    o_ref[:, :LANE] = hi.astype(BF16)
    o_ref[:, LANE:] = jnp.where(lower, xl - hi, 0.0).astype(BF16)


def _indexer_keys3(z, t):
    tm = _tile(t, 1024)
    return pl.pallas_call(
        _ik3_kernel,
        out_shape=jax.ShapeDtypeStruct((t, 2 * LANE), BF16),
        grid=(t // tm,),
        in_specs=[pl.BlockSpec((tm, LANE), lambda i: (i, S_AIK))],
        out_specs=pl.BlockSpec((tm, 2 * LANE), lambda i: (i, 0)),
        compiler_params=_cparams("parallel"),
        name="indexer_keys",
    )(z)


def _dsa_prompt_kernel(q_ref, iq_ref, iw_ref, k_ref, v_ref, ik3_ref, o_ref,
                       key_ref, qh_ref, iq3_ref, m_ref, l_ref, acc_ref, s_ref, p_ref, *, tq, kb, topk):
    i = pl.program_id(0)
    q0 = i * tq
    n_blk = (q0 + tq + kb - 1) // kb
    row = lax.broadcasted_iota(jnp.int32, (tq, kb), 0)
    limit = (((q0 + row) >> 6) + 1) << 6
    _indexer_query3(iq_ref, iq3_ref)
    iw = iw_ref[...]
    wgt = [iw[:, HEAD_DIM + hd:HEAD_DIM + hd + 1] * (A_IDX_HEADS ** -0.5 * HEAD_DIM ** -0.5)
           for hd in range(A_IDX_HEADS)]

    def score_blk(b, carry):
        start = pl.multiple_of(b * kb, kb)
        ikb = ik3_ref[pl.ds(start, kb), :]
        sc = None
        for hd in range(A_IDX_HEADS):
            term = jnp.maximum(_dot_nt(iq3_ref[hd], ikb), 0.0) * wgt[hd]
            sc = term if sc is None else sc + term
        kpos = start + lax.broadcasted_iota(jnp.int32, (tq, kb), 1)
        key_ref[:, pl.ds(start, kb)] = jnp.where(kpos < limit, _to_key(sc), NEG_KEY)
        return carry

    lax.fori_loop(0, n_blk, score_blk, 0)
    thr, cut = _topk_threshold(key_ref, n_blk, kb, topk)
    _dsa_query_heads(q_ref, qh_ref)
    _softmax_init(m_ref, l_ref, acc_ref)

    def attn_blk(b, carry):
        start = pl.multiple_of(b * kb, kb)
        sel = _selected(key_ref[:, pl.ds(start, kb)], start, thr, cut)
        _softmax_block(sel, k_ref[pl.ds(start, kb), :], v_ref[pl.ds(start, kb), :],
                       [qh_ref[hd] for hd in range(A_HEADS)], m_ref, l_ref, acc_ref, s_ref, p_ref)
        return carry

    lax.fori_loop(0, n_blk, attn_blk, 0)
    for j in range(A_HEADS // 2):
        outs = [acc_ref[2 * j + e] / jnp.sum(l_ref[2 * j + e], axis=-1, keepdims=True) for e in range(2)]
        o_ref[:, j * LANE:(j + 1) * LANE] = _pair_heads(outs[0], outs[1], (2 * j) // (A_HEADS // A_KV_HEADS)).astype(o_ref.dtype)


def _dsa_prompt(z, zb, t):
    tq, kb = 128, 512
    topk = min(A_TOPK_MAX, t // 4)
    full = lambda s: pl.BlockSpec((t, LANE), lambda i: (0, s))
    heads = lambda dt, w=LANE: pltpu.VMEM((A_HEADS, tq, w), dt)
    return pl.pallas_call(
        functools.partial(_dsa_prompt_kernel, tq=tq, kb=kb, topk=topk),
        out_shape=jax.ShapeDtypeStruct((t, BRANCH_WIDTH), BF16),
        grid=(t // tq,),
        in_specs=[pl.BlockSpec((tq, 4 * LANE), lambda i: (i, S_AQ // 4)),
                  pl.BlockSpec((tq, 2 * LANE), lambda i: (i, S_AIQ // 2)),
                  pl.BlockSpec((tq, LANE), lambda i: (i, S_AIK)),
                  full(S_AK), full(S_AV), pl.BlockSpec((t, 2 * LANE), lambda i: (0, 0))],
        out_specs=pl.BlockSpec((tq, BRANCH_WIDTH), lambda i: (i, 0)),
        scratch_shapes=[pltpu.VMEM((tq, t), jnp.int32), heads(BF16),
                        pltpu.VMEM((A_IDX_HEADS, tq, 2 * LANE), BF16),
                        heads(F32), heads(F32), heads(F32), heads(F32, kb), heads(BF16, kb)],
        compiler_params=_cparams("arbitrary"),
        name="dsa_prompt",
    )(z, z, z, zb, zb, _indexer_keys3(z, t))


def _dsa_sample_kernel(q_ref, iq_ref, new_ik_ref, new_k_ref, new_v_ref, ck_ref, cv_ref, cik_ref, o_ref,
                       key_ref, qh_ref, *, ts, past, topk, q_pos0):
    width = key_ref.shape[1]
    iq, iw = iq_ref[...], new_ik_ref[...]
    row_c = lax.broadcasted_iota(jnp.int32, (ts, past), 0)
    limit_c = (((q_pos0 + row_c) >> 6) + 1) << 6
    kpos_c = lax.broadcasted_iota(jnp.int32, (ts, past), 1)
    sc_c = _indexer_scores(iq, iw, cik_ref[...])
    key_ref[:, :past] = jnp.where(kpos_c < limit_c, _to_key(sc_c), NEG_KEY)
    key_ref[:, past:] = jnp.full((ts, width - past), NEG_KEY, jnp.int32)
    sc_n = _indexer_scores(iq, iw, new_ik_ref[...])
    kpos_n = past + lax.broadcasted_iota(jnp.int32, (ts, ts), 1)
    limit_n = (((q_pos0 + lax.broadcasted_iota(jnp.int32, (ts, ts), 0)) >> 6) + 1) << 6
    key_ref[:, past:past + ts] = jnp.where(kpos_n < limit_n, _to_key(sc_n), NEG_KEY)

    thr, cut = _topk_threshold(key_ref, width // LANE, LANE, topk)
    sel_c = _selected(key_ref[:, :past], 0, thr, cut)
    sel_n = _selected(key_ref[:, past:past + ts], past, thr, cut)
    _dsa_query_heads(q_ref, qh_ref)
    kc, vc = ck_ref[...].astype(BF16), cv_ref[...].astype(BF16)
    kn, vn = new_k_ref[...].astype(BF16), new_v_ref[...].astype(BF16)
    outs = []
    for hd in range(A_HEADS):
        s_c = jnp.where(sel_c, _dot_nt(qh_ref[hd], kc), -jnp.inf)
        s_n = jnp.where(sel_n, _dot_nt(qh_ref[hd], kn), -jnp.inf)
        m = jnp.maximum(jnp.max(s_c, axis=-1, keepdims=True), jnp.max(s_n, axis=-1, keepdims=True))
        p_c, p_n = jnp.exp(s_c - m), jnp.exp(s_n - m)
        l = jnp.sum(p_c, axis=-1, keepdims=True) + jnp.sum(p_n, axis=-1, keepdims=True)
        o = (jnp.dot(p_c.astype(BF16), vc, preferred_element_type=F32)
             + jnp.dot(p_n.astype(BF16), vn, preferred_element_type=F32))
        outs.append(o / l)
    for j in range(A_HEADS // 2):
        o_ref[:, j * LANE:(j + 1) * LANE] = _pair_heads(
            outs[2 * j], outs[2 * j + 1], (2 * j) // (A_HEADS // A_KV_HEADS)).astype(o_ref.dtype)


def _dsa_sample(z, cache_k, cache_v, cache_ik, l, t, nb, ts):
    past = cache_k.shape[2]
    topk = min(A_TOPK_MAX, (past + ts) // 4)
    width = ((past + ts + LANE - 1) // LANE) * LANE
    rb = t // ts
    new = lambda s: pl.BlockSpec((ts, LANE), lambda b: (rb + b, s))
    cache = lambda w: pl.BlockSpec((None, None, past, w), lambda b: (l, b, 0, 0))
    depth = cache_k.shape[0]
    return pl.pallas_call(
        functools.partial(_dsa_sample_kernel, ts=ts, past=past, topk=topk, q_pos0=past),
        out_shape=jax.ShapeDtypeStruct((nb * ts, BRANCH_WIDTH), BF16),
        grid=(nb,),
        in_specs=[pl.BlockSpec((ts, 4 * LANE), lambda b: (rb + b, S_AQ // 4)),
                  pl.BlockSpec((ts, 2 * LANE), lambda b: (rb + b, S_AIQ // 2)),
                  new(S_AIK), new(S_AK), new(S_AV), cache(LANE), cache(LANE), cache(HEAD_DIM)],
        out_specs=pl.BlockSpec((ts, BRANCH_WIDTH), lambda b: (b, 0)),
        scratch_shapes=[pltpu.VMEM((ts, width), jnp.int32), pltpu.VMEM((A_HEADS, ts, LANE), BF16)],
        compiler_params=_cparams("arbitrary"),
        name="dsa_sample",
    )(z, z, z, z, z, cache_k.reshape(depth, nb, past, LANE), cache_v.reshape(depth, nb, past, LANE), cache_ik)


def _band_prompt_kernel(q_ref, k_ref, v_ref, bias_ref, o_ref, *, tq, win):
    i = pl.program_id(1)
    w0 = pl.multiple_of(jnp.maximum(i * tq - B_WINDOW, 0), tq)
    kw = k_ref[pl.ds(w0, win), :]
    vw = v_ref[pl.ds(w0, win), :]
    kc = (w0 + lax.broadcasted_iota(jnp.int32, (tq, win), 1)) >> 6
    qc = (i * tq + lax.broadcasted_iota(jnp.int32, (tq, win), 0)) >> 6
    ok = (kc <= qc) & (kc >= qc - B_PAST_CHUNKS)
    qs = q_ref[...] * (HEAD_DIM ** -0.5)
    outs = []
    for e in range(2):
        qe = jnp.where(_half_mask(qs.shape, e), qs, 0.0).astype(BF16)
        s = jnp.where(ok, _dot_nt(qe, kw) + bias_ref[e], -jnp.inf)
        p = jnp.exp(s - jnp.max(s, axis=-1, keepdims=True))
        l = jnp.sum(p, axis=-1, keepdims=True)
        outs.append(jnp.dot(p.astype(BF16), vw, preferred_element_type=F32) / l)
    o_ref[...] = jnp.where(_half_mask(outs[0].shape, 0), outs[0], outs[1]).astype(o_ref.dtype)


def _band_prompt(z, zb, bias, t, tq):
    win = B_WINDOW + tq
    n_case = B_WINDOW // tq
    return pl.pallas_call(
        functools.partial(_band_prompt_kernel, tq=tq, win=win),
        out_shape=jax.ShapeDtypeStruct((t, BRANCH_WIDTH), BF16),
        grid=(B_HEADS // 2, t // tq),
        in_specs=[pl.BlockSpec((tq, LANE), lambda j, i: (i, S_BQ + j)),
                  pl.BlockSpec((t, LANE), lambda j, i: (0, S_BK + j)),
                  pl.BlockSpec((t, LANE), lambda j, i: (0, S_BV + j)),
                  pl.BlockSpec((None, 2, tq, win), lambda j, i: (jnp.minimum(i, n_case), j, 0, 0))],
        out_specs=pl.BlockSpec((tq, LANE), lambda j, i: (i, j)),
        compiler_params=_cparams("parallel", "arbitrary"),
        name="band_prompt",
    )(z, zb, zb, bias)


def _band_sample_kernel(q_ref, kn_ref, vn_ref, kc_ref, vc_ref, bc_ref, bn_ref, o_ref):
    kc, vc = kc_ref[...].astype(BF16), vc_ref[...].astype(BF16)
    kn, vn = kn_ref[...].astype(BF16), vn_ref[...].astype(BF16)
    qs = q_ref[...] * (HEAD_DIM ** -0.5)
    outs = []
    for e in range(2):
        qe = jnp.where(_half_mask(qs.shape, e), qs, 0.0).astype(BF16)
        s_c = _dot_nt(qe, kc) + bc_ref[e]
        s_n = _dot_nt(qe, kn) + bn_ref[e]
        m = jnp.maximum(jnp.max(s_c, axis=-1, keepdims=True), jnp.max(s_n, axis=-1, keepdims=True))
        p_c, p_n = jnp.exp(s_c - m), jnp.exp(s_n - m)
        l = jnp.sum(p_c, axis=-1, keepdims=True) + jnp.sum(p_n, axis=-1, keepdims=True)
        o = (jnp.dot(p_c.astype(BF16), vc, preferred_element_type=F32)
             + jnp.dot(p_n.astype(BF16), vn, preferred_element_type=F32))
        outs.append(o / l)
    o_ref[...] = jnp.where(_half_mask(outs[0].shape, 0), outs[0], outs[1]).astype(o_ref.dtype)


def _band_sample(z, cache_k, cache_v, bias_c, bias_n, l, t, nb, ts):
    depth, _, pb = cache_k.shape[:3]
    rb = t // ts
    cache = pl.BlockSpec((None, None, pb, LANE), lambda b, j: (l, b, 0, j))
    return pl.pallas_call(
        _band_sample_kernel,
        out_shape=jax.ShapeDtypeStruct((nb * ts, BRANCH_WIDTH), BF16),
        grid=(nb, B_HEADS // 2),
        in_specs=[pl.BlockSpec((ts, LANE), lambda b, j: (rb + b, S_BQ + j)),
                  pl.BlockSpec((ts, LANE), lambda b, j: (rb + b, S_BK + j)),
                  pl.BlockSpec((ts, LANE), lambda b, j: (rb + b, S_BV + j)),
                  cache, cache,
                  pl.BlockSpec((2, ts, pb), lambda b, j: (j, 0, 0)),
                  pl.BlockSpec((2, ts, ts), lambda b, j: (j, 0, 0))],
        out_specs=pl.BlockSpec((ts, LANE), lambda b, j: (b, j)),
        compiler_params=_cparams("parallel", "arbitrary"),
        name="band_sample",
    )(z, z, z, cache_k.reshape(depth, nb, pb, BRANCH_WIDTH), cache_v.reshape(depth, nb, pb, BRANCH_WIDTH),
      bias_c, bias_n)


def _band_bias(table, tq):
    n_case = B_WINDOW // tq
    win = B_WINDOW + tq
    width = win + B_WINDOW
    j = np.arange(width + tq - 1) - (tq - 1) - B_WINDOW
    ext = table.astype(F32)[:, np.clip(j, -B_REL_CLIP, B_REL_CLIP) + B_REL_CLIP]
    toe = jnp.stack([ext[:, tq - 1 - r:tq - 1 - r + width] for r in range(tq)], axis=1)
    return jnp.stack([toe[:, :, B_WINDOW - c * tq:B_WINDOW - c * tq + win] for c in range(n_case + 1)], axis=0)


def _lambda(lam_ref, lam_init):
    lv = lam_ref[...]
    return (jnp.exp(jnp.sum(lv[0:1] * lv[1:2], axis=-1, keepdims=True))
            - jnp.exp(jnp.sum(lv[2:3] * lv[3:4], axis=-1, keepdims=True)) + lam_init)


def _diff_finish(o0, o1, lam, on_ref, lam_init):
    attn = o0 - lam * o1
    ms = jnp.mean(attn * attn, axis=-1, keepdims=True)
    return (attn * lax.rsqrt(ms + NORM_EPS) * on_ref[...]) * (1.0 - lam_init)


def _diff_prompt_kernel(q_ref, k_ref, v_ref, lam_ref, on_ref, o_ref, qh_ref, m_ref, l_ref, acc_ref, s_ref, p_ref,
                        *, tq, kb, lam_init):
    i = pl.program_id(1)
    q0 = i * tq
    n_blk = (q0 + tq + kb - 1) // kb
    row = lax.broadcasted_iota(jnp.int32, (tq, kb), 0)
    limit = (((q0 + row) >> 6) + 1) << 6
    qs = q_ref[...] * (HEAD_DIM ** -0.5)
    for c in range(2):
        qh_ref[c] = jnp.where(_half_mask(qs.shape, c), qs, 0.0).astype(BF16)
    _softmax_init(m_ref, l_ref, acc_ref)

    def attn_blk(b, carry):
        start = pl.multiple_of(b * kb, kb)
        ok = start + lax.broadcasted_iota(jnp.int32, (tq, kb), 1) < limit
        _softmax_block(ok, k_ref[pl.ds(start, kb), :], v_ref[pl.ds(start, kb), :],
                       [qh_ref[c] for c in range(2)], m_ref, l_ref, acc_ref, s_ref, p_ref)
        return carry

    lax.fori_loop(0, n_blk, attn_blk, 0)
    lam = _lambda(lam_ref, lam_init)
    outs = [acc_ref[c] / jnp.sum(l_ref[c], axis=-1, keepdims=True) for c in range(2)]
    o_ref[...] = _diff_finish(outs[0], outs[1], lam, on_ref, lam_init).astype(o_ref.dtype)


def _diff_prompt(z, zb, c_lam_all, c_on_all, l, t, lam_init):
    tq, kb = _tile(t, 256), 512
    maps = lambda dt, w=LANE: pltpu.VMEM((2, tq, w), dt)
    return pl.pallas_call(
        functools.partial(_diff_prompt_kernel, tq=tq, kb=kb, lam_init=lam_init),
        out_shape=jax.ShapeDtypeStruct((t, BRANCH_WIDTH), BF16),
        grid=(C_HEADS, t // tq),
        in_specs=[pl.BlockSpec((tq, LANE), lambda h, i: (i, S_CQ + h)),
                  pl.BlockSpec((t, LANE), lambda h, i: (0, S_CK + h)),
                  pl.BlockSpec((t, LANE), lambda h, i: (0, S_CV + h)),
                  pl.BlockSpec((None, 4, HEAD_DIM), lambda h, i: (l, 0, 0)),
                  _layer_vec(l, LANE)],
        out_specs=pl.BlockSpec((tq, LANE), lambda h, i: (i, h)),
        scratch_shapes=[maps(BF16), maps(F32), maps(F32), maps(F32), maps(F32, kb), maps(BF16, kb)],
        compiler_params=_cparams("parallel", "arbitrary"),
        name="diff_prompt",
    )(z, zb, zb, c_lam_all, _vec3(c_on_all))


def _diff_sample_kernel(q_ref, kn_ref, vn_ref, kc_ref, vc_ref, lam_ref, on_ref, o_ref, *, lam_init):
    kc, vc = kc_ref[...].astype(BF16), vc_ref[...].astype(BF16)
    kn, vn = kn_ref[...].astype(BF16), vn_ref[...].astype(BF16)
    qs = q_ref[...] * (HEAD_DIM ** -0.5)
    outs = []
    for c in range(2):
        qc = jnp.where(_half_mask(qs.shape, c), qs, 0.0).astype(BF16)
        s_c = _dot_nt(qc, kc)
        s_n = _dot_nt(qc, kn)
        m = jnp.maximum(jnp.max(s_c, axis=-1, keepdims=True), jnp.max(s_n, axis=-1, keepdims=True))
        p_c, p_n = jnp.exp(s_c - m), jnp.exp(s_n - m)
        l = jnp.sum(p_c, axis=-1, keepdims=True) + jnp.sum(p_n, axis=-1, keepdims=True)
        o = (jnp.dot(p_c.astype(BF16), vc, preferred_element_type=F32)
             + jnp.dot(p_n.astype(BF16), vn, preferred_element_type=F32))
        outs.append(o / l)
    lam = _lambda(lam_ref, lam_init)
    o_ref[...] = _diff_finish(outs[0], outs[1], lam, on_ref, lam_init).astype(o_ref.dtype)


def _diff_sample(z, cache_k, cache_v, c_lam_all, c_on_all, l, t, nb, ts, lam_init):
    depth, _, past = cache_k.shape[:3]
    rb = t // ts
    cache = pl.BlockSpec((None, None, past, LANE), lambda b, h: (l, b, 0, h))
    return pl.pallas_call(
        functools.partial(_diff_sample_kernel, lam_init=lam_init),
        out_shape=jax.ShapeDtypeStruct((nb * ts, BRANCH_WIDTH), BF16),
        grid=(nb, C_HEADS),
        in_specs=[pl.BlockSpec((ts, LANE), lambda b, h: (rb + b, S_CQ + h)),
                  pl.BlockSpec((ts, LANE), lambda b, h: (rb + b, S_CK + h)),
                  pl.BlockSpec((ts, LANE), lambda b, h: (rb + b, S_CV + h)),
                  cache, cache,
                  pl.BlockSpec((None, 4, HEAD_DIM), lambda b, h: (l, 0, 0)),
                  _layer_vec(l, LANE)],
        out_specs=pl.BlockSpec((ts, LANE), lambda b, h: (b, h)),
        compiler_params=_cparams("parallel", "arbitrary"),
        name="diff_sample",
    )(z, z, z, cache_k.reshape(depth, nb, past, BRANCH_WIDTH), cache_v.reshape(depth, nb, past, BRANCH_WIDTH),
      c_lam_all, _vec3(c_on_all))


def _rwkv_pre_kernel(zd_ref, prev_ref, first_ref, mu_ref, w0_ref, w2_ref, a0_ref, a2_ref, g2_ref, kkw_ref,
                     ka_ref, bd_ref, r_ref, w_ref, k_ref, v_ref, kk_ref, b_ref, g_ref, *, tm, t, ts):
    zf = zd_ref[...]
    grow = pl.program_id(0) * tm + lax.broadcasted_iota(jnp.int32, (tm, 1), 0)
    zs = jnp.where(grow % tm == 0, prev_ref[7:8, :], pltpu.roll(zf, 1, 0))
    seq_start = (grow >= t) & ((grow - t) % ts == 0)
    zs = jnp.where(seq_start, first_ref[...], zs)
    zs = jnp.where(grow == 0, 0.0, zs)
    zm = zf + (zs - zf) * mu_ref[...]
    bw = BRANCH_WIDTH
    r, k, v = zm[:, :bw], zm[:, bw:2 * bw], zm[:, 2 * bw:3 * bw]
    wl, al, gl = zm[:, 3 * bw:3 * bw + 64], zm[:, 3 * bw + 64:3 * bw + 128], zm[:, 3 * bw + 128:]
    dot = lambda a, b: jnp.dot(a.astype(BF16), b.astype(BF16), preferred_element_type=F32)
    u = -(w0_ref[...] + dot(jnp.tanh(wl), w2_ref[...]))
    softplus = jnp.maximum(u, 0.0) + jnp.log(1.0 + jnp.exp(-jnp.abs(u)))
    w = -softplus - 0.5
    a = jax.nn.sigmoid(a0_ref[...] + dot(al, a2_ref[...]))
    kk = k * kkw_ref[...]
    nrm = jnp.sqrt(_seg_sum(kk * kk, bd_ref[...]))
    kk = kk / jnp.maximum(nrm, 1e-12)
    r_ref[...] = r
    w_ref[...] = -jnp.exp(w)
    k_ref[...] = k * (1.0 + (a - 1.0) * ka_ref[...])
    v_ref[...] = v
    kk_ref[...] = kk
    b_ref[...] = kk * a
    g_ref[...] = dot(jax.nn.sigmoid(gl), g2_ref[...])


def _rwkv_pre(zd, zfirst, p, l, t, ts):
    m, dc = zd.shape
    tm = _tile(math.gcd(t, m - t), 512)
    bw = BRANCH_WIDTH
    npt = t // tm
    rows = pl.BlockSpec((tm, dc), lambda i: (i, 0))
    vec = lambda n: _layer_vec(l, n)
    mat = lambda k: pl.BlockSpec((None, k, bw), lambda i: (l, 0, 0))
    out = jax.ShapeDtypeStruct((m, bw), F32)
    return pl.pallas_call(
        functools.partial(_rwkv_pre_kernel, tm=tm, t=t, ts=ts),
        out_shape=[out] * 7,
        grid=(m // tm,),
        in_specs=[rows,
                  pl.BlockSpec((8, dc), lambda i: (jnp.maximum(i * (tm // 8) - 1, 0), 0)),
                  pl.BlockSpec((tm, dc), lambda i: (jnp.maximum(i - npt, 0), 0)),
                  vec(dc), vec(bw), mat(64), vec(bw), mat(64), mat(128), vec(bw), vec(bw),
                  pl.BlockSpec((bw, bw), lambda i: (0, 0))],
        out_specs=[pl.BlockSpec((tm, bw), lambda i: (i, 0))] * 7,
        compiler_params=_cparams("parallel"),
        name="rwkv_pre",
    )(zd, zd, zfirst, _vec3(p["d_mu"]), _vec3(p["d_w0"]), p["d_w2"], _vec3(p["d_a0"]), p["d_a2"], p["d_g2"],
      _vec3(p["d_k_k"]), _vec3(p["d_k_a"]), _ones_blockdiag(bw))


RW_CHUNK = 16
RW_BLOCK = 128
RW_SLOTS = BRANCH_WIDTH // LANE


def _dot3(a, b):
    ah, al = _split2(a)
    bh, bl = _split2(b)
    d = lambda x, y: jnp.dot(x, y, preferred_element_type=F32)
    return d(ah, bh) + d(al, bh) + d(ah, bl)


def _rwkv_chunk_kernel(*refs, seq_chunks):
    if seq_chunks:
        (r_ref, lw_ref, k_ref, v_ref, kk_ref, b_ref, s0_ref, y_ref, sf_ref,
         h_ref, u_ref, ab_ref, rb_ref, u0_ref, y0_ref, bt_ref, kt_ref, eg_ref) = refs
    else:
        (r_ref, lw_ref, k_ref, v_ref, kk_ref, b_ref, y_ref, sf_ref,
         h_ref, u_ref, ab_ref, rb_ref, u0_ref, y0_ref, bt_ref, kt_ref, eg_ref) = refs
        s0_ref = None

        @pl.when(pl.program_id(0) == 0)
        def _():
            h_ref[...] = jnp.zeros_like(h_ref)

    n = RW_BLOCK
    n_chunks = n // RW_CHUNK
    row = lax.broadcasted_iota(jnp.int32, (n, n), 0)
    col = lax.broadcasted_iota(jnp.int32, (n, n), 1)
    same = (row // RW_CHUNK) == (col // RW_CHUNK)
    strict, incl = same & (col < row), same & (col <= row)
    eye = jnp.where(row == col, 1.0, 0.0)
    head_diag = (row // HEAD_DIM) == (col // HEAD_DIM)
    in_chunk = row % RW_CHUNK
    k_pick = jnp.where(lax.broadcasted_iota(jnp.int32, (n, HEAD_DIM), 0) % HEAD_DIM
                       == lax.broadcasted_iota(jnp.int32, (n, HEAD_DIM), 1), 1.0, 0.0)
    bf = lambda x: x.astype(BF16)
    mm = lambda x, y: jnp.dot(bf(x), bf(y), preferred_element_type=F32)
    u_ref[...] = jnp.zeros_like(u_ref)

    for j in range(RW_SLOTS):
        sl = slice(j * LANE, (j + 1) * LANE)
        lw, v = lw_ref[:, sl], v_ref[:, sl]
        g = lw
        for d in (1, 2, 4, 8):
            g = g + jnp.where(in_chunk >= d, pltpu.roll(g, d, 0), 0.0)
        inv_g = jnp.exp(-g)
        a_t = -kk_ref[:, sl] * jnp.exp(g - lw)
        b_t, k_t = b_ref[:, sl] * inv_g, k_ref[:, sl] * inv_g
        r_t = r_ref[:, sl] * jnp.exp(g)
        b16, k16, v16 = bf(b_t), bf(k_t), bf(v)
        both = lambda f: [f(e) for e in range(2)]
        halves = both(lambda e: _half_mask((n, LANE), e))
        a_e = both(lambda e: bf(jnp.where(halves[e], a_t, 0.0)))
        r_e = both(lambda e: bf(jnp.where(halves[e], r_t, 0.0)))
        n_ab = both(lambda e: jnp.where(strict, _dot_nt(a_e[e], b16), 0.0))
        n_ak = both(lambda e: jnp.where(strict, _dot_nt(a_e[e], k16), 0.0))
        m_rb = both(lambda e: bf(jnp.where(incl, _dot_nt(r_e[e], b16), 0.0)))
        m_rk = both(lambda e: bf(jnp.where(incl, _dot_nt(r_e[e], k16), 0.0)))
        w_e = both(lambda e: mm(n_ak[e], v16))
        n2 = both(lambda e: mm(n_ab[e], n_ab[e]))
        n4 = both(lambda e: mm(n2[e], n2[e]))
        n8 = both(lambda e: mm(n4[e], n4[e]))
        tinv = both(lambda e: eye + n_ab[e])
        for pw in (n2, n4, n8):
            tinv = both(lambda e: tinv[e] + mm(tinv[e], pw[e]))
        t16 = both(lambda e: bf(tinv[e]))
        a_bar = both(lambda e: mm(t16[e], a_t))
        u0 = both(lambda e: mm(t16[e], w_e[e]))
        r_bar = both(lambda e: r_t + mm(m_rb[e], a_bar[e]))
        y0 = both(lambda e: mm(m_rb[e], u0[e]) + mm(m_rk[e], v16))
        pick = lambda pair: jnp.where(halves[0], pair[0], pair[1])
        ab_ref[j], rb_ref[j], u0_ref[j], y0_ref[j] = pick(a_bar), pick(r_bar), pick(u0), pick(y0)
        bt_ref[j], kt_ref[j], eg_ref[j] = b_t.T, k_t.T, jnp.exp(g).T

    for c in range(n_chunks):
        rows = slice(c * RW_CHUNK, (c + 1) * RW_CHUNK)
        col_c = (col // RW_CHUNK) == c
        for j in range(RW_SLOTS):
            sl = slice(j * LANE, (j + 1) * LANE)
            if seq_chunks and c % seq_chunks == 0:
                x = s0_ref[c // seq_chunks, sl, :]
                h = jnp.where(head_diag, _dot_nt3(k_pick, x), 0.0)
            else:
                h = h_ref[j]
            res = _dot3(jnp.concatenate([ab_ref[j, rows, :], rb_ref[j, rows, :]], axis=0), h)
            u_c = res[:RW_CHUNK] + u0_ref[j, rows, :]
            y_ref[rows, sl] = res[RW_CHUNK:] + y0_ref[j, rows, :]
            u_ref[j, rows, :] = u_c
            bk = jnp.concatenate([jnp.where(col_c, bt_ref[j], 0.0), jnp.where(col_c, kt_ref[j], 0.0)], axis=1)
            uv = jnp.concatenate([u_ref[j], v_ref[:, sl]], axis=0)
            inc = jnp.where(head_diag, mm(bk, uv), 0.0)
            g_end = eg_ref[j, :, (c + 1) * RW_CHUNK - 1:(c + 1) * RW_CHUNK]
            h = g_end * (h + inc)
            h_ref[j] = h
            if seq_chunks and (c + 1) % seq_chunks == 0:
                ht = h.T
                sf_ref[c // seq_chunks, sl, :] = (ht + pltpu.roll(ht, HEAD_DIM, 1))[:, :HEAD_DIM]

    if not seq_chunks:
        @pl.when(pl.program_id(0) == pl.num_programs(0) - 1)
        def _():
            for j in range(RW_SLOTS):
                ht = h_ref[j].T
                sf_ref[0, j * LANE:(j + 1) * LANE, :] = (ht + pltpu.roll(ht, HEAD_DIM, 1))[:, :HEAD_DIM]


def _rwkv_chunked(ops, s0, row0, n_seq, t):
    n = RW_BLOCK
    bw = BRANCH_WIDTH
    assert row0 % n == 0 and (n_seq * t) % n == 0 and t % RW_CHUNK == 0
    rb = row0 // n
    rows = pl.BlockSpec((n, bw), lambda i: (rb + i, 0))
    if s0 is None:
        assert n_seq == 1
        seq_chunks, per_blk, extra, extra_specs = 0, 1, [], []
        sf_spec = pl.BlockSpec((1, bw, HEAD_DIM), lambda i: (0, 0, 0))
    else:
        assert n % t == 0
        seq_chunks, per_blk = t // RW_CHUNK, n // t
        extra = [s0.reshape(n_seq, bw, HEAD_DIM)]
        extra_specs = [pl.BlockSpec((per_blk, bw, HEAD_DIM), lambda i: (i, 0, 0))]
        sf_spec = pl.BlockSpec((per_blk, bw, HEAD_DIM), lambda i: (i, 0, 0))
    slot = lambda dt=F32: pltpu.VMEM((RW_SLOTS, n, LANE), dt)
    y, sf = pl.pallas_call(
        functools.partial(_rwkv_chunk_kernel, seq_chunks=seq_chunks),
        out_shape=[jax.ShapeDtypeStruct((n_seq * t, bw), F32), jax.ShapeDtypeStruct((n_seq, bw, HEAD_DIM), F32)],
        grid=(n_seq * t // n,),
        in_specs=[rows] * 6 + extra_specs,
        out_specs=[pl.BlockSpec((n, bw), lambda i: (i, 0)), sf_spec],
        scratch_shapes=[slot() for _ in range(9)],
        compiler_params=_cparams("arbitrary"),
        name="rwkv_chunked",
    )(*ops, *extra)
    return y, sf.reshape(n_seq, D_HEADS, HEAD_DIM, HEAD_DIM)


def _rwkv_post_kernel(y_ref, r_ref, k_ref, v_ref, g_ref, lnw_ref, lnb_ref, rk_ref, bd_ref, o_ref):
    bd = bd_ref[...]
    y = y_ref[...]
    mean = _seg_sum(y, bd) * (1.0 / HEAD_DIM)
    yc = y - mean
    var = _seg_sum(yc * yc, bd) * (1.0 / HEAD_DIM)
    yn = yc * lax.rsqrt(var + D_GN_EPS) * lnw_ref[...] + lnb_ref[...]
    bonus = _seg_sum(r_ref[...] * k_ref[...] * rk_ref[...], bd) * v_ref[...]
    o_ref[...] = ((yn + bonus) * g_ref[...]).astype(o_ref.dtype)


def _rwkv_post(y, r, k, v, g, p, l):
    m, bw = y.shape
    tm = _tile(m, 512)
    rows = pl.BlockSpec((tm, bw), lambda i: (i, 0))
    vec = _layer_vec(l, bw)
    return pl.pallas_call(
        _rwkv_post_kernel,
        out_shape=jax.ShapeDtypeStruct((m, bw), BF16),
        grid=(m // tm,),
        in_specs=[rows] * 5 + [vec] * 3 + [pl.BlockSpec((bw, bw), lambda i: (0, 0))],
        out_specs=rows,
        compiler_params=_cparams("parallel"),
        name="rwkv_post",
    )(y, r, k, v, g, _vec3(p["d_ln_w"]), _vec3(p["d_ln_b"]), _vec3(p["d_r_k"]), _ones_blockdiag(bw))


def _merge_kernel(o_ref, wbr_ref, h_ref, wg_ref, out_ref, acc_ref):
    n = pl.program_id(2)

    @pl.when(n == 0)
    def _():
        acc_ref[...] = jnp.zeros_like(acc_ref)

    u = jnp.dot(o_ref[...], wbr_ref[...].astype(BF16), preferred_element_type=F32)
    gate = jax.nn.sigmoid(jnp.dot(h_ref[...], wg_ref[...], preferred_element_type=F32))
    acc_ref[...] += u * gate

    @pl.when(n == N_BRANCH - 1)
    def _():
        out_ref[...] = acc_ref[...].astype(out_ref.dtype)


def _merge(o_all, w_br_all, h, w_pack, l):
    m, d = h.shape
    tm, tn = _tile(m, 1024), 512
    nj = d // tn
    g0 = S_GATE * LANE // tn
    return pl.pallas_call(
        _merge_kernel,
        out_shape=jax.ShapeDtypeStruct((m, d), BF16),
        grid=(m // tm, nj, N_BRANCH),
        in_specs=[pl.BlockSpec((tm, BRANCH_WIDTH), lambda i, j, n: (i, n)),
                  pl.BlockSpec((None, None, BRANCH_WIDTH, tn), lambda i, j, n: (l, n, 0, j)),
                  pl.BlockSpec((tm, d), lambda i, j, n: (i, 0)),
                  pl.BlockSpec((d, tn), lambda i, j, n: (0, g0 + n * nj + j))],
        out_specs=pl.BlockSpec((tm, tn), lambda i, j, n: (i, j)),
        scratch_shapes=[pltpu.VMEM((tm, tn), F32)],
        compiler_params=_cparams("parallel", "parallel", "arbitrary"),
        name="branch_merge",
    )(o_all, w_br_all, h, w_pack)


def _ple_kernel(x_ref, pe_ref, wple_ref, h_ref, wg_ref, o_ref):
    emb = jnp.dot(pe_ref[...].astype(BF16), wple_ref[...].astype(BF16), preferred_element_type=F32)
    gate = jax.nn.sigmoid(jnp.dot(h_ref[...], wg_ref[...].astype(BF16), preferred_element_type=F32))
    o_ref[...] = x_ref[...] + emb * gate


def _ple(x, pe, w_ple_all, h, w_gate_all, l):
    m, d = x.shape
    pd = pe.shape[1]
    tm, tn = _tile(m, 1024), 512
    return pl.pallas_call(
        _ple_kernel,
        out_shape=jax.ShapeDtypeStruct((m, d), F32),
        grid=(m // tm, d // tn),
        in_specs=[pl.BlockSpec((tm, tn), lambda i, j: (i, j)),
                  pl.BlockSpec((tm, pd), lambda i, j: (i, 0)),
                  pl.BlockSpec((None, pd, tn), lambda i, j: (l, 0, j)),
                  pl.BlockSpec((tm, d), lambda i, j: (i, 0)),
                  pl.BlockSpec((None, d, tn), lambda i, j: (l, 0, j))],
        out_specs=pl.BlockSpec((tm, tn), lambda i, j: (i, j)),
        compiler_params=_cparams("parallel", "parallel"),
        name="ple",
    )(x, pe, w_ple_all, h, w_gate_all)


def _layer(l, x, pe, p, t, nb, ts, tabs):
    m = x.shape[0]
    lam_init = 0.8 - 0.6 * math.exp(-0.3 * l)
    dc = p["d_mu"].shape[1]
    d = x.shape[1]
    lw = {"a_qn": p["a_q_norm"][l], "a_kn": p["a_k_norm"][l], "b_qn": p["b_q_norm"][l], "b_kn": p["b_k_norm"][l],
          "c_qn": p["c_q_norm"][l], "c_kn": p["c_k_norm"][l]}

    h = _rmsnorm(x, p["norm1_g"], l)
    w_pack = _pack_w_in(p["w_in"], l)
    z, zb = _qkv_projection(h, w_pack, *_column_vectors(lw), *tabs)
    zd = _matmul(h, w_pack, lambda acc: acc, F32, n=dc, col0=S_D * LANE, tn=256, name="rwkv_projection")

    oa = jnp.concatenate([_dsa_prompt(z, zb, t),
                          _dsa_sample(z, p["cache_a_k"], p["cache_a_v"], p["cache_a_kidx"], l, t, nb, ts)], axis=0)
    tq_b = 128
    bias = _band_bias(p["b_rel_bias"][l], tq_b)
    pb = p["cache_b_k"].shape[2]
    generic = bias[B_WINDOW // tq_b]
    ob = jnp.concatenate([
        _band_prompt(z, zb, bias, t, tq_b),
        _band_sample(z, p["cache_b_k"], p["cache_b_v"], generic[:, :ts, B_WINDOW - pb:B_WINDOW],
                     generic[:, :ts, B_WINDOW:B_WINDOW + ts], l, t, nb, ts)], axis=0)
    c_on = p["c_out_norm"]
    oc = jnp.concatenate([_diff_prompt(z, zb, p["c_lambda"], c_on, l, t, lam_init),
                          _diff_sample(z, p["cache_c_k"], p["cache_c_v"], p["c_lambda"], c_on, l, t, nb, ts, lam_init)],
                         axis=0)

    zfirst = jnp.broadcast_to(p["state_d_shift"][l], (nb, ts, dc)).reshape(nb * ts, dc)
    r, w, k, v, kk, b, g = _rwkv_pre(zd, zfirst, p, l, t, ts)
    ops = (r, w, k, v, kk, b)
    y_p, wkv_p = _rwkv_chunked(ops, None, 0, 1, t)
    y_s, wkv_s = _rwkv_chunked(ops, p["state_d_wkv"][l], t, nb, ts)
    od = _rwkv_post(jnp.concatenate([y_p, y_s], axis=0), r, k, v, g, p, l)

    ug = _merge(jnp.concatenate([oa, ob, oc, od], axis=1), p["w_branch"], h, w_pack, l)
    res = lambda acc, r_: r_ + acc
    x = _matmul(ug, p["w_out"], res, F32, n=d, layer=l, residual=x, name="out_proj")
    h2 = _rmsnorm(x, p["norm2_g"], l)
    ffn = p["w_up"].shape[2]
    up = _matmul(h2, p["w_up"], lambda acc: jnp.square(jnp.maximum(acc, 0.0)), BF16, n=ffn, layer=l, name="mlp_up")
    x = _matmul(up, p["w_down"], res, F32, n=d, layer=l, residual=x, name="mlp_down")
    h3 = _rmsnorm(x, p["norm3_g"], l)
    x = _ple(x, pe, p["w_ple"], h3, p["w_ple_gate"], l)

    slot = lambda rows, s, n=1: z[rows, s * LANE:(s + n) * LANE]

    def rows_of(sl, lead):
        a = lambda s, n, shape: slot(sl, s, n).reshape(lead + shape)
        ak = a(S_AK, 1, (A_KV_HEADS, HEAD_DIM))
        av = a(S_AV, 1, (A_KV_HEADS, HEAD_DIM))
        aik = slot(sl, S_AIK)[:, :HEAD_DIM].reshape(lead + (HEAD_DIM,))
        bk = a(S_BK, 4, (B_HEADS, HEAD_DIM))
        bv = a(S_BV, 4, (B_HEADS, HEAD_DIM))
        ck = a(S_CK, 4, (C_HEADS, 2, HEAD_DIM))
        cv = a(S_CV, 4, (C_HEADS, 2 * HEAD_DIM))
        return ak, av, aik, bk, bv, ck, cv

    keep = min(B_WINDOW, t)
    pak, pav, paik, pbk, pbv, pck, pcv = rows_of(slice(0, t), (1, t))
    new_p = (pak, pav, paik, pbk[:, t - keep:], pbv[:, t - keep:], pck, pcv, wkv_p, zd[t - 1:t].reshape(1, 1, dc))
    new_s = rows_of(slice(t, m), (nb, ts)) + (wkv_s, zd[t:].reshape(nb, ts, dc)[:, -1:])
    return x, new_p, new_s


def kernel(x_prompt, x_sample, cache_a_k, cache_a_v, cache_a_kidx, cache_b_k, cache_b_v, cache_c_k, cache_c_v, state_d_wkv, state_d_shift, p_prompt, p_sample, norm1_g, w_in, a_q_norm, a_k_norm, b_q_norm, b_k_norm, b_rel_bias, c_q_norm, c_k_norm, c_lambda, c_out_norm, d_mu, d_w0, d_w2, d_a0, d_a2, d_g2, d_k_k, d_k_a, d_r_k, d_ln_w, d_ln_b, w_branch, w_out, norm2_g, w_up, w_down, norm3_g, w_ple, w_ple_gate):
    batch, t, d = x_prompt.shape
    nb, ts, _ = x_sample.shape
    past = cache_a_k.shape[2]
    depth = w_in.shape[0]
    assert batch == 1 and t % 512 == 0 and past % CHUNK == 0 and ts <= CHUNK and (nb * ts) % 8 == 0
    p = dict(cache_a_k=cache_a_k, cache_a_v=cache_a_v, cache_a_kidx=cache_a_kidx, cache_b_k=cache_b_k,
             cache_b_v=cache_b_v, cache_c_k=cache_c_k, cache_c_v=cache_c_v, state_d_wkv=state_d_wkv,
             state_d_shift=state_d_shift, norm1_g=norm1_g, w_in=w_in, a_q_norm=a_q_norm, a_k_norm=a_k_norm,
             b_q_norm=b_q_norm, b_k_norm=b_k_norm, b_rel_bias=b_rel_bias, c_q_norm=c_q_norm, c_k_norm=c_k_norm,
             c_lambda=c_lambda, c_out_norm=c_out_norm, d_mu=d_mu, d_w0=d_w0, d_w2=d_w2, d_a0=d_a0, d_a2=d_a2,
             d_g2=d_g2, d_k_k=d_k_k, d_k_a=d_k_a, d_r_k=d_r_k, d_ln_w=d_ln_w, d_ln_b=d_ln_b, w_branch=w_branch,
             w_out=w_out, norm2_g=norm2_g, w_up=w_up, w_down=w_down, norm3_g=norm3_g, w_ple=w_ple,
             w_ple_gate=w_ple_gate)
    x = jnp.concatenate([x_prompt[0], x_sample.reshape(nb * ts, d)], axis=0)
    pos = jnp.concatenate([jnp.arange(t, dtype=jnp.int32),
                           jnp.tile(past + jnp.arange(ts, dtype=jnp.int32), nb)])
    tabs = _rope_tables(pos)
    st_p = [[] for _ in range(9)]
    st_s = [[] for _ in range(9)]
    for l in range(depth):
        pe = jnp.concatenate([p_prompt[l, 0], p_sample[l].reshape(nb * ts, -1)], axis=0)
        x, new_p, new_s = _layer(l, x, pe, p, t, nb, ts, tabs)
        for lst, arr in zip(st_p, new_p):
            lst.append(arr)
        for lst, arr in zip(st_s, new_s):
            lst.append(arr)
    outs_p = [jnp.stack(s, axis=0) for s in st_p]
    outs_s = [jnp.stack(s, axis=0) for s in st_s]
    return (x[:t].reshape(1, t, d), x[t:].reshape(nb, ts, d), *outs_p, *outs_s)
```

```python
import functools
import math

import numpy as np
import jax
import jax.numpy as jnp
from jax import lax
from jax.experimental import pallas as pl
from jax.experimental.pallas import tpu as pltpu

F32 = jnp.float32
BF16 = jnp.bfloat16

CHUNK = 64
HEAD_DIM = 64
ROPE_DIM = 16
ROPE_THETA = 500000.0
N_BRANCH = 4
BRANCH_WIDTH = 512
A_HEADS, A_KV_HEADS, A_IDX_HEADS = 8, 2, 4
A_TOPK_MAX = 256
B_HEADS = 8
B_PAST_CHUNKS = 8
B_WINDOW = B_PAST_CHUNKS * CHUNK
B_REL_CLIP = 128
C_HEADS = 4
D_HEADS = 8
D_GN_EPS = 64e-5
NORM_EPS = 1e-6

LANE = 128
VMEM_LIMIT = 48 * 1024 * 1024

S_AQ, S_AK, S_AV, S_AIQ, S_AIK = 0, 4, 5, 6, 8
S_BQ, S_BK, S_BV = 10, 14, 18
S_CQ, S_CK, S_CV = 22, 26, 30
S_D, S_GATE = 34, 48
N_QKV_SLOTS = 34
N_D_SLOTS = 14
A_COLS = 1092
A_SLOTS = 9
SHIFT = A_COLS - (A_SLOTS - 1) * LANE

NEG_KEY = -2139095041
INT_MIN = -2147483648
M_FLOOR = -1e30


def _cparams(*sem):
    return pltpu.CompilerParams(dimension_semantics=sem, vmem_limit_bytes=VMEM_LIMIT)


def _tile(n, pref):
    t = min(n, pref)
    while n % t:
        t -= 8
    return t


def _split2(x):
    hi = x.astype(BF16)
    lo = (x - hi.astype(F32)).astype(BF16)
    return hi, lo


def _seg_sum(x, ones_bd):
    hi, lo = _split2(x)
    return (jnp.dot(hi, ones_bd, preferred_element_type=F32)
            + jnp.dot(lo, ones_bd, preferred_element_type=F32))


def _dot_nt(a, b):
    return lax.dot_general(a, b, (((1,), (1,)), ((), ())), preferred_element_type=F32)


def _dot_nt3(a, b):
    ah, al = _split2(a)
    bh, bl = _split2(b)
    return _dot_nt(ah, bh) + _dot_nt(ah, bl) + _dot_nt(al, bh)


def _ones_blockdiag(n):
    i = np.arange(n)
    return jnp.asarray((i[:, None] // HEAD_DIM) == (i[None, :] // HEAD_DIM), dtype=BF16)


def _vec3(a):
    return a.reshape(a.shape[0], 1, -1)


def _layer_vec(l, n):
    return pl.BlockSpec((None, 1, n), lambda *_: (l, 0, 0))


def _half_mask(shape, half):
    lane = lax.broadcasted_iota(jnp.int32, shape, len(shape) - 1)
    return (lane < HEAD_DIM) if half == 0 else (lane >= HEAD_DIM)


def _rms_kernel(x_ref, g_ref, o_ref):
    x = x_ref[...]
    ms = jnp.mean(x * x, axis=-1, keepdims=True)
    o_ref[...] = (x * lax.rsqrt(ms + NORM_EPS) * g_ref[...]).astype(o_ref.dtype)


def _rmsnorm(x, g_all, l):
    m, d = x.shape
    tm = _tile(m, 512)
    return pl.pallas_call(
        _rms_kernel,
        out_shape=jax.ShapeDtypeStruct((m, d), BF16),
        grid=(m // tm,),
        in_specs=[pl.BlockSpec((tm, d), lambda i: (i, 0)), _layer_vec(l, d)],
        out_specs=pl.BlockSpec((tm, d), lambda i: (i, 0)),
        compiler_params=_cparams("parallel"),
        name="rmsnorm",
    )(x, _vec3(g_all))


def _pack_kernel(a_ref, b_ref, o_ref):
    j = pl.program_id(0)
    a, b = a_ref[...], b_ref[...]
    lane = lax.broadcasted_iota(jnp.int32, a.shape, 1)
    shifted = jnp.where(lane < LANE - SHIFT, pltpu.roll(a, LANE - SHIFT, 1), pltpu.roll(b, LANE - SHIFT, 1))
    out = jnp.where(j < A_SLOTS, a, jnp.where(j == A_SLOTS, 0.0, shifted))
    o_ref[...] = out.astype(o_ref.dtype)


def _pack_w_in(w_in_all, l):
    _, d, n_in = w_in_all.shape
    n_slots = A_SLOTS + 1 + (n_in - A_COLS) // LANE
    assert (n_in - A_COLS) % LANE == 0 and 0 < SHIFT < LANE
    src_a = lambda j: jnp.where(j < A_SLOTS, j, j - 2)
    src_b = lambda j: jnp.where(j < A_SLOTS, j, j - 1)
    return pl.pallas_call(
        _pack_kernel,
        out_shape=jax.ShapeDtypeStruct((d, n_slots * LANE), BF16),
        grid=(n_slots,),
        in_specs=[pl.BlockSpec((None, d, LANE), lambda j: (l, 0, src_a(j))),
                  pl.BlockSpec((None, d, LANE), lambda j: (l, 0, src_b(j)))],
        out_specs=pl.BlockSpec((d, LANE), lambda j: (0, j)),
        compiler_params=_cparams("parallel"),
        name="pack_w_in",
    )(w_in_all, w_in_all)


def _mm_kernel(*refs, n_extra, nk, epilogue):
    a_ref, w_ref = refs[0], refs[1]
    extra = refs[2:2 + n_extra]
    o_ref = refs[2 + n_extra]
    acc_ref = refs[3 + n_extra]
    k = pl.program_id(2)

    @pl.when(k == 0)
    def _():
        acc_ref[...] = jnp.zeros_like(acc_ref)

    acc_ref[...] += jnp.dot(a_ref[...], w_ref[...].astype(BF16), preferred_element_type=F32)

    @pl.when(k == nk - 1)
    def _():
        o_ref[...] = epilogue(acc_ref[...], *[e[...] for e in extra]).astype(o_ref.dtype)


def _matmul(a, w, epilogue, out_dtype, *, n, layer=None, col0=0, residual=None,
            tm=1024, tn=512, tk=2048, name="matmul"):
    m, kdim = a.shape
    tm, tn, tk = _tile(m, tm), _tile(n, tn), _tile(kdim, tk)
    nk = kdim // tk
    cb = col0 // tn
    assert col0 % tn == 0
    if layer is None:
        w_spec = pl.BlockSpec((tk, tn), lambda i, j, k: (k, cb + j))
    else:
        w_spec = pl.BlockSpec((None, tk, tn), lambda i, j, k: (layer, k, cb + j))
    specs = [pl.BlockSpec((tm, tk), lambda i, j, k: (i, k)), w_spec]
    extras = []
    if residual is not None:
        specs.append(pl.BlockSpec((tm, tn), lambda i, j, k: (i, j)))
        extras.append(residual)
    return pl.pallas_call(
        functools.partial(_mm_kernel, n_extra=len(extras), nk=nk, epilogue=epilogue),
        out_shape=jax.ShapeDtypeStruct((m, n), out_dtype),
        grid=(m // tm, n // tn, nk),
        in_specs=specs,
        out_specs=pl.BlockSpec((tm, tn), lambda i, j, k: (i, j)),
        scratch_shapes=[pltpu.VMEM((tm, tn), F32)],
        compiler_params=_cparams("parallel", "parallel", "arbitrary"),
        name=name,
    )(a, w, *extras)


def _cast_kernel(w_ref, o_ref):
    o_ref[...] = w_ref[...].astype(o_ref.dtype)


def _cast_bf16(w_all, l):
    _, kdim, n = w_all.shape
    tk = _tile(kdim, 512)
    return pl.pallas_call(
        _cast_kernel,
        out_shape=jax.ShapeDtypeStruct((kdim, n), BF16),
        grid=(kdim // tk,),
        in_specs=[pl.BlockSpec((None, tk, n), lambda i: (l, i, 0))],
        out_specs=pl.BlockSpec((tk, n), lambda i: (i, 0)),
        compiler_params=_cparams("parallel"),
        name="cast_bf16",
    )(w_all)


def _proj_kernel(h_ref, w_ref, gain_ref, nf_ref, rf_ref, cos_ref, sa_ref, sb_ref, bd_ref, o_ref, ob_ref, xs_ref,
                 *, sub):
    w, bd = w_ref[...], bd_ref[...]
    normed = nf_ref[...] > 0.5
    gain, rf = gain_ref[...], rf_ref[...]
    xs_ref[...] = jnp.dot(h_ref[...], w, preferred_element_type=F32)
    for c in range(h_ref.shape[0] // sub):
        rows = slice(c * sub, (c + 1) * sub)
        x = xs_ref[rows, :]
        ms = _seg_sum(x * x, bd) * (1.0 / HEAD_DIM)
        y = x * jnp.where(normed, lax.rsqrt(ms + NORM_EPS) * gain, 1.0)
        cosv, sav, sbv = cos_ref[rows, :], sa_ref[rows, :], sb_ref[rows, :]
        for s in range(2):
            ys = y[:, s * LANE:(s + 1) * LANE]
            f = rf[:, s * LANE:(s + 1) * LANE]
            roped = ys * cosv + pltpu.roll(ys, 8, 1) * sav + pltpu.roll(ys, LANE - 8, 1) * sbv
            out = jnp.where(f > 0.5, roped, ys)
            o_ref[rows, s * LANE:(s + 1) * LANE] = out
            ob_ref[rows, s * LANE:(s + 1) * LANE] = out.astype(BF16)


def _qkv_projection(h, w_pack, gain, nf, rf, cos_t, sin_a, sin_b):
    m, d = h.shape
    tm, tn = _tile(m, 1024), 2 * LANE
    zw = N_QKV_SLOTS * LANE
    row = lambda i, j: (0, j)
    tab = lambda i, j: (i, 0)
    return pl.pallas_call(
        functools.partial(_proj_kernel, sub=_tile(tm, 256)),
        out_shape=[jax.ShapeDtypeStruct((m, zw), F32), jax.ShapeDtypeStruct((m, zw), BF16)],
        grid=(m // tm, zw // tn),
        in_specs=[pl.BlockSpec((tm, d), lambda i, j: (i, 0)), pl.BlockSpec((d, tn), lambda i, j: (0, j)),
                  pl.BlockSpec((1, tn), row), pl.BlockSpec((1, tn), row), pl.BlockSpec((1, tn), row),
                  pl.BlockSpec((tm, LANE), tab), pl.BlockSpec((tm, LANE), tab), pl.BlockSpec((tm, LANE), tab),
                  pl.BlockSpec((tn, tn), lambda i, j: (0, 0))],
        out_specs=[pl.BlockSpec((tm, tn), lambda i, j: (i, j))] * 2,
        scratch_shapes=[pltpu.VMEM((tm, tn), F32)],
        compiler_params=_cparams("parallel", "arbitrary"),
        name="qkv_projection",
    )(h, w_pack, gain, nf, rf, cos_t, sin_a, sin_b, _ones_blockdiag(tn))


def _column_vectors(lw):
    f = lambda v: jnp.asarray(v, F32).reshape(-1)
    ones = lambda n: jnp.ones((n,), F32)
    zeros = lambda n: jnp.zeros((n,), F32)
    rep = lambda v, n: jnp.tile(f(v), n)
    groups = [
        (rep(lw["a_qn"], 8), 1.0, 1.0), (rep(lw["a_kn"], 2), 1.0, 1.0), (ones(128), 0.0, 0.0),
        (ones(256), 0.0, 1.0), (ones(64), 0.0, 1.0), (ones(64 + LANE), 0.0, 0.0),
        (rep(lw["b_qn"], 8), 1.0, 0.0), (rep(lw["b_kn"], 8), 1.0, 0.0), (ones(512), 0.0, 0.0),
        (rep(lw["c_qn"], 4), 1.0, 1.0), (rep(lw["c_kn"], 4), 1.0, 1.0), (ones(512), 0.0, 0.0)]
    gain = jnp.concatenate([g for g, _, _ in groups]).reshape(1, -1)
    nf = jnp.concatenate([ones(g.shape[0]) * a for g, a, _ in groups]).reshape(1, -1)
    rf = jnp.concatenate([ones(g.shape[0]) * b for g, _, b in groups]).reshape(1, -1)
    del zeros
    return gain, nf, rf


def _rope_tables(pos):
    half = ROPE_DIM // 2
    inv_freq = ROPE_THETA ** (-jnp.arange(0, ROPE_DIM, 2, dtype=F32) / ROPE_DIM)
    ang = pos.astype(F32)[:, None] * inv_freq[None, :]
    cos, sin = jnp.cos(ang), jnp.sin(ang)
    rows = pos.shape[0]
    one = jnp.ones((rows, HEAD_DIM - ROPE_DIM), F32)
    zero = jnp.zeros((rows, HEAD_DIM - ROPE_DIM), F32)
    z8 = jnp.zeros((rows, half), F32)
    cos_h = jnp.concatenate([cos, cos, one], axis=1)
    sa_h = jnp.concatenate([z8, sin, zero], axis=1)
    sb_h = jnp.concatenate([-sin, z8, zero], axis=1)
    dup = lambda t: jnp.concatenate([t, t], axis=1)
    return dup(cos_h), dup(sa_h), dup(sb_h)


def _to_key(score):
    bits = pltpu.bitcast(score + 0.0, jnp.int32)
    return jnp.where(bits < 0, bits ^ 0x7FFFFFFF, bits)


def _indexer_scores(iq, iw, ik):
    kk = ik[:, :HEAD_DIM]
    sc = None
    for hd in range(A_IDX_HEADS):
        logit = _dot_nt3(iq[:, hd * HEAD_DIM:(hd + 1) * HEAD_DIM], kk)
        wgt = iw[:, HEAD_DIM + hd:HEAD_DIM + hd + 1] * (A_IDX_HEADS ** -0.5 * HEAD_DIM ** -0.5)
        term = jnp.maximum(logit, 0.0) * wgt
        sc = term if sc is None else sc + term
    return sc


def _lane_fold(x, op):
    out = x[:, :LANE]
    for s in range(1, x.shape[1] // LANE):
        out = op(out, x[:, s * LANE:(s + 1) * LANE])
    return out


def _count(key_ref, n_blk, blk, pred):
    rows = key_ref.shape[0]

    def body(b, acc):
        start = pl.multiple_of(b * blk, blk)
        kb = key_ref[:, pl.ds(start, blk)]
        return acc + _lane_fold(jnp.where(pred(kb, start), 1.0, 0.0), jnp.add)

    acc = lax.fori_loop(0, n_blk, body, jnp.zeros((rows, LANE), F32))
    return jnp.sum(acc, axis=-1, keepdims=True)


def _topk_threshold(key_ref, n_blk, blk, topk):
    rows = key_ref.shape[0]
    kf = float(topk)
    c0 = _count(key_ref, n_blk, blk, lambda kb, st: kb >= 0)
    ans = jnp.where(c0 >= kf, 0, INT_MIN).astype(jnp.int32)

    def bit_step(it, ans):
        cand = ans + jnp.left_shift(jnp.int32(1), 30 - it)
        c = _count(key_ref, n_blk, blk, lambda kb, st: kb >= cand)
        return jnp.where(c >= kf, cand, ans)

    thr = lax.fori_loop(0, 31, bit_step, ans)
    n_gt = _count(key_ref, n_blk, blk, lambda kb, st: kb > thr)
    n_eq = _count(key_ref, n_blk, blk, lambda kb, st: kb == thr)
    need = kf - n_gt
    nbits = int(n_blk * blk).bit_length() if isinstance(n_blk, int) else 14
    cut_all = jnp.full((rows, 1), 1 << nbits, jnp.int32)
    tie_overflow = jnp.max(jnp.where((n_eq > need) & (thr != NEG_KEY), 1.0, 0.0)) > 0.5

    def search_cut():
        def cut_step(it, cut):
            cand = cut + jnp.left_shift(jnp.int32(1), nbits - 1 - it)

            def pred(kb, st):
                idx = st + lax.broadcasted_iota(jnp.int32, kb.shape, 1)
                return (kb == thr) & (idx < cand)

            c = _count(key_ref, n_blk, blk, pred)
            return jnp.where(c <= need, cand, cut)

        return lax.fori_loop(0, nbits, cut_step, jnp.zeros((rows, 1), jnp.int32))

    cut = lax.cond(tie_overflow, search_cut, lambda: cut_all)
    return thr, cut


def _selected(kb, first_idx, thr, cut):
    idx = first_idx + lax.broadcasted_iota(jnp.int32, kb.shape, 1)
    return ((kb > thr) | ((kb == thr) & (idx < cut))) & (kb > NEG_KEY)


def _pair_heads(o_even, o_odd, group_half):
    lane = lax.broadcasted_iota(jnp.int32, o_even.shape, 1)
    if group_half == 0:
        return jnp.where(lane < HEAD_DIM, o_even, pltpu.roll(o_odd, HEAD_DIM, 1))
    return jnp.where(lane < HEAD_DIM, pltpu.roll(o_even, HEAD_DIM, 1), o_odd)


def _dsa_query_heads(q_ref, qh_ref):
    for hd in range(A_HEADS):
        slot, half = hd // 2, hd % 2
        group = hd // (A_HEADS // A_KV_HEADS)
        qs = q_ref[:, slot * LANE:(slot + 1) * LANE] * (HEAD_DIM ** -0.5)
        if half != group:
            qs = pltpu.roll(qs, HEAD_DIM, 1)
        qh_ref[hd] = jnp.where(_half_mask(qs.shape, group), qs, 0.0).astype(BF16)


def _softmax_block(sel, kblk, vblk, q_heads, m_ref, l_ref, acc_ref, s_ref, p_ref):
    n = len(q_heads)
    reps = kblk.shape[0] // LANE
    for hd in range(n):
        s_ref[hd] = _dot_nt(q_heads[hd], kblk)
    alphas = []
    for hd in range(n):
        s = jnp.where(sel, s_ref[hd], -jnp.inf)
        s_ref[hd] = s
        blk_max = jnp.max(_lane_fold(s, jnp.maximum), axis=-1, keepdims=True)
        m_prev = m_ref[hd]
        m_new = jnp.maximum(m_prev, blk_max)
        alphas.append(jnp.exp(m_prev - m_new))
        m_ref[hd] = m_new
    for hd in range(n):
        p = jnp.exp(s_ref[hd] - jnp.tile(m_ref[hd], (1, reps)))
        l_ref[hd] = alphas[hd] * l_ref[hd] + _lane_fold(p, jnp.add)
        p_ref[hd] = p.astype(BF16)
    for hd in range(n):
        acc_ref[hd] = alphas[hd] * acc_ref[hd] + jnp.dot(p_ref[hd], vblk, preferred_element_type=F32)


def _softmax_init(m_ref, l_ref, acc_ref):
    m_ref[...] = jnp.full(m_ref.shape, M_FLOOR, F32)
    l_ref[...] = jnp.zeros(l_ref.shape, F32)
    acc_ref[...] = jnp.zeros(acc_ref.shape, F32)


def _indexer_query3(iq_ref, iq3_ref):
    for hd in range(A_IDX_HEADS):
        slot, half = hd // 2, hd % 2
        x = iq_ref[:, slot * LANE:(slot + 1) * LANE]
        xl = jnp.where(_half_mask(x.shape, half), x, pltpu.roll(x, HEAD_DIM, 1))
        hi = xl.astype(BF16).astype(F32)
        lower = _half_mask(x.shape, 0)
        iq3_ref[hd, :, :LANE] = jnp.where(lower, hi, xl - hi).astype(BF16)
        iq3_ref[hd, :, LANE:] = jnp.where(lower, hi, 0.0).astype(BF16)


def _ik3_kernel(x_ref, o_ref):
    x = x_ref[...]
    lower = _half_mask(x.shape, 0)
    xl = jnp.where(lower, x, pltpu.roll(x, HEAD_DIM, 1))
    hi = xl.astype(BF16).astype(F32)
    o_ref[:, :LANE] = hi.astype(BF16)
    o_ref[:, LANE:] = jnp.where(lower, xl - hi, 0.0).astype(BF16)


def _indexer_keys3(z, t):
    tm = _tile(t, 1024)
    return pl.pallas_call(
        _ik3_kernel,
        out_shape=jax.ShapeDtypeStruct((t, 2 * LANE), BF16),
        grid=(t // tm,),
        in_specs=[pl.BlockSpec((tm, LANE), lambda i: (i, S_AIK))],
        out_specs=pl.BlockSpec((tm, 2 * LANE), lambda i: (i, 0)),
        compiler_params=_cparams("parallel"),
        name="indexer_keys",
    )(z)


def _dsa_prompt_kernel(q_ref, iq_ref, iw_ref, k_ref, v_ref, ik3_ref, o_ref,
                       key_ref, qh_ref, iq3_ref, m_ref, l_ref, acc_ref, s_ref, p_ref, *, tq, kb, topk):
    i = pl.program_id(0)
    q0 = i * tq
    n_blk = (q0 + tq + kb - 1) // kb
    row = lax.broadcasted_iota(jnp.int32, (tq, kb), 0)
    limit = (((q0 + row) >> 6) + 1) << 6
    _indexer_query3(iq_ref, iq3_ref)
    iw = iw_ref[...]
    wgt = [iw[:, HEAD_DIM + hd:HEAD_DIM + hd + 1] * (A_IDX_HEADS ** -0.5 * HEAD_DIM ** -0.5)
           for hd in range(A_IDX_HEADS)]

    def score_blk(b, carry):
        start = pl.multiple_of(b * kb, kb)
        ikb = ik3_ref[pl.ds(start, kb), :]
        sc = None
        for hd in range(A_IDX_HEADS):
            term = jnp.maximum(_dot_nt(iq3_ref[hd], ikb), 0.0) * wgt[hd]
            sc = term if sc is None else sc + term
        kpos = start + lax.broadcasted_iota(jnp.int32, (tq, kb), 1)
        key_ref[:, pl.ds(start, kb)] = jnp.where(kpos < limit, _to_key(sc), NEG_KEY)
        return carry

    lax.fori_loop(0, n_blk, score_blk, 0)
    thr, cut = _topk_threshold(key_ref, n_blk, kb, topk)
    _dsa_query_heads(q_ref, qh_ref)
    _softmax_init(m_ref, l_ref, acc_ref)

    def attn_blk(b, carry):
        start = pl.multiple_of(b * kb, kb)
        sel = _selected(key_ref[:, pl.ds(start, kb)], start, thr, cut)
        _softmax_block(sel, k_ref[pl.ds(start, kb), :], v_ref[pl.ds(start, kb), :],
                       [qh_ref[hd] for hd in range(A_HEADS)], m_ref, l_ref, acc_ref, s_ref, p_ref)
        return carry

    lax.fori_loop(0, n_blk, attn_blk, 0)
    for j in range(A_HEADS // 2):
        outs = [acc_ref[2 * j + e] / jnp.sum(l_ref[2 * j + e], axis=-1, keepdims=True) for e in range(2)]
        o_ref[:, j * LANE:(j + 1) * LANE] = _pair_heads(outs[0], outs[1], (2 * j) // (A_HEADS // A_KV_HEADS)).astype(o_ref.dtype)


def _mixer_call(kern, o_all, m, *, grid, in_specs, out_block, out_index, args, scratch=(), sem, name):
    n_in = len(args)
    if o_all is None:
        fn, specs, alias, extra = kern, list(in_specs), {}, []
    else:
        fn = lambda *refs: kern(*refs[:n_in], *refs[n_in + 1:])
        specs, alias, extra = list(in_specs) + [pl.BlockSpec(memory_space=pl.ANY)], {n_in: 0}, [o_all]
    return pl.pallas_call(
        fn,
        out_shape=jax.ShapeDtypeStruct((m, N_BRANCH * BRANCH_WIDTH), BF16),
        grid=grid,
        in_specs=specs,
        out_specs=pl.BlockSpec(out_block, out_index),
        scratch_shapes=list(scratch),
        input_output_aliases=alias,
        compiler_params=_cparams(*sem),
        name=name,
    )(*args, *extra)


def _dsa_prompt(z, zb, t, o_all, m):
    tq, kb = 128, 512
    topk = min(A_TOPK_MAX, t // 4)
    full = lambda s: pl.BlockSpec((t, LANE), lambda i: (0, s))
    heads = lambda dt, w=LANE: pltpu.VMEM((A_HEADS, tq, w), dt)
    return _mixer_call(
        functools.partial(_dsa_prompt_kernel, tq=tq, kb=kb, topk=topk), o_all, m,
        grid=(t // tq,),
        in_specs=[pl.BlockSpec((tq, 4 * LANE), lambda i: (i, S_AQ // 4)),
                  pl.BlockSpec((tq, 2 * LANE), lambda i: (i, S_AIQ // 2)),
                  pl.BlockSpec((tq, LANE), lambda i: (i, S_AIK)),
                  full(S_AK), full(S_AV), pl.BlockSpec((t, 2 * LANE), lambda i: (0, 0))],
        out_block=(tq, BRANCH_WIDTH), out_index=lambda i: (i, 0),
        scratch=[pltpu.VMEM((tq, t), jnp.int32), heads(BF16),
                 pltpu.VMEM((A_IDX_HEADS, tq, 2 * LANE), BF16),
                 heads(F32), heads(F32), heads(F32), heads(F32, kb), heads(BF16, kb)],
        sem=("arbitrary",), name="dsa_prompt",
        args=(z, z, z, zb, zb, _indexer_keys3(z, t)))


def _feature_major(cache, n_feat):
    nd = cache.ndim
    perm = (0, 1) + tuple(range(3, nd)) + (2,)
    return jnp.transpose(cache, perm).reshape(cache.shape[0], cache.shape[1], n_feat, cache.shape[2])


def _dsa_sample_kernel(q_ref, iq_ref, new_ik_ref, new_k_ref, new_v_ref, ckt_ref, cvt_ref, cikt_ref, o_ref,
                       key_ref, qh_ref, *, ts, n_seq, past, topk, q_pos0):
    rows_all = n_seq * ts
    width = key_ref.shape[1]
    iq_all, iw_all = iq_ref[...], new_ik_ref[...]
    row_c = lax.broadcasted_iota(jnp.int32, (ts, past), 0)
    limit_c = (((q_pos0 + row_c) >> 6) + 1) << 6
    kpos_c = lax.broadcasted_iota(jnp.int32, (ts, past), 1)
    kpos_n = past + lax.broadcasted_iota(jnp.int32, (ts, ts), 1)
    limit_n = (((q_pos0 + lax.broadcasted_iota(jnp.int32, (ts, ts), 0)) >> 6) + 1) << 6
    key_ref[:, past:] = jnp.full((rows_all, width - past), NEG_KEY, jnp.int32)
    for g in range(n_seq):
        rows = slice(g * ts, (g + 1) * ts)
        iq, iw = iq_all[rows], iw_all[rows]
        ikt = cikt_ref[g]
        ik_new = iw[:, :HEAD_DIM]
        sc_c = sc_n = None
        for hd in range(A_IDX_HEADS):
            qh = iq[:, hd * HEAD_DIM:(hd + 1) * HEAD_DIM]
            wgt = iw[:, HEAD_DIM + hd:HEAD_DIM + hd + 1] * (A_IDX_HEADS ** -0.5 * HEAD_DIM ** -0.5)
            t_c = jnp.maximum(_dot3(qh, ikt), 0.0) * wgt
            t_n = jnp.maximum(_dot_nt3(qh, ik_new), 0.0) * wgt
            sc_c = t_c if sc_c is None else sc_c + t_c
            sc_n = t_n if sc_n is None else sc_n + t_n
        key_ref[rows, :past] = jnp.where(kpos_c < limit_c, _to_key(sc_c), NEG_KEY)
        key_ref[rows, past:past + ts] = jnp.where(kpos_n < limit_n, _to_key(sc_n), NEG_KEY)

    thr, cut = _topk_threshold(key_ref, width // LANE, LANE, topk)
    _dsa_query_heads(q_ref, qh_ref)
    for g in range(n_seq):
        rows = slice(g * ts, (g + 1) * ts)
        sel_c = _selected(key_ref[rows, :past], 0, thr[rows], cut[rows])
        sel_n = _selected(key_ref[rows, past:past + ts], past, thr[rows], cut[rows])
        kct, vct = ckt_ref[g].astype(BF16), cvt_ref[g].astype(BF16)
        kn, vn = new_k_ref[rows, :], new_v_ref[rows, :]
        outs = []
        for hd in range(A_HEADS):
            qh = qh_ref[hd, rows, :]
            s_c = jnp.where(sel_c, jnp.dot(qh, kct, preferred_element_type=F32), -jnp.inf)
            s_n = jnp.where(sel_n, _dot_nt(qh, kn), -jnp.inf)
            m = jnp.maximum(jnp.max(s_c, axis=-1, keepdims=True), jnp.max(s_n, axis=-1, keepdims=True))
            p_c, p_n = jnp.exp(s_c - m), jnp.exp(s_n - m)
            l = jnp.sum(p_c, axis=-1, keepdims=True) + jnp.sum(p_n, axis=-1, keepdims=True)
            o = _dot_nt(p_c.astype(BF16), vct) + jnp.dot(p_n.astype(BF16), vn, preferred_element_type=F32)
            outs.append(o / l)
        for j in range(A_HEADS // 2):
            o_ref[rows, j * LANE:(j + 1) * LANE] = _pair_heads(
                outs[2 * j], outs[2 * j + 1], (2 * j) // (A_HEADS // A_KV_HEADS)).astype(o_ref.dtype)


def _dsa_sample(z, zb, cache_k, cache_v, cache_ik, l, t, nb, ts, o_all, m):
    past = cache_k.shape[2]
    topk = min(A_TOPK_MAX, (past + ts) // 4)
    width = ((past + ts + LANE - 1) // LANE) * LANE
    n_seq = LANE // ts
    rows = n_seq * ts
    assert LANE % ts == 0 and nb % n_seq == 0 and t % rows == 0
    rb = t // rows
    new = lambda s: pl.BlockSpec((rows, LANE), lambda i: (rb + i, s))
    cache = lambda w: pl.BlockSpec((None, n_seq, w, past), lambda i: (l, i, 0, 0))
    return _mixer_call(
        functools.partial(_dsa_sample_kernel, ts=ts, n_seq=n_seq, past=past, topk=topk, q_pos0=past), o_all, m,
        grid=(nb // n_seq,),
        in_specs=[pl.BlockSpec((rows, 4 * LANE), lambda i: (rb + i, S_AQ // 4)),
                  pl.BlockSpec((rows, 2 * LANE), lambda i: (rb + i, S_AIQ // 2)),
                  new(S_AIK), new(S_AK), new(S_AV), cache(LANE), cache(LANE), cache(HEAD_DIM)],
        out_block=(rows, BRANCH_WIDTH), out_index=lambda i: (rb + i, 0),
        scratch=[pltpu.VMEM((rows, width), jnp.int32), pltpu.VMEM((A_HEADS, rows, LANE), BF16)],
        sem=("arbitrary",), name="dsa_sample",
        args=(z, z, z, zb, zb, _feature_major(cache_k, LANE), _feature_major(cache_v, LANE),
              _feature_major(cache_ik, HEAD_DIM)))


def _band_prompt_kernel(q_ref, k_ref, v_ref, bias_ref, o_ref, *, tq, win):
    i = pl.program_id(1)
    w0 = pl.multiple_of(jnp.maximum(i * tq - B_WINDOW, 0), tq)
    kw = k_ref[pl.ds(w0, win), :]
    vw = v_ref[pl.ds(w0, win), :]
    kc = (w0 + lax.broadcasted_iota(jnp.int32, (tq, win), 1)) >> 6
    qc = (i * tq + lax.broadcasted_iota(jnp.int32, (tq, win), 0)) >> 6
    ok = (kc <= qc) & (kc >= qc - B_PAST_CHUNKS)
    qs = q_ref[...] * (HEAD_DIM ** -0.5)
    outs = []
    for e in range(2):
        qe = jnp.where(_half_mask(qs.shape, e), qs, 0.0).astype(BF16)
        s = jnp.where(ok, _dot_nt(qe, kw) + bias_ref[e], -jnp.inf)
        p = jnp.exp(s - jnp.max(s, axis=-1, keepdims=True))
        l = jnp.sum(p, axis=-1, keepdims=True)
        outs.append(jnp.dot(p.astype(BF16), vw, preferred_element_type=F32) / l)
    o_ref[...] = jnp.where(_half_mask(outs[0].shape, 0), outs[0], outs[1]).astype(o_ref.dtype)


def _band_prompt(z, zb, bias, t, tq, o_all, m):
    win = B_WINDOW + tq
    n_case = B_WINDOW // tq
    return _mixer_call(
        functools.partial(_band_prompt_kernel, tq=tq, win=win), o_all, m,
        grid=(B_HEADS // 2, t // tq),
        in_specs=[pl.BlockSpec((tq, LANE), lambda j, i: (i, S_BQ + j)),
                  pl.BlockSpec((t, LANE), lambda j, i: (0, S_BK + j)),
                  pl.BlockSpec((t, LANE), lambda j, i: (0, S_BV + j)),
                  pl.BlockSpec((None, 2, tq, win), lambda j, i: (jnp.minimum(i, n_case), j, 0, 0))],
        out_block=(tq, LANE), out_index=lambda j, i: (i, BRANCH_WIDTH // LANE + j),
        sem=("parallel", "arbitrary"), name="band_prompt", args=(z, zb, zb, bias))


def _band_sample_kernel(q_ref, kn_ref, vn_ref, kct_ref, vct_ref, bc_ref, bn_ref, o_ref):
    kct, vct = kct_ref[...].astype(BF16), vct_ref[...].astype(BF16)
    kn, vn = kn_ref[...], vn_ref[...]
    qs = q_ref[...] * (HEAD_DIM ** -0.5)
    outs = []
    for e in range(2):
        qe = jnp.where(_half_mask(qs.shape, e), qs, 0.0).astype(BF16)
        s_c = jnp.dot(qe, kct, preferred_element_type=F32) + bc_ref[e]
        s_n = _dot_nt(qe, kn) + bn_ref[e]
        m = jnp.maximum(jnp.max(s_c, axis=-1, keepdims=True), jnp.max(s_n, axis=-1, keepdims=True))
        p_c, p_n = jnp.exp(s_c - m), jnp.exp(s_n - m)
        l = jnp.sum(p_c, axis=-1, keepdims=True) + jnp.sum(p_n, axis=-1, keepdims=True)
        o = _dot_nt(p_c.astype(BF16), vct) + jnp.dot(p_n.astype(BF16), vn, preferred_element_type=F32)
        outs.append(o / l)
    o_ref[...] = jnp.where(_half_mask(outs[0].shape, 0), outs[0], outs[1]).astype(o_ref.dtype)


def _band_sample(z, zb, cache_k, cache_v, bias_c, bias_n, l, t, nb, ts, o_all, m):
    pb = cache_k.shape[2]
    rb = t // ts
    cache = pl.BlockSpec((None, None, LANE, pb), lambda b, j: (l, b, j, 0))
    return _mixer_call(
        _band_sample_kernel, o_all, m,
        grid=(nb, B_HEADS // 2),
        in_specs=[pl.BlockSpec((ts, LANE), lambda b, j: (rb + b, S_BQ + j)),
                  pl.BlockSpec((ts, LANE), lambda b, j: (rb + b, S_BK + j)),
                  pl.BlockSpec((ts, LANE), lambda b, j: (rb + b, S_BV + j)),
                  cache, cache,
                  pl.BlockSpec((2, ts, pb), lambda b, j: (j, 0, 0)),
                  pl.BlockSpec((2, ts, ts), lambda b, j: (j, 0, 0))],
        out_block=(ts, LANE), out_index=lambda b, j: (rb + b, BRANCH_WIDTH // LANE + j),
        sem=("parallel", "arbitrary"), name="band_sample",
        args=(z, zb, zb, _feature_major(cache_k, BRANCH_WIDTH), _feature_major(cache_v, BRANCH_WIDTH),
              bias_c, bias_n))


def _band_bias(table, tq):
    n_case = B_WINDOW // tq
    win = B_WINDOW + tq
    width = win + B_WINDOW
    j = np.arange(width + tq - 1) - (tq - 1) - B_WINDOW
    ext = table.astype(F32)[:, np.clip(j, -B_REL_CLIP, B_REL_CLIP) + B_REL_CLIP]
    toe = jnp.stack([ext[:, tq - 1 - r:tq - 1 - r + width] for r in range(tq)], axis=1)
    return jnp.stack([toe[:, :, B_WINDOW - c * tq:B_WINDOW - c * tq + win] for c in range(n_case + 1)], axis=0)


def _lambda(lam_ref, lam_init):
    lv = lam_ref[...]
    return (jnp.exp(jnp.sum(lv[0:1] * lv[1:2], axis=-1, keepdims=True))
            - jnp.exp(jnp.sum(lv[2:3] * lv[3:4], axis=-1, keepdims=True)) + lam_init)


def _diff_finish(o0, o1, lam, on_ref, lam_init):
    attn = o0 - lam * o1
    ms = jnp.mean(attn * attn, axis=-1, keepdims=True)
    return (attn * lax.rsqrt(ms + NORM_EPS) * on_ref[...]) * (1.0 - lam_init)


def _diff_prompt_kernel(q_ref, k_ref, v_ref, lam_ref, on_ref, o_ref, qh_ref, m_ref, l_ref, acc_ref, s_ref, p_ref,
                        *, tq, kb, lam_init):
    i = pl.program_id(1)
    q0 = i * tq
    n_blk = (q0 + tq + kb - 1) // kb
    row = lax.broadcasted_iota(jnp.int32, (tq, kb), 0)
    limit = (((q0 + row) >> 6) + 1) << 6
    qs = q_ref[...] * (HEAD_DIM ** -0.5)
    for c in range(2):
        qh_ref[c] = jnp.where(_half_mask(qs.shape, c), qs, 0.0).astype(BF16)
    _softmax_init(m_ref, l_ref, acc_ref)

    def attn_blk(b, carry):
        start = pl.multiple_of(b * kb, kb)
        ok = start + lax.broadcasted_iota(jnp.int32, (tq, kb), 1) < limit
        _softmax_block(ok, k_ref[pl.ds(start, kb), :], v_ref[pl.ds(start, kb), :],
                       [qh_ref[c] for c in range(2)], m_ref, l_ref, acc_ref, s_ref, p_ref)
        return carry

    lax.fori_loop(0, n_blk, attn_blk, 0)
    lam = _lambda(lam_ref, lam_init)
    outs = [acc_ref[c] / jnp.sum(l_ref[c], axis=-1, keepdims=True) for c in range(2)]
    o_ref[...] = _diff_finish(outs[0], outs[1], lam, on_ref, lam_init).astype(o_ref.dtype)


def _diff_prompt(z, zb, c_lam_all, c_on_all, l, t, lam_init, o_all, m):
    tq, kb = _tile(t, 256), 512
    maps = lambda dt, w=LANE: pltpu.VMEM((2, tq, w), dt)
    return _mixer_call(
        functools.partial(_diff_prompt_kernel, tq=tq, kb=kb, lam_init=lam_init), o_all, m,
        grid=(C_HEADS, t // tq),
        in_specs=[pl.BlockSpec((tq, LANE), lambda h, i: (i, S_CQ + h)),
                  pl.BlockSpec((t, LANE), lambda h, i: (0, S_CK + h)),
                  pl.BlockSpec((t, LANE), lambda h, i: (0, S_CV + h)),
                  pl.BlockSpec((None, 4, HEAD_DIM), lambda h, i: (l, 0, 0)),
                  _layer_vec(l, LANE)],
        out_block=(tq, LANE), out_index=lambda h, i: (i, 2 * BRANCH_WIDTH // LANE + h),
        scratch=[maps(BF16), maps(F32), maps(F32), maps(F32), maps(F32, kb), maps(BF16, kb)],
        sem=("parallel", "arbitrary"), name="diff_prompt",
        args=(z, zb, zb, c_lam_all, _vec3(c_on_all)))


def _diff_sample_kernel(q_ref, kn_ref, vn_ref, kct_ref, vc_ref, lam_ref, on_ref, o_ref, *, past, lam_init):
    kct = kct_ref[...].astype(BF16)
    vc = vc_ref[pl.ds(pl.program_id(1), past, stride=C_HEADS), :].astype(BF16)
    kn, vn = kn_ref[...], vn_ref[...]
    qs = q_ref[...] * (HEAD_DIM ** -0.5)
    outs = []
    for c in range(2):
        qc = jnp.where(_half_mask(qs.shape, c), qs, 0.0).astype(BF16)
        s_c = jnp.dot(qc, kct, preferred_element_type=F32)
        s_n = _dot_nt(qc, kn)
        m = jnp.maximum(jnp.max(s_c, axis=-1, keepdims=True), jnp.max(s_n, axis=-1, keepdims=True))
        p_c, p_n = jnp.exp(s_c - m), jnp.exp(s_n - m)
        l = jnp.sum(p_c, axis=-1, keepdims=True) + jnp.sum(p_n, axis=-1, keepdims=True)
        o = (jnp.dot(p_c.astype(BF16), vc, preferred_element_type=F32)
             + jnp.dot(p_n.astype(BF16), vn, preferred_element_type=F32))
        outs.append(o / l)
    lam = _lambda(lam_ref, lam_init)
    o_ref[...] = _diff_finish(outs[0], outs[1], lam, on_ref, lam_init).astype(o_ref.dtype)


def _diff_sample(z, zb, cache_k, cache_v, c_lam_all, c_on_all, l, t, nb, ts, lam_init, o_all, m):
    depth, _, past = cache_k.shape[:3]
    rb = t // ts
    return _mixer_call(
        functools.partial(_diff_sample_kernel, past=past, lam_init=lam_init), o_all, m,
        grid=(nb, C_HEADS),
        in_specs=[pl.BlockSpec((ts, LANE), lambda b, h: (rb + b, S_CQ + h)),
                  pl.BlockSpec((ts, LANE), lambda b, h: (rb + b, S_CK + h)),
                  pl.BlockSpec((ts, LANE), lambda b, h: (rb + b, S_CV + h)),
                  pl.BlockSpec((None, None, LANE, past), lambda b, h: (l, b, h, 0)),
                  pl.BlockSpec((None, None, past * C_HEADS, LANE), lambda b, h: (l, b, 0, 0)),
                  pl.BlockSpec((None, 4, HEAD_DIM), lambda b, h: (l, 0, 0)),
                  _layer_vec(l, LANE)],
        out_block=(ts, LANE), out_index=lambda b, h: (rb + b, 2 * BRANCH_WIDTH // LANE + h),
        sem=("parallel", "arbitrary"), name="diff_sample",
        args=(z, zb, zb, _feature_major(cache_k, BRANCH_WIDTH),
              cache_v.reshape(depth, nb, past * C_HEADS, LANE), c_lam_all, _vec3(c_on_all)))


def _rwkv_pre_kernel(zd_ref, prev_ref, first_ref, mu_ref, w0_ref, w2_ref, a0_ref, a2_ref, g2_ref, kkw_ref,
                     ka_ref, bd_ref, r_ref, w_ref, k_ref, v_ref, kk_ref, b_ref, g_ref, *, tm, t, ts):
    zf = zd_ref[...]
    grow = pl.program_id(0) * tm + lax.broadcasted_iota(jnp.int32, (tm, 1), 0)
    zs = jnp.where(grow % tm == 0, prev_ref[7:8, :], pltpu.roll(zf, 1, 0))
    seq_start = (grow >= t) & ((grow - t) % ts == 0)
    zs = jnp.where(seq_start, first_ref[...], zs)
    zs = jnp.where(grow == 0, 0.0, zs)
    zm = zf + (zs - zf) * mu_ref[...]
    bw = BRANCH_WIDTH
    r, k, v = zm[:, :bw], zm[:, bw:2 * bw], zm[:, 2 * bw:3 * bw]
    wl, al, gl = zm[:, 3 * bw:3 * bw + 64], zm[:, 3 * bw + 64:3 * bw + 128], zm[:, 3 * bw + 128:]
    dot = lambda a, b: jnp.dot(a.astype(BF16), b.astype(BF16), preferred_element_type=F32)
    u = -(w0_ref[...] + dot(jnp.tanh(wl), w2_ref[...]))
    softplus = jnp.maximum(u, 0.0) + jnp.log(1.0 + jnp.exp(-jnp.abs(u)))
    w = -softplus - 0.5
    a = jax.nn.sigmoid(a0_ref[...] + dot(al, a2_ref[...]))
    kk = k * kkw_ref[...]
    nrm = jnp.sqrt(_seg_sum(kk * kk, bd_ref[...]))
    kk = kk / jnp.maximum(nrm, 1e-12)
    r_ref[...] = r
    w_ref[...] = -jnp.exp(w)
    k_ref[...] = k * (1.0 + (a - 1.0) * ka_ref[...])
    v_ref[...] = v
    kk_ref[...] = kk
    b_ref[...] = kk * a
    g_ref[...] = dot(jax.nn.sigmoid(gl), g2_ref[...])


def _rwkv_pre(zd, zfirst, p, l, t, ts):
    m, dc = zd.shape
    tm = _tile(math.gcd(t, m - t), 512)
    bw = BRANCH_WIDTH
    npt = t // tm
    rows = pl.BlockSpec((tm, dc), lambda i: (i, 0))
    vec = lambda n: _layer_vec(l, n)
    mat = lambda k: pl.BlockSpec((None, k, bw), lambda i: (l, 0, 0))
    out = jax.ShapeDtypeStruct((m, bw), F32)
    return pl.pallas_call(
        functools.partial(_rwkv_pre_kernel, tm=tm, t=t, ts=ts),
        out_shape=[out] * 7,
        grid=(m // tm,),
        in_specs=[rows,
                  pl.BlockSpec((8, dc), lambda i: (jnp.maximum(i * (tm // 8) - 1, 0), 0)),
                  pl.BlockSpec((tm, dc), lambda i: (jnp.maximum(i - npt, 0), 0)),
                  vec(dc), vec(bw), mat(64), vec(bw), mat(64), mat(128), vec(bw), vec(bw),
                  pl.BlockSpec((bw, bw), lambda i: (0, 0))],
        out_specs=[pl.BlockSpec((tm, bw), lambda i: (i, 0))] * 7,
        compiler_params=_cparams("parallel"),
        name="rwkv_pre",
    )(zd, zd, zfirst, _vec3(p["d_mu"]), _vec3(p["d_w0"]), p["d_w2"], _vec3(p["d_a0"]), p["d_a2"], p["d_g2"],
      _vec3(p["d_k_k"]), _vec3(p["d_k_a"]), _ones_blockdiag(bw))


RW_CHUNK = 16
RW_BLOCK = 128
RW_SLOTS = BRANCH_WIDTH // LANE


def _dot3(a, b):
    ah, al = _split2(a)
    bh, bl = _split2(b)
    d = lambda x, y: jnp.dot(x, y, preferred_element_type=F32)
    return d(ah, bh) + d(al, bh) + d(ah, bl)


def _rwkv_chunk_kernel(*refs, seq_chunks):
    if seq_chunks:
        (r_ref, lw_ref, k_ref, v_ref, kk_ref, b_ref, s0_ref, y_ref, sf_ref,
         h_ref, u_ref, ab_ref, rb_ref, u0_ref, y0_ref, bt_ref, kt_ref, eg_ref) = refs
    else:
        (r_ref, lw_ref, k_ref, v_ref, kk_ref, b_ref, y_ref, sf_ref,
         h_ref, u_ref, ab_ref, rb_ref, u0_ref, y0_ref, bt_ref, kt_ref, eg_ref) = refs
        s0_ref = None

        @pl.when(pl.program_id(0) == 0)
        def _():
            h_ref[...] = jnp.zeros_like(h_ref)

    n = RW_BLOCK
    n_chunks = n // RW_CHUNK
    row = lax.broadcasted_iota(jnp.int32, (n, n), 0)
    col = lax.broadcasted_iota(jnp.int32, (n, n), 1)
    same = (row // RW_CHUNK) == (col // RW_CHUNK)
    strict, incl = same & (col < row), same & (col <= row)
    eye = jnp.where(row == col, 1.0, 0.0)
    head_diag = (row // HEAD_DIM) == (col // HEAD_DIM)
    in_chunk = row % RW_CHUNK
    k_pick = jnp.where(lax.broadcasted_iota(jnp.int32, (n, HEAD_DIM), 0) % HEAD_DIM
                       == lax.broadcasted_iota(jnp.int32, (n, HEAD_DIM), 1), 1.0, 0.0)
    bf = lambda x: x.astype(BF16)
    mm = lambda x, y: jnp.dot(bf(x), bf(y), preferred_element_type=F32)
    u_ref[...] = jnp.zeros_like(u_ref)

    for j in range(RW_SLOTS):
        sl = slice(j * LANE, (j + 1) * LANE)
        lw, v = lw_ref[:, sl], v_ref[:, sl]
        g = lw
        for d in (1, 2, 4, 8):
            g = g + jnp.where(in_chunk >= d, pltpu.roll(g, d, 0), 0.0)
        inv_g = jnp.exp(-g)
        a_t = -kk_ref[:, sl] * jnp.exp(g - lw)
        b_t, k_t = b_ref[:, sl] * inv_g, k_ref[:, sl] * inv_g
        r_t = r_ref[:, sl] * jnp.exp(g)
        b16, k16, v16 = bf(b_t), bf(k_t), bf(v)
        both = lambda f: [f(e) for e in range(2)]
        halves = both(lambda e: _half_mask((n, LANE), e))
        a_e = both(lambda e: bf(jnp.where(halves[e], a_t, 0.0)))
        r_e = both(lambda e: bf(jnp.where(halves[e], r_t, 0.0)))
        n_ab = both(lambda e: jnp.where(strict, _dot_nt(a_e[e], b16), 0.0))
        n_ak = both(lambda e: jnp.where(strict, _dot_nt(a_e[e], k16), 0.0))
        m_rb = both(lambda e: bf(jnp.where(incl, _dot_nt(r_e[e], b16), 0.0)))
        m_rk = both(lambda e: bf(jnp.where(incl, _dot_nt(r_e[e], k16), 0.0)))
        w_e = both(lambda e: mm(n_ak[e], v16))
        n2 = both(lambda e: mm(n_ab[e], n_ab[e]))
        n4 = both(lambda e: mm(n2[e], n2[e]))
        n8 = both(lambda e: mm(n4[e], n4[e]))
        tinv = both(lambda e: eye + n_ab[e])
        for pw in (n2, n4, n8):
            tinv = both(lambda e: tinv[e] + mm(tinv[e], pw[e]))
        t16 = both(lambda e: bf(tinv[e]))
        a_bar = both(lambda e: mm(t16[e], a_t))
        u0 = both(lambda e: mm(t16[e], w_e[e]))
        r_bar = both(lambda e: r_t + mm(m_rb[e], a_bar[e]))
        y0 = both(lambda e: mm(m_rb[e], u0[e]) + mm(m_rk[e], v16))
        pick = lambda pair: jnp.where(halves[0], pair[0], pair[1])
        ab_ref[j], rb_ref[j], u0_ref[j], y0_ref[j] = pick(a_bar), pick(r_bar), pick(u0), pick(y0)
        bt_ref[j], kt_ref[j], eg_ref[j] = b_t.T, k_t.T, jnp.exp(g).T

    for c in range(n_chunks):
        rows = slice(c * RW_CHUNK, (c + 1) * RW_CHUNK)
        col_c = (col // RW_CHUNK) == c
        for j in range(RW_SLOTS):
            sl = slice(j * LANE, (j + 1) * LANE)
            if seq_chunks and c % seq_chunks == 0:
                x = s0_ref[c // seq_chunks, sl, :]
                h = jnp.where(head_diag, _dot_nt3(k_pick, x), 0.0)
            else:
                h = h_ref[j]
            res = _dot3(jnp.concatenate([ab_ref[j, rows, :], rb_ref[j, rows, :]], axis=0), h)
            u_c = res[:RW_CHUNK] + u0_ref[j, rows, :]
            y_ref[rows, sl] = res[RW_CHUNK:] + y0_ref[j, rows, :]
            u_ref[j, rows, :] = u_c
            bk = jnp.concatenate([jnp.where(col_c, bt_ref[j], 0.0), jnp.where(col_c, kt_ref[j], 0.0)], axis=1)
            uv = jnp.concatenate([u_ref[j], v_ref[:, sl]], axis=0)
            inc = jnp.where(head_diag, mm(bk, uv), 0.0)
            g_end = eg_ref[j, :, (c + 1) * RW_CHUNK - 1:(c + 1) * RW_CHUNK]
            h = g_end * (h + inc)
            h_ref[j] = h
            if seq_chunks and (c + 1) % seq_chunks == 0:
                ht = h.T
                sf_ref[c // seq_chunks, sl, :] = (ht + pltpu.roll(ht, HEAD_DIM, 1))[:, :HEAD_DIM]

    if not seq_chunks:
        @pl.when(pl.program_id(0) == pl.num_programs(0) - 1)
        def _():
            for j in range(RW_SLOTS):
                ht = h_ref[j].T
                sf_ref[0, j * LANE:(j + 1) * LANE, :] = (ht + pltpu.roll(ht, HEAD_DIM, 1))[:, :HEAD_DIM]


def _rwkv_chunked(ops, s0, row0, n_seq, t):
    n = RW_BLOCK
    bw = BRANCH_WIDTH
    assert row0 % n == 0 and (n_seq * t) % n == 0 and t % RW_CHUNK == 0
    rb = row0 // n
    rows = pl.BlockSpec((n, bw), lambda i: (rb + i, 0))
    if s0 is None:
        assert n_seq == 1
        seq_chunks, per_blk, extra, extra_specs = 0, 1, [], []
        sf_spec = pl.BlockSpec((1, bw, HEAD_DIM), lambda i: (0, 0, 0))
    else:
        assert n % t == 0
        seq_chunks, per_blk = t // RW_CHUNK, n // t
        extra = [s0.reshape(n_seq, bw, HEAD_DIM)]
        extra_specs = [pl.BlockSpec((per_blk, bw, HEAD_DIM), lambda i: (i, 0, 0))]
        sf_spec = pl.BlockSpec((per_blk, bw, HEAD_DIM), lambda i: (i, 0, 0))
    slot = lambda dt=F32: pltpu.VMEM((RW_SLOTS, n, LANE), dt)
    y, sf = pl.pallas_call(
        functools.partial(_rwkv_chunk_kernel, seq_chunks=seq_chunks),
        out_shape=[jax.ShapeDtypeStruct((n_seq * t, bw), F32), jax.ShapeDtypeStruct((n_seq, bw, HEAD_DIM), F32)],
        grid=(n_seq * t // n,),
        in_specs=[rows] * 6 + extra_specs,
        out_specs=[pl.BlockSpec((n, bw), lambda i: (i, 0)), sf_spec],
        scratch_shapes=[slot() for _ in range(9)],
        compiler_params=_cparams("arbitrary"),
        name="rwkv_chunked",
    )(*ops, *extra)
    return y, sf.reshape(n_seq, D_HEADS, HEAD_DIM, HEAD_DIM)


def _rwkv_post_kernel(y_ref, r_ref, k_ref, v_ref, g_ref, lnw_ref, lnb_ref, rk_ref, bd_ref, o_ref):
    bd = bd_ref[...]
    y = y_ref[...]
    mean = _seg_sum(y, bd) * (1.0 / HEAD_DIM)
    yc = y - mean
    var = _seg_sum(yc * yc, bd) * (1.0 / HEAD_DIM)
    yn = yc * lax.rsqrt(var + D_GN_EPS) * lnw_ref[...] + lnb_ref[...]
    bonus = _seg_sum(r_ref[...] * k_ref[...] * rk_ref[...], bd) * v_ref[...]
    o_ref[...] = ((yn + bonus) * g_ref[...]).astype(o_ref.dtype)


def _rwkv_post(y, r, k, v, g, p, l, o_all):
    m, bw = y.shape
    tm = _tile(m, 512)
    rows = pl.BlockSpec((tm, bw), lambda i: (i, 0))
    vec = _layer_vec(l, bw)
    return _mixer_call(
        _rwkv_post_kernel, o_all, m,
        grid=(m // tm,),
        in_specs=[rows] * 5 + [vec] * 3 + [pl.BlockSpec((bw, bw), lambda i: (0, 0))],
        out_block=(tm, bw), out_index=lambda i: (i, N_BRANCH - 1),
        sem=("parallel",), name="rwkv_post",
        args=(y, r, k, v, g, _vec3(p["d_ln_w"]), _vec3(p["d_ln_b"]), _vec3(p["d_r_k"]), _ones_blockdiag(bw)))


def _merge_kernel(o_ref, wbr_ref, h_ref, wg_ref, out_ref, acc_ref):
    n = pl.program_id(2)

    @pl.when(n == 0)
    def _():
        acc_ref[...] = jnp.zeros_like(acc_ref)

    u = jnp.dot(o_ref[...], wbr_ref[...].astype(BF16), preferred_element_type=F32)
    gate = jax.nn.sigmoid(jnp.dot(h_ref[...], wg_ref[...], preferred_element_type=F32))
    acc_ref[...] += u * gate

    @pl.when(n == N_BRANCH - 1)
    def _():
        out_ref[...] = acc_ref[...].astype(out_ref.dtype)


def _merge(o_all, w_br_all, h, w_pack, l):
    m, d = h.shape
    tm, tn = _tile(m, 1024), 512
    nj = d // tn
    g0 = S_GATE * LANE // tn
    return pl.pallas_call(
        _merge_kernel,
        out_shape=jax.ShapeDtypeStruct((m, d), BF16),
        grid=(m // tm, nj, N_BRANCH),
        in_specs=[pl.BlockSpec((tm, BRANCH_WIDTH), lambda i, j, n: (i, n)),
                  pl.BlockSpec((None, None, BRANCH_WIDTH, tn), lambda i, j, n: (l, n, 0, j)),
                  pl.BlockSpec((tm, d), lambda i, j, n: (i, 0)),
                  pl.BlockSpec((d, tn), lambda i, j, n: (0, g0 + n * nj + j))],
        out_specs=pl.BlockSpec((tm, tn), lambda i, j, n: (i, j)),
        scratch_shapes=[pltpu.VMEM((tm, tn), F32)],
        compiler_params=_cparams("parallel", "parallel", "arbitrary"),
        name="branch_merge",
    )(o_all, w_br_all, h, w_pack)


def _ple_kernel(x_ref, pe_ref, wple_ref, h_ref, wg_ref, o_ref):
    emb = jnp.dot(pe_ref[...].astype(BF16), wple_ref[...].astype(BF16), preferred_element_type=F32)
    gate = jax.nn.sigmoid(jnp.dot(h_ref[...], wg_ref[...].astype(BF16), preferred_element_type=F32))
    o_ref[...] = x_ref[...] + emb * gate


def _ple(x, pe, w_ple_all, h, w_gate_all, l):
    m, d = x.shape
    pd = pe.shape[1]
    tm, tn = _tile(m, 1024), 512
    return pl.pallas_call(
        _ple_kernel,
        out_shape=jax.ShapeDtypeStruct((m, d), F32),
        grid=(m // tm, d // tn),
        in_specs=[pl.BlockSpec((tm, tn), lambda i, j: (i, j)),
                  pl.BlockSpec((tm, pd), lambda i, j: (i, 0)),
                  pl.BlockSpec((None, pd, tn), lambda i, j: (l, 0, j)),
                  pl.BlockSpec((tm, d), lambda i, j: (i, 0)),
                  pl.BlockSpec((None, d, tn), lambda i, j: (l, 0, j))],
        out_specs=pl.BlockSpec((tm, tn), lambda i, j: (i, j)),
        compiler_params=_cparams("parallel", "parallel"),
        name="ple",
    )(x, pe, w_ple_all, h, w_gate_all)


def _layer(l, x, pe, p, t, nb, ts, tabs):
    m = x.shape[0]
    lam_init = 0.8 - 0.6 * math.exp(-0.3 * l)
    dc = p["d_mu"].shape[1]
    d = x.shape[1]
    lw = {"a_qn": p["a_q_norm"][l], "a_kn": p["a_k_norm"][l], "b_qn": p["b_q_norm"][l], "b_kn": p["b_k_norm"][l],
          "c_qn": p["c_q_norm"][l], "c_kn": p["c_k_norm"][l]}

    h = _rmsnorm(x, p["norm1_g"], l)
    w_pack = _pack_w_in(p["w_in"], l)
    z, zb = _qkv_projection(h, w_pack, *_column_vectors(lw), *tabs)
    zd = _matmul(h, w_pack, lambda acc: acc, F32, n=dc, col0=S_D * LANE, tn=256, name="rwkv_projection")

    o_all = _dsa_prompt(z, zb, t, None, m)
    o_all = _dsa_sample(z, zb, p["cache_a_k"], p["cache_a_v"], p["cache_a_kidx"], l, t, nb, ts, o_all, m)
    tq_b = 128
    bias = _band_bias(p["b_rel_bias"][l], tq_b)
    pb = p["cache_b_k"].shape[2]
    generic = bias[B_WINDOW // tq_b]
    o_all = _band_prompt(z, zb, bias, t, tq_b, o_all, m)
    o_all = _band_sample(z, zb, p["cache_b_k"], p["cache_b_v"], generic[:, :ts, B_WINDOW - pb:B_WINDOW],
                         generic[:, :ts, B_WINDOW:B_WINDOW + ts], l, t, nb, ts, o_all, m)
    c_on = p["c_out_norm"]
    o_all = _diff_prompt(z, zb, p["c_lambda"], c_on, l, t, lam_init, o_all, m)
    o_all = _diff_sample(z, zb, p["cache_c_k"], p["cache_c_v"], p["c_lambda"], c_on, l, t, nb, ts, lam_init,
                         o_all, m)

    zfirst = jnp.broadcast_to(p["state_d_shift"][l], (nb, ts, dc)).reshape(nb * ts, dc)
    r, w, k, v, kk, b, g = _rwkv_pre(zd, zfirst, p, l, t, ts)
    ops = (r, w, k, v, kk, b)
    y_p, wkv_p = _rwkv_chunked(ops, None, 0, 1, t)
    y_s, wkv_s = _rwkv_chunked(ops, p["state_d_wkv"][l], t, nb, ts)
    o_all = _rwkv_post(jnp.concatenate([y_p, y_s], axis=0), r, k, v, g, p, l, o_all)

    ug = _merge(o_all, p["w_branch"], h, w_pack, l)
    res = lambda acc, r_: r_ + acc
    x = _matmul(ug, p["w_out"], res, F32, n=d, layer=l, residual=x, name="out_proj")
    h2 = _rmsnorm(x, p["norm2_g"], l)
    ffn = p["w_up"].shape[2]
    up = _matmul(h2, p["w_up"], lambda acc: jnp.square(jnp.maximum(acc, 0.0)), BF16, n=ffn, layer=l, name="mlp_up")
    x = _matmul(up, _cast_bf16(p["w_down"], l), res, F32, n=d, residual=x, tn=1024, name="mlp_down")
    h3 = _rmsnorm(x, p["norm3_g"], l)
    x = _ple(x, pe, p["w_ple"], h3, p["w_ple_gate"], l)

    slot = lambda rows, s, n=1: z[rows, s * LANE:(s + n) * LANE]

    def rows_of(sl, lead):
        a = lambda s, n, shape: slot(sl, s, n).reshape(lead + shape)
        ak = a(S_AK, 1, (A_KV_HEADS, HEAD_DIM))
        av = a(S_AV, 1, (A_KV_HEADS, HEAD_DIM))
        aik = slot(sl, S_AIK)[:, :HEAD_DIM].reshape(lead + (HEAD_DIM,))
        bk = a(S_BK, 4, (B_HEADS, HEAD_DIM))
        bv = a(S_BV, 4, (B_HEADS, HEAD_DIM))
        ck = a(S_CK, 4, (C_HEADS, 2, HEAD_DIM))
        cv = a(S_CV, 4, (C_HEADS, 2 * HEAD_DIM))
        return ak, av, aik, bk, bv, ck, cv

    keep = min(B_WINDOW, t)
    pak, pav, paik, pbk, pbv, pck, pcv = rows_of(slice(0, t), (1, t))
    new_p = (pak, pav, paik, pbk[:, t - keep:], pbv[:, t - keep:], pck, pcv, wkv_p, zd[t - 1:t].reshape(1, 1, dc))
    new_s = rows_of(slice(t, m), (nb, ts)) + (wkv_s, zd[t:].reshape(nb, ts, dc)[:, -1:])
    return x, new_p, new_s


def kernel(x_prompt, x_sample, cache_a_k, cache_a_v, cache_a_kidx, cache_b_k, cache_b_v, cache_c_k, cache_c_v, state_d_wkv, state_d_shift, p_prompt, p_sample, norm1_g, w_in, a_q_norm, a_k_norm, b_q_norm, b_k_norm, b_rel_bias, c_q_norm, c_k_norm, c_lambda, c_out_norm, d_mu, d_w0, d_w2, d_a0, d_a2, d_g2, d_k_k, d_k_a, d_r_k, d_ln_w, d_ln_b, w_branch, w_out, norm2_g, w_up, w_down, norm3_g, w_ple, w_ple_gate):
    batch, t, d = x_prompt.shape
    nb, ts, _ = x_sample.shape
    past = cache_a_k.shape[2]
    depth = w_in.shape[0]
    assert batch == 1 and t % 512 == 0 and past % CHUNK == 0 and ts <= CHUNK and (nb * ts) % 8 == 0
    p = dict(cache_a_k=cache_a_k, cache_a_v=cache_a_v, cache_a_kidx=cache_a_kidx, cache_b_k=cache_b_k,
             cache_b_v=cache_b_v, cache_c_k=cache_c_k, cache_c_v=cache_c_v, state_d_wkv=state_d_wkv,
             state_d_shift=state_d_shift, norm1_g=norm1_g, w_in=w_in, a_q_norm=a_q_norm, a_k_norm=a_k_norm,
             b_q_norm=b_q_norm, b_k_norm=b_k_norm, b_rel_bias=b_rel_bias, c_q_norm=c_q_norm, c_k_norm=c_k_norm,
             c_lambda=c_lambda, c_out_norm=c_out_norm, d_mu=d_mu, d_w0=d_w0, d_w2=d_w2, d_a0=d_a0, d_a2=d_a2,
             d_g2=d_g2, d_k_k=d_k_k, d_k_a=d_k_a, d_r_k=d_r_k, d_ln_w=d_ln_w, d_ln_b=d_ln_b, w_branch=w_branch,
             w_out=w_out, norm2_g=norm2_g, w_up=w_up, w_down=w_down, norm3_g=norm3_g, w_ple=w_ple,
             w_ple_gate=w_ple_gate)
    x = jnp.concatenate([x_prompt[0], x_sample.reshape(nb * ts, d)], axis=0)
    pos = jnp.concatenate([jnp.arange(t, dtype=jnp.int32),
                           jnp.tile(past + jnp.arange(ts, dtype=jnp.int32), nb)])
    tabs = _rope_tables(pos)
    st_p = [[] for _ in range(9)]
    st_s = [[] for _ in range(9)]
    for l in range(depth):
        pe = jnp.concatenate([p_prompt[l, 0], p_sample[l].reshape(nb * ts, -1)], axis=0)
        x, new_p, new_s = _layer(l, x, pe, p, t, nb, ts, tabs)
        for lst, arr in zip(st_p, new_p):
            lst.append(arr)
        for lst, arr in zip(st_s, new_s):
            lst.append(arr)
    outs_p = [jnp.stack(s, axis=0) for s in st_p]
    outs_s = [jnp.stack(s, axis=0) for s in st_s]
    return (x[:t].reshape(1, t, d), x[t:].reshape(nb, ts, d), *outs_p, *outs_s)
```

```python
import functools
import math

import numpy as np
import jax
import jax.numpy as jnp
from jax import lax
from jax.experimental import pallas as pl
from jax.experimental.pallas import tpu as pltpu

F32 = jnp.float32
BF16 = jnp.bfloat16

CHUNK = 64
HEAD_DIM = 64
ROPE_DIM = 16
ROPE_THETA = 500000.0
N_BRANCH = 4
BRANCH_WIDTH = 512
A_HEADS, A_KV_HEADS, A_IDX_HEADS = 8, 2, 4
A_TOPK_MAX = 256
B_HEADS = 8
B_PAST_CHUNKS = 8
B_WINDOW = B_PAST_CHUNKS * CHUNK
B_REL_CLIP = 128
C_HEADS = 4
D_HEADS = 8
D_GN_EPS = 64e-5
NORM_EPS = 1e-6

LANE = 128
VMEM_LIMIT = 48 * 1024 * 1024

S_AQ, S_AK, S_AV, S_AIQ, S_AIK = 0, 4, 5, 6, 8
S_BQ, S_BK, S_BV = 10, 14, 18
S_CQ, S_CK, S_CV = 22, 26, 30
S_D, S_GATE = 34, 48
N_QKV_SLOTS = 34
N_D_SLOTS = 14
A_COLS = 1092
A_SLOTS = 9
SHIFT = A_COLS - (A_SLOTS - 1) * LANE

NEG_KEY = -2139095041
INT_MIN = -2147483648
M_FLOOR = -1e30


def _cparams(*sem):
    return pltpu.CompilerParams(dimension_semantics=sem, vmem_limit_bytes=VMEM_LIMIT)


def _tile(n, pref):
    t = min(n, pref)
    while n % t:
        t -= 8
    return t


def _split2(x):
    hi = x.astype(BF16)
    lo = (x - hi.astype(F32)).astype(BF16)
    return hi, lo


def _seg_sum(x, ones_bd):
    hi, lo = _split2(x)
    return (jnp.dot(hi, ones_bd, preferred_element_type=F32)
            + jnp.dot(lo, ones_bd, preferred_element_type=F32))


def _dot_nt(a, b):
    return lax.dot_general(a, b, (((1,), (1,)), ((), ())), preferred_element_type=F32)


def _dot_nt3(a, b):
    ah, al = _split2(a)
    bh, bl = _split2(b)
    return _dot_nt(ah, bh) + _dot_nt(ah, bl) + _dot_nt(al, bh)


def _ones_blockdiag(n):
    i = np.arange(n)
    return jnp.asarray((i[:, None] // HEAD_DIM) == (i[None, :] // HEAD_DIM), dtype=BF16)


def _vec3(a):
    return a.reshape(a.shape[0], 1, -1)


def _layer_vec(l, n):
    return pl.BlockSpec((None, 1, n), lambda *_: (l, 0, 0))


def _half_mask(shape, half):
    lane = lax.broadcasted_iota(jnp.int32, shape, len(shape) - 1)
    return (lane < HEAD_DIM) if half == 0 else (lane >= HEAD_DIM)


def _rms_kernel(x_ref, g_ref, o_ref):
    x = x_ref[...]
    ms = jnp.mean(x * x, axis=-1, keepdims=True)
    o_ref[...] = (x * lax.rsqrt(ms + NORM_EPS) * g_ref[...]).astype(o_ref.dtype)


def _rmsnorm(x, g_all, l):
    m, d = x.shape
    tm = _tile(m, 512)
    return pl.pallas_call(
        _rms_kernel,
        out_shape=jax.ShapeDtypeStruct((m, d), BF16),
        grid=(m // tm,),
        in_specs=[pl.BlockSpec((tm, d), lambda i: (i, 0)), _layer_vec(l, d)],
        out_specs=pl.BlockSpec((tm, d), lambda i: (i, 0)),
        compiler_params=_cparams("parallel"),
        name="rmsnorm",
    )(x, _vec3(g_all))


def _pack_kernel(a_ref, b_ref, o_ref):
    j = pl.program_id(0)
    row = lax.broadcasted_iota(jnp.int32, (LANE, a_ref.shape[2]), 0)
    for l in range(a_ref.shape[1]):
        a, b = a_ref[:, l, :], b_ref[:, l, :]
        shifted = jnp.where(row < LANE - SHIFT, pltpu.roll(a, LANE - SHIFT, 0), pltpu.roll(b, LANE - SHIFT, 0))
        out = jnp.where(j < A_SLOTS, a, jnp.where(j == A_SLOTS, 0.0, shifted))
        o_ref[l] = out.T.astype(o_ref.dtype)


def _pack_w_in(w_in_all):
    depth, d, n_in = w_in_all.shape
    n_slots = A_SLOTS + 1 + (n_in - A_COLS) // LANE
    assert (n_in - A_COLS) % LANE == 0 and 0 < SHIFT < LANE
    src_a = lambda j: jnp.where(j < A_SLOTS, j, j - 2)
    src_b = lambda j: jnp.where(j < A_SLOTS, j, j - 1)
    w_t = jnp.transpose(w_in_all, (2, 0, 1))
    return pl.pallas_call(
        _pack_kernel,
        out_shape=jax.ShapeDtypeStruct((depth, d, n_slots * LANE), BF16),
        grid=(n_slots,),
        in_specs=[pl.BlockSpec((LANE, depth, d), lambda j: (src_a(j), 0, 0)),
                  pl.BlockSpec((LANE, depth, d), lambda j: (src_b(j), 0, 0))],
        out_specs=pl.BlockSpec((depth, d, LANE), lambda j: (0, 0, j)),
        compiler_params=_cparams("parallel"),
        name="pack_w_in",
    )(w_t, w_t)


def _mm_kernel(*refs, n_extra, nk, epilogue):
    a_ref, w_ref = refs[0], refs[1]
    extra = refs[2:2 + n_extra]
    o_ref = refs[2 + n_extra]
    acc_ref = refs[3 + n_extra]
    k = pl.program_id(2)

    @pl.when(k == 0)
    def _():
        acc_ref[...] = jnp.zeros_like(acc_ref)

    acc_ref[...] += jnp.dot(a_ref[...], w_ref[...].astype(BF16), preferred_element_type=F32)

    @pl.when(k == nk - 1)
    def _():
        o_ref[...] = epilogue(acc_ref[...], *[e[...] for e in extra]).astype(o_ref.dtype)


def _matmul(a, w, epilogue, out_dtype, *, n, layer=None, col0=0, residual=None,
            tm=1024, tn=512, tk=2048, name="matmul"):
    m, kdim = a.shape
    tm, tn, tk = _tile(m, tm), _tile(n, tn), _tile(kdim, tk)
    nk = kdim // tk
    cb = col0 // tn
    assert col0 % tn == 0
    if layer is None:
        w_spec = pl.BlockSpec((tk, tn), lambda i, j, k: (k, cb + j))
    else:
        w_spec = pl.BlockSpec((None, tk, tn), lambda i, j, k: (layer, k, cb + j))
    specs = [pl.BlockSpec((tm, tk), lambda i, j, k: (i, k)), w_spec]
    extras = []
    if residual is not None:
        specs.append(pl.BlockSpec((tm, tn), lambda i, j, k: (i, j)))
        extras.append(residual)
    return pl.pallas_call(
        functools.partial(_mm_kernel, n_extra=len(extras), nk=nk, epilogue=epilogue),
        out_shape=jax.ShapeDtypeStruct((m, n), out_dtype),
        grid=(m // tm, n // tn, nk),
        in_specs=specs,
        out_specs=pl.BlockSpec((tm, tn), lambda i, j, k: (i, j)),
        scratch_shapes=[pltpu.VMEM((tm, tn), F32)],
        compiler_params=_cparams("parallel", "parallel", "arbitrary"),
        name=name,
    )(a, w, *extras)


def _cast_kernel(w_ref, o_ref):
    o_ref[...] = w_ref[...].astype(o_ref.dtype)


def _cast_bf16(w_all, l):
    _, kdim, n = w_all.shape
    tk = _tile(kdim, 512)
    return pl.pallas_call(
        _cast_kernel,
        out_shape=jax.ShapeDtypeStruct((kdim, n), BF16),
        grid=(kdim // tk,),
        in_specs=[pl.BlockSpec((None, tk, n), lambda i: (l, i, 0))],
        out_specs=pl.BlockSpec((tk, n), lambda i: (i, 0)),
        compiler_params=_cparams("parallel"),
        name="cast_bf16",
    )(w_all)


def _proj_kernel(h_ref, w_ref, gain_ref, nf_ref, rf_ref, cos_ref, sa_ref, sb_ref, bd_ref, o_ref, ob_ref, xs_ref,
                 *, sub):
    w, bd = w_ref[...], bd_ref[...]
    normed = nf_ref[...] > 0.5
    gain, rf = gain_ref[...], rf_ref[...]
    xs_ref[...] = jnp.dot(h_ref[...], w, preferred_element_type=F32)
    for c in range(h_ref.shape[0] // sub):
        rows = slice(c * sub, (c + 1) * sub)
        x = xs_ref[rows, :]
        ms = _seg_sum(x * x, bd) * (1.0 / HEAD_DIM)
        y = x * jnp.where(normed, lax.rsqrt(ms + NORM_EPS) * gain, 1.0)
        cosv, sav, sbv = cos_ref[rows, :], sa_ref[rows, :], sb_ref[rows, :]
        for s in range(2):
            ys = y[:, s * LANE:(s + 1) * LANE]
            f = rf[:, s * LANE:(s + 1) * LANE]
            roped = ys * cosv + pltpu.roll(ys, 8, 1) * sav + pltpu.roll(ys, LANE - 8, 1) * sbv
            out = jnp.where(f > 0.5, roped, ys)
            o_ref[rows, s * LANE:(s + 1) * LANE] = out
            ob_ref[rows, s * LANE:(s + 1) * LANE] = out.astype(BF16)


def _qkv_projection(h, w_pack, l, gain, nf, rf, cos_t, sin_a, sin_b):
    m, d = h.shape
    tm, tn = _tile(m, 1024), 2 * LANE
    zw = N_QKV_SLOTS * LANE
    row = lambda i, j: (0, j)
    tab = lambda i, j: (i, 0)
    return pl.pallas_call(
        functools.partial(_proj_kernel, sub=_tile(tm, 256)),
        out_shape=[jax.ShapeDtypeStruct((m, zw), F32), jax.ShapeDtypeStruct((m, zw), BF16)],
        grid=(m // tm, zw // tn),
        in_specs=[pl.BlockSpec((tm, d), lambda i, j: (i, 0)), pl.BlockSpec((None, d, tn), lambda i, j: (l, 0, j)),
                  pl.BlockSpec((1, tn), row), pl.BlockSpec((1, tn), row), pl.BlockSpec((1, tn), row),
                  pl.BlockSpec((tm, LANE), tab), pl.BlockSpec((tm, LANE), tab), pl.BlockSpec((tm, LANE), tab),
                  pl.BlockSpec((tn, tn), lambda i, j: (0, 0))],
        out_specs=[pl.BlockSpec((tm, tn), lambda i, j: (i, j))] * 2,
        scratch_shapes=[pltpu.VMEM((tm, tn), F32)],
        compiler_params=_cparams("parallel", "arbitrary"),
        name="qkv_projection",
    )(h, w_pack, gain, nf, rf, cos_t, sin_a, sin_b, _ones_blockdiag(tn))


def _column_vectors(lw):
    f = lambda v: jnp.asarray(v, F32).reshape(-1)
    ones = lambda n: jnp.ones((n,), F32)
    zeros = lambda n: jnp.zeros((n,), F32)
    rep = lambda v, n: jnp.tile(f(v), n)
    groups = [
        (rep(lw["a_qn"], 8), 1.0, 1.0), (rep(lw["a_kn"], 2), 1.0, 1.0), (ones(128), 0.0, 0.0),
        (ones(256), 0.0, 1.0), (ones(64), 0.0, 1.0), (ones(64 + LANE), 0.0, 0.0),
        (rep(lw["b_qn"], 8), 1.0, 0.0), (rep(lw["b_kn"], 8), 1.0, 0.0), (ones(512), 0.0, 0.0),
        (rep(lw["c_qn"], 4), 1.0, 1.0), (rep(lw["c_kn"], 4), 1.0, 1.0), (ones(512), 0.0, 0.0)]
    gain = jnp.concatenate([g for g, _, _ in groups]).reshape(1, -1)
    nf = jnp.concatenate([ones(g.shape[0]) * a for g, a, _ in groups]).reshape(1, -1)
    rf = jnp.concatenate([ones(g.shape[0]) * b for g, _, b in groups]).reshape(1, -1)
    del zeros
    return gain, nf, rf


def _rope_tables(pos):
    half = ROPE_DIM // 2
    inv_freq = ROPE_THETA ** (-jnp.arange(0, ROPE_DIM, 2, dtype=F32) / ROPE_DIM)
    ang = pos.astype(F32)[:, None] * inv_freq[None, :]
    cos, sin = jnp.cos(ang), jnp.sin(ang)
    rows = pos.shape[0]
    one = jnp.ones((rows, HEAD_DIM - ROPE_DIM), F32)
    zero = jnp.zeros((rows, HEAD_DIM - ROPE_DIM), F32)
    z8 = jnp.zeros((rows, half), F32)
    cos_h = jnp.concatenate([cos, cos, one], axis=1)
    sa_h = jnp.concatenate([z8, sin, zero], axis=1)
    sb_h = jnp.concatenate([-sin, z8, zero], axis=1)
    dup = lambda t: jnp.concatenate([t, t], axis=1)
    return dup(cos_h), dup(sa_h), dup(sb_h)


def _to_key(score):
    bits = pltpu.bitcast(score + 0.0, jnp.int32)
    return jnp.where(bits < 0, bits ^ 0x7FFFFFFF, bits)


def _indexer_scores(iq, iw, ik):
    kk = ik[:, :HEAD_DIM]
    sc = None
    for hd in range(A_IDX_HEADS):
        logit = _dot_nt3(iq[:, hd * HEAD_DIM:(hd + 1) * HEAD_DIM], kk)
        wgt = iw[:, HEAD_DIM + hd:HEAD_DIM + hd + 1] * (A_IDX_HEADS ** -0.5 * HEAD_DIM ** -0.5)
        term = jnp.maximum(logit, 0.0) * wgt
        sc = term if sc is None else sc + term
    return sc


def _lane_fold(x, op):
    out = x[:, :LANE]
    for s in range(1, x.shape[1] // LANE):
        out = op(out, x[:, s * LANE:(s + 1) * LANE])
    return out


def _count(key_ref, n_blk, blk, pred):
    rows = key_ref.shape[0]

    def body(b, acc):
        start = pl.multiple_of(b * blk, blk)
        kb = key_ref[:, pl.ds(start, blk)]
        return acc + _lane_fold(jnp.where(pred(kb, start), 1.0, 0.0), jnp.add)

    acc = lax.fori_loop(0, n_blk, body, jnp.zeros((rows, LANE), F32))
    return jnp.sum(acc, axis=-1, keepdims=True)


def _topk_threshold(key_ref, n_blk, blk, topk):
    rows = key_ref.shape[0]
    kf = float(topk)
    c0 = _count(key_ref, n_blk, blk, lambda kb, st: kb >= 0)
    ans = jnp.where(c0 >= kf, 0, INT_MIN).astype(jnp.int32)

    def bit_step(it, ans):
        cand = ans + jnp.left_shift(jnp.int32(1), 30 - it)
        c = _count(key_ref, n_blk, blk, lambda kb, st: kb >= cand)
        return jnp.where(c >= kf, cand, ans)

    thr = lax.fori_loop(0, 31, bit_step, ans)
    n_gt = _count(key_ref, n_blk, blk, lambda kb, st: kb > thr)
    n_eq = _count(key_ref, n_blk, blk, lambda kb, st: kb == thr)
    need = kf - n_gt
    nbits = int(n_blk * blk).bit_length() if isinstance(n_blk, int) else 14
    cut_all = jnp.full((rows, 1), 1 << nbits, jnp.int32)
    tie_overflow = jnp.max(jnp.where((n_eq > need) & (thr != NEG_KEY), 1.0, 0.0)) > 0.5

    def search_cut():
        def cut_step(it, cut):
            cand = cut + jnp.left_shift(jnp.int32(1), nbits - 1 - it)

            def pred(kb, st):
                idx = st + lax.broadcasted_iota(jnp.int32, kb.shape, 1)
                return (kb == thr) & (idx < cand)

            c = _count(key_ref, n_blk, blk, pred)
            return jnp.where(c <= need, cand, cut)

        return lax.fori_loop(0, nbits, cut_step, jnp.zeros((rows, 1), jnp.int32))

    cut = lax.cond(tie_overflow, search_cut, lambda: cut_all)
    return thr, cut


def _selected(kb, first_idx, thr, cut):
    idx = first_idx + lax.broadcasted_iota(jnp.int32, kb.shape, 1)
    return ((kb > thr) | ((kb == thr) & (idx < cut))) & (kb > NEG_KEY)


def _pair_heads(o_even, o_odd, group_half):
    lane = lax.broadcasted_iota(jnp.int32, o_even.shape, 1)
    if group_half == 0:
        return jnp.where(lane < HEAD_DIM, o_even, pltpu.roll(o_odd, HEAD_DIM, 1))
    return jnp.where(lane < HEAD_DIM, pltpu.roll(o_even, HEAD_DIM, 1), o_odd)


def _dsa_query_heads(q_ref, qh_ref):
    for hd in range(A_HEADS):
        slot, half = hd // 2, hd % 2
        group = hd // (A_HEADS // A_KV_HEADS)
        qs = q_ref[:, slot * LANE:(slot + 1) * LANE] * (HEAD_DIM ** -0.5)
        if half != group:
            qs = pltpu.roll(qs, HEAD_DIM, 1)
        qh_ref[hd] = jnp.where(_half_mask(qs.shape, group), qs, 0.0).astype(BF16)


def _softmax_block(sel, kblk, vblk, q_heads, m_ref, acc_ref, s_ref, p_ref):
    n = len(q_heads)
    reps = kblk.shape[0] // LANE
    v_ones = jnp.concatenate([vblk, jnp.ones_like(vblk)], axis=1)
    for hd in range(n):
        s_ref[hd] = _dot_nt(q_heads[hd], kblk)
    alphas = []
    for hd in range(n):
        s = s_ref[hd]
        if sel is not None:
            s = jnp.where(sel[hd] if isinstance(sel, (list, tuple)) else sel, s, -jnp.inf)
            s_ref[hd] = s
        blk_max = jnp.max(_lane_fold(s, jnp.maximum), axis=-1, keepdims=True)
        m_prev = m_ref[hd]
        m_new = jnp.maximum(m_prev, blk_max)
        alphas.append(jnp.exp(m_prev - m_new))
        m_ref[hd] = m_new
    for hd in range(n):
        p_ref[hd] = jnp.exp(s_ref[hd] - jnp.tile(m_ref[hd], (1, reps))).astype(BF16)
    for hd in range(n):
        alpha2 = jnp.tile(alphas[hd], (1, 2))
        acc_ref[hd] = alpha2 * acc_ref[hd] + jnp.dot(p_ref[hd], v_ones, preferred_element_type=F32)


def _softmax_init(m_ref, acc_ref):
    m_ref[...] = jnp.full(m_ref.shape, M_FLOOR, F32)
    acc_ref[...] = jnp.zeros(acc_ref.shape, F32)


def _softmax_result(acc_ref, hd):
    acc = acc_ref[hd]
    return acc[:, :LANE] / acc[:, LANE:]


def _indexer_query3(iq_ref, iq3_ref):
    for hd in range(A_IDX_HEADS):
        slot, half = hd // 2, hd % 2
        x = iq_ref[:, slot * LANE:(slot + 1) * LANE]
        xl = jnp.where(_half_mask(x.shape, half), x, pltpu.roll(x, HEAD_DIM, 1))
        hi = xl.astype(BF16).astype(F32)
        lower = _half_mask(x.shape, 0)
        iq3_ref[hd, :, :LANE] = jnp.where(lower, hi, xl - hi).astype(BF16)
        iq3_ref[hd, :, LANE:] = jnp.where(lower, hi, 0.0).astype(BF16)


def _ik3_kernel(x_ref, o_ref):
    x = x_ref[...]
    lower = _half_mask(x.shape, 0)
    xl = jnp.where(lower, x, pltpu.roll(x, HEAD_DIM, 1))
    hi = xl.astype(BF16).astype(F32)
    o_ref[:, :LANE] = hi.astype(BF16)
    o_ref[:, LANE:] = jnp.where(lower, xl - hi, 0.0).astype(BF16)


def _indexer_keys3(z, t):
    tm = _tile(t, 1024)
    return pl.pallas_call(
        _ik3_kernel,
        out_shape=jax.ShapeDtypeStruct((t, 2 * LANE), BF16),
        grid=(t // tm,),
        in_specs=[pl.BlockSpec((tm, LANE), lambda i: (i, S_AIK))],
        out_specs=pl.BlockSpec((tm, 2 * LANE), lambda i: (i, 0)),
        compiler_params=_cparams("parallel"),
        name="indexer_keys",
    )(z)


def _dsa_prompt_kernel(q_ref, iq_ref, iw_ref, k_ref, v_ref, ik3_ref, o_ref,
                       key_ref, qh_ref, iq3_ref, m_ref, acc_ref, s_ref, p_ref, *, tq, kb, topk):
    i = pl.program_id(0)
    q0 = i * tq
    n_blk = (q0 + tq + kb - 1) // kb
    row = lax.broadcasted_iota(jnp.int32, (tq, kb), 0)
    limit = (((q0 + row) >> 6) + 1) << 6
    _indexer_query3(iq_ref, iq3_ref)
    iw = iw_ref[...]
    wgt = [iw[:, HEAD_DIM + hd:HEAD_DIM + hd + 1] * (A_IDX_HEADS ** -0.5 * HEAD_DIM ** -0.5)
           for hd in range(A_IDX_HEADS)]

    def score_blk(b, carry):
        start = pl.multiple_of(b * kb, kb)
        ikb = ik3_ref[pl.ds(start, kb), :]
        for hd in range(A_IDX_HEADS):
            s_ref[hd] = _dot_nt(iq3_ref[hd], ikb)
        sc = None
        for hd in range(A_IDX_HEADS):
            term = jnp.maximum(s_ref[hd], 0.0) * wgt[hd]
            sc = term if sc is None else sc + term
        kpos = start + lax.broadcasted_iota(jnp.int32, (tq, kb), 1)
        key_ref[:, pl.ds(start, kb)] = jnp.where(kpos < limit, _to_key(sc), NEG_KEY)
        return carry

    lax.fori_loop(0, n_blk, score_blk, 0)
    thr, cut = _topk_threshold(key_ref, n_blk, kb, topk)
    _dsa_query_heads(q_ref, qh_ref)
    _softmax_init(m_ref, acc_ref)

    def attn_blk(b, carry):
        start = pl.multiple_of(b * kb, kb)
        sel = _selected(key_ref[:, pl.ds(start, kb)], start, thr, cut)
        _softmax_block(sel, k_ref[pl.ds(start, kb), :], v_ref[pl.ds(start, kb), :],
                       [qh_ref[hd] for hd in range(A_HEADS)], m_ref, acc_ref, s_ref, p_ref)
        return carry

    lax.fori_loop(0, n_blk, attn_blk, 0)
    for j in range(A_HEADS // 2):
        outs = [_softmax_result(acc_ref, 2 * j + e) for e in range(2)]
        o_ref[:, j * LANE:(j + 1) * LANE] = _pair_heads(outs[0], outs[1], (2 * j) // (A_HEADS // A_KV_HEADS)).astype(o_ref.dtype)


def _mixer_call(kern, o_all, m, *, grid, in_specs, out_block, out_index, args, scratch=(), sem, name):
    n_in = len(args)
    if o_all is None:
        fn, specs, alias, extra = kern, list(in_specs), {}, []
    else:
        fn = lambda *refs: kern(*refs[:n_in], *refs[n_in + 1:])
        specs, alias, extra = list(in_specs) + [pl.BlockSpec(memory_space=pl.ANY)], {n_in: 0}, [o_all]
    return pl.pallas_call(
        fn,
        out_shape=jax.ShapeDtypeStruct((m, N_BRANCH * BRANCH_WIDTH), BF16),
        grid=grid,
        in_specs=specs,
        out_specs=pl.BlockSpec(out_block, out_index),
        scratch_shapes=list(scratch),
        input_output_aliases=alias,
        compiler_params=_cparams(*sem),
        name=name,
    )(*args, *extra)


def _dsa_prompt(z, zb, t, o_all, m):
    tq, kb = 128, 512
    topk = min(A_TOPK_MAX, t // 4)
    full = lambda s: pl.BlockSpec((t, LANE), lambda i: (0, s))
    heads = lambda dt, w=LANE: pltpu.VMEM((A_HEADS, tq, w), dt)
    return _mixer_call(
        functools.partial(_dsa_prompt_kernel, tq=tq, kb=kb, topk=topk), o_all, m,
        grid=(t // tq,),
        in_specs=[pl.BlockSpec((tq, 4 * LANE), lambda i: (i, S_AQ // 4)),
                  pl.BlockSpec((tq, 2 * LANE), lambda i: (i, S_AIQ // 2)),
                  pl.BlockSpec((tq, LANE), lambda i: (i, S_AIK)),
                  full(S_AK), full(S_AV), pl.BlockSpec((t, 2 * LANE), lambda i: (0, 0))],
        out_block=(tq, BRANCH_WIDTH), out_index=lambda i: (i, 0),
        scratch=[pltpu.VMEM((tq, t), jnp.int32), heads(BF16),
                 pltpu.VMEM((A_IDX_HEADS, tq, 2 * LANE), BF16),
                 heads(F32), heads(F32, 2 * LANE), heads(F32, kb), heads(BF16, kb)],
        sem=("arbitrary",), name="dsa_prompt",
        args=(z, z, z, zb, zb, _indexer_keys3(z, t)))


def _feature_major(cache, n_feat):
    nd = cache.ndim
    perm = (0, 1) + tuple(range(3, nd)) + (2,)
    return jnp.transpose(cache, perm).reshape(cache.shape[0], cache.shape[1], n_feat, cache.shape[2])


def _dsa_sample_kernel(q_ref, iq_ref, new_ik_ref, new_k_ref, new_v_ref, ckt_ref, cvt_ref, cikt_ref, o_ref,
                       key_ref, qh_ref, *, ts, n_seq, past, topk, q_pos0):
    rows_all = n_seq * ts
    width = key_ref.shape[1]
    iq_all, iw_all = iq_ref[...], new_ik_ref[...]
    row_c = lax.broadcasted_iota(jnp.int32, (ts, past), 0)
    limit_c = (((q_pos0 + row_c) >> 6) + 1) << 6
    kpos_c = lax.broadcasted_iota(jnp.int32, (ts, past), 1)
    kpos_n = past + lax.broadcasted_iota(jnp.int32, (ts, ts), 1)
    limit_n = (((q_pos0 + lax.broadcasted_iota(jnp.int32, (ts, ts), 0)) >> 6) + 1) << 6
    key_ref[:, past:] = jnp.full((rows_all, width - past), NEG_KEY, jnp.int32)
    for g in range(n_seq):
        rows = slice(g * ts, (g + 1) * ts)
        iq, iw = iq_all[rows], iw_all[rows]
        ikt = cikt_ref[g]
        ik_new = iw[:, :HEAD_DIM]
        sc_c = sc_n = None
        for hd in range(A_IDX_HEADS):
            qh = iq[:, hd * HEAD_DIM:(hd + 1) * HEAD_DIM]
            wgt = iw[:, HEAD_DIM + hd:HEAD_DIM + hd + 1] * (A_IDX_HEADS ** -0.5 * HEAD_DIM ** -0.5)
            t_c = jnp.maximum(_dot3(qh, ikt), 0.0) * wgt
            t_n = jnp.maximum(_dot_nt3(qh, ik_new), 0.0) * wgt
            sc_c = t_c if sc_c is None else sc_c + t_c
            sc_n = t_n if sc_n is None else sc_n + t_n
        key_ref[rows, :past] = jnp.where(kpos_c < limit_c, _to_key(sc_c), NEG_KEY)
        key_ref[rows, past:past + ts] = jnp.where(kpos_n < limit_n, _to_key(sc_n), NEG_KEY)

    thr, cut = _topk_threshold(key_ref, width // LANE, LANE, topk)
    _dsa_query_heads(q_ref, qh_ref)
    for g in range(n_seq):
        rows = slice(g * ts, (g + 1) * ts)
        sel_c = _selected(key_ref[rows, :past], 0, thr[rows], cut[rows])
        sel_n = _selected(key_ref[rows, past:past + ts], past, thr[rows], cut[rows])
        kct, vct = ckt_ref[g].astype(BF16), cvt_ref[g].astype(BF16)
        kn, vn = new_k_ref[rows, :], new_v_ref[rows, :]
        outs = []
        for hd in range(A_HEADS):
            qh = qh_ref[hd, rows, :]
            s_c = jnp.where(sel_c, jnp.dot(qh, kct, preferred_element_type=F32), -jnp.inf)
            s_n = jnp.where(sel_n, _dot_nt(qh, kn), -jnp.inf)
            m = jnp.maximum(jnp.max(s_c, axis=-1, keepdims=True), jnp.max(s_n, axis=-1, keepdims=True))
            p_c, p_n = jnp.exp(s_c - m), jnp.exp(s_n - m)
            l = jnp.sum(p_c, axis=-1, keepdims=True) + jnp.sum(p_n, axis=-1, keepdims=True)
            o = _dot_nt(p_c.astype(BF16), vct) + jnp.dot(p_n.astype(BF16), vn, preferred_element_type=F32)
            outs.append(o / l)
        for j in range(A_HEADS // 2):
            o_ref[rows, j * LANE:(j + 1) * LANE] = _pair_heads(
                outs[2 * j], outs[2 * j + 1], (2 * j) // (A_HEADS // A_KV_HEADS)).astype(o_ref.dtype)


def _dsa_sample(z, zb, cache_k, cache_v, cache_ik, l, t, nb, ts, o_all, m):
    past = cache_k.shape[2]
    topk = min(A_TOPK_MAX, (past + ts) // 4)
    width = ((past + ts + LANE - 1) // LANE) * LANE
    n_seq = LANE // ts
    rows = n_seq * ts
    assert LANE % ts == 0 and nb % n_seq == 0 and t % rows == 0
    rb = t // rows
    new = lambda s: pl.BlockSpec((rows, LANE), lambda i: (rb + i, s))
    cache = lambda w: pl.BlockSpec((None, n_seq, w, past), lambda i: (l, i, 0, 0))
    return _mixer_call(
        functools.partial(_dsa_sample_kernel, ts=ts, n_seq=n_seq, past=past, topk=topk, q_pos0=past), o_all, m,
        grid=(nb // n_seq,),
        in_specs=[pl.BlockSpec((rows, 4 * LANE), lambda i: (rb + i, S_AQ // 4)),
                  pl.BlockSpec((rows, 2 * LANE), lambda i: (rb + i, S_AIQ // 2)),
                  new(S_AIK), new(S_AK), new(S_AV), cache(LANE), cache(LANE), cache(HEAD_DIM)],
        out_block=(rows, BRANCH_WIDTH), out_index=lambda i: (rb + i, 0),
        scratch=[pltpu.VMEM((rows, width), jnp.int32), pltpu.VMEM((A_HEADS, rows, LANE), BF16)],
        sem=("arbitrary",), name="dsa_sample",
        args=(z, z, z, zb, zb, _feature_major(cache_k, LANE), _feature_major(cache_v, LANE),
              _feature_major(cache_ik, HEAD_DIM)))


def _band_prompt_kernel(q_ref, k_ref, v_ref, bias_ref, o_ref, *, tq, win):
    i = pl.program_id(1)
    w0 = pl.multiple_of(jnp.maximum(i * tq - B_WINDOW, 0), tq)
    kw = k_ref[pl.ds(w0, win), :]
    vw = v_ref[pl.ds(w0, win), :]
    kc = (w0 + lax.broadcasted_iota(jnp.int32, (tq, win), 1)) >> 6
    qc = (i * tq + lax.broadcasted_iota(jnp.int32, (tq, win), 0)) >> 6
    ok = (kc <= qc) & (kc >= qc - B_PAST_CHUNKS)
    qs = q_ref[...] * (HEAD_DIM ** -0.5)
    outs = []
    for e in range(2):
        qe = jnp.where(_half_mask(qs.shape, e), qs, 0.0).astype(BF16)
        s = jnp.where(ok, _dot_nt(qe, kw) + bias_ref[e], -jnp.inf)
        p = jnp.exp(s - jnp.max(s, axis=-1, keepdims=True))
        l = jnp.sum(p, axis=-1, keepdims=True)
        outs.append(jnp.dot(p.astype(BF16), vw, preferred_element_type=F32) / l)
    o_ref[...] = jnp.where(_half_mask(outs[0].shape, 0), outs[0], outs[1]).astype(o_ref.dtype)


def _band_prompt(z, zb, bias, t, tq, o_all, m):
    win = B_WINDOW + tq
    n_case = B_WINDOW // tq
    return _mixer_call(
        functools.partial(_band_prompt_kernel, tq=tq, win=win), o_all, m,
        grid=(B_HEADS // 2, t // tq),
        in_specs=[pl.BlockSpec((tq, LANE), lambda j, i: (i, S_BQ + j)),
                  pl.BlockSpec((t, LANE), lambda j, i: (0, S_BK + j)),
                  pl.BlockSpec((t, LANE), lambda j, i: (0, S_BV + j)),
                  pl.BlockSpec((None, 2, tq, win), lambda j, i: (jnp.minimum(i, n_case), j, 0, 0))],
        out_block=(tq, LANE), out_index=lambda j, i: (i, BRANCH_WIDTH // LANE + j),
        sem=("parallel", "arbitrary"), name="band_prompt", args=(z, zb, zb, bias))


def _band_sample_kernel(q_ref, kn_ref, vn_ref, kct_ref, vct_ref, bc_ref, bn_ref, o_ref):
    kct, vct = kct_ref[...].astype(BF16), vct_ref[...].astype(BF16)
    kn, vn = kn_ref[...], vn_ref[...]
    qs = q_ref[...] * (HEAD_DIM ** -0.5)
    outs = []
    for e in range(2):
        qe = jnp.where(_half_mask(qs.shape, e), qs, 0.0).astype(BF16)
        s_c = jnp.dot(qe, kct, preferred_element_type=F32) + bc_ref[e]
        s_n = _dot_nt(qe, kn) + bn_ref[e]
        m = jnp.maximum(jnp.max(s_c, axis=-1, keepdims=True), jnp.max(s_n, axis=-1, keepdims=True))
        p_c, p_n = jnp.exp(s_c - m), jnp.exp(s_n - m)
        l = jnp.sum(p_c, axis=-1, keepdims=True) + jnp.sum(p_n, axis=-1, keepdims=True)
        o = _dot_nt(p_c.astype(BF16), vct) + jnp.dot(p_n.astype(BF16), vn, preferred_element_type=F32)
        outs.append(o / l)
    o_ref[...] = jnp.where(_half_mask(outs[0].shape, 0), outs[0], outs[1]).astype(o_ref.dtype)


def _band_sample(z, zb, cache_k, cache_v, bias_c, bias_n, l, t, nb, ts, o_all, m):
    pb = cache_k.shape[2]
    rb = t // ts
    cache = pl.BlockSpec((None, None, LANE, pb), lambda b, j: (l, b, j, 0))
    return _mixer_call(
        _band_sample_kernel, o_all, m,
        grid=(nb, B_HEADS // 2),
        in_specs=[pl.BlockSpec((ts, LANE), lambda b, j: (rb + b, S_BQ + j)),
                  pl.BlockSpec((ts, LANE), lambda b, j: (rb + b, S_BK + j)),
                  pl.BlockSpec((ts, LANE), lambda b, j: (rb + b, S_BV + j)),
                  cache, cache,
                  pl.BlockSpec((2, ts, pb), lambda b, j: (j, 0, 0)),
                  pl.BlockSpec((2, ts, ts), lambda b, j: (j, 0, 0))],
        out_block=(ts, LANE), out_index=lambda b, j: (rb + b, BRANCH_WIDTH // LANE + j),
        sem=("parallel", "arbitrary"), name="band_sample",
        args=(z, zb, zb, _feature_major(cache_k, BRANCH_WIDTH), _feature_major(cache_v, BRANCH_WIDTH),
              bias_c, bias_n))


def _band_bias(table, tq):
    n_case = B_WINDOW // tq
    win = B_WINDOW + tq
    width = win + B_WINDOW
    j = np.arange(width + tq - 1) - (tq - 1) - B_WINDOW
    ext = table.astype(F32)[:, np.clip(j, -B_REL_CLIP, B_REL_CLIP) + B_REL_CLIP]
    toe = jnp.stack([ext[:, tq - 1 - r:tq - 1 - r + width] for r in range(tq)], axis=1)
    return jnp.stack([toe[:, :, B_WINDOW - c * tq:B_WINDOW - c * tq + win] for c in range(n_case + 1)], axis=0)


def _lambda(lam_ref, lam_init):
    lv = lam_ref[...]
    return (jnp.exp(jnp.sum(lv[0:1] * lv[1:2], axis=-1, keepdims=True))
            - jnp.exp(jnp.sum(lv[2:3] * lv[3:4], axis=-1, keepdims=True)) + lam_init)


def _diff_finish(o0, o1, lam, on_ref, lam_init):
    attn = o0 - lam * o1
    ms = jnp.mean(attn * attn, axis=-1, keepdims=True)
    return (attn * lax.rsqrt(ms + NORM_EPS) * on_ref[...]) * (1.0 - lam_init)


def _diff_prompt_kernel(q_ref, k_ref, v_ref, lam_ref, on_ref, o_ref, qh_ref, m_ref, acc_ref, s_ref, p_ref,
                        *, tq, kb, tg, lam_init):
    i = pl.program_id(1)
    q0 = i * tq
    n_grp = tq // tg
    n_blk = (q0 + tq + kb - 1) // kb
    row = lax.broadcasted_iota(jnp.int32, (tg, kb), 0)
    limits = [(((q0 + g * tg + row) >> 6) + 1) << 6 for g in range(n_grp)]
    for g in range(n_grp):
        qs = q_ref[g * tg:(g + 1) * tg, :] * (HEAD_DIM ** -0.5)
        for c in range(2):
            qh_ref[2 * g + c] = jnp.where(_half_mask(qs.shape, c), qs, 0.0).astype(BF16)
    _softmax_init(m_ref, acc_ref)
    n_full = (q0 + CHUNK) // kb

    def attn_blk(b, carry, masked):
        start = pl.multiple_of(b * kb, kb)
        ok = None
        if masked:
            kpos = start + lax.broadcasted_iota(jnp.int32, (tg, kb), 1)
            ok = [kpos < limits[u // 2] for u in range(2 * n_grp)]
        _softmax_block(ok, k_ref[pl.ds(start, kb), :], v_ref[pl.ds(start, kb), :],
                       [qh_ref[u] for u in range(2 * n_grp)], m_ref, acc_ref, s_ref, p_ref)
        return carry

    lax.fori_loop(0, n_full, functools.partial(attn_blk, masked=False), 0)
    lax.fori_loop(n_full, n_blk, functools.partial(attn_blk, masked=True), 0)
    lam = _lambda(lam_ref, lam_init)
    for g in range(n_grp):
        outs = [_softmax_result(acc_ref, 2 * g + c) for c in range(2)]
        o_ref[g * tg:(g + 1) * tg, :] = _diff_finish(outs[0], outs[1], lam, on_ref, lam_init).astype(o_ref.dtype)


def _diff_prompt(z, zb, c_lam_all, c_on_all, l, t, lam_init, o_all, m):
    tq, kb, tg = _tile(t, 512), 512, 128
    maps = lambda dt, w=LANE: pltpu.VMEM((2 * tq // tg, tg, w), dt)
    return _mixer_call(
        functools.partial(_diff_prompt_kernel, tq=tq, kb=kb, tg=tg, lam_init=lam_init), o_all, m,
        grid=(C_HEADS, t // tq),
        in_specs=[pl.BlockSpec((tq, LANE), lambda h, i: (i, S_CQ + h)),
                  pl.BlockSpec((t, LANE), lambda h, i: (0, S_CK + h)),
                  pl.BlockSpec((t, LANE), lambda h, i: (0, S_CV + h)),
                  pl.BlockSpec((None, 4, HEAD_DIM), lambda h, i: (l, 0, 0)),
                  _layer_vec(l, LANE)],
        out_block=(tq, LANE), out_index=lambda h, i: (i, 2 * BRANCH_WIDTH // LANE + h),
        scratch=[maps(BF16), maps(F32), maps(F32, 2 * LANE), maps(F32, kb), maps(BF16, kb)],
        sem=("parallel", "arbitrary"), name="diff_prompt",
        args=(z, zb, zb, c_lam_all, _vec3(c_on_all)))


def _diff_sample_kernel(q_ref, kn_ref, vn_ref, kct_ref, vc_ref, lam_ref, on_ref, o_ref, *, past, lam_init):
    kct = kct_ref[...].astype(BF16)
    vc = vc_ref[pl.ds(pl.program_id(1), past, stride=C_HEADS), :].astype(BF16)
    kn, vn = kn_ref[...], vn_ref[...]
    qs = q_ref[...] * (HEAD_DIM ** -0.5)
    outs = []
    for c in range(2):
        qc = jnp.where(_half_mask(qs.shape, c), qs, 0.0).astype(BF16)
        s_c = jnp.dot(qc, kct, preferred_element_type=F32)
        s_n = _dot_nt(qc, kn)
        m = jnp.maximum(jnp.max(s_c, axis=-1, keepdims=True), jnp.max(s_n, axis=-1, keepdims=True))
        p_c, p_n = jnp.exp(s_c - m), jnp.exp(s_n - m)
        l = jnp.sum(p_c, axis=-1, keepdims=True) + jnp.sum(p_n, axis=-1, keepdims=True)
        o = (jnp.dot(p_c.astype(BF16), vc, preferred_element_type=F32)
             + jnp.dot(p_n.astype(BF16), vn, preferred_element_type=F32))
        outs.append(o / l)
    lam = _lambda(lam_ref, lam_init)
    o_ref[...] = _diff_finish(outs[0], outs[1], lam, on_ref, lam_init).astype(o_ref.dtype)


def _diff_sample(z, zb, cache_k, cache_v, c_lam_all, c_on_all, l, t, nb, ts, lam_init, o_all, m):
    depth, _, past = cache_k.shape[:3]
    rb = t // ts
    return _mixer_call(
        functools.partial(_diff_sample_kernel, past=past, lam_init=lam_init), o_all, m,
        grid=(nb, C_HEADS),
        in_specs=[pl.BlockSpec((ts, LANE), lambda b, h: (rb + b, S_CQ + h)),
                  pl.BlockSpec((ts, LANE), lambda b, h: (rb + b, S_CK + h)),
                  pl.BlockSpec((ts, LANE), lambda b, h: (rb + b, S_CV + h)),
                  pl.BlockSpec((None, None, LANE, past), lambda b, h: (l, b, h, 0)),
                  pl.BlockSpec((None, None, past * C_HEADS, LANE), lambda b, h: (l, b, 0, 0)),
                  pl.BlockSpec((None, 4, HEAD_DIM), lambda b, h: (l, 0, 0)),
                  _layer_vec(l, LANE)],
        out_block=(ts, LANE), out_index=lambda b, h: (rb + b, 2 * BRANCH_WIDTH // LANE + h),
        sem=("parallel", "arbitrary"), name="diff_sample",
        args=(z, zb, zb, _feature_major(cache_k, BRANCH_WIDTH),
              cache_v.reshape(depth, nb, past * C_HEADS, LANE), c_lam_all, _vec3(c_on_all)))


def _rwkv_pre_kernel(zd_ref, prev_ref, first_ref, mu_ref, w0_ref, w2_ref, a0_ref, a2_ref, g2_ref, kkw_ref,
                     ka_ref, bd_ref, r_ref, w_ref, k_ref, v_ref, kk_ref, b_ref, g_ref, *, tm, t, ts):
    zf = zd_ref[...]
    grow = pl.program_id(0) * tm + lax.broadcasted_iota(jnp.int32, (tm, 1), 0)
    zs = jnp.where(grow % tm == 0, prev_ref[7:8, :], pltpu.roll(zf, 1, 0))
    seq_start = (grow >= t) & ((grow - t) % ts == 0)
    zs = jnp.where(seq_start, first_ref[...], zs)
    zs = jnp.where(grow == 0, 0.0, zs)
    zm = zf + (zs - zf) * mu_ref[...]
    bw = BRANCH_WIDTH
    r, k, v = zm[:, :bw], zm[:, bw:2 * bw], zm[:, 2 * bw:3 * bw]
    wl, al, gl = zm[:, 3 * bw:3 * bw + 64], zm[:, 3 * bw + 64:3 * bw + 128], zm[:, 3 * bw + 128:]
    dot = lambda a, b: jnp.dot(a.astype(BF16), b.astype(BF16), preferred_element_type=F32)
    u = -(w0_ref[...] + dot(jnp.tanh(wl), w2_ref[...]))
    softplus = jnp.maximum(u, 0.0) + jnp.log(1.0 + jnp.exp(-jnp.abs(u)))
    w = -softplus - 0.5
    a = jax.nn.sigmoid(a0_ref[...] + dot(al, a2_ref[...]))
    kk = k * kkw_ref[...]
    nrm = jnp.sqrt(_seg_sum(kk * kk, bd_ref[...]))
    kk = kk / jnp.maximum(nrm, 1e-12)
    r_ref[...] = r
    w_ref[...] = -jnp.exp(w)
    k_ref[...] = k * (1.0 + (a - 1.0) * ka_ref[...])
    v_ref[...] = v
    kk_ref[...] = kk
    b_ref[...] = kk * a
    g_ref[...] = dot(jax.nn.sigmoid(gl), g2_ref[...])


def _rwkv_pre(zd, zfirst, p, l, t, ts):
    m, dc = zd.shape
    tm = _tile(math.gcd(t, m - t), 512)
    bw = BRANCH_WIDTH
    npt = t // tm
    rows = pl.BlockSpec((tm, dc), lambda i: (i, 0))
    vec = lambda n: _layer_vec(l, n)
    mat = lambda k: pl.BlockSpec((None, k, bw), lambda i: (l, 0, 0))
    out = jax.ShapeDtypeStruct((m, bw), F32)
    return pl.pallas_call(
        functools.partial(_rwkv_pre_kernel, tm=tm, t=t, ts=ts),
        out_shape=[out] * 7,
        grid=(m // tm,),
        in_specs=[rows,
                  pl.BlockSpec((8, dc), lambda i: (jnp.maximum(i * (tm // 8) - 1, 0), 0)),
                  pl.BlockSpec((tm, dc), lambda i: (jnp.maximum(i - npt, 0), 0)),
                  vec(dc), vec(bw), mat(64), vec(bw), mat(64), mat(128), vec(bw), vec(bw),
                  pl.BlockSpec((bw, bw), lambda i: (0, 0))],
        out_specs=[pl.BlockSpec((tm, bw), lambda i: (i, 0))] * 7,
        compiler_params=_cparams("parallel"),
        name="rwkv_pre",
    )(zd, zd, zfirst, _vec3(p["d_mu"]), _vec3(p["d_w0"]), p["d_w2"], _vec3(p["d_a0"]), p["d_a2"], p["d_g2"],
      _vec3(p["d_k_k"]), _vec3(p["d_k_a"]), _ones_blockdiag(bw))


RW_CHUNK = 16
RW_BLOCK = 128
RW_SLOTS = BRANCH_WIDTH // LANE


def _dot3(a, b):
    ah, al = _split2(a)
    bh, bl = _split2(b)
    d = lambda x, y: jnp.dot(x, y, preferred_element_type=F32)
    return d(ah, bh) + d(al, bh) + d(ah, bl)


def _rwkv_chunk_kernel(*refs, seq_chunks):
    if seq_chunks:
        (r_ref, lw_ref, k_ref, v_ref, kk_ref, b_ref, s0_ref, y_ref, sf_ref,
         h_ref, u_ref, ab_ref, rb_ref, u0_ref, y0_ref, bt_ref, kt_ref, eg_ref) = refs
    else:
        (r_ref, lw_ref, k_ref, v_ref, kk_ref, b_ref, y_ref, sf_ref,
         h_ref, u_ref, ab_ref, rb_ref, u0_ref, y0_ref, bt_ref, kt_ref, eg_ref) = refs
        s0_ref = None

        @pl.when(pl.program_id(0) == 0)
        def _():
            h_ref[...] = jnp.zeros_like(h_ref)

    n = RW_BLOCK
    n_chunks = n // RW_CHUNK
    row = lax.broadcasted_iota(jnp.int32, (n, n), 0)
    col = lax.broadcasted_iota(jnp.int32, (n, n), 1)
    same = (row // RW_CHUNK) == (col // RW_CHUNK)
    strict, incl = same & (col < row), same & (col <= row)
    eye = jnp.where(row == col, 1.0, 0.0)
    head_diag = (row // HEAD_DIM) == (col // HEAD_DIM)
    in_chunk = row % RW_CHUNK
    k_pick = jnp.where(lax.broadcasted_iota(jnp.int32, (n, HEAD_DIM), 0) % HEAD_DIM
                       == lax.broadcasted_iota(jnp.int32, (n, HEAD_DIM), 1), 1.0, 0.0)
    bf = lambda x: x.astype(BF16)
    mm = lambda x, y: jnp.dot(bf(x), bf(y), preferred_element_type=F32)
    u_ref[...] = jnp.zeros_like(u_ref)

    halves = [_half_mask((n, LANE), e) for e in range(2)]
    group = 2
    for j0 in range(0, RW_SLOTS, group):
        slot_data = []
        for j in range(j0, j0 + group):
            sl = slice(j * LANE, (j + 1) * LANE)
            lw, v = lw_ref[:, sl], v_ref[:, sl]
            g = lw
            for d in (1, 2, 4, 8):
                g = g + jnp.where(in_chunk >= d, pltpu.roll(g, d, 0), 0.0)
            inv_g = jnp.exp(-g)
            a_t = -kk_ref[:, sl] * jnp.exp(g - lw)
            b_t, k_t = b_ref[:, sl] * inv_g, k_ref[:, sl] * inv_g
            r_t = r_ref[:, sl] * jnp.exp(g)
            bt_ref[j], kt_ref[j], eg_ref[j] = b_t.T, k_t.T, jnp.exp(g).T
            slot_data.append((a_t, r_t, bf(b_t), bf(k_t), bf(v)))
        chains = [(s, e) for s in range(group) for e in range(2)]
        each = lambda f: [f(s, e, i) for i, (s, e) in enumerate(chains)]
        a_e = each(lambda s, e, i: bf(jnp.where(halves[e], slot_data[s][0], 0.0)))
        r_e = each(lambda s, e, i: bf(jnp.where(halves[e], slot_data[s][1], 0.0)))
        n_ab = each(lambda s, e, i: jnp.where(strict, _dot_nt(a_e[i], slot_data[s][2]), 0.0))
        n_ak = each(lambda s, e, i: jnp.where(strict, _dot_nt(a_e[i], slot_data[s][3]), 0.0))
        m_rb = each(lambda s, e, i: bf(jnp.where(incl, _dot_nt(r_e[i], slot_data[s][2]), 0.0)))
        m_rk = each(lambda s, e, i: bf(jnp.where(incl, _dot_nt(r_e[i], slot_data[s][3]), 0.0)))
        w_e = each(lambda s, e, i: mm(n_ak[i], slot_data[s][4]))
        n2 = each(lambda s, e, i: mm(n_ab[i], n_ab[i]))
        n4 = each(lambda s, e, i: mm(n2[i], n2[i]))
        n8 = each(lambda s, e, i: mm(n4[i], n4[i]))
        tinv = each(lambda s, e, i: eye + n_ab[i])
        for pw in (n2, n4, n8):
            tinv = each(lambda s, e, i: tinv[i] + mm(tinv[i], pw[i]))
        t16 = each(lambda s, e, i: bf(tinv[i]))
        a_bar = each(lambda s, e, i: mm(t16[i], slot_data[s][0]))
        u0 = each(lambda s, e, i: mm(t16[i], w_e[i]))
        r_bar = each(lambda s, e, i: slot_data[s][1] + mm(m_rb[i], a_bar[i]))
        y0 = each(lambda s, e, i: mm(m_rb[i], u0[i]) + mm(m_rk[i], slot_data[s][4]))
        for s in range(group):
            pick = lambda vals: jnp.where(halves[0], vals[2 * s], vals[2 * s + 1])
            j = j0 + s
            ab_ref[j], rb_ref[j], u0_ref[j], y0_ref[j] = pick(a_bar), pick(r_bar), pick(u0), pick(y0)

    for c in range(n_chunks):
        rows = slice(c * RW_CHUNK, (c + 1) * RW_CHUNK)
        col_c = (col // RW_CHUNK) == c
        for j in range(RW_SLOTS):
            sl = slice(j * LANE, (j + 1) * LANE)
            if seq_chunks and c % seq_chunks == 0:
                x = s0_ref[c // seq_chunks, sl, :]
                h = jnp.where(head_diag, _dot_nt3(k_pick, x), 0.0)
            else:
                h = h_ref[j]
            res = _dot3(jnp.concatenate([ab_ref[j, rows, :], rb_ref[j, rows, :]], axis=0), h)
            u_c = res[:RW_CHUNK] + u0_ref[j, rows, :]
            y_ref[rows, sl] = res[RW_CHUNK:] + y0_ref[j, rows, :]
            u_ref[j, rows, :] = u_c
            bk = jnp.concatenate([jnp.where(col_c, bt_ref[j], 0.0), jnp.where(col_c, kt_ref[j], 0.0)], axis=1)
            uv = jnp.concatenate([u_ref[j], v_ref[:, sl]], axis=0)
            inc = jnp.where(head_diag, mm(bk, uv), 0.0)
            g_end = eg_ref[j, :, (c + 1) * RW_CHUNK - 1:(c + 1) * RW_CHUNK]
            h = g_end * (h + inc)
            h_ref[j] = h
            if seq_chunks and (c + 1) % seq_chunks == 0:
                ht = h.T
                sf_ref[c // seq_chunks, sl, :] = (ht + pltpu.roll(ht, HEAD_DIM, 1))[:, :HEAD_DIM]

    if not seq_chunks:
        @pl.when(pl.program_id(0) == pl.num_programs(0) - 1)
        def _():
            for j in range(RW_SLOTS):
                ht = h_ref[j].T
                sf_ref[0, j * LANE:(j + 1) * LANE, :] = (ht + pltpu.roll(ht, HEAD_DIM, 1))[:, :HEAD_DIM]


def _rwkv_chunked(ops, s0, row0, n_seq, t):
    n = RW_BLOCK
    bw = BRANCH_WIDTH
    assert row0 % n == 0 and (n_seq * t) % n == 0 and t % RW_CHUNK == 0
    rb = row0 // n
    rows = pl.BlockSpec((n, bw), lambda i: (rb + i, 0))
    if s0 is None:
        assert n_seq == 1
        seq_chunks, per_blk, extra, extra_specs = 0, 1, [], []
        sf_spec = pl.BlockSpec((1, bw, HEAD_DIM), lambda i: (0, 0, 0))
    else:
        assert n % t == 0
        seq_chunks, per_blk = t // RW_CHUNK, n // t
        extra = [s0.reshape(n_seq, bw, HEAD_DIM)]
        extra_specs = [pl.BlockSpec((per_blk, bw, HEAD_DIM), lambda i: (i, 0, 0))]
        sf_spec = pl.BlockSpec((per_blk, bw, HEAD_DIM), lambda i: (i, 0, 0))
    slot = lambda dt=F32: pltpu.VMEM((RW_SLOTS, n, LANE), dt)
    y, sf = pl.pallas_call(
        functools.partial(_rwkv_chunk_kernel, seq_chunks=seq_chunks),
        out_shape=[jax.ShapeDtypeStruct((n_seq * t, bw), F32), jax.ShapeDtypeStruct((n_seq, bw, HEAD_DIM), F32)],
        grid=(n_seq * t // n,),
        in_specs=[rows] * 6 + extra_specs,
        out_specs=[pl.BlockSpec((n, bw), lambda i: (i, 0)), sf_spec],
        scratch_shapes=[slot() for _ in range(9)],
        compiler_params=_cparams("arbitrary"),
        name="rwkv_chunked",
    )(*ops, *extra)
    return y, sf.reshape(n_seq, D_HEADS, HEAD_DIM, HEAD_DIM)


def _rwkv_post_kernel(y_ref, r_ref, k_ref, v_ref, g_ref, lnw_ref, lnb_ref, rk_ref, bd_ref, o_ref):
    bd = bd_ref[...]
    y = y_ref[...]
    mean = _seg_sum(y, bd) * (1.0 / HEAD_DIM)
    yc = y - mean
    var = _seg_sum(yc * yc, bd) * (1.0 / HEAD_DIM)
    yn = yc * lax.rsqrt(var + D_GN_EPS) * lnw_ref[...] + lnb_ref[...]
    bonus = _seg_sum(r_ref[...] * k_ref[...] * rk_ref[...], bd) * v_ref[...]
    o_ref[...] = ((yn + bonus) * g_ref[...]).astype(o_ref.dtype)


def _rwkv_post(y, r, k, v, g, p, l, o_all):
    m, bw = y.shape
    tm = _tile(m, 512)
    rows = pl.BlockSpec((tm, bw), lambda i: (i, 0))
    vec = _layer_vec(l, bw)
    return _mixer_call(
        _rwkv_post_kernel, o_all, m,
        grid=(m // tm,),
        in_specs=[rows] * 5 + [vec] * 3 + [pl.BlockSpec((bw, bw), lambda i: (0, 0))],
        out_block=(tm, bw), out_index=lambda i: (i, N_BRANCH - 1),
        sem=("parallel",), name="rwkv_post",
        args=(y, r, k, v, g, _vec3(p["d_ln_w"]), _vec3(p["d_ln_b"]), _vec3(p["d_r_k"]), _ones_blockdiag(bw)))


def _merge_kernel(o_ref, wbr_ref, h_ref, wg_ref, out_ref, acc_ref):
    n = pl.program_id(2)

    @pl.when(n == 0)
    def _():
        acc_ref[...] = jnp.zeros_like(acc_ref)

    u = jnp.dot(o_ref[...], wbr_ref[...].astype(BF16), preferred_element_type=F32)
    gate = jax.nn.sigmoid(jnp.dot(h_ref[...], wg_ref[...], preferred_element_type=F32))
    acc_ref[...] += u * gate

    @pl.when(n == N_BRANCH - 1)
    def _():
        out_ref[...] = acc_ref[...].astype(out_ref.dtype)


def _merge(o_all, w_br_all, h, w_pack, l):
    m, d = h.shape
    tm, tn = _tile(m, 1024), 512
    nj = d // tn
    g0 = S_GATE * LANE // tn
    return pl.pallas_call(
        _merge_kernel,
        out_shape=jax.ShapeDtypeStruct((m, d), BF16),
        grid=(m // tm, nj, N_BRANCH),
        in_specs=[pl.BlockSpec((tm, BRANCH_WIDTH), lambda i, j, n: (i, n)),
                  pl.BlockSpec((None, None, BRANCH_WIDTH, tn), lambda i, j, n: (l, n, 0, j)),
                  pl.BlockSpec((tm, d), lambda i, j, n: (i, 0)),
                  pl.BlockSpec((None, d, tn), lambda i, j, n: (l, 0, g0 + n * nj + j))],
        out_specs=pl.BlockSpec((tm, tn), lambda i, j, n: (i, j)),
        scratch_shapes=[pltpu.VMEM((tm, tn), F32)],
        compiler_params=_cparams("parallel", "parallel", "arbitrary"),
        name="branch_merge",
    )(o_all, w_br_all, h, w_pack)


def _ple_kernel(x_ref, pe_ref, wple_ref, h_ref, wg_ref, o_ref):
    emb = jnp.dot(pe_ref[...].astype(BF16), wple_ref[...].astype(BF16), preferred_element_type=F32)
    gate = jax.nn.sigmoid(jnp.dot(h_ref[...], wg_ref[...].astype(BF16), preferred_element_type=F32))
    o_ref[...] = x_ref[...] + emb * gate


def _ple(x, pe, w_ple_all, h, w_gate_all, l):
    m, d = x.shape
    pd = pe.shape[1]
    tm, tn = _tile(m, 1024), 512
    return pl.pallas_call(
        _ple_kernel,
        out_shape=jax.ShapeDtypeStruct((m, d), F32),
        grid=(m // tm, d // tn),
        in_specs=[pl.BlockSpec((tm, tn), lambda i, j: (i, j)),
                  pl.BlockSpec((tm, pd), lambda i, j: (i, 0)),
                  pl.BlockSpec((None, pd, tn), lambda i, j: (l, 0, j)),
                  pl.BlockSpec((tm, d), lambda i, j: (i, 0)),
                  pl.BlockSpec((None, d, tn), lambda i, j: (l, 0, j))],
        out_specs=pl.BlockSpec((tm, tn), lambda i, j: (i, j)),
        compiler_params=_cparams("parallel", "parallel"),
        name="ple",
    )(x, pe, w_ple_all, h, w_gate_all)


def _layer(l, x, pe, p, t, nb, ts, tabs):
    m = x.shape[0]
    lam_init = 0.8 - 0.6 * math.exp(-0.3 * l)
    dc = p["d_mu"].shape[1]
    d = x.shape[1]
    lw = {"a_qn": p["a_q_norm"][l], "a_kn": p["a_k_norm"][l], "b_qn": p["b_q_norm"][l], "b_kn": p["b_k_norm"][l],
          "c_qn": p["c_q_norm"][l], "c_kn": p["c_k_norm"][l]}

    h = _rmsnorm(x, p["norm1_g"], l)
    w_pack = p["w_pack"]
    z, zb = _qkv_projection(h, w_pack, l, *_column_vectors(lw), *tabs)
    zd = _matmul(h, w_pack, lambda acc: acc, F32, n=dc, layer=l, col0=S_D * LANE, tn=256, name="rwkv_projection")

    o_all = _dsa_prompt(z, zb, t, None, m)
    o_all = _dsa_sample(z, zb, p["cache_a_k"], p["cache_a_v"], p["cache_a_kidx"], l, t, nb, ts, o_all, m)
    tq_b = 128
    bias = _band_bias(p["b_rel_bias"][l], tq_b)
    pb = p["cache_b_k"].shape[2]
    generic = bias[B_WINDOW // tq_b]
    o_all = _band_prompt(z, zb, bias, t, tq_b, o_all, m)
    o_all = _band_sample(z, zb, p["cache_b_k"], p["cache_b_v"], generic[:, :ts, B_WINDOW - pb:B_WINDOW],
                         generic[:, :ts, B_WINDOW:B_WINDOW + ts], l, t, nb, ts, o_all, m)
    c_on = p["c_out_norm"]
    o_all = _diff_prompt(z, zb, p["c_lambda"], c_on, l, t, lam_init, o_all, m)
    o_all = _diff_sample(z, zb, p["cache_c_k"], p["cache_c_v"], p["c_lambda"], c_on, l, t, nb, ts, lam_init,
                         o_all, m)

    zfirst = jnp.broadcast_to(p["state_d_shift"][l], (nb, ts, dc)).reshape(nb * ts, dc)
    r, w, k, v, kk, b, g = _rwkv_pre(zd, zfirst, p, l, t, ts)
    ops = (r, w, k, v, kk, b)
    y_p, wkv_p = _rwkv_chunked(ops, None, 0, 1, t)
    y_s, wkv_s = _rwkv_chunked(ops, p["state_d_wkv"][l], t, nb, ts)
    o_all = _rwkv_post(jnp.concatenate([y_p, y_s], axis=0), r, k, v, g, p, l, o_all)

    ug = _merge(o_all, p["w_branch"], h, w_pack, l)
    res = lambda acc, r_: r_ + acc
    x = _matmul(ug, p["w_out"], res, F32, n=d, layer=l, residual=x, name="out_proj")
    h2 = _rmsnorm(x, p["norm2_g"], l)
    ffn = p["w_up"].shape[2]
    up = _matmul(h2, p["w_up"], lambda acc: jnp.square(jnp.maximum(acc, 0.0)), BF16, n=ffn, layer=l, name="mlp_up")
    x = _matmul(up, _cast_bf16(p["w_down"], l), res, F32, n=d, residual=x, tn=1024, name="mlp_down")
    h3 = _rmsnorm(x, p["norm3_g"], l)
    x = _ple(x, pe, p["w_ple"], h3, p["w_ple_gate"], l)

    slot = lambda rows, s, n=1: z[rows, s * LANE:(s + n) * LANE]

    def rows_of(sl, lead):
        a = lambda s, n, shape: slot(sl, s, n).reshape(lead + shape)
        ak = a(S_AK, 1, (A_KV_HEADS, HEAD_DIM))
        av = a(S_AV, 1, (A_KV_HEADS, HEAD_DIM))
        aik = slot(sl, S_AIK)[:, :HEAD_DIM].reshape(lead + (HEAD_DIM,))
        bk = a(S_BK, 4, (B_HEADS, HEAD_DIM))
        bv = a(S_BV, 4, (B_HEADS, HEAD_DIM))
        ck = a(S_CK, 4, (C_HEADS, 2, HEAD_DIM))
        cv = a(S_CV, 4, (C_HEADS, 2 * HEAD_DIM))
        return ak, av, aik, bk, bv, ck, cv

    keep = min(B_WINDOW, t)
    pak, pav, paik, pbk, pbv, pck, pcv = rows_of(slice(0, t), (1, t))
    new_p = (pak, pav, paik, pbk[:, t - keep:], pbv[:, t - keep:], pck, pcv, wkv_p, zd[t - 1:t].reshape(1, 1, dc))
    new_s = rows_of(slice(t, m), (nb, ts)) + (wkv_s, zd[t:].reshape(nb, ts, dc)[:, -1:])
    return x, new_p, new_s


def kernel(x_prompt, x_sample, cache_a_k, cache_a_v, cache_a_kidx, cache_b_k, cache_b_v, cache_c_k, cache_c_v, state_d_wkv, state_d_shift, p_prompt, p_sample, norm1_g, w_in, a_q_norm, a_k_norm, b_q_norm, b_k_norm, b_rel_bias, c_q_norm, c_k_norm, c_lambda, c_out_norm, d_mu, d_w0, d_w2, d_a0, d_a2, d_g2, d_k_k, d_k_a, d_r_k, d_ln_w, d_ln_b, w_branch, w_out, norm2_g, w_up, w_down, norm3_g, w_ple, w_ple_gate):
    batch, t, d = x_prompt.shape
    nb, ts, _ = x_sample.shape
    past = cache_a_k.shape[2]
    depth = w_in.shape[0]
    assert batch == 1 and t % 512 == 0 and past % CHUNK == 0 and ts <= CHUNK and (nb * ts) % 8 == 0
    p = dict(cache_a_k=cache_a_k, cache_a_v=cache_a_v, cache_a_kidx=cache_a_kidx, cache_b_k=cache_b_k,
             cache_b_v=cache_b_v, cache_c_k=cache_c_k, cache_c_v=cache_c_v, state_d_wkv=state_d_wkv,
             state_d_shift=state_d_shift, norm1_g=norm1_g, w_in=w_in, a_q_norm=a_q_norm, a_k_norm=a_k_norm,
             b_q_norm=b_q_norm, b_k_norm=b_k_norm, b_rel_bias=b_rel_bias, c_q_norm=c_q_norm, c_k_norm=c_k_norm,
             c_lambda=c_lambda, c_out_norm=c_out_norm, d_mu=d_mu, d_w0=d_w0, d_w2=d_w2, d_a0=d_a0, d_a2=d_a2,
             d_g2=d_g2, d_k_k=d_k_k, d_k_a=d_k_a, d_r_k=d_r_k, d_ln_w=d_ln_w, d_ln_b=d_ln_b, w_branch=w_branch,
             w_out=w_out, norm2_g=norm2_g, w_up=w_up, w_down=w_down, norm3_g=norm3_g, w_ple=w_ple,
             w_ple_gate=w_ple_gate)
    p["w_pack"] = _pack_w_in(w_in)
    x = jnp.concatenate([x_prompt[0], x_sample.reshape(nb * ts, d)], axis=0)
    pos = jnp.concatenate([jnp.arange(t, dtype=jnp.int32),
                           jnp.tile(past + jnp.arange(ts, dtype=jnp.int32), nb)])
    tabs = _rope_tables(pos)
    st_p = [[] for _ in range(9)]
    st_s = [[] for _ in range(9)]
    for l in range(depth):
        pe = jnp.concatenate([p_prompt[l, 0], p_sample[l].reshape(nb * ts, -1)], axis=0)
        x, new_p, new_s = _layer(l, x, pe, p, t, nb, ts, tabs)
        for lst, arr in zip(st_p, new_p):
            lst.append(arr)
        for lst, arr in zip(st_s, new_s):
            lst.append(arr)
    outs_p = [jnp.stack(s, axis=0) for s in st_p]
    outs_s = [jnp.stack(s, axis=0) for s in st_s]
    return (x[:t].reshape(1, t, d), x[t:].reshape(nb, ts, d), *outs_p, *outs_s)
```

```python
import functools
import math

import numpy as np
import jax
import jax.numpy as jnp
from jax import lax
from jax.experimental import pallas as pl
from jax.experimental.pallas import tpu as pltpu

F32 = jnp.float32
BF16 = jnp.bfloat16

CHUNK = 64
HEAD_DIM = 64
ROPE_DIM = 16
ROPE_THETA = 500000.0
N_BRANCH = 4
BRANCH_WIDTH = 512
A_HEADS, A_KV_HEADS, A_IDX_HEADS = 8, 2, 4
A_TOPK_MAX = 256
B_HEADS = 8
B_PAST_CHUNKS = 8
B_WINDOW = B_PAST_CHUNKS * CHUNK
B_REL_CLIP = 128
C_HEADS = 4
D_HEADS = 8
D_GN_EPS = 64e-5
NORM_EPS = 1e-6

LANE = 128
VMEM_LIMIT = 48 * 1024 * 1024

S_AQ, S_AK, S_AV, S_AIQ, S_AIK = 0, 4, 5, 6, 8
S_BQ, S_BK, S_BV = 10, 14, 18
S_CQ, S_CK, S_CV = 22, 26, 30
S_D, S_GATE = 34, 48
N_QKV_SLOTS = 34
N_D_SLOTS = 14
A_COLS = 1092
A_SLOTS = 9
SHIFT = A_COLS - (A_SLOTS - 1) * LANE

NEG_KEY = -2139095041
INT_MIN = -2147483648
M_FLOOR = -1e30


def _cparams(*sem):
    return pltpu.CompilerParams(dimension_semantics=sem, vmem_limit_bytes=VMEM_LIMIT)


def _tile(n, pref):
    t = min(n, pref)
    while n % t:
        t -= 8
    return t


def _split2(x):
    hi = x.astype(BF16)
    lo = (x - hi.astype(F32)).astype(BF16)
    return hi, lo


def _seg_sum(x, ones_bd):
    hi, lo = _split2(x)
    return (jnp.dot(hi, ones_bd, preferred_element_type=F32)
            + jnp.dot(lo, ones_bd, preferred_element_type=F32))


def _dot_nt(a, b):
    return lax.dot_general(a, b, (((1,), (1,)), ((), ())), preferred_element_type=F32)


def _dot_nt3(a, b):
    ah, al = _split2(a)
    bh, bl = _split2(b)
    return _dot_nt(ah, bh) + _dot_nt(ah, bl) + _dot_nt(al, bh)


def _ones_blockdiag(n):
    i = np.arange(n)
    return jnp.asarray((i[:, None] // HEAD_DIM) == (i[None, :] // HEAD_DIM), dtype=BF16)


def _vec3(a):
    return a.reshape(a.shape[0], 1, -1)


def _layer_vec(l, n):
    return pl.BlockSpec((None, 1, n), lambda *_: (l, 0, 0))


def _half_mask(shape, half):
    lane = lax.broadcasted_iota(jnp.int32, shape, len(shape) - 1)
    return (lane < HEAD_DIM) if half == 0 else (lane >= HEAD_DIM)


def _rms_kernel(x_ref, g_ref, o_ref):
    x = x_ref[...]
    ms = jnp.mean(x * x, axis=-1, keepdims=True)
    o_ref[...] = (x * lax.rsqrt(ms + NORM_EPS) * g_ref[...]).astype(o_ref.dtype)


def _rmsnorm(x, g_all, l):
    m, d = x.shape
    tm = _tile(m, 512)
    return pl.pallas_call(
        _rms_kernel,
        out_shape=jax.ShapeDtypeStruct((m, d), BF16),
        grid=(m // tm,),
        in_specs=[pl.BlockSpec((tm, d), lambda i: (i, 0)), _layer_vec(l, d)],
        out_specs=pl.BlockSpec((tm, d), lambda i: (i, 0)),
        compiler_params=_cparams("parallel"),
        name="rmsnorm",
    )(x, _vec3(g_all))


def _pack_kernel(a_ref, b_ref, o_ref):
    j = pl.program_id(0)
    row = lax.broadcasted_iota(jnp.int32, (LANE, a_ref.shape[2]), 0)
    for l in range(a_ref.shape[1]):
        a, b = a_ref[:, l, :], b_ref[:, l, :]
        shifted = jnp.where(row < LANE - SHIFT, pltpu.roll(a, LANE - SHIFT, 0), pltpu.roll(b, LANE - SHIFT, 0))
        out = jnp.where(j < A_SLOTS, a, jnp.where(j == A_SLOTS, 0.0, shifted))
        o_ref[l] = out.T.astype(o_ref.dtype)


def _pack_w_in(w_in_all):
    depth, d, n_in = w_in_all.shape
    n_slots = A_SLOTS + 1 + (n_in - A_COLS) // LANE
    assert (n_in - A_COLS) % LANE == 0 and 0 < SHIFT < LANE
    src_a = lambda j: jnp.where(j < A_SLOTS, j, j - 2)
    src_b = lambda j: jnp.where(j < A_SLOTS, j, j - 1)
    w_t = jnp.transpose(w_in_all, (2, 0, 1))
    return pl.pallas_call(
        _pack_kernel,
        out_shape=jax.ShapeDtypeStruct((depth, d, n_slots * LANE), BF16),
        grid=(n_slots,),
        in_specs=[pl.BlockSpec((LANE, depth, d), lambda j: (src_a(j), 0, 0)),
                  pl.BlockSpec((LANE, depth, d), lambda j: (src_b(j), 0, 0))],
        out_specs=pl.BlockSpec((depth, d, LANE), lambda j: (0, 0, j)),
        compiler_params=_cparams("parallel"),
        name="pack_w_in",
    )(w_t, w_t)


def _mm_kernel(*refs, n_extra, nk, epilogue):
    a_ref, w_ref = refs[0], refs[1]
    extra = refs[2:2 + n_extra]
    o_ref = refs[2 + n_extra]
    acc_ref = refs[3 + n_extra]
    k = pl.program_id(2)

    @pl.when(k == 0)
    def _():
        acc_ref[...] = jnp.zeros_like(acc_ref)

    acc_ref[...] += jnp.dot(a_ref[...], w_ref[...].astype(BF16), preferred_element_type=F32)

    @pl.when(k == nk - 1)
    def _():
        o_ref[...] = epilogue(acc_ref[...], *[e[...] for e in extra]).astype(o_ref.dtype)


def _matmul(a, w, epilogue, out_dtype, *, n, layer=None, col0=0, residual=None,
            tm=1024, tn=512, tk=2048, name="matmul"):
    m, kdim = a.shape
    tm, tn, tk = _tile(m, tm), _tile(n, tn), _tile(kdim, tk)
    nk = kdim // tk
    cb = col0 // tn
    assert col0 % tn == 0
    if layer is None:
        w_spec = pl.BlockSpec((tk, tn), lambda i, j, k: (k, cb + j))
    else:
        w_spec = pl.BlockSpec((None, tk, tn), lambda i, j, k: (layer, k, cb + j))
    specs = [pl.BlockSpec((tm, tk), lambda i, j, k: (i, k)), w_spec]
    extras = []
    if residual is not None:
        specs.append(pl.BlockSpec((tm, tn), lambda i, j, k: (i, j)))
        extras.append(residual)
    return pl.pallas_call(
        functools.partial(_mm_kernel, n_extra=len(extras), nk=nk, epilogue=epilogue),
        out_shape=jax.ShapeDtypeStruct((m, n), out_dtype),
        grid=(m // tm, n // tn, nk),
        in_specs=specs,
        out_specs=pl.BlockSpec((tm, tn), lambda i, j, k: (i, j)),
        scratch_shapes=[pltpu.VMEM((tm, tn), F32)],
        compiler_params=_cparams("parallel", "parallel", "arbitrary"),
        name=name,
    )(a, w, *extras)


def _cast_kernel(w_ref, o_ref):
    o_ref[...] = w_ref[...].astype(o_ref.dtype)


def _cast_bf16(w_all, l):
    _, kdim, n = w_all.shape
    tk = _tile(kdim, 512)
    return pl.pallas_call(
        _cast_kernel,
        out_shape=jax.ShapeDtypeStruct((kdim, n), BF16),
        grid=(kdim // tk,),
        in_specs=[pl.BlockSpec((None, tk, n), lambda i: (l, i, 0))],
        out_specs=pl.BlockSpec((tk, n), lambda i: (i, 0)),
        compiler_params=_cparams("parallel"),
        name="cast_bf16",
    )(w_all)


def _proj_kernel(h_ref, w_ref, gain_ref, nf_ref, rf_ref, cos_ref, sa_ref, sb_ref, bd_ref, o_ref, ob_ref, xs_ref,
                 *, sub):
    w, bd = w_ref[...], bd_ref[...]
    normed = nf_ref[...] > 0.5
    gain, rf = gain_ref[...], rf_ref[...]
    xs_ref[...] = jnp.dot(h_ref[...], w, preferred_element_type=F32)
    for c in range(h_ref.shape[0] // sub):
        rows = slice(c * sub, (c + 1) * sub)
        x = xs_ref[rows, :]
        ms = _seg_sum(x * x, bd) * (1.0 / HEAD_DIM)
        y = x * jnp.where(normed, lax.rsqrt(ms + NORM_EPS) * gain, 1.0)
        cosv, sav, sbv = cos_ref[rows, :], sa_ref[rows, :], sb_ref[rows, :]
        for s in range(2):
            ys = y[:, s * LANE:(s + 1) * LANE]
            f = rf[:, s * LANE:(s + 1) * LANE]
            roped = ys * cosv + pltpu.roll(ys, 8, 1) * sav + pltpu.roll(ys, LANE - 8, 1) * sbv
            out = jnp.where(f > 0.5, roped, ys)
            o_ref[rows, s * LANE:(s + 1) * LANE] = out
            ob_ref[rows, s * LANE:(s + 1) * LANE] = out.astype(BF16)


def _qkv_projection(h, w_pack, l, gain, nf, rf, cos_t, sin_a, sin_b):
    m, d = h.shape
    tm, tn = _tile(m, 1024), 2 * LANE
    zw = N_QKV_SLOTS * LANE
    row = lambda i, j: (0, j)
    tab = lambda i, j: (i, 0)
    return pl.pallas_call(
        functools.partial(_proj_kernel, sub=_tile(tm, 256)),
        out_shape=[jax.ShapeDtypeStruct((m, zw), F32), jax.ShapeDtypeStruct((m, zw), BF16)],
        grid=(m // tm, zw // tn),
        in_specs=[pl.BlockSpec((tm, d), lambda i, j: (i, 0)), pl.BlockSpec((None, d, tn), lambda i, j: (l, 0, j)),
                  pl.BlockSpec((1, tn), row), pl.BlockSpec((1, tn), row), pl.BlockSpec((1, tn), row),
                  pl.BlockSpec((tm, LANE), tab), pl.BlockSpec((tm, LANE), tab), pl.BlockSpec((tm, LANE), tab),
                  pl.BlockSpec((tn, tn), lambda i, j: (0, 0))],
        out_specs=[pl.BlockSpec((tm, tn), lambda i, j: (i, j))] * 2,
        scratch_shapes=[pltpu.VMEM((tm, tn), F32)],
        compiler_params=_cparams("parallel", "arbitrary"),
        name="qkv_projection",
    )(h, w_pack, gain, nf, rf, cos_t, sin_a, sin_b, _ones_blockdiag(tn))


def _column_vectors(lw):
    f = lambda v: jnp.asarray(v, F32).reshape(-1)
    ones = lambda n: jnp.ones((n,), F32)
    zeros = lambda n: jnp.zeros((n,), F32)
    rep = lambda v, n: jnp.tile(f(v), n)
    groups = [
        (rep(lw["a_qn"], 8), 1.0, 1.0), (rep(lw["a_kn"], 2), 1.0, 1.0), (ones(128), 0.0, 0.0),
        (ones(256), 0.0, 1.0), (ones(64), 0.0, 1.0), (ones(64 + LANE), 0.0, 0.0),
        (rep(lw["b_qn"], 8), 1.0, 0.0), (rep(lw["b_kn"], 8), 1.0, 0.0), (ones(512), 0.0, 0.0),
        (rep(lw["c_qn"], 4), 1.0, 1.0), (rep(lw["c_kn"], 4), 1.0, 1.0), (ones(512), 0.0, 0.0)]
    gain = jnp.concatenate([g for g, _, _ in groups]).reshape(1, -1)
    nf = jnp.concatenate([ones(g.shape[0]) * a for g, a, _ in groups]).reshape(1, -1)
    rf = jnp.concatenate([ones(g.shape[0]) * b for g, _, b in groups]).reshape(1, -1)
    del zeros
    return gain, nf, rf


def _rope_tables(pos):
    half = ROPE_DIM // 2
    inv_freq = ROPE_THETA ** (-jnp.arange(0, ROPE_DIM, 2, dtype=F32) / ROPE_DIM)
    ang = pos.astype(F32)[:, None] * inv_freq[None, :]
    cos, sin = jnp.cos(ang), jnp.sin(ang)
    rows = pos.shape[0]
    one = jnp.ones((rows, HEAD_DIM - ROPE_DIM), F32)
    zero = jnp.zeros((rows, HEAD_DIM - ROPE_DIM), F32)
    z8 = jnp.zeros((rows, half), F32)
    cos_h = jnp.concatenate([cos, cos, one], axis=1)
    sa_h = jnp.concatenate([z8, sin, zero], axis=1)
    sb_h = jnp.concatenate([-sin, z8, zero], axis=1)
    dup = lambda t: jnp.concatenate([t, t], axis=1)
    return dup(cos_h), dup(sa_h), dup(sb_h)


def _to_key(score):
    bits = pltpu.bitcast(score + 0.0, jnp.int32)
    return jnp.where(bits < 0, bits ^ 0x7FFFFFFF, bits)


def _indexer_scores(iq, iw, ik):
    kk = ik[:, :HEAD_DIM]
    sc = None
    for hd in range(A_IDX_HEADS):
        logit = _dot_nt3(iq[:, hd * HEAD_DIM:(hd + 1) * HEAD_DIM], kk)
        wgt = iw[:, HEAD_DIM + hd:HEAD_DIM + hd + 1] * (A_IDX_HEADS ** -0.5 * HEAD_DIM ** -0.5)
        term = jnp.maximum(logit, 0.0) * wgt
        sc = term if sc is None else sc + term
    return sc


def _lane_fold(x, op):
    out = x[:, :LANE]
    for s in range(1, x.shape[1] // LANE):
        out = op(out, x[:, s * LANE:(s + 1) * LANE])
    return out


def _count(key_ref, n_blk, blk, pred):
    rows = key_ref.shape[0]
    grp = min(rows, LANE)
    parts = []
    for r0 in range(0, rows, grp):
        def body(b, acc, r0=r0):
            start = pl.multiple_of(b * blk, blk)
            kb = key_ref[r0:r0 + grp, pl.ds(start, blk)]
            return acc + _lane_fold(jnp.where(pred(kb, start, slice(r0, r0 + grp)), 1.0, 0.0), jnp.add)

        parts.append(lax.fori_loop(0, n_blk, body, jnp.zeros((grp, LANE), F32)))
    acc = parts[0] if len(parts) == 1 else jnp.concatenate(parts, axis=0)
    return jnp.sum(acc, axis=-1, keepdims=True)


def _topk_threshold(key_ref, n_blk, blk, topk):
    rows = key_ref.shape[0]
    kf = float(topk)
    c0 = _count(key_ref, n_blk, blk, lambda kb, st, rs: kb >= 0)
    ans = jnp.where(c0 >= kf, 0, INT_MIN).astype(jnp.int32)

    def bit_step(it, ans):
        cand = ans + jnp.left_shift(jnp.int32(1), 30 - it)
        c = _count(key_ref, n_blk, blk, lambda kb, st, rs: kb >= cand[rs])
        return jnp.where(c >= kf, cand, ans)

    thr = lax.fori_loop(0, 31, bit_step, ans)
    n_gt = _count(key_ref, n_blk, blk, lambda kb, st, rs: kb > thr[rs])
    n_eq = _count(key_ref, n_blk, blk, lambda kb, st, rs: kb == thr[rs])
    need = kf - n_gt
    nbits = int(n_blk * blk).bit_length() if isinstance(n_blk, int) else 14
    cut_all = jnp.full((rows, 1), 1 << nbits, jnp.int32)
    tie_overflow = jnp.max(jnp.where((n_eq > need) & (thr != NEG_KEY), 1.0, 0.0)) > 0.5

    def search_cut():
        def cut_step(it, cut):
            cand = cut + jnp.left_shift(jnp.int32(1), nbits - 1 - it)

            def pred(kb, st, rs):
                idx = st + lax.broadcasted_iota(jnp.int32, kb.shape, 1)
                return (kb == thr[rs]) & (idx < cand[rs])

            c = _count(key_ref, n_blk, blk, pred)
            return jnp.where(c <= need, cand, cut)

        return lax.fori_loop(0, nbits, cut_step, jnp.zeros((rows, 1), jnp.int32))

    cut = lax.cond(tie_overflow, search_cut, lambda: cut_all)
    return thr, cut


def _selected(kb, first_idx, thr, cut):
    idx = first_idx + lax.broadcasted_iota(jnp.int32, kb.shape, 1)
    return ((kb > thr) | ((kb == thr) & (idx < cut))) & (kb > NEG_KEY)


def _pair_heads(o_even, o_odd, group_half):
    lane = lax.broadcasted_iota(jnp.int32, o_even.shape, 1)
    if group_half == 0:
        return jnp.where(lane < HEAD_DIM, o_even, pltpu.roll(o_odd, HEAD_DIM, 1))
    return jnp.where(lane < HEAD_DIM, pltpu.roll(o_even, HEAD_DIM, 1), o_odd)


def _dsa_query_heads(q_ref, qh_ref):
    for hd in range(A_HEADS):
        slot, half = hd // 2, hd % 2
        group = hd // (A_HEADS // A_KV_HEADS)
        qs = q_ref[:, slot * LANE:(slot + 1) * LANE] * (HEAD_DIM ** -0.5)
        if half != group:
            qs = pltpu.roll(qs, HEAD_DIM, 1)
        qh_ref[hd] = jnp.where(_half_mask(qs.shape, group), qs, 0.0).astype(BF16)


def _softmax_block(sel, kblk, vblk, q_heads, m_ref, acc_ref, s_ref, p_ref):
    n = len(q_heads)
    reps = kblk.shape[0] // LANE
    v_ones = jnp.concatenate([vblk, jnp.ones_like(vblk)], axis=1)
    for hd in range(n):
        s_ref[hd] = _dot_nt(q_heads[hd], kblk)
    alphas = []
    for hd in range(n):
        s = s_ref[hd]
        if sel is not None:
            s = jnp.where(sel[hd] if isinstance(sel, (list, tuple)) else sel, s, -jnp.inf)
            s_ref[hd] = s
        blk_max = jnp.max(_lane_fold(s, jnp.maximum), axis=-1, keepdims=True)
        m_prev = m_ref[hd]
        m_new = jnp.maximum(m_prev, blk_max)
        alphas.append(jnp.exp(m_prev - m_new))
        m_ref[hd] = m_new
    for hd in range(n):
        p_ref[hd] = jnp.exp(s_ref[hd] - jnp.tile(m_ref[hd], (1, reps))).astype(BF16)
    for hd in range(n):
        alpha2 = jnp.tile(alphas[hd], (1, 2))
        acc_ref[hd] = alpha2 * acc_ref[hd] + jnp.dot(p_ref[hd], v_ones, preferred_element_type=F32)


def _softmax_init(m_ref, acc_ref):
    m_ref[...] = jnp.full(m_ref.shape, M_FLOOR, F32)
    acc_ref[...] = jnp.zeros(acc_ref.shape, F32)


def _softmax_result(acc_ref, hd):
    acc = acc_ref[hd]
    return acc[:, :LANE] / acc[:, LANE:]


def _indexer_query3(iq_ref, iq3_ref):
    for hd in range(A_IDX_HEADS):
        slot, half = hd // 2, hd % 2
        x = iq_ref[:, slot * LANE:(slot + 1) * LANE]
        xl = jnp.where(_half_mask(x.shape, half), x, pltpu.roll(x, HEAD_DIM, 1))
        hi = xl.astype(BF16).astype(F32)
        lower = _half_mask(x.shape, 0)
        iq3_ref[hd, :, :LANE] = jnp.where(lower, hi, xl - hi).astype(BF16)
        iq3_ref[hd, :, LANE:] = jnp.where(lower, hi, 0.0).astype(BF16)


def _ik3_kernel(x_ref, o_ref):
    x = x_ref[...]
    lower = _half_mask(x.shape, 0)
    xl = jnp.where(lower, x, pltpu.roll(x, HEAD_DIM, 1))
    hi = xl.astype(BF16).astype(F32)
    o_ref[:, :LANE] = hi.astype(BF16)
    o_ref[:, LANE:] = jnp.where(lower, xl - hi, 0.0).astype(BF16)


def _indexer_keys3(z, t):
    tm = _tile(t, 1024)
    return pl.pallas_call(
        _ik3_kernel,
        out_shape=jax.ShapeDtypeStruct((t, 2 * LANE), BF16),
        grid=(t // tm,),
        in_specs=[pl.BlockSpec((tm, LANE), lambda i: (i, S_AIK))],
        out_specs=pl.BlockSpec((tm, 2 * LANE), lambda i: (i, 0)),
        compiler_params=_cparams("parallel"),
        name="indexer_keys",
    )(z)


def _dsa_prompt_kernel(q_ref, iq_ref, iw_ref, k_ref, v_ref, ik3_ref, o_ref,
                       key_ref, qh_ref, iq3_ref, m_ref, acc_ref, s_ref, p_ref, *, tq, kb, topk):
    i = pl.program_id(0)
    q0 = i * tq
    n_blk = (q0 + tq + kb - 1) // kb
    row = lax.broadcasted_iota(jnp.int32, (tq, kb), 0)
    limit = (((q0 + row) >> 6) + 1) << 6
    _indexer_query3(iq_ref, iq3_ref)
    iw = iw_ref[...]
    wgt = [iw[:, HEAD_DIM + hd:HEAD_DIM + hd + 1] * (A_IDX_HEADS ** -0.5 * HEAD_DIM ** -0.5)
           for hd in range(A_IDX_HEADS)]

    def score_blk(b, carry):
        start = pl.multiple_of(b * kb, kb)
        ikb = ik3_ref[pl.ds(start, kb), :]
        for hd in range(A_IDX_HEADS):
            s_ref[hd] = _dot_nt(iq3_ref[hd], ikb)
        sc = None
        for hd in range(A_IDX_HEADS):
            term = jnp.maximum(s_ref[hd], 0.0) * wgt[hd]
            sc = term if sc is None else sc + term
        kpos = start + lax.broadcasted_iota(jnp.int32, (tq, kb), 1)
        key_ref[:, pl.ds(start, kb)] = jnp.where(kpos < limit, _to_key(sc), NEG_KEY)
        return carry

    lax.fori_loop(0, n_blk, score_blk, 0)
    thr, cut = _topk_threshold(key_ref, n_blk, kb, topk)
    _dsa_query_heads(q_ref, qh_ref)
    _softmax_init(m_ref, acc_ref)

    def attn_blk(b, carry):
        start = pl.multiple_of(b * kb, kb)
        sel = _selected(key_ref[:, pl.ds(start, kb)], start, thr, cut)
        _softmax_block(sel, k_ref[pl.ds(start, kb), :], v_ref[pl.ds(start, kb), :],
                       [qh_ref[hd] for hd in range(A_HEADS)], m_ref, acc_ref, s_ref, p_ref)
        return carry

    lax.fori_loop(0, n_blk, attn_blk, 0)
    for j in range(A_HEADS // 2):
        outs = [_softmax_result(acc_ref, 2 * j + e) for e in range(2)]
        o_ref[:, j * LANE:(j + 1) * LANE] = _pair_heads(outs[0], outs[1], (2 * j) // (A_HEADS // A_KV_HEADS)).astype(o_ref.dtype)


def _mixer_call(kern, o_all, m, *, grid, in_specs, out_block, out_index, args, scratch=(), sem, name):
    n_in = len(args)
    if o_all is None:
        fn, specs, alias, extra = kern, list(in_specs), {}, []
    else:
        fn = lambda *refs: kern(*refs[:n_in], *refs[n_in + 1:])
        specs, alias, extra = list(in_specs) + [pl.BlockSpec(memory_space=pl.ANY)], {n_in: 0}, [o_all]
    return pl.pallas_call(
        fn,
        out_shape=jax.ShapeDtypeStruct((m, N_BRANCH * BRANCH_WIDTH), BF16),
        grid=grid,
        in_specs=specs,
        out_specs=pl.BlockSpec(out_block, out_index),
        scratch_shapes=list(scratch),
        input_output_aliases=alias,
        compiler_params=_cparams(*sem),
        name=name,
    )(*args, *extra)


def _dsa_prompt(z, zb, t, o_all, m):
    tq, kb = 256, 512
    topk = min(A_TOPK_MAX, t // 4)
    full = lambda s: pl.BlockSpec((t, LANE), lambda i: (0, s))
    heads = lambda dt, w=LANE: pltpu.VMEM((A_HEADS, tq, w), dt)
    return _mixer_call(
        functools.partial(_dsa_prompt_kernel, tq=tq, kb=kb, topk=topk), o_all, m,
        grid=(t // tq,),
        in_specs=[pl.BlockSpec((tq, 4 * LANE), lambda i: (i, S_AQ // 4)),
                  pl.BlockSpec((tq, 2 * LANE), lambda i: (i, S_AIQ // 2)),
                  pl.BlockSpec((tq, LANE), lambda i: (i, S_AIK)),
                  full(S_AK), full(S_AV), pl.BlockSpec((t, 2 * LANE), lambda i: (0, 0))],
        out_block=(tq, BRANCH_WIDTH), out_index=lambda i: (i, 0),
        scratch=[pltpu.VMEM((tq, t), jnp.int32), heads(BF16),
                 pltpu.VMEM((A_IDX_HEADS, tq, 2 * LANE), BF16),
                 heads(F32), heads(F32, 2 * LANE), heads(F32, kb), heads(BF16, kb)],
        sem=("arbitrary",), name="dsa_prompt",
        args=(z, z, z, zb, zb, _indexer_keys3(z, t)))


def _feature_major(cache, n_feat):
    nd = cache.ndim
    perm = (0, 1) + tuple(range(3, nd)) + (2,)
    return jnp.transpose(cache, perm).reshape(cache.shape[0], cache.shape[1], n_feat, cache.shape[2])


def _dsa_sample_kernel(q_ref, iq_ref, new_ik_ref, new_k_ref, new_v_ref, ckt_ref, cvt_ref, cikt_ref, o_ref,
                       key_ref, qh_ref, *, ts, n_seq, past, topk, q_pos0):
    rows_all = n_seq * ts
    width = key_ref.shape[1]
    iq_all, iw_all = iq_ref[...], new_ik_ref[...]
    row_c = lax.broadcasted_iota(jnp.int32, (ts, past), 0)
    limit_c = (((q_pos0 + row_c) >> 6) + 1) << 6
    kpos_c = lax.broadcasted_iota(jnp.int32, (ts, past), 1)
    kpos_n = past + lax.broadcasted_iota(jnp.int32, (ts, ts), 1)
    limit_n = (((q_pos0 + lax.broadcasted_iota(jnp.int32, (ts, ts), 0)) >> 6) + 1) << 6
    key_ref[:, past:] = jnp.full((rows_all, width - past), NEG_KEY, jnp.int32)
    for g in range(n_seq):
        rows = slice(g * ts, (g + 1) * ts)
        iq, iw = iq_all[rows], iw_all[rows]
        ikt = cikt_ref[g]
        ik_new = iw[:, :HEAD_DIM]
        sc_c = sc_n = None
        for hd in range(A_IDX_HEADS):
            qh = iq[:, hd * HEAD_DIM:(hd + 1) * HEAD_DIM]
            wgt = iw[:, HEAD_DIM + hd:HEAD_DIM + hd + 1] * (A_IDX_HEADS ** -0.5 * HEAD_DIM ** -0.5)
            t_c = jnp.maximum(_dot3(qh, ikt), 0.0) * wgt
            t_n = jnp.maximum(_dot_nt3(qh, ik_new), 0.0) * wgt
            sc_c = t_c if sc_c is None else sc_c + t_c
            sc_n = t_n if sc_n is None else sc_n + t_n
        key_ref[rows, :past] = jnp.where(kpos_c < limit_c, _to_key(sc_c), NEG_KEY)
        key_ref[rows, past:past + ts] = jnp.where(kpos_n < limit_n, _to_key(sc_n), NEG_KEY)

    thr, cut = _topk_threshold(key_ref, width // LANE, LANE, topk)
    _dsa_query_heads(q_ref, qh_ref)
    for g in range(n_seq):
        rows = slice(g * ts, (g + 1) * ts)
        sel_c = _selected(key_ref[rows, :past], 0, thr[rows], cut[rows])
        sel_n = _selected(key_ref[rows, past:past + ts], past, thr[rows], cut[rows])
        kct, vct = ckt_ref[g].astype(BF16), cvt_ref[g].astype(BF16)
        kn, vn = new_k_ref[rows, :], new_v_ref[rows, :]
        outs = []
        for hd in range(A_HEADS):
            qh = qh_ref[hd, rows, :]
            s_c = jnp.where(sel_c, jnp.dot(qh, kct, preferred_element_type=F32), -jnp.inf)
            s_n = jnp.where(sel_n, _dot_nt(qh, kn), -jnp.inf)
            m = jnp.maximum(jnp.max(s_c, axis=-1, keepdims=True), jnp.max(s_n, axis=-1, keepdims=True))
            p_c, p_n = jnp.exp(s_c - m), jnp.exp(s_n - m)
            l = jnp.sum(p_c, axis=-1, keepdims=True) + jnp.sum(p_n, axis=-1, keepdims=True)
            o = _dot_nt(p_c.astype(BF16), vct) + jnp.dot(p_n.astype(BF16), vn, preferred_element_type=F32)
            outs.append(o / l)
        for j in range(A_HEADS // 2):
            o_ref[rows, j * LANE:(j + 1) * LANE] = _pair_heads(
                outs[2 * j], outs[2 * j + 1], (2 * j) // (A_HEADS // A_KV_HEADS)).astype(o_ref.dtype)


def _dsa_sample(z, zb, cache_k, cache_v, cache_ik, l, t, nb, ts, o_all, m):
    past = cache_k.shape[2]
    topk = min(A_TOPK_MAX, (past + ts) // 4)
    width = ((past + ts + LANE - 1) // LANE) * LANE
    n_seq = LANE // ts
    rows = n_seq * ts
    assert LANE % ts == 0 and nb % n_seq == 0 and t % rows == 0
    rb = t // rows
    new = lambda s: pl.BlockSpec((rows, LANE), lambda i: (rb + i, s))
    cache = lambda w: pl.BlockSpec((None, n_seq, w, past), lambda i: (l, i, 0, 0))
    return _mixer_call(
        functools.partial(_dsa_sample_kernel, ts=ts, n_seq=n_seq, past=past, topk=topk, q_pos0=past), o_all, m,
        grid=(nb // n_seq,),
        in_specs=[pl.BlockSpec((rows, 4 * LANE), lambda i: (rb + i, S_AQ // 4)),
                  pl.BlockSpec((rows, 2 * LANE), lambda i: (rb + i, S_AIQ // 2)),
                  new(S_AIK), new(S_AK), new(S_AV), cache(LANE), cache(LANE), cache(HEAD_DIM)],
        out_block=(rows, BRANCH_WIDTH), out_index=lambda i: (rb + i, 0),
        scratch=[pltpu.VMEM((rows, width), jnp.int32), pltpu.VMEM((A_HEADS, rows, LANE), BF16)],
        sem=("arbitrary",), name="dsa_sample",
        args=(z, z, z, zb, zb, _feature_major(cache_k, LANE), _feature_major(cache_v, LANE),
              _feature_major(cache_ik, HEAD_DIM)))


def _band_prompt_kernel(q_ref, k_ref, v_ref, bias_ref, o_ref, *, tq, win):
    i = pl.program_id(1)
    w0 = pl.multiple_of(jnp.maximum(i * tq - B_WINDOW, 0), tq)
    kw = k_ref[pl.ds(w0, win), :]
    vw = v_ref[pl.ds(w0, win), :]
    kc = (w0 + lax.broadcasted_iota(jnp.int32, (tq, win), 1)) >> 6
    qc = (i * tq + lax.broadcasted_iota(jnp.int32, (tq, win), 0)) >> 6
    ok = (kc <= qc) & (kc >= qc - B_PAST_CHUNKS)
    qs = q_ref[...] * (HEAD_DIM ** -0.5)
    outs = []
    for e in range(2):
        qe = jnp.where(_half_mask(qs.shape, e), qs, 0.0).astype(BF16)
        s = jnp.where(ok, _dot_nt(qe, kw) + bias_ref[e], -jnp.inf)
        p = jnp.exp(s - jnp.max(s, axis=-1, keepdims=True))
        l = jnp.sum(p, axis=-1, keepdims=True)
        outs.append(jnp.dot(p.astype(BF16), vw, preferred_element_type=F32) / l)
    o_ref[...] = jnp.where(_half_mask(outs[0].shape, 0), outs[0], outs[1]).astype(o_ref.dtype)


def _band_prompt(z, zb, bias, t, tq, o_all, m):
    win = B_WINDOW + tq
    n_case = B_WINDOW // tq
    return _mixer_call(
        functools.partial(_band_prompt_kernel, tq=tq, win=win), o_all, m,
        grid=(B_HEADS // 2, t // tq),
        in_specs=[pl.BlockSpec((tq, LANE), lambda j, i: (i, S_BQ + j)),
                  pl.BlockSpec((t, LANE), lambda j, i: (0, S_BK + j)),
                  pl.BlockSpec((t, LANE), lambda j, i: (0, S_BV + j)),
                  pl.BlockSpec((None, 2, tq, win), lambda j, i: (jnp.minimum(i, n_case), j, 0, 0))],
        out_block=(tq, LANE), out_index=lambda j, i: (i, BRANCH_WIDTH // LANE + j),
        sem=("parallel", "arbitrary"), name="band_prompt", args=(z, zb, zb, bias))


def _band_sample_kernel(z_ref, zb_ref, kct_ref, vct_ref, bc_ref, bn_ref, o_ref):
    for j in range(B_HEADS // 2):
        slot = lambda ref, s: ref[:, (s + j) * LANE:(s + j + 1) * LANE]
        feat = slice(j * LANE, (j + 1) * LANE)
        kct, vct = kct_ref[feat, :].astype(BF16), vct_ref[feat, :].astype(BF16)
        kn, vn = slot(zb_ref, S_BK), slot(zb_ref, S_BV)
        qs = slot(z_ref, S_BQ) * (HEAD_DIM ** -0.5)
        outs = []
        for e in range(2):
            qe = jnp.where(_half_mask(qs.shape, e), qs, 0.0).astype(BF16)
            s_c = jnp.dot(qe, kct, preferred_element_type=F32) + bc_ref[2 * j + e]
            s_n = _dot_nt(qe, kn) + bn_ref[2 * j + e]
            m = jnp.maximum(jnp.max(s_c, axis=-1, keepdims=True), jnp.max(s_n, axis=-1, keepdims=True))
            p_c, p_n = jnp.exp(s_c - m), jnp.exp(s_n - m)
            l = jnp.sum(p_c, axis=-1, keepdims=True) + jnp.sum(p_n, axis=-1, keepdims=True)
            o = _dot_nt(p_c.astype(BF16), vct) + jnp.dot(p_n.astype(BF16), vn, preferred_element_type=F32)
            outs.append(o / l)
        o_ref[:, feat] = jnp.where(_half_mask(outs[0].shape, 0), outs[0], outs[1]).astype(o_ref.dtype)


def _band_sample(z, zb, cache_k, cache_v, bias_c, bias_n, l, t, nb, ts, o_all, m):
    pb = cache_k.shape[2]
    rb = t // ts
    bw = BRANCH_WIDTH
    cache = pl.BlockSpec((None, None, bw, pb), lambda b: (l, b, 0, 0))
    full = lambda a: pl.BlockSpec(a.shape, lambda b: (0,) * a.ndim)
    return _mixer_call(
        _band_sample_kernel, o_all, m,
        grid=(nb,),
        in_specs=[pl.BlockSpec((ts, z.shape[1]), lambda b: (rb + b, 0)),
                  pl.BlockSpec((ts, z.shape[1]), lambda b: (rb + b, 0)),
                  cache, cache, full(bias_c), full(bias_n)],
        out_block=(ts, bw), out_index=lambda b: (rb + b, 1),
        sem=("parallel",), name="band_sample",
        args=(z, zb, _feature_major(cache_k, bw), _feature_major(cache_v, bw), bias_c, bias_n))


def _band_bias(table, tq):
    n_case = B_WINDOW // tq
    win = B_WINDOW + tq
    width = win + B_WINDOW
    j = np.arange(width + tq - 1) - (tq - 1) - B_WINDOW
    ext = table.astype(F32)[:, np.clip(j, -B_REL_CLIP, B_REL_CLIP) + B_REL_CLIP]
    toe = jnp.stack([ext[:, tq - 1 - r:tq - 1 - r + width] for r in range(tq)], axis=1)
    return jnp.stack([toe[:, :, B_WINDOW - c * tq:B_WINDOW - c * tq + win] for c in range(n_case + 1)], axis=0)


def _lambda(lam_ref, lam_init):
    lv = lam_ref[...]
    return (jnp.exp(jnp.sum(lv[0:1] * lv[1:2], axis=-1, keepdims=True))
            - jnp.exp(jnp.sum(lv[2:3] * lv[3:4], axis=-1, keepdims=True)) + lam_init)


def _diff_finish(o0, o1, lam, on_ref, lam_init):
    attn = o0 - lam * o1
    ms = jnp.mean(attn * attn, axis=-1, keepdims=True)
    return (attn * lax.rsqrt(ms + NORM_EPS) * on_ref[...]) * (1.0 - lam_init)


def _diff_prompt_kernel(q_ref, k_ref, v_ref, lam_ref, on_ref, o_ref, qh_ref, m_ref, acc_ref, s_ref, p_ref,
                        *, tq, kb, tg, lam_init):
    i = pl.program_id(1)
    q0 = i * tq
    n_grp = tq // tg
    n_blk = (q0 + tq + kb - 1) // kb
    row = lax.broadcasted_iota(jnp.int32, (tg, kb), 0)
    limits = [(((q0 + g * tg + row) >> 6) + 1) << 6 for g in range(n_grp)]
    for g in range(n_grp):
        qs = q_ref[g * tg:(g + 1) * tg, :] * (HEAD_DIM ** -0.5)
        for c in range(2):
            qh_ref[2 * g + c] = jnp.where(_half_mask(qs.shape, c), qs, 0.0).astype(BF16)
    _softmax_init(m_ref, acc_ref)
    n_full = (q0 + CHUNK) // kb

    def attn_blk(b, carry, masked):
        start = pl.multiple_of(b * kb, kb)
        ok = None
        if masked:
            kpos = start + lax.broadcasted_iota(jnp.int32, (tg, kb), 1)
            ok = [kpos < limits[u // 2] for u in range(2 * n_grp)]
        _softmax_block(ok, k_ref[pl.ds(start, kb), :], v_ref[pl.ds(start, kb), :],
                       [qh_ref[u] for u in range(2 * n_grp)], m_ref, acc_ref, s_ref, p_ref)
        return carry

    lax.fori_loop(0, n_full, functools.partial(attn_blk, masked=False), 0)
    lax.fori_loop(n_full, n_blk, functools.partial(attn_blk, masked=True), 0)
    lam = _lambda(lam_ref, lam_init)
    for g in range(n_grp):
        outs = [_softmax_result(acc_ref, 2 * g + c) for c in range(2)]
        o_ref[g * tg:(g + 1) * tg, :] = _diff_finish(outs[0], outs[1], lam, on_ref, lam_init).astype(o_ref.dtype)


def _diff_prompt(z, zb, c_lam_all, c_on_all, l, t, lam_init, o_all, m):
    tq, kb, tg = _tile(t, 512), 512, 128
    maps = lambda dt, w=LANE: pltpu.VMEM((2 * tq // tg, tg, w), dt)
    return _mixer_call(
        functools.partial(_diff_prompt_kernel, tq=tq, kb=kb, tg=tg, lam_init=lam_init), o_all, m,
        grid=(C_HEADS, t // tq),
        in_specs=[pl.BlockSpec((tq, LANE), lambda h, i: (i, S_CQ + h)),
                  pl.BlockSpec((t, LANE), lambda h, i: (0, S_CK + h)),
                  pl.BlockSpec((t, LANE), lambda h, i: (0, S_CV + h)),
                  pl.BlockSpec((None, 4, HEAD_DIM), lambda h, i: (l, 0, 0)),
                  _layer_vec(l, LANE)],
        out_block=(tq, LANE), out_index=lambda h, i: (i, 2 * BRANCH_WIDTH // LANE + h),
        scratch=[maps(BF16), maps(F32), maps(F32, 2 * LANE), maps(F32, kb), maps(BF16, kb)],
        sem=("parallel", "arbitrary"), name="diff_prompt",
        args=(z, zb, zb, c_lam_all, _vec3(c_on_all)))


def _diff_sample_kernel(z_ref, zb_ref, kct_ref, vc_ref, lam_ref, on_ref, o_ref, *, past, lam_init):
    lam = _lambda(lam_ref, lam_init)
    for h in range(C_HEADS):
        slot = lambda ref, s: ref[:, (s + h) * LANE:(s + h + 1) * LANE]
        feat = slice(h * LANE, (h + 1) * LANE)
        kct = kct_ref[feat, :].astype(BF16)
        vc = vc_ref[pl.ds(h, past, stride=C_HEADS), :].astype(BF16)
        kn, vn = slot(zb_ref, S_CK), slot(zb_ref, S_CV)
        qs = slot(z_ref, S_CQ) * (HEAD_DIM ** -0.5)
        outs = []
        for c in range(2):
            qc = jnp.where(_half_mask(qs.shape, c), qs, 0.0).astype(BF16)
            s_c = jnp.dot(qc, kct, preferred_element_type=F32)
            s_n = _dot_nt(qc, kn)
            m = jnp.maximum(jnp.max(s_c, axis=-1, keepdims=True), jnp.max(s_n, axis=-1, keepdims=True))
            p_c, p_n = jnp.exp(s_c - m), jnp.exp(s_n - m)
            l = jnp.sum(p_c, axis=-1, keepdims=True) + jnp.sum(p_n, axis=-1, keepdims=True)
            o = (jnp.dot(p_c.astype(BF16), vc, preferred_element_type=F32)
                 + jnp.dot(p_n.astype(BF16), vn, preferred_element_type=F32))
            outs.append(o / l)
        o_ref[:, feat] = _diff_finish(outs[0], outs[1], lam, on_ref, lam_init).astype(o_ref.dtype)


def _diff_sample(z, zb, cache_k, cache_v, c_lam_all, c_on_all, l, t, nb, ts, lam_init, o_all, m):
    depth, _, past = cache_k.shape[:3]
    rb = t // ts
    bw = BRANCH_WIDTH
    return _mixer_call(
        functools.partial(_diff_sample_kernel, past=past, lam_init=lam_init), o_all, m,
        grid=(nb,),
        in_specs=[pl.BlockSpec((ts, z.shape[1]), lambda b: (rb + b, 0)),
                  pl.BlockSpec((ts, z.shape[1]), lambda b: (rb + b, 0)),
                  pl.BlockSpec((None, None, bw, past), lambda b: (l, b, 0, 0)),
                  pl.BlockSpec((None, None, past * C_HEADS, LANE), lambda b: (l, b, 0, 0)),
                  pl.BlockSpec((None, 4, HEAD_DIM), lambda b: (l, 0, 0)),
                  _layer_vec(l, LANE)],
        out_block=(ts, bw), out_index=lambda b: (rb + b, 2),
        sem=("parallel",), name="diff_sample",
        args=(z, zb, _feature_major(cache_k, bw), cache_v.reshape(depth, nb, past * C_HEADS, LANE),
              c_lam_all, _vec3(c_on_all)))


def _rwkv_pre_kernel(zd_ref, prev_ref, first_ref, mu_ref, w0_ref, w2_ref, a0_ref, a2_ref, g2_ref, kkw_ref,
                     ka_ref, bd_ref, r_ref, w_ref, k_ref, v_ref, kk_ref, b_ref, g_ref, *, tm, t, ts):
    zf = zd_ref[...]
    grow = pl.program_id(0) * tm + lax.broadcasted_iota(jnp.int32, (tm, 1), 0)
    zs = jnp.where(grow % tm == 0, prev_ref[7:8, :], pltpu.roll(zf, 1, 0))
    seq_start = (grow >= t) & ((grow - t) % ts == 0)
    zs = jnp.where(seq_start, first_ref[...], zs)
    zs = jnp.where(grow == 0, 0.0, zs)
    zm = zf + (zs - zf) * mu_ref[...]
    bw = BRANCH_WIDTH
    r, k, v = zm[:, :bw], zm[:, bw:2 * bw], zm[:, 2 * bw:3 * bw]
    wl, al, gl = zm[:, 3 * bw:3 * bw + 64], zm[:, 3 * bw + 64:3 * bw + 128], zm[:, 3 * bw + 128:]
    dot = lambda a, b: jnp.dot(a.astype(BF16), b.astype(BF16), preferred_element_type=F32)
    u = -(w0_ref[...] + dot(jnp.tanh(wl), w2_ref[...]))
    softplus = jnp.maximum(u, 0.0) + jnp.log(1.0 + jnp.exp(-jnp.abs(u)))
    w = -softplus - 0.5
    a = jax.nn.sigmoid(a0_ref[...] + dot(al, a2_ref[...]))
    kk = k * kkw_ref[...]
    nrm = jnp.sqrt(_seg_sum(kk * kk, bd_ref[...]))
    kk = kk / jnp.maximum(nrm, 1e-12)
    r_ref[...] = r
    w_ref[...] = -jnp.exp(w)
    k_ref[...] = k * (1.0 + (a - 1.0) * ka_ref[...])
    v_ref[...] = v
    kk_ref[...] = kk
    b_ref[...] = kk * a
    g_ref[...] = dot(jax.nn.sigmoid(gl), g2_ref[...])


def _rwkv_pre(zd, zfirst, p, l, t, ts):
    m, dc = zd.shape
    tm = _tile(math.gcd(t, m - t), 512)
    bw = BRANCH_WIDTH
    npt = t // tm
    rows = pl.BlockSpec((tm, dc), lambda i: (i, 0))
    vec = lambda n: _layer_vec(l, n)
    mat = lambda k: pl.BlockSpec((None, k, bw), lambda i: (l, 0, 0))
    out = jax.ShapeDtypeStruct((m, bw), F32)
    return pl.pallas_call(
        functools.partial(_rwkv_pre_kernel, tm=tm, t=t, ts=ts),
        out_shape=[out] * 7,
        grid=(m // tm,),
        in_specs=[rows,
                  pl.BlockSpec((8, dc), lambda i: (jnp.maximum(i * (tm // 8) - 1, 0), 0)),
                  pl.BlockSpec((tm, dc), lambda i: (jnp.maximum(i - npt, 0), 0)),
                  vec(dc), vec(bw), mat(64), vec(bw), mat(64), mat(128), vec(bw), vec(bw),
                  pl.BlockSpec((bw, bw), lambda i: (0, 0))],
        out_specs=[pl.BlockSpec((tm, bw), lambda i: (i, 0))] * 7,
        compiler_params=_cparams("parallel"),
        name="rwkv_pre",
    )(zd, zd, zfirst, _vec3(p["d_mu"]), _vec3(p["d_w0"]), p["d_w2"], _vec3(p["d_a0"]), p["d_a2"], p["d_g2"],
      _vec3(p["d_k_k"]), _vec3(p["d_k_a"]), _ones_blockdiag(bw))


RW_CHUNK_MAX = 64
RW_BLOCK = 128
RW_SLOTS = BRANCH_WIDTH // LANE


def _dot3(a, b):
    ah, al = _split2(a)
    bh, bl = _split2(b)
    d = lambda x, y: jnp.dot(x, y, preferred_element_type=F32)
    return d(ah, bh) + d(al, bh) + d(ah, bl)


def _rwkv_chunk_kernel(*refs, seq_chunks, RW_CHUNK):
    if seq_chunks:
        (r_ref, lw_ref, k_ref, v_ref, kk_ref, b_ref, s0_ref, y_ref, sf_ref,
         h_ref, u_ref, ab_ref, rb_ref, u0_ref, y0_ref, bt_ref, kt_ref, eg_ref) = refs
    else:
        (r_ref, lw_ref, k_ref, v_ref, kk_ref, b_ref, y_ref, sf_ref,
         h_ref, u_ref, ab_ref, rb_ref, u0_ref, y0_ref, bt_ref, kt_ref, eg_ref) = refs
        s0_ref = None

        @pl.when(pl.program_id(0) == 0)
        def _():
            h_ref[...] = jnp.zeros_like(h_ref)

    n = RW_BLOCK
    n_chunks = n // RW_CHUNK
    row = lax.broadcasted_iota(jnp.int32, (n, n), 0)
    col = lax.broadcasted_iota(jnp.int32, (n, n), 1)
    same = (row // RW_CHUNK) == (col // RW_CHUNK)
    strict, incl = same & (col < row), same & (col <= row)
    eye = jnp.where(row == col, 1.0, 0.0)
    head_diag = (row // HEAD_DIM) == (col // HEAD_DIM)
    in_chunk = row % RW_CHUNK
    k_pick = jnp.where(lax.broadcasted_iota(jnp.int32, (n, HEAD_DIM), 0) % HEAD_DIM
                       == lax.broadcasted_iota(jnp.int32, (n, HEAD_DIM), 1), 1.0, 0.0)
    bf = lambda x: x.astype(BF16)
    mm = lambda x, y: jnp.dot(bf(x), bf(y), preferred_element_type=F32)
    u_ref[...] = jnp.zeros_like(u_ref)

    halves = [_half_mask((n, LANE), e) for e in range(2)]
    group = 2
    for j0 in range(0, RW_SLOTS, group):
        slot_data = []
        for j in range(j0, j0 + group):
            sl = slice(j * LANE, (j + 1) * LANE)
            lw, v = lw_ref[:, sl], v_ref[:, sl]
            g = lw
            for d in [1 << s for s in range(RW_CHUNK.bit_length() - 1)]:
                g = g + jnp.where(in_chunk >= d, pltpu.roll(g, d, 0), 0.0)
            inv_g = jnp.exp(-g)
            a_t = -kk_ref[:, sl] * jnp.exp(g - lw)
            b_t, k_t = b_ref[:, sl] * inv_g, k_ref[:, sl] * inv_g
            r_t = r_ref[:, sl] * jnp.exp(g)
            bt_ref[j], kt_ref[j], eg_ref[j] = b_t.T, k_t.T, jnp.exp(g).T
            slot_data.append((a_t, r_t, bf(b_t), bf(k_t), bf(v)))
        chains = [(s, e) for s in range(group) for e in range(2)]
        each = lambda f: [f(s, e, i) for i, (s, e) in enumerate(chains)]
        a_e = each(lambda s, e, i: bf(jnp.where(halves[e], slot_data[s][0], 0.0)))
        r_e = each(lambda s, e, i: bf(jnp.where(halves[e], slot_data[s][1], 0.0)))
        n_ab = each(lambda s, e, i: jnp.where(strict, _dot_nt(a_e[i], slot_data[s][2]), 0.0))
        n_ak = each(lambda s, e, i: jnp.where(strict, _dot_nt(a_e[i], slot_data[s][3]), 0.0))
        m_rb = each(lambda s, e, i: bf(jnp.where(incl, _dot_nt(r_e[i], slot_data[s][2]), 0.0)))
        m_rk = each(lambda s, e, i: bf(jnp.where(incl, _dot_nt(r_e[i], slot_data[s][3]), 0.0)))
        w_e = each(lambda s, e, i: mm(n_ak[i], slot_data[s][4]))
        tinv = each(lambda s, e, i: eye + n_ab[i])
        pw = n_ab
        for _ in range(RW_CHUNK.bit_length() - 2):
            pw = each(lambda s, e, i: mm(pw[i], pw[i]))
            tinv = each(lambda s, e, i: tinv[i] + mm(tinv[i], pw[i]))
        t16 = each(lambda s, e, i: bf(tinv[i]))
        a_bar = each(lambda s, e, i: mm(t16[i], slot_data[s][0]))
        u0 = each(lambda s, e, i: mm(t16[i], w_e[i]))
        r_bar = each(lambda s, e, i: slot_data[s][1] + mm(m_rb[i], a_bar[i]))
        y0 = each(lambda s, e, i: mm(m_rb[i], u0[i]) + mm(m_rk[i], slot_data[s][4]))
        for s in range(group):
            pick = lambda vals: jnp.where(halves[0], vals[2 * s], vals[2 * s + 1])
            j = j0 + s
            ab_ref[j], rb_ref[j], u0_ref[j], y0_ref[j] = pick(a_bar), pick(r_bar), pick(u0), pick(y0)

    for c in range(n_chunks):
        rows = slice(c * RW_CHUNK, (c + 1) * RW_CHUNK)
        col_c = (col // RW_CHUNK) == c
        for j in range(RW_SLOTS):
            sl = slice(j * LANE, (j + 1) * LANE)
            if seq_chunks and c % seq_chunks == 0:
                x = s0_ref[c // seq_chunks, sl, :]
                h = jnp.where(head_diag, _dot_nt3(k_pick, x), 0.0)
            else:
                h = h_ref[j]
            res = _dot3(jnp.concatenate([ab_ref[j, rows, :], rb_ref[j, rows, :]], axis=0), h)
            u_c = res[:RW_CHUNK] + u0_ref[j, rows, :]
            y_ref[rows, sl] = res[RW_CHUNK:] + y0_ref[j, rows, :]
            u_ref[j, rows, :] = u_c
            bk = jnp.concatenate([jnp.where(col_c, bt_ref[j], 0.0), jnp.where(col_c, kt_ref[j], 0.0)], axis=1)
            uv = jnp.concatenate([u_ref[j], v_ref[:, sl]], axis=0)
            inc = jnp.where(head_diag, mm(bk, uv), 0.0)
            g_end = eg_ref[j, :, (c + 1) * RW_CHUNK - 1:(c + 1) * RW_CHUNK]
            h = g_end * (h + inc)
            h_ref[j] = h
            if seq_chunks and (c + 1) % seq_chunks == 0:
                ht = h.T
                sf_ref[c // seq_chunks, sl, :] = (ht + pltpu.roll(ht, HEAD_DIM, 1))[:, :HEAD_DIM]

    if not seq_chunks:
        @pl.when(pl.program_id(0) == pl.num_programs(0) - 1)
        def _():
            for j in range(RW_SLOTS):
                ht = h_ref[j].T
                sf_ref[0, j * LANE:(j + 1) * LANE, :] = (ht + pltpu.roll(ht, HEAD_DIM, 1))[:, :HEAD_DIM]


def _rwkv_chunked(ops, s0, row0, n_seq, t):
    n = RW_BLOCK
    bw = BRANCH_WIDTH
    chunk = math.gcd(t, RW_CHUNK_MAX)
    assert row0 % n == 0 and (n_seq * t) % n == 0 and chunk >= 8 and chunk & (chunk - 1) == 0
    rb = row0 // n
    rows = pl.BlockSpec((n, bw), lambda i: (rb + i, 0))
    if s0 is None:
        assert n_seq == 1
        seq_chunks, per_blk, extra, extra_specs = 0, 1, [], []
        sf_spec = pl.BlockSpec((1, bw, HEAD_DIM), lambda i: (0, 0, 0))
    else:
        assert n % t == 0
        seq_chunks, per_blk = t // chunk, n // t
        extra = [s0.reshape(n_seq, bw, HEAD_DIM)]
        extra_specs = [pl.BlockSpec((per_blk, bw, HEAD_DIM), lambda i: (i, 0, 0))]
        sf_spec = pl.BlockSpec((per_blk, bw, HEAD_DIM), lambda i: (i, 0, 0))
    slot = lambda dt=F32: pltpu.VMEM((RW_SLOTS, n, LANE), dt)
    y, sf = pl.pallas_call(
        functools.partial(_rwkv_chunk_kernel, seq_chunks=seq_chunks, RW_CHUNK=chunk),
        out_shape=[jax.ShapeDtypeStruct((n_seq * t, bw), F32), jax.ShapeDtypeStruct((n_seq, bw, HEAD_DIM), F32)],
        grid=(n_seq * t // n,),
        in_specs=[rows] * 6 + extra_specs,
        out_specs=[pl.BlockSpec((n, bw), lambda i: (i, 0)), sf_spec],
        scratch_shapes=[slot() for _ in range(9)],
        compiler_params=_cparams("arbitrary"),
        name="rwkv_chunked",
    )(*ops, *extra)
    return y, sf.reshape(n_seq, D_HEADS, HEAD_DIM, HEAD_DIM)


def _rwkv_post_kernel(y_ref, r_ref, k_ref, v_ref, g_ref, lnw_ref, lnb_ref, rk_ref, bd_ref, o_ref):
    bd = bd_ref[...]
    y = y_ref[...]
    mean = _seg_sum(y, bd) * (1.0 / HEAD_DIM)
    yc = y - mean
    var = _seg_sum(yc * yc, bd) * (1.0 / HEAD_DIM)
    yn = yc * lax.rsqrt(var + D_GN_EPS) * lnw_ref[...] + lnb_ref[...]
    bonus = _seg_sum(r_ref[...] * k_ref[...] * rk_ref[...], bd) * v_ref[...]
    o_ref[...] = ((yn + bonus) * g_ref[...]).astype(o_ref.dtype)


def _rwkv_post(y, r, k, v, g, p, l, o_all):
    m, bw = y.shape
    tm = _tile(m, 512)
    rows = pl.BlockSpec((tm, bw), lambda i: (i, 0))
    vec = _layer_vec(l, bw)
    return _mixer_call(
        _rwkv_post_kernel, o_all, m,
        grid=(m // tm,),
        in_specs=[rows] * 5 + [vec] * 3 + [pl.BlockSpec((bw, bw), lambda i: (0, 0))],
        out_block=(tm, bw), out_index=lambda i: (i, N_BRANCH - 1),
        sem=("parallel",), name="rwkv_post",
        args=(y, r, k, v, g, _vec3(p["d_ln_w"]), _vec3(p["d_ln_b"]), _vec3(p["d_r_k"]), _ones_blockdiag(bw)))


def _merge_kernel(o_ref, wbr_ref, h_ref, wg_ref, out_ref, acc_ref):
    n = pl.program_id(2)

    @pl.when(n == 0)
    def _():
        acc_ref[...] = jnp.zeros_like(acc_ref)

    u = jnp.dot(o_ref[...], wbr_ref[...].astype(BF16), preferred_element_type=F32)
    gate = jax.nn.sigmoid(jnp.dot(h_ref[...], wg_ref[...], preferred_element_type=F32))
    acc_ref[...] += u * gate

    @pl.when(n == N_BRANCH - 1)
    def _():
        out_ref[...] = acc_ref[...].astype(out_ref.dtype)


def _merge(o_all, w_br_all, h, w_pack, l):
    m, d = h.shape
    tm, tn = _tile(m, 1024), 512
    nj = d // tn
    g0 = S_GATE * LANE // tn
    return pl.pallas_call(
        _merge_kernel,
        out_shape=jax.ShapeDtypeStruct((m, d), BF16),
        grid=(m // tm, nj, N_BRANCH),
        in_specs=[pl.BlockSpec((tm, BRANCH_WIDTH), lambda i, j, n: (i, n)),
                  pl.BlockSpec((None, None, BRANCH_WIDTH, tn), lambda i, j, n: (l, n, 0, j)),
                  pl.BlockSpec((tm, d), lambda i, j, n: (i, 0)),
                  pl.BlockSpec((None, d, tn), lambda i, j, n: (l, 0, g0 + n * nj + j))],
        out_specs=pl.BlockSpec((tm, tn), lambda i, j, n: (i, j)),
        scratch_shapes=[pltpu.VMEM((tm, tn), F32)],
        compiler_params=_cparams("parallel", "parallel", "arbitrary"),
        name="branch_merge",
    )(o_all, w_br_all, h, w_pack)


def _ple_kernel(x_ref, pe_ref, wple_ref, h_ref, wg_ref, o_ref):
    emb = jnp.dot(pe_ref[...].astype(BF16), wple_ref[...].astype(BF16), preferred_element_type=F32)
    gate = jax.nn.sigmoid(jnp.dot(h_ref[...], wg_ref[...].astype(BF16), preferred_element_type=F32))
    o_ref[...] = x_ref[...] + emb * gate


def _ple(x, pe, w_ple_all, h, w_gate_all, l):
    m, d = x.shape
    pd = pe.shape[1]
    tm, tn = _tile(m, 1024), 512
    return pl.pallas_call(
        _ple_kernel,
        out_shape=jax.ShapeDtypeStruct((m, d), F32),
        grid=(m // tm, d // tn),
        in_specs=[pl.BlockSpec((tm, tn), lambda i, j: (i, j)),
                  pl.BlockSpec((tm, pd), lambda i, j: (i, 0)),
                  pl.BlockSpec((None, pd, tn), lambda i, j: (l, 0, j)),
                  pl.BlockSpec((tm, d), lambda i, j: (i, 0)),
                  pl.BlockSpec((None, d, tn), lambda i, j: (l, 0, j))],
        out_specs=pl.BlockSpec((tm, tn), lambda i, j: (i, j)),
        compiler_params=_cparams("parallel", "parallel"),
        name="ple",
    )(x, pe, w_ple_all, h, w_gate_all)


def _layer(l, x, pe, p, t, nb, ts, tabs):
    m = x.shape[0]
    lam_init = 0.8 - 0.6 * math.exp(-0.3 * l)
    dc = p["d_mu"].shape[1]
    d = x.shape[1]
    lw = {"a_qn": p["a_q_norm"][l], "a_kn": p["a_k_norm"][l], "b_qn": p["b_q_norm"][l], "b_kn": p["b_k_norm"][l],
          "c_qn": p["c_q_norm"][l], "c_kn": p["c_k_norm"][l]}

    h = _rmsnorm(x, p["norm1_g"], l)
    w_pack = p["w_pack"]
    z, zb = _qkv_projection(h, w_pack, l, *_column_vectors(lw), *tabs)
    zd = _matmul(h, w_pack, lambda acc: acc, F32, n=dc, layer=l, col0=S_D * LANE, tn=256, name="rwkv_projection")

    o_all = _dsa_prompt(z, zb, t, None, m)
    o_all = _dsa_sample(z, zb, p["cache_a_k"], p["cache_a_v"], p["cache_a_kidx"], l, t, nb, ts, o_all, m)
    tq_b = 128
    bias = _band_bias(p["b_rel_bias"][l], tq_b)
    pb = p["cache_b_k"].shape[2]
    generic = bias[B_WINDOW // tq_b]
    o_all = _band_prompt(z, zb, bias, t, tq_b, o_all, m)
    o_all = _band_sample(z, zb, p["cache_b_k"], p["cache_b_v"], generic[:, :ts, B_WINDOW - pb:B_WINDOW],
                         generic[:, :ts, B_WINDOW:B_WINDOW + ts], l, t, nb, ts, o_all, m)
    c_on = p["c_out_norm"]
    o_all = _diff_prompt(z, zb, p["c_lambda"], c_on, l, t, lam_init, o_all, m)
    o_all = _diff_sample(z, zb, p["cache_c_k"], p["cache_c_v"], p["c_lambda"], c_on, l, t, nb, ts, lam_init,
                         o_all, m)

    zfirst = jnp.broadcast_to(p["state_d_shift"][l], (nb, ts, dc)).reshape(nb * ts, dc)
    r, w, k, v, kk, b, g = _rwkv_pre(zd, zfirst, p, l, t, ts)
    ops = (r, w, k, v, kk, b)
    y_p, wkv_p = _rwkv_chunked(ops, None, 0, 1, t)
    y_s, wkv_s = _rwkv_chunked(ops, p["state_d_wkv"][l], t, nb, ts)
    o_all = _rwkv_post(jnp.concatenate([y_p, y_s], axis=0), r, k, v, g, p, l, o_all)

    ug = _merge(o_all, p["w_branch"], h, w_pack, l)
    res = lambda acc, r_: r_ + acc
    x = _matmul(ug, p["w_out"], res, F32, n=d, layer=l, residual=x, name="out_proj")
    h2 = _rmsnorm(x, p["norm2_g"], l)
    ffn = p["w_up"].shape[2]
    up = _matmul(h2, p["w_up"], lambda acc: jnp.square(jnp.maximum(acc, 0.0)), BF16, n=ffn, layer=l, name="mlp_up")
    x = _matmul(up, _cast_bf16(p["w_down"], l), res, F32, n=d, residual=x, tn=1024, name="mlp_down")
    h3 = _rmsnorm(x, p["norm3_g"], l)
    x = _ple(x, pe, p["w_ple"], h3, p["w_ple_gate"], l)

    slot = lambda rows, s, n=1: z[rows, s * LANE:(s + n) * LANE]

    def rows_of(sl, lead):
        a = lambda s, n, shape: slot(sl, s, n).reshape(lead + shape)
        ak = a(S_AK, 1, (A_KV_HEADS, HEAD_DIM))
        av = a(S_AV, 1, (A_KV_HEADS, HEAD_DIM))
        aik = slot(sl, S_AIK)[:, :HEAD_DIM].reshape(lead + (HEAD_DIM,))
        bk = a(S_BK, 4, (B_HEADS, HEAD_DIM))
        bv = a(S_BV, 4, (B_HEADS, HEAD_DIM))
        ck = a(S_CK, 4, (C_HEADS, 2, HEAD_DIM))
        cv = a(S_CV, 4, (C_HEADS, 2 * HEAD_DIM))
        return ak, av, aik, bk, bv, ck, cv

    keep = min(B_WINDOW, t)
    pak, pav, paik, pbk, pbv, pck, pcv = rows_of(slice(0, t), (1, t))
    new_p = (pak, pav, paik, pbk[:, t - keep:], pbv[:, t - keep:], pck, pcv, wkv_p, zd[t - 1:t].reshape(1, 1, dc))
    new_s = rows_of(slice(t, m), (nb, ts)) + (wkv_s, zd[t:].reshape(nb, ts, dc)[:, -1:])
    return x, new_p, new_s


def kernel(x_prompt, x_sample, cache_a_k, cache_a_v, cache_a_kidx, cache_b_k, cache_b_v, cache_c_k, cache_c_v, state_d_wkv, state_d_shift, p_prompt, p_sample, norm1_g, w_in, a_q_norm, a_k_norm, b_q_norm, b_k_norm, b_rel_bias, c_q_norm, c_k_norm, c_lambda, c_out_norm, d_mu, d_w0, d_w2, d_a0, d_a2, d_g2, d_k_k, d_k_a, d_r_k, d_ln_w, d_ln_b, w_branch, w_out, norm2_g, w_up, w_down, norm3_g, w_ple, w_ple_gate):
    batch, t, d = x_prompt.shape
    nb, ts, _ = x_sample.shape
    past = cache_a_k.shape[2]
    depth = w_in.shape[0]
    assert batch == 1 and t % 512 == 0 and past % CHUNK == 0 and ts <= CHUNK and (nb * ts) % 8 == 0
    p = dict(cache_a_k=cache_a_k, cache_a_v=cache_a_v, cache_a_kidx=cache_a_kidx, cache_b_k=cache_b_k,
             cache_b_v=cache_b_v, cache_c_k=cache_c_k, cache_c_v=cache_c_v, state_d_wkv=state_d_wkv,
             state_d_shift=state_d_shift, norm1_g=norm1_g, w_in=w_in, a_q_norm=a_q_norm, a_k_norm=a_k_norm,
             b_q_norm=b_q_norm, b_k_norm=b_k_norm, b_rel_bias=b_rel_bias, c_q_norm=c_q_norm, c_k_norm=c_k_norm,
             c_lambda=c_lambda, c_out_norm=c_out_norm, d_mu=d_mu, d_w0=d_w0, d_w2=d_w2, d_a0=d_a0, d_a2=d_a2,
             d_g2=d_g2, d_k_k=d_k_k, d_k_a=d_k_a, d_r_k=d_r_k, d_ln_w=d_ln_w, d_ln_b=d_ln_b, w_branch=w_branch,
             w_out=w_out, norm2_g=norm2_g, w_up=w_up, w_down=w_down, norm3_g=norm3_g, w_ple=w_ple,
             w_ple_gate=w_ple_gate)
    p["w_pack"] = _pack_w_in(w_in)
    x = jnp.concatenate([x_prompt[0], x_sample.reshape(nb * ts, d)], axis=0)
    pos = jnp.concatenate([jnp.arange(t, dtype=jnp.int32),
                           jnp.tile(past + jnp.arange(ts, dtype=jnp.int32), nb)])
    tabs = _rope_tables(pos)
    st_p = [[] for _ in range(9)]
    st_s = [[] for _ in range(9)]
    for l in range(depth):
        pe = jnp.concatenate([p_prompt[l, 0], p_sample[l].reshape(nb * ts, -1)], axis=0)
        x, new_p, new_s = _layer(l, x, pe, p, t, nb, ts, tabs)
        for lst, arr in zip(st_p, new_p):
            lst.append(arr)
        for lst, arr in zip(st_s, new_s):
            lst.append(arr)
    outs_p = [jnp.stack(s, axis=0) for s in st_p]
    outs_s = [jnp.stack(s, axis=0) for s in st_s]
    return (x[:t].reshape(1, t, d), x[t:].reshape(nb, ts, d), *outs_p, *outs_s)
```

```python
import functools
import math

import numpy as np
import jax
import jax.numpy as jnp
from jax import lax
from jax.experimental import pallas as pl
from jax.experimental.pallas import tpu as pltpu

F32 = jnp.float32
BF16 = jnp.bfloat16

CHUNK = 64
HEAD_DIM = 64
ROPE_DIM = 16
ROPE_THETA = 500000.0
N_BRANCH = 4
BRANCH_WIDTH = 512
A_HEADS, A_KV_HEADS, A_IDX_HEADS = 8, 2, 4
A_TOPK_MAX = 256
B_HEADS = 8
B_PAST_CHUNKS = 8
B_WINDOW = B_PAST_CHUNKS * CHUNK
B_REL_CLIP = 128
C_HEADS = 4
D_HEADS = 8
D_GN_EPS = 64e-5
NORM_EPS = 1e-6

LANE = 128
VMEM_LIMIT = 48 * 1024 * 1024

S_AQ, S_AK, S_AV, S_AIQ, S_AIK = 0, 4, 5, 6, 8
S_BQ, S_BK, S_BV = 10, 14, 18
S_CQ, S_CK, S_CV = 22, 26, 30
S_D, S_GATE = 34, 48
N_QKV_SLOTS = 34
N_D_SLOTS = 14
A_COLS = 1092
A_SLOTS = 9
SHIFT = A_COLS - (A_SLOTS - 1) * LANE

NEG_KEY = -2139095041
INT_MIN = -2147483648
M_FLOOR = -1e30


def _cparams(*sem):
    return pltpu.CompilerParams(dimension_semantics=sem, vmem_limit_bytes=VMEM_LIMIT)


def _tile(n, pref):
    t = min(n, pref)
    while n % t:
        t -= 8
    return t


def _split2(x):
    hi = x.astype(BF16)
    lo = (x - hi.astype(F32)).astype(BF16)
    return hi, lo


def _seg_sum(x, ones_bd):
    hi, lo = _split2(x)
    return (jnp.dot(hi, ones_bd, preferred_element_type=F32)
            + jnp.dot(lo, ones_bd, preferred_element_type=F32))


def _dot_nt(a, b):
    return lax.dot_general(a, b, (((1,), (1,)), ((), ())), preferred_element_type=F32)


def _dot_nt3(a, b):
    ah, al = _split2(a)
    bh, bl = _split2(b)
    return _dot_nt(ah, bh) + _dot_nt(ah, bl) + _dot_nt(al, bh)


def _ones_blockdiag(n):
    i = np.arange(n)
    return jnp.asarray((i[:, None] // HEAD_DIM) == (i[None, :] // HEAD_DIM), dtype=BF16)


def _vec3(a):
    return a.reshape(a.shape[0], 1, -1)


def _layer_vec(l, n):
    return pl.BlockSpec((None, 1, n), lambda *_: (l, 0, 0))


def _half_mask(shape, half):
    lane = lax.broadcasted_iota(jnp.int32, shape, len(shape) - 1)
    return (lane < HEAD_DIM) if half == 0 else (lane >= HEAD_DIM)


def _rms_kernel(x_ref, g_ref, o_ref):
    x = x_ref[...]
    ms = jnp.mean(x * x, axis=-1, keepdims=True)
    o_ref[...] = (x * lax.rsqrt(ms + NORM_EPS) * g_ref[...]).astype(o_ref.dtype)


def _rmsnorm(x, g_all, l):
    m, d = x.shape
    tm = _tile(m, 512)
    return pl.pallas_call(
        _rms_kernel,
        out_shape=jax.ShapeDtypeStruct((m, d), BF16),
        grid=(m // tm,),
        in_specs=[pl.BlockSpec((tm, d), lambda i: (i, 0)), _layer_vec(l, d)],
        out_specs=pl.BlockSpec((tm, d), lambda i: (i, 0)),
        compiler_params=_cparams("parallel"),
        name="rmsnorm",
    )(x, _vec3(g_all))


def _pack_kernel(a_ref, b_ref, o_ref):
    j = pl.program_id(0)
    row = lax.broadcasted_iota(jnp.int32, (LANE, a_ref.shape[2]), 0)
    for l in range(a_ref.shape[1]):
        a, b = a_ref[:, l, :], b_ref[:, l, :]
        shifted = jnp.where(row < LANE - SHIFT, pltpu.roll(a, LANE - SHIFT, 0), pltpu.roll(b, LANE - SHIFT, 0))
        out = jnp.where(j < A_SLOTS, a, jnp.where(j == A_SLOTS, 0.0, shifted))
        o_ref[l] = out.T.astype(o_ref.dtype)


def _pack_w_in(w_in_all):
    depth, d, n_in = w_in_all.shape
    n_slots = A_SLOTS + 1 + (n_in - A_COLS) // LANE
    assert (n_in - A_COLS) % LANE == 0 and 0 < SHIFT < LANE
    src_a = lambda j: jnp.where(j < A_SLOTS, j, j - 2)
    src_b = lambda j: jnp.where(j < A_SLOTS, j, j - 1)
    w_t = jnp.transpose(w_in_all, (2, 0, 1))
    return pl.pallas_call(
        _pack_kernel,
        out_shape=jax.ShapeDtypeStruct((depth, d, n_slots * LANE), BF16),
        grid=(n_slots,),
        in_specs=[pl.BlockSpec((LANE, depth, d), lambda j: (src_a(j), 0, 0)),
                  pl.BlockSpec((LANE, depth, d), lambda j: (src_b(j), 0, 0))],
        out_specs=pl.BlockSpec((depth, d, LANE), lambda j: (0, 0, j)),
        compiler_params=_cparams("parallel"),
        name="pack_w_in",
    )(w_t, w_t)


def _mm_kernel(*refs, n_extra, nk, epilogue):
    a_ref, w_ref = refs[0], refs[1]
    extra = refs[2:2 + n_extra]
    o_ref = refs[2 + n_extra]
    if nk == 1:
        acc = jnp.dot(a_ref[...], w_ref[...].astype(BF16), preferred_element_type=F32)
        o_ref[...] = epilogue(acc, *[e[...] for e in extra]).astype(o_ref.dtype)
        return
    acc_ref = refs[3 + n_extra]
    k = pl.program_id(2)

    @pl.when(k == 0)
    def _():
        acc_ref[...] = jnp.zeros_like(acc_ref)

    acc_ref[...] += jnp.dot(a_ref[...], w_ref[...].astype(BF16), preferred_element_type=F32)

    @pl.when(k == nk - 1)
    def _():
        o_ref[...] = epilogue(acc_ref[...], *[e[...] for e in extra]).astype(o_ref.dtype)


def _matmul(a, w, epilogue, out_dtype, *, n, layer=None, col0=0, residual=None,
            tm=1024, tn=512, tk=2048, name="matmul"):
    m, kdim = a.shape
    tm, tn, tk = _tile(m, tm), _tile(n, tn), _tile(kdim, tk)
    nk = kdim // tk
    cb = col0 // tn
    assert col0 % tn == 0
    if layer is None:
        w_spec = pl.BlockSpec((tk, tn), lambda i, j, k: (k, cb + j))
    else:
        w_spec = pl.BlockSpec((None, tk, tn), lambda i, j, k: (layer, k, cb + j))
    specs = [pl.BlockSpec((tm, tk), lambda i, j, k: (i, k)), w_spec]
    extras = []
    if residual is not None:
        specs.append(pl.BlockSpec((tm, tn), lambda i, j, k: (i, j)))
        extras.append(residual)
    return pl.pallas_call(
        functools.partial(_mm_kernel, n_extra=len(extras), nk=nk, epilogue=epilogue),
        out_shape=jax.ShapeDtypeStruct((m, n), out_dtype),
        grid=(m // tm, n // tn, nk),
        in_specs=specs,
        out_specs=pl.BlockSpec((tm, tn), lambda i, j, k: (i, j)),
        scratch_shapes=[pltpu.VMEM((tm, tn), F32)] if nk > 1 else [],
        compiler_params=_cparams("parallel", "parallel", "arbitrary"),
        name=name,
    )(a, w, *extras)


def _cast_kernel(w_ref, o_ref):
    o_ref[...] = w_ref[...].astype(o_ref.dtype)


def _cast_bf16(w_all, l):
    _, kdim, n = w_all.shape
    tk = _tile(kdim, 512)
    return pl.pallas_call(
        _cast_kernel,
        out_shape=jax.ShapeDtypeStruct((kdim, n), BF16),
        grid=(kdim // tk,),
        in_specs=[pl.BlockSpec((None, tk, n), lambda i: (l, i, 0))],
        out_specs=pl.BlockSpec((tk, n), lambda i: (i, 0)),
        compiler_params=_cparams("parallel"),
        name="cast_bf16",
    )(w_all)


def _proj_kernel(h_ref, w_ref, gain_ref, nf_ref, rf_ref, cos_ref, sa_ref, sb_ref, bd_ref, o_ref, ob_ref, xs_ref,
                 *, sub, plain_tiles, norm_tiles):
    w, bd = w_ref[...], bd_ref[...]
    xs_ref[...] = jnp.dot(h_ref[...], w, preferred_element_type=F32)

    def epilogue(with_norm, with_rope):
        normed = nf_ref[...] > 0.5
        gain, rf = gain_ref[...], rf_ref[...]
        for c in range(h_ref.shape[0] // sub):
            rows = slice(c * sub, (c + 1) * sub)
            y = xs_ref[rows, :]
            if with_norm:
                ms = _seg_sum(y * y, bd) * (1.0 / HEAD_DIM)
                y = y * jnp.where(normed, lax.rsqrt(ms + NORM_EPS) * gain, 1.0)
            if with_rope:
                cosv, sav, sbv = cos_ref[rows, :], sa_ref[rows, :], sb_ref[rows, :]
            for s in range(2):
                out = y[:, s * LANE:(s + 1) * LANE]
                if with_rope:
                    f = rf[:, s * LANE:(s + 1) * LANE]
                    roped = out * cosv + pltpu.roll(out, 8, 1) * sav + pltpu.roll(out, LANE - 8, 1) * sbv
                    out = jnp.where(f > 0.5, roped, out)
                o_ref[rows, s * LANE:(s + 1) * LANE] = out
                ob_ref[rows, s * LANE:(s + 1) * LANE] = out.astype(BF16)

    j = pl.program_id(1)
    any_of = lambda tiles: functools.reduce(jnp.logical_or, [j == tl for tl in tiles])
    is_plain, is_norm = any_of(plain_tiles), any_of(norm_tiles)
    pl.when(is_plain)(lambda: epilogue(False, False))
    pl.when(is_norm)(lambda: epilogue(True, False))
    pl.when(jnp.logical_not(jnp.logical_or(is_plain, is_norm)))(lambda: epilogue(True, True))


def _qkv_projection(h, w_pack, l, gain, nf, rf, cos_t, sin_a, sin_b):
    m, d = h.shape
    tm, tn = _tile(m, 1024), 2 * LANE
    zw = N_QKV_SLOTS * LANE
    row = lambda i, j: (0, j)
    tab = lambda i, j: (i, 0)
    no_rope = set(range(S_BQ, S_CQ)) | {S_AV, S_AIK + 1} | set(range(S_CV, N_QKV_SLOTS))
    no_norm = set(range(S_AV, S_BQ)) | set(range(S_BV, S_CQ)) | set(range(S_CV, N_QKV_SLOTS))
    tiles = range(N_QKV_SLOTS // 2)
    plain = tuple(tl for tl in tiles if {2 * tl, 2 * tl + 1} <= (no_rope & no_norm))
    norm_only = tuple(tl for tl in tiles if {2 * tl, 2 * tl + 1} <= no_rope and tl not in plain)
    return pl.pallas_call(
        functools.partial(_proj_kernel, sub=_tile(tm, 256), plain_tiles=plain, norm_tiles=norm_only),
        out_shape=[jax.ShapeDtypeStruct((m, zw), F32), jax.ShapeDtypeStruct((m, zw), BF16)],
        grid=(m // tm, zw // tn),
        in_specs=[pl.BlockSpec((tm, d), lambda i, j: (i, 0)), pl.BlockSpec((None, d, tn), lambda i, j: (l, 0, j)),
                  pl.BlockSpec((1, tn), row), pl.BlockSpec((1, tn), row), pl.BlockSpec((1, tn), row),
                  pl.BlockSpec((tm, LANE), tab), pl.BlockSpec((tm, LANE), tab), pl.BlockSpec((tm, LANE), tab),
                  pl.BlockSpec((tn, tn), lambda i, j: (0, 0))],
        out_specs=[pl.BlockSpec((tm, tn), lambda i, j: (i, j))] * 2,
        scratch_shapes=[pltpu.VMEM((tm, tn), F32)],
        compiler_params=_cparams("parallel", "arbitrary"),
        name="qkv_projection",
    )(h, w_pack, gain, nf, rf, cos_t, sin_a, sin_b, _ones_blockdiag(tn))


def _column_vectors(lw):
    f = lambda v: jnp.asarray(v, F32).reshape(-1)
    ones = lambda n: jnp.ones((n,), F32)
    zeros = lambda n: jnp.zeros((n,), F32)
    rep = lambda v, n: jnp.tile(f(v), n)
    groups = [
        (rep(lw["a_qn"], 8), 1.0, 1.0), (rep(lw["a_kn"], 2), 1.0, 1.0), (ones(128), 0.0, 0.0),
        (ones(256), 0.0, 1.0), (ones(64), 0.0, 1.0), (ones(64 + LANE), 0.0, 0.0),
        (rep(lw["b_qn"], 8), 1.0, 0.0), (rep(lw["b_kn"], 8), 1.0, 0.0), (ones(512), 0.0, 0.0),
        (rep(lw["c_qn"], 4), 1.0, 1.0), (rep(lw["c_kn"], 4), 1.0, 1.0), (ones(512), 0.0, 0.0)]
    gain = jnp.concatenate([g for g, _, _ in groups]).reshape(1, -1)
    nf = jnp.concatenate([ones(g.shape[0]) * a for g, a, _ in groups]).reshape(1, -1)
    rf = jnp.concatenate([ones(g.shape[0]) * b for g, _, b in groups]).reshape(1, -1)
    del zeros
    return gain, nf, rf


def _rope_tables(pos):
    half = ROPE_DIM // 2
    inv_freq = ROPE_THETA ** (-jnp.arange(0, ROPE_DIM, 2, dtype=F32) / ROPE_DIM)
    ang = pos.astype(F32)[:, None] * inv_freq[None, :]
    cos, sin = jnp.cos(ang), jnp.sin(ang)
    rows = pos.shape[0]
    one = jnp.ones((rows, HEAD_DIM - ROPE_DIM), F32)
    zero = jnp.zeros((rows, HEAD_DIM - ROPE_DIM), F32)
    z8 = jnp.zeros((rows, half), F32)
    cos_h = jnp.concatenate([cos, cos, one], axis=1)
    sa_h = jnp.concatenate([z8, sin, zero], axis=1)
    sb_h = jnp.concatenate([-sin, z8, zero], axis=1)
    dup = lambda t: jnp.concatenate([t, t], axis=1)
    return dup(cos_h), dup(sa_h), dup(sb_h)


def _to_key(score):
    bits = pltpu.bitcast(score + 0.0, jnp.int32)
    return jnp.where(bits < 0, bits ^ 0x7FFFFFFF, bits)


def _indexer_scores(iq, iw, ik):
    kk = ik[:, :HEAD_DIM]
    sc = None
    for hd in range(A_IDX_HEADS):
        logit = _dot_nt3(iq[:, hd * HEAD_DIM:(hd + 1) * HEAD_DIM], kk)
        wgt = iw[:, HEAD_DIM + hd:HEAD_DIM + hd + 1] * (A_IDX_HEADS ** -0.5 * HEAD_DIM ** -0.5)
        term = jnp.maximum(logit, 0.0) * wgt
        sc = term if sc is None else sc + term
    return sc


def _lane_fold(x, op):
    out = x[:, :LANE]
    for s in range(1, x.shape[1] // LANE):
        out = op(out, x[:, s * LANE:(s + 1) * LANE])
    return out


def _count(key_ref, n_blk, blk, pred):
    rows = key_ref.shape[0]
    grp = min(rows, LANE)
    parts = []
    for r0 in range(0, rows, grp):
        def body(b, acc, r0=r0):
            start = pl.multiple_of(b * blk, blk)
            kb = key_ref[r0:r0 + grp, pl.ds(start, blk)]
            return acc + _lane_fold(jnp.where(pred(kb, start, slice(r0, r0 + grp)), 1.0, 0.0), jnp.add)

        parts.append(lax.fori_loop(0, n_blk, body, jnp.zeros((grp, LANE), F32)))
    acc = parts[0] if len(parts) == 1 else jnp.concatenate(parts, axis=0)
    return jnp.sum(acc, axis=-1, keepdims=True)


def _topk_threshold(key_ref, n_blk, blk, topk):
    rows = key_ref.shape[0]
    kf = float(topk)
    c0 = _count(key_ref, n_blk, blk, lambda kb, st, rs: kb >= 0)
    ans = jnp.where(c0 >= kf, 0, INT_MIN).astype(jnp.int32)

    def bit_step(it, ans):
        cand = ans + jnp.left_shift(jnp.int32(1), 30 - it)
        c = _count(key_ref, n_blk, blk, lambda kb, st, rs: kb >= cand[rs])
        return jnp.where(c >= kf, cand, ans)

    thr = lax.fori_loop(0, 31, bit_step, ans)
    n_gt = _count(key_ref, n_blk, blk, lambda kb, st, rs: kb > thr[rs])
    n_eq = _count(key_ref, n_blk, blk, lambda kb, st, rs: kb == thr[rs])
    need = kf - n_gt
    nbits = int(n_blk * blk).bit_length() if isinstance(n_blk, int) else 14
    cut_all = jnp.full((rows, 1), 1 << nbits, jnp.int32)
    tie_overflow = jnp.max(jnp.where((n_eq > need) & (thr != NEG_KEY), 1.0, 0.0)) > 0.5

    def search_cut():
        def cut_step(it, cut):
            cand = cut + jnp.left_shift(jnp.int32(1), nbits - 1 - it)

            def pred(kb, st, rs):
                idx = st + lax.broadcasted_iota(jnp.int32, kb.shape, 1)
                return (kb == thr[rs]) & (idx < cand[rs])

            c = _count(key_ref, n_blk, blk, pred)
            return jnp.where(c <= need, cand, cut)

        return lax.fori_loop(0, nbits, cut_step, jnp.zeros((rows, 1), jnp.int32))

    cut = lax.cond(tie_overflow, search_cut, lambda: cut_all)
    return thr, cut


def _selected(kb, first_idx, thr, cut):
    idx = first_idx + lax.broadcasted_iota(jnp.int32, kb.shape, 1)
    return ((kb > thr) | ((kb == thr) & (idx < cut))) & (kb > NEG_KEY)


def _pair_heads(o_even, o_odd, group_half):
    lane = lax.broadcasted_iota(jnp.int32, o_even.shape, 1)
    if group_half == 0:
        return jnp.where(lane < HEAD_DIM, o_even, pltpu.roll(o_odd, HEAD_DIM, 1))
    return jnp.where(lane < HEAD_DIM, pltpu.roll(o_even, HEAD_DIM, 1), o_odd)


def _dsa_query_heads(q_ref, qh_ref):
    for hd in range(A_HEADS):
        slot, half = hd // 2, hd % 2
        group = hd // (A_HEADS // A_KV_HEADS)
        qs = q_ref[:, slot * LANE:(slot + 1) * LANE] * (HEAD_DIM ** -0.5)
        if half != group:
            qs = pltpu.roll(qs, HEAD_DIM, 1)
        qh_ref[hd] = jnp.where(_half_mask(qs.shape, group), qs, 0.0).astype(BF16)


def _softmax_block(sel, kblk, vblk, q_heads, m_ref, acc_ref, s_ref, p_ref):
    n = len(q_heads)
    reps = kblk.shape[0] // LANE
    v_ones = jnp.concatenate([vblk, jnp.ones_like(vblk)], axis=1)
    for hd in range(n):
        s_ref[hd] = _dot_nt(q_heads[hd], kblk)
    alphas = []
    for hd in range(n):
        s = s_ref[hd]
        if sel is not None:
            s = jnp.where(sel[hd] if isinstance(sel, (list, tuple)) else sel, s, -jnp.inf)
            s_ref[hd] = s
        blk_max = jnp.max(_lane_fold(s, jnp.maximum), axis=-1, keepdims=True)
        m_prev = m_ref[hd]
        m_new = jnp.maximum(m_prev, blk_max)
        alphas.append(jnp.exp(m_prev - m_new))
        m_ref[hd] = m_new
    for hd in range(n):
        p_ref[hd] = jnp.exp(s_ref[hd] - jnp.tile(m_ref[hd], (1, reps))).astype(BF16)
    for hd in range(n):
        alpha2 = jnp.tile(alphas[hd], (1, 2))
        acc_ref[hd] = alpha2 * acc_ref[hd] + jnp.dot(p_ref[hd], v_ones, preferred_element_type=F32)


def _softmax_init(m_ref, acc_ref):
    m_ref[...] = jnp.full(m_ref.shape, M_FLOOR, F32)
    acc_ref[...] = jnp.zeros(acc_ref.shape, F32)


def _softmax_result(acc_ref, hd):
    acc = acc_ref[hd]
    return acc[:, :LANE] / acc[:, LANE:]


def _indexer_query3(iq_ref, iq3_ref):
    for hd in range(A_IDX_HEADS):
        slot, half = hd // 2, hd % 2
        x = iq_ref[:, slot * LANE:(slot + 1) * LANE]
        xl = jnp.where(_half_mask(x.shape, half), x, pltpu.roll(x, HEAD_DIM, 1))
        hi = xl.astype(BF16).astype(F32)
        lower = _half_mask(x.shape, 0)
        iq3_ref[hd, :, :LANE] = jnp.where(lower, hi, xl - hi).astype(BF16)
        iq3_ref[hd, :, LANE:] = jnp.where(lower, hi, 0.0).astype(BF16)


def _ik3_kernel(x_ref, o_ref):
    x = x_ref[...]
    lower = _half_mask(x.shape, 0)
    xl = jnp.where(lower, x, pltpu.roll(x, HEAD_DIM, 1))
    hi = xl.astype(BF16).astype(F32)
    o_ref[:, :LANE] = hi.astype(BF16)
    o_ref[:, LANE:] = jnp.where(lower, xl - hi, 0.0).astype(BF16)


def _indexer_keys3(z, t):
    tm = _tile(t, 1024)
    return pl.pallas_call(
        _ik3_kernel,
        out_shape=jax.ShapeDtypeStruct((t, 2 * LANE), BF16),
        grid=(t // tm,),
        in_specs=[pl.BlockSpec((tm, LANE), lambda i: (i, S_AIK))],
        out_specs=pl.BlockSpec((tm, 2 * LANE), lambda i: (i, 0)),
        compiler_params=_cparams("parallel"),
        name="indexer_keys",
    )(z)


def _dsa_prompt_kernel(q_ref, iq_ref, iw_ref, k_ref, v_ref, ik3_ref, o_ref,
                       key_ref, qh_ref, iq3_ref, m_ref, acc_ref, s_ref, p_ref, *, tq, kb, topk):
    i = pl.program_id(0)
    q0 = i * tq
    n_blk = (q0 + tq + kb - 1) // kb
    row = lax.broadcasted_iota(jnp.int32, (tq, kb), 0)
    limit = (((q0 + row) >> 6) + 1) << 6
    _indexer_query3(iq_ref, iq3_ref)
    iw = iw_ref[...]
    wgt = [iw[:, HEAD_DIM + hd:HEAD_DIM + hd + 1] * (A_IDX_HEADS ** -0.5 * HEAD_DIM ** -0.5)
           for hd in range(A_IDX_HEADS)]

    def score_blk(b, carry):
        start = pl.multiple_of(b * kb, kb)
        ikb = ik3_ref[pl.ds(start, kb), :]
        for hd in range(A_IDX_HEADS):
            s_ref[hd] = _dot_nt(iq3_ref[hd], ikb)
        sc = None
        for hd in range(A_IDX_HEADS):
            term = jnp.maximum(s_ref[hd], 0.0) * wgt[hd]
            sc = term if sc is None else sc + term
        kpos = start + lax.broadcasted_iota(jnp.int32, (tq, kb), 1)
        key_ref[:, pl.ds(start, kb)] = jnp.where(kpos < limit, _to_key(sc), NEG_KEY)
        return carry

    lax.fori_loop(0, n_blk, score_blk, 0)
    thr, cut = _topk_threshold(key_ref, n_blk, kb, topk)
    _dsa_query_heads(q_ref, qh_ref)
    _softmax_init(m_ref, acc_ref)

    def attn_blk(b, carry):
        start = pl.multiple_of(b * kb, kb)
        sel = _selected(key_ref[:, pl.ds(start, kb)], start, thr, cut)
        _softmax_block(sel, k_ref[pl.ds(start, kb), :], v_ref[pl.ds(start, kb), :],
                       [qh_ref[hd] for hd in range(A_HEADS)], m_ref, acc_ref, s_ref, p_ref)
        return carry

    lax.fori_loop(0, n_blk, attn_blk, 0)
    for j in range(A_HEADS // 2):
        outs = [_softmax_result(acc_ref, 2 * j + e) for e in range(2)]
        o_ref[:, j * LANE:(j + 1) * LANE] = _pair_heads(outs[0], outs[1], (2 * j) // (A_HEADS // A_KV_HEADS)).astype(o_ref.dtype)


def _mixer_call(kern, o_all, m, *, grid, in_specs, out_block, out_index, args, scratch=(), sem, name):
    n_in = len(args)
    if o_all is None:
        fn, specs, alias, extra = kern, list(in_specs), {}, []
    else:
        fn = lambda *refs: kern(*refs[:n_in], *refs[n_in + 1:])
        specs, alias, extra = list(in_specs) + [pl.BlockSpec(memory_space=pl.ANY)], {n_in: 0}, [o_all]
    return pl.pallas_call(
        fn,
        out_shape=jax.ShapeDtypeStruct((m, N_BRANCH * BRANCH_WIDTH), BF16),
        grid=grid,
        in_specs=specs,
        out_specs=pl.BlockSpec(out_block, out_index),
        scratch_shapes=list(scratch),
        input_output_aliases=alias,
        compiler_params=_cparams(*sem),
        name=name,
    )(*args, *extra)


def _dsa_prompt(z, zb, t, o_all, m):
    tq, kb = 256, 512
    topk = min(A_TOPK_MAX, t // 4)
    full = lambda s: pl.BlockSpec((t, LANE), lambda i: (0, s))
    heads = lambda dt, w=LANE: pltpu.VMEM((A_HEADS, tq, w), dt)
    return _mixer_call(
        functools.partial(_dsa_prompt_kernel, tq=tq, kb=kb, topk=topk), o_all, m,
        grid=(t // tq,),
        in_specs=[pl.BlockSpec((tq, 4 * LANE), lambda i: (i, S_AQ // 4)),
                  pl.BlockSpec((tq, 2 * LANE), lambda i: (i, S_AIQ // 2)),
                  pl.BlockSpec((tq, LANE), lambda i: (i, S_AIK)),
                  full(S_AK), full(S_AV), pl.BlockSpec((t, 2 * LANE), lambda i: (0, 0))],
        out_block=(tq, BRANCH_WIDTH), out_index=lambda i: (i, 0),
        scratch=[pltpu.VMEM((tq, t), jnp.int32), heads(BF16),
                 pltpu.VMEM((A_IDX_HEADS, tq, 2 * LANE), BF16),
                 heads(F32), heads(F32, 2 * LANE), heads(F32, kb), heads(BF16, kb)],
        sem=("arbitrary",), name="dsa_prompt",
        args=(z, z, z, zb, zb, _indexer_keys3(z, t)))


def _feature_major(cache, n_feat):
    nd = cache.ndim
    perm = (0, 1) + tuple(range(3, nd)) + (2,)
    return jnp.transpose(cache, perm).reshape(cache.shape[0], cache.shape[1], n_feat, cache.shape[2])


def _dsa_sample_kernel(q_ref, iq_ref, new_ik_ref, new_k_ref, new_v_ref, ckt_ref, cvt_ref, cikt_ref, o_ref,
                       key_ref, qh_ref, *, ts, n_seq, past, topk, q_pos0):
    rows_all = n_seq * ts
    width = key_ref.shape[1]
    iq_all, iw_all = iq_ref[...], new_ik_ref[...]
    row_c = lax.broadcasted_iota(jnp.int32, (ts, past), 0)
    limit_c = (((q_pos0 + row_c) >> 6) + 1) << 6
    kpos_c = lax.broadcasted_iota(jnp.int32, (ts, past), 1)
    kpos_n = past + lax.broadcasted_iota(jnp.int32, (ts, ts), 1)
    limit_n = (((q_pos0 + lax.broadcasted_iota(jnp.int32, (ts, ts), 0)) >> 6) + 1) << 6
    key_ref[:, past:] = jnp.full((rows_all, width - past), NEG_KEY, jnp.int32)
    for g in range(n_seq):
        rows = slice(g * ts, (g + 1) * ts)
        iq, iw = iq_all[rows], iw_all[rows]
        ikt = cikt_ref[g]
        ik_new = iw[:, :HEAD_DIM]
        sc_c = sc_n = None
        for hd in range(A_IDX_HEADS):
            qh = iq[:, hd * HEAD_DIM:(hd + 1) * HEAD_DIM]
            wgt = iw[:, HEAD_DIM + hd:HEAD_DIM + hd + 1] * (A_IDX_HEADS ** -0.5 * HEAD_DIM ** -0.5)
            t_c = jnp.maximum(_dot3(qh, ikt), 0.0) * wgt
            t_n = jnp.maximum(_dot_nt3(qh, ik_new), 0.0) * wgt
            sc_c = t_c if sc_c is None else sc_c + t_c
            sc_n = t_n if sc_n is None else sc_n + t_n
        key_ref[rows, :past] = jnp.where(kpos_c < limit_c, _to_key(sc_c), NEG_KEY)
        key_ref[rows, past:past + ts] = jnp.where(kpos_n < limit_n, _to_key(sc_n), NEG_KEY)

    thr, cut = _topk_threshold(key_ref, width // LANE, LANE, topk)
    _dsa_query_heads(q_ref, qh_ref)
    for g in range(n_seq):
        rows = slice(g * ts, (g + 1) * ts)
        sel_c = _selected(key_ref[rows, :past], 0, thr[rows], cut[rows])
        sel_n = _selected(key_ref[rows, past:past + ts], past, thr[rows], cut[rows])
        kct, vct = ckt_ref[g].astype(BF16), cvt_ref[g].astype(BF16)
        kn, vn = new_k_ref[rows, :], new_v_ref[rows, :]
        outs = []
        for hd in range(A_HEADS):
            qh = qh_ref[hd, rows, :]
            s_c = jnp.where(sel_c, jnp.dot(qh, kct, preferred_element_type=F32), -jnp.inf)
            s_n = jnp.where(sel_n, _dot_nt(qh, kn), -jnp.inf)
            m = jnp.maximum(jnp.max(s_c, axis=-1, keepdims=True), jnp.max(s_n, axis=-1, keepdims=True))
            p_c, p_n = jnp.exp(s_c - m), jnp.exp(s_n - m)
            l = jnp.sum(p_c, axis=-1, keepdims=True) + jnp.sum(p_n, axis=-1, keepdims=True)
            o = _dot_nt(p_c.astype(BF16), vct) + jnp.dot(p_n.astype(BF16), vn, preferred_element_type=F32)
            outs.append(o / l)
        for j in range(A_HEADS // 2):
            o_ref[rows, j * LANE:(j + 1) * LANE] = _pair_heads(
                outs[2 * j], outs[2 * j + 1], (2 * j) // (A_HEADS // A_KV_HEADS)).astype(o_ref.dtype)


def _dsa_sample(z, zb, cache_k, cache_v, cache_ik, l, t, nb, ts, o_all, m):
    past = cache_k.shape[2]
    topk = min(A_TOPK_MAX, (past + ts) // 4)
    width = ((past + ts + LANE - 1) // LANE) * LANE
    n_seq = LANE // ts
    rows = n_seq * ts
    assert LANE % ts == 0 and nb % n_seq == 0 and t % rows == 0
    rb = t // rows
    new = lambda s: pl.BlockSpec((rows, LANE), lambda i: (rb + i, s))
    cache = lambda w: pl.BlockSpec((None, n_seq, w, past), lambda i: (l, i, 0, 0))
    return _mixer_call(
        functools.partial(_dsa_sample_kernel, ts=ts, n_seq=n_seq, past=past, topk=topk, q_pos0=past), o_all, m,
        grid=(nb // n_seq,),
        in_specs=[pl.BlockSpec((rows, 4 * LANE), lambda i: (rb + i, S_AQ // 4)),
                  pl.BlockSpec((rows, 2 * LANE), lambda i: (rb + i, S_AIQ // 2)),
                  new(S_AIK), new(S_AK), new(S_AV), cache(LANE), cache(LANE), cache(HEAD_DIM)],
        out_block=(rows, BRANCH_WIDTH), out_index=lambda i: (rb + i, 0),
        scratch=[pltpu.VMEM((rows, width), jnp.int32), pltpu.VMEM((A_HEADS, rows, LANE), BF16)],
        sem=("arbitrary",), name="dsa_sample",
        args=(z, z, z, zb, zb, _feature_major(cache_k, LANE), _feature_major(cache_v, LANE),
              _feature_major(cache_ik, HEAD_DIM)))


def _band_prompt_kernel(q_ref, k_ref, v_ref, bias0_ref, bias1_ref, o_ref, s_ref, p_ref, *, tq, win):
    i = pl.program_id(1)
    units = [(r, e) for r in range(2) for e in range(2)]
    biases = (bias0_ref, bias1_ref)
    w0s, oks = [], []
    for r in range(2):
        q0 = (2 * i + r) * tq
        w0 = pl.multiple_of(jnp.maximum(q0 - B_WINDOW, 0), tq)
        kc = (w0 + lax.broadcasted_iota(jnp.int32, (tq, win), 1)) >> 6
        qc = (q0 + lax.broadcasted_iota(jnp.int32, (tq, win), 0)) >> 6
        w0s.append(w0)
        oks.append((kc <= qc) & (kc >= qc - B_PAST_CHUNKS))
    for u, (r, e) in enumerate(units):
        qs = q_ref[r * tq:(r + 1) * tq, :] * (HEAD_DIM ** -0.5)
        qe = jnp.where(_half_mask(qs.shape, e), qs, 0.0).astype(BF16)
        s_ref[u] = _dot_nt(qe, k_ref[pl.ds(w0s[r], win), :])
    for u, (r, e) in enumerate(units):
        s = jnp.where(oks[r], s_ref[u] + biases[r][e], -jnp.inf)
        p_ref[u] = jnp.exp(s - jnp.max(s, axis=-1, keepdims=True)).astype(BF16)
    outs = []
    for u, (r, e) in enumerate(units):
        vw = v_ref[pl.ds(w0s[r], win), :]
        acc = jnp.dot(p_ref[u], jnp.concatenate([vw, jnp.ones_like(vw)], axis=1), preferred_element_type=F32)
        outs.append(acc[:, :LANE] / acc[:, LANE:])
    for r in range(2):
        o_ref[r * tq:(r + 1) * tq, :] = jnp.where(_half_mask(outs[0].shape, 0), outs[2 * r],
                                                   outs[2 * r + 1]).astype(o_ref.dtype)


def _band_prompt(z, zb, bias, t, tq, o_all, m):
    win = B_WINDOW + tq
    n_case = B_WINDOW // tq
    assert t % (2 * tq) == 0
    case = lambda r: pl.BlockSpec((None, 2, tq, win), lambda j, i: (jnp.minimum(2 * i + r, n_case), j, 0, 0))
    return _mixer_call(
        functools.partial(_band_prompt_kernel, tq=tq, win=win), o_all, m,
        grid=(B_HEADS // 2, t // (2 * tq)),
        in_specs=[pl.BlockSpec((2 * tq, LANE), lambda j, i: (i, S_BQ + j)),
                  pl.BlockSpec((t, LANE), lambda j, i: (0, S_BK + j)),
                  pl.BlockSpec((t, LANE), lambda j, i: (0, S_BV + j)),
                  case(0), case(1)],
        out_block=(2 * tq, LANE), out_index=lambda j, i: (i, BRANCH_WIDTH // LANE + j),
        scratch=[pltpu.VMEM((4, tq, win), F32), pltpu.VMEM((4, tq, win), BF16)],
        sem=("parallel", "arbitrary"), name="band_prompt", args=(z, zb, zb, bias, bias))


def _band_sample_kernel(z_ref, zb_ref, kct_ref, vct_ref, bc_ref, bn_ref, o_ref):
    for j in range(B_HEADS // 2):
        slot = lambda ref, s: ref[:, (s + j) * LANE:(s + j + 1) * LANE]
        feat = slice(j * LANE, (j + 1) * LANE)
        kct, vct = kct_ref[feat, :].astype(BF16), vct_ref[feat, :].astype(BF16)
        kn, vn = slot(zb_ref, S_BK), slot(zb_ref, S_BV)
        qs = slot(z_ref, S_BQ) * (HEAD_DIM ** -0.5)
        outs = []
        for e in range(2):
            qe = jnp.where(_half_mask(qs.shape, e), qs, 0.0).astype(BF16)
            s_c = jnp.dot(qe, kct, preferred_element_type=F32) + bc_ref[2 * j + e]
            s_n = _dot_nt(qe, kn) + bn_ref[2 * j + e]
            m = jnp.maximum(jnp.max(s_c, axis=-1, keepdims=True), jnp.max(s_n, axis=-1, keepdims=True))
            p_c, p_n = jnp.exp(s_c - m), jnp.exp(s_n - m)
            l = jnp.sum(p_c, axis=-1, keepdims=True) + jnp.sum(p_n, axis=-1, keepdims=True)
            o = _dot_nt(p_c.astype(BF16), vct) + jnp.dot(p_n.astype(BF16), vn, preferred_element_type=F32)
            outs.append(o / l)
        o_ref[:, feat] = jnp.where(_half_mask(outs[0].shape, 0), outs[0], outs[1]).astype(o_ref.dtype)


def _band_sample(z, zb, cache_k, cache_v, bias_c, bias_n, l, t, nb, ts, o_all, m):
    pb = cache_k.shape[2]
    rb = t // ts
    bw = BRANCH_WIDTH
    cache = pl.BlockSpec((None, None, bw, pb), lambda b: (l, b, 0, 0))
    full = lambda a: pl.BlockSpec(a.shape, lambda b: (0,) * a.ndim)
    return _mixer_call(
        _band_sample_kernel, o_all, m,
        grid=(nb,),
        in_specs=[pl.BlockSpec((ts, z.shape[1]), lambda b: (rb + b, 0)),
                  pl.BlockSpec((ts, z.shape[1]), lambda b: (rb + b, 0)),
                  cache, cache, full(bias_c), full(bias_n)],
        out_block=(ts, bw), out_index=lambda b: (rb + b, 1),
        sem=("parallel",), name="band_sample",
        args=(z, zb, _feature_major(cache_k, bw), _feature_major(cache_v, bw), bias_c, bias_n))


def _band_bias(table, tq):
    n_case = B_WINDOW // tq
    win = B_WINDOW + tq
    width = win + B_WINDOW
    j = np.arange(width + tq - 1) - (tq - 1) - B_WINDOW
    ext = table.astype(F32)[:, np.clip(j, -B_REL_CLIP, B_REL_CLIP) + B_REL_CLIP]
    toe = jnp.stack([ext[:, tq - 1 - r:tq - 1 - r + width] for r in range(tq)], axis=1)
    return jnp.stack([toe[:, :, B_WINDOW - c * tq:B_WINDOW - c * tq + win] for c in range(n_case + 1)], axis=0)


def _lambda(lam_ref, lam_init):
    lv = lam_ref[...]
    return (jnp.exp(jnp.sum(lv[0:1] * lv[1:2], axis=-1, keepdims=True))
            - jnp.exp(jnp.sum(lv[2:3] * lv[3:4], axis=-1, keepdims=True)) + lam_init)


def _diff_finish(o0, o1, lam, on_ref, lam_init):
    attn = o0 - lam * o1
    ms = jnp.mean(attn * attn, axis=-1, keepdims=True)
    return (attn * lax.rsqrt(ms + NORM_EPS) * on_ref[...]) * (1.0 - lam_init)


def _diff_prompt_kernel(q_ref, k_ref, v_ref, lam_ref, on_ref, o_ref, qh_ref, m_ref, acc_ref, s_ref, p_ref,
                        *, tq, kb, tg, lam_init):
    i = pl.program_id(1)
    q0 = i * tq
    n_grp = tq // tg
    n_blk = (q0 + tq + kb - 1) // kb
    row = lax.broadcasted_iota(jnp.int32, (tg, kb), 0)
    limits = [(((q0 + g * tg + row) >> 6) + 1) << 6 for g in range(n_grp)]
    for g in range(n_grp):
        qs = q_ref[g * tg:(g + 1) * tg, :] * (HEAD_DIM ** -0.5)
        for c in range(2):
            qh_ref[2 * g + c] = jnp.where(_half_mask(qs.shape, c), qs, 0.0).astype(BF16)
    _softmax_init(m_ref, acc_ref)
    n_full = (q0 + CHUNK) // kb

    def attn_blk(b, carry, masked):
        start = pl.multiple_of(b * kb, kb)
        ok = None
        if masked:
            kpos = start + lax.broadcasted_iota(jnp.int32, (tg, kb), 1)
            ok = [kpos < limits[u // 2] for u in range(2 * n_grp)]
        _softmax_block(ok, k_ref[pl.ds(start, kb), :], v_ref[pl.ds(start, kb), :],
                       [qh_ref[u] for u in range(2 * n_grp)], m_ref, acc_ref, s_ref, p_ref)
        return carry

    lax.fori_loop(0, n_full, functools.partial(attn_blk, masked=False), 0)
    lax.fori_loop(n_full, n_blk, functools.partial(attn_blk, masked=True), 0)
    lam = _lambda(lam_ref, lam_init)
    for g in range(n_grp):
        outs = [_softmax_result(acc_ref, 2 * g + c) for c in range(2)]
        o_ref[g * tg:(g + 1) * tg, :] = _diff_finish(outs[0], outs[1], lam, on_ref, lam_init).astype(o_ref.dtype)


def _diff_prompt(z, zb, c_lam_all, c_on_all, l, t, lam_init, o_all, m):
    tq, kb, tg = _tile(t, 512), 512, 128
    maps = lambda dt, w=LANE: pltpu.VMEM((2 * tq // tg, tg, w), dt)
    return _mixer_call(
        functools.partial(_diff_prompt_kernel, tq=tq, kb=kb, tg=tg, lam_init=lam_init), o_all, m,
        grid=(C_HEADS, t // tq),
        in_specs=[pl.BlockSpec((tq, LANE), lambda h, i: (i, S_CQ + h)),
                  pl.BlockSpec((t, LANE), lambda h, i: (0, S_CK + h)),
                  pl.BlockSpec((t, LANE), lambda h, i: (0, S_CV + h)),
                  pl.BlockSpec((None, 4, HEAD_DIM), lambda h, i: (l, 0, 0)),
                  _layer_vec(l, LANE)],
        out_block=(tq, LANE), out_index=lambda h, i: (i, 2 * BRANCH_WIDTH // LANE + h),
        scratch=[maps(BF16), maps(F32), maps(F32, 2 * LANE), maps(F32, kb), maps(BF16, kb)],
        sem=("parallel", "arbitrary"), name="diff_prompt",
        args=(z, zb, zb, c_lam_all, _vec3(c_on_all)))


def _diff_sample_kernel(z_ref, zb_ref, kct_ref, vc_ref, lam_ref, on_ref, o_ref, *, past, lam_init):
    lam = _lambda(lam_ref, lam_init)
    for h in range(C_HEADS):
        slot = lambda ref, s: ref[:, (s + h) * LANE:(s + h + 1) * LANE]
        feat = slice(h * LANE, (h + 1) * LANE)
        kct = kct_ref[feat, :].astype(BF16)
        vc = vc_ref[pl.ds(h, past, stride=C_HEADS), :].astype(BF16)
        kn, vn = slot(zb_ref, S_CK), slot(zb_ref, S_CV)
        qs = slot(z_ref, S_CQ) * (HEAD_DIM ** -0.5)
        outs = []
        for c in range(2):
            qc = jnp.where(_half_mask(qs.shape, c), qs, 0.0).astype(BF16)
            s_c = jnp.dot(qc, kct, preferred_element_type=F32)
            s_n = _dot_nt(qc, kn)
            m = jnp.maximum(jnp.max(s_c, axis=-1, keepdims=True), jnp.max(s_n, axis=-1, keepdims=True))
            p_c, p_n = jnp.exp(s_c - m), jnp.exp(s_n - m)
            l = jnp.sum(p_c, axis=-1, keepdims=True) + jnp.sum(p_n, axis=-1, keepdims=True)
            o = (jnp.dot(p_c.astype(BF16), vc, preferred_element_type=F32)
                 + jnp.dot(p_n.astype(BF16), vn, preferred_element_type=F32))
            outs.append(o / l)
        o_ref[:, feat] = _diff_finish(outs[0], outs[1], lam, on_ref, lam_init).astype(o_ref.dtype)


def _diff_sample(z, zb, cache_k, cache_v, c_lam_all, c_on_all, l, t, nb, ts, lam_init, o_all, m):
    depth, _, past = cache_k.shape[:3]
    rb = t // ts
    bw = BRANCH_WIDTH
    return _mixer_call(
        functools.partial(_diff_sample_kernel, past=past, lam_init=lam_init), o_all, m,
        grid=(nb,),
        in_specs=[pl.BlockSpec((ts, z.shape[1]), lambda b: (rb + b, 0)),
                  pl.BlockSpec((ts, z.shape[1]), lambda b: (rb + b, 0)),
                  pl.BlockSpec((None, None, bw, past), lambda b: (l, b, 0, 0)),
                  pl.BlockSpec((None, None, past * C_HEADS, LANE), lambda b: (l, b, 0, 0)),
                  pl.BlockSpec((None, 4, HEAD_DIM), lambda b: (l, 0, 0)),
                  _layer_vec(l, LANE)],
        out_block=(ts, bw), out_index=lambda b: (rb + b, 2),
        sem=("parallel",), name="diff_sample",
        args=(z, zb, _feature_major(cache_k, bw), cache_v.reshape(depth, nb, past * C_HEADS, LANE),
              c_lam_all, _vec3(c_on_all)))


def _rwkv_pre_kernel(zd_ref, prev_ref, first_ref, mu_ref, w0_ref, w2_ref, a0_ref, a2_ref, g2_ref, kkw_ref,
                     ka_ref, bd_ref, r_ref, w_ref, k_ref, v_ref, kk_ref, b_ref, g_ref, *, tm, t, ts):
    zf = zd_ref[...]
    grow = pl.program_id(0) * tm + lax.broadcasted_iota(jnp.int32, (tm, 1), 0)
    zs = jnp.where(grow % tm == 0, prev_ref[7:8, :], pltpu.roll(zf, 1, 0))
    seq_start = (grow >= t) & ((grow - t) % ts == 0)
    zs = jnp.where(seq_start, first_ref[...], zs)
    zs = jnp.where(grow == 0, 0.0, zs)
    zm = zf + (zs - zf) * mu_ref[...]
    bw = BRANCH_WIDTH
    r, k, v = zm[:, :bw], zm[:, bw:2 * bw], zm[:, 2 * bw:3 * bw]
    wl, al, gl = zm[:, 3 * bw:3 * bw + 64], zm[:, 3 * bw + 64:3 * bw + 128], zm[:, 3 * bw + 128:]
    dot = lambda a, b: jnp.dot(a.astype(BF16), b.astype(BF16), preferred_element_type=F32)
    u = -(w0_ref[...] + dot(jnp.tanh(wl), w2_ref[...]))
    softplus = jnp.maximum(u, 0.0) + jnp.log(1.0 + jnp.exp(-jnp.abs(u)))
    w = -softplus - 0.5
    a = jax.nn.sigmoid(a0_ref[...] + dot(al, a2_ref[...]))
    kk = k * kkw_ref[...]
    nrm = jnp.sqrt(_seg_sum(kk * kk, bd_ref[...]))
    kk = kk / jnp.maximum(nrm, 1e-12)
    r_ref[...] = r
    w_ref[...] = -jnp.exp(w)
    k_ref[...] = k * (1.0 + (a - 1.0) * ka_ref[...])
    v_ref[...] = v
    kk_ref[...] = kk
    b_ref[...] = kk * a
    g_ref[...] = dot(jax.nn.sigmoid(gl), g2_ref[...])


def _rwkv_pre(zd, zfirst, p, l, t, ts):
    m, dc = zd.shape
    tm = _tile(math.gcd(t, m - t), 512)
    bw = BRANCH_WIDTH
    npt = t // tm
    rows = pl.BlockSpec((tm, dc), lambda i: (i, 0))
    vec = lambda n: _layer_vec(l, n)
    mat = lambda k: pl.BlockSpec((None, k, bw), lambda i: (l, 0, 0))
    out = jax.ShapeDtypeStruct((m, bw), F32)
    return pl.pallas_call(
        functools.partial(_rwkv_pre_kernel, tm=tm, t=t, ts=ts),
        out_shape=[out] * 7,
        grid=(m // tm,),
        in_specs=[rows,
                  pl.BlockSpec((8, dc), lambda i: (jnp.maximum(i * (tm // 8) - 1, 0), 0)),
                  pl.BlockSpec((tm, dc), lambda i: (jnp.maximum(i - npt, 0), 0)),
                  vec(dc), vec(bw), mat(64), vec(bw), mat(64), mat(128), vec(bw), vec(bw),
                  pl.BlockSpec((bw, bw), lambda i: (0, 0))],
        out_specs=[pl.BlockSpec((tm, bw), lambda i: (i, 0))] * 7,
        compiler_params=_cparams("parallel"),
        name="rwkv_pre",
    )(zd, zd, zfirst, _vec3(p["d_mu"]), _vec3(p["d_w0"]), p["d_w2"], _vec3(p["d_a0"]), p["d_a2"], p["d_g2"],
      _vec3(p["d_k_k"]), _vec3(p["d_k_a"]), _ones_blockdiag(bw))


RW_CHUNK_MAX = 64
RW_BLOCK = 128
RW_SLOTS = BRANCH_WIDTH // LANE


def _dot3(a, b):
    ah, al = _split2(a)
    bh, bl = _split2(b)
    d = lambda x, y: jnp.dot(x, y, preferred_element_type=F32)
    return d(ah, bh) + d(al, bh) + d(ah, bl)


def _rwkv_chunk_kernel(*refs, seq_chunks, RW_CHUNK):
    if seq_chunks:
        (r_ref, lw_ref, k_ref, v_ref, kk_ref, b_ref, s0_ref, y_ref, sf_ref,
         h_ref, u_ref, ab_ref, rb_ref, u0_ref, y0_ref, bt_ref, kt_ref, eg_ref) = refs
    else:
        (r_ref, lw_ref, k_ref, v_ref, kk_ref, b_ref, y_ref, sf_ref,
         h_ref, u_ref, ab_ref, rb_ref, u0_ref, y0_ref, bt_ref, kt_ref, eg_ref) = refs
        s0_ref = None

        @pl.when(pl.program_id(0) == 0)
        def _():
            h_ref[...] = jnp.zeros_like(h_ref)

    n = RW_BLOCK
    n_chunks = n // RW_CHUNK
    row = lax.broadcasted_iota(jnp.int32, (n, n), 0)
    col = lax.broadcasted_iota(jnp.int32, (n, n), 1)
    same = (row // RW_CHUNK) == (col // RW_CHUNK)
    strict, incl = same & (col < row), same & (col <= row)
    eye = jnp.where(row == col, 1.0, 0.0)
    head_diag = (row // HEAD_DIM) == (col // HEAD_DIM)
    in_chunk = row % RW_CHUNK
    k_pick = jnp.where(lax.broadcasted_iota(jnp.int32, (n, HEAD_DIM), 0) % HEAD_DIM
                       == lax.broadcasted_iota(jnp.int32, (n, HEAD_DIM), 1), 1.0, 0.0)
    bf = lambda x: x.astype(BF16)
    mm = lambda x, y: jnp.dot(bf(x), bf(y), preferred_element_type=F32)
    u_ref[...] = jnp.zeros_like(u_ref)

    halves = [_half_mask((n, LANE), e) for e in range(2)]
    group = 2
    for j0 in range(0, RW_SLOTS, group):
        slot_data = []
        for j in range(j0, j0 + group):
            sl = slice(j * LANE, (j + 1) * LANE)
            lw, v = lw_ref[:, sl], v_ref[:, sl]
            g = lw
            for d in [1 << s for s in range(RW_CHUNK.bit_length() - 1)]:
                g = g + jnp.where(in_chunk >= d, pltpu.roll(g, d, 0), 0.0)
            inv_g = jnp.exp(-g)
            a_t = -kk_ref[:, sl] * jnp.exp(g - lw)
            b_t, k_t = b_ref[:, sl] * inv_g, k_ref[:, sl] * inv_g
            r_t = r_ref[:, sl] * jnp.exp(g)
            bt_ref[j], kt_ref[j], eg_ref[j] = b_t.T, k_t.T, jnp.exp(g).T
            slot_data.append((a_t, r_t, bf(b_t), bf(k_t), bf(v)))
        chains = [(s, e) for s in range(group) for e in range(2)]
        each = lambda f: [f(s, e, i) for i, (s, e) in enumerate(chains)]
        a_e = each(lambda s, e, i: bf(jnp.where(halves[e], slot_data[s][0], 0.0)))
        r_e = each(lambda s, e, i: bf(jnp.where(halves[e], slot_data[s][1], 0.0)))
        n_ab = each(lambda s, e, i: jnp.where(strict, _dot_nt(a_e[i], slot_data[s][2]), 0.0))
        n_ak = each(lambda s, e, i: jnp.where(strict, _dot_nt(a_e[i], slot_data[s][3]), 0.0))
        m_rb = each(lambda s, e, i: bf(jnp.where(incl, _dot_nt(r_e[i], slot_data[s][2]), 0.0)))
        m_rk = each(lambda s, e, i: bf(jnp.where(incl, _dot_nt(r_e[i], slot_data[s][3]), 0.0)))
        w_e = each(lambda s, e, i: mm(n_ak[i], slot_data[s][4]))
        tinv = each(lambda s, e, i: eye + n_ab[i])
        pw = n_ab
        for _ in range(RW_CHUNK.bit_length() - 2):
            pw = each(lambda s, e, i: mm(pw[i], pw[i]))
            tinv = each(lambda s, e, i: tinv[i] + mm(tinv[i], pw[i]))
        t16 = each(lambda s, e, i: bf(tinv[i]))
        a_bar = each(lambda s, e, i: mm(t16[i], slot_data[s][0]))
        u0 = each(lambda s, e, i: mm(t16[i], w_e[i]))
        r_bar = each(lambda s, e, i: slot_data[s][1] + mm(m_rb[i], a_bar[i]))
        y0 = each(lambda s, e, i: mm(m_rb[i], u0[i]) + mm(m_rk[i], slot_data[s][4]))
        for s in range(group):
            pick = lambda vals: jnp.where(halves[0], vals[2 * s], vals[2 * s + 1])
            j = j0 + s
            ab_ref[j], rb_ref[j], u0_ref[j], y0_ref[j] = pick(a_bar), pick(r_bar), pick(u0), pick(y0)

    for c in range(n_chunks):
        rows = slice(c * RW_CHUNK, (c + 1) * RW_CHUNK)
        col_c = (col // RW_CHUNK) == c
        for j in range(RW_SLOTS):
            sl = slice(j * LANE, (j + 1) * LANE)
            if seq_chunks and c % seq_chunks == 0:
                x = s0_ref[c // seq_chunks, sl, :]
                h = jnp.where(head_diag, _dot_nt3(k_pick, x), 0.0)
            else:
                h = h_ref[j]
            res = _dot3(jnp.concatenate([ab_ref[j, rows, :], rb_ref[j, rows, :]], axis=0), h)
            u_c = res[:RW_CHUNK] + u0_ref[j, rows, :]
            y_ref[rows, sl] = res[RW_CHUNK:] + y0_ref[j, rows, :]
            u_ref[j, rows, :] = u_c
            bk = jnp.concatenate([jnp.where(col_c, bt_ref[j], 0.0), jnp.where(col_c, kt_ref[j], 0.0)], axis=1)
            uv = jnp.concatenate([u_ref[j], v_ref[:, sl]], axis=0)
            inc = jnp.where(head_diag, mm(bk, uv), 0.0)
            g_end = eg_ref[j, :, (c + 1) * RW_CHUNK - 1:(c + 1) * RW_CHUNK]
            h = g_end * (h + inc)
            h_ref[j] = h
            if seq_chunks and (c + 1) % seq_chunks == 0:
                ht = h.T
                sf_ref[c // seq_chunks, sl, :] = (ht + pltpu.roll(ht, HEAD_DIM, 1))[:, :HEAD_DIM]

    if not seq_chunks:
        @pl.when(pl.program_id(0) == pl.num_programs(0) - 1)
        def _():
            for j in range(RW_SLOTS):
                ht = h_ref[j].T
                sf_ref[0, j * LANE:(j + 1) * LANE, :] = (ht + pltpu.roll(ht, HEAD_DIM, 1))[:, :HEAD_DIM]


def _rwkv_chunked(ops, s0, row0, n_seq, t):
    n = RW_BLOCK
    bw = BRANCH_WIDTH
    chunk = math.gcd(t, RW_CHUNK_MAX)
    assert row0 % n == 0 and (n_seq * t) % n == 0 and chunk >= 8 and chunk & (chunk - 1) == 0
    rb = row0 // n
    rows = pl.BlockSpec((n, bw), lambda i: (rb + i, 0))
    if s0 is None:
        assert n_seq == 1
        seq_chunks, per_blk, extra, extra_specs = 0, 1, [], []
        sf_spec = pl.BlockSpec((1, bw, HEAD_DIM), lambda i: (0, 0, 0))
    else:
        assert n % t == 0
        seq_chunks, per_blk = t // chunk, n // t
        extra = [s0.reshape(n_seq, bw, HEAD_DIM)]
        extra_specs = [pl.BlockSpec((per_blk, bw, HEAD_DIM), lambda i: (i, 0, 0))]
        sf_spec = pl.BlockSpec((per_blk, bw, HEAD_DIM), lambda i: (i, 0, 0))
    slot = lambda dt=F32: pltpu.VMEM((RW_SLOTS, n, LANE), dt)
    y, sf = pl.pallas_call(
        functools.partial(_rwkv_chunk_kernel, seq_chunks=seq_chunks, RW_CHUNK=chunk),
        out_shape=[jax.ShapeDtypeStruct((n_seq * t, bw), F32), jax.ShapeDtypeStruct((n_seq, bw, HEAD_DIM), F32)],
        grid=(n_seq * t // n,),
        in_specs=[rows] * 6 + extra_specs,
        out_specs=[pl.BlockSpec((n, bw), lambda i: (i, 0)), sf_spec],
        scratch_shapes=[slot() for _ in range(9)],
        compiler_params=_cparams("arbitrary"),
        name="rwkv_chunked",
    )(*ops, *extra)
    return y, sf.reshape(n_seq, D_HEADS, HEAD_DIM, HEAD_DIM)


def _rwkv_post_kernel(y_ref, r_ref, k_ref, v_ref, g_ref, lnw_ref, lnb_ref, rk_ref, bd_ref, o_ref):
    bd = bd_ref[...]
    y = y_ref[...]
    mean = _seg_sum(y, bd) * (1.0 / HEAD_DIM)
    yc = y - mean
    var = _seg_sum(yc * yc, bd) * (1.0 / HEAD_DIM)
    yn = yc * lax.rsqrt(var + D_GN_EPS) * lnw_ref[...] + lnb_ref[...]
    bonus = _seg_sum(r_ref[...] * k_ref[...] * rk_ref[...], bd) * v_ref[...]
    o_ref[...] = ((yn + bonus) * g_ref[...]).astype(o_ref.dtype)


def _rwkv_post(y, r, k, v, g, p, l, o_all):
    m, bw = y.shape
    tm = _tile(m, 512)
    rows = pl.BlockSpec((tm, bw), lambda i: (i, 0))
    vec = _layer_vec(l, bw)
    return _mixer_call(
        _rwkv_post_kernel, o_all, m,
        grid=(m // tm,),
        in_specs=[rows] * 5 + [vec] * 3 + [pl.BlockSpec((bw, bw), lambda i: (0, 0))],
        out_block=(tm, bw), out_index=lambda i: (i, N_BRANCH - 1),
        sem=("parallel",), name="rwkv_post",
        args=(y, r, k, v, g, _vec3(p["d_ln_w"]), _vec3(p["d_ln_b"]), _vec3(p["d_r_k"]), _ones_blockdiag(bw)))


def _merge_kernel(o_ref, wbr_ref, h_ref, wg_ref, out_ref, acc_ref):
    n = pl.program_id(2)

    @pl.when(n == 0)
    def _():
        acc_ref[...] = jnp.zeros_like(acc_ref)

    u = jnp.dot(o_ref[...], wbr_ref[...].astype(BF16), preferred_element_type=F32)
    gate = jax.nn.sigmoid(jnp.dot(h_ref[...], wg_ref[...], preferred_element_type=F32))
    acc_ref[...] += u * gate

    @pl.when(n == N_BRANCH - 1)
    def _():
        out_ref[...] = acc_ref[...].astype(out_ref.dtype)


def _merge(o_all, w_br_all, h, w_pack, l):
    m, d = h.shape
    tm, tn = _tile(m, 1024), 512
    nj = d // tn
    g0 = S_GATE * LANE // tn
    return pl.pallas_call(
        _merge_kernel,
        out_shape=jax.ShapeDtypeStruct((m, d), BF16),
        grid=(m // tm, nj, N_BRANCH),
        in_specs=[pl.BlockSpec((tm, BRANCH_WIDTH), lambda i, j, n: (i, n)),
                  pl.BlockSpec((None, None, BRANCH_WIDTH, tn), lambda i, j, n: (l, n, 0, j)),
                  pl.BlockSpec((tm, d), lambda i, j, n: (i, 0)),
                  pl.BlockSpec((None, d, tn), lambda i, j, n: (l, 0, g0 + n * nj + j))],
        out_specs=pl.BlockSpec((tm, tn), lambda i, j, n: (i, j)),
        scratch_shapes=[pltpu.VMEM((tm, tn), F32)],
        compiler_params=_cparams("parallel", "parallel", "arbitrary"),
        name="branch_merge",
    )(o_all, w_br_all, h, w_pack)


def _ple_kernel(x_ref, pe_ref, wple_ref, h_ref, wg_ref, o_ref):
    emb = jnp.dot(pe_ref[...].astype(BF16), wple_ref[...].astype(BF16), preferred_element_type=F32)
    gate = jax.nn.sigmoid(jnp.dot(h_ref[...], wg_ref[...].astype(BF16), preferred_element_type=F32))
    o_ref[...] = x_ref[...] + emb * gate


def _ple(x, pe, w_ple_all, h, w_gate_all, l):
    m, d = x.shape
    pd = pe.shape[1]
    tm, tn = _tile(m, 1024), 512
    return pl.pallas_call(
        _ple_kernel,
        out_shape=jax.ShapeDtypeStruct((m, d), F32),
        grid=(m // tm, d // tn),
        in_specs=[pl.BlockSpec((tm, tn), lambda i, j: (i, j)),
                  pl.BlockSpec((tm, pd), lambda i, j: (i, 0)),
                  pl.BlockSpec((None, pd, tn), lambda i, j: (l, 0, j)),
                  pl.BlockSpec((tm, d), lambda i, j: (i, 0)),
                  pl.BlockSpec((None, d, tn), lambda i, j: (l, 0, j))],
        out_specs=pl.BlockSpec((tm, tn), lambda i, j: (i, j)),
        compiler_params=_cparams("parallel", "parallel"),
        name="ple",
    )(x, pe, w_ple_all, h, w_gate_all)


def _layer(l, x, pe, p, t, nb, ts, tabs):
    m = x.shape[0]
    lam_init = 0.8 - 0.6 * math.exp(-0.3 * l)
    dc = p["d_mu"].shape[1]
    d = x.shape[1]
    lw = {"a_qn": p["a_q_norm"][l], "a_kn": p["a_k_norm"][l], "b_qn": p["b_q_norm"][l], "b_kn": p["b_k_norm"][l],
          "c_qn": p["c_q_norm"][l], "c_kn": p["c_k_norm"][l]}

    h = _rmsnorm(x, p["norm1_g"], l)
    w_pack = p["w_pack"]
    z, zb = _qkv_projection(h, w_pack, l, *_column_vectors(lw), *tabs)
    zd = _matmul(h, w_pack, lambda acc: acc, F32, n=dc, layer=l, col0=S_D * LANE, tn=256, name="rwkv_projection")

    o_all = _dsa_prompt(z, zb, t, None, m)
    o_all = _dsa_sample(z, zb, p["cache_a_k"], p["cache_a_v"], p["cache_a_kidx"], l, t, nb, ts, o_all, m)
    tq_b = 128
    bias = _band_bias(p["b_rel_bias"][l], tq_b)
    pb = p["cache_b_k"].shape[2]
    generic = bias[B_WINDOW // tq_b]
    o_all = _band_prompt(z, zb, bias, t, tq_b, o_all, m)
    o_all = _band_sample(z, zb, p["cache_b_k"], p["cache_b_v"], generic[:, :ts, B_WINDOW - pb:B_WINDOW],
                         generic[:, :ts, B_WINDOW:B_WINDOW + ts], l, t, nb, ts, o_all, m)
    c_on = p["c_out_norm"]
    o_all = _diff_prompt(z, zb, p["c_lambda"], c_on, l, t, lam_init, o_all, m)
    o_all = _diff_sample(z, zb, p["cache_c_k"], p["cache_c_v"], p["c_lambda"], c_on, l, t, nb, ts, lam_init,
                         o_all, m)

    zfirst = jnp.broadcast_to(p["state_d_shift"][l], (nb, ts, dc)).reshape(nb * ts, dc)
    r, w, k, v, kk, b, g = _rwkv_pre(zd, zfirst, p, l, t, ts)
    ops = (r, w, k, v, kk, b)
    y_p, wkv_p = _rwkv_chunked(ops, None, 0, 1, t)
    y_s, wkv_s = _rwkv_chunked(ops, p["state_d_wkv"][l], t, nb, ts)
    o_all = _rwkv_post(jnp.concatenate([y_p, y_s], axis=0), r, k, v, g, p, l, o_all)

    ug = _merge(o_all, p["w_branch"], h, w_pack, l)
    res = lambda acc, r_: r_ + acc
    x = _matmul(ug, p["w_out"], res, F32, n=d, layer=l, residual=x, name="out_proj")
    h2 = _rmsnorm(x, p["norm2_g"], l)
    ffn = p["w_up"].shape[2]
    up = _matmul(h2, p["w_up"], lambda acc: jnp.square(jnp.maximum(acc, 0.0)), BF16, n=ffn, layer=l, name="mlp_up")
    x = _matmul(up, _cast_bf16(p["w_down"], l), res, F32, n=d, residual=x, tn=1024, name="mlp_down")
    h3 = _rmsnorm(x, p["norm3_g"], l)
    x = _ple(x, pe, p["w_ple"], h3, p["w_ple_gate"], l)

    slot = lambda rows, s, n=1: z[rows, s * LANE:(s + n) * LANE]

    def rows_of(sl, lead):
        a = lambda s, n, shape: slot(sl, s, n).reshape(lead + shape)
        ak = a(S_AK, 1, (A_KV_HEADS, HEAD_DIM))
        av = a(S_AV, 1, (A_KV_HEADS, HEAD_DIM))
        aik = slot(sl, S_AIK)[:, :HEAD_DIM].reshape(lead + (HEAD_DIM,))
        bk = a(S_BK, 4, (B_HEADS, HEAD_DIM))
        bv = a(S_BV, 4, (B_HEADS, HEAD_DIM))
        ck = a(S_CK, 4, (C_HEADS, 2, HEAD_DIM))
        cv = a(S_CV, 4, (C_HEADS, 2 * HEAD_DIM))
        return ak, av, aik, bk, bv, ck, cv

    keep = min(B_WINDOW, t)
    pak, pav, paik, pbk, pbv, pck, pcv = rows_of(slice(0, t), (1, t))
    new_p = (pak, pav, paik, pbk[:, t - keep:], pbv[:, t - keep:], pck, pcv, wkv_p, zd[t - 1:t].reshape(1, 1, dc))
    new_s = rows_of(slice(t, m), (nb, ts)) + (wkv_s, zd[t:].reshape(nb, ts, dc)[:, -1:])
    return x, new_p, new_s


def kernel(x_prompt, x_sample, cache_a_k, cache_a_v, cache_a_kidx, cache_b_k, cache_b_v, cache_c_k, cache_c_v, state_d_wkv, state_d_shift, p_prompt, p_sample, norm1_g, w_in, a_q_norm, a_k_norm, b_q_norm, b_k_norm, b_rel_bias, c_q_norm, c_k_norm, c_lambda, c_out_norm, d_mu, d_w0, d_w2, d_a0, d_a2, d_g2, d_k_k, d_k_a, d_r_k, d_ln_w, d_ln_b, w_branch, w_out, norm2_g, w_up, w_down, norm3_g, w_ple, w_ple_gate):
    batch, t, d = x_prompt.shape
    nb, ts, _ = x_sample.shape
    past = cache_a_k.shape[2]
    depth = w_in.shape[0]
    assert batch == 1 and t % 512 == 0 and past % CHUNK == 0 and ts <= CHUNK and (nb * ts) % 8 == 0
    p = dict(cache_a_k=cache_a_k, cache_a_v=cache_a_v, cache_a_kidx=cache_a_kidx, cache_b_k=cache_b_k,
             cache_b_v=cache_b_v, cache_c_k=cache_c_k, cache_c_v=cache_c_v, state_d_wkv=state_d_wkv,
             state_d_shift=state_d_shift, norm1_g=norm1_g, w_in=w_in, a_q_norm=a_q_norm, a_k_norm=a_k_norm,
             b_q_norm=b_q_norm, b_k_norm=b_k_norm, b_rel_bias=b_rel_bias, c_q_norm=c_q_norm, c_k_norm=c_k_norm,
             c_lambda=c_lambda, c_out_norm=c_out_norm, d_mu=d_mu, d_w0=d_w0, d_w2=d_w2, d_a0=d_a0, d_a2=d_a2,
             d_g2=d_g2, d_k_k=d_k_k, d_k_a=d_k_a, d_r_k=d_r_k, d_ln_w=d_ln_w, d_ln_b=d_ln_b, w_branch=w_branch,
             w_out=w_out, norm2_g=norm2_g, w_up=w_up, w_down=w_down, norm3_g=norm3_g, w_ple=w_ple,
             w_ple_gate=w_ple_gate)
    p["w_pack"] = _pack_w_in(w_in)
    x = jnp.concatenate([x_prompt[0], x_sample.reshape(nb * ts, d)], axis=0)
    pos = jnp.concatenate([jnp.arange(t, dtype=jnp.int32),
                           jnp.tile(past + jnp.arange(ts, dtype=jnp.int32), nb)])
    tabs = _rope_tables(pos)
    st_p = [[] for _ in range(9)]
    st_s = [[] for _ in range(9)]
    for l in range(depth):
        pe = jnp.concatenate([p_prompt[l, 0], p_sample[l].reshape(nb * ts, -1)], axis=0)
        x, new_p, new_s = _layer(l, x, pe, p, t, nb, ts, tabs)
        for lst, arr in zip(st_p, new_p):
            lst.append(arr)
        for lst, arr in zip(st_s, new_s):
            lst.append(arr)
    outs_p = [jnp.stack(s, axis=0) for s in st_p]
    outs_s = [jnp.stack(s, axis=0) for s in st_s]
    return (x[:t].reshape(1, t, d), x[t:].reshape(nb, ts, d), *outs_p, *outs_s)
```

```python
import functools
import math

import numpy as np
import jax
import jax.numpy as jnp
from jax import lax
from jax.experimental import pallas as pl
from jax.experimental.pallas import tpu as pltpu

F32 = jnp.float32
BF16 = jnp.bfloat16

CHUNK = 64
HEAD_DIM = 64
ROPE_DIM = 16
ROPE_THETA = 500000.0
N_BRANCH = 4
BRANCH_WIDTH = 512
A_HEADS, A_KV_HEADS, A_IDX_HEADS = 8, 2, 4
A_TOPK_MAX = 256
B_HEADS = 8
B_PAST_CHUNKS = 8
B_WINDOW = B_PAST_CHUNKS * CHUNK
B_REL_CLIP = 128
C_HEADS = 4
D_HEADS = 8
D_GN_EPS = 64e-5
NORM_EPS = 1e-6

LANE = 128
VMEM_LIMIT = 48 * 1024 * 1024

S_AQ, S_AK, S_AV, S_AIQ, S_AIK = 0, 4, 5, 6, 8
S_BQ, S_BK, S_BV = 10, 14, 18
S_CQ, S_CK, S_CV = 22, 26, 30
S_D, S_GATE = 34, 48
N_QKV_SLOTS = 34
N_D_SLOTS = 14
A_COLS = 1092
A_SLOTS = 9
SHIFT = A_COLS - (A_SLOTS - 1) * LANE

NEG_KEY = -2139095041
INT_MIN = -2147483648
MIN_NORMAL_KEY = 0x00800000
MIN_NORMAL = float(np.float32(2.0 ** -126))
M_FLOOR = -1e30


def _cparams(*sem):
    return pltpu.CompilerParams(dimension_semantics=sem, vmem_limit_bytes=VMEM_LIMIT)


def _tile(n, pref):
    t = min(n, pref)
    while n % t:
        t -= 8
    return t


def _split2(x):
    hi = x.astype(BF16)
    lo = (x - hi.astype(F32)).astype(BF16)
    return hi, lo


def _seg_sum(x, ones_bd):
    hi, lo = _split2(x)
    return (jnp.dot(hi, ones_bd, preferred_element_type=F32)
            + jnp.dot(lo, ones_bd, preferred_element_type=F32))


def _dot_nt(a, b):
    return lax.dot_general(a, b, (((1,), (1,)), ((), ())), preferred_element_type=F32)


def _dot_nt3(a, b):
    ah, al = _split2(a)
    bh, bl = _split2(b)
    return _dot_nt(ah, bh) + _dot_nt(ah, bl) + _dot_nt(al, bh)


def _ones_blockdiag(n):
    i = np.arange(n)
    return jnp.asarray((i[:, None] // HEAD_DIM) == (i[None, :] // HEAD_DIM), dtype=BF16)


def _vec3(a):
    return a.reshape(a.shape[0], 1, -1)


def _layer_vec(l, n):
    return pl.BlockSpec((None, 1, n), lambda *_: (l, 0, 0))


def _half_mask(shape, half):
    lane = lax.broadcasted_iota(jnp.int32, shape, len(shape) - 1)
    return (lane < HEAD_DIM) if half == 0 else (lane >= HEAD_DIM)


def _rms_kernel(x_ref, g_ref, o_ref):
    x = x_ref[...]
    ms = jnp.mean(x * x, axis=-1, keepdims=True)
    o_ref[...] = (x * lax.rsqrt(ms + NORM_EPS) * g_ref[...]).astype(o_ref.dtype)


def _rmsnorm(x, g_all, l):
    m, d = x.shape
    tm = _tile(m, 512)
    return pl.pallas_call(
        _rms_kernel,
        out_shape=jax.ShapeDtypeStruct((m, d), BF16),
        grid=(m // tm,),
        in_specs=[pl.BlockSpec((tm, d), lambda i: (i, 0)), _layer_vec(l, d)],
        out_specs=pl.BlockSpec((tm, d), lambda i: (i, 0)),
        compiler_params=_cparams("parallel"),
        name="rmsnorm",
    )(x, _vec3(g_all))


def _pack_kernel(a_ref, b_ref, o_ref):
    j = pl.program_id(0)
    row = lax.broadcasted_iota(jnp.int32, (LANE, a_ref.shape[2]), 0)
    for l in range(a_ref.shape[1]):
        a, b = a_ref[:, l, :], b_ref[:, l, :]
        shifted = jnp.where(row < LANE - SHIFT, pltpu.roll(a, LANE - SHIFT, 0), pltpu.roll(b, LANE - SHIFT, 0))
        out = jnp.where(j < A_SLOTS, a, jnp.where(j == A_SLOTS, 0.0, shifted))
        o_ref[l] = out.T.astype(o_ref.dtype)


def _pack_w_in(w_in_all):
    depth, d, n_in = w_in_all.shape
    n_slots = A_SLOTS + 1 + (n_in - A_COLS) // LANE
    assert (n_in - A_COLS) % LANE == 0 and 0 < SHIFT < LANE
    src_a = lambda j: jnp.where(j < A_SLOTS, j, j - 2)
    src_b = lambda j: jnp.where(j < A_SLOTS, j, j - 1)
    w_t = jnp.transpose(w_in_all, (2, 0, 1))
    return pl.pallas_call(
        _pack_kernel,
        out_shape=jax.ShapeDtypeStruct((depth, d, n_slots * LANE), BF16),
        grid=(n_slots,),
        in_specs=[pl.BlockSpec((LANE, depth, d), lambda j: (src_a(j), 0, 0)),
                  pl.BlockSpec((LANE, depth, d), lambda j: (src_b(j), 0, 0))],
        out_specs=pl.BlockSpec((depth, d, LANE), lambda j: (0, 0, j)),
        compiler_params=_cparams("parallel"),
        name="pack_w_in",
    )(w_t, w_t)


def _mm_kernel(*refs, n_extra, nk, epilogue):
    a_ref, w_ref = refs[0], refs[1]
    extra = refs[2:2 + n_extra]
    o_ref = refs[2 + n_extra]
    if nk == 1:
        acc = jnp.dot(a_ref[...], w_ref[...].astype(BF16), preferred_element_type=F32)
        o_ref[...] = epilogue(acc, *[e[...] for e in extra]).astype(o_ref.dtype)
        return
    acc_ref = refs[3 + n_extra]
    k = pl.program_id(2)

    @pl.when(k == 0)
    def _():
        acc_ref[...] = jnp.zeros_like(acc_ref)

    acc_ref[...] += jnp.dot(a_ref[...], w_ref[...].astype(BF16), preferred_element_type=F32)

    @pl.when(k == nk - 1)
    def _():
        o_ref[...] = epilogue(acc_ref[...], *[e[...] for e in extra]).astype(o_ref.dtype)


def _matmul(a, w, epilogue, out_dtype, *, n, layer=None, col0=0, residual=None,
            tm=1024, tn=512, tk=2048, name="matmul"):
    m, kdim = a.shape
    tm, tn, tk = _tile(m, tm), _tile(n, tn), _tile(kdim, tk)
    nk = kdim // tk
    cb = col0 // tn
    assert col0 % tn == 0
    if layer is None:
        w_spec = pl.BlockSpec((tk, tn), lambda i, j, k: (k, cb + j))
    else:
        w_spec = pl.BlockSpec((None, tk, tn), lambda i, j, k: (layer, k, cb + j))
    specs = [pl.BlockSpec((tm, tk), lambda i, j, k: (i, k)), w_spec]
    extras = []
    if residual is not None:
        specs.append(pl.BlockSpec((tm, tn), lambda i, j, k: (i, j)))
        extras.append(residual)
    return pl.pallas_call(
        functools.partial(_mm_kernel, n_extra=len(extras), nk=nk, epilogue=epilogue),
        out_shape=jax.ShapeDtypeStruct((m, n), out_dtype),
        grid=(m // tm, n // tn, nk),
        in_specs=specs,
        out_specs=pl.BlockSpec((tm, tn), lambda i, j, k: (i, j)),
        scratch_shapes=[pltpu.VMEM((tm, tn), F32)] if nk > 1 else [],
        compiler_params=_cparams("parallel", "parallel", "arbitrary"),
        name=name,
    )(a, w, *extras)


def _cast_kernel(w_ref, o_ref):
    o_ref[...] = w_ref[...].astype(o_ref.dtype)


def _cast_bf16(w_all, l):
    _, kdim, n = w_all.shape
    tk = _tile(kdim, 512)
    return pl.pallas_call(
        _cast_kernel,
        out_shape=jax.ShapeDtypeStruct((kdim, n), BF16),
        grid=(kdim // tk,),
        in_specs=[pl.BlockSpec((None, tk, n), lambda i: (l, i, 0))],
        out_specs=pl.BlockSpec((tk, n), lambda i: (i, 0)),
        compiler_params=_cparams("parallel"),
        name="cast_bf16",
    )(w_all)


def _proj_kernel(h_ref, w_ref, gain_ref, nf_ref, rf_ref, cos_ref, sa_ref, sb_ref, bd_ref, o_ref, ob_ref, xs_ref,
                 *, sub, plain_tiles, norm_tiles):
    w, bd = w_ref[...], bd_ref[...]
    xs_ref[...] = jnp.dot(h_ref[...], w, preferred_element_type=F32)

    def epilogue(with_norm, with_rope):
        normed = nf_ref[...] > 0.5
        gain, rf = gain_ref[...], rf_ref[...]
        for c in range(h_ref.shape[0] // sub):
            rows = slice(c * sub, (c + 1) * sub)
            y = xs_ref[rows, :]
            if with_norm:
                ms = _seg_sum(y * y, bd) * (1.0 / HEAD_DIM)
                y = y * jnp.where(normed, lax.rsqrt(ms + NORM_EPS) * gain, 1.0)
            if with_rope:
                cosv, sav, sbv = cos_ref[rows, :], sa_ref[rows, :], sb_ref[rows, :]
            for s in range(2):
                out = y[:, s * LANE:(s + 1) * LANE]
                if with_rope:
                    f = rf[:, s * LANE:(s + 1) * LANE]
                    roped = out * cosv + pltpu.roll(out, 8, 1) * sav + pltpu.roll(out, LANE - 8, 1) * sbv
                    out = jnp.where(f > 0.5, roped, out)
                o_ref[rows, s * LANE:(s + 1) * LANE] = out
                ob_ref[rows, s * LANE:(s + 1) * LANE] = out.astype(BF16)

    j = pl.program_id(1)
    any_of = lambda tiles: functools.reduce(jnp.logical_or, [j == tl for tl in tiles])
    is_plain, is_norm = any_of(plain_tiles), any_of(norm_tiles)
    pl.when(is_plain)(lambda: epilogue(False, False))
    pl.when(is_norm)(lambda: epilogue(True, False))
    pl.when(jnp.logical_not(jnp.logical_or(is_plain, is_norm)))(lambda: epilogue(True, True))


def _qkv_projection(h, w_pack, l, gain, nf, rf, cos_t, sin_a, sin_b):
    m, d = h.shape
    tm, tn = _tile(m, 1024), 2 * LANE
    zw = N_QKV_SLOTS * LANE
    row = lambda i, j: (0, j)
    tab = lambda i, j: (i, 0)
    no_rope = set(range(S_BQ, S_CQ)) | {S_AV, S_AIK + 1} | set(range(S_CV, N_QKV_SLOTS))
    no_norm = set(range(S_AV, S_BQ)) | set(range(S_BV, S_CQ)) | set(range(S_CV, N_QKV_SLOTS))
    tiles = range(N_QKV_SLOTS // 2)
    plain = tuple(tl for tl in tiles if {2 * tl, 2 * tl + 1} <= (no_rope & no_norm))
    norm_only = tuple(tl for tl in tiles if {2 * tl, 2 * tl + 1} <= no_rope and tl not in plain)
    return pl.pallas_call(
        functools.partial(_proj_kernel, sub=_tile(tm, 256), plain_tiles=plain, norm_tiles=norm_only),
        out_shape=[jax.ShapeDtypeStruct((m, zw), F32), jax.ShapeDtypeStruct((m, zw), BF16)],
        grid=(m // tm, zw // tn),
        in_specs=[pl.BlockSpec((tm, d), lambda i, j: (i, 0)), pl.BlockSpec((None, d, tn), lambda i, j: (l, 0, j)),
                  pl.BlockSpec((1, tn), row), pl.BlockSpec((1, tn), row), pl.BlockSpec((1, tn), row),
                  pl.BlockSpec((tm, LANE), tab), pl.BlockSpec((tm, LANE), tab), pl.BlockSpec((tm, LANE), tab),
                  pl.BlockSpec((tn, tn), lambda i, j: (0, 0))],
        out_specs=[pl.BlockSpec((tm, tn), lambda i, j: (i, j))] * 2,
        scratch_shapes=[pltpu.VMEM((tm, tn), F32)],
        compiler_params=_cparams("parallel", "arbitrary"),
        name="qkv_projection",
    )(h, w_pack, gain, nf, rf, cos_t, sin_a, sin_b, _ones_blockdiag(tn))


def _column_vectors(lw):
    f = lambda v: jnp.asarray(v, F32).reshape(-1)
    ones = lambda n: jnp.ones((n,), F32)
    zeros = lambda n: jnp.zeros((n,), F32)
    rep = lambda v, n: jnp.tile(f(v), n)
    groups = [
        (rep(lw["a_qn"], 8), 1.0, 1.0), (rep(lw["a_kn"], 2), 1.0, 1.0), (ones(128), 0.0, 0.0),
        (ones(256), 0.0, 1.0), (ones(64), 0.0, 1.0), (ones(64 + LANE), 0.0, 0.0),
        (rep(lw["b_qn"], 8), 1.0, 0.0), (rep(lw["b_kn"], 8), 1.0, 0.0), (ones(512), 0.0, 0.0),
        (rep(lw["c_qn"], 4), 1.0, 1.0), (rep(lw["c_kn"], 4), 1.0, 1.0), (ones(512), 0.0, 0.0)]
    gain = jnp.concatenate([g for g, _, _ in groups]).reshape(1, -1)
    nf = jnp.concatenate([ones(g.shape[0]) * a for g, a, _ in groups]).reshape(1, -1)
    rf = jnp.concatenate([ones(g.shape[0]) * b for g, _, b in groups]).reshape(1, -1)
    del zeros
    return gain, nf, rf


def _rope_tables(pos):
    half = ROPE_DIM // 2
    inv_freq = ROPE_THETA ** (-jnp.arange(0, ROPE_DIM, 2, dtype=F32) / ROPE_DIM)
    ang = pos.astype(F32)[:, None] * inv_freq[None, :]
    cos, sin = jnp.cos(ang), jnp.sin(ang)
    rows = pos.shape[0]
    one = jnp.ones((rows, HEAD_DIM - ROPE_DIM), F32)
    zero = jnp.zeros((rows, HEAD_DIM - ROPE_DIM), F32)
    z8 = jnp.zeros((rows, half), F32)
    cos_h = jnp.concatenate([cos, cos, one], axis=1)
    sa_h = jnp.concatenate([z8, sin, zero], axis=1)
    sb_h = jnp.concatenate([-sin, z8, zero], axis=1)
    dup = lambda t: jnp.concatenate([t, t], axis=1)
    return dup(cos_h), dup(sa_h), dup(sb_h)


def _to_key(score):
    bits = pltpu.bitcast(score + 0.0, jnp.int32)
    return jnp.where(bits < 0, bits ^ 0x7FFFFFFF, bits)


def _indexer_scores(iq, iw, ik):
    kk = ik[:, :HEAD_DIM]
    sc = None
    for hd in range(A_IDX_HEADS):
        logit = _dot_nt3(iq[:, hd * HEAD_DIM:(hd + 1) * HEAD_DIM], kk)
        wgt = iw[:, HEAD_DIM + hd:HEAD_DIM + hd + 1] * (A_IDX_HEADS ** -0.5 * HEAD_DIM ** -0.5)
        term = jnp.maximum(logit, 0.0) * wgt
        sc = term if sc is None else sc + term
    return sc


def _lane_fold(x, op):
    out = x[:, :LANE]
    for s in range(1, x.shape[1] // LANE):
        out = op(out, x[:, s * LANE:(s + 1) * LANE])
    return out


def _count(key_ref, n_blk, blk, pred):
    rows = key_ref.shape[0]
    grp = min(rows, LANE)
    parts = []
    for r0 in range(0, rows, grp):
        def body(b, acc, r0=r0):
            start = pl.multiple_of(b * blk, blk)
            kb = key_ref[r0:r0 + grp, pl.ds(start, blk)]
            return acc + _lane_fold(jnp.where(pred(kb, start, slice(r0, r0 + grp)), 1.0, 0.0), jnp.add)

        parts.append(lax.fori_loop(0, n_blk, body, jnp.zeros((grp, LANE), F32)))
    acc = parts[0] if len(parts) == 1 else jnp.concatenate(parts, axis=0)
    return jnp.sum(acc, axis=-1, keepdims=True)


def _truncate16(x):
    return pltpu.bitcast(pltpu.bitcast(x, jnp.int32) & jnp.int32(-65536), F32)


def _count_ge_bf16(tb_ref, n_blk, blk, cand):
    rows = tb_ref.shape[0]
    grp = min(rows, LANE)
    one, zero = jnp.ones((grp, LANE), BF16), jnp.zeros((grp, LANE), BF16)
    parts = []
    for r0 in range(0, rows, grp):
        cand_b = jnp.broadcast_to(cand[r0:r0 + grp], (grp, LANE)).astype(BF16)

        def body(b, acc, r0=r0, cand_b=cand_b):
            start = pl.multiple_of(b * blk, blk)
            for s in range(blk // LANE):
                tile = tb_ref[r0:r0 + grp, pl.ds(start + s * LANE, LANE)]
                acc = acc + jnp.where(tile >= cand_b, one, zero)
            return acc

        parts.append(lax.fori_loop(0, n_blk, body, zero).astype(F32))
    acc = parts[0] if len(parts) == 1 else jnp.concatenate(parts, axis=0)
    return jnp.sum(acc, axis=-1, keepdims=True)


def _topk_threshold(key_ref, n_blk, blk, topk, tb_ref=None):
    rows = key_ref.shape[0]
    kf = float(topk)
    c0 = _count(key_ref, n_blk, blk, lambda kb, st, rs: kb >= 0)
    ans = jnp.where(c0 >= kf, 0, INT_MIN).astype(jnp.int32)

    def bit_step(it, ans):
        cand = ans + jnp.left_shift(jnp.int32(1), 30 - it)
        c = _count(key_ref, n_blk, blk, lambda kb, st, rs: kb >= cand[rs])
        return jnp.where(c >= kf, cand, ans)

    def bit_step_bf16(it, ans):
        cand = ans + jnp.left_shift(jnp.int32(1), 30 - it)
        bits = jnp.where(cand >= 0, cand, cand ^ 0x7FFFFFFF)
        cand_f = jnp.where(cand <= NEG_KEY, -jnp.inf, _truncate16(pltpu.bitcast(bits, F32)))
        cand_f = jnp.where((cand > 0) & (cand < MIN_NORMAL_KEY), MIN_NORMAL, cand_f)
        c = _count_ge_bf16(tb_ref, n_blk, blk, cand_f)
        return jnp.where(c >= kf, cand, ans)

    if tb_ref is None:
        thr = lax.fori_loop(0, 31, bit_step, ans)
    else:
        ans = lax.fori_loop(0, 15, bit_step_bf16, ans)
        thr = lax.fori_loop(15, 31, bit_step, ans)
    n_gt = _count(key_ref, n_blk, blk, lambda kb, st, rs: kb > thr[rs])
    n_eq = _count(key_ref, n_blk, blk, lambda kb, st, rs: kb == thr[rs])
    need = kf - n_gt
    nbits = int(n_blk * blk).bit_length() if isinstance(n_blk, int) else 14
    cut_all = jnp.full((rows, 1), 1 << nbits, jnp.int32)
    tie_overflow = jnp.max(jnp.where((n_eq > need) & (thr != NEG_KEY), 1.0, 0.0)) > 0.5

    def search_cut():
        def cut_step(it, cut):
            cand = cut + jnp.left_shift(jnp.int32(1), nbits - 1 - it)

            def pred(kb, st, rs):
                idx = st + lax.broadcasted_iota(jnp.int32, kb.shape, 1)
                return (kb == thr[rs]) & (idx < cand[rs])

            c = _count(key_ref, n_blk, blk, pred)
            return jnp.where(c <= need, cand, cut)

        return lax.fori_loop(0, nbits, cut_step, jnp.zeros((rows, 1), jnp.int32))

    cut = lax.cond(tie_overflow, search_cut, lambda: cut_all)
    return thr, cut


def _selected(kb, first_idx, thr, cut):
    idx = first_idx + lax.broadcasted_iota(jnp.int32, kb.shape, 1)
    return ((kb > thr) | ((kb == thr) & (idx < cut))) & (kb > NEG_KEY)


def _pair_heads(o_even, o_odd, group_half):
    lane = lax.broadcasted_iota(jnp.int32, o_even.shape, 1)
    if group_half == 0:
        return jnp.where(lane < HEAD_DIM, o_even, pltpu.roll(o_odd, HEAD_DIM, 1))
    return jnp.where(lane < HEAD_DIM, pltpu.roll(o_even, HEAD_DIM, 1), o_odd)


def _dsa_query_heads(q_ref, qh_ref):
    for hd in range(A_HEADS):
        slot, half = hd // 2, hd % 2
        group = hd // (A_HEADS // A_KV_HEADS)
        qs = q_ref[:, slot * LANE:(slot + 1) * LANE] * (HEAD_DIM ** -0.5)
        if half != group:
            qs = pltpu.roll(qs, HEAD_DIM, 1)
        qh_ref[hd] = jnp.where(_half_mask(qs.shape, group), qs, 0.0).astype(BF16)


def _softmax_block(sel, kblk, vblk, q_heads, m_ref, acc_ref, s_ref, p_ref):
    n = len(q_heads)
    reps = kblk.shape[0] // LANE
    v_ones = jnp.concatenate([vblk, jnp.ones_like(vblk)], axis=1)
    for hd in range(n):
        s_ref[hd] = _dot_nt(q_heads[hd], kblk)
    alphas = []
    for hd in range(n):
        s = s_ref[hd]
        if sel is not None:
            s = jnp.where(sel[hd] if isinstance(sel, (list, tuple)) else sel, s, -jnp.inf)
            s_ref[hd] = s
        blk_max = jnp.max(_lane_fold(s, jnp.maximum), axis=-1, keepdims=True)
        m_prev = m_ref[hd]
        m_new = jnp.maximum(m_prev, blk_max)
        alphas.append(jnp.exp(m_prev - m_new))
        m_ref[hd] = m_new
    for hd in range(n):
        p_ref[hd] = jnp.exp(s_ref[hd] - jnp.tile(m_ref[hd], (1, reps))).astype(BF16)
    for hd in range(n):
        alpha2 = jnp.tile(alphas[hd], (1, 2))
        acc_ref[hd] = alpha2 * acc_ref[hd] + jnp.dot(p_ref[hd], v_ones, preferred_element_type=F32)


def _softmax_init(m_ref, acc_ref):
    m_ref[...] = jnp.full(m_ref.shape, M_FLOOR, F32)
    acc_ref[...] = jnp.zeros(acc_ref.shape, F32)


def _softmax_result(acc_ref, hd):
    acc = acc_ref[hd]
    return acc[:, :LANE] / acc[:, LANE:]


def _indexer_query3(iq_ref, iq3_ref):
    for hd in range(A_IDX_HEADS):
        slot, half = hd // 2, hd % 2
        x = iq_ref[:, slot * LANE:(slot + 1) * LANE]
        xl = jnp.where(_half_mask(x.shape, half), x, pltpu.roll(x, HEAD_DIM, 1))
        hi = xl.astype(BF16).astype(F32)
        lower = _half_mask(x.shape, 0)
        iq3_ref[hd, :, :LANE] = jnp.where(lower, hi, xl - hi).astype(BF16)
        iq3_ref[hd, :, LANE:] = jnp.where(lower, hi, 0.0).astype(BF16)


def _ik3_kernel(x_ref, o_ref):
    x = x_ref[...]
    lower = _half_mask(x.shape, 0)
    xl = jnp.where(lower, x, pltpu.roll(x, HEAD_DIM, 1))
    hi = xl.astype(BF16).astype(F32)
    o_ref[:, :LANE] = hi.astype(BF16)
    o_ref[:, LANE:] = jnp.where(lower, xl - hi, 0.0).astype(BF16)


def _indexer_keys3(z, t):
    tm = _tile(t, 1024)
    return pl.pallas_call(
        _ik3_kernel,
        out_shape=jax.ShapeDtypeStruct((t, 2 * LANE), BF16),
        grid=(t // tm,),
        in_specs=[pl.BlockSpec((tm, LANE), lambda i: (i, S_AIK))],
        out_specs=pl.BlockSpec((tm, 2 * LANE), lambda i: (i, 0)),
        compiler_params=_cparams("parallel"),
        name="indexer_keys",
    )(z)


def _dsa_prompt_kernel(q_ref, iq_ref, iw_ref, k_ref, v_ref, ik3_ref, o_ref,
                       key_ref, tb_ref, qh_ref, iq3_ref, m_ref, acc_ref, s_ref, p_ref, *, tq, kb, topk):
    i = pl.program_id(0)
    q0 = i * tq
    n_blk = (q0 + tq + kb - 1) // kb
    row = lax.broadcasted_iota(jnp.int32, (tq, kb), 0)
    limit = (((q0 + row) >> 6) + 1) << 6
    _indexer_query3(iq_ref, iq3_ref)
    iw = iw_ref[...]
    wgt = [iw[:, HEAD_DIM + hd:HEAD_DIM + hd + 1] * (A_IDX_HEADS ** -0.5 * HEAD_DIM ** -0.5)
           for hd in range(A_IDX_HEADS)]

    def score_blk(b, carry):
        start = pl.multiple_of(b * kb, kb)
        ikb = ik3_ref[pl.ds(start, kb), :]
        for hd in range(A_IDX_HEADS):
            s_ref[hd] = _dot_nt(iq3_ref[hd], ikb)
        sc = None
        for hd in range(A_IDX_HEADS):
            term = jnp.maximum(s_ref[hd], 0.0) * wgt[hd]
            sc = term if sc is None else sc + term
        adm = start + lax.broadcasted_iota(jnp.int32, (tq, kb), 1) < limit
        sc = sc + 0.0
        key_ref[:, pl.ds(start, kb)] = jnp.where(adm, _to_key(sc), NEG_KEY)
        tb_ref[:, pl.ds(start, kb)] = jnp.where(adm, _truncate16(sc), -jnp.inf).astype(BF16)
        return carry

    lax.fori_loop(0, n_blk, score_blk, 0)
    thr, cut = _topk_threshold(key_ref, n_blk, kb, topk, tb_ref)
    _dsa_query_heads(q_ref, qh_ref)
    _softmax_init(m_ref, acc_ref)

    def attn_blk(b, carry):
        start = pl.multiple_of(b * kb, kb)
        sel = _selected(key_ref[:, pl.ds(start, kb)], start, thr, cut)
        _softmax_block(sel, k_ref[pl.ds(start, kb), :], v_ref[pl.ds(start, kb), :],
                       [qh_ref[hd] for hd in range(A_HEADS)], m_ref, acc_ref, s_ref, p_ref)
        return carry

    lax.fori_loop(0, n_blk, attn_blk, 0)
    for j in range(A_HEADS // 2):
        outs = [_softmax_result(acc_ref, 2 * j + e) for e in range(2)]
        o_ref[:, j * LANE:(j + 1) * LANE] = _pair_heads(outs[0], outs[1], (2 * j) // (A_HEADS // A_KV_HEADS)).astype(o_ref.dtype)


def _mixer_call(kern, o_all, m, *, grid, in_specs, out_block, out_index, args, scratch=(), sem, name):
    n_in = len(args)
    if o_all is None:
        fn, specs, alias, extra = kern, list(in_specs), {}, []
    else:
        fn = lambda *refs: kern(*refs[:n_in], *refs[n_in + 1:])
        specs, alias, extra = list(in_specs) + [pl.BlockSpec(memory_space=pl.ANY)], {n_in: 0}, [o_all]
    return pl.pallas_call(
        fn,
        out_shape=jax.ShapeDtypeStruct((m, N_BRANCH * BRANCH_WIDTH), BF16),
        grid=grid,
        in_specs=specs,
        out_specs=pl.BlockSpec(out_block, out_index),
        scratch_shapes=list(scratch),
        input_output_aliases=alias,
        compiler_params=_cparams(*sem),
        name=name,
    )(*args, *extra)


def _dsa_prompt(z, zb, t, o_all, m):
    tq, kb = 256, 512
    topk = min(A_TOPK_MAX, t // 4)
    full = lambda s: pl.BlockSpec((t, LANE), lambda i: (0, s))
    heads = lambda dt, w=LANE: pltpu.VMEM((A_HEADS, tq, w), dt)
    return _mixer_call(
        functools.partial(_dsa_prompt_kernel, tq=tq, kb=kb, topk=topk), o_all, m,
        grid=(t // tq,),
        in_specs=[pl.BlockSpec((tq, 4 * LANE), lambda i: (i, S_AQ // 4)),
                  pl.BlockSpec((tq, 2 * LANE), lambda i: (i, S_AIQ // 2)),
                  pl.BlockSpec((tq, LANE), lambda i: (i, S_AIK)),
                  full(S_AK), full(S_AV), pl.BlockSpec((t, 2 * LANE), lambda i: (0, 0))],
        out_block=(tq, BRANCH_WIDTH), out_index=lambda i: (i, 0),
        scratch=[pltpu.VMEM((tq, t), jnp.int32), pltpu.VMEM((tq, t), BF16), heads(BF16),
                 pltpu.VMEM((A_IDX_HEADS, tq, 2 * LANE), BF16),
                 heads(F32), heads(F32, 2 * LANE), heads(F32, kb), heads(BF16, kb)],
        sem=("arbitrary",), name="dsa_prompt",
        args=(z, z, z, zb, zb, _indexer_keys3(z, t)))


def _feature_major(cache, n_feat):
    nd = cache.ndim
    perm = (0, 1) + tuple(range(3, nd)) + (2,)
    return jnp.transpose(cache, perm).reshape(cache.shape[0], cache.shape[1], n_feat, cache.shape[2])


def _dsa_sample_kernel(q_ref, iq_ref, new_ik_ref, new_k_ref, new_v_ref, ckt_ref, cvt_ref, cikt_ref, o_ref,
                       key_ref, qh_ref, *, ts, n_seq, past, topk, q_pos0):
    rows_all = n_seq * ts
    width = key_ref.shape[1]
    iq_all, iw_all = iq_ref[...], new_ik_ref[...]
    row_c = lax.broadcasted_iota(jnp.int32, (ts, past), 0)
    limit_c = (((q_pos0 + row_c) >> 6) + 1) << 6
    kpos_c = lax.broadcasted_iota(jnp.int32, (ts, past), 1)
    kpos_n = past + lax.broadcasted_iota(jnp.int32, (ts, ts), 1)
    limit_n = (((q_pos0 + lax.broadcasted_iota(jnp.int32, (ts, ts), 0)) >> 6) + 1) << 6
    key_ref[:, past:] = jnp.full((rows_all, width - past), NEG_KEY, jnp.int32)
    for g in range(n_seq):
        rows = slice(g * ts, (g + 1) * ts)
        iq, iw = iq_all[rows], iw_all[rows]
        ikt = cikt_ref[g]
        ik_new = iw[:, :HEAD_DIM]
        sc_c = sc_n = None
        for hd in range(A_IDX_HEADS):
            qh = iq[:, hd * HEAD_DIM:(hd + 1) * HEAD_DIM]
            wgt = iw[:, HEAD_DIM + hd:HEAD_DIM + hd + 1] * (A_IDX_HEADS ** -0.5 * HEAD_DIM ** -0.5)
            t_c = jnp.maximum(_dot3(qh, ikt), 0.0) * wgt
            t_n = jnp.maximum(_dot_nt3(qh, ik_new), 0.0) * wgt
            sc_c = t_c if sc_c is None else sc_c + t_c
            sc_n = t_n if sc_n is None else sc_n + t_n
        key_ref[rows, :past] = jnp.where(kpos_c < limit_c, _to_key(sc_c), NEG_KEY)
        key_ref[rows, past:past + ts] = jnp.where(kpos_n < limit_n, _to_key(sc_n), NEG_KEY)

    thr, cut = _topk_threshold(key_ref, width // LANE, LANE, topk)
    _dsa_query_heads(q_ref, qh_ref)
    for g in range(n_seq):
        rows = slice(g * ts, (g + 1) * ts)
        sel_c = _selected(key_ref[rows, :past], 0, thr[rows], cut[rows])
        sel_n = _selected(key_ref[rows, past:past + ts], past, thr[rows], cut[rows])
        kct, vct = ckt_ref[g].astype(BF16), cvt_ref[g].astype(BF16)
        kn, vn = new_k_ref[rows, :], new_v_ref[rows, :]
        outs = []
        for hd in range(A_HEADS):
            qh = qh_ref[hd, rows, :]
            s_c = jnp.where(sel_c, jnp.dot(qh, kct, preferred_element_type=F32), -jnp.inf)
            s_n = jnp.where(sel_n, _dot_nt(qh, kn), -jnp.inf)
            m = jnp.maximum(jnp.max(s_c, axis=-1, keepdims=True), jnp.max(s_n, axis=-1, keepdims=True))
            p_c, p_n = jnp.exp(s_c - m), jnp.exp(s_n - m)
            l = jnp.sum(p_c, axis=-1, keepdims=True) + jnp.sum(p_n, axis=-1, keepdims=True)
            o = _dot_nt(p_c.astype(BF16), vct) + jnp.dot(p_n.astype(BF16), vn, preferred_element_type=F32)
            outs.append(o / l)
        for j in range(A_HEADS // 2):
            o_ref[rows, j * LANE:(j + 1) * LANE] = _pair_heads(
                outs[2 * j], outs[2 * j + 1], (2 * j) // (A_HEADS // A_KV_HEADS)).astype(o_ref.dtype)


def _dsa_sample(z, zb, cache_k, cache_v, cache_ik, l, t, nb, ts, o_all, m):
    past = cache_k.shape[2]
    topk = min(A_TOPK_MAX, (past + ts) // 4)
    width = ((past + ts + LANE - 1) // LANE) * LANE
    n_seq = LANE // ts
    rows = n_seq * ts
    assert LANE % ts == 0 and nb % n_seq == 0 and t % rows == 0
    rb = t // rows
    new = lambda s: pl.BlockSpec((rows, LANE), lambda i: (rb + i, s))
    cache = lambda w: pl.BlockSpec((None, n_seq, w, past), lambda i: (l, i, 0, 0))
    return _mixer_call(
        functools.partial(_dsa_sample_kernel, ts=ts, n_seq=n_seq, past=past, topk=topk, q_pos0=past), o_all, m,
        grid=(nb // n_seq,),
        in_specs=[pl.BlockSpec((rows, 4 * LANE), lambda i: (rb + i, S_AQ // 4)),
                  pl.BlockSpec((rows, 2 * LANE), lambda i: (rb + i, S_AIQ // 2)),
                  new(S_AIK), new(S_AK), new(S_AV), cache(LANE), cache(LANE), cache(HEAD_DIM)],
        out_block=(rows, BRANCH_WIDTH), out_index=lambda i: (rb + i, 0),
        scratch=[pltpu.VMEM((rows, width), jnp.int32), pltpu.VMEM((A_HEADS, rows, LANE), BF16)],
        sem=("arbitrary",), name="dsa_sample",
        args=(z, z, z, zb, zb, _feature_major(cache_k, LANE), _feature_major(cache_v, LANE),
              _feature_major(cache_ik, HEAD_DIM)))


def _band_prompt_kernel(q_ref, k_ref, v_ref, bias0_ref, bias1_ref, o_ref, s_ref, p_ref, *, tq, win):
    i = pl.program_id(1)
    units = [(r, e) for r in range(2) for e in range(2)]
    biases = (bias0_ref, bias1_ref)
    w0s, oks = [], []
    for r in range(2):
        q0 = (2 * i + r) * tq
        w0 = pl.multiple_of(jnp.maximum(q0 - B_WINDOW, 0), tq)
        kc = (w0 + lax.broadcasted_iota(jnp.int32, (tq, win), 1)) >> 6
        qc = (q0 + lax.broadcasted_iota(jnp.int32, (tq, win), 0)) >> 6
        w0s.append(w0)
        oks.append((kc <= qc) & (kc >= qc - B_PAST_CHUNKS))
    for u, (r, e) in enumerate(units):
        qs = q_ref[r * tq:(r + 1) * tq, :] * (HEAD_DIM ** -0.5)
        qe = jnp.where(_half_mask(qs.shape, e), qs, 0.0).astype(BF16)
        s_ref[u] = _dot_nt(qe, k_ref[pl.ds(w0s[r], win), :])
    for u, (r, e) in enumerate(units):
        s = jnp.where(oks[r], s_ref[u] + biases[r][e], -jnp.inf)
        p_ref[u] = jnp.exp(s - jnp.max(s, axis=-1, keepdims=True)).astype(BF16)
    outs = []
    for u, (r, e) in enumerate(units):
        vw = v_ref[pl.ds(w0s[r], win), :]
        acc = jnp.dot(p_ref[u], jnp.concatenate([vw, jnp.ones_like(vw)], axis=1), preferred_element_type=F32)
        outs.append(acc[:, :LANE] / acc[:, LANE:])
    for r in range(2):
        o_ref[r * tq:(r + 1) * tq, :] = jnp.where(_half_mask(outs[0].shape, 0), outs[2 * r],
                                                   outs[2 * r + 1]).astype(o_ref.dtype)


def _band_prompt(z, zb, bias, t, tq, o_all, m):
    win = B_WINDOW + tq
    n_case = B_WINDOW // tq
    assert t % (2 * tq) == 0
    case = lambda r: pl.BlockSpec((None, 2, tq, win), lambda j, i: (jnp.minimum(2 * i + r, n_case), j, 0, 0))
    return _mixer_call(
        functools.partial(_band_prompt_kernel, tq=tq, win=win), o_all, m,
        grid=(B_HEADS // 2, t // (2 * tq)),
        in_specs=[pl.BlockSpec((2 * tq, LANE), lambda j, i: (i, S_BQ + j)),
                  pl.BlockSpec((t, LANE), lambda j, i: (0, S_BK + j)),
                  pl.BlockSpec((t, LANE), lambda j, i: (0, S_BV + j)),
                  case(0), case(1)],
        out_block=(2 * tq, LANE), out_index=lambda j, i: (i, BRANCH_WIDTH // LANE + j),
        scratch=[pltpu.VMEM((4, tq, win), F32), pltpu.VMEM((4, tq, win), BF16)],
        sem=("parallel", "arbitrary"), name="band_prompt", args=(z, zb, zb, bias, bias))


def _band_sample_kernel(z_ref, zb_ref, kct_ref, vct_ref, bc_ref, bn_ref, o_ref):
    for j in range(B_HEADS // 2):
        slot = lambda ref, s: ref[:, (s + j) * LANE:(s + j + 1) * LANE]
        feat = slice(j * LANE, (j + 1) * LANE)
        kct, vct = kct_ref[feat, :].astype(BF16), vct_ref[feat, :].astype(BF16)
        kn, vn = slot(zb_ref, S_BK), slot(zb_ref, S_BV)
        qs = slot(z_ref, S_BQ) * (HEAD_DIM ** -0.5)
        outs = []
        for e in range(2):
            qe = jnp.where(_half_mask(qs.shape, e), qs, 0.0).astype(BF16)
            s_c = jnp.dot(qe, kct, preferred_element_type=F32) + bc_ref[2 * j + e]
            s_n = _dot_nt(qe, kn) + bn_ref[2 * j + e]
            m = jnp.maximum(jnp.max(s_c, axis=-1, keepdims=True), jnp.max(s_n, axis=-1, keepdims=True))
            p_c, p_n = jnp.exp(s_c - m), jnp.exp(s_n - m)
            l = jnp.sum(p_c, axis=-1, keepdims=True) + jnp.sum(p_n, axis=-1, keepdims=True)
            o = _dot_nt(p_c.astype(BF16), vct) + jnp.dot(p_n.astype(BF16), vn, preferred_element_type=F32)
            outs.append(o / l)
        o_ref[:, feat] = jnp.where(_half_mask(outs[0].shape, 0), outs[0], outs[1]).astype(o_ref.dtype)


def _band_sample(z, zb, cache_k, cache_v, bias_c, bias_n, l, t, nb, ts, o_all, m):
    pb = cache_k.shape[2]
    rb = t // ts
    bw = BRANCH_WIDTH
    cache = pl.BlockSpec((None, None, bw, pb), lambda b: (l, b, 0, 0))
    full = lambda a: pl.BlockSpec(a.shape, lambda b: (0,) * a.ndim)
    return _mixer_call(
        _band_sample_kernel, o_all, m,
        grid=(nb,),
        in_specs=[pl.BlockSpec((ts, z.shape[1]), lambda b: (rb + b, 0)),
                  pl.BlockSpec((ts, z.shape[1]), lambda b: (rb + b, 0)),
                  cache, cache, full(bias_c), full(bias_n)],
        out_block=(ts, bw), out_index=lambda b: (rb + b, 1),
        sem=("parallel",), name="band_sample",
        args=(z, zb, _feature_major(cache_k, bw), _feature_major(cache_v, bw), bias_c, bias_n))


def _band_bias(table, tq):
    n_case = B_WINDOW // tq
    win = B_WINDOW + tq
    width = win + B_WINDOW
    j = np.arange(width + tq - 1) - (tq - 1) - B_WINDOW
    ext = table.astype(F32)[:, np.clip(j, -B_REL_CLIP, B_REL_CLIP) + B_REL_CLIP]
    n = ext.shape[1]
    x = jnp.roll(ext, -(tq - 1), axis=1)
    toe = jnp.tile(x, (1, tq))[:, :tq * (n - 1)].reshape(ext.shape[0], tq, n - 1)[:, :, :width]
    return jnp.stack([toe[:, :, B_WINDOW - c * tq:B_WINDOW - c * tq + win] for c in range(n_case + 1)], axis=0)


def _lambda(lam_ref, lam_init):
    lv = lam_ref[...]
    return (jnp.exp(jnp.sum(lv[0:1] * lv[1:2], axis=-1, keepdims=True))
            - jnp.exp(jnp.sum(lv[2:3] * lv[3:4], axis=-1, keepdims=True)) + lam_init)


def _diff_finish(o0, o1, lam, on_ref, lam_init):
    attn = o0 - lam * o1
    ms = jnp.mean(attn * attn, axis=-1, keepdims=True)
    return (attn * lax.rsqrt(ms + NORM_EPS) * on_ref[...]) * (1.0 - lam_init)


def _diff_prompt_kernel(q_ref, k_ref, v_ref, lam_ref, on_ref, o_ref, qh_ref, m_ref, acc_ref, s_ref, p_ref,
                        *, tq, kb, tg, lam_init):
    i = pl.program_id(1)
    q0 = i * tq
    n_grp = tq // tg
    n_blk = (q0 + tq + kb - 1) // kb
    row = lax.broadcasted_iota(jnp.int32, (tg, kb), 0)
    limits = [(((q0 + g * tg + row) >> 6) + 1) << 6 for g in range(n_grp)]
    for g in range(n_grp):
        qs = q_ref[g * tg:(g + 1) * tg, :] * (HEAD_DIM ** -0.5)
        for c in range(2):
            qh_ref[2 * g + c] = jnp.where(_half_mask(qs.shape, c), qs, 0.0).astype(BF16)
    _softmax_init(m_ref, acc_ref)
    n_full = (q0 + CHUNK) // kb

    def attn_blk(b, carry, masked):
        start = pl.multiple_of(b * kb, kb)
        ok = None
        if masked:
            kpos = start + lax.broadcasted_iota(jnp.int32, (tg, kb), 1)
            ok = [kpos < limits[u // 2] for u in range(2 * n_grp)]
        _softmax_block(ok, k_ref[pl.ds(start, kb), :], v_ref[pl.ds(start, kb), :],
                       [qh_ref[u] for u in range(2 * n_grp)], m_ref, acc_ref, s_ref, p_ref)
        return carry

    lax.fori_loop(0, n_full, functools.partial(attn_blk, masked=False), 0)
    lax.fori_loop(n_full, n_blk, functools.partial(attn_blk, masked=True), 0)
    lam = _lambda(lam_ref, lam_init)
    for g in range(n_grp):
        outs = [_softmax_result(acc_ref, 2 * g + c) for c in range(2)]
        o_ref[g * tg:(g + 1) * tg, :] = _diff_finish(outs[0], outs[1], lam, on_ref, lam_init).astype(o_ref.dtype)


def _diff_prompt(z, zb, c_lam_all, c_on_all, l, t, lam_init, o_all, m):
    tq, kb, tg = _tile(t, 512), 512, 128
    maps = lambda dt, w=LANE: pltpu.VMEM((2 * tq // tg, tg, w), dt)
    return _mixer_call(
        functools.partial(_diff_prompt_kernel, tq=tq, kb=kb, tg=tg, lam_init=lam_init), o_all, m,
        grid=(C_HEADS, t // tq),
        in_specs=[pl.BlockSpec((tq, LANE), lambda h, i: (i, S_CQ + h)),
                  pl.BlockSpec((t, LANE), lambda h, i: (0, S_CK + h)),
                  pl.BlockSpec((t, LANE), lambda h, i: (0, S_CV + h)),
                  pl.BlockSpec((None, 4, HEAD_DIM), lambda h, i: (l, 0, 0)),
                  _layer_vec(l, LANE)],
        out_block=(tq, LANE), out_index=lambda h, i: (i, 2 * BRANCH_WIDTH // LANE + h),
        scratch=[maps(BF16), maps(F32), maps(F32, 2 * LANE), maps(F32, kb), maps(BF16, kb)],
        sem=("parallel", "arbitrary"), name="diff_prompt",
        args=(z, zb, zb, c_lam_all, _vec3(c_on_all)))


def _diff_sample_kernel(z_ref, zb_ref, kct_ref, vc_ref, lam_ref, on_ref, o_ref, *, past, lam_init):
    lam = _lambda(lam_ref, lam_init)
    for h in range(C_HEADS):
        slot = lambda ref, s: ref[:, (s + h) * LANE:(s + h + 1) * LANE]
        feat = slice(h * LANE, (h + 1) * LANE)
        kct = kct_ref[feat, :].astype(BF16)
        vc = vc_ref[pl.ds(h, past, stride=C_HEADS), :].astype(BF16)
        kn, vn = slot(zb_ref, S_CK), slot(zb_ref, S_CV)
        qs = slot(z_ref, S_CQ) * (HEAD_DIM ** -0.5)
        outs = []
        for c in range(2):
            qc = jnp.where(_half_mask(qs.shape, c), qs, 0.0).astype(BF16)
            s_c = jnp.dot(qc, kct, preferred_element_type=F32)
            s_n = _dot_nt(qc, kn)
            m = jnp.maximum(jnp.max(s_c, axis=-1, keepdims=True), jnp.max(s_n, axis=-1, keepdims=True))
            p_c, p_n = jnp.exp(s_c - m), jnp.exp(s_n - m)
            l = jnp.sum(p_c, axis=-1, keepdims=True) + jnp.sum(p_n, axis=-1, keepdims=True)
            o = (jnp.dot(p_c.astype(BF16), vc, preferred_element_type=F32)
                 + jnp.dot(p_n.astype(BF16), vn, preferred_element_type=F32))
            outs.append(o / l)
        o_ref[:, feat] = _diff_finish(outs[0], outs[1], lam, on_ref, lam_init).astype(o_ref.dtype)


def _diff_sample(z, zb, cache_k, cache_v, c_lam_all, c_on_all, l, t, nb, ts, lam_init, o_all, m):
    depth, _, past = cache_k.shape[:3]
    rb = t // ts
    bw = BRANCH_WIDTH
    return _mixer_call(
        functools.partial(_diff_sample_kernel, past=past, lam_init=lam_init), o_all, m,
        grid=(nb,),
        in_specs=[pl.BlockSpec((ts, z.shape[1]), lambda b: (rb + b, 0)),
                  pl.BlockSpec((ts, z.shape[1]), lambda b: (rb + b, 0)),
                  pl.BlockSpec((None, None, bw, past), lambda b: (l, b, 0, 0)),
                  pl.BlockSpec((None, None, past * C_HEADS, LANE), lambda b: (l, b, 0, 0)),
                  pl.BlockSpec((None, 4, HEAD_DIM), lambda b: (l, 0, 0)),
                  _layer_vec(l, LANE)],
        out_block=(ts, bw), out_index=lambda b: (rb + b, 2),
        sem=("parallel",), name="diff_sample",
        args=(z, zb, _feature_major(cache_k, bw), cache_v.reshape(depth, nb, past * C_HEADS, LANE),
              c_lam_all, _vec3(c_on_all)))


def _rwkv_pre_kernel(zd_ref, prev_ref, first_ref, mu_ref, w0_ref, w2_ref, a0_ref, a2_ref, g2_ref, kkw_ref,
                     ka_ref, bd_ref, r_ref, w_ref, k_ref, v_ref, kk_ref, b_ref, g_ref, *, tm, t, ts):
    zf = zd_ref[...]
    grow = pl.program_id(0) * tm + lax.broadcasted_iota(jnp.int32, (tm, 1), 0)
    zs = jnp.where(grow % tm == 0, prev_ref[7:8, :], pltpu.roll(zf, 1, 0))
    seq_start = (grow >= t) & ((grow - t) % ts == 0)
    zs = jnp.where(seq_start, first_ref[...], zs)
    zs = jnp.where(grow == 0, 0.0, zs)
    zm = zf + (zs - zf) * mu_ref[...]
    bw = BRANCH_WIDTH
    r, k, v = zm[:, :bw], zm[:, bw:2 * bw], zm[:, 2 * bw:3 * bw]
    wl, al, gl = zm[:, 3 * bw:3 * bw + 64], zm[:, 3 * bw + 64:3 * bw + 128], zm[:, 3 * bw + 128:]
    dot = lambda a, b: jnp.dot(a.astype(BF16), b.astype(BF16), preferred_element_type=F32)
    u = -(w0_ref[...] + dot(jnp.tanh(wl), w2_ref[...]))
    softplus = jnp.maximum(u, 0.0) + jnp.log(1.0 + jnp.exp(-jnp.abs(u)))
    w = -softplus - 0.5
    a = jax.nn.sigmoid(a0_ref[...] + dot(al, a2_ref[...]))
    kk = k * kkw_ref[...]
    nrm = jnp.sqrt(_seg_sum(kk * kk, bd_ref[...]))
    kk = kk / jnp.maximum(nrm, 1e-12)
    r_ref[...] = r
    w_ref[...] = -jnp.exp(w)
    k_ref[...] = k * (1.0 + (a - 1.0) * ka_ref[...])
    v_ref[...] = v
    kk_ref[...] = kk
    b_ref[...] = kk * a
    g_ref[...] = dot(jax.nn.sigmoid(gl), g2_ref[...])


def _rwkv_pre(zd, zfirst, p, l, t, ts):
    m, dc = zd.shape
    tm = _tile(math.gcd(t, m - t), 512)
    bw = BRANCH_WIDTH
    npt = t // tm
    rows = pl.BlockSpec((tm, dc), lambda i: (i, 0))
    vec = lambda n: _layer_vec(l, n)
    mat = lambda k: pl.BlockSpec((None, k, bw), lambda i: (l, 0, 0))
    out = jax.ShapeDtypeStruct((m, bw), F32)
    return pl.pallas_call(
        functools.partial(_rwkv_pre_kernel, tm=tm, t=t, ts=ts),
        out_shape=[out] * 7,
        grid=(m // tm,),
        in_specs=[rows,
                  pl.BlockSpec((8, dc), lambda i: (jnp.maximum(i * (tm // 8) - 1, 0), 0)),
                  pl.BlockSpec((tm, dc), lambda i: (jnp.maximum(i - npt, 0), 0)),
                  vec(dc), vec(bw), mat(64), vec(bw), mat(64), mat(128), vec(bw), vec(bw),
                  pl.BlockSpec((bw, bw), lambda i: (0, 0))],
        out_specs=[pl.BlockSpec((tm, bw), lambda i: (i, 0))] * 7,
        compiler_params=_cparams("parallel"),
        name="rwkv_pre",
    )(zd, zd, zfirst, _vec3(p["d_mu"]), _vec3(p["d_w0"]), p["d_w2"], _vec3(p["d_a0"]), p["d_a2"], p["d_g2"],
      _vec3(p["d_k_k"]), _vec3(p["d_k_a"]), _ones_blockdiag(bw))


RW_CHUNK_MAX = 64
RW_BLOCK = 128
RW_SLOTS = BRANCH_WIDTH // LANE


def _dot3(a, b):
    ah, al = _split2(a)
    bh, bl = _split2(b)
    d = lambda x, y: jnp.dot(x, y, preferred_element_type=F32)
    return d(ah, bh) + d(al, bh) + d(ah, bl)


def _rwkv_chunk_kernel(*refs, seq_chunks, RW_CHUNK):
    if seq_chunks:
        (r_ref, lw_ref, k_ref, v_ref, kk_ref, b_ref, s0_ref, y_ref, sf_ref,
         h_ref, u_ref, ab_ref, rb_ref, u0_ref, y0_ref, bt_ref, kt_ref, eg_ref) = refs
    else:
        (r_ref, lw_ref, k_ref, v_ref, kk_ref, b_ref, y_ref, sf_ref,
         h_ref, u_ref, ab_ref, rb_ref, u0_ref, y0_ref, bt_ref, kt_ref, eg_ref) = refs
        s0_ref = None

        @pl.when(pl.program_id(0) == 0)
        def _():
            h_ref[...] = jnp.zeros_like(h_ref)

    n = RW_BLOCK
    n_chunks = n // RW_CHUNK
    row = lax.broadcasted_iota(jnp.int32, (n, n), 0)
    col = lax.broadcasted_iota(jnp.int32, (n, n), 1)
    same = (row // RW_CHUNK) == (col // RW_CHUNK)
    strict, incl = same & (col < row), same & (col <= row)
    eye = jnp.where(row == col, 1.0, 0.0)
    head_diag = (row // HEAD_DIM) == (col // HEAD_DIM)
    in_chunk = row % RW_CHUNK
    k_pick = jnp.where(lax.broadcasted_iota(jnp.int32, (n, HEAD_DIM), 0) % HEAD_DIM
                       == lax.broadcasted_iota(jnp.int32, (n, HEAD_DIM), 1), 1.0, 0.0)
    bf = lambda x: x.astype(BF16)
    mm = lambda x, y: jnp.dot(bf(x), bf(y), preferred_element_type=F32)
    u_ref[...] = jnp.zeros_like(u_ref)

    halves = [_half_mask((n, LANE), e) for e in range(2)]
    group = 2
    for j0 in range(0, RW_SLOTS, group):
        slot_data = []
        for j in range(j0, j0 + group):
            sl = slice(j * LANE, (j + 1) * LANE)
            lw, v = lw_ref[:, sl], v_ref[:, sl]
            g = lw
            for d in [1 << s for s in range(RW_CHUNK.bit_length() - 1)]:
                g = g + jnp.where(in_chunk >= d, pltpu.roll(g, d, 0), 0.0)
            inv_g = jnp.exp(-g)
            a_t = -kk_ref[:, sl] * jnp.exp(g - lw)
            b_t, k_t = b_ref[:, sl] * inv_g, k_ref[:, sl] * inv_g
            r_t = r_ref[:, sl] * jnp.exp(g)
            bt_ref[j], kt_ref[j], eg_ref[j] = b_t.T, k_t.T, jnp.exp(g).T
            slot_data.append((a_t, r_t, bf(b_t), bf(k_t), bf(v)))
        chains = [(s, e) for s in range(group) for e in range(2)]
        each = lambda f: [f(s, e, i) for i, (s, e) in enumerate(chains)]
        a_e = each(lambda s, e, i: bf(jnp.where(halves[e], slot_data[s][0], 0.0)))
        r_e = each(lambda s, e, i: bf(jnp.where(halves[e], slot_data[s][1], 0.0)))
        n_ab = each(lambda s, e, i: jnp.where(strict, _dot_nt(a_e[i], slot_data[s][2]), 0.0))
        n_ak = each(lambda s, e, i: jnp.where(strict, _dot_nt(a_e[i], slot_data[s][3]), 0.0))
        m_rb = each(lambda s, e, i: bf(jnp.where(incl, _dot_nt(r_e[i], slot_data[s][2]), 0.0)))
        m_rk = each(lambda s, e, i: bf(jnp.where(incl, _dot_nt(r_e[i], slot_data[s][3]), 0.0)))
        w_e = each(lambda s, e, i: mm(n_ak[i], slot_data[s][4]))
        tinv = each(lambda s, e, i: eye + n_ab[i])
        pw = n_ab
        for _ in range(RW_CHUNK.bit_length() - 2):
            pw = each(lambda s, e, i: mm(pw[i], pw[i]))
            tinv = each(lambda s, e, i: tinv[i] + mm(tinv[i], pw[i]))
        t16 = each(lambda s, e, i: bf(tinv[i]))
        a_bar = each(lambda s, e, i: mm(t16[i], slot_data[s][0]))
        u0 = each(lambda s, e, i: mm(t16[i], w_e[i]))
        r_bar = each(lambda s, e, i: slot_data[s][1] + mm(m_rb[i], a_bar[i]))
        y0 = each(lambda s, e, i: mm(m_rb[i], u0[i]) + mm(m_rk[i], slot_data[s][4]))
        for s in range(group):
            pick = lambda vals: jnp.where(halves[0], vals[2 * s], vals[2 * s + 1])
            j = j0 + s
            ab_ref[j], rb_ref[j], u0_ref[j], y0_ref[j] = pick(a_bar), pick(r_bar), pick(u0), pick(y0)

    for c in range(n_chunks):
        rows = slice(c * RW_CHUNK, (c + 1) * RW_CHUNK)
        col_c = (col // RW_CHUNK) == c
        for j in range(RW_SLOTS):
            sl = slice(j * LANE, (j + 1) * LANE)
            if seq_chunks and c % seq_chunks == 0:
                x = s0_ref[c // seq_chunks, sl, :]
                h = jnp.where(head_diag, _dot_nt3(k_pick, x), 0.0)
            else:
                h = h_ref[j]
            res = _dot3(jnp.concatenate([ab_ref[j, rows, :], rb_ref[j, rows, :]], axis=0), h)
            u_c = res[:RW_CHUNK] + u0_ref[j, rows, :]
            y_ref[rows, sl] = res[RW_CHUNK:] + y0_ref[j, rows, :]
            u_ref[j, rows, :] = u_c
            bk = jnp.concatenate([jnp.where(col_c, bt_ref[j], 0.0), jnp.where(col_c, kt_ref[j], 0.0)], axis=1)
            uv = jnp.concatenate([u_ref[j], v_ref[:, sl]], axis=0)
            inc = jnp.where(head_diag, mm(bk, uv), 0.0)
            g_end = eg_ref[j, :, (c + 1) * RW_CHUNK - 1:(c + 1) * RW_CHUNK]
            h = g_end * (h + inc)
            h_ref[j] = h
            if seq_chunks and (c + 1) % seq_chunks == 0:
                ht = h.T
                sf_ref[c // seq_chunks, sl, :] = (ht + pltpu.roll(ht, HEAD_DIM, 1))[:, :HEAD_DIM]

    if not seq_chunks:
        @pl.when(pl.program_id(0) == pl.num_programs(0) - 1)
        def _():
            for j in range(RW_SLOTS):
                ht = h_ref[j].T
                sf_ref[0, j * LANE:(j + 1) * LANE, :] = (ht + pltpu.roll(ht, HEAD_DIM, 1))[:, :HEAD_DIM]


def _rwkv_chunked(ops, s0, row0, n_seq, t):
    n = RW_BLOCK
    bw = BRANCH_WIDTH
    chunk = math.gcd(t, RW_CHUNK_MAX)
    assert row0 % n == 0 and (n_seq * t) % n == 0 and chunk >= 8 and chunk & (chunk - 1) == 0
    rb = row0 // n
    rows = pl.BlockSpec((n, bw), lambda i: (rb + i, 0))
    if s0 is None:
        assert n_seq == 1
        seq_chunks, per_blk, extra, extra_specs = 0, 1, [], []
        sf_spec = pl.BlockSpec((1, bw, HEAD_DIM), lambda i: (0, 0, 0))
    else:
        assert n % t == 0
        seq_chunks, per_blk = t // chunk, n // t
        extra = [s0.reshape(n_seq, bw, HEAD_DIM)]
        extra_specs = [pl.BlockSpec((per_blk, bw, HEAD_DIM), lambda i: (i, 0, 0))]
        sf_spec = pl.BlockSpec((per_blk, bw, HEAD_DIM), lambda i: (i, 0, 0))
    slot = lambda dt=F32: pltpu.VMEM((RW_SLOTS, n, LANE), dt)
    y, sf = pl.pallas_call(
        functools.partial(_rwkv_chunk_kernel, seq_chunks=seq_chunks, RW_CHUNK=chunk),
        out_shape=[jax.ShapeDtypeStruct((n_seq * t, bw), F32), jax.ShapeDtypeStruct((n_seq, bw, HEAD_DIM), F32)],
        grid=(n_seq * t // n,),
        in_specs=[rows] * 6 + extra_specs,
        out_specs=[pl.BlockSpec((n, bw), lambda i: (i, 0)), sf_spec],
        scratch_shapes=[slot() for _ in range(9)],
        compiler_params=_cparams("arbitrary"),
        name="rwkv_chunked",
    )(*ops, *extra)
    return y, sf.reshape(n_seq, D_HEADS, HEAD_DIM, HEAD_DIM)


def _rwkv_post_kernel(y_ref, r_ref, k_ref, v_ref, g_ref, lnw_ref, lnb_ref, rk_ref, bd_ref, o_ref):
    bd = bd_ref[...]
    y = y_ref[...]
    mean = _seg_sum(y, bd) * (1.0 / HEAD_DIM)
    yc = y - mean
    var = _seg_sum(yc * yc, bd) * (1.0 / HEAD_DIM)
    yn = yc * lax.rsqrt(var + D_GN_EPS) * lnw_ref[...] + lnb_ref[...]
    bonus = _seg_sum(r_ref[...] * k_ref[...] * rk_ref[...], bd) * v_ref[...]
    o_ref[...] = ((yn + bonus) * g_ref[...]).astype(o_ref.dtype)


def _rwkv_post(y, r, k, v, g, p, l, o_all):
    m, bw = y.shape
    tm = _tile(m, 512)
    rows = pl.BlockSpec((tm, bw), lambda i: (i, 0))
    vec = _layer_vec(l, bw)
    return _mixer_call(
        _rwkv_post_kernel, o_all, m,
        grid=(m // tm,),
        in_specs=[rows] * 5 + [vec] * 3 + [pl.BlockSpec((bw, bw), lambda i: (0, 0))],
        out_block=(tm, bw), out_index=lambda i: (i, N_BRANCH - 1),
        sem=("parallel",), name="rwkv_post",
        args=(y, r, k, v, g, _vec3(p["d_ln_w"]), _vec3(p["d_ln_b"]), _vec3(p["d_r_k"]), _ones_blockdiag(bw)))


def _merge_kernel(o_ref, wbr_ref, h_ref, wg_ref, out_ref, acc_ref):
    n = pl.program_id(2)

    @pl.when(n == 0)
    def _():
        acc_ref[...] = jnp.zeros_like(acc_ref)

    u = jnp.dot(o_ref[...], wbr_ref[...].astype(BF16), preferred_element_type=F32)
    gate = jax.nn.sigmoid(jnp.dot(h_ref[...], wg_ref[...], preferred_element_type=F32))
    acc_ref[...] += u * gate

    @pl.when(n == N_BRANCH - 1)
    def _():
        out_ref[...] = acc_ref[...].astype(out_ref.dtype)


def _merge(o_all, w_br_all, h, w_pack, l):
    m, d = h.shape
    tm, tn = _tile(m, 1024), 512
    nj = d // tn
    g0 = S_GATE * LANE // tn
    return pl.pallas_call(
        _merge_kernel,
        out_shape=jax.ShapeDtypeStruct((m, d), BF16),
        grid=(m // tm, nj, N_BRANCH),
        in_specs=[pl.BlockSpec((tm, BRANCH_WIDTH), lambda i, j, n: (i, n)),
                  pl.BlockSpec((None, None, BRANCH_WIDTH, tn), lambda i, j, n: (l, n, 0, j)),
                  pl.BlockSpec((tm, d), lambda i, j, n: (i, 0)),
                  pl.BlockSpec((None, d, tn), lambda i, j, n: (l, 0, g0 + n * nj + j))],
        out_specs=pl.BlockSpec((tm, tn), lambda i, j, n: (i, j)),
        scratch_shapes=[pltpu.VMEM((tm, tn), F32)],
        compiler_params=_cparams("parallel", "parallel", "arbitrary"),
        name="branch_merge",
    )(o_all, w_br_all, h, w_pack)


def _ple_kernel(x_ref, pe_ref, wple_ref, h_ref, wg_ref, o_ref):
    emb = jnp.dot(pe_ref[...].astype(BF16), wple_ref[...].astype(BF16), preferred_element_type=F32)
    gate = jax.nn.sigmoid(jnp.dot(h_ref[...], wg_ref[...].astype(BF16), preferred_element_type=F32))
    o_ref[...] = x_ref[...] + emb * gate


def _ple(x, pe, w_ple_all, h, w_gate_all, l):
    m, d = x.shape
    pd = pe.shape[1]
    tm, tn = _tile(m, 1024), 512
    return pl.pallas_call(
        _ple_kernel,
        out_shape=jax.ShapeDtypeStruct((m, d), F32),
        grid=(m // tm, d // tn),
        in_specs=[pl.BlockSpec((tm, tn), lambda i, j: (i, j)),
                  pl.BlockSpec((tm, pd), lambda i, j: (i, 0)),
                  pl.BlockSpec((None, pd, tn), lambda i, j: (l, 0, j)),
                  pl.BlockSpec((tm, d), lambda i, j: (i, 0)),
                  pl.BlockSpec((None, d, tn), lambda i, j: (l, 0, j))],
        out_specs=pl.BlockSpec((tm, tn), lambda i, j: (i, j)),
        compiler_params=_cparams("parallel", "parallel"),
        name="ple",
    )(x, pe, w_ple_all, h, w_gate_all)


def _layer(l, x, pe, p, t, nb, ts, tabs):
    m = x.shape[0]
    lam_init = 0.8 - 0.6 * math.exp(-0.3 * l)
    dc = p["d_mu"].shape[1]
    d = x.shape[1]
    lw = {"a_qn": p["a_q_norm"][l], "a_kn": p["a_k_norm"][l], "b_qn": p["b_q_norm"][l], "b_kn": p["b_k_norm"][l],
          "c_qn": p["c_q_norm"][l], "c_kn": p["c_k_norm"][l]}

    h = _rmsnorm(x, p["norm1_g"], l)
    w_pack = p["w_pack"]
    z, zb = _qkv_projection(h, w_pack, l, *_column_vectors(lw), *tabs)
    zd = _matmul(h, w_pack, lambda acc: acc, F32, n=dc, layer=l, col0=S_D * LANE, tn=256, name="rwkv_projection")

    o_all = _dsa_prompt(z, zb, t, None, m)
    o_all = _dsa_sample(z, zb, p["cache_a_k"], p["cache_a_v"], p["cache_a_kidx"], l, t, nb, ts, o_all, m)
    tq_b = 128
    bias = _band_bias(p["b_rel_bias"][l], tq_b)
    pb = p["cache_b_k"].shape[2]
    generic = bias[B_WINDOW // tq_b]
    o_all = _band_prompt(z, zb, bias, t, tq_b, o_all, m)
    o_all = _band_sample(z, zb, p["cache_b_k"], p["cache_b_v"], generic[:, :ts, B_WINDOW - pb:B_WINDOW],
                         generic[:, :ts, B_WINDOW:B_WINDOW + ts], l, t, nb, ts, o_all, m)
    c_on = p["c_out_norm"]
    o_all = _diff_prompt(z, zb, p["c_lambda"], c_on, l, t, lam_init, o_all, m)
    o_all = _diff_sample(z, zb, p["cache_c_k"], p["cache_c_v"], p["c_lambda"], c_on, l, t, nb, ts, lam_init,
                         o_all, m)

    zfirst = jnp.broadcast_to(p["state_d_shift"][l], (nb, ts, dc)).reshape(nb * ts, dc)
    r, w, k, v, kk, b, g = _rwkv_pre(zd, zfirst, p, l, t, ts)
    ops = (r, w, k, v, kk, b)
    y_p, wkv_p = _rwkv_chunked(ops, None, 0, 1, t)
    y_s, wkv_s = _rwkv_chunked(ops, p["state_d_wkv"][l], t, nb, ts)
    o_all = _rwkv_post(jnp.concatenate([y_p, y_s], axis=0), r, k, v, g, p, l, o_all)

    ug = _merge(o_all, p["w_branch"], h, w_pack, l)
    res = lambda acc, r_: r_ + acc
    x = _matmul(ug, p["w_out"], res, F32, n=d, layer=l, residual=x, name="out_proj")
    h2 = _rmsnorm(x, p["norm2_g"], l)
    ffn = p["w_up"].shape[2]
    up = _matmul(h2, p["w_up"], lambda acc: jnp.square(jnp.maximum(acc, 0.0)), BF16, n=ffn, layer=l, name="mlp_up")
    x = _matmul(up, _cast_bf16(p["w_down"], l), res, F32, n=d, residual=x, tn=1024, name="mlp_down")
    h3 = _rmsnorm(x, p["norm3_g"], l)
    x = _ple(x, pe, p["w_ple"], h3, p["w_ple_gate"], l)

    slot = lambda rows, s, n=1: z[rows, s * LANE:(s + n) * LANE]

    def rows_of(sl, lead):
        a = lambda s, n, shape: slot(sl, s, n).reshape(lead + shape)
        ak = a(S_AK, 1, (A_KV_HEADS, HEAD_DIM))
        av = a(S_AV, 1, (A_KV_HEADS, HEAD_DIM))
        aik = slot(sl, S_AIK)[:, :HEAD_DIM].reshape(lead + (HEAD_DIM,))
        bk = a(S_BK, 4, (B_HEADS, HEAD_DIM))
        bv = a(S_BV, 4, (B_HEADS, HEAD_DIM))
        ck = a(S_CK, 4, (C_HEADS, 2, HEAD_DIM))
        cv = a(S_CV, 4, (C_HEADS, 2 * HEAD_DIM))
        return ak, av, aik, bk, bv, ck, cv

    keep = min(B_WINDOW, t)
    pak, pav, paik, pbk, pbv, pck, pcv = rows_of(slice(0, t), (1, t))
    new_p = (pak, pav, paik, pbk[:, t - keep:], pbv[:, t - keep:], pck, pcv, wkv_p, zd[t - 1:t].reshape(1, 1, dc))
    new_s = rows_of(slice(t, m), (nb, ts)) + (wkv_s, zd[t:].reshape(nb, ts, dc)[:, -1:])
    return x, new_p, new_s


def kernel(x_prompt, x_sample, cache_a_k, cache_a_v, cache_a_kidx, cache_b_k, cache_b_v, cache_c_k, cache_c_v, state_d_wkv, state_d_shift, p_prompt, p_sample, norm1_g, w_in, a_q_norm, a_k_norm, b_q_norm, b_k_norm, b_rel_bias, c_q_norm, c_k_norm, c_lambda, c_out_norm, d_mu, d_w0, d_w2, d_a0, d_a2, d_g2, d_k_k, d_k_a, d_r_k, d_ln_w, d_ln_b, w_branch, w_out, norm2_g, w_up, w_down, norm3_g, w_ple, w_ple_gate):
    batch, t, d = x_prompt.shape
    nb, ts, _ = x_sample.shape
    past = cache_a_k.shape[2]
    depth = w_in.shape[0]
    assert batch == 1 and t % 512 == 0 and past % CHUNK == 0 and ts <= CHUNK and (nb * ts) % 8 == 0
    p = dict(cache_a_k=cache_a_k, cache_a_v=cache_a_v, cache_a_kidx=cache_a_kidx, cache_b_k=cache_b_k,
             cache_b_v=cache_b_v, cache_c_k=cache_c_k, cache_c_v=cache_c_v, state_d_wkv=state_d_wkv,
             state_d_shift=state_d_shift, norm1_g=norm1_g, w_in=w_in, a_q_norm=a_q_norm, a_k_norm=a_k_norm,
             b_q_norm=b_q_norm, b_k_norm=b_k_norm, b_rel_bias=b_rel_bias, c_q_norm=c_q_norm, c_k_norm=c_k_norm,
             c_lambda=c_lambda, c_out_norm=c_out_norm, d_mu=d_mu, d_w0=d_w0, d_w2=d_w2, d_a0=d_a0, d_a2=d_a2,
             d_g2=d_g2, d_k_k=d_k_k, d_k_a=d_k_a, d_r_k=d_r_k, d_ln_w=d_ln_w, d_ln_b=d_ln_b, w_branch=w_branch,
             w_out=w_out, norm2_g=norm2_g, w_up=w_up, w_down=w_down, norm3_g=norm3_g, w_ple=w_ple,
             w_ple_gate=w_ple_gate)
    p["w_pack"] = _pack_w_in(w_in)
    x = jnp.concatenate([x_prompt[0], x_sample.reshape(nb * ts, d)], axis=0)
    pos = jnp.concatenate([jnp.arange(t, dtype=jnp.int32),
                           jnp.tile(past + jnp.arange(ts, dtype=jnp.int32), nb)])
    tabs = _rope_tables(pos)
    st_p = [[] for _ in range(9)]
    st_s = [[] for _ in range(9)]
    for l in range(depth):
        pe = jnp.concatenate([p_prompt[l, 0], p_sample[l].reshape(nb * ts, -1)], axis=0)
        x, new_p, new_s = _layer(l, x, pe, p, t, nb, ts, tabs)
        for lst, arr in zip(st_p, new_p):
            lst.append(arr)
        for lst, arr in zip(st_s, new_s):
            lst.append(arr)
    outs_p = [jnp.stack(s, axis=0) for s in st_p]
    outs_s = [jnp.stack(s, axis=0) for s in st_s]
    return (x[:t].reshape(1, t, d), x[t:].reshape(nb, ts, d), *outs_p, *outs_s)
```

```python
import functools
import math

import numpy as np
import jax
import jax.numpy as jnp
from jax import lax
from jax.experimental import pallas as pl
from jax.experimental.pallas import tpu as pltpu

F32 = jnp.float32
BF16 = jnp.bfloat16

CHUNK = 64
HEAD_DIM = 64
ROPE_DIM = 16
ROPE_THETA = 500000.0
N_BRANCH = 4
BRANCH_WIDTH = 512
A_HEADS, A_KV_HEADS, A_IDX_HEADS = 8, 2, 4
A_TOPK_MAX = 256
B_HEADS = 8
B_PAST_CHUNKS = 8
B_WINDOW = B_PAST_CHUNKS * CHUNK
B_REL_CLIP = 128
C_HEADS = 4
D_HEADS = 8
D_GN_EPS = 64e-5
NORM_EPS = 1e-6

LANE = 128
VMEM_LIMIT = 48 * 1024 * 1024

S_AQ, S_AK, S_AV, S_AIQ, S_AIK = 0, 4, 5, 6, 8
S_BQ, S_BK, S_BV = 10, 14, 18
S_CQ, S_CK, S_CV = 22, 26, 30
S_D, S_GATE = 34, 48
N_QKV_SLOTS = 34
N_D_SLOTS = 14
A_COLS = 1092
A_SLOTS = 9
SHIFT = A_COLS - (A_SLOTS - 1) * LANE

NEG_KEY = -2139095041
INT_MIN = -2147483648
M_FLOOR = -1e30


def _cparams(*sem):
    return pltpu.CompilerParams(dimension_semantics=sem, vmem_limit_bytes=VMEM_LIMIT)


def _tile(n, pref):
    t = min(n, pref)
    while n % t:
        t -= 8
    return t


def _split2(x):
    hi = x.astype(BF16)
    lo = (x - hi.astype(F32)).astype(BF16)
    return hi, lo


def _seg_sum(x, ones_bd):
    hi, lo = _split2(x)
    return (jnp.dot(hi, ones_bd, preferred_element_type=F32)
            + jnp.dot(lo, ones_bd, preferred_element_type=F32))


def _dot_nt(a, b):
    return lax.dot_general(a, b, (((1,), (1,)), ((), ())), preferred_element_type=F32)


def _dot_nt3(a, b):
    ah, al = _split2(a)
    bh, bl = _split2(b)
    return _dot_nt(ah, bh) + _dot_nt(ah, bl) + _dot_nt(al, bh)


def _ones_blockdiag(n):
    i = np.arange(n)
    return jnp.asarray((i[:, None] // HEAD_DIM) == (i[None, :] // HEAD_DIM), dtype=BF16)


def _vec3(a):
    return a.reshape(a.shape[0], 1, -1)


def _layer_vec(l, n):
    return pl.BlockSpec((None, 1, n), lambda *_: (l, 0, 0))


def _half_mask(shape, half):
    lane = lax.broadcasted_iota(jnp.int32, shape, len(shape) - 1)
    return (lane < HEAD_DIM) if half == 0 else (lane >= HEAD_DIM)


def _rms_kernel(x_ref, g_ref, o_ref):
    x = x_ref[...]
    ms = jnp.mean(x * x, axis=-1, keepdims=True)
    o_ref[...] = (x * lax.rsqrt(ms + NORM_EPS) * g_ref[...]).astype(o_ref.dtype)


def _rmsnorm(x, g_all, l):
    m, d = x.shape
    tm = _tile(m, 512)
    return pl.pallas_call(
        _rms_kernel,
        out_shape=jax.ShapeDtypeStruct((m, d), BF16),
        grid=(m // tm,),
        in_specs=[pl.BlockSpec((tm, d), lambda i: (i, 0)), _layer_vec(l, d)],
        out_specs=pl.BlockSpec((tm, d), lambda i: (i, 0)),
        compiler_params=_cparams("parallel"),
        name="rmsnorm",
    )(x, _vec3(g_all))


def _pack_kernel(a_ref, b_ref, o_ref):
    j = pl.program_id(0)
    row = lax.broadcasted_iota(jnp.int32, (LANE, a_ref.shape[2]), 0)
    for l in range(a_ref.shape[1]):
        a, b = a_ref[:, l, :], b_ref[:, l, :]
        shifted = jnp.where(row < LANE - SHIFT, pltpu.roll(a, LANE - SHIFT, 0), pltpu.roll(b, LANE - SHIFT, 0))
        out = jnp.where(j < A_SLOTS, a, jnp.where(j == A_SLOTS, 0.0, shifted))
        o_ref[l] = out.T.astype(o_ref.dtype)


def _pack_w_in(w_in_all):
    depth, d, n_in = w_in_all.shape
    n_slots = A_SLOTS + 1 + (n_in - A_COLS) // LANE
    assert (n_in - A_COLS) % LANE == 0 and 0 < SHIFT < LANE
    src_a = lambda j: jnp.where(j < A_SLOTS, j, j - 2)
    src_b = lambda j: jnp.where(j < A_SLOTS, j, j - 1)
    w_t = jnp.transpose(w_in_all, (2, 0, 1))
    return pl.pallas_call(
        _pack_kernel,
        out_shape=jax.ShapeDtypeStruct((depth, d, n_slots * LANE), BF16),
        grid=(n_slots,),
        in_specs=[pl.BlockSpec((LANE, depth, d), lambda j: (src_a(j), 0, 0)),
                  pl.BlockSpec((LANE, depth, d), lambda j: (src_b(j), 0, 0))],
        out_specs=pl.BlockSpec((depth, d, LANE), lambda j: (0, 0, j)),
        compiler_params=_cparams("parallel"),
        name="pack_w_in",
    )(w_t, w_t)


def _mm_kernel(*refs, n_extra, nk, epilogue):
    a_ref, w_ref = refs[0], refs[1]
    extra = refs[2:2 + n_extra]
    o_ref = refs[2 + n_extra]
    if nk == 1:
        acc = jnp.dot(a_ref[...], w_ref[...].astype(BF16), preferred_element_type=F32)
        o_ref[...] = epilogue(acc, *[e[...] for e in extra]).astype(o_ref.dtype)
        return
    acc_ref = refs[3 + n_extra]
    k = pl.program_id(2)

    @pl.when(k == 0)
    def _():
        acc_ref[...] = jnp.zeros_like(acc_ref)

    acc_ref[...] += jnp.dot(a_ref[...], w_ref[...].astype(BF16), preferred_element_type=F32)

    @pl.when(k == nk - 1)
    def _():
        o_ref[...] = epilogue(acc_ref[...], *[e[...] for e in extra]).astype(o_ref.dtype)


def _matmul(a, w, epilogue, out_dtype, *, n, layer=None, col0=0, residual=None,
            tm=1024, tn=512, tk=2048, name="matmul"):
    m, kdim = a.shape
    tm, tn, tk = _tile(m, tm), _tile(n, tn), _tile(kdim, tk)
    nk = kdim // tk
    cb = col0 // tn
    assert col0 % tn == 0
    if layer is None:
        w_spec = pl.BlockSpec((tk, tn), lambda i, j, k: (k, cb + j))
    else:
        w_spec = pl.BlockSpec((None, tk, tn), lambda i, j, k: (layer, k, cb + j))
    specs = [pl.BlockSpec((tm, tk), lambda i, j, k: (i, k)), w_spec]
    extras = []
    if residual is not None:
        specs.append(pl.BlockSpec((tm, tn), lambda i, j, k: (i, j)))
        extras.append(residual)
    return pl.pallas_call(
        functools.partial(_mm_kernel, n_extra=len(extras), nk=nk, epilogue=epilogue),
        out_shape=jax.ShapeDtypeStruct((m, n), out_dtype),
        grid=(m // tm, n // tn, nk),
        in_specs=specs,
        out_specs=pl.BlockSpec((tm, tn), lambda i, j, k: (i, j)),
        scratch_shapes=[pltpu.VMEM((tm, tn), F32)] if nk > 1 else [],
        compiler_params=_cparams("parallel", "parallel", "arbitrary"),
        name=name,
    )(a, w, *extras)


def _cast_kernel(w_ref, o_ref):
    o_ref[...] = w_ref[...].astype(o_ref.dtype)


def _cast_bf16(w_all, l):
    _, kdim, n = w_all.shape
    tk = _tile(kdim, 512)
    return pl.pallas_call(
        _cast_kernel,
        out_shape=jax.ShapeDtypeStruct((kdim, n), BF16),
        grid=(kdim // tk,),
        in_specs=[pl.BlockSpec((None, tk, n), lambda i: (l, i, 0))],
        out_specs=pl.BlockSpec((tk, n), lambda i: (i, 0)),
        compiler_params=_cparams("parallel"),
        name="cast_bf16",
    )(w_all)


def _proj_kernel(h_ref, w_ref, gain_ref, nf_ref, rf_ref, cos_ref, sa_ref, sb_ref, bd_ref, o_ref, ob_ref, xs_ref,
                 *, sub, plain_tiles, norm_tiles):
    w, bd = w_ref[...], bd_ref[...]
    xs_ref[...] = jnp.dot(h_ref[...], w, preferred_element_type=F32)

    def epilogue(with_norm, with_rope):
        normed = nf_ref[...] > 0.5
        gain, rf = gain_ref[...], rf_ref[...]
        for c in range(h_ref.shape[0] // sub):
            rows = slice(c * sub, (c + 1) * sub)
            y = xs_ref[rows, :]
            if with_norm:
                ms = _seg_sum(y * y, bd) * (1.0 / HEAD_DIM)
                y = y * jnp.where(normed, lax.rsqrt(ms + NORM_EPS) * gain, 1.0)
            if with_rope:
                cosv, sav, sbv = cos_ref[rows, :], sa_ref[rows, :], sb_ref[rows, :]
            for s in range(2):
                out = y[:, s * LANE:(s + 1) * LANE]
                if with_rope:
                    f = rf[:, s * LANE:(s + 1) * LANE]
                    roped = out * cosv + pltpu.roll(out, 8, 1) * sav + pltpu.roll(out, LANE - 8, 1) * sbv
                    out = jnp.where(f > 0.5, roped, out)
                o_ref[rows, s * LANE:(s + 1) * LANE] = out
                ob_ref[rows, s * LANE:(s + 1) * LANE] = out.astype(BF16)

    j = pl.program_id(1)
    any_of = lambda tiles: functools.reduce(jnp.logical_or, [j == tl for tl in tiles])
    is_plain, is_norm = any_of(plain_tiles), any_of(norm_tiles)
    pl.when(is_plain)(lambda: epilogue(False, False))
    pl.when(is_norm)(lambda: epilogue(True, False))
    pl.when(jnp.logical_not(jnp.logical_or(is_plain, is_norm)))(lambda: epilogue(True, True))


def _qkv_projection(h, w_pack, l, gain, nf, rf, cos_t, sin_a, sin_b):
    m, d = h.shape
    tm, tn = _tile(m, 1024), 2 * LANE
    zw = N_QKV_SLOTS * LANE
    row = lambda i, j: (0, j)
    tab = lambda i, j: (i, 0)
    no_rope = set(range(S_BQ, S_CQ)) | {S_AV, S_AIK + 1} | set(range(S_CV, N_QKV_SLOTS))
    no_norm = set(range(S_AV, S_BQ)) | set(range(S_BV, S_CQ)) | set(range(S_CV, N_QKV_SLOTS))
    tiles = range(N_QKV_SLOTS // 2)
    plain = tuple(tl for tl in tiles if {2 * tl, 2 * tl + 1} <= (no_rope & no_norm))
    norm_only = tuple(tl for tl in tiles if {2 * tl, 2 * tl + 1} <= no_rope and tl not in plain)
    return pl.pallas_call(
        functools.partial(_proj_kernel, sub=_tile(tm, 256), plain_tiles=plain, norm_tiles=norm_only),
        out_shape=[jax.ShapeDtypeStruct((m, zw), F32), jax.ShapeDtypeStruct((m, zw), BF16)],
        grid=(m // tm, zw // tn),
        in_specs=[pl.BlockSpec((tm, d), lambda i, j: (i, 0)), pl.BlockSpec((None, d, tn), lambda i, j: (l, 0, j)),
                  pl.BlockSpec((1, tn), row), pl.BlockSpec((1, tn), row), pl.BlockSpec((1, tn), row),
                  pl.BlockSpec((tm, LANE), tab), pl.BlockSpec((tm, LANE), tab), pl.BlockSpec((tm, LANE), tab),
                  pl.BlockSpec((tn, tn), lambda i, j: (0, 0))],
        out_specs=[pl.BlockSpec((tm, tn), lambda i, j: (i, j))] * 2,
        scratch_shapes=[pltpu.VMEM((tm, tn), F32)],
        compiler_params=_cparams("parallel", "arbitrary"),
        name="qkv_projection",
    )(h, w_pack, gain, nf, rf, cos_t, sin_a, sin_b, _ones_blockdiag(tn))


def _column_vectors(lw):
    f = lambda v: jnp.asarray(v, F32).reshape(-1)
    ones = lambda n: jnp.ones((n,), F32)
    zeros = lambda n: jnp.zeros((n,), F32)
    rep = lambda v, n: jnp.tile(f(v), n)
    groups = [
        (rep(lw["a_qn"], 8), 1.0, 1.0), (rep(lw["a_kn"], 2), 1.0, 1.0), (ones(128), 0.0, 0.0),
        (ones(256), 0.0, 1.0), (ones(64), 0.0, 1.0), (ones(64 + LANE), 0.0, 0.0),
        (rep(lw["b_qn"], 8), 1.0, 0.0), (rep(lw["b_kn"], 8), 1.0, 0.0), (ones(512), 0.0, 0.0),
        (rep(lw["c_qn"], 4), 1.0, 1.0), (rep(lw["c_kn"], 4), 1.0, 1.0), (ones(512), 0.0, 0.0)]
    gain = jnp.concatenate([g for g, _, _ in groups]).reshape(1, -1)
    flags = lambda idx: jnp.asarray(
        np.concatenate([np.full((g.shape[0],), grp[idx], np.float32) for grp in groups for g in grp[:1]])[None, :])
    del zeros
    return gain, flags(1), flags(2)


def _rope_tables(pos):
    half = ROPE_DIM // 2
    inv_freq = ROPE_THETA ** (-jnp.arange(0, ROPE_DIM, 2, dtype=F32) / ROPE_DIM)
    ang = pos.astype(F32)[:, None] * inv_freq[None, :]
    cos, sin = jnp.cos(ang), jnp.sin(ang)
    rows = pos.shape[0]
    one = jnp.ones((rows, HEAD_DIM - ROPE_DIM), F32)
    zero = jnp.zeros((rows, HEAD_DIM - ROPE_DIM), F32)
    z8 = jnp.zeros((rows, half), F32)
    cos_h = jnp.concatenate([cos, cos, one], axis=1)
    sa_h = jnp.concatenate([z8, sin, zero], axis=1)
    sb_h = jnp.concatenate([-sin, z8, zero], axis=1)
    dup = lambda t: jnp.concatenate([t, t], axis=1)
    return dup(cos_h), dup(sa_h), dup(sb_h)


def _to_key(score):
    bits = pltpu.bitcast(score + 0.0, jnp.int32)
    return jnp.where(bits < 0, bits ^ 0x7FFFFFFF, bits)


def _indexer_scores(iq, iw, ik):
    kk = ik[:, :HEAD_DIM]
    sc = None
    for hd in range(A_IDX_HEADS):
        logit = _dot_nt3(iq[:, hd * HEAD_DIM:(hd + 1) * HEAD_DIM], kk)
        wgt = iw[:, HEAD_DIM + hd:HEAD_DIM + hd + 1] * (A_IDX_HEADS ** -0.5 * HEAD_DIM ** -0.5)
        term = jnp.maximum(logit, 0.0) * wgt
        sc = term if sc is None else sc + term
    return sc


def _lane_fold(x, op):
    out = x[:, :LANE]
    for s in range(1, x.shape[1] // LANE):
        out = op(out, x[:, s * LANE:(s + 1) * LANE])
    return out


def _count(key_ref, n_blk, blk, pred):
    rows = key_ref.shape[0]
    grp = min(rows, LANE)
    parts = []
    for r0 in range(0, rows, grp):
        def body(b, acc, r0=r0):
            start = pl.multiple_of(b * blk, blk)
            kb = key_ref[r0:r0 + grp, pl.ds(start, blk)]
            return acc + _lane_fold(jnp.where(pred(kb, start, slice(r0, r0 + grp)), 1.0, 0.0), jnp.add)

        parts.append(lax.fori_loop(0, n_blk, body, jnp.zeros((grp, LANE), F32)))
    acc = parts[0] if len(parts) == 1 else jnp.concatenate(parts, axis=0)
    return jnp.sum(acc, axis=-1, keepdims=True)


def _topk_threshold(key_ref, n_blk, blk, topk):
    rows = key_ref.shape[0]
    kf = float(topk)
    c0 = _count(key_ref, n_blk, blk, lambda kb, st, rs: kb >= 0)
    ans = jnp.where(c0 >= kf, 0, INT_MIN).astype(jnp.int32)

    def bit_step(it, ans):
        cand = ans + jnp.left_shift(jnp.int32(1), 30 - it)
        c = _count(key_ref, n_blk, blk, lambda kb, st, rs: kb >= cand[rs])
        return jnp.where(c >= kf, cand, ans)

    thr = lax.fori_loop(0, 31, bit_step, ans)
    n_gt = _count(key_ref, n_blk, blk, lambda kb, st, rs: kb > thr[rs])
    n_eq = _count(key_ref, n_blk, blk, lambda kb, st, rs: kb == thr[rs])
    need = kf - n_gt
    nbits = int(n_blk * blk).bit_length() if isinstance(n_blk, int) else 14
    cut_all = jnp.full((rows, 1), 1 << nbits, jnp.int32)
    tie_overflow = jnp.max(jnp.where((n_eq > need) & (thr != NEG_KEY), 1.0, 0.0)) > 0.5

    def search_cut():
        def cut_step(it, cut):
            cand = cut + jnp.left_shift(jnp.int32(1), nbits - 1 - it)

            def pred(kb, st, rs):
                idx = st + lax.broadcasted_iota(jnp.int32, kb.shape, 1)
                return (kb == thr[rs]) & (idx < cand[rs])

            c = _count(key_ref, n_blk, blk, pred)
            return jnp.where(c <= need, cand, cut)

        return lax.fori_loop(0, nbits, cut_step, jnp.zeros((rows, 1), jnp.int32))

    cut = lax.cond(tie_overflow, search_cut, lambda: cut_all)
    return thr, cut


def _selected(kb, first_idx, thr, cut):
    idx = first_idx + lax.broadcasted_iota(jnp.int32, kb.shape, 1)
    return ((kb > thr) | ((kb == thr) & (idx < cut))) & (kb > NEG_KEY)


def _pair_heads(o_even, o_odd, group_half):
    lane = lax.broadcasted_iota(jnp.int32, o_even.shape, 1)
    if group_half == 0:
        return jnp.where(lane < HEAD_DIM, o_even, pltpu.roll(o_odd, HEAD_DIM, 1))
    return jnp.where(lane < HEAD_DIM, pltpu.roll(o_even, HEAD_DIM, 1), o_odd)


def _dsa_query_heads(q_ref, qh_ref):
    for hd in range(A_HEADS):
        slot, half = hd // 2, hd % 2
        group = hd // (A_HEADS // A_KV_HEADS)
        qs = q_ref[:, slot * LANE:(slot + 1) * LANE] * (HEAD_DIM ** -0.5)
        if half != group:
            qs = pltpu.roll(qs, HEAD_DIM, 1)
        qh_ref[hd] = jnp.where(_half_mask(qs.shape, group), qs, 0.0).astype(BF16)


def _softmax_block(sel, kblk, vblk, q_heads, m_ref, acc_ref, s_ref, p_ref):
    n = len(q_heads)
    reps = kblk.shape[0] // LANE
    v_ones = jnp.concatenate([vblk, jnp.ones_like(vblk)], axis=1)
    for hd in range(n):
        s_ref[hd] = _dot_nt(q_heads[hd], kblk)
    alphas = []
    for hd in range(n):
        s = s_ref[hd]
        if sel is not None:
            s = jnp.where(sel[hd] if isinstance(sel, (list, tuple)) else sel, s, -jnp.inf)
            s_ref[hd] = s
        blk_max = jnp.max(_lane_fold(s, jnp.maximum), axis=-1, keepdims=True)
        m_prev = m_ref[hd]
        m_new = jnp.maximum(m_prev, blk_max)
        alphas.append(jnp.exp(m_prev - m_new))
        m_ref[hd] = m_new
    for hd in range(n):
        p_ref[hd] = jnp.exp(s_ref[hd] - jnp.tile(m_ref[hd], (1, reps))).astype(BF16)
    for hd in range(n):
        alpha2 = jnp.tile(alphas[hd], (1, 2))
        acc_ref[hd] = alpha2 * acc_ref[hd] + jnp.dot(p_ref[hd], v_ones, preferred_element_type=F32)


def _softmax_init(m_ref, acc_ref):
    m_ref[...] = jnp.full(m_ref.shape, M_FLOOR, F32)
    acc_ref[...] = jnp.zeros(acc_ref.shape, F32)


def _softmax_result(acc_ref, hd):
    acc = acc_ref[hd]
    return acc[:, :LANE] / acc[:, LANE:]


def _indexer_query3(iq_ref, iq3_ref):
    for hd in range(A_IDX_HEADS):
        slot, half = hd // 2, hd % 2
        x = iq_ref[:, slot * LANE:(slot + 1) * LANE]
        xl = jnp.where(_half_mask(x.shape, half), x, pltpu.roll(x, HEAD_DIM, 1))
        hi = xl.astype(BF16).astype(F32)
        lower = _half_mask(x.shape, 0)
        iq3_ref[hd, :, :LANE] = jnp.where(lower, hi, xl - hi).astype(BF16)
        iq3_ref[hd, :, LANE:] = jnp.where(lower, hi, 0.0).astype(BF16)


def _ik3_kernel(x_ref, o_ref):
    x = x_ref[...]
    lower = _half_mask(x.shape, 0)
    xl = jnp.where(lower, x, pltpu.roll(x, HEAD_DIM, 1))
    hi = xl.astype(BF16).astype(F32)
    o_ref[:, :LANE] = hi.astype(BF16)
    o_ref[:, LANE:] = jnp.where(lower, xl - hi, 0.0).astype(BF16)


def _indexer_keys3(z, t):
    tm = _tile(t, 1024)
    return pl.pallas_call(
        _ik3_kernel,
        out_shape=jax.ShapeDtypeStruct((t, 2 * LANE), BF16),
        grid=(t // tm,),
        in_specs=[pl.BlockSpec((tm, LANE), lambda i: (i, S_AIK))],
        out_specs=pl.BlockSpec((tm, 2 * LANE), lambda i: (i, 0)),
        compiler_params=_cparams("parallel"),
        name="indexer_keys",
    )(z)


def _dsa_prompt_kernel(q_ref, iq_ref, iw_ref, k_ref, v_ref, ik3_ref, o_ref,
                       key_ref, qh_ref, iq3_ref, m_ref, acc_ref, s_ref, p_ref, *, tq, kb, topk):
    i = pl.program_id(0)
    q0 = i * tq
    n_blk = (q0 + tq + kb - 1) // kb
    row = lax.broadcasted_iota(jnp.int32, (tq, kb), 0)
    limit = (((q0 + row) >> 6) + 1) << 6
    _indexer_query3(iq_ref, iq3_ref)
    iw = iw_ref[...]
    wgt = [iw[:, HEAD_DIM + hd:HEAD_DIM + hd + 1] * (A_IDX_HEADS ** -0.5 * HEAD_DIM ** -0.5)
           for hd in range(A_IDX_HEADS)]

    def score_blk(b, carry):
        start = pl.multiple_of(b * kb, kb)
        ikb = ik3_ref[pl.ds(start, kb), :]
        for hd in range(A_IDX_HEADS):
            s_ref[hd] = _dot_nt(iq3_ref[hd], ikb)
        sc = None
        for hd in range(A_IDX_HEADS):
            term = jnp.maximum(s_ref[hd], 0.0) * wgt[hd]
            sc = term if sc is None else sc + term
        adm = start + lax.broadcasted_iota(jnp.int32, (tq, kb), 1) < limit
        key_ref[:, pl.ds(start, kb)] = jnp.where(adm, _to_key(sc), NEG_KEY)
        return carry

    lax.fori_loop(0, n_blk, score_blk, 0)
    thr, cut = _topk_threshold(key_ref, n_blk, kb, topk)
    _dsa_query_heads(q_ref, qh_ref)
    _softmax_init(m_ref, acc_ref)

    def attn_blk(b, carry):
        start = pl.multiple_of(b * kb, kb)
        sel = _selected(key_ref[:, pl.ds(start, kb)], start, thr, cut)
        _softmax_block(sel, k_ref[pl.ds(start, kb), :], v_ref[pl.ds(start, kb), :],
                       [qh_ref[hd] for hd in range(A_HEADS)], m_ref, acc_ref, s_ref, p_ref)
        return carry

    lax.fori_loop(0, n_blk, attn_blk, 0)
    for j in range(A_HEADS // 2):
        outs = [_softmax_result(acc_ref, 2 * j + e) for e in range(2)]
        o_ref[:, j * LANE:(j + 1) * LANE] = _pair_heads(outs[0], outs[1], (2 * j) // (A_HEADS // A_KV_HEADS)).astype(o_ref.dtype)


def _mixer_call(kern, o_all, m, *, grid, in_specs, out_block, out_index, args, scratch=(), sem, name):
    n_in = len(args)
    if o_all is None:
        fn, specs, alias, extra = kern, list(in_specs), {}, []
    else:
        fn = lambda *refs: kern(*refs[:n_in], *refs[n_in + 1:])
        specs, alias, extra = list(in_specs) + [pl.BlockSpec(memory_space=pl.ANY)], {n_in: 0}, [o_all]
    return pl.pallas_call(
        fn,
        out_shape=jax.ShapeDtypeStruct((m, N_BRANCH * BRANCH_WIDTH), BF16),
        grid=grid,
        in_specs=specs,
        out_specs=pl.BlockSpec(out_block, out_index),
        scratch_shapes=list(scratch),
        input_output_aliases=alias,
        compiler_params=_cparams(*sem),
        name=name,
    )(*args, *extra)


def _dsa_prompt(z, zb, t, o_all, m):
    tq, kb = 256, 512
    topk = min(A_TOPK_MAX, t // 4)
    full = lambda s: pl.BlockSpec((t, LANE), lambda i: (0, s))
    heads = lambda dt, w=LANE: pltpu.VMEM((A_HEADS, tq, w), dt)
    return _mixer_call(
        functools.partial(_dsa_prompt_kernel, tq=tq, kb=kb, topk=topk), o_all, m,
        grid=(t // tq,),
        in_specs=[pl.BlockSpec((tq, 4 * LANE), lambda i: (i, S_AQ // 4)),
                  pl.BlockSpec((tq, 2 * LANE), lambda i: (i, S_AIQ // 2)),
                  pl.BlockSpec((tq, LANE), lambda i: (i, S_AIK)),
                  full(S_AK), full(S_AV), pl.BlockSpec((t, 2 * LANE), lambda i: (0, 0))],
        out_block=(tq, BRANCH_WIDTH), out_index=lambda i: (i, 0),
        scratch=[pltpu.VMEM((tq, t), jnp.int32), heads(BF16),
                 pltpu.VMEM((A_IDX_HEADS, tq, 2 * LANE), BF16),
                 heads(F32), heads(F32, 2 * LANE), heads(F32, kb), heads(BF16, kb)],
        sem=("arbitrary",), name="dsa_prompt",
        args=(z, z, z, zb, zb, _indexer_keys3(z, t)))


def _feature_major(cache, n_feat):
    nd = cache.ndim
    perm = (0, 1) + tuple(range(3, nd)) + (2,)
    return jnp.transpose(cache, perm).reshape(cache.shape[0], cache.shape[1], n_feat, cache.shape[2])


def _dsa_sample_kernel(q_ref, iq_ref, new_ik_ref, new_k_ref, new_v_ref, ckt_ref, cvt_ref, cikt_ref, o_ref,
                       key_ref, qh_ref, *, ts, n_seq, past, topk, q_pos0):
    rows_all = n_seq * ts
    width = key_ref.shape[1]
    iq_all, iw_all = iq_ref[...], new_ik_ref[...]
    row_c = lax.broadcasted_iota(jnp.int32, (ts, past), 0)
    limit_c = (((q_pos0 + row_c) >> 6) + 1) << 6
    kpos_c = lax.broadcasted_iota(jnp.int32, (ts, past), 1)
    kpos_n = past + lax.broadcasted_iota(jnp.int32, (ts, ts), 1)
    limit_n = (((q_pos0 + lax.broadcasted_iota(jnp.int32, (ts, ts), 0)) >> 6) + 1) << 6
    key_ref[:, past:] = jnp.full((rows_all, width - past), NEG_KEY, jnp.int32)
    for g in range(n_seq):
        rows = slice(g * ts, (g + 1) * ts)
        iq, iw = iq_all[rows], iw_all[rows]
        ikt = cikt_ref[g]
        ik_new = iw[:, :HEAD_DIM]
        sc_c = sc_n = None
        for hd in range(A_IDX_HEADS):
            qh = iq[:, hd * HEAD_DIM:(hd + 1) * HEAD_DIM]
            wgt = iw[:, HEAD_DIM + hd:HEAD_DIM + hd + 1] * (A_IDX_HEADS ** -0.5 * HEAD_DIM ** -0.5)
            t_c = jnp.maximum(_dot3(qh, ikt), 0.0) * wgt
            t_n = jnp.maximum(_dot_nt3(qh, ik_new), 0.0) * wgt
            sc_c = t_c if sc_c is None else sc_c + t_c
            sc_n = t_n if sc_n is None else sc_n + t_n
        key_ref[rows, :past] = jnp.where(kpos_c < limit_c, _to_key(sc_c), NEG_KEY)
        key_ref[rows, past:past + ts] = jnp.where(kpos_n < limit_n, _to_key(sc_n), NEG_KEY)

    thr, cut = _topk_threshold(key_ref, width // LANE, LANE, topk)
    _dsa_query_heads(q_ref, qh_ref)
    for g in range(n_seq):
        rows = slice(g * ts, (g + 1) * ts)
        sel_c = _selected(key_ref[rows, :past], 0, thr[rows], cut[rows])
        sel_n = _selected(key_ref[rows, past:past + ts], past, thr[rows], cut[rows])
        kct, vct = ckt_ref[g].astype(BF16), cvt_ref[g].astype(BF16)
        kn, vn = new_k_ref[rows, :], new_v_ref[rows, :]
        outs = []
        for hd in range(A_HEADS):
            qh = qh_ref[hd, rows, :]
            s_c = jnp.where(sel_c, jnp.dot(qh, kct, preferred_element_type=F32), -jnp.inf)
            s_n = jnp.where(sel_n, _dot_nt(qh, kn), -jnp.inf)
            m = jnp.maximum(jnp.max(s_c, axis=-1, keepdims=True), jnp.max(s_n, axis=-1, keepdims=True))
            p_c, p_n = jnp.exp(s_c - m), jnp.exp(s_n - m)
            l = jnp.sum(p_c, axis=-1, keepdims=True) + jnp.sum(p_n, axis=-1, keepdims=True)
            o = _dot_nt(p_c.astype(BF16), vct) + jnp.dot(p_n.astype(BF16), vn, preferred_element_type=F32)
            outs.append(o / l)
        for j in range(A_HEADS // 2):
            o_ref[rows, j * LANE:(j + 1) * LANE] = _pair_heads(
                outs[2 * j], outs[2 * j + 1], (2 * j) // (A_HEADS // A_KV_HEADS)).astype(o_ref.dtype)


def _dsa_sample(z, zb, cache_k, cache_v, cache_ik, l, t, nb, ts, o_all, m):
    past = cache_k.shape[2]
    topk = min(A_TOPK_MAX, (past + ts) // 4)
    width = ((past + ts + LANE - 1) // LANE) * LANE
    n_seq = LANE // ts
    rows = n_seq * ts
    assert LANE % ts == 0 and nb % n_seq == 0 and t % rows == 0
    rb = t // rows
    new = lambda s: pl.BlockSpec((rows, LANE), lambda i: (rb + i, s))
    cache = lambda w: pl.BlockSpec((None, n_seq, w, past), lambda i: (l, i, 0, 0))
    return _mixer_call(
        functools.partial(_dsa_sample_kernel, ts=ts, n_seq=n_seq, past=past, topk=topk, q_pos0=past), o_all, m,
        grid=(nb // n_seq,),
        in_specs=[pl.BlockSpec((rows, 4 * LANE), lambda i: (rb + i, S_AQ // 4)),
                  pl.BlockSpec((rows, 2 * LANE), lambda i: (rb + i, S_AIQ // 2)),
                  new(S_AIK), new(S_AK), new(S_AV), cache(LANE), cache(LANE), cache(HEAD_DIM)],
        out_block=(rows, BRANCH_WIDTH), out_index=lambda i: (rb + i, 0),
        scratch=[pltpu.VMEM((rows, width), jnp.int32), pltpu.VMEM((A_HEADS, rows, LANE), BF16)],
        sem=("arbitrary",), name="dsa_sample",
        args=(z, z, z, zb, zb, _feature_major(cache_k, LANE), _feature_major(cache_v, LANE),
              _feature_major(cache_ik, HEAD_DIM)))


def _band_prompt_kernel(q_ref, k_ref, v_ref, bias0_ref, bias1_ref, o_ref, s_ref, p_ref, *, tq, win):
    i = pl.program_id(1)
    units = [(r, e) for r in range(2) for e in range(2)]
    biases = (bias0_ref, bias1_ref)
    w0s, oks = [], []
    for r in range(2):
        q0 = (2 * i + r) * tq
        w0 = pl.multiple_of(jnp.maximum(q0 - B_WINDOW, 0), tq)
        kc = (w0 + lax.broadcasted_iota(jnp.int32, (tq, win), 1)) >> 6
        qc = (q0 + lax.broadcasted_iota(jnp.int32, (tq, win), 0)) >> 6
        w0s.append(w0)
        oks.append((kc <= qc) & (kc >= qc - B_PAST_CHUNKS))
    for u, (r, e) in enumerate(units):
        qs = q_ref[r * tq:(r + 1) * tq, :] * (HEAD_DIM ** -0.5)
        qe = jnp.where(_half_mask(qs.shape, e), qs, 0.0).astype(BF16)
        s_ref[u] = _dot_nt(qe, k_ref[pl.ds(w0s[r], win), :])
    for u, (r, e) in enumerate(units):
        s = jnp.where(oks[r], s_ref[u] + biases[r][e], -jnp.inf)
        p_ref[u] = jnp.exp(s - jnp.max(s, axis=-1, keepdims=True)).astype(BF16)
    outs = []
    for u, (r, e) in enumerate(units):
        vw = v_ref[pl.ds(w0s[r], win), :]
        acc = jnp.dot(p_ref[u], jnp.concatenate([vw, jnp.ones_like(vw)], axis=1), preferred_element_type=F32)
        outs.append(acc[:, :LANE] / acc[:, LANE:])
    for r in range(2):
        o_ref[r * tq:(r + 1) * tq, :] = jnp.where(_half_mask(outs[0].shape, 0), outs[2 * r],
                                                   outs[2 * r + 1]).astype(o_ref.dtype)


def _band_prompt(z, zb, bias, t, tq, o_all, m):
    win = B_WINDOW + tq
    n_case = B_WINDOW // tq
    assert t % (2 * tq) == 0
    case = lambda r: pl.BlockSpec((None, 2, tq, win), lambda j, i: (jnp.minimum(2 * i + r, n_case), j, 0, 0))
    return _mixer_call(
        functools.partial(_band_prompt_kernel, tq=tq, win=win), o_all, m,
        grid=(B_HEADS // 2, t // (2 * tq)),
        in_specs=[pl.BlockSpec((2 * tq, LANE), lambda j, i: (i, S_BQ + j)),
                  pl.BlockSpec((t, LANE), lambda j, i: (0, S_BK + j)),
                  pl.BlockSpec((t, LANE), lambda j, i: (0, S_BV + j)),
                  case(0), case(1)],
        out_block=(2 * tq, LANE), out_index=lambda j, i: (i, BRANCH_WIDTH // LANE + j),
        scratch=[pltpu.VMEM((4, tq, win), F32), pltpu.VMEM((4, tq, win), BF16)],
        sem=("parallel", "arbitrary"), name="band_prompt", args=(z, zb, zb, bias, bias))


def _band_sample_kernel(z_ref, zb_ref, kct_ref, vct_ref, bc_ref, bn_ref, o_ref):
    for j in range(B_HEADS // 2):
        slot = lambda ref, s: ref[:, (s + j) * LANE:(s + j + 1) * LANE]
        feat = slice(j * LANE, (j + 1) * LANE)
        kct, vct = kct_ref[feat, :].astype(BF16), vct_ref[feat, :].astype(BF16)
        kn, vn = slot(zb_ref, S_BK), slot(zb_ref, S_BV)
        qs = slot(z_ref, S_BQ) * (HEAD_DIM ** -0.5)
        outs = []
        for e in range(2):
            qe = jnp.where(_half_mask(qs.shape, e), qs, 0.0).astype(BF16)
            s_c = jnp.dot(qe, kct, preferred_element_type=F32) + bc_ref[2 * j + e]
            s_n = _dot_nt(qe, kn) + bn_ref[2 * j + e]
            m = jnp.maximum(jnp.max(s_c, axis=-1, keepdims=True), jnp.max(s_n, axis=-1, keepdims=True))
            p_c, p_n = jnp.exp(s_c - m), jnp.exp(s_n - m)
            l = jnp.sum(p_c, axis=-1, keepdims=True) + jnp.sum(p_n, axis=-1, keepdims=True)
            o = _dot_nt(p_c.astype(BF16), vct) + jnp.dot(p_n.astype(BF16), vn, preferred_element_type=F32)
            outs.append(o / l)
        o_ref[:, feat] = jnp.where(_half_mask(outs[0].shape, 0), outs[0], outs[1]).astype(o_ref.dtype)


def _band_sample(z, zb, cache_k, cache_v, bias_c, bias_n, l, t, nb, ts, o_all, m):
    pb = cache_k.shape[2]
    rb = t // ts
    bw = BRANCH_WIDTH
    cache = pl.BlockSpec((None, None, bw, pb), lambda b: (l, b, 0, 0))
    full = lambda a: pl.BlockSpec(a.shape, lambda b: (0,) * a.ndim)
    return _mixer_call(
        _band_sample_kernel, o_all, m,
        grid=(nb,),
        in_specs=[pl.BlockSpec((ts, z.shape[1]), lambda b: (rb + b, 0)),
                  pl.BlockSpec((ts, z.shape[1]), lambda b: (rb + b, 0)),
                  cache, cache, full(bias_c), full(bias_n)],
        out_block=(ts, bw), out_index=lambda b: (rb + b, 1),
        sem=("parallel",), name="band_sample",
        args=(z, zb, _feature_major(cache_k, bw), _feature_major(cache_v, bw), bias_c, bias_n))


def _band_bias(table, tq):
    n_case = B_WINDOW // tq
    win = B_WINDOW + tq
    width = win + B_WINDOW
    j = np.arange(width + tq - 1) - (tq - 1) - B_WINDOW
    ext = table.astype(F32)[:, np.clip(j, -B_REL_CLIP, B_REL_CLIP) + B_REL_CLIP]
    n = ext.shape[1]
    x = jnp.roll(ext, -(tq - 1), axis=1)
    toe = jnp.tile(x, (1, tq))[:, :tq * (n - 1)].reshape(ext.shape[0], tq, n - 1)[:, :, :width]
    return jnp.stack([toe[:, :, B_WINDOW - c * tq:B_WINDOW - c * tq + win] for c in range(n_case + 1)], axis=0)


def _lambda(lam_ref, lam_init):
    lv = lam_ref[...]
    return (jnp.exp(jnp.sum(lv[0:1] * lv[1:2], axis=-1, keepdims=True))
            - jnp.exp(jnp.sum(lv[2:3] * lv[3:4], axis=-1, keepdims=True)) + lam_init)


def _diff_finish(o0, o1, lam, on_ref, lam_init):
    attn = o0 - lam * o1
    ms = jnp.mean(attn * attn, axis=-1, keepdims=True)
    return (attn * lax.rsqrt(ms + NORM_EPS) * on_ref[...]) * (1.0 - lam_init)


def _diff_prompt_kernel(q_ref, k_ref, v_ref, lam_ref, on_ref, o_ref, qh_ref, m_ref, acc_ref, s_ref, p_ref,
                        *, tq, kb, tg, lam_init):
    i = pl.program_id(1)
    q0 = i * tq
    n_grp = tq // tg
    n_blk = (q0 + tq + kb - 1) // kb
    row = lax.broadcasted_iota(jnp.int32, (tg, kb), 0)
    limits = [(((q0 + g * tg + row) >> 6) + 1) << 6 for g in range(n_grp)]
    for g in range(n_grp):
        qs = q_ref[g * tg:(g + 1) * tg, :] * (HEAD_DIM ** -0.5)
        for c in range(2):
            qh_ref[2 * g + c] = jnp.where(_half_mask(qs.shape, c), qs, 0.0).astype(BF16)
    _softmax_init(m_ref, acc_ref)
    n_full = (q0 + CHUNK) // kb

    def attn_blk(b, carry, masked):
        start = pl.multiple_of(b * kb, kb)
        ok = None
        if masked:
            kpos = start + lax.broadcasted_iota(jnp.int32, (tg, kb), 1)
            ok = [kpos < limits[u // 2] for u in range(2 * n_grp)]
        _softmax_block(ok, k_ref[pl.ds(start, kb), :], v_ref[pl.ds(start, kb), :],
                       [qh_ref[u] for u in range(2 * n_grp)], m_ref, acc_ref, s_ref, p_ref)
        return carry

    lax.fori_loop(0, n_full, functools.partial(attn_blk, masked=False), 0)
    lax.fori_loop(n_full, n_blk, functools.partial(attn_blk, masked=True), 0)
    lam = _lambda(lam_ref, lam_init)
    for g in range(n_grp):
        outs = [_softmax_result(acc_ref, 2 * g + c) for c in range(2)]
        o_ref[g * tg:(g + 1) * tg, :] = _diff_finish(outs[0], outs[1], lam, on_ref, lam_init).astype(o_ref.dtype)


def _diff_prompt(z, zb, c_lam_all, c_on_all, l, t, lam_init, o_all, m):
    tq, kb, tg = _tile(t, 512), 512, 128
    maps = lambda dt, w=LANE: pltpu.VMEM((2 * tq // tg, tg, w), dt)
    return _mixer_call(
        functools.partial(_diff_prompt_kernel, tq=tq, kb=kb, tg=tg, lam_init=lam_init), o_all, m,
        grid=(C_HEADS, t // tq),
        in_specs=[pl.BlockSpec((tq, LANE), lambda h, i: (i, S_CQ + h)),
                  pl.BlockSpec((t, LANE), lambda h, i: (0, S_CK + h)),
                  pl.BlockSpec((t, LANE), lambda h, i: (0, S_CV + h)),
                  pl.BlockSpec((None, 4, HEAD_DIM), lambda h, i: (l, 0, 0)),
                  _layer_vec(l, LANE)],
        out_block=(tq, LANE), out_index=lambda h, i: (i, 2 * BRANCH_WIDTH // LANE + h),
        scratch=[maps(BF16), maps(F32), maps(F32, 2 * LANE), maps(F32, kb), maps(BF16, kb)],
        sem=("parallel", "arbitrary"), name="diff_prompt",
        args=(z, zb, zb, c_lam_all, _vec3(c_on_all)))


def _diff_sample_kernel(z_ref, zb_ref, kct_ref, vc_ref, lam_ref, on_ref, o_ref, *, past, lam_init):
    lam = _lambda(lam_ref, lam_init)
    for h in range(C_HEADS):
        slot = lambda ref, s: ref[:, (s + h) * LANE:(s + h + 1) * LANE]
        feat = slice(h * LANE, (h + 1) * LANE)
        kct = kct_ref[feat, :].astype(BF16)
        vc = vc_ref[pl.ds(h, past, stride=C_HEADS), :].astype(BF16)
        kn, vn = slot(zb_ref, S_CK), slot(zb_ref, S_CV)
        qs = slot(z_ref, S_CQ) * (HEAD_DIM ** -0.5)
        outs = []
        for c in range(2):
            qc = jnp.where(_half_mask(qs.shape, c), qs, 0.0).astype(BF16)
            s_c = jnp.dot(qc, kct, preferred_element_type=F32)
            s_n = _dot_nt(qc, kn)
            m = jnp.maximum(jnp.max(s_c, axis=-1, keepdims=True), jnp.max(s_n, axis=-1, keepdims=True))
            p_c, p_n = jnp.exp(s_c - m), jnp.exp(s_n - m)
            l = jnp.sum(p_c, axis=-1, keepdims=True) + jnp.sum(p_n, axis=-1, keepdims=True)
            o = (jnp.dot(p_c.astype(BF16), vc, preferred_element_type=F32)
                 + jnp.dot(p_n.astype(BF16), vn, preferred_element_type=F32))
            outs.append(o / l)
        o_ref[:, feat] = _diff_finish(outs[0], outs[1], lam, on_ref, lam_init).astype(o_ref.dtype)


def _diff_sample(z, zb, cache_k, cache_v, c_lam_all, c_on_all, l, t, nb, ts, lam_init, o_all, m):
    depth, _, past = cache_k.shape[:3]
    rb = t // ts
    bw = BRANCH_WIDTH
    return _mixer_call(
        functools.partial(_diff_sample_kernel, past=past, lam_init=lam_init), o_all, m,
        grid=(nb,),
        in_specs=[pl.BlockSpec((ts, z.shape[1]), lambda b: (rb + b, 0)),
                  pl.BlockSpec((ts, z.shape[1]), lambda b: (rb + b, 0)),
                  pl.BlockSpec((None, None, bw, past), lambda b: (l, b, 0, 0)),
                  pl.BlockSpec((None, None, past * C_HEADS, LANE), lambda b: (l, b, 0, 0)),
                  pl.BlockSpec((None, 4, HEAD_DIM), lambda b: (l, 0, 0)),
                  _layer_vec(l, LANE)],
        out_block=(ts, bw), out_index=lambda b: (rb + b, 2),
        sem=("parallel",), name="diff_sample",
        args=(z, zb, _feature_major(cache_k, bw), cache_v.reshape(depth, nb, past * C_HEADS, LANE),
              c_lam_all, _vec3(c_on_all)))


def _rwkv_pre_kernel(zd_ref, prev_ref, first_ref, mu_ref, w0_ref, w2_ref, a0_ref, a2_ref, g2_ref, kkw_ref,
                     ka_ref, bd_ref, r_ref, w_ref, k_ref, v_ref, kk_ref, b_ref, g_ref, *, tm, t, ts):
    zf = zd_ref[...]
    grow = pl.program_id(0) * tm + lax.broadcasted_iota(jnp.int32, (tm, 1), 0)
    zs = jnp.where(grow % tm == 0, prev_ref[7:8, :], pltpu.roll(zf, 1, 0))
    seq_start = (grow >= t) & ((grow - t) % ts == 0)
    zs = jnp.where(seq_start, first_ref[...], zs)
    zs = jnp.where(grow == 0, 0.0, zs)
    zm = zf + (zs - zf) * mu_ref[...]
    bw = BRANCH_WIDTH
    r, k, v = zm[:, :bw], zm[:, bw:2 * bw], zm[:, 2 * bw:3 * bw]
    wl, al, gl = zm[:, 3 * bw:3 * bw + 64], zm[:, 3 * bw + 64:3 * bw + 128], zm[:, 3 * bw + 128:]
    dot = lambda a, b: jnp.dot(a.astype(BF16), b.astype(BF16), preferred_element_type=F32)
    u = -(w0_ref[...] + dot(jnp.tanh(wl), w2_ref[...]))
    softplus = jnp.maximum(u, 0.0) + jnp.log(1.0 + jnp.exp(-jnp.abs(u)))
    w = -softplus - 0.5
    a = jax.nn.sigmoid(a0_ref[...] + dot(al, a2_ref[...]))
    kk = k * kkw_ref[...]
    nrm = jnp.sqrt(_seg_sum(kk * kk, bd_ref[...]))
    kk = kk / jnp.maximum(nrm, 1e-12)
    r_ref[...] = r
    w_ref[...] = -jnp.exp(w)
    k_ref[...] = k * (1.0 + (a - 1.0) * ka_ref[...])
    v_ref[...] = v
    kk_ref[...] = kk
    b_ref[...] = kk * a
    g_ref[...] = dot(jax.nn.sigmoid(gl), g2_ref[...])


def _rwkv_pre(zd, zfirst, p, l, t, ts):
    m, dc = zd.shape
    tm = _tile(math.gcd(t, m - t), 512)
    bw = BRANCH_WIDTH
    npt = t // tm
    rows = pl.BlockSpec((tm, dc), lambda i: (i, 0))
    vec = lambda n: _layer_vec(l, n)
    mat = lambda k: pl.BlockSpec((None, k, bw), lambda i: (l, 0, 0))
    out = jax.ShapeDtypeStruct((m, bw), F32)
    return pl.pallas_call(
        functools.partial(_rwkv_pre_kernel, tm=tm, t=t, ts=ts),
        out_shape=[out] * 7,
        grid=(m // tm,),
        in_specs=[rows,
                  pl.BlockSpec((8, dc), lambda i: (jnp.maximum(i * (tm // 8) - 1, 0), 0)),
                  pl.BlockSpec((tm, dc), lambda i: (jnp.maximum(i - npt, 0), 0)),
                  vec(dc), vec(bw), mat(64), vec(bw), mat(64), mat(128), vec(bw), vec(bw),
                  pl.BlockSpec((bw, bw), lambda i: (0, 0))],
        out_specs=[pl.BlockSpec((tm, bw), lambda i: (i, 0))] * 7,
        compiler_params=_cparams("parallel"),
        name="rwkv_pre",
    )(zd, zd, zfirst, _vec3(p["d_mu"]), _vec3(p["d_w0"]), p["d_w2"], _vec3(p["d_a0"]), p["d_a2"], p["d_g2"],
      _vec3(p["d_k_k"]), _vec3(p["d_k_a"]), _ones_blockdiag(bw))


RW_CHUNK_MAX = 64
RW_BLOCK = 128
RW_SLOTS = BRANCH_WIDTH // LANE


def _dot3(a, b):
    ah, al = _split2(a)
    bh, bl = _split2(b)
    d = lambda x, y: jnp.dot(x, y, preferred_element_type=F32)
    return d(ah, bh) + d(al, bh) + d(ah, bl)


def _rwkv_chunk_kernel(*refs, seq_chunks, RW_CHUNK):
    if seq_chunks:
        (r_ref, lw_ref, k_ref, v_ref, kk_ref, b_ref, s0_ref, y_ref, sf_ref,
         h_ref, u_ref, ab_ref, rb_ref, u0_ref, y0_ref, bt_ref, kt_ref, eg_ref) = refs
    else:
        (r_ref, lw_ref, k_ref, v_ref, kk_ref, b_ref, y_ref, sf_ref,
         h_ref, u_ref, ab_ref, rb_ref, u0_ref, y0_ref, bt_ref, kt_ref, eg_ref) = refs
        s0_ref = None

        @pl.when(pl.program_id(0) == 0)
        def _():
            h_ref[...] = jnp.zeros_like(h_ref)

    n = RW_BLOCK
    n_chunks = n // RW_CHUNK
    row = lax.broadcasted_iota(jnp.int32, (n, n), 0)
    col = lax.broadcasted_iota(jnp.int32, (n, n), 1)
    same = (row // RW_CHUNK) == (col // RW_CHUNK)
    strict, incl = same & (col < row), same & (col <= row)
    eye = jnp.where(row == col, 1.0, 0.0)
    head_diag = (row // HEAD_DIM) == (col // HEAD_DIM)
    in_chunk = row % RW_CHUNK
    k_pick = jnp.where(lax.broadcasted_iota(jnp.int32, (n, HEAD_DIM), 0) % HEAD_DIM
                       == lax.broadcasted_iota(jnp.int32, (n, HEAD_DIM), 1), 1.0, 0.0)
    bf = lambda x: x.astype(BF16)
    mm = lambda x, y: jnp.dot(bf(x), bf(y), preferred_element_type=F32)
    u_ref[...] = jnp.zeros_like(u_ref)

    halves = [_half_mask((n, LANE), e) for e in range(2)]
    group = 2
    for j0 in range(0, RW_SLOTS, group):
        slot_data = []
        for j in range(j0, j0 + group):
            sl = slice(j * LANE, (j + 1) * LANE)
            lw, v = lw_ref[:, sl], v_ref[:, sl]
            g = lw
            for d in [1 << s for s in range(RW_CHUNK.bit_length() - 1)]:
                g = g + jnp.where(in_chunk >= d, pltpu.roll(g, d, 0), 0.0)
            inv_g = jnp.exp(-g)
            a_t = -kk_ref[:, sl] * jnp.exp(g - lw)
            b_t, k_t = b_ref[:, sl] * inv_g, k_ref[:, sl] * inv_g
            r_t = r_ref[:, sl] * jnp.exp(g)
            bt_ref[j], kt_ref[j], eg_ref[j] = b_t.T, k_t.T, jnp.exp(g).T
            slot_data.append((a_t, r_t, bf(b_t), bf(k_t), bf(v)))
        chains = [(s, e) for s in range(group) for e in range(2)]
        each = lambda f: [f(s, e, i) for i, (s, e) in enumerate(chains)]
        a_e = each(lambda s, e, i: bf(jnp.where(halves[e], slot_data[s][0], 0.0)))
        r_e = each(lambda s, e, i: bf(jnp.where(halves[e], slot_data[s][1], 0.0)))
        n_ab = each(lambda s, e, i: jnp.where(strict, _dot_nt(a_e[i], slot_data[s][2]), 0.0))
        n_ak = each(lambda s, e, i: jnp.where(strict, _dot_nt(a_e[i], slot_data[s][3]), 0.0))
        m_rb = each(lambda s, e, i: bf(jnp.where(incl, _dot_nt(r_e[i], slot_data[s][2]), 0.0)))
        m_rk = each(lambda s, e, i: bf(jnp.where(incl, _dot_nt(r_e[i], slot_data[s][3]), 0.0)))
        w_e = each(lambda s, e, i: mm(n_ak[i], slot_data[s][4]))
        tinv = each(lambda s, e, i: eye + n_ab[i])
        pw = n_ab
        for _ in range(RW_CHUNK.bit_length() - 2):
            pw = each(lambda s, e, i: mm(pw[i], pw[i]))
            tinv = each(lambda s, e, i: tinv[i] + mm(tinv[i], pw[i]))
        t16 = each(lambda s, e, i: bf(tinv[i]))
        a_bar = each(lambda s, e, i: mm(t16[i], slot_data[s][0]))
        u0 = each(lambda s, e, i: mm(t16[i], w_e[i]))
        r_bar = each(lambda s, e, i: slot_data[s][1] + mm(m_rb[i], a_bar[i]))
        y0 = each(lambda s, e, i: mm(m_rb[i], u0[i]) + mm(m_rk[i], slot_data[s][4]))
        for s in range(group):
            pick = lambda vals: jnp.where(halves[0], vals[2 * s], vals[2 * s + 1])
            j = j0 + s
            ab_ref[j], rb_ref[j], u0_ref[j], y0_ref[j] = pick(a_bar), pick(r_bar), pick(u0), pick(y0)

    for c in range(n_chunks):
        rows = slice(c * RW_CHUNK, (c + 1) * RW_CHUNK)
        col_c = (col // RW_CHUNK) == c
        for j in range(RW_SLOTS):
            sl = slice(j * LANE, (j + 1) * LANE)
            if seq_chunks and c % seq_chunks == 0:
                x = s0_ref[c // seq_chunks, sl, :]
                h = jnp.where(head_diag, _dot_nt3(k_pick, x), 0.0)
            else:
                h = h_ref[j]
            res = _dot3(jnp.concatenate([ab_ref[j, rows, :], rb_ref[j, rows, :]], axis=0), h)
            u_c = res[:RW_CHUNK] + u0_ref[j, rows, :]
            y_ref[rows, sl] = res[RW_CHUNK:] + y0_ref[j, rows, :]
            u_ref[j, rows, :] = u_c
            bk = jnp.concatenate([jnp.where(col_c, bt_ref[j], 0.0), jnp.where(col_c, kt_ref[j], 0.0)], axis=1)
            uv = jnp.concatenate([u_ref[j], v_ref[:, sl]], axis=0)
            inc = jnp.where(head_diag, mm(bk, uv), 0.0)
            g_end = eg_ref[j, :, (c + 1) * RW_CHUNK - 1:(c + 1) * RW_CHUNK]
            h = g_end * (h + inc)
            h_ref[j] = h
            if seq_chunks and (c + 1) % seq_chunks == 0:
                ht = h.T
                sf_ref[c // seq_chunks, sl, :] = (ht + pltpu.roll(ht, HEAD_DIM, 1))[:, :HEAD_DIM]

    if not seq_chunks:
        @pl.when(pl.program_id(0) == pl.num_programs(0) - 1)
        def _():
            for j in range(RW_SLOTS):
                ht = h_ref[j].T
                sf_ref[0, j * LANE:(j + 1) * LANE, :] = (ht + pltpu.roll(ht, HEAD_DIM, 1))[:, :HEAD_DIM]


def _rwkv_chunked(ops, s0, row0, n_seq, t):
    n = RW_BLOCK
    bw = BRANCH_WIDTH
    chunk = math.gcd(t, RW_CHUNK_MAX)
    assert row0 % n == 0 and (n_seq * t) % n == 0 and chunk >= 8 and chunk & (chunk - 1) == 0
    rb = row0 // n
    rows = pl.BlockSpec((n, bw), lambda i: (rb + i, 0))
    if s0 is None:
        assert n_seq == 1
        seq_chunks, per_blk, extra, extra_specs = 0, 1, [], []
        sf_spec = pl.BlockSpec((1, bw, HEAD_DIM), lambda i: (0, 0, 0))
    else:
        assert n % t == 0
        seq_chunks, per_blk = t // chunk, n // t
        extra = [s0.reshape(n_seq, bw, HEAD_DIM)]
        extra_specs = [pl.BlockSpec((per_blk, bw, HEAD_DIM), lambda i: (i, 0, 0))]
        sf_spec = pl.BlockSpec((per_blk, bw, HEAD_DIM), lambda i: (i, 0, 0))
    slot = lambda dt=F32: pltpu.VMEM((RW_SLOTS, n, LANE), dt)
    y, sf = pl.pallas_call(
        functools.partial(_rwkv_chunk_kernel, seq_chunks=seq_chunks, RW_CHUNK=chunk),
        out_shape=[jax.ShapeDtypeStruct((n_seq * t, bw), F32), jax.ShapeDtypeStruct((n_seq, bw, HEAD_DIM), F32)],
        grid=(n_seq * t // n,),
        in_specs=[rows] * 6 + extra_specs,
        out_specs=[pl.BlockSpec((n, bw), lambda i: (i, 0)), sf_spec],
        scratch_shapes=[slot() for _ in range(9)],
        compiler_params=_cparams("arbitrary"),
        name="rwkv_chunked",
    )(*ops, *extra)
    return y, sf.reshape(n_seq, D_HEADS, HEAD_DIM, HEAD_DIM)


def _rwkv_post_kernel(y_ref, r_ref, k_ref, v_ref, g_ref, lnw_ref, lnb_ref, rk_ref, bd_ref, o_ref):
    bd = bd_ref[...]
    y = y_ref[...]
    mean = _seg_sum(y, bd) * (1.0 / HEAD_DIM)
    yc = y - mean
    var = _seg_sum(yc * yc, bd) * (1.0 / HEAD_DIM)
    yn = yc * lax.rsqrt(var + D_GN_EPS) * lnw_ref[...] + lnb_ref[...]
    bonus = _seg_sum(r_ref[...] * k_ref[...] * rk_ref[...], bd) * v_ref[...]
    o_ref[...] = ((yn + bonus) * g_ref[...]).astype(o_ref.dtype)


def _rwkv_post(y, r, k, v, g, p, l, o_all):
    m, bw = y.shape
    tm = _tile(m, 512)
    rows = pl.BlockSpec((tm, bw), lambda i: (i, 0))
    vec = _layer_vec(l, bw)
    return _mixer_call(
        _rwkv_post_kernel, o_all, m,
        grid=(m // tm,),
        in_specs=[rows] * 5 + [vec] * 3 + [pl.BlockSpec((bw, bw), lambda i: (0, 0))],
        out_block=(tm, bw), out_index=lambda i: (i, N_BRANCH - 1),
        sem=("parallel",), name="rwkv_post",
        args=(y, r, k, v, g, _vec3(p["d_ln_w"]), _vec3(p["d_ln_b"]), _vec3(p["d_r_k"]), _ones_blockdiag(bw)))


def _merge_kernel(o_ref, wbr_ref, h_ref, wg_ref, out_ref, acc_ref):
    n = pl.program_id(2)

    @pl.when(n == 0)
    def _():
        acc_ref[...] = jnp.zeros_like(acc_ref)

    u = jnp.dot(o_ref[...], wbr_ref[...].astype(BF16), preferred_element_type=F32)
    gate = jax.nn.sigmoid(jnp.dot(h_ref[...], wg_ref[...], preferred_element_type=F32))
    acc_ref[...] += u * gate

    @pl.when(n == N_BRANCH - 1)
    def _():
        out_ref[...] = acc_ref[...].astype(out_ref.dtype)


def _merge(o_all, w_br_all, h, w_pack, l):
    m, d = h.shape
    tm, tn = _tile(m, 1024), 512
    nj = d // tn
    g0 = S_GATE * LANE // tn
    return pl.pallas_call(
        _merge_kernel,
        out_shape=jax.ShapeDtypeStruct((m, d), BF16),
        grid=(m // tm, nj, N_BRANCH),
        in_specs=[pl.BlockSpec((tm, BRANCH_WIDTH), lambda i, j, n: (i, n)),
                  pl.BlockSpec((None, None, BRANCH_WIDTH, tn), lambda i, j, n: (l, n, 0, j)),
                  pl.BlockSpec((tm, d), lambda i, j, n: (i, 0)),
                  pl.BlockSpec((None, d, tn), lambda i, j, n: (l, 0, g0 + n * nj + j))],
        out_specs=pl.BlockSpec((tm, tn), lambda i, j, n: (i, j)),
        scratch_shapes=[pltpu.VMEM((tm, tn), F32)],
        compiler_params=_cparams("parallel", "parallel", "arbitrary"),
        name="branch_merge",
    )(o_all, w_br_all, h, w_pack)


def _ple_kernel(x_ref, pe_ref, wple_ref, h_ref, wg_ref, o_ref):
    emb = jnp.dot(pe_ref[...].astype(BF16), wple_ref[...].astype(BF16), preferred_element_type=F32)
    gate = jax.nn.sigmoid(jnp.dot(h_ref[...], wg_ref[...].astype(BF16), preferred_element_type=F32))
    o_ref[...] = x_ref[...] + emb * gate


def _ple(x, pe, w_ple_all, h, w_gate_all, l):
    m, d = x.shape
    pd = pe.shape[1]
    tm, tn = _tile(m, 1024), 512
    return pl.pallas_call(
        _ple_kernel,
        out_shape=jax.ShapeDtypeStruct((m, d), F32),
        grid=(m // tm, d // tn),
        in_specs=[pl.BlockSpec((tm, tn), lambda i, j: (i, j)),
                  pl.BlockSpec((tm, pd), lambda i, j: (i, 0)),
                  pl.BlockSpec((None, pd, tn), lambda i, j: (l, 0, j)),
                  pl.BlockSpec((tm, d), lambda i, j: (i, 0)),
                  pl.BlockSpec((None, d, tn), lambda i, j: (l, 0, j))],
        out_specs=pl.BlockSpec((tm, tn), lambda i, j: (i, j)),
        compiler_params=_cparams("parallel", "parallel"),
        name="ple",
    )(x, pe, w_ple_all, h, w_gate_all)


def _layer(l, x, pe, p, t, nb, ts, tabs):
    m = x.shape[0]
    lam_init = 0.8 - 0.6 * math.exp(-0.3 * l)
    dc = p["d_mu"].shape[1]
    d = x.shape[1]
    lw = {"a_qn": p["a_q_norm"][l], "a_kn": p["a_k_norm"][l], "b_qn": p["b_q_norm"][l], "b_kn": p["b_k_norm"][l],
          "c_qn": p["c_q_norm"][l], "c_kn": p["c_k_norm"][l]}

    h = _rmsnorm(x, p["norm1_g"], l)
    w_pack = p["w_pack"]
    z, zb = _qkv_projection(h, w_pack, l, *_column_vectors(lw), *tabs)
    zd = _matmul(h, w_pack, lambda acc: acc, F32, n=dc, layer=l, col0=S_D * LANE, tn=256, name="rwkv_projection")

    o_all = _dsa_prompt(z, zb, t, None, m)
    o_all = _dsa_sample(z, zb, p["cache_a_k"], p["cache_a_v"], p["cache_a_kidx"], l, t, nb, ts, o_all, m)
    tq_b = 128
    bias = _band_bias(p["b_rel_bias"][l], tq_b)
    pb = p["cache_b_k"].shape[2]
    generic = bias[B_WINDOW // tq_b]
    o_all = _band_prompt(z, zb, bias, t, tq_b, o_all, m)
    o_all = _band_sample(z, zb, p["cache_b_k"], p["cache_b_v"], generic[:, :ts, B_WINDOW - pb:B_WINDOW],
                         generic[:, :ts, B_WINDOW:B_WINDOW + ts], l, t, nb, ts, o_all, m)
    c_on = p["c_out_norm"]
    o_all = _diff_prompt(z, zb, p["c_lambda"], c_on, l, t, lam_init, o_all, m)
    o_all = _diff_sample(z, zb, p["cache_c_k"], p["cache_c_v"], p["c_lambda"], c_on, l, t, nb, ts, lam_init,
                         o_all, m)

    zfirst = jnp.broadcast_to(p["state_d_shift"][l], (nb, ts, dc)).reshape(nb * ts, dc)
    r, w, k, v, kk, b, g = _rwkv_pre(zd, zfirst, p, l, t, ts)
    ops = (r, w, k, v, kk, b)
    y_p, wkv_p = _rwkv_chunked(ops, None, 0, 1, t)
    y_s, wkv_s = _rwkv_chunked(ops, p["state_d_wkv"][l], t, nb, ts)
    o_all = _rwkv_post(jnp.concatenate([y_p, y_s], axis=0), r, k, v, g, p, l, o_all)

    ug = _merge(o_all, p["w_branch"], h, w_pack, l)
    res = lambda acc, r_: r_ + acc
    x = _matmul(ug, p["w_out"], res, F32, n=d, layer=l, residual=x, name="out_proj")
    h2 = _rmsnorm(x, p["norm2_g"], l)
    ffn = p["w_up"].shape[2]
    up = _matmul(h2, p["w_up"], lambda acc: jnp.square(jnp.maximum(acc, 0.0)), BF16, n=ffn, layer=l, name="mlp_up")
    x = _matmul(up, _cast_bf16(p["w_down"], l), res, F32, n=d, residual=x, tn=1024, name="mlp_down")
    h3 = _rmsnorm(x, p["norm3_g"], l)
    x = _ple(x, pe, p["w_ple"], h3, p["w_ple_gate"], l)

    slot = lambda rows, s, n=1: z[rows, s * LANE:(s + n) * LANE]

    def rows_of(sl, lead):
        a = lambda s, n, shape: slot(sl, s, n).reshape(lead + shape)
        ak = a(S_AK, 1, (A_KV_HEADS, HEAD_DIM))
        av = a(S_AV, 1, (A_KV_HEADS, HEAD_DIM))
        aik = slot(sl, S_AIK)[:, :HEAD_DIM].reshape(lead + (HEAD_DIM,))
        bk = a(S_BK, 4, (B_HEADS, HEAD_DIM))
        bv = a(S_BV, 4, (B_HEADS, HEAD_DIM))
        ck = a(S_CK, 4, (C_HEADS, 2, HEAD_DIM))
        cv = a(S_CV, 4, (C_HEADS, 2 * HEAD_DIM))
        return ak, av, aik, bk, bv, ck, cv

    keep = min(B_WINDOW, t)
    pak, pav, paik, pbk, pbv, pck, pcv = rows_of(slice(0, t), (1, t))
    new_p = (pak, pav, paik, pbk[:, t - keep:], pbv[:, t - keep:], pck, pcv, wkv_p, zd[t - 1:t].reshape(1, 1, dc))
    new_s = rows_of(slice(t, m), (nb, ts)) + (wkv_s, zd[t:].reshape(nb, ts, dc)[:, -1:])
    return x, new_p, new_s


def kernel(x_prompt, x_sample, cache_a_k, cache_a_v, cache_a_kidx, cache_b_k, cache_b_v, cache_c_k, cache_c_v, state_d_wkv, state_d_shift, p_prompt, p_sample, norm1_g, w_in, a_q_norm, a_k_norm, b_q_norm, b_k_norm, b_rel_bias, c_q_norm, c_k_norm, c_lambda, c_out_norm, d_mu, d_w0, d_w2, d_a0, d_a2, d_g2, d_k_k, d_k_a, d_r_k, d_ln_w, d_ln_b, w_branch, w_out, norm2_g, w_up, w_down, norm3_g, w_ple, w_ple_gate):
    batch, t, d = x_prompt.shape
    nb, ts, _ = x_sample.shape
    past = cache_a_k.shape[2]
    depth = w_in.shape[0]
    assert batch == 1 and t % 512 == 0 and past % CHUNK == 0 and ts <= CHUNK and (nb * ts) % 8 == 0
    p = dict(cache_a_k=cache_a_k, cache_a_v=cache_a_v, cache_a_kidx=cache_a_kidx, cache_b_k=cache_b_k,
             cache_b_v=cache_b_v, cache_c_k=cache_c_k, cache_c_v=cache_c_v, state_d_wkv=state_d_wkv,
             state_d_shift=state_d_shift, norm1_g=norm1_g, w_in=w_in, a_q_norm=a_q_norm, a_k_norm=a_k_norm,
             b_q_norm=b_q_norm, b_k_norm=b_k_norm, b_rel_bias=b_rel_bias, c_q_norm=c_q_norm, c_k_norm=c_k_norm,
             c_lambda=c_lambda, c_out_norm=c_out_norm, d_mu=d_mu, d_w0=d_w0, d_w2=d_w2, d_a0=d_a0, d_a2=d_a2,
             d_g2=d_g2, d_k_k=d_k_k, d_k_a=d_k_a, d_r_k=d_r_k, d_ln_w=d_ln_w, d_ln_b=d_ln_b, w_branch=w_branch,
             w_out=w_out, norm2_g=norm2_g, w_up=w_up, w_down=w_down, norm3_g=norm3_g, w_ple=w_ple,
             w_ple_gate=w_ple_gate)
    p["w_pack"] = _pack_w_in(w_in)
    x = jnp.concatenate([x_prompt[0], x_sample.reshape(nb * ts, d)], axis=0)
    pos = jnp.concatenate([jnp.arange(t, dtype=jnp.int32),
                           jnp.tile(past + jnp.arange(ts, dtype=jnp.int32), nb)])
    tabs = _rope_tables(pos)
    st_p = [[] for _ in range(9)]
    st_s = [[] for _ in range(9)]
    for l in range(depth):
        pe = jnp.concatenate([p_prompt[l, 0], p_sample[l].reshape(nb * ts, -1)], axis=0)
        x, new_p, new_s = _layer(l, x, pe, p, t, nb, ts, tabs)
        for lst, arr in zip(st_p, new_p):
            lst.append(arr)
        for lst, arr in zip(st_s, new_s):
            lst.append(arr)
    outs_p = [jnp.stack(s, axis=0) for s in st_p]
    outs_s = [jnp.stack(s, axis=0) for s in st_s]
    return (x[:t].reshape(1, t, d), x[t:].reshape(nb, ts, d), *outs_p, *outs_s)
```

```python
import functools
import math

import numpy as np
import jax
import jax.numpy as jnp
from jax import lax
from jax.experimental import pallas as pl
from jax.experimental.pallas import tpu as pltpu

F32 = jnp.float32
BF16 = jnp.bfloat16

CHUNK = 64
HEAD_DIM = 64
ROPE_DIM = 16
ROPE_THETA = 500000.0
N_BRANCH = 4
BRANCH_WIDTH = 512
A_HEADS, A_KV_HEADS, A_IDX_HEADS = 8, 2, 4
A_TOPK_MAX = 256
B_HEADS = 8
B_PAST_CHUNKS = 8
B_WINDOW = B_PAST_CHUNKS * CHUNK
B_REL_CLIP = 128
C_HEADS = 4
D_HEADS = 8
D_GN_EPS = 64e-5
NORM_EPS = 1e-6

LANE = 128
VMEM_LIMIT = 48 * 1024 * 1024

S_AQ, S_AK, S_AV, S_AIQ, S_AIK = 0, 4, 5, 6, 8
S_BQ, S_BK, S_BV = 10, 14, 18
S_CQ, S_CK, S_CV = 22, 26, 30
S_D, S_GATE = 34, 48
N_QKV_SLOTS = 34
N_D_SLOTS = 14
A_COLS = 1092
A_SLOTS = 9
SHIFT = A_COLS - (A_SLOTS - 1) * LANE

NEG_KEY = -2139095041
INT_MIN = -2147483648
M_FLOOR = -1e30


def _cparams(*sem):
    return pltpu.CompilerParams(dimension_semantics=sem, vmem_limit_bytes=VMEM_LIMIT)


def _tile(n, pref):
    t = min(n, pref)
    while n % t:
        t -= 8
    return t


def _split2(x):
    hi = x.astype(BF16)
    lo = (x - hi.astype(F32)).astype(BF16)
    return hi, lo


def _seg_sum(x, ones_bd):
    hi, lo = _split2(x)
    return (jnp.dot(hi, ones_bd, preferred_element_type=F32)
            + jnp.dot(lo, ones_bd, preferred_element_type=F32))


def _dot_nt(a, b):
    return lax.dot_general(a, b, (((1,), (1,)), ((), ())), preferred_element_type=F32)


def _dot_nt3(a, b):
    ah, al = _split2(a)
    bh, bl = _split2(b)
    return _dot_nt(ah, bh) + _dot_nt(ah, bl) + _dot_nt(al, bh)


def _ones_blockdiag(n):
    i = np.arange(n)
    return jnp.asarray((i[:, None] // HEAD_DIM) == (i[None, :] // HEAD_DIM), dtype=BF16)


def _vec3(a):
    return a.reshape(a.shape[0], 1, -1)


def _layer_vec(l, n):
    return pl.BlockSpec((None, 1, n), lambda *_: (l, 0, 0))


def _half_mask(shape, half):
    lane = lax.broadcasted_iota(jnp.int32, shape, len(shape) - 1)
    return (lane < HEAD_DIM) if half == 0 else (lane >= HEAD_DIM)


def _rms_kernel(x_ref, g_ref, o_ref):
    x = x_ref[...]
    ms = jnp.mean(x * x, axis=-1, keepdims=True)
    o_ref[...] = (x * lax.rsqrt(ms + NORM_EPS) * g_ref[...]).astype(o_ref.dtype)


def _rmsnorm(x, g_all, l):
    m, d = x.shape
    tm = _tile(m, 512)
    return pl.pallas_call(
        _rms_kernel,
        out_shape=jax.ShapeDtypeStruct((m, d), BF16),
        grid=(m // tm,),
        in_specs=[pl.BlockSpec((tm, d), lambda i: (i, 0)), _layer_vec(l, d)],
        out_specs=pl.BlockSpec((tm, d), lambda i: (i, 0)),
        compiler_params=_cparams("parallel"),
        name="rmsnorm",
    )(x, _vec3(g_all))


def _pack_kernel(a_ref, b_ref, o_ref):
    j = pl.program_id(0)
    row = lax.broadcasted_iota(jnp.int32, (LANE, a_ref.shape[2]), 0)
    for l in range(a_ref.shape[1]):
        a, b = a_ref[:, l, :], b_ref[:, l, :]
        shifted = jnp.where(row < LANE - SHIFT, pltpu.roll(a, LANE - SHIFT, 0), pltpu.roll(b, LANE - SHIFT, 0))
        out = jnp.where(j < A_SLOTS, a, jnp.where(j == A_SLOTS, 0.0, shifted))
        o_ref[l] = out.T.astype(o_ref.dtype)


def _pack_w_in(w_in_all):
    depth, d, n_in = w_in_all.shape
    n_slots = A_SLOTS + 1 + (n_in - A_COLS) // LANE
    assert (n_in - A_COLS) % LANE == 0 and 0 < SHIFT < LANE
    src_a = lambda j: jnp.where(j < A_SLOTS, j, j - 2)
    src_b = lambda j: jnp.where(j < A_SLOTS, j, j - 1)
    w_t = jnp.transpose(w_in_all, (2, 0, 1))
    return pl.pallas_call(
        _pack_kernel,
        out_shape=jax.ShapeDtypeStruct((depth, d, n_slots * LANE), BF16),
        grid=(n_slots,),
        in_specs=[pl.BlockSpec((LANE, depth, d), lambda j: (src_a(j), 0, 0)),
                  pl.BlockSpec((LANE, depth, d), lambda j: (src_b(j), 0, 0))],
        out_specs=pl.BlockSpec((depth, d, LANE), lambda j: (0, 0, j)),
        compiler_params=_cparams("parallel"),
        name="pack_w_in",
    )(w_t, w_t)


def _mm_kernel(*refs, n_extra, nk, epilogue):
    a_ref, w_ref = refs[0], refs[1]
    extra = refs[2:2 + n_extra]
    o_ref = refs[2 + n_extra]
    if nk == 1:
        acc = jnp.dot(a_ref[...], w_ref[...].astype(BF16), preferred_element_type=F32)
        o_ref[...] = epilogue(acc, *[e[...] for e in extra]).astype(o_ref.dtype)
        return
    acc_ref = refs[3 + n_extra]
    k = pl.program_id(2)

    @pl.when(k == 0)
    def _():
        acc_ref[...] = jnp.zeros_like(acc_ref)

    acc_ref[...] += jnp.dot(a_ref[...], w_ref[...].astype(BF16), preferred_element_type=F32)

    @pl.when(k == nk - 1)
    def _():
        o_ref[...] = epilogue(acc_ref[...], *[e[...] for e in extra]).astype(o_ref.dtype)


def _matmul(a, w, epilogue, out_dtype, *, n, layer=None, col0=0, residual=None,
            tm=1024, tn=512, tk=2048, name="matmul"):
    m, kdim = a.shape
    tm, tn, tk = _tile(m, tm), _tile(n, tn), _tile(kdim, tk)
    nk = kdim // tk
    cb = col0 // tn
    assert col0 % tn == 0
    if layer is None:
        w_spec = pl.BlockSpec((tk, tn), lambda i, j, k: (k, cb + j))
    else:
        w_spec = pl.BlockSpec((None, tk, tn), lambda i, j, k: (layer, k, cb + j))
    specs = [pl.BlockSpec((tm, tk), lambda i, j, k: (i, k)), w_spec]
    extras = []
    if residual is not None:
        specs.append(pl.BlockSpec((tm, tn), lambda i, j, k: (i, j)))
        extras.append(residual)
    return pl.pallas_call(
        functools.partial(_mm_kernel, n_extra=len(extras), nk=nk, epilogue=epilogue),
        out_shape=jax.ShapeDtypeStruct((m, n), out_dtype),
        grid=(m // tm, n // tn, nk),
        in_specs=specs,
        out_specs=pl.BlockSpec((tm, tn), lambda i, j, k: (i, j)),
        scratch_shapes=[pltpu.VMEM((tm, tn), F32)] if nk > 1 else [],
        compiler_params=_cparams("parallel", "parallel", "arbitrary"),
        name=name,
    )(a, w, *extras)


def _cast_kernel(w_ref, o_ref):
    o_ref[...] = w_ref[...].astype(o_ref.dtype)


def _cast_bf16(w_all, l):
    _, kdim, n = w_all.shape
    tk = _tile(kdim, 512)
    return pl.pallas_call(
        _cast_kernel,
        out_shape=jax.ShapeDtypeStruct((kdim, n), BF16),
        grid=(kdim // tk,),
        in_specs=[pl.BlockSpec((None, tk, n), lambda i: (l, i, 0))],
        out_specs=pl.BlockSpec((tk, n), lambda i: (i, 0)),
        compiler_params=_cparams("parallel"),
        name="cast_bf16",
    )(w_all)


def _proj_kernel(h_ref, w_ref, gain_ref, nf_ref, rf_ref, cos_ref, sa_ref, sb_ref, bd_ref, o_ref, ob_ref,
                 xa_ref, xb_ref, *, sub, plain_tiles, norm_tiles):
    bd = bd_ref[...]
    j = pl.program_id(1)

    def matmul(dst_ref):
        dst_ref[...] = jnp.dot(h_ref[...], w_ref[...], preferred_element_type=F32)

    def epilogue(dst_ref, src_ref, with_norm, with_rope):
        matmul(dst_ref)
        normed = nf_ref[...] > 0.5
        gain, rf = gain_ref[...], rf_ref[...]
        for c in range(h_ref.shape[0] // sub):
            rows = slice(c * sub, (c + 1) * sub)
            y = src_ref[rows, :]
            if with_norm:
                ms = _seg_sum(y * y, bd) * (1.0 / HEAD_DIM)
                y = y * jnp.where(normed, lax.rsqrt(ms + NORM_EPS) * gain, 1.0)
            if with_rope:
                cosv, sav, sbv = cos_ref[rows, :], sa_ref[rows, :], sb_ref[rows, :]
            for s in range(2):
                out = y[:, s * LANE:(s + 1) * LANE]
                if with_rope:
                    f = rf[:, s * LANE:(s + 1) * LANE]
                    roped = out * cosv + pltpu.roll(out, 8, 1) * sav + pltpu.roll(out, LANE - 8, 1) * sbv
                    out = jnp.where(f > 0.5, roped, out)
                o_ref[rows, s * LANE:(s + 1) * LANE] = out
                ob_ref[rows, s * LANE:(s + 1) * LANE] = out.astype(BF16)

    any_of = lambda tiles: functools.reduce(jnp.logical_or, [j - 1 == tl for tl in tiles])
    is_first, is_plain, is_norm = j == 0, any_of(plain_tiles), any_of(norm_tiles)
    is_full = jnp.logical_not(is_first | is_plain | is_norm)
    pl.when(is_first)(lambda: matmul(xa_ref))
    for parity, (dst, src) in enumerate(((xa_ref, xb_ref), (xb_ref, xa_ref))):
        here = (j % 2) == parity
        pl.when(here & is_plain)(functools.partial(epilogue, dst, src, False, False))
        pl.when(here & is_norm)(functools.partial(epilogue, dst, src, True, False))
        pl.when(here & is_full)(functools.partial(epilogue, dst, src, True, True))


def _qkv_projection(h, w_pack, l, gain, nf, rf, cos_t, sin_a, sin_b):
    m, d = h.shape
    tm, tn = _tile(m, 1024), 2 * LANE
    zw = N_QKV_SLOTS * LANE
    n_tiles = zw // tn
    prev = lambda j: jnp.maximum(j - 1, 0)
    row = lambda i, j: (0, prev(j))
    tab = lambda i, j: (i, 0)
    no_rope = set(range(S_BQ, S_CQ)) | {S_AV, S_AIK + 1} | set(range(S_CV, N_QKV_SLOTS))
    no_norm = set(range(S_AV, S_BQ)) | set(range(S_BV, S_CQ)) | set(range(S_CV, N_QKV_SLOTS))
    tiles = range(N_QKV_SLOTS // 2)
    plain = tuple(tl for tl in tiles if {2 * tl, 2 * tl + 1} <= (no_rope & no_norm))
    norm_only = tuple(tl for tl in tiles if {2 * tl, 2 * tl + 1} <= no_rope and tl not in plain)
    return pl.pallas_call(
        functools.partial(_proj_kernel, sub=_tile(tm, 256), plain_tiles=plain, norm_tiles=norm_only),
        out_shape=[jax.ShapeDtypeStruct((m, zw), F32), jax.ShapeDtypeStruct((m, zw), BF16)],
        grid=(m // tm, n_tiles + 1),
        in_specs=[pl.BlockSpec((tm, d), lambda i, j: (i, 0)),
                  pl.BlockSpec((None, d, tn), lambda i, j: (l, 0, jnp.minimum(j, n_tiles - 1))),
                  pl.BlockSpec((1, tn), row), pl.BlockSpec((1, tn), row), pl.BlockSpec((1, tn), row),
                  pl.BlockSpec((tm, LANE), tab), pl.BlockSpec((tm, LANE), tab), pl.BlockSpec((tm, LANE), tab),
                  pl.BlockSpec((tn, tn), lambda i, j: (0, 0))],
        out_specs=[pl.BlockSpec((tm, tn), lambda i, j: (i, prev(j)))] * 2,
        scratch_shapes=[pltpu.VMEM((tm, tn), F32), pltpu.VMEM((tm, tn), F32)],
        compiler_params=_cparams("parallel", "arbitrary"),
        name="qkv_projection",
    )(h, w_pack, gain, nf, rf, cos_t, sin_a, sin_b, _ones_blockdiag(tn))


def _column_vectors(lw):
    f = lambda v: jnp.asarray(v, F32).reshape(-1)
    ones = lambda n: jnp.ones((n,), F32)
    zeros = lambda n: jnp.zeros((n,), F32)
    rep = lambda v, n: jnp.tile(f(v), n)
    groups = [
        (rep(lw["a_qn"], 8), 1.0, 1.0), (rep(lw["a_kn"], 2), 1.0, 1.0), (ones(128), 0.0, 0.0),
        (ones(256), 0.0, 1.0), (ones(64), 0.0, 1.0), (ones(64 + LANE), 0.0, 0.0),
        (rep(lw["b_qn"], 8), 1.0, 0.0), (rep(lw["b_kn"], 8), 1.0, 0.0), (ones(512), 0.0, 0.0),
        (rep(lw["c_qn"], 4), 1.0, 1.0), (rep(lw["c_kn"], 4), 1.0, 1.0), (ones(512), 0.0, 0.0)]
    gain = jnp.concatenate([g for g, _, _ in groups]).reshape(1, -1)
    flags = lambda idx: jnp.asarray(
        np.concatenate([np.full((g.shape[0],), grp[idx], np.float32) for grp in groups for g in grp[:1]])[None, :])
    del zeros
    return gain, flags(1), flags(2)


def _rope_tables(pos):
    half = ROPE_DIM // 2
    inv_freq = ROPE_THETA ** (-jnp.arange(0, ROPE_DIM, 2, dtype=F32) / ROPE_DIM)
    ang = pos.astype(F32)[:, None] * inv_freq[None, :]
    cos, sin = jnp.cos(ang), jnp.sin(ang)
    rows = pos.shape[0]
    one = jnp.ones((rows, HEAD_DIM - ROPE_DIM), F32)
    zero = jnp.zeros((rows, HEAD_DIM - ROPE_DIM), F32)
    z8 = jnp.zeros((rows, half), F32)
    cos_h = jnp.concatenate([cos, cos, one], axis=1)
    sa_h = jnp.concatenate([z8, sin, zero], axis=1)
    sb_h = jnp.concatenate([-sin, z8, zero], axis=1)
    dup = lambda t: jnp.concatenate([t, t], axis=1)
    return dup(cos_h), dup(sa_h), dup(sb_h)


def _to_key(score):
    bits = pltpu.bitcast(score + 0.0, jnp.int32)
    return jnp.where(bits < 0, bits ^ 0x7FFFFFFF, bits)


def _indexer_scores(iq, iw, ik):
    kk = ik[:, :HEAD_DIM]
    sc = None
    for hd in range(A_IDX_HEADS):
        logit = _dot_nt3(iq[:, hd * HEAD_DIM:(hd + 1) * HEAD_DIM], kk)
        wgt = iw[:, HEAD_DIM + hd:HEAD_DIM + hd + 1] * (A_IDX_HEADS ** -0.5 * HEAD_DIM ** -0.5)
        term = jnp.maximum(logit, 0.0) * wgt
        sc = term if sc is None else sc + term
    return sc


def _lane_fold(x, op):
    out = x[:, :LANE]
    for s in range(1, x.shape[1] // LANE):
        out = op(out, x[:, s * LANE:(s + 1) * LANE])
    return out


def _count(key_ref, n_blk, blk, pred):
    rows = key_ref.shape[0]
    grp = min(rows, LANE)
    parts = []
    for r0 in range(0, rows, grp):
        def body(b, acc, r0=r0):
            start = pl.multiple_of(b * blk, blk)
            kb = key_ref[r0:r0 + grp, pl.ds(start, blk)]
            return acc + _lane_fold(jnp.where(pred(kb, start, slice(r0, r0 + grp)), 1.0, 0.0), jnp.add)

        parts.append(lax.fori_loop(0, n_blk, body, jnp.zeros((grp, LANE), F32)))
    acc = parts[0] if len(parts) == 1 else jnp.concatenate(parts, axis=0)
    return jnp.sum(acc, axis=-1, keepdims=True)


def _topk_threshold(key_ref, n_blk, blk, topk):
    rows = key_ref.shape[0]
    kf = float(topk)
    c0 = _count(key_ref, n_blk, blk, lambda kb, st, rs: kb >= 0)
    ans = jnp.where(c0 >= kf, 0, INT_MIN).astype(jnp.int32)

    def bit_step(it, ans):
        cand = ans + jnp.left_shift(jnp.int32(1), 30 - it)
        c = _count(key_ref, n_blk, blk, lambda kb, st, rs: kb >= cand[rs])
        return jnp.where(c >= kf, cand, ans)

    thr = lax.fori_loop(0, 31, bit_step, ans)
    n_gt = _count(key_ref, n_blk, blk, lambda kb, st, rs: kb > thr[rs])
    n_eq = _count(key_ref, n_blk, blk, lambda kb, st, rs: kb == thr[rs])
    need = kf - n_gt
    nbits = int(n_blk * blk).bit_length() if isinstance(n_blk, int) else 14
    cut_all = jnp.full((rows, 1), 1 << nbits, jnp.int32)
    tie_overflow = jnp.max(jnp.where((n_eq > need) & (thr != NEG_KEY), 1.0, 0.0)) > 0.5

    def search_cut():
        def cut_step(it, cut):
            cand = cut + jnp.left_shift(jnp.int32(1), nbits - 1 - it)

            def pred(kb, st, rs):
                idx = st + lax.broadcasted_iota(jnp.int32, kb.shape, 1)
                return (kb == thr[rs]) & (idx < cand[rs])

            c = _count(key_ref, n_blk, blk, pred)
            return jnp.where(c <= need, cand, cut)

        return lax.fori_loop(0, nbits, cut_step, jnp.zeros((rows, 1), jnp.int32))

    cut = lax.cond(tie_overflow, search_cut, lambda: cut_all)
    return thr, cut


def _selected(kb, first_idx, thr, cut):
    idx = first_idx + lax.broadcasted_iota(jnp.int32, kb.shape, 1)
    return ((kb > thr) | ((kb == thr) & (idx < cut))) & (kb > NEG_KEY)


def _pair_heads(o_even, o_odd, group_half):
    lane = lax.broadcasted_iota(jnp.int32, o_even.shape, 1)
    if group_half == 0:
        return jnp.where(lane < HEAD_DIM, o_even, pltpu.roll(o_odd, HEAD_DIM, 1))
    return jnp.where(lane < HEAD_DIM, pltpu.roll(o_even, HEAD_DIM, 1), o_odd)


def _dsa_query_heads(q_ref, qh_ref):
    for hd in range(A_HEADS):
        slot, half = hd // 2, hd % 2
        group = hd // (A_HEADS // A_KV_HEADS)
        qs = q_ref[:, slot * LANE:(slot + 1) * LANE] * (HEAD_DIM ** -0.5)
        if half != group:
            qs = pltpu.roll(qs, HEAD_DIM, 1)
        qh_ref[hd] = jnp.where(_half_mask(qs.shape, group), qs, 0.0).astype(BF16)


def _softmax_block(sel, kblk, vblk, q_heads, m_ref, acc_ref, s_ref, p_ref):
    n = len(q_heads)
    reps = kblk.shape[0] // LANE
    v_ones = jnp.concatenate([vblk, jnp.ones_like(vblk)], axis=1)
    for hd in range(n):
        s_ref[hd] = _dot_nt(q_heads[hd], kblk)
    alphas = []
    for hd in range(n):
        s = s_ref[hd]
        if sel is not None:
            s = jnp.where(sel[hd] if isinstance(sel, (list, tuple)) else sel, s, -jnp.inf)
            s_ref[hd] = s
        blk_max = jnp.max(_lane_fold(s, jnp.maximum), axis=-1, keepdims=True)
        m_prev = m_ref[hd]
        m_new = jnp.maximum(m_prev, blk_max)
        alphas.append(jnp.exp(m_prev - m_new))
        m_ref[hd] = m_new
    for hd in range(n):
        p_ref[hd] = jnp.exp(s_ref[hd] - jnp.tile(m_ref[hd], (1, reps))).astype(BF16)
    for hd in range(n):
        alpha2 = jnp.tile(alphas[hd], (1, 2))
        acc_ref[hd] = alpha2 * acc_ref[hd] + jnp.dot(p_ref[hd], v_ones, preferred_element_type=F32)


def _softmax_init(m_ref, acc_ref):
    m_ref[...] = jnp.full(m_ref.shape, M_FLOOR, F32)
    acc_ref[...] = jnp.zeros(acc_ref.shape, F32)


def _softmax_result(acc_ref, hd):
    acc = acc_ref[hd]
    return acc[:, :LANE] / acc[:, LANE:]


def _indexer_query3(iq_ref, iq3_ref):
    for hd in range(A_IDX_HEADS):
        slot, half = hd // 2, hd % 2
        x = iq_ref[:, slot * LANE:(slot + 1) * LANE]
        xl = jnp.where(_half_mask(x.shape, half), x, pltpu.roll(x, HEAD_DIM, 1))
        hi = xl.astype(BF16).astype(F32)
        lower = _half_mask(x.shape, 0)
        iq3_ref[hd, :, :LANE] = jnp.where(lower, hi, xl - hi).astype(BF16)
        iq3_ref[hd, :, LANE:] = jnp.where(lower, hi, 0.0).astype(BF16)


def _ik3_kernel(x_ref, o_ref):
    x = x_ref[...]
    lower = _half_mask(x.shape, 0)
    xl = jnp.where(lower, x, pltpu.roll(x, HEAD_DIM, 1))
    hi = xl.astype(BF16).astype(F32)
    o_ref[:, :LANE] = hi.astype(BF16)
    o_ref[:, LANE:] = jnp.where(lower, xl - hi, 0.0).astype(BF16)


def _indexer_keys3(z, t):
    tm = _tile(t, 1024)
    return pl.pallas_call(
        _ik3_kernel,
        out_shape=jax.ShapeDtypeStruct((t, 2 * LANE), BF16),
        grid=(t // tm,),
        in_specs=[pl.BlockSpec((tm, LANE), lambda i: (i, S_AIK))],
        out_specs=pl.BlockSpec((tm, 2 * LANE), lambda i: (i, 0)),
        compiler_params=_cparams("parallel"),
        name="indexer_keys",
    )(z)


def _dsa_prompt_kernel(q_ref, iq_ref, iw_ref, k_ref, v_ref, ik3_ref, o_ref,
                       key_ref, qh_ref, iq3_ref, m_ref, acc_ref, s_ref, p_ref, *, tq, kb, topk):
    i = pl.program_id(0)
    q0 = i * tq
    n_blk = (q0 + tq + kb - 1) // kb
    row = lax.broadcasted_iota(jnp.int32, (tq, kb), 0)
    limit = (((q0 + row) >> 6) + 1) << 6
    _indexer_query3(iq_ref, iq3_ref)
    iw = iw_ref[...]
    wgt = [iw[:, HEAD_DIM + hd:HEAD_DIM + hd + 1] * (A_IDX_HEADS ** -0.5 * HEAD_DIM ** -0.5)
           for hd in range(A_IDX_HEADS)]

    def score_blk(b, carry):
        start = pl.multiple_of(b * kb, kb)
        ikb = ik3_ref[pl.ds(start, kb), :]
        for hd in range(A_IDX_HEADS):
            s_ref[hd] = _dot_nt(iq3_ref[hd], ikb)
        sc = None
        for hd in range(A_IDX_HEADS):
            term = jnp.maximum(s_ref[hd], 0.0) * wgt[hd]
            sc = term if sc is None else sc + term
        adm = start + lax.broadcasted_iota(jnp.int32, (tq, kb), 1) < limit
        key_ref[:, pl.ds(start, kb)] = jnp.where(adm, _to_key(sc), NEG_KEY)
        return carry

    lax.fori_loop(0, n_blk, score_blk, 0)
    thr, cut = _topk_threshold(key_ref, n_blk, kb, topk)
    _dsa_query_heads(q_ref, qh_ref)
    _softmax_init(m_ref, acc_ref)

    def attn_blk(b, carry):
        start = pl.multiple_of(b * kb, kb)
        sel = _selected(key_ref[:, pl.ds(start, kb)], start, thr, cut)
        _softmax_block(sel, k_ref[pl.ds(start, kb), :], v_ref[pl.ds(start, kb), :],
                       [qh_ref[hd] for hd in range(A_HEADS)], m_ref, acc_ref, s_ref, p_ref)
        return carry

    lax.fori_loop(0, n_blk, attn_blk, 0)
    for j in range(A_HEADS // 2):
        outs = [_softmax_result(acc_ref, 2 * j + e) for e in range(2)]
        o_ref[:, j * LANE:(j + 1) * LANE] = _pair_heads(outs[0], outs[1], (2 * j) // (A_HEADS // A_KV_HEADS)).astype(o_ref.dtype)


def _mixer_call(kern, o_all, m, *, grid, in_specs, out_block, out_index, args, scratch=(), sem, name):
    n_in = len(args)
    if o_all is None:
        fn, specs, alias, extra = kern, list(in_specs), {}, []
    else:
        fn = lambda *refs: kern(*refs[:n_in], *refs[n_in + 1:])
        specs, alias, extra = list(in_specs) + [pl.BlockSpec(memory_space=pl.ANY)], {n_in: 0}, [o_all]
    return pl.pallas_call(
        fn,
        out_shape=jax.ShapeDtypeStruct((m, N_BRANCH * BRANCH_WIDTH), BF16),
        grid=grid,
        in_specs=specs,
        out_specs=pl.BlockSpec(out_block, out_index),
        scratch_shapes=list(scratch),
        input_output_aliases=alias,
        compiler_params=_cparams(*sem),
        name=name,
    )(*args, *extra)


def _dsa_prompt(z, zb, t, o_all, m):
    tq, kb = 256, 512
    topk = min(A_TOPK_MAX, t // 4)
    full = lambda s: pl.BlockSpec((t, LANE), lambda i: (0, s))
    heads = lambda dt, w=LANE: pltpu.VMEM((A_HEADS, tq, w), dt)
    return _mixer_call(
        functools.partial(_dsa_prompt_kernel, tq=tq, kb=kb, topk=topk), o_all, m,
        grid=(t // tq,),
        in_specs=[pl.BlockSpec((tq, 4 * LANE), lambda i: (i, S_AQ // 4)),
                  pl.BlockSpec((tq, 2 * LANE), lambda i: (i, S_AIQ // 2)),
                  pl.BlockSpec((tq, LANE), lambda i: (i, S_AIK)),
                  full(S_AK), full(S_AV), pl.BlockSpec((t, 2 * LANE), lambda i: (0, 0))],
        out_block=(tq, BRANCH_WIDTH), out_index=lambda i: (i, 0),
        scratch=[pltpu.VMEM((tq, t), jnp.int32), heads(BF16),
                 pltpu.VMEM((A_IDX_HEADS, tq, 2 * LANE), BF16),
                 heads(F32), heads(F32, 2 * LANE), heads(F32, kb), heads(BF16, kb)],
        sem=("arbitrary",), name="dsa_prompt",
        args=(z, z, z, zb, zb, _indexer_keys3(z, t)))


def _feature_major(cache, n_feat):
    nd = cache.ndim
    perm = (0, 1) + tuple(range(3, nd)) + (2,)
    return jnp.transpose(cache, perm).reshape(cache.shape[0], cache.shape[1], n_feat, cache.shape[2])


def _dsa_sample_kernel(q_ref, iq_ref, new_ik_ref, new_k_ref, new_v_ref, ckt_ref, cvt_ref, cikt_ref, o_ref,
                       key_ref, qh_ref, *, ts, n_seq, past, topk, q_pos0):
    rows_all = n_seq * ts
    width = key_ref.shape[1]
    iq_all, iw_all = iq_ref[...], new_ik_ref[...]
    row_c = lax.broadcasted_iota(jnp.int32, (ts, past), 0)
    limit_c = (((q_pos0 + row_c) >> 6) + 1) << 6
    kpos_c = lax.broadcasted_iota(jnp.int32, (ts, past), 1)
    kpos_n = past + lax.broadcasted_iota(jnp.int32, (ts, ts), 1)
    limit_n = (((q_pos0 + lax.broadcasted_iota(jnp.int32, (ts, ts), 0)) >> 6) + 1) << 6
    key_ref[:, past:] = jnp.full((rows_all, width - past), NEG_KEY, jnp.int32)
    for g in range(n_seq):
        rows = slice(g * ts, (g + 1) * ts)
        iq, iw = iq_all[rows], iw_all[rows]
        ikt = cikt_ref[g]
        ik_new = iw[:, :HEAD_DIM]
        sc_c = sc_n = None
        for hd in range(A_IDX_HEADS):
            qh = iq[:, hd * HEAD_DIM:(hd + 1) * HEAD_DIM]
            wgt = iw[:, HEAD_DIM + hd:HEAD_DIM + hd + 1] * (A_IDX_HEADS ** -0.5 * HEAD_DIM ** -0.5)
            t_c = jnp.maximum(_dot3(qh, ikt), 0.0) * wgt
            t_n = jnp.maximum(_dot_nt3(qh, ik_new), 0.0) * wgt
            sc_c = t_c if sc_c is None else sc_c + t_c
            sc_n = t_n if sc_n is None else sc_n + t_n
        key_ref[rows, :past] = jnp.where(kpos_c < limit_c, _to_key(sc_c), NEG_KEY)
        key_ref[rows, past:past + ts] = jnp.where(kpos_n < limit_n, _to_key(sc_n), NEG_KEY)

    thr, cut = _topk_threshold(key_ref, width // LANE, LANE, topk)
    _dsa_query_heads(q_ref, qh_ref)
    for g in range(n_seq):
        rows = slice(g * ts, (g + 1) * ts)
        sel_c = _selected(key_ref[rows, :past], 0, thr[rows], cut[rows])
        sel_n = _selected(key_ref[rows, past:past + ts], past, thr[rows], cut[rows])
        kct, vct = ckt_ref[g].astype(BF16), cvt_ref[g].astype(BF16)
        kn, vn = new_k_ref[rows, :], new_v_ref[rows, :]
        outs = []
        for hd in range(A_HEADS):
            qh = qh_ref[hd, rows, :]
            s_c = jnp.where(sel_c, jnp.dot(qh, kct, preferred_element_type=F32), -jnp.inf)
            s_n = jnp.where(sel_n, _dot_nt(qh, kn), -jnp.inf)
            m = jnp.maximum(jnp.max(s_c, axis=-1, keepdims=True), jnp.max(s_n, axis=-1, keepdims=True))
            p_c, p_n = jnp.exp(s_c - m), jnp.exp(s_n - m)
            l = jnp.sum(p_c, axis=-1, keepdims=True) + jnp.sum(p_n, axis=-1, keepdims=True)
            o = _dot_nt(p_c.astype(BF16), vct) + jnp.dot(p_n.astype(BF16), vn, preferred_element_type=F32)
            outs.append(o / l)
        for j in range(A_HEADS // 2):
            o_ref[rows, j * LANE:(j + 1) * LANE] = _pair_heads(
                outs[2 * j], outs[2 * j + 1], (2 * j) // (A_HEADS // A_KV_HEADS)).astype(o_ref.dtype)


def _dsa_sample(z, zb, cache_k, cache_v, cache_ik, l, t, nb, ts, o_all, m):
    past = cache_k.shape[2]
    topk = min(A_TOPK_MAX, (past + ts) // 4)
    width = ((past + ts + LANE - 1) // LANE) * LANE
    n_seq = LANE // ts
    rows = n_seq * ts
    assert LANE % ts == 0 and nb % n_seq == 0 and t % rows == 0
    rb = t // rows
    new = lambda s: pl.BlockSpec((rows, LANE), lambda i: (rb + i, s))
    cache = lambda w: pl.BlockSpec((None, n_seq, w, past), lambda i: (l, i, 0, 0))
    return _mixer_call(
        functools.partial(_dsa_sample_kernel, ts=ts, n_seq=n_seq, past=past, topk=topk, q_pos0=past), o_all, m,
        grid=(nb // n_seq,),
        in_specs=[pl.BlockSpec((rows, 4 * LANE), lambda i: (rb + i, S_AQ // 4)),
                  pl.BlockSpec((rows, 2 * LANE), lambda i: (rb + i, S_AIQ // 2)),
                  new(S_AIK), new(S_AK), new(S_AV), cache(LANE), cache(LANE), cache(HEAD_DIM)],
        out_block=(rows, BRANCH_WIDTH), out_index=lambda i: (rb + i, 0),
        scratch=[pltpu.VMEM((rows, width), jnp.int32), pltpu.VMEM((A_HEADS, rows, LANE), BF16)],
        sem=("arbitrary",), name="dsa_sample",
        args=(z, z, z, zb, zb, _feature_major(cache_k, LANE), _feature_major(cache_v, LANE),
              _feature_major(cache_ik, HEAD_DIM)))


def _band_prompt_kernel(q_ref, k_ref, v_ref, bias0_ref, bias1_ref, o_ref, s_ref, p_ref, *, tq, win):
    i = pl.program_id(1)
    units = [(r, e) for r in range(2) for e in range(2)]
    biases = (bias0_ref, bias1_ref)
    w0s, oks = [], []
    for r in range(2):
        q0 = (2 * i + r) * tq
        w0 = pl.multiple_of(jnp.maximum(q0 - B_WINDOW, 0), tq)
        kc = (w0 + lax.broadcasted_iota(jnp.int32, (tq, win), 1)) >> 6
        qc = (q0 + lax.broadcasted_iota(jnp.int32, (tq, win), 0)) >> 6
        w0s.append(w0)
        oks.append((kc <= qc) & (kc >= qc - B_PAST_CHUNKS))
    for u, (r, e) in enumerate(units):
        qs = q_ref[r * tq:(r + 1) * tq, :] * (HEAD_DIM ** -0.5)
        qe = jnp.where(_half_mask(qs.shape, e), qs, 0.0).astype(BF16)
        s_ref[u] = _dot_nt(qe, k_ref[pl.ds(w0s[r], win), :])
    for u, (r, e) in enumerate(units):
        s = jnp.where(oks[r], s_ref[u] + biases[r][e], -jnp.inf)
        p_ref[u] = jnp.exp(s - jnp.max(s, axis=-1, keepdims=True)).astype(BF16)
    outs = []
    for u, (r, e) in enumerate(units):
        vw = v_ref[pl.ds(w0s[r], win), :]
        acc = jnp.dot(p_ref[u], jnp.concatenate([vw, jnp.ones_like(vw)], axis=1), preferred_element_type=F32)
        outs.append(acc[:, :LANE] / acc[:, LANE:])
    for r in range(2):
        o_ref[r * tq:(r + 1) * tq, :] = jnp.where(_half_mask(outs[0].shape, 0), outs[2 * r],
                                                   outs[2 * r + 1]).astype(o_ref.dtype)


def _band_prompt(z, zb, bias, t, tq, o_all, m):
    win = B_WINDOW + tq
    n_case = B_WINDOW // tq
    assert t % (2 * tq) == 0
    case = lambda r: pl.BlockSpec((None, 2, tq, win), lambda j, i: (jnp.minimum(2 * i + r, n_case), j, 0, 0))
    return _mixer_call(
        functools.partial(_band_prompt_kernel, tq=tq, win=win), o_all, m,
        grid=(B_HEADS // 2, t // (2 * tq)),
        in_specs=[pl.BlockSpec((2 * tq, LANE), lambda j, i: (i, S_BQ + j)),
                  pl.BlockSpec((t, LANE), lambda j, i: (0, S_BK + j)),
                  pl.BlockSpec((t, LANE), lambda j, i: (0, S_BV + j)),
                  case(0), case(1)],
        out_block=(2 * tq, LANE), out_index=lambda j, i: (i, BRANCH_WIDTH // LANE + j),
        scratch=[pltpu.VMEM((4, tq, win), F32), pltpu.VMEM((4, tq, win), BF16)],
        sem=("parallel", "arbitrary"), name="band_prompt", args=(z, zb, zb, bias, bias))


def _band_sample_kernel(z_ref, zb_ref, kct_ref, vct_ref, bc_ref, bn_ref, o_ref):
    for j in range(B_HEADS // 2):
        slot = lambda ref, s: ref[:, (s + j) * LANE:(s + j + 1) * LANE]
        feat = slice(j * LANE, (j + 1) * LANE)
        kct, vct = kct_ref[feat, :].astype(BF16), vct_ref[feat, :].astype(BF16)
        kn, vn = slot(zb_ref, S_BK), slot(zb_ref, S_BV)
        qs = slot(z_ref, S_BQ) * (HEAD_DIM ** -0.5)
        outs = []
        for e in range(2):
            qe = jnp.where(_half_mask(qs.shape, e), qs, 0.0).astype(BF16)
            s_c = jnp.dot(qe, kct, preferred_element_type=F32) + bc_ref[2 * j + e]
            s_n = _dot_nt(qe, kn) + bn_ref[2 * j + e]
            m = jnp.maximum(jnp.max(s_c, axis=-1, keepdims=True), jnp.max(s_n, axis=-1, keepdims=True))
            p_c, p_n = jnp.exp(s_c - m), jnp.exp(s_n - m)
            l = jnp.sum(p_c, axis=-1, keepdims=True) + jnp.sum(p_n, axis=-1, keepdims=True)
            o = _dot_nt(p_c.astype(BF16), vct) + jnp.dot(p_n.astype(BF16), vn, preferred_element_type=F32)
            outs.append(o / l)
        o_ref[:, feat] = jnp.where(_half_mask(outs[0].shape, 0), outs[0], outs[1]).astype(o_ref.dtype)


def _band_sample(z, zb, cache_k, cache_v, bias_c, bias_n, l, t, nb, ts, o_all, m):
    pb = cache_k.shape[2]
    rb = t // ts
    bw = BRANCH_WIDTH
    cache = pl.BlockSpec((None, None, bw, pb), lambda b: (l, b, 0, 0))
    full = lambda a: pl.BlockSpec(a.shape, lambda b: (0,) * a.ndim)
    return _mixer_call(
        _band_sample_kernel, o_all, m,
        grid=(nb,),
        in_specs=[pl.BlockSpec((ts, z.shape[1]), lambda b: (rb + b, 0)),
                  pl.BlockSpec((ts, z.shape[1]), lambda b: (rb + b, 0)),
                  cache, cache, full(bias_c), full(bias_n)],
        out_block=(ts, bw), out_index=lambda b: (rb + b, 1),
        sem=("parallel",), name="band_sample",
        args=(z, zb, _feature_major(cache_k, bw), _feature_major(cache_v, bw), bias_c, bias_n))


def _band_bias(table, tq):
    n_case = B_WINDOW // tq
    win = B_WINDOW + tq
    width = win + B_WINDOW
    j = np.arange(width + tq - 1) - (tq - 1) - B_WINDOW
    ext = table.astype(F32)[:, np.clip(j, -B_REL_CLIP, B_REL_CLIP) + B_REL_CLIP]
    n = ext.shape[1]
    x = jnp.roll(ext, -(tq - 1), axis=1)
    toe = jnp.tile(x, (1, tq))[:, :tq * (n - 1)].reshape(ext.shape[0], tq, n - 1)[:, :, :width]
    return jnp.stack([toe[:, :, B_WINDOW - c * tq:B_WINDOW - c * tq + win] for c in range(n_case + 1)], axis=0)


def _lambda(lam_ref, lam_init):
    lv = lam_ref[...]
    return (jnp.exp(jnp.sum(lv[0:1] * lv[1:2], axis=-1, keepdims=True))
            - jnp.exp(jnp.sum(lv[2:3] * lv[3:4], axis=-1, keepdims=True)) + lam_init)


def _diff_finish(o0, o1, lam, on_ref, lam_init):
    attn = o0 - lam * o1
    ms = jnp.mean(attn * attn, axis=-1, keepdims=True)
    return (attn * lax.rsqrt(ms + NORM_EPS) * on_ref[...]) * (1.0 - lam_init)


def _diff_prompt_kernel(q_ref, k_ref, v_ref, lam_ref, on_ref, o_ref, qh_ref, m_ref, acc_ref, s_ref, p_ref,
                        *, tq, kb, tg, lam_init):
    i = pl.program_id(1)
    q0 = i * tq
    n_grp = tq // tg
    n_blk = (q0 + tq + kb - 1) // kb
    row = lax.broadcasted_iota(jnp.int32, (tg, kb), 0)
    limits = [(((q0 + g * tg + row) >> 6) + 1) << 6 for g in range(n_grp)]
    for g in range(n_grp):
        qs = q_ref[g * tg:(g + 1) * tg, :] * (HEAD_DIM ** -0.5)
        for c in range(2):
            qh_ref[2 * g + c] = jnp.where(_half_mask(qs.shape, c), qs, 0.0).astype(BF16)
    _softmax_init(m_ref, acc_ref)
    n_full = (q0 + CHUNK) // kb

    def attn_blk(b, carry, masked):
        start = pl.multiple_of(b * kb, kb)
        ok = None
        if masked:
            kpos = start + lax.broadcasted_iota(jnp.int32, (tg, kb), 1)
            ok = [kpos < limits[u // 2] for u in range(2 * n_grp)]
        _softmax_block(ok, k_ref[pl.ds(start, kb), :], v_ref[pl.ds(start, kb), :],
                       [qh_ref[u] for u in range(2 * n_grp)], m_ref, acc_ref, s_ref, p_ref)
        return carry

    lax.fori_loop(0, n_full, functools.partial(attn_blk, masked=False), 0)
    lax.fori_loop(n_full, n_blk, functools.partial(attn_blk, masked=True), 0)
    lam = _lambda(lam_ref, lam_init)
    for g in range(n_grp):
        outs = [_softmax_result(acc_ref, 2 * g + c) for c in range(2)]
        o_ref[g * tg:(g + 1) * tg, :] = _diff_finish(outs[0], outs[1], lam, on_ref, lam_init).astype(o_ref.dtype)


def _diff_prompt(z, zb, c_lam_all, c_on_all, l, t, lam_init, o_all, m):
    tq, kb, tg = _tile(t, 512), 512, 128
    maps = lambda dt, w=LANE: pltpu.VMEM((2 * tq // tg, tg, w), dt)
    return _mixer_call(
        functools.partial(_diff_prompt_kernel, tq=tq, kb=kb, tg=tg, lam_init=lam_init), o_all, m,
        grid=(C_HEADS, t // tq),
        in_specs=[pl.BlockSpec((tq, LANE), lambda h, i: (i, S_CQ + h)),
                  pl.BlockSpec((t, LANE), lambda h, i: (0, S_CK + h)),
                  pl.BlockSpec((t, LANE), lambda h, i: (0, S_CV + h)),
                  pl.BlockSpec((None, 4, HEAD_DIM), lambda h, i: (l, 0, 0)),
                  _layer_vec(l, LANE)],
        out_block=(tq, LANE), out_index=lambda h, i: (i, 2 * BRANCH_WIDTH // LANE + h),
        scratch=[maps(BF16), maps(F32), maps(F32, 2 * LANE), maps(F32, kb), maps(BF16, kb)],
        sem=("parallel", "arbitrary"), name="diff_prompt",
        args=(z, zb, zb, c_lam_all, _vec3(c_on_all)))


def _diff_sample_kernel(z_ref, zb_ref, kct_ref, vc_ref, lam_ref, on_ref, o_ref, *, past, lam_init):
    lam = _lambda(lam_ref, lam_init)
    for h in range(C_HEADS):
        slot = lambda ref, s: ref[:, (s + h) * LANE:(s + h + 1) * LANE]
        feat = slice(h * LANE, (h + 1) * LANE)
        kct = kct_ref[feat, :].astype(BF16)
        vc = vc_ref[pl.ds(h, past, stride=C_HEADS), :].astype(BF16)
        kn, vn = slot(zb_ref, S_CK), slot(zb_ref, S_CV)
        qs = slot(z_ref, S_CQ) * (HEAD_DIM ** -0.5)
        outs = []
        for c in range(2):
            qc = jnp.where(_half_mask(qs.shape, c), qs, 0.0).astype(BF16)
            s_c = jnp.dot(qc, kct, preferred_element_type=F32)
            s_n = _dot_nt(qc, kn)
            m = jnp.maximum(jnp.max(s_c, axis=-1, keepdims=True), jnp.max(s_n, axis=-1, keepdims=True))
            p_c, p_n = jnp.exp(s_c - m), jnp.exp(s_n - m)
            l = jnp.sum(p_c, axis=-1, keepdims=True) + jnp.sum(p_n, axis=-1, keepdims=True)
            o = (jnp.dot(p_c.astype(BF16), vc, preferred_element_type=F32)
                 + jnp.dot(p_n.astype(BF16), vn, preferred_element_type=F32))
            outs.append(o / l)
        o_ref[:, feat] = _diff_finish(outs[0], outs[1], lam, on_ref, lam_init).astype(o_ref.dtype)


def _diff_sample(z, zb, cache_k, cache_v, c_lam_all, c_on_all, l, t, nb, ts, lam_init, o_all, m):
    depth, _, past = cache_k.shape[:3]
    rb = t // ts
    bw = BRANCH_WIDTH
    return _mixer_call(
        functools.partial(_diff_sample_kernel, past=past, lam_init=lam_init), o_all, m,
        grid=(nb,),
        in_specs=[pl.BlockSpec((ts, z.shape[1]), lambda b: (rb + b, 0)),
                  pl.BlockSpec((ts, z.shape[1]), lambda b: (rb + b, 0)),
                  pl.BlockSpec((None, None, bw, past), lambda b: (l, b, 0, 0)),
                  pl.BlockSpec((None, None, past * C_HEADS, LANE), lambda b: (l, b, 0, 0)),
                  pl.BlockSpec((None, 4, HEAD_DIM), lambda b: (l, 0, 0)),
                  _layer_vec(l, LANE)],
        out_block=(ts, bw), out_index=lambda b: (rb + b, 2),
        sem=("parallel",), name="diff_sample",
        args=(z, zb, _feature_major(cache_k, bw), cache_v.reshape(depth, nb, past * C_HEADS, LANE),
              c_lam_all, _vec3(c_on_all)))


def _rwkv_pre_kernel(zd_ref, prev_ref, first_ref, mu_ref, w0_ref, w2_ref, a0_ref, a2_ref, g2_ref, kkw_ref,
                     ka_ref, bd_ref, r_ref, w_ref, k_ref, v_ref, kk_ref, b_ref, g_ref, *, tm, t, ts):
    zf = zd_ref[...]
    grow = pl.program_id(0) * tm + lax.broadcasted_iota(jnp.int32, (tm, 1), 0)
    zs = jnp.where(grow % tm == 0, prev_ref[7:8, :], pltpu.roll(zf, 1, 0))
    seq_start = (grow >= t) & ((grow - t) % ts == 0)
    zs = jnp.where(seq_start, first_ref[...], zs)
    zs = jnp.where(grow == 0, 0.0, zs)
    zm = zf + (zs - zf) * mu_ref[...]
    bw = BRANCH_WIDTH
    r, k, v = zm[:, :bw], zm[:, bw:2 * bw], zm[:, 2 * bw:3 * bw]
    wl, al, gl = zm[:, 3 * bw:3 * bw + 64], zm[:, 3 * bw + 64:3 * bw + 128], zm[:, 3 * bw + 128:]
    dot = lambda a, b: jnp.dot(a.astype(BF16), b.astype(BF16), preferred_element_type=F32)
    u = -(w0_ref[...] + dot(jnp.tanh(wl), w2_ref[...]))
    softplus = jnp.maximum(u, 0.0) + jnp.log(1.0 + jnp.exp(-jnp.abs(u)))
    w = -softplus - 0.5
    a = jax.nn.sigmoid(a0_ref[...] + dot(al, a2_ref[...]))
    kk = k * kkw_ref[...]
    nrm = jnp.sqrt(_seg_sum(kk * kk, bd_ref[...]))
    kk = kk / jnp.maximum(nrm, 1e-12)
    r_ref[...] = r
    w_ref[...] = -jnp.exp(w)
    k_ref[...] = k * (1.0 + (a - 1.0) * ka_ref[...])
    v_ref[...] = v
    kk_ref[...] = kk
    b_ref[...] = kk * a
    g_ref[...] = dot(jax.nn.sigmoid(gl), g2_ref[...])


def _rwkv_pre(zd, zfirst, p, l, t, ts):
    m, dc = zd.shape
    tm = _tile(math.gcd(t, m - t), 512)
    bw = BRANCH_WIDTH
    npt = t // tm
    rows = pl.BlockSpec((tm, dc), lambda i: (i, 0))
    vec = lambda n: _layer_vec(l, n)
    mat = lambda k: pl.BlockSpec((None, k, bw), lambda i: (l, 0, 0))
    out = jax.ShapeDtypeStruct((m, bw), F32)
    return pl.pallas_call(
        functools.partial(_rwkv_pre_kernel, tm=tm, t=t, ts=ts),
        out_shape=[out] * 7,
        grid=(m // tm,),
        in_specs=[rows,
                  pl.BlockSpec((8, dc), lambda i: (jnp.maximum(i * (tm // 8) - 1, 0), 0)),
                  pl.BlockSpec((tm, dc), lambda i: (jnp.maximum(i - npt, 0), 0)),
                  vec(dc), vec(bw), mat(64), vec(bw), mat(64), mat(128), vec(bw), vec(bw),
                  pl.BlockSpec((bw, bw), lambda i: (0, 0))],
        out_specs=[pl.BlockSpec((tm, bw), lambda i: (i, 0))] * 7,
        compiler_params=_cparams("parallel"),
        name="rwkv_pre",
    )(zd, zd, zfirst, _vec3(p["d_mu"]), _vec3(p["d_w0"]), p["d_w2"], _vec3(p["d_a0"]), p["d_a2"], p["d_g2"],
      _vec3(p["d_k_k"]), _vec3(p["d_k_a"]), _ones_blockdiag(bw))


RW_CHUNK_MAX = 64
RW_BLOCK = 128
RW_SLOTS = BRANCH_WIDTH // LANE


def _dot3(a, b):
    ah, al = _split2(a)
    bh, bl = _split2(b)
    d = lambda x, y: jnp.dot(x, y, preferred_element_type=F32)
    return d(ah, bh) + d(al, bh) + d(ah, bl)


def _rwkv_chunk_kernel(*refs, seq_chunks, RW_CHUNK):
    if seq_chunks:
        (r_ref, lw_ref, k_ref, v_ref, kk_ref, b_ref, s0_ref, y_ref, sf_ref,
         h_ref, u_ref, ab_ref, rb_ref, u0_ref, y0_ref, bt_ref, kt_ref, eg_ref) = refs
    else:
        (r_ref, lw_ref, k_ref, v_ref, kk_ref, b_ref, y_ref, sf_ref,
         h_ref, u_ref, ab_ref, rb_ref, u0_ref, y0_ref, bt_ref, kt_ref, eg_ref) = refs
        s0_ref = None

        @pl.when(pl.program_id(0) == 0)
        def _():
            h_ref[...] = jnp.zeros_like(h_ref)

    n = RW_BLOCK
    n_chunks = n // RW_CHUNK
    row = lax.broadcasted_iota(jnp.int32, (n, n), 0)
    col = lax.broadcasted_iota(jnp.int32, (n, n), 1)
    same = (row // RW_CHUNK) == (col // RW_CHUNK)
    strict, incl = same & (col < row), same & (col <= row)
    eye = jnp.where(row == col, 1.0, 0.0)
    head_diag = (row // HEAD_DIM) == (col // HEAD_DIM)
    in_chunk = row % RW_CHUNK
    k_pick = jnp.where(lax.broadcasted_iota(jnp.int32, (n, HEAD_DIM), 0) % HEAD_DIM
                       == lax.broadcasted_iota(jnp.int32, (n, HEAD_DIM), 1), 1.0, 0.0)
    bf = lambda x: x.astype(BF16)
    mm = lambda x, y: jnp.dot(bf(x), bf(y), preferred_element_type=F32)
    u_ref[...] = jnp.zeros_like(u_ref)

    halves = [_half_mask((n, LANE), e) for e in range(2)]
    group = 2
    for j0 in range(0, RW_SLOTS, group):
        slot_data = []
        for j in range(j0, j0 + group):
            sl = slice(j * LANE, (j + 1) * LANE)
            lw, v = lw_ref[:, sl], v_ref[:, sl]
            g = lw
            for d in [1 << s for s in range(RW_CHUNK.bit_length() - 1)]:
                g = g + jnp.where(in_chunk >= d, pltpu.roll(g, d, 0), 0.0)
            inv_g = jnp.exp(-g)
            a_t = -kk_ref[:, sl] * jnp.exp(g - lw)
            b_t, k_t = b_ref[:, sl] * inv_g, k_ref[:, sl] * inv_g
            r_t = r_ref[:, sl] * jnp.exp(g)
            bt_ref[j], kt_ref[j], eg_ref[j] = b_t.T, k_t.T, jnp.exp(g).T
            slot_data.append((a_t, r_t, bf(b_t), bf(k_t), bf(v)))
        chains = [(s, e) for s in range(group) for e in range(2)]
        each = lambda f: [f(s, e, i) for i, (s, e) in enumerate(chains)]
        a_e = each(lambda s, e, i: bf(jnp.where(halves[e], slot_data[s][0], 0.0)))
        r_e = each(lambda s, e, i: bf(jnp.where(halves[e], slot_data[s][1], 0.0)))
        n_ab = each(lambda s, e, i: jnp.where(strict, _dot_nt(a_e[i], slot_data[s][2]), 0.0))
        n_ak = each(lambda s, e, i: jnp.where(strict, _dot_nt(a_e[i], slot_data[s][3]), 0.0))
        m_rb = each(lambda s, e, i: bf(jnp.where(incl, _dot_nt(r_e[i], slot_data[s][2]), 0.0)))
        m_rk = each(lambda s, e, i: bf(jnp.where(incl, _dot_nt(r_e[i], slot_data[s][3]), 0.0)))
        w_e = each(lambda s, e, i: mm(n_ak[i], slot_data[s][4]))
        tinv = each(lambda s, e, i: eye + n_ab[i])
        pw = n_ab
        for _ in range(RW_CHUNK.bit_length() - 2):
            pw = each(lambda s, e, i: mm(pw[i], pw[i]))
            tinv = each(lambda s, e, i: tinv[i] + mm(tinv[i], pw[i]))
        t16 = each(lambda s, e, i: bf(tinv[i]))
        a_bar = each(lambda s, e, i: mm(t16[i], slot_data[s][0]))
        u0 = each(lambda s, e, i: mm(t16[i], w_e[i]))
        r_bar = each(lambda s, e, i: slot_data[s][1] + mm(m_rb[i], a_bar[i]))
        y0 = each(lambda s, e, i: mm(m_rb[i], u0[i]) + mm(m_rk[i], slot_data[s][4]))
        for s in range(group):
            pick = lambda vals: jnp.where(halves[0], vals[2 * s], vals[2 * s + 1])
            j = j0 + s
            ab_ref[j], rb_ref[j], u0_ref[j], y0_ref[j] = pick(a_bar), pick(r_bar), pick(u0), pick(y0)

    for c in range(n_chunks):
        rows = slice(c * RW_CHUNK, (c + 1) * RW_CHUNK)
        col_c = (col // RW_CHUNK) == c
        for j in range(RW_SLOTS):
            sl = slice(j * LANE, (j + 1) * LANE)
            if seq_chunks and c % seq_chunks == 0:
                x = s0_ref[c // seq_chunks, sl, :]
                h = jnp.where(head_diag, _dot_nt3(k_pick, x), 0.0)
            else:
                h = h_ref[j]
            res = _dot3(jnp.concatenate([ab_ref[j, rows, :], rb_ref[j, rows, :]], axis=0), h)
            u_c = res[:RW_CHUNK] + u0_ref[j, rows, :]
            y_ref[rows, sl] = res[RW_CHUNK:] + y0_ref[j, rows, :]
            u_ref[j, rows, :] = u_c
            bk = jnp.concatenate([jnp.where(col_c, bt_ref[j], 0.0), jnp.where(col_c, kt_ref[j], 0.0)], axis=1)
            uv = jnp.concatenate([u_ref[j], v_ref[:, sl]], axis=0)
            inc = jnp.where(head_diag, mm(bk, uv), 0.0)
            g_end = eg_ref[j, :, (c + 1) * RW_CHUNK - 1:(c + 1) * RW_CHUNK]
            h = g_end * (h + inc)
            h_ref[j] = h
            if seq_chunks and (c + 1) % seq_chunks == 0:
                ht = h.T
                sf_ref[c // seq_chunks, sl, :] = (ht + pltpu.roll(ht, HEAD_DIM, 1))[:, :HEAD_DIM]

    if not seq_chunks:
        @pl.when(pl.program_id(0) == pl.num_programs(0) - 1)
        def _():
            for j in range(RW_SLOTS):
                ht = h_ref[j].T
                sf_ref[0, j * LANE:(j + 1) * LANE, :] = (ht + pltpu.roll(ht, HEAD_DIM, 1))[:, :HEAD_DIM]


def _rwkv_chunked(ops, s0, row0, n_seq, t):
    n = RW_BLOCK
    bw = BRANCH_WIDTH
    chunk = math.gcd(t, RW_CHUNK_MAX)
    assert row0 % n == 0 and (n_seq * t) % n == 0 and chunk >= 8 and chunk & (chunk - 1) == 0
    rb = row0 // n
    rows = pl.BlockSpec((n, bw), lambda i: (rb + i, 0))
    if s0 is None:
        assert n_seq == 1
        seq_chunks, per_blk, extra, extra_specs = 0, 1, [], []
        sf_spec = pl.BlockSpec((1, bw, HEAD_DIM), lambda i: (0, 0, 0))
    else:
        assert n % t == 0
        seq_chunks, per_blk = t // chunk, n // t
        extra = [s0.reshape(n_seq, bw, HEAD_DIM)]
        extra_specs = [pl.BlockSpec((per_blk, bw, HEAD_DIM), lambda i: (i, 0, 0))]
        sf_spec = pl.BlockSpec((per_blk, bw, HEAD_DIM), lambda i: (i, 0, 0))
    slot = lambda dt=F32: pltpu.VMEM((RW_SLOTS, n, LANE), dt)
    y, sf = pl.pallas_call(
        functools.partial(_rwkv_chunk_kernel, seq_chunks=seq_chunks, RW_CHUNK=chunk),
        out_shape=[jax.ShapeDtypeStruct((n_seq * t, bw), F32), jax.ShapeDtypeStruct((n_seq, bw, HEAD_DIM), F32)],
        grid=(n_seq * t // n,),
        in_specs=[rows] * 6 + extra_specs,
        out_specs=[pl.BlockSpec((n, bw), lambda i: (i, 0)), sf_spec],
        scratch_shapes=[slot() for _ in range(9)],
        compiler_params=_cparams("arbitrary"),
        name="rwkv_chunked",
    )(*ops, *extra)
    return y, sf.reshape(n_seq, D_HEADS, HEAD_DIM, HEAD_DIM)


def _rwkv_post_kernel(y_ref, r_ref, k_ref, v_ref, g_ref, lnw_ref, lnb_ref, rk_ref, bd_ref, o_ref):
    bd = bd_ref[...]
    y = y_ref[...]
    mean = _seg_sum(y, bd) * (1.0 / HEAD_DIM)
    yc = y - mean
    var = _seg_sum(yc * yc, bd) * (1.0 / HEAD_DIM)
    yn = yc * lax.rsqrt(var + D_GN_EPS) * lnw_ref[...] + lnb_ref[...]
    bonus = _seg_sum(r_ref[...] * k_ref[...] * rk_ref[...], bd) * v_ref[...]
    o_ref[...] = ((yn + bonus) * g_ref[...]).astype(o_ref.dtype)


def _rwkv_post(y, r, k, v, g, p, l, o_all):
    m, bw = y.shape
    tm = _tile(m, 512)
    rows = pl.BlockSpec((tm, bw), lambda i: (i, 0))
    vec = _layer_vec(l, bw)
    return _mixer_call(
        _rwkv_post_kernel, o_all, m,
        grid=(m // tm,),
        in_specs=[rows] * 5 + [vec] * 3 + [pl.BlockSpec((bw, bw), lambda i: (0, 0))],
        out_block=(tm, bw), out_index=lambda i: (i, N_BRANCH - 1),
        sem=("parallel",), name="rwkv_post",
        args=(y, r, k, v, g, _vec3(p["d_ln_w"]), _vec3(p["d_ln_b"]), _vec3(p["d_r_k"]), _ones_blockdiag(bw)))


def _merge_kernel(o_ref, wbr_ref, h_ref, wg_ref, out_ref, acc_ref):
    n = pl.program_id(2)

    @pl.when(n == 0)
    def _():
        acc_ref[...] = jnp.zeros_like(acc_ref)

    u = jnp.dot(o_ref[...], wbr_ref[...].astype(BF16), preferred_element_type=F32)
    gate = jax.nn.sigmoid(jnp.dot(h_ref[...], wg_ref[...], preferred_element_type=F32))
    acc_ref[...] += u * gate

    @pl.when(n == N_BRANCH - 1)
    def _():
        out_ref[...] = acc_ref[...].astype(out_ref.dtype)


def _merge(o_all, w_br_all, h, w_pack, l):
    m, d = h.shape
    tm, tn = _tile(m, 1024), 512
    nj = d // tn
    g0 = S_GATE * LANE // tn
    return pl.pallas_call(
        _merge_kernel,
        out_shape=jax.ShapeDtypeStruct((m, d), BF16),
        grid=(m // tm, nj, N_BRANCH),
        in_specs=[pl.BlockSpec((tm, BRANCH_WIDTH), lambda i, j, n: (i, n)),
                  pl.BlockSpec((None, None, BRANCH_WIDTH, tn), lambda i, j, n: (l, n, 0, j)),
                  pl.BlockSpec((tm, d), lambda i, j, n: (i, 0)),
                  pl.BlockSpec((None, d, tn), lambda i, j, n: (l, 0, g0 + n * nj + j))],
        out_specs=pl.BlockSpec((tm, tn), lambda i, j, n: (i, j)),
        scratch_shapes=[pltpu.VMEM((tm, tn), F32)],
        compiler_params=_cparams("parallel", "parallel", "arbitrary"),
        name="branch_merge",
    )(o_all, w_br_all, h, w_pack)


def _ple_kernel(x_ref, pe_ref, wple_ref, h_ref, wg_ref, o_ref):
    emb = jnp.dot(pe_ref[...].astype(BF16), wple_ref[...].astype(BF16), preferred_element_type=F32)
    gate = jax.nn.sigmoid(jnp.dot(h_ref[...], wg_ref[...].astype(BF16), preferred_element_type=F32))
    o_ref[...] = x_ref[...] + emb * gate


def _ple(x, pe, w_ple_all, h, w_gate_all, l):
    m, d = x.shape
    pd = pe.shape[1]
    tm, tn = _tile(m, 1024), 512
    return pl.pallas_call(
        _ple_kernel,
        out_shape=jax.ShapeDtypeStruct((m, d), F32),
        grid=(m // tm, d // tn),
        in_specs=[pl.BlockSpec((tm, tn), lambda i, j: (i, j)),
                  pl.BlockSpec((tm, pd), lambda i, j: (i, 0)),
                  pl.BlockSpec((None, pd, tn), lambda i, j: (l, 0, j)),
                  pl.BlockSpec((tm, d), lambda i, j: (i, 0)),
                  pl.BlockSpec((None, d, tn), lambda i, j: (l, 0, j))],
        out_specs=pl.BlockSpec((tm, tn), lambda i, j: (i, j)),
        compiler_params=_cparams("parallel", "parallel"),
        name="ple",
    )(x, pe, w_ple_all, h, w_gate_all)


def _layer(l, x, pe, p, t, nb, ts, tabs):
    m = x.shape[0]
    lam_init = 0.8 - 0.6 * math.exp(-0.3 * l)
    dc = p["d_mu"].shape[1]
    d = x.shape[1]
    lw = {"a_qn": p["a_q_norm"][l], "a_kn": p["a_k_norm"][l], "b_qn": p["b_q_norm"][l], "b_kn": p["b_k_norm"][l],
          "c_qn": p["c_q_norm"][l], "c_kn": p["c_k_norm"][l]}

    h = _rmsnorm(x, p["norm1_g"], l)
    w_pack = p["w_pack"]
    z, zb = _qkv_projection(h, w_pack, l, *_column_vectors(lw), *tabs)
    zd = _matmul(h, w_pack, lambda acc: acc, F32, n=dc, layer=l, col0=S_D * LANE, tn=256, name="rwkv_projection")

    o_all = _dsa_prompt(z, zb, t, None, m)
    o_all = _dsa_sample(z, zb, p["cache_a_k"], p["cache_a_v"], p["cache_a_kidx"], l, t, nb, ts, o_all, m)
    tq_b = 128
    bias = _band_bias(p["b_rel_bias"][l], tq_b)
    pb = p["cache_b_k"].shape[2]
    generic = bias[B_WINDOW // tq_b]
    o_all = _band_prompt(z, zb, bias, t, tq_b, o_all, m)
    o_all = _band_sample(z, zb, p["cache_b_k"], p["cache_b_v"], generic[:, :ts, B_WINDOW - pb:B_WINDOW],
                         generic[:, :ts, B_WINDOW:B_WINDOW + ts], l, t, nb, ts, o_all, m)
    c_on = p["c_out_norm"]
    o_all = _diff_prompt(z, zb, p["c_lambda"], c_on, l, t, lam_init, o_all, m)
    o_all = _diff_sample(z, zb, p["cache_c_k"], p["cache_c_v"], p["c_lambda"], c_on, l, t, nb, ts, lam_init,
                         o_all, m)

    zfirst = jnp.broadcast_to(p["state_d_shift"][l], (nb, ts, dc)).reshape(nb * ts, dc)
    r, w, k, v, kk, b, g = _rwkv_pre(zd, zfirst, p, l, t, ts)
    ops = (r, w, k, v, kk, b)
    y_p, wkv_p = _rwkv_chunked(ops, None, 0, 1, t)
    y_s, wkv_s = _rwkv_chunked(ops, p["state_d_wkv"][l], t, nb, ts)
    o_all = _rwkv_post(jnp.concatenate([y_p, y_s], axis=0), r, k, v, g, p, l, o_all)

    ug = _merge(o_all, p["w_branch"], h, w_pack, l)
    res = lambda acc, r_: r_ + acc
    x = _matmul(ug, p["w_out"], res, F32, n=d, layer=l, residual=x, name="out_proj")
    h2 = _rmsnorm(x, p["norm2_g"], l)
    ffn = p["w_up"].shape[2]
    up = _matmul(h2, p["w_up"], lambda acc: jnp.square(jnp.maximum(acc, 0.0)), BF16, n=ffn, layer=l, name="mlp_up")
    x = _matmul(up, _cast_bf16(p["w_down"], l), res, F32, n=d, residual=x, tn=1024, name="mlp_down")
    h3 = _rmsnorm(x, p["norm3_g"], l)
    x = _ple(x, pe, p["w_ple"], h3, p["w_ple_gate"], l)

    slot = lambda rows, s, n=1: z[rows, s * LANE:(s + n) * LANE]

    def rows_of(sl, lead):
        a = lambda s, n, shape: slot(sl, s, n).reshape(lead + shape)
        ak = a(S_AK, 1, (A_KV_HEADS, HEAD_DIM))
        av = a(S_AV, 1, (A_KV_HEADS, HEAD_DIM))
        aik = slot(sl, S_AIK)[:, :HEAD_DIM].reshape(lead + (HEAD_DIM,))
        bk = a(S_BK, 4, (B_HEADS, HEAD_DIM))
        bv = a(S_BV, 4, (B_HEADS, HEAD_DIM))
        ck = a(S_CK, 4, (C_HEADS, 2, HEAD_DIM))
        cv = a(S_CV, 4, (C_HEADS, 2 * HEAD_DIM))
        return ak, av, aik, bk, bv, ck, cv

    keep = min(B_WINDOW, t)
    pak, pav, paik, pbk, pbv, pck, pcv = rows_of(slice(0, t), (1, t))
    new_p = (pak, pav, paik, pbk[:, t - keep:], pbv[:, t - keep:], pck, pcv, wkv_p, zd[t - 1:t].reshape(1, 1, dc))
    new_s = rows_of(slice(t, m), (nb, ts)) + (wkv_s, zd[t:].reshape(nb, ts, dc)[:, -1:])
    return x, new_p, new_s


def kernel(x_prompt, x_sample, cache_a_k, cache_a_v, cache_a_kidx, cache_b_k, cache_b_v, cache_c_k, cache_c_v, state_d_wkv, state_d_shift, p_prompt, p_sample, norm1_g, w_in, a_q_norm, a_k_norm, b_q_norm, b_k_norm, b_rel_bias, c_q_norm, c_k_norm, c_lambda, c_out_norm, d_mu, d_w0, d_w2, d_a0, d_a2, d_g2, d_k_k, d_k_a, d_r_k, d_ln_w, d_ln_b, w_branch, w_out, norm2_g, w_up, w_down, norm3_g, w_ple, w_ple_gate):
    batch, t, d = x_prompt.shape
    nb, ts, _ = x_sample.shape
    past = cache_a_k.shape[2]
    depth = w_in.shape[0]
    assert batch == 1 and t % 512 == 0 and past % CHUNK == 0 and ts <= CHUNK and (nb * ts) % 8 == 0
    p = dict(cache_a_k=cache_a_k, cache_a_v=cache_a_v, cache_a_kidx=cache_a_kidx, cache_b_k=cache_b_k,
             cache_b_v=cache_b_v, cache_c_k=cache_c_k, cache_c_v=cache_c_v, state_d_wkv=state_d_wkv,
             state_d_shift=state_d_shift, norm1_g=norm1_g, w_in=w_in, a_q_norm=a_q_norm, a_k_norm=a_k_norm,
             b_q_norm=b_q_norm, b_k_norm=b_k_norm, b_rel_bias=b_rel_bias, c_q_norm=c_q_norm, c_k_norm=c_k_norm,
             c_lambda=c_lambda, c_out_norm=c_out_norm, d_mu=d_mu, d_w0=d_w0, d_w2=d_w2, d_a0=d_a0, d_a2=d_a2,
             d_g2=d_g2, d_k_k=d_k_k, d_k_a=d_k_a, d_r_k=d_r_k, d_ln_w=d_ln_w, d_ln_b=d_ln_b, w_branch=w_branch,
             w_out=w_out, norm2_g=norm2_g, w_up=w_up, w_down=w_down, norm3_g=norm3_g, w_ple=w_ple,
             w_ple_gate=w_ple_gate)
    p["w_pack"] = _pack_w_in(w_in)
    x = jnp.concatenate([x_prompt[0], x_sample.reshape(nb * ts, d)], axis=0)
    pos = jnp.concatenate([jnp.arange(t, dtype=jnp.int32),
                           jnp.tile(past + jnp.arange(ts, dtype=jnp.int32), nb)])
    tabs = _rope_tables(pos)
    st_p = [[] for _ in range(9)]
    st_s = [[] for _ in range(9)]
    for l in range(depth):
        pe = jnp.concatenate([p_prompt[l, 0], p_sample[l].reshape(nb * ts, -1)], axis=0)
        x, new_p, new_s = _layer(l, x, pe, p, t, nb, ts, tabs)
        for lst, arr in zip(st_p, new_p):
            lst.append(arr)
        for lst, arr in zip(st_s, new_s):
            lst.append(arr)
    outs_p = [jnp.stack(s, axis=0) for s in st_p]
    outs_s = [jnp.stack(s, axis=0) for s in st_s]
    return (x[:t].reshape(1, t, d), x[t:].reshape(nb, ts, d), *outs_p, *outs_s)
```

```python
import functools
import math

import numpy as np
import jax
import jax.numpy as jnp
from jax import lax
from jax.experimental import pallas as pl
from jax.experimental.pallas import tpu as pltpu

F32 = jnp.float32
BF16 = jnp.bfloat16

CHUNK = 64
HEAD_DIM = 64
ROPE_DIM = 16
ROPE_THETA = 500000.0
N_BRANCH = 4
BRANCH_WIDTH = 512
A_HEADS, A_KV_HEADS, A_IDX_HEADS = 8, 2, 4
A_TOPK_MAX = 256
B_HEADS = 8
B_PAST_CHUNKS = 8
B_WINDOW = B_PAST_CHUNKS * CHUNK
B_REL_CLIP = 128
C_HEADS = 4
D_HEADS = 8
D_GN_EPS = 64e-5
NORM_EPS = 1e-6

LANE = 128
VMEM_LIMIT = 48 * 1024 * 1024

S_AQ, S_AK, S_AV, S_AIQ, S_AIK = 0, 4, 5, 6, 8
S_BQ, S_BK, S_BV = 10, 14, 18
S_CQ, S_CK, S_CV = 22, 26, 30
S_D, S_GATE = 34, 48
N_QKV_SLOTS = 34
N_D_SLOTS = 14
A_COLS = 1092
A_SLOTS = 9
SHIFT = A_COLS - (A_SLOTS - 1) * LANE

NEG_KEY = -2139095041
INT_MIN = -2147483648
M_FLOOR = -1e30


def _cparams(*sem):
    return pltpu.CompilerParams(dimension_semantics=sem, vmem_limit_bytes=VMEM_LIMIT)


def _tile(n, pref):
    t = min(n, pref)
    while n % t:
        t -= 8
    return t


def _split2(x):
    hi = x.astype(BF16)
    lo = (x - hi.astype(F32)).astype(BF16)
    return hi, lo


def _seg_sum(x, ones_bd):
    hi, lo = _split2(x)
    return (jnp.dot(hi, ones_bd, preferred_element_type=F32)
            + jnp.dot(lo, ones_bd, preferred_element_type=F32))


def _dot_nt(a, b):
    return lax.dot_general(a, b, (((1,), (1,)), ((), ())), preferred_element_type=F32)


def _dot_nt3(a, b):
    ah, al = _split2(a)
    bh, bl = _split2(b)
    return _dot_nt(ah, bh) + _dot_nt(ah, bl) + _dot_nt(al, bh)


def _ones_blockdiag(n):
    i = np.arange(n)
    return jnp.asarray((i[:, None] // HEAD_DIM) == (i[None, :] // HEAD_DIM), dtype=BF16)


def _vec3(a):
    return a.reshape(a.shape[0], 1, -1)


def _layer_vec(l, n):
    return pl.BlockSpec((None, 1, n), lambda *_: (l, 0, 0))


def _half_mask(shape, half):
    lane = lax.broadcasted_iota(jnp.int32, shape, len(shape) - 1)
    return (lane < HEAD_DIM) if half == 0 else (lane >= HEAD_DIM)


def _rms_kernel(x_ref, g_ref, o_ref):
    x = x_ref[...]
    ms = jnp.mean(x * x, axis=-1, keepdims=True)
    o_ref[...] = (x * lax.rsqrt(ms + NORM_EPS) * g_ref[...]).astype(o_ref.dtype)


def _rmsnorm(x, g_all, l):
    m, d = x.shape
    tm = _tile(m, 512)
    return pl.pallas_call(
        _rms_kernel,
        out_shape=jax.ShapeDtypeStruct((m, d), BF16),
        grid=(m // tm,),
        in_specs=[pl.BlockSpec((tm, d), lambda i: (i, 0)), _layer_vec(l, d)],
        out_specs=pl.BlockSpec((tm, d), lambda i: (i, 0)),
        compiler_params=_cparams("parallel"),
        name="rmsnorm",
    )(x, _vec3(g_all))


def _pack_kernel(a_ref, b_ref, o_ref):
    j = pl.program_id(0)
    row = lax.broadcasted_iota(jnp.int32, (LANE, a_ref.shape[2]), 0)
    for l in range(a_ref.shape[1]):
        a, b = a_ref[:, l, :], b_ref[:, l, :]
        shifted = jnp.where(row < LANE - SHIFT, pltpu.roll(a, LANE - SHIFT, 0), pltpu.roll(b, LANE - SHIFT, 0))
        out = jnp.where(j < A_SLOTS, a, jnp.where(j == A_SLOTS, 0.0, shifted))
        o_ref[l] = out.T.astype(o_ref.dtype)


def _pack_w_in(w_in_all):
    depth, d, n_in = w_in_all.shape
    n_slots = A_SLOTS + 1 + (n_in - A_COLS) // LANE
    assert (n_in - A_COLS) % LANE == 0 and 0 < SHIFT < LANE
    src_a = lambda j: jnp.where(j < A_SLOTS, j, j - 2)
    src_b = lambda j: jnp.where(j < A_SLOTS, j, j - 1)
    w_t = jnp.transpose(w_in_all, (2, 0, 1))
    return pl.pallas_call(
        _pack_kernel,
        out_shape=jax.ShapeDtypeStruct((depth, d, n_slots * LANE), BF16),
        grid=(n_slots,),
        in_specs=[pl.BlockSpec((LANE, depth, d), lambda j: (src_a(j), 0, 0)),
                  pl.BlockSpec((LANE, depth, d), lambda j: (src_b(j), 0, 0))],
        out_specs=pl.BlockSpec((depth, d, LANE), lambda j: (0, 0, j)),
        compiler_params=_cparams("parallel"),
        name="pack_w_in",
    )(w_t, w_t)


def _mm_kernel(*refs, n_extra, nk, epilogue):
    a_ref, w_ref = refs[0], refs[1]
    extra = refs[2:2 + n_extra]
    o_ref = refs[2 + n_extra]
    if nk == 1:
        acc = jnp.dot(a_ref[...], w_ref[...].astype(BF16), preferred_element_type=F32)
        o_ref[...] = epilogue(acc, *[e[...] for e in extra]).astype(o_ref.dtype)
        return
    acc_ref = refs[3 + n_extra]
    k = pl.program_id(2)

    @pl.when(k == 0)
    def _():
        acc_ref[...] = jnp.zeros_like(acc_ref)

    acc_ref[...] += jnp.dot(a_ref[...], w_ref[...].astype(BF16), preferred_element_type=F32)

    @pl.when(k == nk - 1)
    def _():
        o_ref[...] = epilogue(acc_ref[...], *[e[...] for e in extra]).astype(o_ref.dtype)


def _matmul(a, w, epilogue, out_dtype, *, n, layer=None, col0=0, residual=None,
            tm=1536, tn=512, tk=2048, name="matmul"):
    m, kdim = a.shape
    tm, tn, tk = _tile(m, tm), _tile(n, tn), _tile(kdim, tk)
    nk = kdim // tk
    cb = col0 // tn
    assert col0 % tn == 0
    if layer is None:
        w_spec = pl.BlockSpec((tk, tn), lambda i, j, k: (k, cb + j))
    else:
        w_spec = pl.BlockSpec((None, tk, tn), lambda i, j, k: (layer, k, cb + j))
    specs = [pl.BlockSpec((tm, tk), lambda i, j, k: (i, k)), w_spec]
    extras = []
    if residual is not None:
        specs.append(pl.BlockSpec((tm, tn), lambda i, j, k: (i, j)))
        extras.append(residual)
    return pl.pallas_call(
        functools.partial(_mm_kernel, n_extra=len(extras), nk=nk, epilogue=epilogue),
        out_shape=jax.ShapeDtypeStruct((m, n), out_dtype),
        grid=(m // tm, n // tn, nk),
        in_specs=specs,
        out_specs=pl.BlockSpec((tm, tn), lambda i, j, k: (i, j)),
        scratch_shapes=[pltpu.VMEM((tm, tn), F32)] if nk > 1 else [],
        compiler_params=_cparams("parallel", "parallel", "arbitrary"),
        name=name,
    )(a, w, *extras)


def _cast_kernel(w_ref, o_ref):
    o_ref[...] = w_ref[...].astype(o_ref.dtype)


def _cast_bf16(w_all, l):
    _, kdim, n = w_all.shape
    tk = _tile(kdim, 512)
    return pl.pallas_call(
        _cast_kernel,
        out_shape=jax.ShapeDtypeStruct((kdim, n), BF16),
        grid=(kdim // tk,),
        in_specs=[pl.BlockSpec((None, tk, n), lambda i: (l, i, 0))],
        out_specs=pl.BlockSpec((tk, n), lambda i: (i, 0)),
        compiler_params=_cparams("parallel"),
        name="cast_bf16",
    )(w_all)


def _proj_kernel(h_ref, w_ref, gain_ref, nf_ref, rf_ref, cos_ref, sa_ref, sb_ref, bd_ref, o_ref, ob_ref,
                 xa_ref, xb_ref, *, sub, plain_tiles, norm_tiles):
    bd = bd_ref[...]
    j = pl.program_id(1)

    def matmul(dst_ref):
        dst_ref[...] = jnp.dot(h_ref[...], w_ref[...], preferred_element_type=F32)

    def epilogue(dst_ref, src_ref, with_norm, with_rope):
        matmul(dst_ref)
        normed = nf_ref[...] > 0.5
        gain, rf = gain_ref[...], rf_ref[...]
        for c in range(h_ref.shape[0] // sub):
            rows = slice(c * sub, (c + 1) * sub)
            y = src_ref[rows, :]
            if with_norm:
                ms = _seg_sum(y * y, bd) * (1.0 / HEAD_DIM)
                y = y * jnp.where(normed, lax.rsqrt(ms + NORM_EPS) * gain, 1.0)
            if with_rope:
                cosv, sav, sbv = cos_ref[rows, :], sa_ref[rows, :], sb_ref[rows, :]
            for s in range(2):
                out = y[:, s * LANE:(s + 1) * LANE]
                if with_rope:
                    f = rf[:, s * LANE:(s + 1) * LANE]
                    roped = out * cosv + pltpu.roll(out, 8, 1) * sav + pltpu.roll(out, LANE - 8, 1) * sbv
                    out = jnp.where(f > 0.5, roped, out)
                o_ref[rows, s * LANE:(s + 1) * LANE] = out
                ob_ref[rows, s * LANE:(s + 1) * LANE] = out.astype(BF16)

    any_of = lambda tiles: functools.reduce(jnp.logical_or, [j - 1 == tl for tl in tiles])
    is_first, is_plain, is_norm = j == 0, any_of(plain_tiles), any_of(norm_tiles)
    is_full = jnp.logical_not(is_first | is_plain | is_norm)
    pl.when(is_first)(lambda: matmul(xa_ref))
    for parity, (dst, src) in enumerate(((xa_ref, xb_ref), (xb_ref, xa_ref))):
        here = (j % 2) == parity
        pl.when(here & is_plain)(functools.partial(epilogue, dst, src, False, False))
        pl.when(here & is_norm)(functools.partial(epilogue, dst, src, True, False))
        pl.when(here & is_full)(functools.partial(epilogue, dst, src, True, True))


def _qkv_projection(h, w_pack, l, gain, nf, rf, cos_t, sin_a, sin_b):
    m, d = h.shape
    tm, tn = _tile(m, 1536), 2 * LANE
    zw = N_QKV_SLOTS * LANE
    n_tiles = zw // tn
    prev = lambda j: jnp.maximum(j - 1, 0)
    row = lambda i, j: (0, prev(j))
    tab = lambda i, j: (i, 0)
    no_rope = set(range(S_BQ, S_CQ)) | {S_AV, S_AIK + 1} | set(range(S_CV, N_QKV_SLOTS))
    no_norm = set(range(S_AV, S_BQ)) | set(range(S_BV, S_CQ)) | set(range(S_CV, N_QKV_SLOTS))
    tiles = range(N_QKV_SLOTS // 2)
    plain = tuple(tl for tl in tiles if {2 * tl, 2 * tl + 1} <= (no_rope & no_norm))
    norm_only = tuple(tl for tl in tiles if {2 * tl, 2 * tl + 1} <= no_rope and tl not in plain)
    return pl.pallas_call(
        functools.partial(_proj_kernel, sub=_tile(tm, 256), plain_tiles=plain, norm_tiles=norm_only),
        out_shape=[jax.ShapeDtypeStruct((m, zw), F32), jax.ShapeDtypeStruct((m, zw), BF16)],
        grid=(m // tm, n_tiles + 1),
        in_specs=[pl.BlockSpec((tm, d), lambda i, j: (i, 0)),
                  pl.BlockSpec((None, d, tn), lambda i, j: (l, 0, jnp.minimum(j, n_tiles - 1))),
                  pl.BlockSpec((1, tn), row), pl.BlockSpec((1, tn), row), pl.BlockSpec((1, tn), row),
                  pl.BlockSpec((tm, LANE), tab), pl.BlockSpec((tm, LANE), tab), pl.BlockSpec((tm, LANE), tab),
                  pl.BlockSpec((tn, tn), lambda i, j: (0, 0))],
        out_specs=[pl.BlockSpec((tm, tn), lambda i, j: (i, prev(j)))] * 2,
        scratch_shapes=[pltpu.VMEM((tm, tn), F32), pltpu.VMEM((tm, tn), F32)],
        compiler_params=_cparams("parallel", "arbitrary"),
        name="qkv_projection",
    )(h, w_pack, gain, nf, rf, cos_t, sin_a, sin_b, _ones_blockdiag(tn))


def _column_vectors(lw):
    f = lambda v: jnp.asarray(v, F32).reshape(-1)
    ones = lambda n: jnp.ones((n,), F32)
    zeros = lambda n: jnp.zeros((n,), F32)
    rep = lambda v, n: jnp.tile(f(v), n)
    groups = [
        (rep(lw["a_qn"], 8), 1.0, 1.0), (rep(lw["a_kn"], 2), 1.0, 1.0), (ones(128), 0.0, 0.0),
        (ones(256), 0.0, 1.0), (ones(64), 0.0, 1.0), (ones(64 + LANE), 0.0, 0.0),
        (rep(lw["b_qn"], 8), 1.0, 0.0), (rep(lw["b_kn"], 8), 1.0, 0.0), (ones(512), 0.0, 0.0),
        (rep(lw["c_qn"], 4), 1.0, 1.0), (rep(lw["c_kn"], 4), 1.0, 1.0), (ones(512), 0.0, 0.0)]
    gain = jnp.concatenate([g for g, _, _ in groups]).reshape(1, -1)
    flags = lambda idx: jnp.asarray(
        np.concatenate([np.full((g.shape[0],), grp[idx], np.float32) for grp in groups for g in grp[:1]])[None, :])
    del zeros
    return gain, flags(1), flags(2)


def _rope_tables(pos):
    half = ROPE_DIM // 2
    inv_freq = ROPE_THETA ** (-jnp.arange(0, ROPE_DIM, 2, dtype=F32) / ROPE_DIM)
    ang = pos.astype(F32)[:, None] * inv_freq[None, :]
    cos, sin = jnp.cos(ang), jnp.sin(ang)
    rows = pos.shape[0]
    one = jnp.ones((rows, HEAD_DIM - ROPE_DIM), F32)
    zero = jnp.zeros((rows, HEAD_DIM - ROPE_DIM), F32)
    z8 = jnp.zeros((rows, half), F32)
    cos_h = jnp.concatenate([cos, cos, one], axis=1)
    sa_h = jnp.concatenate([z8, sin, zero], axis=1)
    sb_h = jnp.concatenate([-sin, z8, zero], axis=1)
    dup = lambda t: jnp.concatenate([t, t], axis=1)
    return dup(cos_h), dup(sa_h), dup(sb_h)


def _to_key(score):
    bits = pltpu.bitcast(score + 0.0, jnp.int32)
    return jnp.where(bits < 0, bits ^ 0x7FFFFFFF, bits)


def _indexer_scores(iq, iw, ik):
    kk = ik[:, :HEAD_DIM]
    sc = None
    for hd in range(A_IDX_HEADS):
        logit = _dot_nt3(iq[:, hd * HEAD_DIM:(hd + 1) * HEAD_DIM], kk)
        wgt = iw[:, HEAD_DIM + hd:HEAD_DIM + hd + 1] * (A_IDX_HEADS ** -0.5 * HEAD_DIM ** -0.5)
        term = jnp.maximum(logit, 0.0) * wgt
        sc = term if sc is None else sc + term
    return sc


def _lane_fold(x, op):
    out = x[:, :LANE]
    for s in range(1, x.shape[1] // LANE):
        out = op(out, x[:, s * LANE:(s + 1) * LANE])
    return out


def _count(key_ref, n_blk, blk, pred):
    rows = key_ref.shape[0]
    grp = min(rows, LANE)
    parts = []
    for r0 in range(0, rows, grp):
        def body(b, acc, r0=r0):
            start = pl.multiple_of(b * blk, blk)
            kb = key_ref[r0:r0 + grp, pl.ds(start, blk)]
            return acc + _lane_fold(jnp.where(pred(kb, start, slice(r0, r0 + grp)), 1.0, 0.0), jnp.add)

        parts.append(lax.fori_loop(0, n_blk, body, jnp.zeros((grp, LANE), F32)))
    acc = parts[0] if len(parts) == 1 else jnp.concatenate(parts, axis=0)
    return jnp.sum(acc, axis=-1, keepdims=True)


def _topk_threshold(key_ref, n_blk, blk, topk):
    rows = key_ref.shape[0]
    kf = float(topk)
    c0 = _count(key_ref, n_blk, blk, lambda kb, st, rs: kb >= 0)
    ans = jnp.where(c0 >= kf, 0, INT_MIN).astype(jnp.int32)

    def bit_step(it, ans):
        cand = ans + jnp.left_shift(jnp.int32(1), 30 - it)
        c = _count(key_ref, n_blk, blk, lambda kb, st, rs: kb >= cand[rs])
        return jnp.where(c >= kf, cand, ans)

    thr = lax.fori_loop(0, 31, bit_step, ans)
    n_gt = _count(key_ref, n_blk, blk, lambda kb, st, rs: kb > thr[rs])
    n_eq = _count(key_ref, n_blk, blk, lambda kb, st, rs: kb == thr[rs])
    need = kf - n_gt
    nbits = int(n_blk * blk).bit_length() if isinstance(n_blk, int) else 14
    cut_all = jnp.full((rows, 1), 1 << nbits, jnp.int32)
    tie_overflow = jnp.max(jnp.where((n_eq > need) & (thr != NEG_KEY), 1.0, 0.0)) > 0.5

    def search_cut():
        def cut_step(it, cut):
            cand = cut + jnp.left_shift(jnp.int32(1), nbits - 1 - it)

            def pred(kb, st, rs):
                idx = st + lax.broadcasted_iota(jnp.int32, kb.shape, 1)
                return (kb == thr[rs]) & (idx < cand[rs])

            c = _count(key_ref, n_blk, blk, pred)
            return jnp.where(c <= need, cand, cut)

        return lax.fori_loop(0, nbits, cut_step, jnp.zeros((rows, 1), jnp.int32))

    cut = lax.cond(tie_overflow, search_cut, lambda: cut_all)
    return thr, cut


def _selected(kb, first_idx, thr, cut):
    idx = first_idx + lax.broadcasted_iota(jnp.int32, kb.shape, 1)
    return ((kb > thr) | ((kb == thr) & (idx < cut))) & (kb > NEG_KEY)


def _pair_heads(o_even, o_odd, group_half):
    lane = lax.broadcasted_iota(jnp.int32, o_even.shape, 1)
    if group_half == 0:
        return jnp.where(lane < HEAD_DIM, o_even, pltpu.roll(o_odd, HEAD_DIM, 1))
    return jnp.where(lane < HEAD_DIM, pltpu.roll(o_even, HEAD_DIM, 1), o_odd)


def _dsa_query_heads(q_ref, qh_ref):
    for hd in range(A_HEADS):
        slot, half = hd // 2, hd % 2
        group = hd // (A_HEADS // A_KV_HEADS)
        qs = q_ref[:, slot * LANE:(slot + 1) * LANE] * (HEAD_DIM ** -0.5)
        if half != group:
            qs = pltpu.roll(qs, HEAD_DIM, 1)
        qh_ref[hd] = jnp.where(_half_mask(qs.shape, group), qs, 0.0).astype(BF16)


def _softmax_block(sel, kblk, vblk, q_heads, m_ref, acc_ref, s_ref, p_ref):
    n = len(q_heads)
    reps = kblk.shape[0] // LANE
    v_ones = jnp.concatenate([vblk, jnp.ones_like(vblk)], axis=1)
    for hd in range(n):
        s_ref[hd] = _dot_nt(q_heads[hd], kblk)
    alphas = []
    for hd in range(n):
        s = s_ref[hd]
        if sel is not None:
            s = jnp.where(sel[hd] if isinstance(sel, (list, tuple)) else sel, s, -jnp.inf)
            s_ref[hd] = s
        blk_max = jnp.max(_lane_fold(s, jnp.maximum), axis=-1, keepdims=True)
        m_prev = m_ref[hd]
        m_new = jnp.maximum(m_prev, blk_max)
        alphas.append(jnp.exp(m_prev - m_new))
        m_ref[hd] = m_new
    for hd in range(n):
        p_ref[hd] = jnp.exp(s_ref[hd] - jnp.tile(m_ref[hd], (1, reps))).astype(BF16)
    for hd in range(n):
        alpha2 = jnp.tile(alphas[hd], (1, 2))
        acc_ref[hd] = alpha2 * acc_ref[hd] + jnp.dot(p_ref[hd], v_ones, preferred_element_type=F32)


def _softmax_init(m_ref, acc_ref):
    m_ref[...] = jnp.full(m_ref.shape, M_FLOOR, F32)
    acc_ref[...] = jnp.zeros(acc_ref.shape, F32)


def _softmax_result(acc_ref, hd):
    acc = acc_ref[hd]
    return acc[:, :LANE] / acc[:, LANE:]


def _indexer_query3(iq_ref, iq3_ref):
    for hd in range(A_IDX_HEADS):
        slot, half = hd // 2, hd % 2
        x = iq_ref[:, slot * LANE:(slot + 1) * LANE]
        xl = jnp.where(_half_mask(x.shape, half), x, pltpu.roll(x, HEAD_DIM, 1))
        hi = xl.astype(BF16).astype(F32)
        lower = _half_mask(x.shape, 0)
        iq3_ref[hd, :, :LANE] = jnp.where(lower, hi, xl - hi).astype(BF16)
        iq3_ref[hd, :, LANE:] = jnp.where(lower, hi, 0.0).astype(BF16)


def _ik3_kernel(x_ref, o_ref):
    x = x_ref[...]
    lower = _half_mask(x.shape, 0)
    xl = jnp.where(lower, x, pltpu.roll(x, HEAD_DIM, 1))
    hi = xl.astype(BF16).astype(F32)
    o_ref[:, :LANE] = hi.astype(BF16)
    o_ref[:, LANE:] = jnp.where(lower, xl - hi, 0.0).astype(BF16)


def _indexer_keys3(z, t):
    tm = _tile(t, 1024)
    return pl.pallas_call(
        _ik3_kernel,
        out_shape=jax.ShapeDtypeStruct((t, 2 * LANE), BF16),
        grid=(t // tm,),
        in_specs=[pl.BlockSpec((tm, LANE), lambda i: (i, S_AIK))],
        out_specs=pl.BlockSpec((tm, 2 * LANE), lambda i: (i, 0)),
        compiler_params=_cparams("parallel"),
        name="indexer_keys",
    )(z)


def _dsa_prompt_kernel(q_ref, iq_ref, iw_ref, k_ref, v_ref, ik3_ref, o_ref,
                       key_ref, qh_ref, iq3_ref, m_ref, acc_ref, s_ref, p_ref, *, tq, kb, topk):
    i = pl.program_id(0)
    q0 = i * tq
    n_blk = (q0 + tq + kb - 1) // kb
    row = lax.broadcasted_iota(jnp.int32, (tq, kb), 0)
    limit = (((q0 + row) >> 6) + 1) << 6
    _indexer_query3(iq_ref, iq3_ref)
    iw = iw_ref[...]
    wgt = [iw[:, HEAD_DIM + hd:HEAD_DIM + hd + 1] * (A_IDX_HEADS ** -0.5 * HEAD_DIM ** -0.5)
           for hd in range(A_IDX_HEADS)]

    def score_blk(b, carry):
        start = pl.multiple_of(b * kb, kb)
        ikb = ik3_ref[pl.ds(start, kb), :]
        for hd in range(A_IDX_HEADS):
            s_ref[hd] = _dot_nt(iq3_ref[hd], ikb)
        sc = None
        for hd in range(A_IDX_HEADS):
            term = jnp.maximum(s_ref[hd], 0.0) * wgt[hd]
            sc = term if sc is None else sc + term
        adm = start + lax.broadcasted_iota(jnp.int32, (tq, kb), 1) < limit
        key_ref[:, pl.ds(start, kb)] = jnp.where(adm, _to_key(sc), NEG_KEY)
        return carry

    lax.fori_loop(0, n_blk, score_blk, 0)
    thr, cut = _topk_threshold(key_ref, n_blk, kb, topk)
    _dsa_query_heads(q_ref, qh_ref)
    _softmax_init(m_ref, acc_ref)

    def attn_blk(b, carry):
        start = pl.multiple_of(b * kb, kb)
        sel = _selected(key_ref[:, pl.ds(start, kb)], start, thr, cut)
        _softmax_block(sel, k_ref[pl.ds(start, kb), :], v_ref[pl.ds(start, kb), :],
                       [qh_ref[hd] for hd in range(A_HEADS)], m_ref, acc_ref, s_ref, p_ref)
        return carry

    lax.fori_loop(0, n_blk, attn_blk, 0)
    for j in range(A_HEADS // 2):
        outs = [_softmax_result(acc_ref, 2 * j + e) for e in range(2)]
        o_ref[:, j * LANE:(j + 1) * LANE] = _pair_heads(outs[0], outs[1], (2 * j) // (A_HEADS // A_KV_HEADS)).astype(o_ref.dtype)


def _mixer_call(kern, o_all, m, *, grid, in_specs, out_block, out_index, args, scratch=(), sem, name):
    n_in = len(args)
    if o_all is None:
        fn, specs, alias, extra = kern, list(in_specs), {}, []
    else:
        fn = lambda *refs: kern(*refs[:n_in], *refs[n_in + 1:])
        specs, alias, extra = list(in_specs) + [pl.BlockSpec(memory_space=pl.ANY)], {n_in: 0}, [o_all]
    return pl.pallas_call(
        fn,
        out_shape=jax.ShapeDtypeStruct((m, N_BRANCH * BRANCH_WIDTH), BF16),
        grid=grid,
        in_specs=specs,
        out_specs=pl.BlockSpec(out_block, out_index),
        scratch_shapes=list(scratch),
        input_output_aliases=alias,
        compiler_params=_cparams(*sem),
        name=name,
    )(*args, *extra)


def _dsa_prompt(z, zb, t, o_all, m):
    tq, kb = 256, 512
    topk = min(A_TOPK_MAX, t // 4)
    full = lambda s: pl.BlockSpec((t, LANE), lambda i: (0, s))
    heads = lambda dt, w=LANE: pltpu.VMEM((A_HEADS, tq, w), dt)
    return _mixer_call(
        functools.partial(_dsa_prompt_kernel, tq=tq, kb=kb, topk=topk), o_all, m,
        grid=(t // tq,),
        in_specs=[pl.BlockSpec((tq, 4 * LANE), lambda i: (i, S_AQ // 4)),
                  pl.BlockSpec((tq, 2 * LANE), lambda i: (i, S_AIQ // 2)),
                  pl.BlockSpec((tq, LANE), lambda i: (i, S_AIK)),
                  full(S_AK), full(S_AV), pl.BlockSpec((t, 2 * LANE), lambda i: (0, 0))],
        out_block=(tq, BRANCH_WIDTH), out_index=lambda i: (i, 0),
        scratch=[pltpu.VMEM((tq, t), jnp.int32), heads(BF16),
                 pltpu.VMEM((A_IDX_HEADS, tq, 2 * LANE), BF16),
                 heads(F32), heads(F32, 2 * LANE), heads(F32, kb), heads(BF16, kb)],
        sem=("arbitrary",), name="dsa_prompt",
        args=(z, z, z, zb, zb, _indexer_keys3(z, t)))


def _feature_major(cache, n_feat):
    nd = cache.ndim
    perm = (0, 1) + tuple(range(3, nd)) + (2,)
    return jnp.transpose(cache, perm).reshape(cache.shape[0], cache.shape[1], n_feat, cache.shape[2])


def _dsa_sample_kernel(q_ref, iq_ref, new_ik_ref, new_k_ref, new_v_ref, ckt_ref, cvt_ref, cikt_ref, o_ref,
                       key_ref, qh_ref, *, ts, n_seq, past, topk, q_pos0):
    rows_all = n_seq * ts
    width = key_ref.shape[1]
    iq_all, iw_all = iq_ref[...], new_ik_ref[...]
    row_c = lax.broadcasted_iota(jnp.int32, (ts, past), 0)
    limit_c = (((q_pos0 + row_c) >> 6) + 1) << 6
    kpos_c = lax.broadcasted_iota(jnp.int32, (ts, past), 1)
    kpos_n = past + lax.broadcasted_iota(jnp.int32, (ts, ts), 1)
    limit_n = (((q_pos0 + lax.broadcasted_iota(jnp.int32, (ts, ts), 0)) >> 6) + 1) << 6
    key_ref[:, past:] = jnp.full((rows_all, width - past), NEG_KEY, jnp.int32)
    for g in range(n_seq):
        rows = slice(g * ts, (g + 1) * ts)
        iq, iw = iq_all[rows], iw_all[rows]
        ikt = cikt_ref[g]
        ik_new = iw[:, :HEAD_DIM]
        sc_c = sc_n = None
        for hd in range(A_IDX_HEADS):
            qh = iq[:, hd * HEAD_DIM:(hd + 1) * HEAD_DIM]
            wgt = iw[:, HEAD_DIM + hd:HEAD_DIM + hd + 1] * (A_IDX_HEADS ** -0.5 * HEAD_DIM ** -0.5)
            t_c = jnp.maximum(_dot3(qh, ikt), 0.0) * wgt
            t_n = jnp.maximum(_dot_nt3(qh, ik_new), 0.0) * wgt
            sc_c = t_c if sc_c is None else sc_c + t_c
            sc_n = t_n if sc_n is None else sc_n + t_n
        key_ref[rows, :past] = jnp.where(kpos_c < limit_c, _to_key(sc_c), NEG_KEY)
        key_ref[rows, past:past + ts] = jnp.where(kpos_n < limit_n, _to_key(sc_n), NEG_KEY)

    thr, cut = _topk_threshold(key_ref, width // LANE, LANE, topk)
    _dsa_query_heads(q_ref, qh_ref)
    for g in range(n_seq):
        rows = slice(g * ts, (g + 1) * ts)
        sel_c = _selected(key_ref[rows, :past], 0, thr[rows], cut[rows])
        sel_n = _selected(key_ref[rows, past:past + ts], past, thr[rows], cut[rows])
        kct, vct = ckt_ref[g].astype(BF16), cvt_ref[g].astype(BF16)
        kn, vn = new_k_ref[rows, :], new_v_ref[rows, :]
        outs = []
        for hd in range(A_HEADS):
            qh = qh_ref[hd, rows, :]
            s_c = jnp.where(sel_c, jnp.dot(qh, kct, preferred_element_type=F32), -jnp.inf)
            s_n = jnp.where(sel_n, _dot_nt(qh, kn), -jnp.inf)
            m = jnp.maximum(jnp.max(s_c, axis=-1, keepdims=True), jnp.max(s_n, axis=-1, keepdims=True))
            p_c, p_n = jnp.exp(s_c - m), jnp.exp(s_n - m)
            l = jnp.sum(p_c, axis=-1, keepdims=True) + jnp.sum(p_n, axis=-1, keepdims=True)
            o = _dot_nt(p_c.astype(BF16), vct) + jnp.dot(p_n.astype(BF16), vn, preferred_element_type=F32)
            outs.append(o / l)
        for j in range(A_HEADS // 2):
            o_ref[rows, j * LANE:(j + 1) * LANE] = _pair_heads(
                outs[2 * j], outs[2 * j + 1], (2 * j) // (A_HEADS // A_KV_HEADS)).astype(o_ref.dtype)


def _dsa_sample(z, zb, cache_k, cache_v, cache_ik, l, t, nb, ts, o_all, m):
    past = cache_k.shape[2]
    topk = min(A_TOPK_MAX, (past + ts) // 4)
    width = ((past + ts + LANE - 1) // LANE) * LANE
    n_seq = LANE // ts
    rows = n_seq * ts
    assert LANE % ts == 0 and nb % n_seq == 0 and t % rows == 0
    rb = t // rows
    new = lambda s: pl.BlockSpec((rows, LANE), lambda i: (rb + i, s))
    cache = lambda w: pl.BlockSpec((None, n_seq, w, past), lambda i: (l, i, 0, 0))
    return _mixer_call(
        functools.partial(_dsa_sample_kernel, ts=ts, n_seq=n_seq, past=past, topk=topk, q_pos0=past), o_all, m,
        grid=(nb // n_seq,),
        in_specs=[pl.BlockSpec((rows, 4 * LANE), lambda i: (rb + i, S_AQ // 4)),
                  pl.BlockSpec((rows, 2 * LANE), lambda i: (rb + i, S_AIQ // 2)),
                  new(S_AIK), new(S_AK), new(S_AV), cache(LANE), cache(LANE), cache(HEAD_DIM)],
        out_block=(rows, BRANCH_WIDTH), out_index=lambda i: (rb + i, 0),
        scratch=[pltpu.VMEM((rows, width), jnp.int32), pltpu.VMEM((A_HEADS, rows, LANE), BF16)],
        sem=("arbitrary",), name="dsa_sample",
        args=(z, z, z, zb, zb, _feature_major(cache_k, LANE), _feature_major(cache_v, LANE),
              _feature_major(cache_ik, HEAD_DIM)))


def _band_prompt_kernel(q_ref, k_ref, v_ref, bias0_ref, bias1_ref, o_ref, s_ref, p_ref, *, tq, win):
    i = pl.program_id(1)
    units = [(r, e) for r in range(2) for e in range(2)]
    biases = (bias0_ref, bias1_ref)
    w0s, oks = [], []
    for r in range(2):
        q0 = (2 * i + r) * tq
        w0 = pl.multiple_of(jnp.maximum(q0 - B_WINDOW, 0), tq)
        kc = (w0 + lax.broadcasted_iota(jnp.int32, (tq, win), 1)) >> 6
        qc = (q0 + lax.broadcasted_iota(jnp.int32, (tq, win), 0)) >> 6
        w0s.append(w0)
        oks.append((kc <= qc) & (kc >= qc - B_PAST_CHUNKS))
    for u, (r, e) in enumerate(units):
        qs = q_ref[r * tq:(r + 1) * tq, :] * (HEAD_DIM ** -0.5)
        qe = jnp.where(_half_mask(qs.shape, e), qs, 0.0).astype(BF16)
        s_ref[u] = _dot_nt(qe, k_ref[pl.ds(w0s[r], win), :])
    for u, (r, e) in enumerate(units):
        s = jnp.where(oks[r], s_ref[u] + biases[r][e], -jnp.inf)
        p_ref[u] = jnp.exp(s - jnp.max(s, axis=-1, keepdims=True)).astype(BF16)
    outs = []
    for u, (r, e) in enumerate(units):
        vw = v_ref[pl.ds(w0s[r], win), :]
        acc = jnp.dot(p_ref[u], jnp.concatenate([vw, jnp.ones_like(vw)], axis=1), preferred_element_type=F32)
        outs.append(acc[:, :LANE] / acc[:, LANE:])
    for r in range(2):
        o_ref[r * tq:(r + 1) * tq, :] = jnp.where(_half_mask(outs[0].shape, 0), outs[2 * r],
                                                   outs[2 * r + 1]).astype(o_ref.dtype)


def _band_prompt(z, zb, bias, t, tq, o_all, m):
    win = B_WINDOW + tq
    n_case = B_WINDOW // tq
    assert t % (2 * tq) == 0
    case = lambda r: pl.BlockSpec((None, 2, tq, win), lambda j, i: (jnp.minimum(2 * i + r, n_case), j, 0, 0))
    return _mixer_call(
        functools.partial(_band_prompt_kernel, tq=tq, win=win), o_all, m,
        grid=(B_HEADS // 2, t // (2 * tq)),
        in_specs=[pl.BlockSpec((2 * tq, LANE), lambda j, i: (i, S_BQ + j)),
                  pl.BlockSpec((t, LANE), lambda j, i: (0, S_BK + j)),
                  pl.BlockSpec((t, LANE), lambda j, i: (0, S_BV + j)),
                  case(0), case(1)],
        out_block=(2 * tq, LANE), out_index=lambda j, i: (i, BRANCH_WIDTH // LANE + j),
        scratch=[pltpu.VMEM((4, tq, win), F32), pltpu.VMEM((4, tq, win), BF16)],
        sem=("parallel", "arbitrary"), name="band_prompt", args=(z, zb, zb, bias, bias))


def _band_sample_kernel(z_ref, zb_ref, kct_ref, vct_ref, bc_ref, bn_ref, o_ref):
    for j in range(B_HEADS // 2):
        slot = lambda ref, s: ref[:, (s + j) * LANE:(s + j + 1) * LANE]
        feat = slice(j * LANE, (j + 1) * LANE)
        kct, vct = kct_ref[feat, :].astype(BF16), vct_ref[feat, :].astype(BF16)
        kn, vn = slot(zb_ref, S_BK), slot(zb_ref, S_BV)
        qs = slot(z_ref, S_BQ) * (HEAD_DIM ** -0.5)
        outs = []
        for e in range(2):
            qe = jnp.where(_half_mask(qs.shape, e), qs, 0.0).astype(BF16)
            s_c = jnp.dot(qe, kct, preferred_element_type=F32) + bc_ref[2 * j + e]
            s_n = _dot_nt(qe, kn) + bn_ref[2 * j + e]
            m = jnp.maximum(jnp.max(s_c, axis=-1, keepdims=True), jnp.max(s_n, axis=-1, keepdims=True))
            p_c, p_n = jnp.exp(s_c - m), jnp.exp(s_n - m)
            l = jnp.sum(p_c, axis=-1, keepdims=True) + jnp.sum(p_n, axis=-1, keepdims=True)
            o = _dot_nt(p_c.astype(BF16), vct) + jnp.dot(p_n.astype(BF16), vn, preferred_element_type=F32)
            outs.append(o / l)
        o_ref[:, feat] = jnp.where(_half_mask(outs[0].shape, 0), outs[0], outs[1]).astype(o_ref.dtype)


def _band_sample(z, zb, cache_k, cache_v, bias_c, bias_n, l, t, nb, ts, o_all, m):
    pb = cache_k.shape[2]
    rb = t // ts
    bw = BRANCH_WIDTH
    cache = pl.BlockSpec((None, None, bw, pb), lambda b: (l, b, 0, 0))
    full = lambda a: pl.BlockSpec(a.shape, lambda b: (0,) * a.ndim)
    return _mixer_call(
        _band_sample_kernel, o_all, m,
        grid=(nb,),
        in_specs=[pl.BlockSpec((ts, z.shape[1]), lambda b: (rb + b, 0)),
                  pl.BlockSpec((ts, z.shape[1]), lambda b: (rb + b, 0)),
                  cache, cache, full(bias_c), full(bias_n)],
        out_block=(ts, bw), out_index=lambda b: (rb + b, 1),
        sem=("parallel",), name="band_sample",
        args=(z, zb, _feature_major(cache_k, bw), _feature_major(cache_v, bw), bias_c, bias_n))


def _band_bias(table, tq):
    n_case = B_WINDOW // tq
    win = B_WINDOW + tq
    width = win + B_WINDOW
    j = np.arange(width + tq - 1) - (tq - 1) - B_WINDOW
    ext = table.astype(F32)[:, np.clip(j, -B_REL_CLIP, B_REL_CLIP) + B_REL_CLIP]
    n = ext.shape[1]
    x = jnp.roll(ext, -(tq - 1), axis=1)
    toe = jnp.tile(x, (1, tq))[:, :tq * (n - 1)].reshape(ext.shape[0], tq, n - 1)[:, :, :width]
    return jnp.stack([toe[:, :, B_WINDOW - c * tq:B_WINDOW - c * tq + win] for c in range(n_case + 1)], axis=0)


def _lambda(lam_ref, lam_init):
    lv = lam_ref[...]
    return (jnp.exp(jnp.sum(lv[0:1] * lv[1:2], axis=-1, keepdims=True))
            - jnp.exp(jnp.sum(lv[2:3] * lv[3:4], axis=-1, keepdims=True)) + lam_init)


def _diff_finish(o0, o1, lam, on_ref, lam_init):
    attn = o0 - lam * o1
    ms = jnp.mean(attn * attn, axis=-1, keepdims=True)
    return (attn * lax.rsqrt(ms + NORM_EPS) * on_ref[...]) * (1.0 - lam_init)


def _diff_prompt_kernel(q_ref, k_ref, v_ref, lam_ref, on_ref, o_ref, qh_ref, m_ref, acc_ref, s_ref, p_ref,
                        *, tq, kb, tg, lam_init):
    i = pl.program_id(1)
    q0 = i * tq
    n_grp = tq // tg
    n_blk = (q0 + tq + kb - 1) // kb
    row = lax.broadcasted_iota(jnp.int32, (tg, kb), 0)
    limits = [(((q0 + g * tg + row) >> 6) + 1) << 6 for g in range(n_grp)]
    for g in range(n_grp):
        qs = q_ref[g * tg:(g + 1) * tg, :] * (HEAD_DIM ** -0.5)
        for c in range(2):
            qh_ref[2 * g + c] = jnp.where(_half_mask(qs.shape, c), qs, 0.0).astype(BF16)
    _softmax_init(m_ref, acc_ref)
    n_full = (q0 + CHUNK) // kb

    def attn_blk(b, carry, masked):
        start = pl.multiple_of(b * kb, kb)
        ok = None
        if masked:
            kpos = start + lax.broadcasted_iota(jnp.int32, (tg, kb), 1)
            ok = [kpos < limits[u // 2] for u in range(2 * n_grp)]
        _softmax_block(ok, k_ref[pl.ds(start, kb), :], v_ref[pl.ds(start, kb), :],
                       [qh_ref[u] for u in range(2 * n_grp)], m_ref, acc_ref, s_ref, p_ref)
        return carry

    lax.fori_loop(0, n_full, functools.partial(attn_blk, masked=False), 0)
    lax.fori_loop(n_full, n_blk, functools.partial(attn_blk, masked=True), 0)
    lam = _lambda(lam_ref, lam_init)
    for g in range(n_grp):
        outs = [_softmax_result(acc_ref, 2 * g + c) for c in range(2)]
        o_ref[g * tg:(g + 1) * tg, :] = _diff_finish(outs[0], outs[1], lam, on_ref, lam_init).astype(o_ref.dtype)


def _diff_prompt(z, zb, c_lam_all, c_on_all, l, t, lam_init, o_all, m):
    tq, kb, tg = _tile(t, 512), 512, 128
    maps = lambda dt, w=LANE: pltpu.VMEM((2 * tq // tg, tg, w), dt)
    return _mixer_call(
        functools.partial(_diff_prompt_kernel, tq=tq, kb=kb, tg=tg, lam_init=lam_init), o_all, m,
        grid=(C_HEADS, t // tq),
        in_specs=[pl.BlockSpec((tq, LANE), lambda h, i: (i, S_CQ + h)),
                  pl.BlockSpec((t, LANE), lambda h, i: (0, S_CK + h)),
                  pl.BlockSpec((t, LANE), lambda h, i: (0, S_CV + h)),
                  pl.BlockSpec((None, 4, HEAD_DIM), lambda h, i: (l, 0, 0)),
                  _layer_vec(l, LANE)],
        out_block=(tq, LANE), out_index=lambda h, i: (i, 2 * BRANCH_WIDTH // LANE + h),
        scratch=[maps(BF16), maps(F32), maps(F32, 2 * LANE), maps(F32, kb), maps(BF16, kb)],
        sem=("parallel", "arbitrary"), name="diff_prompt",
        args=(z, zb, zb, c_lam_all, _vec3(c_on_all)))


def _diff_sample_kernel(z_ref, zb_ref, kct_ref, vc_ref, lam_ref, on_ref, o_ref, *, past, lam_init):
    lam = _lambda(lam_ref, lam_init)
    for h in range(C_HEADS):
        slot = lambda ref, s: ref[:, (s + h) * LANE:(s + h + 1) * LANE]
        feat = slice(h * LANE, (h + 1) * LANE)
        kct = kct_ref[feat, :].astype(BF16)
        vc = vc_ref[pl.ds(h, past, stride=C_HEADS), :].astype(BF16)
        kn, vn = slot(zb_ref, S_CK), slot(zb_ref, S_CV)
        qs = slot(z_ref, S_CQ) * (HEAD_DIM ** -0.5)
        outs = []
        for c in range(2):
            qc = jnp.where(_half_mask(qs.shape, c), qs, 0.0).astype(BF16)
            s_c = jnp.dot(qc, kct, preferred_element_type=F32)
            s_n = _dot_nt(qc, kn)
            m = jnp.maximum(jnp.max(s_c, axis=-1, keepdims=True), jnp.max(s_n, axis=-1, keepdims=True))
            p_c, p_n = jnp.exp(s_c - m), jnp.exp(s_n - m)
            l = jnp.sum(p_c, axis=-1, keepdims=True) + jnp.sum(p_n, axis=-1, keepdims=True)
            o = (jnp.dot(p_c.astype(BF16), vc, preferred_element_type=F32)
                 + jnp.dot(p_n.astype(BF16), vn, preferred_element_type=F32))
            outs.append(o / l)
        o_ref[:, feat] = _diff_finish(outs[0], outs[1], lam, on_ref, lam_init).astype(o_ref.dtype)


def _diff_sample(z, zb, cache_k, cache_v, c_lam_all, c_on_all, l, t, nb, ts, lam_init, o_all, m):
    depth, _, past = cache_k.shape[:3]
    rb = t // ts
    bw = BRANCH_WIDTH
    return _mixer_call(
        functools.partial(_diff_sample_kernel, past=past, lam_init=lam_init), o_all, m,
        grid=(nb,),
        in_specs=[pl.BlockSpec((ts, z.shape[1]), lambda b: (rb + b, 0)),
                  pl.BlockSpec((ts, z.shape[1]), lambda b: (rb + b, 0)),
                  pl.BlockSpec((None, None, bw, past), lambda b: (l, b, 0, 0)),
                  pl.BlockSpec((None, None, past * C_HEADS, LANE), lambda b: (l, b, 0, 0)),
                  pl.BlockSpec((None, 4, HEAD_DIM), lambda b: (l, 0, 0)),
                  _layer_vec(l, LANE)],
        out_block=(ts, bw), out_index=lambda b: (rb + b, 2),
        sem=("parallel",), name="diff_sample",
        args=(z, zb, _feature_major(cache_k, bw), cache_v.reshape(depth, nb, past * C_HEADS, LANE),
              c_lam_all, _vec3(c_on_all)))


def _rwkv_pre_kernel(zd_ref, prev_ref, first_ref, mu_ref, w0_ref, w2_ref, a0_ref, a2_ref, g2_ref, kkw_ref,
                     ka_ref, bd_ref, r_ref, w_ref, k_ref, v_ref, kk_ref, b_ref, g_ref, *, tm, t, ts):
    zf = zd_ref[...]
    grow = pl.program_id(0) * tm + lax.broadcasted_iota(jnp.int32, (tm, 1), 0)
    zs = jnp.where(grow % tm == 0, prev_ref[7:8, :], pltpu.roll(zf, 1, 0))
    seq_start = (grow >= t) & ((grow - t) % ts == 0)
    zs = jnp.where(seq_start, first_ref[...], zs)
    zs = jnp.where(grow == 0, 0.0, zs)
    zm = zf + (zs - zf) * mu_ref[...]
    bw = BRANCH_WIDTH
    r, k, v = zm[:, :bw], zm[:, bw:2 * bw], zm[:, 2 * bw:3 * bw]
    wl, al, gl = zm[:, 3 * bw:3 * bw + 64], zm[:, 3 * bw + 64:3 * bw + 128], zm[:, 3 * bw + 128:]
    dot = lambda a, b: jnp.dot(a.astype(BF16), b.astype(BF16), preferred_element_type=F32)
    u = -(w0_ref[...] + dot(jnp.tanh(wl), w2_ref[...]))
    softplus = jnp.maximum(u, 0.0) + jnp.log(1.0 + jnp.exp(-jnp.abs(u)))
    w = -softplus - 0.5
    a = jax.nn.sigmoid(a0_ref[...] + dot(al, a2_ref[...]))
    kk = k * kkw_ref[...]
    nrm = jnp.sqrt(_seg_sum(kk * kk, bd_ref[...]))
    kk = kk / jnp.maximum(nrm, 1e-12)
    r_ref[...] = r
    w_ref[...] = -jnp.exp(w)
    k_ref[...] = k * (1.0 + (a - 1.0) * ka_ref[...])
    v_ref[...] = v
    kk_ref[...] = kk
    b_ref[...] = kk * a
    g_ref[...] = dot(jax.nn.sigmoid(gl), g2_ref[...])


def _rwkv_pre(zd, zfirst, p, l, t, ts):
    m, dc = zd.shape
    tm = _tile(math.gcd(t, m - t), 512)
    bw = BRANCH_WIDTH
    npt = t // tm
    rows = pl.BlockSpec((tm, dc), lambda i: (i, 0))
    vec = lambda n: _layer_vec(l, n)
    mat = lambda k: pl.BlockSpec((None, k, bw), lambda i: (l, 0, 0))
    out = jax.ShapeDtypeStruct((m, bw), F32)
    return pl.pallas_call(
        functools.partial(_rwkv_pre_kernel, tm=tm, t=t, ts=ts),
        out_shape=[out] * 7,
        grid=(m // tm,),
        in_specs=[rows,
                  pl.BlockSpec((8, dc), lambda i: (jnp.maximum(i * (tm // 8) - 1, 0), 0)),
                  pl.BlockSpec((tm, dc), lambda i: (jnp.maximum(i - npt, 0), 0)),
                  vec(dc), vec(bw), mat(64), vec(bw), mat(64), mat(128), vec(bw), vec(bw),
                  pl.BlockSpec((bw, bw), lambda i: (0, 0))],
        out_specs=[pl.BlockSpec((tm, bw), lambda i: (i, 0))] * 7,
        compiler_params=_cparams("parallel"),
        name="rwkv_pre",
    )(zd, zd, zfirst, _vec3(p["d_mu"]), _vec3(p["d_w0"]), p["d_w2"], _vec3(p["d_a0"]), p["d_a2"], p["d_g2"],
      _vec3(p["d_k_k"]), _vec3(p["d_k_a"]), _ones_blockdiag(bw))


RW_CHUNK_MAX = 64
RW_BLOCK = 128
RW_SLOTS = BRANCH_WIDTH // LANE


def _dot3(a, b):
    ah, al = _split2(a)
    bh, bl = _split2(b)
    d = lambda x, y: jnp.dot(x, y, preferred_element_type=F32)
    return d(ah, bh) + d(al, bh) + d(ah, bl)


def _rwkv_chunk_kernel(*refs, seq_chunks, RW_CHUNK):
    if seq_chunks:
        (r_ref, lw_ref, k_ref, v_ref, kk_ref, b_ref, s0_ref, y_ref, sf_ref,
         h_ref, u_ref, ab_ref, rb_ref, u0_ref, y0_ref, bt_ref, kt_ref, eg_ref) = refs
    else:
        (r_ref, lw_ref, k_ref, v_ref, kk_ref, b_ref, y_ref, sf_ref,
         h_ref, u_ref, ab_ref, rb_ref, u0_ref, y0_ref, bt_ref, kt_ref, eg_ref) = refs
        s0_ref = None

        @pl.when(pl.program_id(0) == 0)
        def _():
            h_ref[...] = jnp.zeros_like(h_ref)

    n = RW_BLOCK
    n_chunks = n // RW_CHUNK
    row = lax.broadcasted_iota(jnp.int32, (n, n), 0)
    col = lax.broadcasted_iota(jnp.int32, (n, n), 1)
    same = (row // RW_CHUNK) == (col // RW_CHUNK)
    strict, incl = same & (col < row), same & (col <= row)
    eye = jnp.where(row == col, 1.0, 0.0)
    head_diag = (row // HEAD_DIM) == (col // HEAD_DIM)
    in_chunk = row % RW_CHUNK
    k_pick = jnp.where(lax.broadcasted_iota(jnp.int32, (n, HEAD_DIM), 0) % HEAD_DIM
                       == lax.broadcasted_iota(jnp.int32, (n, HEAD_DIM), 1), 1.0, 0.0)
    bf = lambda x: x.astype(BF16)
    mm = lambda x, y: jnp.dot(bf(x), bf(y), preferred_element_type=F32)
    u_ref[...] = jnp.zeros_like(u_ref)

    halves = [_half_mask((n, LANE), e) for e in range(2)]
    group = 2
    for j0 in range(0, RW_SLOTS, group):
        slot_data = []
        for j in range(j0, j0 + group):
            sl = slice(j * LANE, (j + 1) * LANE)
            lw, v = lw_ref[:, sl], v_ref[:, sl]
            g = lw
            for d in [1 << s for s in range(RW_CHUNK.bit_length() - 1)]:
                g = g + jnp.where(in_chunk >= d, pltpu.roll(g, d, 0), 0.0)
            inv_g = jnp.exp(-g)
            a_t = -kk_ref[:, sl] * jnp.exp(g - lw)
            b_t, k_t = b_ref[:, sl] * inv_g, k_ref[:, sl] * inv_g
            r_t = r_ref[:, sl] * jnp.exp(g)
            bt_ref[j], kt_ref[j], eg_ref[j] = b_t.T, k_t.T, jnp.exp(g).T
            slot_data.append((a_t, r_t, bf(b_t), bf(k_t), bf(v)))
        chains = [(s, e) for s in range(group) for e in range(2)]
        each = lambda f: [f(s, e, i) for i, (s, e) in enumerate(chains)]
        a_e = each(lambda s, e, i: bf(jnp.where(halves[e], slot_data[s][0], 0.0)))
        r_e = each(lambda s, e, i: bf(jnp.where(halves[e], slot_data[s][1], 0.0)))
        n_ab = each(lambda s, e, i: jnp.where(strict, _dot_nt(a_e[i], slot_data[s][2]), 0.0))
        n_ak = each(lambda s, e, i: jnp.where(strict, _dot_nt(a_e[i], slot_data[s][3]), 0.0))
        m_rb = each(lambda s, e, i: bf(jnp.where(incl, _dot_nt(r_e[i], slot_data[s][2]), 0.0)))
        m_rk = each(lambda s, e, i: bf(jnp.where(incl, _dot_nt(r_e[i], slot_data[s][3]), 0.0)))
        w_e = each(lambda s, e, i: mm(n_ak[i], slot_data[s][4]))
        tinv = each(lambda s, e, i: eye + n_ab[i])
        pw = n_ab
        for _ in range(RW_CHUNK.bit_length() - 2):
            pw = each(lambda s, e, i: mm(pw[i], pw[i]))
            tinv = each(lambda s, e, i: tinv[i] + mm(tinv[i], pw[i]))
        t16 = each(lambda s, e, i: bf(tinv[i]))
        a_bar = each(lambda s, e, i: mm(t16[i], slot_data[s][0]))
        u0 = each(lambda s, e, i: mm(t16[i], w_e[i]))
        r_bar = each(lambda s, e, i: slot_data[s][1] + mm(m_rb[i], a_bar[i]))
        y0 = each(lambda s, e, i: mm(m_rb[i], u0[i]) + mm(m_rk[i], slot_data[s][4]))
        for s in range(group):
            pick = lambda vals: jnp.where(halves[0], vals[2 * s], vals[2 * s + 1])
            j = j0 + s
            ab_ref[j], rb_ref[j], u0_ref[j], y0_ref[j] = pick(a_bar), pick(r_bar), pick(u0), pick(y0)

    for c in range(n_chunks):
        rows = slice(c * RW_CHUNK, (c + 1) * RW_CHUNK)
        col_c = (col // RW_CHUNK) == c
        for j in range(RW_SLOTS):
            sl = slice(j * LANE, (j + 1) * LANE)
            if seq_chunks and c % seq_chunks == 0:
                x = s0_ref[c // seq_chunks, sl, :]
                h = jnp.where(head_diag, _dot_nt3(k_pick, x), 0.0)
            else:
                h = h_ref[j]
            res = _dot3(jnp.concatenate([ab_ref[j, rows, :], rb_ref[j, rows, :]], axis=0), h)
            u_c = res[:RW_CHUNK] + u0_ref[j, rows, :]
            y_ref[rows, sl] = res[RW_CHUNK:] + y0_ref[j, rows, :]
            u_ref[j, rows, :] = u_c
            bk = jnp.concatenate([jnp.where(col_c, bt_ref[j], 0.0), jnp.where(col_c, kt_ref[j], 0.0)], axis=1)
            uv = jnp.concatenate([u_ref[j], v_ref[:, sl]], axis=0)
            inc = jnp.where(head_diag, mm(bk, uv), 0.0)
            g_end = eg_ref[j, :, (c + 1) * RW_CHUNK - 1:(c + 1) * RW_CHUNK]
            h = g_end * (h + inc)
            h_ref[j] = h
            if seq_chunks and (c + 1) % seq_chunks == 0:
                ht = h.T
                sf_ref[c // seq_chunks, sl, :] = (ht + pltpu.roll(ht, HEAD_DIM, 1))[:, :HEAD_DIM]

    if not seq_chunks:
        @pl.when(pl.program_id(0) == pl.num_programs(0) - 1)
        def _():
            for j in range(RW_SLOTS):
                ht = h_ref[j].T
                sf_ref[0, j * LANE:(j + 1) * LANE, :] = (ht + pltpu.roll(ht, HEAD_DIM, 1))[:, :HEAD_DIM]


def _rwkv_chunked(ops, s0, row0, n_seq, t):
    n = RW_BLOCK
    bw = BRANCH_WIDTH
    chunk = math.gcd(t, RW_CHUNK_MAX)
    assert row0 % n == 0 and (n_seq * t) % n == 0 and chunk >= 8 and chunk & (chunk - 1) == 0
    rb = row0 // n
    rows = pl.BlockSpec((n, bw), lambda i: (rb + i, 0))
    if s0 is None:
        assert n_seq == 1
        seq_chunks, per_blk, extra, extra_specs = 0, 1, [], []
        sf_spec = pl.BlockSpec((1, bw, HEAD_DIM), lambda i: (0, 0, 0))
    else:
        assert n % t == 0
        seq_chunks, per_blk = t // chunk, n // t
        extra = [s0.reshape(n_seq, bw, HEAD_DIM)]
        extra_specs = [pl.BlockSpec((per_blk, bw, HEAD_DIM), lambda i: (i, 0, 0))]
        sf_spec = pl.BlockSpec((per_blk, bw, HEAD_DIM), lambda i: (i, 0, 0))
    slot = lambda dt=F32: pltpu.VMEM((RW_SLOTS, n, LANE), dt)
    y, sf = pl.pallas_call(
        functools.partial(_rwkv_chunk_kernel, seq_chunks=seq_chunks, RW_CHUNK=chunk),
        out_shape=[jax.ShapeDtypeStruct((n_seq * t, bw), F32), jax.ShapeDtypeStruct((n_seq, bw, HEAD_DIM), F32)],
        grid=(n_seq * t // n,),
        in_specs=[rows] * 6 + extra_specs,
        out_specs=[pl.BlockSpec((n, bw), lambda i: (i, 0)), sf_spec],
        scratch_shapes=[slot() for _ in range(9)],
        compiler_params=_cparams("arbitrary"),
        name="rwkv_chunked",
    )(*ops, *extra)
    return y, sf.reshape(n_seq, D_HEADS, HEAD_DIM, HEAD_DIM)


def _rwkv_post_kernel(y_ref, r_ref, k_ref, v_ref, g_ref, lnw_ref, lnb_ref, rk_ref, bd_ref, o_ref):
    bd = bd_ref[...]
    y = y_ref[...]
    mean = _seg_sum(y, bd) * (1.0 / HEAD_DIM)
    yc = y - mean
    var = _seg_sum(yc * yc, bd) * (1.0 / HEAD_DIM)
    yn = yc * lax.rsqrt(var + D_GN_EPS) * lnw_ref[...] + lnb_ref[...]
    bonus = _seg_sum(r_ref[...] * k_ref[...] * rk_ref[...], bd) * v_ref[...]
    o_ref[...] = ((yn + bonus) * g_ref[...]).astype(o_ref.dtype)


def _rwkv_post(y, r, k, v, g, p, l, o_all):
    m, bw = y.shape
    tm = _tile(m, 512)
    rows = pl.BlockSpec((tm, bw), lambda i: (i, 0))
    vec = _layer_vec(l, bw)
    return _mixer_call(
        _rwkv_post_kernel, o_all, m,
        grid=(m // tm,),
        in_specs=[rows] * 5 + [vec] * 3 + [pl.BlockSpec((bw, bw), lambda i: (0, 0))],
        out_block=(tm, bw), out_index=lambda i: (i, N_BRANCH - 1),
        sem=("parallel",), name="rwkv_post",
        args=(y, r, k, v, g, _vec3(p["d_ln_w"]), _vec3(p["d_ln_b"]), _vec3(p["d_r_k"]), _ones_blockdiag(bw)))


def _merge_kernel(o_ref, wbr_ref, h_ref, wg_ref, out_ref, acc_ref):
    n = pl.program_id(2)

    @pl.when(n == 0)
    def _():
        acc_ref[...] = jnp.zeros_like(acc_ref)

    u = jnp.dot(o_ref[...], wbr_ref[...].astype(BF16), preferred_element_type=F32)
    gate = jax.nn.sigmoid(jnp.dot(h_ref[...], wg_ref[...], preferred_element_type=F32))
    acc_ref[...] += u * gate

    @pl.when(n == N_BRANCH - 1)
    def _():
        out_ref[...] = acc_ref[...].astype(out_ref.dtype)


def _merge(o_all, w_br_all, h, w_pack, l):
    m, d = h.shape
    tm, tn = _tile(m, 1536), 512
    nj = d // tn
    g0 = S_GATE * LANE // tn
    return pl.pallas_call(
        _merge_kernel,
        out_shape=jax.ShapeDtypeStruct((m, d), BF16),
        grid=(m // tm, nj, N_BRANCH),
        in_specs=[pl.BlockSpec((tm, BRANCH_WIDTH), lambda i, j, n: (i, n)),
                  pl.BlockSpec((None, None, BRANCH_WIDTH, tn), lambda i, j, n: (l, n, 0, j)),
                  pl.BlockSpec((tm, d), lambda i, j, n: (i, 0)),
                  pl.BlockSpec((None, d, tn), lambda i, j, n: (l, 0, g0 + n * nj + j))],
        out_specs=pl.BlockSpec((tm, tn), lambda i, j, n: (i, j)),
        scratch_shapes=[pltpu.VMEM((tm, tn), F32)],
        compiler_params=_cparams("parallel", "parallel", "arbitrary"),
        name="branch_merge",
    )(o_all, w_br_all, h, w_pack)


def _ple_kernel(x_ref, pe_ref, wple_ref, h_ref, wg_ref, o_ref):
    emb = jnp.dot(pe_ref[...].astype(BF16), wple_ref[...].astype(BF16), preferred_element_type=F32)
    gate = jax.nn.sigmoid(jnp.dot(h_ref[...], wg_ref[...].astype(BF16), preferred_element_type=F32))
    o_ref[...] = x_ref[...] + emb * gate


def _ple(x, pe, w_ple_all, h, w_gate_all, l):
    m, d = x.shape
    pd = pe.shape[1]
    tm, tn = _tile(m, 1536), 512
    return pl.pallas_call(
        _ple_kernel,
        out_shape=jax.ShapeDtypeStruct((m, d), F32),
        grid=(m // tm, d // tn),
        in_specs=[pl.BlockSpec((tm, tn), lambda i, j: (i, j)),
                  pl.BlockSpec((tm, pd), lambda i, j: (i, 0)),
                  pl.BlockSpec((None, pd, tn), lambda i, j: (l, 0, j)),
                  pl.BlockSpec((tm, d), lambda i, j: (i, 0)),
                  pl.BlockSpec((None, d, tn), lambda i, j: (l, 0, j))],
        out_specs=pl.BlockSpec((tm, tn), lambda i, j: (i, j)),
        compiler_params=_cparams("parallel", "parallel"),
        name="ple",
    )(x, pe, w_ple_all, h, w_gate_all)


def _layer(l, x, pe, p, t, nb, ts, tabs):
    m = x.shape[0]
    lam_init = 0.8 - 0.6 * math.exp(-0.3 * l)
    dc = p["d_mu"].shape[1]
    d = x.shape[1]
    lw = {"a_qn": p["a_q_norm"][l], "a_kn": p["a_k_norm"][l], "b_qn": p["b_q_norm"][l], "b_kn": p["b_k_norm"][l],
          "c_qn": p["c_q_norm"][l], "c_kn": p["c_k_norm"][l]}

    h = _rmsnorm(x, p["norm1_g"], l)
    w_pack = p["w_pack"]
    z, zb = _qkv_projection(h, w_pack, l, *_column_vectors(lw), *tabs)
    zd = _matmul(h, w_pack, lambda acc: acc, F32, n=dc, layer=l, col0=S_D * LANE, tn=256, name="rwkv_projection")

    o_all = _dsa_prompt(z, zb, t, None, m)
    o_all = _dsa_sample(z, zb, p["cache_a_k"], p["cache_a_v"], p["cache_a_kidx"], l, t, nb, ts, o_all, m)
    tq_b = 128
    bias = _band_bias(p["b_rel_bias"][l], tq_b)
    pb = p["cache_b_k"].shape[2]
    generic = bias[B_WINDOW // tq_b]
    o_all = _band_prompt(z, zb, bias, t, tq_b, o_all, m)
    o_all = _band_sample(z, zb, p["cache_b_k"], p["cache_b_v"], generic[:, :ts, B_WINDOW - pb:B_WINDOW],
                         generic[:, :ts, B_WINDOW:B_WINDOW + ts], l, t, nb, ts, o_all, m)
    c_on = p["c_out_norm"]
    o_all = _diff_prompt(z, zb, p["c_lambda"], c_on, l, t, lam_init, o_all, m)
    o_all = _diff_sample(z, zb, p["cache_c_k"], p["cache_c_v"], p["c_lambda"], c_on, l, t, nb, ts, lam_init,
                         o_all, m)

    zfirst = jnp.broadcast_to(p["state_d_shift"][l], (nb, ts, dc)).reshape(nb * ts, dc)
    r, w, k, v, kk, b, g = _rwkv_pre(zd, zfirst, p, l, t, ts)
    ops = (r, w, k, v, kk, b)
    y_p, wkv_p = _rwkv_chunked(ops, None, 0, 1, t)
    y_s, wkv_s = _rwkv_chunked(ops, p["state_d_wkv"][l], t, nb, ts)
    o_all = _rwkv_post(jnp.concatenate([y_p, y_s], axis=0), r, k, v, g, p, l, o_all)

    ug = _merge(o_all, p["w_branch"], h, w_pack, l)
    res = lambda acc, r_: r_ + acc
    x = _matmul(ug, p["w_out"], res, F32, n=d, layer=l, residual=x, name="out_proj")
    h2 = _rmsnorm(x, p["norm2_g"], l)
    ffn = p["w_up"].shape[2]
    up = _matmul(h2, p["w_up"], lambda acc: jnp.square(jnp.maximum(acc, 0.0)), BF16, n=ffn, layer=l, name="mlp_up")
    x = _matmul(up, _cast_bf16(p["w_down"], l), res, F32, n=d, residual=x, tm=1024, tn=1024, name="mlp_down")
    h3 = _rmsnorm(x, p["norm3_g"], l)
    x = _ple(x, pe, p["w_ple"], h3, p["w_ple_gate"], l)

    slot = lambda rows, s, n=1: z[rows, s * LANE:(s + n) * LANE]

    def rows_of(sl, lead):
        a = lambda s, n, shape: slot(sl, s, n).reshape(lead + shape)
        ak = a(S_AK, 1, (A_KV_HEADS, HEAD_DIM))
        av = a(S_AV, 1, (A_KV_HEADS, HEAD_DIM))
        aik = slot(sl, S_AIK)[:, :HEAD_DIM].reshape(lead + (HEAD_DIM,))
        bk = a(S_BK, 4, (B_HEADS, HEAD_DIM))
        bv = a(S_BV, 4, (B_HEADS, HEAD_DIM))
        ck = a(S_CK, 4, (C_HEADS, 2, HEAD_DIM))
        cv = a(S_CV, 4, (C_HEADS, 2 * HEAD_DIM))
        return ak, av, aik, bk, bv, ck, cv

    keep = min(B_WINDOW, t)
    pak, pav, paik, pbk, pbv, pck, pcv = rows_of(slice(0, t), (1, t))
    new_p = (pak, pav, paik, pbk[:, t - keep:], pbv[:, t - keep:], pck, pcv, wkv_p, zd[t - 1:t].reshape(1, 1, dc))
    new_s = rows_of(slice(t, m), (nb, ts)) + (wkv_s, zd[t:].reshape(nb, ts, dc)[:, -1:])
    return x, new_p, new_s


def kernel(x_prompt, x_sample, cache_a_k, cache_a_v, cache_a_kidx, cache_b_k, cache_b_v, cache_c_k, cache_c_v, state_d_wkv, state_d_shift, p_prompt, p_sample, norm1_g, w_in, a_q_norm, a_k_norm, b_q_norm, b_k_norm, b_rel_bias, c_q_norm, c_k_norm, c_lambda, c_out_norm, d_mu, d_w0, d_w2, d_a0, d_a2, d_g2, d_k_k, d_k_a, d_r_k, d_ln_w, d_ln_b, w_branch, w_out, norm2_g, w_up, w_down, norm3_g, w_ple, w_ple_gate):
    batch, t, d = x_prompt.shape
    nb, ts, _ = x_sample.shape
    past = cache_a_k.shape[2]
    depth = w_in.shape[0]
    assert batch == 1 and t % 512 == 0 and past % CHUNK == 0 and ts <= CHUNK and (nb * ts) % 8 == 0
    p = dict(cache_a_k=cache_a_k, cache_a_v=cache_a_v, cache_a_kidx=cache_a_kidx, cache_b_k=cache_b_k,
             cache_b_v=cache_b_v, cache_c_k=cache_c_k, cache_c_v=cache_c_v, state_d_wkv=state_d_wkv,
             state_d_shift=state_d_shift, norm1_g=norm1_g, w_in=w_in, a_q_norm=a_q_norm, a_k_norm=a_k_norm,
             b_q_norm=b_q_norm, b_k_norm=b_k_norm, b_rel_bias=b_rel_bias, c_q_norm=c_q_norm, c_k_norm=c_k_norm,
             c_lambda=c_lambda, c_out_norm=c_out_norm, d_mu=d_mu, d_w0=d_w0, d_w2=d_w2, d_a0=d_a0, d_a2=d_a2,
             d_g2=d_g2, d_k_k=d_k_k, d_k_a=d_k_a, d_r_k=d_r_k, d_ln_w=d_ln_w, d_ln_b=d_ln_b, w_branch=w_branch,
             w_out=w_out, norm2_g=norm2_g, w_up=w_up, w_down=w_down, norm3_g=norm3_g, w_ple=w_ple,
             w_ple_gate=w_ple_gate)
    p["w_pack"] = _pack_w_in(w_in)
    x = jnp.concatenate([x_prompt[0], x_sample.reshape(nb * ts, d)], axis=0)
    pos = jnp.concatenate([jnp.arange(t, dtype=jnp.int32),
                           jnp.tile(past + jnp.arange(ts, dtype=jnp.int32), nb)])
    tabs = _rope_tables(pos)
    st_p = [[] for _ in range(9)]
    st_s = [[] for _ in range(9)]
    for l in range(depth):
        pe = jnp.concatenate([p_prompt[l, 0], p_sample[l].reshape(nb * ts, -1)], axis=0)
        x, new_p, new_s = _layer(l, x, pe, p, t, nb, ts, tabs)
        for lst, arr in zip(st_p, new_p):
            lst.append(arr)
        for lst, arr in zip(st_s, new_s):
            lst.append(arr)
    outs_p = [jnp.stack(s, axis=0) for s in st_p]
    outs_s = [jnp.stack(s, axis=0) for s in st_s]
    return (x[:t].reshape(1, t, d), x[t:].reshape(nb, ts, d), *outs_p, *outs_s)
```

```python
import functools
import math

import numpy as np
import jax
import jax.numpy as jnp
from jax import lax
from jax.experimental import pallas as pl
from jax.experimental.pallas import tpu as pltpu

F32 = jnp.float32
BF16 = jnp.bfloat16

CHUNK = 64
HEAD_DIM = 64
ROPE_DIM = 16
ROPE_THETA = 500000.0
N_BRANCH = 4
BRANCH_WIDTH = 512
A_HEADS, A_KV_HEADS, A_IDX_HEADS = 8, 2, 4
A_TOPK_MAX = 256
B_HEADS = 8
B_PAST_CHUNKS = 8
B_WINDOW = B_PAST_CHUNKS * CHUNK
B_REL_CLIP = 128
C_HEADS = 4
D_HEADS = 8
D_GN_EPS = 64e-5
NORM_EPS = 1e-6

LANE = 128
VMEM_LIMIT = 48 * 1024 * 1024

S_AQ, S_AK, S_AV, S_AIQ, S_AIK = 0, 4, 5, 6, 8
S_BQ, S_BK, S_BV = 10, 14, 18
S_CQ, S_CK, S_CV = 22, 26, 30
S_D, S_GATE = 34, 48
N_QKV_SLOTS = 34
N_D_SLOTS = 14
A_COLS = 1092
A_SLOTS = 9
SHIFT = A_COLS - (A_SLOTS - 1) * LANE

NEG_KEY = -2139095041
INT_MIN = -2147483648
M_FLOOR = -1e30


def _cparams(*sem):
    return pltpu.CompilerParams(dimension_semantics=sem, vmem_limit_bytes=VMEM_LIMIT)


def _tile(n, pref):
    t = min(n, pref)
    while n % t:
        t -= 8
    return t


def _split2(x):
    hi = x.astype(BF16)
    lo = (x - hi.astype(F32)).astype(BF16)
    return hi, lo


def _seg_sum(x, ones_bd):
    hi, lo = _split2(x)
    return (jnp.dot(hi, ones_bd, preferred_element_type=F32)
            + jnp.dot(lo, ones_bd, preferred_element_type=F32))


def _dot_nt(a, b):
    return lax.dot_general(a, b, (((1,), (1,)), ((), ())), preferred_element_type=F32)


def _dot_nt3(a, b):
    ah, al = _split2(a)
    bh, bl = _split2(b)
    return _dot_nt(ah, bh) + _dot_nt(ah, bl) + _dot_nt(al, bh)


def _ones_blockdiag(n):
    i = np.arange(n)
    return jnp.asarray((i[:, None] // HEAD_DIM) == (i[None, :] // HEAD_DIM), dtype=BF16)


def _vec3(a):
    return a.reshape(a.shape[0], 1, -1)


def _layer_vec(l, n):
    return pl.BlockSpec((None, 1, n), lambda *_: (l, 0, 0))


def _half_mask(shape, half):
    lane = lax.broadcasted_iota(jnp.int32, shape, len(shape) - 1)
    return (lane < HEAD_DIM) if half == 0 else (lane >= HEAD_DIM)


def _rms_kernel(x_ref, g_ref, o_ref):
    x = x_ref[...]
    ms = jnp.mean(x * x, axis=-1, keepdims=True)
    o_ref[...] = (x * lax.rsqrt(ms + NORM_EPS) * g_ref[...]).astype(o_ref.dtype)


def _rmsnorm(x, g_all, l):
    m, d = x.shape
    tm = _tile(m, 512)
    return pl.pallas_call(
        _rms_kernel,
        out_shape=jax.ShapeDtypeStruct((m, d), BF16),
        grid=(m // tm,),
        in_specs=[pl.BlockSpec((tm, d), lambda i: (i, 0)), _layer_vec(l, d)],
        out_specs=pl.BlockSpec((tm, d), lambda i: (i, 0)),
        compiler_params=_cparams("parallel"),
        name="rmsnorm",
    )(x, _vec3(g_all))


def _pack_kernel(a_ref, b_ref, o_ref):
    j = pl.program_id(0)
    row = lax.broadcasted_iota(jnp.int32, (LANE, a_ref.shape[2]), 0)
    for l in range(a_ref.shape[1]):
        a, b = a_ref[:, l, :], b_ref[:, l, :]
        shifted = jnp.where(row < LANE - SHIFT, pltpu.roll(a, LANE - SHIFT, 0), pltpu.roll(b, LANE - SHIFT, 0))
        out = jnp.where(j < A_SLOTS, a, jnp.where(j == A_SLOTS, 0.0, shifted))
        o_ref[l] = out.T.astype(o_ref.dtype)


def _pack_w_in(w_in_all):
    depth, d, n_in = w_in_all.shape
    n_slots = A_SLOTS + 1 + (n_in - A_COLS) // LANE
    assert (n_in - A_COLS) % LANE == 0 and 0 < SHIFT < LANE
    src_a = lambda j: jnp.where(j < A_SLOTS, j, j - 2)
    src_b = lambda j: jnp.where(j < A_SLOTS, j, j - 1)
    w_t = jnp.transpose(w_in_all, (2, 0, 1))
    return pl.pallas_call(
        _pack_kernel,
        out_shape=jax.ShapeDtypeStruct((depth, d, n_slots * LANE), BF16),
        grid=(n_slots,),
        in_specs=[pl.BlockSpec((LANE, depth, d), lambda j: (src_a(j), 0, 0)),
                  pl.BlockSpec((LANE, depth, d), lambda j: (src_b(j), 0, 0))],
        out_specs=pl.BlockSpec((depth, d, LANE), lambda j: (0, 0, j)),
        compiler_params=_cparams("parallel"),
        name="pack_w_in",
    )(w_t, w_t)


def _mm_kernel(*refs, n_extra, nk, epilogue):
    a_ref, w_ref = refs[0], refs[1]
    extra = refs[2:2 + n_extra]
    o_ref = refs[2 + n_extra]
    if nk == 1:
        acc = jnp.dot(a_ref[...], w_ref[...].astype(BF16), preferred_element_type=F32)
        o_ref[...] = epilogue(acc, *[e[...] for e in extra]).astype(o_ref.dtype)
        return
    acc_ref = refs[3 + n_extra]
    k = pl.program_id(2)

    @pl.when(k == 0)
    def _():
        acc_ref[...] = jnp.zeros_like(acc_ref)

    acc_ref[...] += jnp.dot(a_ref[...], w_ref[...].astype(BF16), preferred_element_type=F32)

    @pl.when(k == nk - 1)
    def _():
        o_ref[...] = epilogue(acc_ref[...], *[e[...] for e in extra]).astype(o_ref.dtype)


def _matmul(a, w, epilogue, out_dtype, *, n, layer=None, col0=0, residual=None,
            tm=1536, tn=512, tk=2048, name="matmul"):
    m, kdim = a.shape
    tm, tn, tk = _tile(m, tm), _tile(n, tn), _tile(kdim, tk)
    nk = kdim // tk
    cb = col0 // tn
    assert col0 % tn == 0
    if layer is None:
        w_spec = pl.BlockSpec((tk, tn), lambda i, j, k: (k, cb + j))
    else:
        w_spec = pl.BlockSpec((None, tk, tn), lambda i, j, k: (layer, k, cb + j))
    specs = [pl.BlockSpec((tm, tk), lambda i, j, k: (i, k)), w_spec]
    extras = []
    if residual is not None:
        specs.append(pl.BlockSpec((tm, tn), lambda i, j, k: (i, j)))
        extras.append(residual)
    return pl.pallas_call(
        functools.partial(_mm_kernel, n_extra=len(extras), nk=nk, epilogue=epilogue),
        out_shape=jax.ShapeDtypeStruct((m, n), out_dtype),
        grid=(m // tm, n // tn, nk),
        in_specs=specs,
        out_specs=pl.BlockSpec((tm, tn), lambda i, j, k: (i, j)),
        scratch_shapes=[pltpu.VMEM((tm, tn), F32)] if nk > 1 else [],
        compiler_params=_cparams("parallel", "parallel", "arbitrary"),
        name=name,
    )(a, w, *extras)


def _cast_kernel(w_ref, o_ref):
    o_ref[...] = w_ref[...].astype(o_ref.dtype)


def _cast_bf16(w_all, l):
    _, kdim, n = w_all.shape
    tk = _tile(kdim, 512)
    return pl.pallas_call(
        _cast_kernel,
        out_shape=jax.ShapeDtypeStruct((kdim, n), BF16),
        grid=(kdim // tk,),
        in_specs=[pl.BlockSpec((None, tk, n), lambda i: (l, i, 0))],
        out_specs=pl.BlockSpec((tk, n), lambda i: (i, 0)),
        compiler_params=_cparams("parallel"),
        name="cast_bf16",
    )(w_all)


def _proj_kernel(h_ref, w_ref, gain_ref, nf_ref, rf_ref, cos_ref, sa_ref, sb_ref, bd_ref, o_ref, ob_ref,
                 xa_ref, xb_ref, *, sub, plain_tiles, norm_tiles):
    bd = bd_ref[...]
    j = pl.program_id(1)

    def matmul(dst_ref):
        dst_ref[...] = jnp.dot(h_ref[...], w_ref[...], preferred_element_type=F32)

    def epilogue(dst_ref, src_ref, with_norm, with_rope):
        matmul(dst_ref)
        normed = nf_ref[...] > 0.5
        gain, rf = gain_ref[...], rf_ref[...]
        for c in range(h_ref.shape[0] // sub):
            rows = slice(c * sub, (c + 1) * sub)
            y = src_ref[rows, :]
            if with_norm:
                ms = _seg_sum(y * y, bd) * (1.0 / HEAD_DIM)
                y = y * jnp.where(normed, lax.rsqrt(ms + NORM_EPS) * gain, 1.0)
            if with_rope:
                cosv, sav, sbv = cos_ref[rows, :], sa_ref[rows, :], sb_ref[rows, :]
            for s in range(2):
                out = y[:, s * LANE:(s + 1) * LANE]
                if with_rope:
                    f = rf[:, s * LANE:(s + 1) * LANE]
                    roped = out * cosv + pltpu.roll(out, 8, 1) * sav + pltpu.roll(out, LANE - 8, 1) * sbv
                    out = jnp.where(f > 0.5, roped, out)
                o_ref[rows, s * LANE:(s + 1) * LANE] = out
                ob_ref[rows, s * LANE:(s + 1) * LANE] = out.astype(BF16)

    any_of = lambda tiles: functools.reduce(jnp.logical_or, [j - 1 == tl for tl in tiles])
    is_first, is_plain, is_norm = j == 0, any_of(plain_tiles), any_of(norm_tiles)
    is_full = jnp.logical_not(is_first | is_plain | is_norm)
    pl.when(is_first)(lambda: matmul(xa_ref))
    for parity, (dst, src) in enumerate(((xa_ref, xb_ref), (xb_ref, xa_ref))):
        here = (j % 2) == parity
        pl.when(here & is_plain)(functools.partial(epilogue, dst, src, False, False))
        pl.when(here & is_norm)(functools.partial(epilogue, dst, src, True, False))
        pl.when(here & is_full)(functools.partial(epilogue, dst, src, True, True))


def _qkv_projection(h, w_pack, l, gain, nf, rf, cos_t, sin_a, sin_b):
    m, d = h.shape
    tm, tn = _tile(m, 1536), 2 * LANE
    zw = N_QKV_SLOTS * LANE
    n_tiles = zw // tn
    prev = lambda j: jnp.maximum(j - 1, 0)
    row = lambda i, j: (0, prev(j))
    tab = lambda i, j: (i, 0)
    no_rope = set(range(S_BQ, S_CQ)) | {S_AV, S_AIK + 1} | set(range(S_CV, N_QKV_SLOTS))
    no_norm = set(range(S_AV, S_BQ)) | set(range(S_BV, S_CQ)) | set(range(S_CV, N_QKV_SLOTS))
    tiles = range(N_QKV_SLOTS // 2)
    plain = tuple(tl for tl in tiles if {2 * tl, 2 * tl + 1} <= (no_rope & no_norm))
    norm_only = tuple(tl for tl in tiles if {2 * tl, 2 * tl + 1} <= no_rope and tl not in plain)
    return pl.pallas_call(
        functools.partial(_proj_kernel, sub=_tile(tm, 256), plain_tiles=plain, norm_tiles=norm_only),
        out_shape=[jax.ShapeDtypeStruct((m, zw), F32), jax.ShapeDtypeStruct((m, zw), BF16)],
        grid=(m // tm, n_tiles + 1),
        in_specs=[pl.BlockSpec((tm, d), lambda i, j: (i, 0)),
                  pl.BlockSpec((None, d, tn), lambda i, j: (l, 0, jnp.minimum(j, n_tiles - 1))),
                  pl.BlockSpec((1, tn), row), pl.BlockSpec((1, tn), row), pl.BlockSpec((1, tn), row),
                  pl.BlockSpec((tm, LANE), tab), pl.BlockSpec((tm, LANE), tab), pl.BlockSpec((tm, LANE), tab),
                  pl.BlockSpec((tn, tn), lambda i, j: (0, 0))],
        out_specs=[pl.BlockSpec((tm, tn), lambda i, j: (i, prev(j)))] * 2,
        scratch_shapes=[pltpu.VMEM((tm, tn), F32), pltpu.VMEM((tm, tn), F32)],
        compiler_params=_cparams("parallel", "arbitrary"),
        name="qkv_projection",
    )(h, w_pack, gain, nf, rf, cos_t, sin_a, sin_b, _ones_blockdiag(tn))


def _column_vectors(lw):
    f = lambda v: jnp.asarray(v, F32).reshape(-1)
    ones = lambda n: jnp.ones((n,), F32)
    zeros = lambda n: jnp.zeros((n,), F32)
    rep = lambda v, n: jnp.tile(f(v), n)
    groups = [
        (rep(lw["a_qn"], 8), 1.0, 1.0), (rep(lw["a_kn"], 2), 1.0, 1.0), (ones(128), 0.0, 0.0),
        (ones(256), 0.0, 1.0), (ones(64), 0.0, 1.0), (ones(64 + LANE), 0.0, 0.0),
        (rep(lw["b_qn"], 8), 1.0, 0.0), (rep(lw["b_kn"], 8), 1.0, 0.0), (ones(512), 0.0, 0.0),
        (rep(lw["c_qn"], 4), 1.0, 1.0), (rep(lw["c_kn"], 4), 1.0, 1.0), (ones(512), 0.0, 0.0)]
    gain = jnp.concatenate([g for g, _, _ in groups]).reshape(1, -1)
    flags = lambda idx: jnp.asarray(
        np.concatenate([np.full((g.shape[0],), grp[idx], np.float32) for grp in groups for g in grp[:1]])[None, :])
    del zeros
    return gain, flags(1), flags(2)


def _rope_tables(pos):
    half = ROPE_DIM // 2
    inv_freq = ROPE_THETA ** (-jnp.arange(0, ROPE_DIM, 2, dtype=F32) / ROPE_DIM)
    ang = pos.astype(F32)[:, None] * inv_freq[None, :]
    cos, sin = jnp.cos(ang), jnp.sin(ang)
    rows = pos.shape[0]
    one = jnp.ones((rows, HEAD_DIM - ROPE_DIM), F32)
    zero = jnp.zeros((rows, HEAD_DIM - ROPE_DIM), F32)
    z8 = jnp.zeros((rows, half), F32)
    cos_h = jnp.concatenate([cos, cos, one], axis=1)
    sa_h = jnp.concatenate([z8, sin, zero], axis=1)
    sb_h = jnp.concatenate([-sin, z8, zero], axis=1)
    dup = lambda t: jnp.concatenate([t, t], axis=1)
    return dup(cos_h), dup(sa_h), dup(sb_h)


def _to_key(score):
    bits = pltpu.bitcast(score + 0.0, jnp.int32)
    return jnp.where(bits < 0, bits ^ 0x7FFFFFFF, bits)


def _indexer_scores(iq, iw, ik):
    kk = ik[:, :HEAD_DIM]
    sc = None
    for hd in range(A_IDX_HEADS):
        logit = _dot_nt3(iq[:, hd * HEAD_DIM:(hd + 1) * HEAD_DIM], kk)
        wgt = iw[:, HEAD_DIM + hd:HEAD_DIM + hd + 1] * (A_IDX_HEADS ** -0.5 * HEAD_DIM ** -0.5)
        term = jnp.maximum(logit, 0.0) * wgt
        sc = term if sc is None else sc + term
    return sc


def _lane_fold(x, op):
    out = x[:, :LANE]
    for s in range(1, x.shape[1] // LANE):
        out = op(out, x[:, s * LANE:(s + 1) * LANE])
    return out


def _count(key_ref, n_blk, blk, pred):
    rows = key_ref.shape[0]
    grp = min(rows, LANE)
    parts = []
    for r0 in range(0, rows, grp):
        def body(b, acc, r0=r0):
            start = pl.multiple_of(b * blk, blk)
            kb = key_ref[r0:r0 + grp, pl.ds(start, blk)]
            return acc + _lane_fold(jnp.where(pred(kb, start, slice(r0, r0 + grp)), 1.0, 0.0), jnp.add)

        parts.append(lax.fori_loop(0, n_blk, body, jnp.zeros((grp, LANE), F32)))
    acc = parts[0] if len(parts) == 1 else jnp.concatenate(parts, axis=0)
    return jnp.sum(acc, axis=-1, keepdims=True)


def _topk_threshold(key_ref, n_blk, blk, topk):
    rows = key_ref.shape[0]
    kf = float(topk)
    c0 = _count(key_ref, n_blk, blk, lambda kb, st, rs: kb >= 0)
    ans = jnp.where(c0 >= kf, 0, INT_MIN).astype(jnp.int32)

    def bit_step(it, ans):
        cand = ans + jnp.left_shift(jnp.int32(1), 30 - it)
        c = _count(key_ref, n_blk, blk, lambda kb, st, rs: kb >= cand[rs])
        return jnp.where(c >= kf, cand, ans)

    thr = lax.fori_loop(0, 31, bit_step, ans)
    n_gt = _count(key_ref, n_blk, blk, lambda kb, st, rs: kb > thr[rs])
    n_eq = _count(key_ref, n_blk, blk, lambda kb, st, rs: kb == thr[rs])
    need = kf - n_gt
    nbits = int(n_blk * blk).bit_length() if isinstance(n_blk, int) else 14
    cut_all = jnp.full((rows, 1), 1 << nbits, jnp.int32)
    tie_overflow = jnp.max(jnp.where((n_eq > need) & (thr != NEG_KEY), 1.0, 0.0)) > 0.5

    def search_cut():
        def cut_step(it, cut):
            cand = cut + jnp.left_shift(jnp.int32(1), nbits - 1 - it)

            def pred(kb, st, rs):
                idx = st + lax.broadcasted_iota(jnp.int32, kb.shape, 1)
                return (kb == thr[rs]) & (idx < cand[rs])

            c = _count(key_ref, n_blk, blk, pred)
            return jnp.where(c <= need, cand, cut)

        return lax.fori_loop(0, nbits, cut_step, jnp.zeros((rows, 1), jnp.int32))

    cut = lax.cond(tie_overflow, search_cut, lambda: cut_all)
    return thr, cut


def _selected(kb, first_idx, thr, cut):
    idx = first_idx + lax.broadcasted_iota(jnp.int32, kb.shape, 1)
    return ((kb > thr) | ((kb == thr) & (idx < cut))) & (kb > NEG_KEY)


def _pair_heads(o_even, o_odd, group_half):
    lane = lax.broadcasted_iota(jnp.int32, o_even.shape, 1)
    if group_half == 0:
        return jnp.where(lane < HEAD_DIM, o_even, pltpu.roll(o_odd, HEAD_DIM, 1))
    return jnp.where(lane < HEAD_DIM, pltpu.roll(o_even, HEAD_DIM, 1), o_odd)


def _dsa_query_heads(q_ref, qh_ref):
    for hd in range(A_HEADS):
        slot, half = hd // 2, hd % 2
        group = hd // (A_HEADS // A_KV_HEADS)
        qs = q_ref[:, slot * LANE:(slot + 1) * LANE] * (HEAD_DIM ** -0.5)
        if half != group:
            qs = pltpu.roll(qs, HEAD_DIM, 1)
        qh_ref[hd] = jnp.where(_half_mask(qs.shape, group), qs, 0.0).astype(BF16)


def _softmax_block(sel, kblk, vblk, q_heads, m_ref, acc_ref, s_ref, p_ref):
    n = len(q_heads)
    reps = kblk.shape[0] // LANE
    v_ones = jnp.concatenate([vblk, jnp.ones_like(vblk)], axis=1)
    for hd in range(n):
        s_ref[hd] = _dot_nt(q_heads[hd], kblk)
    alphas = []
    for hd in range(n):
        s = s_ref[hd]
        if sel is not None:
            s = jnp.where(sel[hd] if isinstance(sel, (list, tuple)) else sel, s, -jnp.inf)
            s_ref[hd] = s
        blk_max = jnp.max(_lane_fold(s, jnp.maximum), axis=-1, keepdims=True)
        m_prev = m_ref[hd]
        m_new = jnp.maximum(m_prev, blk_max)
        alphas.append(jnp.exp(m_prev - m_new))
        m_ref[hd] = m_new
    for hd in range(n):
        p_ref[hd] = jnp.exp(s_ref[hd] - jnp.tile(m_ref[hd], (1, reps))).astype(BF16)
    for hd in range(n):
        alpha2 = jnp.tile(alphas[hd], (1, 2))
        acc_ref[hd] = alpha2 * acc_ref[hd] + jnp.dot(p_ref[hd], v_ones, preferred_element_type=F32)


def _softmax_init(m_ref, acc_ref):
    m_ref[...] = jnp.full(m_ref.shape, M_FLOOR, F32)
    acc_ref[...] = jnp.zeros(acc_ref.shape, F32)


def _softmax_result(acc_ref, hd):
    acc = acc_ref[hd]
    return acc[:, :LANE] / acc[:, LANE:]


def _indexer_query3(iq_ref, iq3_ref):
    for hd in range(A_IDX_HEADS):
        slot, half = hd // 2, hd % 2
        x = iq_ref[:, slot * LANE:(slot + 1) * LANE]
        xl = jnp.where(_half_mask(x.shape, half), x, pltpu.roll(x, HEAD_DIM, 1))
        hi = xl.astype(BF16).astype(F32)
        lower = _half_mask(x.shape, 0)
        iq3_ref[hd, :, :LANE] = jnp.where(lower, hi, xl - hi).astype(BF16)
        iq3_ref[hd, :, LANE:] = jnp.where(lower, hi, 0.0).astype(BF16)


def _ik3_kernel(x_ref, o_ref):
    x = x_ref[...]
    lower = _half_mask(x.shape, 0)
    xl = jnp.where(lower, x, pltpu.roll(x, HEAD_DIM, 1))
    hi = xl.astype(BF16).astype(F32)
    o_ref[:, :LANE] = hi.astype(BF16)
    o_ref[:, LANE:] = jnp.where(lower, xl - hi, 0.0).astype(BF16)


def _indexer_keys3(z, t):
    tm = _tile(t, 1024)
    return pl.pallas_call(
        _ik3_kernel,
        out_shape=jax.ShapeDtypeStruct((t, 2 * LANE), BF16),
        grid=(t // tm,),
        in_specs=[pl.BlockSpec((tm, LANE), lambda i: (i, S_AIK))],
        out_specs=pl.BlockSpec((tm, 2 * LANE), lambda i: (i, 0)),
        compiler_params=_cparams("parallel"),
        name="indexer_keys",
    )(z)


def _dsa_prompt_kernel(q_ref, iq_ref, iw_ref, k_ref, v_ref, ik3_ref, o_ref,
                       key_ref, qh_ref, iq3_ref, m_ref, acc_ref, s_ref, p_ref, *, tq, kb, topk):
    i = pl.program_id(0)
    q0 = i * tq
    n_blk = (q0 + tq + kb - 1) // kb
    row = lax.broadcasted_iota(jnp.int32, (tq, kb), 0)
    limit = (((q0 + row) >> 6) + 1) << 6
    _indexer_query3(iq_ref, iq3_ref)
    iw = iw_ref[...]
    wgt = [iw[:, HEAD_DIM + hd:HEAD_DIM + hd + 1] * (A_IDX_HEADS ** -0.5 * HEAD_DIM ** -0.5)
           for hd in range(A_IDX_HEADS)]

    def score_blk(b, carry):
        start = pl.multiple_of(b * kb, kb)
        ikb = ik3_ref[pl.ds(start, kb), :]
        for hd in range(A_IDX_HEADS):
            s_ref[hd] = _dot_nt(iq3_ref[hd], ikb)
        sc = None
        for hd in range(A_IDX_HEADS):
            term = jnp.maximum(s_ref[hd], 0.0) * wgt[hd]
            sc = term if sc is None else sc + term
        adm = start + lax.broadcasted_iota(jnp.int32, (tq, kb), 1) < limit
        key_ref[:, pl.ds(start, kb)] = jnp.where(adm, _to_key(sc), NEG_KEY)
        return carry

    lax.fori_loop(0, n_blk, score_blk, 0)
    thr, cut = _topk_threshold(key_ref, n_blk, kb, topk)
    _dsa_query_heads(q_ref, qh_ref)
    _softmax_init(m_ref, acc_ref)

    def attn_blk(b, carry):
        start = pl.multiple_of(b * kb, kb)
        sel = _selected(key_ref[:, pl.ds(start, kb)], start, thr, cut)
        _softmax_block(sel, k_ref[pl.ds(start, kb), :], v_ref[pl.ds(start, kb), :],
                       [qh_ref[hd] for hd in range(A_HEADS)], m_ref, acc_ref, s_ref, p_ref)
        return carry

    lax.fori_loop(0, n_blk, attn_blk, 0)
    for j in range(A_HEADS // 2):
        outs = [_softmax_result(acc_ref, 2 * j + e) for e in range(2)]
        o_ref[:, j * LANE:(j + 1) * LANE] = _pair_heads(outs[0], outs[1], (2 * j) // (A_HEADS // A_KV_HEADS)).astype(o_ref.dtype)


def _mixer_call(kern, o_all, m, *, grid, in_specs, out_block, out_index, args, scratch=(), sem, name):
    n_in = len(args)
    if o_all is None:
        fn, specs, alias, extra = kern, list(in_specs), {}, []
    else:
        fn = lambda *refs: kern(*refs[:n_in], *refs[n_in + 1:])
        specs, alias, extra = list(in_specs) + [pl.BlockSpec(memory_space=pl.ANY)], {n_in: 0}, [o_all]
    return pl.pallas_call(
        fn,
        out_shape=jax.ShapeDtypeStruct((m, N_BRANCH * BRANCH_WIDTH), BF16),
        grid=grid,
        in_specs=specs,
        out_specs=pl.BlockSpec(out_block, out_index),
        scratch_shapes=list(scratch),
        input_output_aliases=alias,
        compiler_params=_cparams(*sem),
        name=name,
    )(*args, *extra)


def _dsa_prompt(z, zb, t, o_all, m):
    tq, kb = 256, 512
    topk = min(A_TOPK_MAX, t // 4)
    full = lambda s: pl.BlockSpec((t, LANE), lambda i: (0, s))
    heads = lambda dt, w=LANE: pltpu.VMEM((A_HEADS, tq, w), dt)
    return _mixer_call(
        functools.partial(_dsa_prompt_kernel, tq=tq, kb=kb, topk=topk), o_all, m,
        grid=(t // tq,),
        in_specs=[pl.BlockSpec((tq, 4 * LANE), lambda i: (i, S_AQ // 4)),
                  pl.BlockSpec((tq, 2 * LANE), lambda i: (i, S_AIQ // 2)),
                  pl.BlockSpec((tq, LANE), lambda i: (i, S_AIK)),
                  full(S_AK), full(S_AV), pl.BlockSpec((t, 2 * LANE), lambda i: (0, 0))],
        out_block=(tq, BRANCH_WIDTH), out_index=lambda i: (i, 0),
        scratch=[pltpu.VMEM((tq, t), jnp.int32), heads(BF16),
                 pltpu.VMEM((A_IDX_HEADS, tq, 2 * LANE), BF16),
                 heads(F32), heads(F32, 2 * LANE), heads(F32, kb), heads(BF16, kb)],
        sem=("arbitrary",), name="dsa_prompt",
        args=(z, z, z, zb, zb, _indexer_keys3(z, t)))


def _feature_major(cache, n_feat):
    nd = cache.ndim
    perm = (0, 1) + tuple(range(3, nd)) + (2,)
    return jnp.transpose(cache, perm).reshape(cache.shape[0], cache.shape[1], n_feat, cache.shape[2])


def _dsa_sample_kernel(q_ref, iq_ref, new_ik_ref, new_k_ref, new_v_ref, ckt_ref, cvt_ref, cikt_ref, o_ref,
                       key_ref, qh_ref, *, ts, n_seq, past, topk, q_pos0):
    rows_all = n_seq * ts
    width = key_ref.shape[1]
    iq_all, iw_all = iq_ref[...], new_ik_ref[...]
    row_c = lax.broadcasted_iota(jnp.int32, (ts, past), 0)
    limit_c = (((q_pos0 + row_c) >> 6) + 1) << 6
    kpos_c = lax.broadcasted_iota(jnp.int32, (ts, past), 1)
    kpos_n = past + lax.broadcasted_iota(jnp.int32, (ts, ts), 1)
    limit_n = (((q_pos0 + lax.broadcasted_iota(jnp.int32, (ts, ts), 0)) >> 6) + 1) << 6
    key_ref[:, past:] = jnp.full((rows_all, width - past), NEG_KEY, jnp.int32)
    for g in range(n_seq):
        rows = slice(g * ts, (g + 1) * ts)
        iq, iw = iq_all[rows], iw_all[rows]
        ikt = cikt_ref[g]
        ik_new = iw[:, :HEAD_DIM]
        sc_c = sc_n = None
        for hd in range(A_IDX_HEADS):
            qh = iq[:, hd * HEAD_DIM:(hd + 1) * HEAD_DIM]
            wgt = iw[:, HEAD_DIM + hd:HEAD_DIM + hd + 1] * (A_IDX_HEADS ** -0.5 * HEAD_DIM ** -0.5)
            t_c = jnp.maximum(_dot3(qh, ikt), 0.0) * wgt
            t_n = jnp.maximum(_dot_nt3(qh, ik_new), 0.0) * wgt
            sc_c = t_c if sc_c is None else sc_c + t_c
            sc_n = t_n if sc_n is None else sc_n + t_n
        key_ref[rows, :past] = jnp.where(kpos_c < limit_c, _to_key(sc_c), NEG_KEY)
        key_ref[rows, past:past + ts] = jnp.where(kpos_n < limit_n, _to_key(sc_n), NEG_KEY)

    thr, cut = _topk_threshold(key_ref, width // LANE, LANE, topk)
    _dsa_query_heads(q_ref, qh_ref)
    for g in range(n_seq):
        rows = slice(g * ts, (g + 1) * ts)
        sel_c = _selected(key_ref[rows, :past], 0, thr[rows], cut[rows])
        sel_n = _selected(key_ref[rows, past:past + ts], past, thr[rows], cut[rows])
        kct, vct = ckt_ref[g].astype(BF16), cvt_ref[g].astype(BF16)
        kn, vn = new_k_ref[rows, :], new_v_ref[rows, :]
        outs = []
        for hd in range(A_HEADS):
            qh = qh_ref[hd, rows, :]
            s_c = jnp.where(sel_c, jnp.dot(qh, kct, preferred_element_type=F32), -jnp.inf)
            s_n = jnp.where(sel_n, _dot_nt(qh, kn), -jnp.inf)
            m = jnp.maximum(jnp.max(s_c, axis=-1, keepdims=True), jnp.max(s_n, axis=-1, keepdims=True))
            p_c, p_n = jnp.exp(s_c - m), jnp.exp(s_n - m)
            l = jnp.sum(p_c, axis=-1, keepdims=True) + jnp.sum(p_n, axis=-1, keepdims=True)
            o = _dot_nt(p_c.astype(BF16), vct) + jnp.dot(p_n.astype(BF16), vn, preferred_element_type=F32)
            outs.append(o / l)
        for j in range(A_HEADS // 2):
            o_ref[rows, j * LANE:(j + 1) * LANE] = _pair_heads(
                outs[2 * j], outs[2 * j + 1], (2 * j) // (A_HEADS // A_KV_HEADS)).astype(o_ref.dtype)


def _dsa_sample(z, zb, cache_k, cache_v, cache_ik, l, t, nb, ts, o_all, m):
    past = cache_k.shape[2]
    topk = min(A_TOPK_MAX, (past + ts) // 4)
    width = ((past + ts + LANE - 1) // LANE) * LANE
    n_seq = LANE // ts
    rows = n_seq * ts
    assert LANE % ts == 0 and nb % n_seq == 0 and t % rows == 0
    rb = t // rows
    new = lambda s: pl.BlockSpec((rows, LANE), lambda i: (rb + i, s))
    cache = lambda w: pl.BlockSpec((None, n_seq, w, past), lambda i: (l, i, 0, 0))
    return _mixer_call(
        functools.partial(_dsa_sample_kernel, ts=ts, n_seq=n_seq, past=past, topk=topk, q_pos0=past), o_all, m,
        grid=(nb // n_seq,),
        in_specs=[pl.BlockSpec((rows, 4 * LANE), lambda i: (rb + i, S_AQ // 4)),
                  pl.BlockSpec((rows, 2 * LANE), lambda i: (rb + i, S_AIQ // 2)),
                  new(S_AIK), new(S_AK), new(S_AV), cache(LANE), cache(LANE), cache(HEAD_DIM)],
        out_block=(rows, BRANCH_WIDTH), out_index=lambda i: (rb + i, 0),
        scratch=[pltpu.VMEM((rows, width), jnp.int32), pltpu.VMEM((A_HEADS, rows, LANE), BF16)],
        sem=("arbitrary",), name="dsa_sample",
        args=(z, z, z, zb, zb, _feature_major(cache_k, LANE), _feature_major(cache_v, LANE),
              _feature_major(cache_ik, HEAD_DIM)))


def _band_prompt_kernel(q_ref, k_ref, v_ref, bias0_ref, bias1_ref, o_ref, s_ref, p_ref, *, tq, win):
    i = pl.program_id(1)
    units = [(r, e) for r in range(2) for e in range(2)]
    biases = (bias0_ref, bias1_ref)
    w0s, oks = [], []
    for r in range(2):
        q0 = (2 * i + r) * tq
        w0 = pl.multiple_of(jnp.maximum(q0 - B_WINDOW, 0), tq)
        kc = (w0 + lax.broadcasted_iota(jnp.int32, (tq, win), 1)) >> 6
        qc = (q0 + lax.broadcasted_iota(jnp.int32, (tq, win), 0)) >> 6
        w0s.append(w0)
        oks.append((kc <= qc) & (kc >= qc - B_PAST_CHUNKS))
    for u, (r, e) in enumerate(units):
        qs = q_ref[r * tq:(r + 1) * tq, :] * (HEAD_DIM ** -0.5)
        qe = jnp.where(_half_mask(qs.shape, e), qs, 0.0).astype(BF16)
        s_ref[u] = _dot_nt(qe, k_ref[pl.ds(w0s[r], win), :])
    for u, (r, e) in enumerate(units):
        s = jnp.where(oks[r], s_ref[u] + biases[r][e], -jnp.inf)
        p_ref[u] = jnp.exp(s - jnp.max(s, axis=-1, keepdims=True)).astype(BF16)
    outs = []
    for u, (r, e) in enumerate(units):
        vw = v_ref[pl.ds(w0s[r], win), :]
        acc = jnp.dot(p_ref[u], jnp.concatenate([vw, jnp.ones_like(vw)], axis=1), preferred_element_type=F32)
        outs.append(acc[:, :LANE] / acc[:, LANE:])
    for r in range(2):
        o_ref[r * tq:(r + 1) * tq, :] = jnp.where(_half_mask(outs[0].shape, 0), outs[2 * r],
                                                   outs[2 * r + 1]).astype(o_ref.dtype)


def _band_prompt(z, zb, bias, t, tq, o_all, m):
    win = B_WINDOW + tq
    n_case = B_WINDOW // tq
    assert t % (2 * tq) == 0
    case = lambda r: pl.BlockSpec((None, 2, tq, win), lambda j, i: (jnp.minimum(2 * i + r, n_case), j, 0, 0))
    return _mixer_call(
        functools.partial(_band_prompt_kernel, tq=tq, win=win), o_all, m,
        grid=(B_HEADS // 2, t // (2 * tq)),
        in_specs=[pl.BlockSpec((2 * tq, LANE), lambda j, i: (i, S_BQ + j)),
                  pl.BlockSpec((t, LANE), lambda j, i: (0, S_BK + j)),
                  pl.BlockSpec((t, LANE), lambda j, i: (0, S_BV + j)),
                  case(0), case(1)],
        out_block=(2 * tq, LANE), out_index=lambda j, i: (i, BRANCH_WIDTH // LANE + j),
        scratch=[pltpu.VMEM((4, tq, win), F32), pltpu.VMEM((4, tq, win), BF16)],
        sem=("parallel", "arbitrary"), name="band_prompt", args=(z, zb, zb, bias, bias))


def _band_sample_kernel(z_ref, zb_ref, kct_ref, vct_ref, bc_ref, bn_ref, o_ref):
    for j in range(B_HEADS // 2):
        slot = lambda ref, s: ref[:, (s + j) * LANE:(s + j + 1) * LANE]
        feat = slice(j * LANE, (j + 1) * LANE)
        kct, vct = kct_ref[feat, :].astype(BF16), vct_ref[feat, :].astype(BF16)
        kn, vn = slot(zb_ref, S_BK), slot(zb_ref, S_BV)
        qs = slot(z_ref, S_BQ) * (HEAD_DIM ** -0.5)
        outs = []
        for e in range(2):
            qe = jnp.where(_half_mask(qs.shape, e), qs, 0.0).astype(BF16)
            s_c = jnp.dot(qe, kct, preferred_element_type=F32) + bc_ref[2 * j + e]
            s_n = _dot_nt(qe, kn) + bn_ref[2 * j + e]
            m = jnp.maximum(jnp.max(s_c, axis=-1, keepdims=True), jnp.max(s_n, axis=-1, keepdims=True))
            p_c, p_n = jnp.exp(s_c - m), jnp.exp(s_n - m)
            l = jnp.sum(p_c, axis=-1, keepdims=True) + jnp.sum(p_n, axis=-1, keepdims=True)
            o = _dot_nt(p_c.astype(BF16), vct) + jnp.dot(p_n.astype(BF16), vn, preferred_element_type=F32)
            outs.append(o / l)
        o_ref[:, feat] = jnp.where(_half_mask(outs[0].shape, 0), outs[0], outs[1]).astype(o_ref.dtype)


def _band_sample(z, zb, cache_k, cache_v, bias_c, bias_n, l, t, nb, ts, o_all, m):
    pb = cache_k.shape[2]
    rb = t // ts
    bw = BRANCH_WIDTH
    cache = pl.BlockSpec((None, None, bw, pb), lambda b: (l, b, 0, 0))
    full = lambda a: pl.BlockSpec(a.shape, lambda b: (0,) * a.ndim)
    return _mixer_call(
        _band_sample_kernel, o_all, m,
        grid=(nb,),
        in_specs=[pl.BlockSpec((ts, z.shape[1]), lambda b: (rb + b, 0)),
                  pl.BlockSpec((ts, z.shape[1]), lambda b: (rb + b, 0)),
                  cache, cache, full(bias_c), full(bias_n)],
        out_block=(ts, bw), out_index=lambda b: (rb + b, 1),
        sem=("parallel",), name="band_sample",
        args=(z, zb, _feature_major(cache_k, bw), _feature_major(cache_v, bw), bias_c, bias_n))


def _band_bias(table, tq):
    n_case = B_WINDOW // tq
    win = B_WINDOW + tq
    width = win + B_WINDOW
    j = np.arange(width + tq - 1) - (tq - 1) - B_WINDOW
    ext = table.astype(F32)[:, np.clip(j, -B_REL_CLIP, B_REL_CLIP) + B_REL_CLIP]
    n = ext.shape[1]
    x = jnp.roll(ext, -(tq - 1), axis=1)
    toe = jnp.tile(x, (1, tq))[:, :tq * (n - 1)].reshape(ext.shape[0], tq, n - 1)[:, :, :width]
    return jnp.stack([toe[:, :, B_WINDOW - c * tq:B_WINDOW - c * tq + win] for c in range(n_case + 1)], axis=0)


def _lambda(lam_ref, lam_init):
    lv = lam_ref[...]
    return (jnp.exp(jnp.sum(lv[0:1] * lv[1:2], axis=-1, keepdims=True))
            - jnp.exp(jnp.sum(lv[2:3] * lv[3:4], axis=-1, keepdims=True)) + lam_init)


def _diff_finish(o0, o1, lam, on_ref, lam_init):
    attn = o0 - lam * o1
    ms = jnp.mean(attn * attn, axis=-1, keepdims=True)
    return (attn * lax.rsqrt(ms + NORM_EPS) * on_ref[...]) * (1.0 - lam_init)


def _diff_prompt_kernel(q_ref, k_ref, v_ref, lam_ref, on_ref, o_ref, qh_ref, m_ref, acc_ref, s_ref, p_ref,
                        *, tq, kb, tg, lam_init):
    i = pl.program_id(1)
    q0 = i * tq
    n_grp = tq // tg
    n_blk = (q0 + tq + kb - 1) // kb
    row = lax.broadcasted_iota(jnp.int32, (tg, kb), 0)
    limits = [(((q0 + g * tg + row) >> 6) + 1) << 6 for g in range(n_grp)]
    for g in range(n_grp):
        qs = q_ref[g * tg:(g + 1) * tg, :] * (HEAD_DIM ** -0.5)
        for c in range(2):
            qh_ref[2 * g + c] = jnp.where(_half_mask(qs.shape, c), qs, 0.0).astype(BF16)
    _softmax_init(m_ref, acc_ref)
    n_full = (q0 + CHUNK) // kb

    def attn_blk(b, carry, masked):
        start = pl.multiple_of(b * kb, kb)
        ok = None
        if masked:
            kpos = start + lax.broadcasted_iota(jnp.int32, (tg, kb), 1)
            ok = [kpos < limits[u // 2] for u in range(2 * n_grp)]
        _softmax_block(ok, k_ref[pl.ds(start, kb), :], v_ref[pl.ds(start, kb), :],
                       [qh_ref[u] for u in range(2 * n_grp)], m_ref, acc_ref, s_ref, p_ref)
        return carry

    lax.fori_loop(0, n_full, functools.partial(attn_blk, masked=False), 0)
    lax.fori_loop(n_full, n_blk, functools.partial(attn_blk, masked=True), 0)
    lam = _lambda(lam_ref, lam_init)
    for g in range(n_grp):
        outs = [_softmax_result(acc_ref, 2 * g + c) for c in range(2)]
        o_ref[g * tg:(g + 1) * tg, :] = _diff_finish(outs[0], outs[1], lam, on_ref, lam_init).astype(o_ref.dtype)


def _diff_prompt(z, zb, c_lam_all, c_on_all, l, t, lam_init, o_all, m):
    tq, kb, tg = _tile(t, 512), 512, 128
    maps = lambda dt, w=LANE: pltpu.VMEM((2 * tq // tg, tg, w), dt)
    return _mixer_call(
        functools.partial(_diff_prompt_kernel, tq=tq, kb=kb, tg=tg, lam_init=lam_init), o_all, m,
        grid=(C_HEADS, t // tq),
        in_specs=[pl.BlockSpec((tq, LANE), lambda h, i: (i, S_CQ + h)),
                  pl.BlockSpec((t, LANE), lambda h, i: (0, S_CK + h)),
                  pl.BlockSpec((t, LANE), lambda h, i: (0, S_CV + h)),
                  pl.BlockSpec((None, 4, HEAD_DIM), lambda h, i: (l, 0, 0)),
                  _layer_vec(l, LANE)],
        out_block=(tq, LANE), out_index=lambda h, i: (i, 2 * BRANCH_WIDTH // LANE + h),
        scratch=[maps(BF16), maps(F32), maps(F32, 2 * LANE), maps(F32, kb), maps(BF16, kb)],
        sem=("parallel", "arbitrary"), name="diff_prompt",
        args=(z, zb, zb, c_lam_all, _vec3(c_on_all)))


def _diff_sample_kernel(z_ref, zb_ref, kct_ref, vc_ref, lam_ref, on_ref, o_ref, *, past, lam_init):
    lam = _lambda(lam_ref, lam_init)
    for h in range(C_HEADS):
        slot = lambda ref, s: ref[:, (s + h) * LANE:(s + h + 1) * LANE]
        feat = slice(h * LANE, (h + 1) * LANE)
        kct = kct_ref[feat, :].astype(BF16)
        vc = vc_ref[pl.ds(h, past, stride=C_HEADS), :].astype(BF16)
        kn, vn = slot(zb_ref, S_CK), slot(zb_ref, S_CV)
        qs = slot(z_ref, S_CQ) * (HEAD_DIM ** -0.5)
        outs = []
        for c in range(2):
            qc = jnp.where(_half_mask(qs.shape, c), qs, 0.0).astype(BF16)
            s_c = jnp.dot(qc, kct, preferred_element_type=F32)
            s_n = _dot_nt(qc, kn)
            m = jnp.maximum(jnp.max(s_c, axis=-1, keepdims=True), jnp.max(s_n, axis=-1, keepdims=True))
            p_c, p_n = jnp.exp(s_c - m), jnp.exp(s_n - m)
            l = jnp.sum(p_c, axis=-1, keepdims=True) + jnp.sum(p_n, axis=-1, keepdims=True)
            o = (jnp.dot(p_c.astype(BF16), vc, preferred_element_type=F32)
                 + jnp.dot(p_n.astype(BF16), vn, preferred_element_type=F32))
            outs.append(o / l)
        o_ref[:, feat] = _diff_finish(outs[0], outs[1], lam, on_ref, lam_init).astype(o_ref.dtype)


def _diff_sample(z, zb, cache_k, cache_v, c_lam_all, c_on_all, l, t, nb, ts, lam_init, o_all, m):
    depth, _, past = cache_k.shape[:3]
    rb = t // ts
    bw = BRANCH_WIDTH
    return _mixer_call(
        functools.partial(_diff_sample_kernel, past=past, lam_init=lam_init), o_all, m,
        grid=(nb,),
        in_specs=[pl.BlockSpec((ts, z.shape[1]), lambda b: (rb + b, 0)),
                  pl.BlockSpec((ts, z.shape[1]), lambda b: (rb + b, 0)),
                  pl.BlockSpec((None, None, bw, past), lambda b: (l, b, 0, 0)),
                  pl.BlockSpec((None, None, past * C_HEADS, LANE), lambda b: (l, b, 0, 0)),
                  pl.BlockSpec((None, 4, HEAD_DIM), lambda b: (l, 0, 0)),
                  _layer_vec(l, LANE)],
        out_block=(ts, bw), out_index=lambda b: (rb + b, 2),
        sem=("parallel",), name="diff_sample",
        args=(z, zb, _feature_major(cache_k, bw), cache_v.reshape(depth, nb, past * C_HEADS, LANE),
              c_lam_all, _vec3(c_on_all)))


def _rwkv_pre_kernel(zd_ref, prev_ref, first_ref, mu_ref, w0_ref, w2_ref, a0_ref, a2_ref, g2_ref, kkw_ref,
                     ka_ref, bd_ref, r_ref, w_ref, k_ref, v_ref, kk_ref, b_ref, g_ref, *, tm, t, ts):
    zf = zd_ref[...]
    grow = pl.program_id(0) * tm + lax.broadcasted_iota(jnp.int32, (tm, 1), 0)
    zs = jnp.where(grow % tm == 0, prev_ref[7:8, :], pltpu.roll(zf, 1, 0))
    seq_start = (grow >= t) & ((grow - t) % ts == 0)
    zs = jnp.where(seq_start, first_ref[...], zs)
    zs = jnp.where(grow == 0, 0.0, zs)
    zm = zf + (zs - zf) * mu_ref[...]
    bw = BRANCH_WIDTH
    r, k, v = zm[:, :bw], zm[:, bw:2 * bw], zm[:, 2 * bw:3 * bw]
    wl, al, gl = zm[:, 3 * bw:3 * bw + 64], zm[:, 3 * bw + 64:3 * bw + 128], zm[:, 3 * bw + 128:]
    dot = lambda a, b: jnp.dot(a.astype(BF16), b.astype(BF16), preferred_element_type=F32)
    u = -(w0_ref[...] + dot(jnp.tanh(wl), w2_ref[...]))
    softplus = jnp.maximum(u, 0.0) + jnp.log(1.0 + jnp.exp(-jnp.abs(u)))
    w = -softplus - 0.5
    a = jax.nn.sigmoid(a0_ref[...] + dot(al, a2_ref[...]))
    kk = k * kkw_ref[...]
    nrm = jnp.sqrt(_seg_sum(kk * kk, bd_ref[...]))
    kk = kk / jnp.maximum(nrm, 1e-12)
    r_ref[...] = r
    w_ref[...] = -jnp.exp(w)
    k_ref[...] = k * (1.0 + (a - 1.0) * ka_ref[...])
    v_ref[...] = v
    kk_ref[...] = kk
    b_ref[...] = kk * a
    g_ref[...] = dot(jax.nn.sigmoid(gl), g2_ref[...])


def _rwkv_pre(zd, zfirst, p, l, t, ts):
    m, dc = zd.shape
    tm = _tile(math.gcd(t, m - t), 512)
    bw = BRANCH_WIDTH
    npt = t // tm
    rows = pl.BlockSpec((tm, dc), lambda i: (i, 0))
    vec = lambda n: _layer_vec(l, n)
    mat = lambda k: pl.BlockSpec((None, k, bw), lambda i: (l, 0, 0))
    out = jax.ShapeDtypeStruct((m, bw), F32)
    return pl.pallas_call(
        functools.partial(_rwkv_pre_kernel, tm=tm, t=t, ts=ts),
        out_shape=[out] * 7,
        grid=(m // tm,),
        in_specs=[rows,
                  pl.BlockSpec((8, dc), lambda i: (jnp.maximum(i * (tm // 8) - 1, 0), 0)),
                  pl.BlockSpec((tm, dc), lambda i: (jnp.maximum(i - npt, 0), 0)),
                  vec(dc), vec(bw), mat(64), vec(bw), mat(64), mat(128), vec(bw), vec(bw),
                  pl.BlockSpec((bw, bw), lambda i: (0, 0))],
        out_specs=[pl.BlockSpec((tm, bw), lambda i: (i, 0))] * 7,
        compiler_params=_cparams("parallel"),
        name="rwkv_pre",
    )(zd, zd, zfirst, _vec3(p["d_mu"]), _vec3(p["d_w0"]), p["d_w2"], _vec3(p["d_a0"]), p["d_a2"], p["d_g2"],
      _vec3(p["d_k_k"]), _vec3(p["d_k_a"]), _ones_blockdiag(bw))


RW_CHUNK_MAX = 64
RW_BLOCK = 128
RW_SLOTS = BRANCH_WIDTH // LANE


def _dot3(a, b):
    ah, al = _split2(a)
    bh, bl = _split2(b)
    d = lambda x, y: jnp.dot(x, y, preferred_element_type=F32)
    return d(ah, bh) + d(al, bh) + d(ah, bl)


def _rwkv_chunk_kernel(*refs, seq_chunks, RW_CHUNK):
    if seq_chunks:
        (r_ref, lw_ref, k_ref, v_ref, kk_ref, b_ref, s0_ref, y_ref, sf_ref,
         h_ref, u_ref, ab_ref, rb_ref, u0_ref, y0_ref, bt_ref, kt_ref, eg_ref) = refs
    else:
        (r_ref, lw_ref, k_ref, v_ref, kk_ref, b_ref, y_ref, sf_ref,
         h_ref, u_ref, ab_ref, rb_ref, u0_ref, y0_ref, bt_ref, kt_ref, eg_ref) = refs
        s0_ref = None

        @pl.when(pl.program_id(0) == 0)
        def _():
            h_ref[...] = jnp.zeros_like(h_ref)

    n = RW_BLOCK
    n_chunks = n // RW_CHUNK
    row = lax.broadcasted_iota(jnp.int32, (n, n), 0)
    col = lax.broadcasted_iota(jnp.int32, (n, n), 1)
    same = (row // RW_CHUNK) == (col // RW_CHUNK)
    strict, incl = same & (col < row), same & (col <= row)
    eye = jnp.where(row == col, 1.0, 0.0)
    head_diag = (row // HEAD_DIM) == (col // HEAD_DIM)
    in_chunk = row % RW_CHUNK
    k_pick = jnp.where(lax.broadcasted_iota(jnp.int32, (n, HEAD_DIM), 0) % HEAD_DIM
                       == lax.broadcasted_iota(jnp.int32, (n, HEAD_DIM), 1), 1.0, 0.0)
    bf = lambda x: x.astype(BF16)
    mm = lambda x, y: jnp.dot(bf(x), bf(y), preferred_element_type=F32)
    u_ref[...] = jnp.zeros_like(u_ref)

    halves = [_half_mask((n, LANE), e) for e in range(2)]
    group = 2
    for j0 in range(0, RW_SLOTS, group):
        slot_data = []
        for j in range(j0, j0 + group):
            sl = slice(j * LANE, (j + 1) * LANE)
            lw, v = lw_ref[:, sl], v_ref[:, sl]
            g = lw
            for d in [1 << s for s in range(RW_CHUNK.bit_length() - 1)]:
                g = g + jnp.where(in_chunk >= d, pltpu.roll(g, d, 0), 0.0)
            inv_g = jnp.exp(-g)
            a_t = -kk_ref[:, sl] * jnp.exp(g - lw)
            b_t, k_t = b_ref[:, sl] * inv_g, k_ref[:, sl] * inv_g
            r_t = r_ref[:, sl] * jnp.exp(g)
            bt_ref[j], kt_ref[j], eg_ref[j] = b_t.T, k_t.T, jnp.exp(g).T
            slot_data.append((a_t, r_t, bf(b_t), bf(k_t), bf(v)))
        chains = [(s, e) for s in range(group) for e in range(2)]
        each = lambda f: [f(s, e, i) for i, (s, e) in enumerate(chains)]
        a_e = each(lambda s, e, i: bf(jnp.where(halves[e], slot_data[s][0], 0.0)))
        r_e = each(lambda s, e, i: bf(jnp.where(halves[e], slot_data[s][1], 0.0)))
        n_ab = each(lambda s, e, i: jnp.where(strict, _dot_nt(a_e[i], slot_data[s][2]), 0.0))
        n_ak = each(lambda s, e, i: jnp.where(strict, _dot_nt(a_e[i], slot_data[s][3]), 0.0))
        m_rb = each(lambda s, e, i: bf(jnp.where(incl, _dot_nt(r_e[i], slot_data[s][2]), 0.0)))
        m_rk = each(lambda s, e, i: bf(jnp.where(incl, _dot_nt(r_e[i], slot_data[s][3]), 0.0)))
        w_e = each(lambda s, e, i: mm(n_ak[i], slot_data[s][4]))
        tinv = each(lambda s, e, i: eye + n_ab[i])
        pw = n_ab
        for _ in range(RW_CHUNK.bit_length() - 2):
            pw = each(lambda s, e, i: mm(pw[i], pw[i]))
            tinv = each(lambda s, e, i: tinv[i] + mm(tinv[i], pw[i]))
        t16 = each(lambda s, e, i: bf(tinv[i]))
        a_bar = each(lambda s, e, i: mm(t16[i], slot_data[s][0]))
        u0 = each(lambda s, e, i: mm(t16[i], w_e[i]))
        r_bar = each(lambda s, e, i: slot_data[s][1] + mm(m_rb[i], a_bar[i]))
        y0 = each(lambda s, e, i: mm(m_rb[i], u0[i]) + mm(m_rk[i], slot_data[s][4]))
        for s in range(group):
            pick = lambda vals: jnp.where(halves[0], vals[2 * s], vals[2 * s + 1])
            j = j0 + s
            ab_ref[j], rb_ref[j], u0_ref[j], y0_ref[j] = pick(a_bar), pick(r_bar), pick(u0), pick(y0)

    for c in range(n_chunks):
        rows = slice(c * RW_CHUNK, (c + 1) * RW_CHUNK)
        col_c = (col // RW_CHUNK) == c
        for j in range(RW_SLOTS):
            sl = slice(j * LANE, (j + 1) * LANE)
            if seq_chunks and c % seq_chunks == 0:
                x = s0_ref[c // seq_chunks, sl, :]
                h = jnp.where(head_diag, _dot_nt3(k_pick, x), 0.0)
            else:
                h = h_ref[j]
            res = _dot3(jnp.concatenate([ab_ref[j, rows, :], rb_ref[j, rows, :]], axis=0), h)
            u_c = res[:RW_CHUNK] + u0_ref[j, rows, :]
            y_ref[rows, sl] = res[RW_CHUNK:] + y0_ref[j, rows, :]
            u_ref[j, rows, :] = u_c
            bk = jnp.concatenate([jnp.where(col_c, bt_ref[j], 0.0), jnp.where(col_c, kt_ref[j], 0.0)], axis=1)
            uv = jnp.concatenate([u_ref[j], v_ref[:, sl]], axis=0)
            inc = jnp.where(head_diag, mm(bk, uv), 0.0)
            g_end = eg_ref[j, :, (c + 1) * RW_CHUNK - 1:(c + 1) * RW_CHUNK]
            h = g_end * (h + inc)
            h_ref[j] = h
            if seq_chunks and (c + 1) % seq_chunks == 0:
                ht = h.T
                sf_ref[c // seq_chunks, sl, :] = (ht + pltpu.roll(ht, HEAD_DIM, 1))[:, :HEAD_DIM]

    if not seq_chunks:
        @pl.when(pl.program_id(0) == pl.num_programs(0) - 1)
        def _():
            for j in range(RW_SLOTS):
                ht = h_ref[j].T
                sf_ref[0, j * LANE:(j + 1) * LANE, :] = (ht + pltpu.roll(ht, HEAD_DIM, 1))[:, :HEAD_DIM]


def _rwkv_chunked(ops, s0, row0, n_seq, t):
    n = RW_BLOCK
    bw = BRANCH_WIDTH
    chunk = math.gcd(t, RW_CHUNK_MAX)
    assert row0 % n == 0 and (n_seq * t) % n == 0 and chunk >= 8 and chunk & (chunk - 1) == 0
    rb = row0 // n
    rows = pl.BlockSpec((n, bw), lambda i: (rb + i, 0))
    if s0 is None:
        assert n_seq == 1
        seq_chunks, per_blk, extra, extra_specs = 0, 1, [], []
        sf_spec = pl.BlockSpec((1, bw, HEAD_DIM), lambda i: (0, 0, 0))
    else:
        assert n % t == 0
        seq_chunks, per_blk = t // chunk, n // t
        extra = [s0.reshape(n_seq, bw, HEAD_DIM)]
        extra_specs = [pl.BlockSpec((per_blk, bw, HEAD_DIM), lambda i: (i, 0, 0))]
        sf_spec = pl.BlockSpec((per_blk, bw, HEAD_DIM), lambda i: (i, 0, 0))
    slot = lambda dt=F32: pltpu.VMEM((RW_SLOTS, n, LANE), dt)
    y, sf = pl.pallas_call(
        functools.partial(_rwkv_chunk_kernel, seq_chunks=seq_chunks, RW_CHUNK=chunk),
        out_shape=[jax.ShapeDtypeStruct((n_seq * t, bw), F32), jax.ShapeDtypeStruct((n_seq, bw, HEAD_DIM), F32)],
        grid=(n_seq * t // n,),
        in_specs=[rows] * 6 + extra_specs,
        out_specs=[pl.BlockSpec((n, bw), lambda i: (i, 0)), sf_spec],
        scratch_shapes=[slot() for _ in range(9)],
        compiler_params=_cparams("arbitrary"),
        name="rwkv_chunked",
    )(*ops, *extra)
    return y, sf.reshape(n_seq, D_HEADS, HEAD_DIM, HEAD_DIM)


def _rwkv_post_kernel(y_ref, r_ref, k_ref, v_ref, g_ref, lnw_ref, lnb_ref, rk_ref, bd_ref, o_ref):
    bd = bd_ref[...]
    y = y_ref[...]
    mean = _seg_sum(y, bd) * (1.0 / HEAD_DIM)
    yc = y - mean
    var = _seg_sum(yc * yc, bd) * (1.0 / HEAD_DIM)
    yn = yc * lax.rsqrt(var + D_GN_EPS) * lnw_ref[...] + lnb_ref[...]
    bonus = _seg_sum(r_ref[...] * k_ref[...] * rk_ref[...], bd) * v_ref[...]
    o_ref[...] = ((yn + bonus) * g_ref[...]).astype(o_ref.dtype)


def _rwkv_post(y, r, k, v, g, p, l, o_all):
    m, bw = y.shape
    tm = _tile(m, 512)
    rows = pl.BlockSpec((tm, bw), lambda i: (i, 0))
    vec = _layer_vec(l, bw)
    return _mixer_call(
        _rwkv_post_kernel, o_all, m,
        grid=(m // tm,),
        in_specs=[rows] * 5 + [vec] * 3 + [pl.BlockSpec((bw, bw), lambda i: (0, 0))],
        out_block=(tm, bw), out_index=lambda i: (i, N_BRANCH - 1),
        sem=("parallel",), name="rwkv_post",
        args=(y, r, k, v, g, _vec3(p["d_ln_w"]), _vec3(p["d_ln_b"]), _vec3(p["d_r_k"]), _ones_blockdiag(bw)))


def _merge_kernel(o_ref, wbr_ref, h_ref, wg_ref, out_ref, acc_ref):
    n = pl.program_id(2)

    @pl.when(n == 0)
    def _():
        acc_ref[...] = jnp.zeros_like(acc_ref)

    u = jnp.dot(o_ref[...], wbr_ref[...].astype(BF16), preferred_element_type=F32)
    gate = jax.nn.sigmoid(jnp.dot(h_ref[...], wg_ref[...], preferred_element_type=F32))
    acc_ref[...] += u * gate

    @pl.when(n == N_BRANCH - 1)
    def _():
        out_ref[...] = acc_ref[...].astype(out_ref.dtype)


def _merge(o_all, w_br_all, h, w_pack, l):
    m, d = h.shape
    tm, tn = _tile(m, 2304), 512
    nj = d // tn
    g0 = S_GATE * LANE // tn
    return pl.pallas_call(
        _merge_kernel,
        out_shape=jax.ShapeDtypeStruct((m, d), BF16),
        grid=(m // tm, nj, N_BRANCH),
        in_specs=[pl.BlockSpec((tm, BRANCH_WIDTH), lambda i, j, n: (i, n)),
                  pl.BlockSpec((None, None, BRANCH_WIDTH, tn), lambda i, j, n: (l, n, 0, j)),
                  pl.BlockSpec((tm, d), lambda i, j, n: (i, 0)),
                  pl.BlockSpec((None, d, tn), lambda i, j, n: (l, 0, g0 + n * nj + j))],
        out_specs=pl.BlockSpec((tm, tn), lambda i, j, n: (i, j)),
        scratch_shapes=[pltpu.VMEM((tm, tn), F32)],
        compiler_params=_cparams("parallel", "parallel", "arbitrary"),
        name="branch_merge",
    )(o_all, w_br_all, h, w_pack)


def _ple_kernel(x_ref, pe_ref, wple_ref, h_ref, wg_ref, o_ref):
    emb = jnp.dot(pe_ref[...].astype(BF16), wple_ref[...].astype(BF16), preferred_element_type=F32)
    gate = jax.nn.sigmoid(jnp.dot(h_ref[...], wg_ref[...].astype(BF16), preferred_element_type=F32))
    o_ref[...] = x_ref[...] + emb * gate


def _ple(x, pe, w_ple_all, h, w_gate_all, l):
    m, d = x.shape
    pd = pe.shape[1]
    tm, tn = _tile(m, 1536), 512
    return pl.pallas_call(
        _ple_kernel,
        out_shape=jax.ShapeDtypeStruct((m, d), F32),
        grid=(m // tm, d // tn),
        in_specs=[pl.BlockSpec((tm, tn), lambda i, j: (i, j)),
                  pl.BlockSpec((tm, pd), lambda i, j: (i, 0)),
                  pl.BlockSpec((None, pd, tn), lambda i, j: (l, 0, j)),
                  pl.BlockSpec((tm, d), lambda i, j: (i, 0)),
                  pl.BlockSpec((None, d, tn), lambda i, j: (l, 0, j))],
        out_specs=pl.BlockSpec((tm, tn), lambda i, j: (i, j)),
        compiler_params=_cparams("parallel", "parallel"),
        name="ple",
    )(x, pe, w_ple_all, h, w_gate_all)


def _layer(l, x, pe, p, t, nb, ts, tabs):
    m = x.shape[0]
    lam_init = 0.8 - 0.6 * math.exp(-0.3 * l)
    dc = p["d_mu"].shape[1]
    d = x.shape[1]
    lw = {"a_qn": p["a_q_norm"][l], "a_kn": p["a_k_norm"][l], "b_qn": p["b_q_norm"][l], "b_kn": p["b_k_norm"][l],
          "c_qn": p["c_q_norm"][l], "c_kn": p["c_k_norm"][l]}

    h = _rmsnorm(x, p["norm1_g"], l)
    w_pack = p["w_pack"]
    z, zb = _qkv_projection(h, w_pack, l, *_column_vectors(lw), *tabs)
    zd = _matmul(h, w_pack, lambda acc: acc, F32, n=dc, layer=l, col0=S_D * LANE, tn=256, name="rwkv_projection")

    o_all = _dsa_prompt(z, zb, t, None, m)
    o_all = _dsa_sample(z, zb, p["cache_a_k"], p["cache_a_v"], p["cache_a_kidx"], l, t, nb, ts, o_all, m)
    tq_b = 128
    bias = _band_bias(p["b_rel_bias"][l], tq_b)
    pb = p["cache_b_k"].shape[2]
    generic = bias[B_WINDOW // tq_b]
    o_all = _band_prompt(z, zb, bias, t, tq_b, o_all, m)
    o_all = _band_sample(z, zb, p["cache_b_k"], p["cache_b_v"], generic[:, :ts, B_WINDOW - pb:B_WINDOW],
                         generic[:, :ts, B_WINDOW:B_WINDOW + ts], l, t, nb, ts, o_all, m)
    c_on = p["c_out_norm"]
    o_all = _diff_prompt(z, zb, p["c_lambda"], c_on, l, t, lam_init, o_all, m)
    o_all = _diff_sample(z, zb, p["cache_c_k"], p["cache_c_v"], p["c_lambda"], c_on, l, t, nb, ts, lam_init,
                         o_all, m)

    zfirst = jnp.broadcast_to(p["state_d_shift"][l], (nb, ts, dc)).reshape(nb * ts, dc)
    r, w, k, v, kk, b, g = _rwkv_pre(zd, zfirst, p, l, t, ts)
    ops = (r, w, k, v, kk, b)
    y_p, wkv_p = _rwkv_chunked(ops, None, 0, 1, t)
    y_s, wkv_s = _rwkv_chunked(ops, p["state_d_wkv"][l], t, nb, ts)
    o_all = _rwkv_post(jnp.concatenate([y_p, y_s], axis=0), r, k, v, g, p, l, o_all)

    ug = _merge(o_all, p["w_branch"], h, w_pack, l)
    res = lambda acc, r_: r_ + acc
    x = _matmul(ug, p["w_out"], res, F32, n=d, layer=l, residual=x, name="out_proj")
    h2 = _rmsnorm(x, p["norm2_g"], l)
    ffn = p["w_up"].shape[2]
    up = _matmul(h2, p["w_up"], lambda acc: jnp.square(jnp.maximum(acc, 0.0)), BF16, n=ffn, layer=l, tm=2304,
                 name="mlp_up")
    x = _matmul(up, _cast_bf16(p["w_down"], l), res, F32, n=d, residual=x, tm=1024, tn=1024, name="mlp_down")
    h3 = _rmsnorm(x, p["norm3_g"], l)
    x = _ple(x, pe, p["w_ple"], h3, p["w_ple_gate"], l)

    slot = lambda rows, s, n=1: z[rows, s * LANE:(s + n) * LANE]

    def rows_of(sl, lead):
        a = lambda s, n, shape: slot(sl, s, n).reshape(lead + shape)
        ak = a(S_AK, 1, (A_KV_HEADS, HEAD_DIM))
        av = a(S_AV, 1, (A_KV_HEADS, HEAD_DIM))
        aik = slot(sl, S_AIK)[:, :HEAD_DIM].reshape(lead + (HEAD_DIM,))
        bk = a(S_BK, 4, (B_HEADS, HEAD_DIM))
        bv = a(S_BV, 4, (B_HEADS, HEAD_DIM))
        ck = a(S_CK, 4, (C_HEADS, 2, HEAD_DIM))
        cv = a(S_CV, 4, (C_HEADS, 2 * HEAD_DIM))
        return ak, av, aik, bk, bv, ck, cv

    keep = min(B_WINDOW, t)
    pak, pav, paik, pbk, pbv, pck, pcv = rows_of(slice(0, t), (1, t))
    new_p = (pak, pav, paik, pbk[:, t - keep:], pbv[:, t - keep:], pck, pcv, wkv_p, zd[t - 1:t].reshape(1, 1, dc))
    new_s = rows_of(slice(t, m), (nb, ts)) + (wkv_s, zd[t:].reshape(nb, ts, dc)[:, -1:])
    return x, new_p, new_s


def kernel(x_prompt, x_sample, cache_a_k, cache_a_v, cache_a_kidx, cache_b_k, cache_b_v, cache_c_k, cache_c_v, state_d_wkv, state_d_shift, p_prompt, p_sample, norm1_g, w_in, a_q_norm, a_k_norm, b_q_norm, b_k_norm, b_rel_bias, c_q_norm, c_k_norm, c_lambda, c_out_norm, d_mu, d_w0, d_w2, d_a0, d_a2, d_g2, d_k_k, d_k_a, d_r_k, d_ln_w, d_ln_b, w_branch, w_out, norm2_g, w_up, w_down, norm3_g, w_ple, w_ple_gate):
    batch, t, d = x_prompt.shape
    nb, ts, _ = x_sample.shape
    past = cache_a_k.shape[2]
    depth = w_in.shape[0]
    assert batch == 1 and t % 512 == 0 and past % CHUNK == 0 and ts <= CHUNK and (nb * ts) % 8 == 0
    p = dict(cache_a_k=cache_a_k, cache_a_v=cache_a_v, cache_a_kidx=cache_a_kidx, cache_b_k=cache_b_k,
             cache_b_v=cache_b_v, cache_c_k=cache_c_k, cache_c_v=cache_c_v, state_d_wkv=state_d_wkv,
             state_d_shift=state_d_shift, norm1_g=norm1_g, w_in=w_in, a_q_norm=a_q_norm, a_k_norm=a_k_norm,
             b_q_norm=b_q_norm, b_k_norm=b_k_norm, b_rel_bias=b_rel_bias, c_q_norm=c_q_norm, c_k_norm=c_k_norm,
             c_lambda=c_lambda, c_out_norm=c_out_norm, d_mu=d_mu, d_w0=d_w0, d_w2=d_w2, d_a0=d_a0, d_a2=d_a2,
             d_g2=d_g2, d_k_k=d_k_k, d_k_a=d_k_a, d_r_k=d_r_k, d_ln_w=d_ln_w, d_ln_b=d_ln_b, w_branch=w_branch,
             w_out=w_out, norm2_g=norm2_g, w_up=w_up, w_down=w_down, norm3_g=norm3_g, w_ple=w_ple,
             w_ple_gate=w_ple_gate)
    p["w_pack"] = _pack_w_in(w_in)
    x = jnp.concatenate([x_prompt[0], x_sample.reshape(nb * ts, d)], axis=0)
    pos = jnp.concatenate([jnp.arange(t, dtype=jnp.int32),
                           jnp.tile(past + jnp.arange(ts, dtype=jnp.int32), nb)])
    tabs = _rope_tables(pos)
    st_p = [[] for _ in range(9)]
    st_s = [[] for _ in range(9)]
    for l in range(depth):
        pe = jnp.concatenate([p_prompt[l, 0], p_sample[l].reshape(nb * ts, -1)], axis=0)
        x, new_p, new_s = _layer(l, x, pe, p, t, nb, ts, tabs)
        for lst, arr in zip(st_p, new_p):
            lst.append(arr)
        for lst, arr in zip(st_s, new_s):
            lst.append(arr)
    outs_p = [jnp.stack(s, axis=0) for s in st_p]
    outs_s = [jnp.stack(s, axis=0) for s in st_s]
    return (x[:t].reshape(1, t, d), x[t:].reshape(nb, ts, d), *outs_p, *outs_s)
```

```python
import functools
import math

import numpy as np
import jax
import jax.numpy as jnp
from jax import lax
from jax.experimental import pallas as pl
from jax.experimental.pallas import tpu as pltpu

F32 = jnp.float32
BF16 = jnp.bfloat16

CHUNK = 64
HEAD_DIM = 64
ROPE_DIM = 16
ROPE_THETA = 500000.0
N_BRANCH = 4
BRANCH_WIDTH = 512
A_HEADS, A_KV_HEADS, A_IDX_HEADS = 8, 2, 4
A_TOPK_MAX = 256
B_HEADS = 8
B_PAST_CHUNKS = 8
B_WINDOW = B_PAST_CHUNKS * CHUNK
B_REL_CLIP = 128
C_HEADS = 4
D_HEADS = 8
D_GN_EPS = 64e-5
NORM_EPS = 1e-6

LANE = 128
VMEM_LIMIT = 48 * 1024 * 1024

S_AQ, S_AK, S_AV, S_AIQ, S_AIK = 0, 4, 5, 6, 8
S_BQ, S_BK, S_BV = 10, 14, 18
S_CQ, S_CK, S_CV = 22, 26, 30
S_D, S_GATE = 34, 48
N_QKV_SLOTS = 34
N_D_SLOTS = 14
A_COLS = 1092
A_SLOTS = 9
SHIFT = A_COLS - (A_SLOTS - 1) * LANE

NEG_KEY = -2139095041
INT_MIN = -2147483648
M_FLOOR = -1e30


def _cparams(*sem):
    return pltpu.CompilerParams(dimension_semantics=sem, vmem_limit_bytes=VMEM_LIMIT)


def _tile(n, pref):
    t = min(n, pref)
    while n % t:
        t -= 8
    return t


def _split2(x):
    hi = x.astype(BF16)
    lo = (x - hi.astype(F32)).astype(BF16)
    return hi, lo


def _seg_sum(x, ones_bd):
    hi, lo = _split2(x)
    return (jnp.dot(hi, ones_bd, preferred_element_type=F32)
            + jnp.dot(lo, ones_bd, preferred_element_type=F32))


def _dot_nt(a, b):
    return lax.dot_general(a, b, (((1,), (1,)), ((), ())), preferred_element_type=F32)


def _dot_nt3(a, b):
    ah, al = _split2(a)
    bh, bl = _split2(b)
    return _dot_nt(ah, bh) + _dot_nt(ah, bl) + _dot_nt(al, bh)


def _ones_blockdiag(n):
    i = np.arange(n)
    return jnp.asarray((i[:, None] // HEAD_DIM) == (i[None, :] // HEAD_DIM), dtype=BF16)


def _vec3(a):
    return a.reshape(a.shape[0], 1, -1)


def _layer_vec(l, n):
    return pl.BlockSpec((None, 1, n), lambda *_: (l, 0, 0))


def _half_mask(shape, half):
    lane = lax.broadcasted_iota(jnp.int32, shape, len(shape) - 1)
    return (lane < HEAD_DIM) if half == 0 else (lane >= HEAD_DIM)


def _rms_kernel(x_ref, g_ref, o_ref):
    x = x_ref[...]
    ms = jnp.mean(x * x, axis=-1, keepdims=True)
    o_ref[...] = (x * lax.rsqrt(ms + NORM_EPS) * g_ref[...]).astype(o_ref.dtype)


def _rmsnorm(x, g_all, l):
    m, d = x.shape
    tm = _tile(m, 512)
    return pl.pallas_call(
        _rms_kernel,
        out_shape=jax.ShapeDtypeStruct((m, d), BF16),
        grid=(m // tm,),
        in_specs=[pl.BlockSpec((tm, d), lambda i: (i, 0)), _layer_vec(l, d)],
        out_specs=pl.BlockSpec((tm, d), lambda i: (i, 0)),
        compiler_params=_cparams("parallel"),
        name="rmsnorm",
    )(x, _vec3(g_all))


def _pack_kernel(a_ref, b_ref, o_ref):
    j = pl.program_id(0)
    row = lax.broadcasted_iota(jnp.int32, (LANE, a_ref.shape[2]), 0)
    for l in range(a_ref.shape[1]):
        a, b = a_ref[:, l, :], b_ref[:, l, :]
        shifted = jnp.where(row < LANE - SHIFT, pltpu.roll(a, LANE - SHIFT, 0), pltpu.roll(b, LANE - SHIFT, 0))
        out = jnp.where(j < A_SLOTS, a, jnp.where(j == A_SLOTS, 0.0, shifted))
        o_ref[l] = out.T.astype(o_ref.dtype)


def _pack_w_in(w_in_all):
    depth, d, n_in = w_in_all.shape
    n_slots = A_SLOTS + 1 + (n_in - A_COLS) // LANE
    assert (n_in - A_COLS) % LANE == 0 and 0 < SHIFT < LANE
    src_a = lambda j: jnp.where(j < A_SLOTS, j, j - 2)
    src_b = lambda j: jnp.where(j < A_SLOTS, j, j - 1)
    w_t = jnp.transpose(w_in_all, (2, 0, 1))
    return pl.pallas_call(
        _pack_kernel,
        out_shape=jax.ShapeDtypeStruct((depth, d, n_slots * LANE), BF16),
        grid=(n_slots,),
        in_specs=[pl.BlockSpec((LANE, depth, d), lambda j: (src_a(j), 0, 0)),
                  pl.BlockSpec((LANE, depth, d), lambda j: (src_b(j), 0, 0))],
        out_specs=pl.BlockSpec((depth, d, LANE), lambda j: (0, 0, j)),
        compiler_params=_cparams("parallel"),
        name="pack_w_in",
    )(w_t, w_t)


def _mm_kernel(*refs, n_extra, nk, epilogue):
    a_ref, w_ref = refs[0], refs[1]
    extra = refs[2:2 + n_extra]
    o_ref = refs[2 + n_extra]
    if nk == 1:
        acc = jnp.dot(a_ref[...], w_ref[...].astype(BF16), preferred_element_type=F32)
        o_ref[...] = epilogue(acc, *[e[...] for e in extra]).astype(o_ref.dtype)
        return
    acc_ref = refs[3 + n_extra]
    k = pl.program_id(2)

    @pl.when(k == 0)
    def _():
        acc_ref[...] = jnp.zeros_like(acc_ref)

    acc_ref[...] += jnp.dot(a_ref[...], w_ref[...].astype(BF16), preferred_element_type=F32)

    @pl.when(k == nk - 1)
    def _():
        o_ref[...] = epilogue(acc_ref[...], *[e[...] for e in extra]).astype(o_ref.dtype)


def _matmul(a, w, epilogue, out_dtype, *, n, layer=None, col0=0, residual=None,
            tm=1536, tn=512, tk=2048, name="matmul"):
    m, kdim = a.shape
    tm, tn, tk = _tile(m, tm), _tile(n, tn), _tile(kdim, tk)
    nk = kdim // tk
    cb = col0 // tn
    assert col0 % tn == 0
    if layer is None:
        w_spec = pl.BlockSpec((tk, tn), lambda i, j, k: (k, cb + j))
    else:
        w_spec = pl.BlockSpec((None, tk, tn), lambda i, j, k: (layer, k, cb + j))
    specs = [pl.BlockSpec((tm, tk), lambda i, j, k: (i, k)), w_spec]
    extras = []
    if residual is not None:
        specs.append(pl.BlockSpec((tm, tn), lambda i, j, k: (i, j)))
        extras.append(residual)
    return pl.pallas_call(
        functools.partial(_mm_kernel, n_extra=len(extras), nk=nk, epilogue=epilogue),
        out_shape=jax.ShapeDtypeStruct((m, n), out_dtype),
        grid=(m // tm, n // tn, nk),
        in_specs=specs,
        out_specs=pl.BlockSpec((tm, tn), lambda i, j, k: (i, j)),
        scratch_shapes=[pltpu.VMEM((tm, tn), F32)] if nk > 1 else [],
        compiler_params=_cparams("parallel", "parallel", "arbitrary"),
        name=name,
    )(a, w, *extras)


def _cast_kernel(w_ref, o_ref):
    o_ref[...] = w_ref[...].astype(o_ref.dtype)


def _cast_bf16(w_all, l):
    _, kdim, n = w_all.shape
    tk = _tile(kdim, 512)
    return pl.pallas_call(
        _cast_kernel,
        out_shape=jax.ShapeDtypeStruct((kdim, n), BF16),
        grid=(kdim // tk,),
        in_specs=[pl.BlockSpec((None, tk, n), lambda i: (l, i, 0))],
        out_specs=pl.BlockSpec((tk, n), lambda i: (i, 0)),
        compiler_params=_cparams("parallel"),
        name="cast_bf16",
    )(w_all)


def _proj_kernel(h_ref, w_ref, gain_ref, nf_ref, rf_ref, cos_ref, sa_ref, sb_ref, bd_ref, o_ref, ob_ref,
                 xa_ref, xb_ref, *, sub, plain_tiles, norm_tiles):
    bd = bd_ref[...]
    j = pl.program_id(1)

    def matmul(dst_ref):
        dst_ref[...] = jnp.dot(h_ref[...], w_ref[...], preferred_element_type=F32)

    def epilogue(dst_ref, src_ref, with_norm, with_rope):
        matmul(dst_ref)
        normed = nf_ref[...] > 0.5
        gain, rf = gain_ref[...], rf_ref[...]
        for c in range(h_ref.shape[0] // sub):
            rows = slice(c * sub, (c + 1) * sub)
            y = src_ref[rows, :]
            if with_norm:
                ms = _seg_sum(y * y, bd) * (1.0 / HEAD_DIM)
                y = y * jnp.where(normed, lax.rsqrt(ms + NORM_EPS) * gain, 1.0)
            if with_rope:
                cosv, sav, sbv = cos_ref[rows, :], sa_ref[rows, :], sb_ref[rows, :]
            for s in range(2):
                out = y[:, s * LANE:(s + 1) * LANE]
                if with_rope:
                    f = rf[:, s * LANE:(s + 1) * LANE]
                    roped = out * cosv + pltpu.roll(out, 8, 1) * sav + pltpu.roll(out, LANE - 8, 1) * sbv
                    out = jnp.where(f > 0.5, roped, out)
                o_ref[rows, s * LANE:(s + 1) * LANE] = out
                ob_ref[rows, s * LANE:(s + 1) * LANE] = out.astype(BF16)

    any_of = lambda tiles: functools.reduce(jnp.logical_or, [j - 1 == tl for tl in tiles])
    is_first, is_plain, is_norm = j == 0, any_of(plain_tiles), any_of(norm_tiles)
    is_full = jnp.logical_not(is_first | is_plain | is_norm)
    pl.when(is_first)(lambda: matmul(xa_ref))
    for parity, (dst, src) in enumerate(((xa_ref, xb_ref), (xb_ref, xa_ref))):
        here = (j % 2) == parity
        pl.when(here & is_plain)(functools.partial(epilogue, dst, src, False, False))
        pl.when(here & is_norm)(functools.partial(epilogue, dst, src, True, False))
        pl.when(here & is_full)(functools.partial(epilogue, dst, src, True, True))


def _qkv_projection(h, w_pack, l, gain, nf, rf, cos_t, sin_a, sin_b):
    m, d = h.shape
    tm, tn = _tile(m, 2304), 2 * LANE
    zw = N_QKV_SLOTS * LANE
    n_tiles = zw // tn
    prev = lambda j: jnp.maximum(j - 1, 0)
    row = lambda i, j: (0, prev(j))
    tab = lambda i, j: (i, 0)
    no_rope = set(range(S_BQ, S_CQ)) | {S_AV, S_AIK + 1} | set(range(S_CV, N_QKV_SLOTS))
    no_norm = set(range(S_AV, S_BQ)) | set(range(S_BV, S_CQ)) | set(range(S_CV, N_QKV_SLOTS))
    tiles = range(N_QKV_SLOTS // 2)
    plain = tuple(tl for tl in tiles if {2 * tl, 2 * tl + 1} <= (no_rope & no_norm))
    norm_only = tuple(tl for tl in tiles if {2 * tl, 2 * tl + 1} <= no_rope and tl not in plain)
    return pl.pallas_call(
        functools.partial(_proj_kernel, sub=_tile(tm, 256), plain_tiles=plain, norm_tiles=norm_only),
        out_shape=[jax.ShapeDtypeStruct((m, zw), F32), jax.ShapeDtypeStruct((m, zw), BF16)],
        grid=(m // tm, n_tiles + 1),
        in_specs=[pl.BlockSpec((tm, d), lambda i, j: (i, 0)),
                  pl.BlockSpec((None, d, tn), lambda i, j: (l, 0, jnp.minimum(j, n_tiles - 1))),
                  pl.BlockSpec((1, tn), row), pl.BlockSpec((1, tn), row), pl.BlockSpec((1, tn), row),
                  pl.BlockSpec((tm, LANE), tab), pl.BlockSpec((tm, LANE), tab), pl.BlockSpec((tm, LANE), tab),
                  pl.BlockSpec((tn, tn), lambda i, j: (0, 0))],
        out_specs=[pl.BlockSpec((tm, tn), lambda i, j: (i, prev(j)))] * 2,
        scratch_shapes=[pltpu.VMEM((tm, tn), F32), pltpu.VMEM((tm, tn), F32)],
        compiler_params=_cparams("parallel", "arbitrary"),
        name="qkv_projection",
    )(h, w_pack, gain, nf, rf, cos_t, sin_a, sin_b, _ones_blockdiag(tn))


def _column_vectors(lw):
    f = lambda v: jnp.asarray(v, F32).reshape(-1)
    ones = lambda n: jnp.ones((n,), F32)
    zeros = lambda n: jnp.zeros((n,), F32)
    rep = lambda v, n: jnp.tile(f(v), n)
    groups = [
        (rep(lw["a_qn"], 8), 1.0, 1.0), (rep(lw["a_kn"], 2), 1.0, 1.0), (ones(128), 0.0, 0.0),
        (ones(256), 0.0, 1.0), (ones(64), 0.0, 1.0), (ones(64 + LANE), 0.0, 0.0),
        (rep(lw["b_qn"], 8), 1.0, 0.0), (rep(lw["b_kn"], 8), 1.0, 0.0), (ones(512), 0.0, 0.0),
        (rep(lw["c_qn"], 4), 1.0, 1.0), (rep(lw["c_kn"], 4), 1.0, 1.0), (ones(512), 0.0, 0.0)]
    gain = jnp.concatenate([g for g, _, _ in groups]).reshape(1, -1)
    flags = lambda idx: jnp.asarray(
        np.concatenate([np.full((g.shape[0],), grp[idx], np.float32) for grp in groups for g in grp[:1]])[None, :])
    del zeros
    return gain, flags(1), flags(2)


def _rope_tables(pos):
    half = ROPE_DIM // 2
    inv_freq = ROPE_THETA ** (-jnp.arange(0, ROPE_DIM, 2, dtype=F32) / ROPE_DIM)
    ang = pos.astype(F32)[:, None] * inv_freq[None, :]
    cos, sin = jnp.cos(ang), jnp.sin(ang)
    rows = pos.shape[0]
    one = jnp.ones((rows, HEAD_DIM - ROPE_DIM), F32)
    zero = jnp.zeros((rows, HEAD_DIM - ROPE_DIM), F32)
    z8 = jnp.zeros((rows, half), F32)
    cos_h = jnp.concatenate([cos, cos, one], axis=1)
    sa_h = jnp.concatenate([z8, sin, zero], axis=1)
    sb_h = jnp.concatenate([-sin, z8, zero], axis=1)
    dup = lambda t: jnp.concatenate([t, t], axis=1)
    return dup(cos_h), dup(sa_h), dup(sb_h)


def _to_key(score):
    bits = pltpu.bitcast(score + 0.0, jnp.int32)
    return jnp.where(bits < 0, bits ^ 0x7FFFFFFF, bits)


def _indexer_scores(iq, iw, ik):
    kk = ik[:, :HEAD_DIM]
    sc = None
    for hd in range(A_IDX_HEADS):
        logit = _dot_nt3(iq[:, hd * HEAD_DIM:(hd + 1) * HEAD_DIM], kk)
        wgt = iw[:, HEAD_DIM + hd:HEAD_DIM + hd + 1] * (A_IDX_HEADS ** -0.5 * HEAD_DIM ** -0.5)
        term = jnp.maximum(logit, 0.0) * wgt
        sc = term if sc is None else sc + term
    return sc


def _lane_fold(x, op):
    out = x[:, :LANE]
    for s in range(1, x.shape[1] // LANE):
        out = op(out, x[:, s * LANE:(s + 1) * LANE])
    return out


def _count(key_ref, n_blk, blk, pred):
    rows = key_ref.shape[0]
    grp = min(rows, LANE)
    parts = []
    for r0 in range(0, rows, grp):
        def body(b, acc, r0=r0):
            start = pl.multiple_of(b * blk, blk)
            kb = key_ref[r0:r0 + grp, pl.ds(start, blk)]
            return acc + _lane_fold(jnp.where(pred(kb, start, slice(r0, r0 + grp)), 1.0, 0.0), jnp.add)

        parts.append(lax.fori_loop(0, n_blk, body, jnp.zeros((grp, LANE), F32)))
    acc = parts[0] if len(parts) == 1 else jnp.concatenate(parts, axis=0)
    return jnp.sum(acc, axis=-1, keepdims=True)


def _topk_threshold(key_ref, n_blk, blk, topk):
    rows = key_ref.shape[0]
    kf = float(topk)
    c0 = _count(key_ref, n_blk, blk, lambda kb, st, rs: kb >= 0)
    ans = jnp.where(c0 >= kf, 0, INT_MIN).astype(jnp.int32)

    def bit_step(it, ans):
        cand = ans + jnp.left_shift(jnp.int32(1), 30 - it)
        c = _count(key_ref, n_blk, blk, lambda kb, st, rs: kb >= cand[rs])
        return jnp.where(c >= kf, cand, ans)

    thr = lax.fori_loop(0, 31, bit_step, ans)
    n_gt = _count(key_ref, n_blk, blk, lambda kb, st, rs: kb > thr[rs])
    n_eq = _count(key_ref, n_blk, blk, lambda kb, st, rs: kb == thr[rs])
    need = kf - n_gt
    nbits = int(n_blk * blk).bit_length() if isinstance(n_blk, int) else 14
    cut_all = jnp.full((rows, 1), 1 << nbits, jnp.int32)
    tie_overflow = jnp.max(jnp.where((n_eq > need) & (thr != NEG_KEY), 1.0, 0.0)) > 0.5

    def search_cut():
        def cut_step(it, cut):
            cand = cut + jnp.left_shift(jnp.int32(1), nbits - 1 - it)

            def pred(kb, st, rs):
                idx = st + lax.broadcasted_iota(jnp.int32, kb.shape, 1)
                return (kb == thr[rs]) & (idx < cand[rs])

            c = _count(key_ref, n_blk, blk, pred)
            return jnp.where(c <= need, cand, cut)

        return lax.fori_loop(0, nbits, cut_step, jnp.zeros((rows, 1), jnp.int32))

    cut = lax.cond(tie_overflow, search_cut, lambda: cut_all)
    return thr, cut


def _selected(kb, first_idx, thr, cut):
    idx = first_idx + lax.broadcasted_iota(jnp.int32, kb.shape, 1)
    return ((kb > thr) | ((kb == thr) & (idx < cut))) & (kb > NEG_KEY)


def _pair_heads(o_even, o_odd, group_half):
    lane = lax.broadcasted_iota(jnp.int32, o_even.shape, 1)
    if group_half == 0:
        return jnp.where(lane < HEAD_DIM, o_even, pltpu.roll(o_odd, HEAD_DIM, 1))
    return jnp.where(lane < HEAD_DIM, pltpu.roll(o_even, HEAD_DIM, 1), o_odd)


def _dsa_query_heads(q_ref, qh_ref):
    for hd in range(A_HEADS):
        slot, half = hd // 2, hd % 2
        group = hd // (A_HEADS // A_KV_HEADS)
        qs = q_ref[:, slot * LANE:(slot + 1) * LANE] * (HEAD_DIM ** -0.5)
        if half != group:
            qs = pltpu.roll(qs, HEAD_DIM, 1)
        qh_ref[hd] = jnp.where(_half_mask(qs.shape, group), qs, 0.0).astype(BF16)


def _softmax_block(sel, kblk, vblk, q_heads, m_ref, acc_ref, s_ref, p_ref):
    n = len(q_heads)
    reps = kblk.shape[0] // LANE
    v_ones = jnp.concatenate([vblk, jnp.ones_like(vblk)], axis=1)
    for hd in range(n):
        s_ref[hd] = _dot_nt(q_heads[hd], kblk)
    alphas = []
    for hd in range(n):
        s = s_ref[hd]
        if sel is not None:
            s = jnp.where(sel[hd] if isinstance(sel, (list, tuple)) else sel, s, -jnp.inf)
            s_ref[hd] = s
        blk_max = jnp.max(_lane_fold(s, jnp.maximum), axis=-1, keepdims=True)
        m_prev = m_ref[hd]
        m_new = jnp.maximum(m_prev, blk_max)
        alphas.append(jnp.exp(m_prev - m_new))
        m_ref[hd] = m_new
    for hd in range(n):
        p_ref[hd] = jnp.exp(s_ref[hd] - jnp.tile(m_ref[hd], (1, reps))).astype(BF16)
    for hd in range(n):
        alpha2 = jnp.tile(alphas[hd], (1, 2))
        acc_ref[hd] = alpha2 * acc_ref[hd] + jnp.dot(p_ref[hd], v_ones, preferred_element_type=F32)


def _softmax_init(m_ref, acc_ref):
    m_ref[...] = jnp.full(m_ref.shape, M_FLOOR, F32)
    acc_ref[...] = jnp.zeros(acc_ref.shape, F32)


def _softmax_result(acc_ref, hd):
    acc = acc_ref[hd]
    return acc[:, :LANE] / acc[:, LANE:]


def _indexer_query3(iq_ref, iq3_ref):
    for hd in range(A_IDX_HEADS):
        slot, half = hd // 2, hd % 2
        x = iq_ref[:, slot * LANE:(slot + 1) * LANE]
        xl = jnp.where(_half_mask(x.shape, half), x, pltpu.roll(x, HEAD_DIM, 1))
        hi = xl.astype(BF16).astype(F32)
        lower = _half_mask(x.shape, 0)
        iq3_ref[hd, :, :LANE] = jnp.where(lower, hi, xl - hi).astype(BF16)
        iq3_ref[hd, :, LANE:] = jnp.where(lower, hi, 0.0).astype(BF16)


def _ik3_kernel(x_ref, o_ref):
    x = x_ref[...]
    lower = _half_mask(x.shape, 0)
    xl = jnp.where(lower, x, pltpu.roll(x, HEAD_DIM, 1))
    hi = xl.astype(BF16).astype(F32)
    o_ref[:, :LANE] = hi.astype(BF16)
    o_ref[:, LANE:] = jnp.where(lower, xl - hi, 0.0).astype(BF16)


def _indexer_keys3(z, t):
    tm = _tile(t, 1024)
    return pl.pallas_call(
        _ik3_kernel,
        out_shape=jax.ShapeDtypeStruct((t, 2 * LANE), BF16),
        grid=(t // tm,),
        in_specs=[pl.BlockSpec((tm, LANE), lambda i: (i, S_AIK))],
        out_specs=pl.BlockSpec((tm, 2 * LANE), lambda i: (i, 0)),
        compiler_params=_cparams("parallel"),
        name="indexer_keys",
    )(z)


def _dsa_prompt_kernel(q_ref, iq_ref, iw_ref, k_ref, v_ref, ik3_ref, o_ref,
                       key_ref, qh_ref, iq3_ref, m_ref, acc_ref, s_ref, p_ref, *, tq, kb, topk):
    i = pl.program_id(0)
    q0 = i * tq
    n_blk = (q0 + tq + kb - 1) // kb
    row = lax.broadcasted_iota(jnp.int32, (tq, kb), 0)
    limit = (((q0 + row) >> 6) + 1) << 6
    _indexer_query3(iq_ref, iq3_ref)
    iw = iw_ref[...]
    wgt = [iw[:, HEAD_DIM + hd:HEAD_DIM + hd + 1] * (A_IDX_HEADS ** -0.5 * HEAD_DIM ** -0.5)
           for hd in range(A_IDX_HEADS)]

    def score_blk(b, carry):
        start = pl.multiple_of(b * kb, kb)
        ikb = ik3_ref[pl.ds(start, kb), :]
        for hd in range(A_IDX_HEADS):
            s_ref[hd] = _dot_nt(iq3_ref[hd], ikb)
        sc = None
        for hd in range(A_IDX_HEADS):
            term = jnp.maximum(s_ref[hd], 0.0) * wgt[hd]
            sc = term if sc is None else sc + term
        adm = start + lax.broadcasted_iota(jnp.int32, (tq, kb), 1) < limit
        key_ref[:, pl.ds(start, kb)] = jnp.where(adm, _to_key(sc), NEG_KEY)
        return carry

    lax.fori_loop(0, n_blk, score_blk, 0)
    thr, cut = _topk_threshold(key_ref, n_blk, kb, topk)
    _dsa_query_heads(q_ref, qh_ref)
    _softmax_init(m_ref, acc_ref)

    def attn_blk(b, carry):
        start = pl.multiple_of(b * kb, kb)
        sel = _selected(key_ref[:, pl.ds(start, kb)], start, thr, cut)
        _softmax_block(sel, k_ref[pl.ds(start, kb), :], v_ref[pl.ds(start, kb), :],
                       [qh_ref[hd] for hd in range(A_HEADS)], m_ref, acc_ref, s_ref, p_ref)
        return carry

    lax.fori_loop(0, n_blk, attn_blk, 0)
    for j in range(A_HEADS // 2):
        outs = [_softmax_result(acc_ref, 2 * j + e) for e in range(2)]
        o_ref[:, j * LANE:(j + 1) * LANE] = _pair_heads(outs[0], outs[1], (2 * j) // (A_HEADS // A_KV_HEADS)).astype(o_ref.dtype)


def _mixer_call(kern, o_all, m, *, grid, in_specs, out_block, out_index, args, scratch=(), sem, name):
    n_in = len(args)
    if o_all is None:
        fn, specs, alias, extra = kern, list(in_specs), {}, []
    else:
        fn = lambda *refs: kern(*refs[:n_in], *refs[n_in + 1:])
        specs, alias, extra = list(in_specs) + [pl.BlockSpec(memory_space=pl.ANY)], {n_in: 0}, [o_all]
    return pl.pallas_call(
        fn,
        out_shape=jax.ShapeDtypeStruct((m, N_BRANCH * BRANCH_WIDTH), BF16),
        grid=grid,
        in_specs=specs,
        out_specs=pl.BlockSpec(out_block, out_index),
        scratch_shapes=list(scratch),
        input_output_aliases=alias,
        compiler_params=_cparams(*sem),
        name=name,
    )(*args, *extra)


def _dsa_prompt(z, zb, t, o_all, m):
    tq, kb = 256, 512
    topk = min(A_TOPK_MAX, t // 4)
    full = lambda s: pl.BlockSpec((t, LANE), lambda i: (0, s))
    heads = lambda dt, w=LANE: pltpu.VMEM((A_HEADS, tq, w), dt)
    return _mixer_call(
        functools.partial(_dsa_prompt_kernel, tq=tq, kb=kb, topk=topk), o_all, m,
        grid=(t // tq,),
        in_specs=[pl.BlockSpec((tq, 4 * LANE), lambda i: (i, S_AQ // 4)),
                  pl.BlockSpec((tq, 2 * LANE), lambda i: (i, S_AIQ // 2)),
                  pl.BlockSpec((tq, LANE), lambda i: (i, S_AIK)),
                  full(S_AK), full(S_AV), pl.BlockSpec((t, 2 * LANE), lambda i: (0, 0))],
        out_block=(tq, BRANCH_WIDTH), out_index=lambda i: (i, 0),
        scratch=[pltpu.VMEM((tq, t), jnp.int32), heads(BF16),
                 pltpu.VMEM((A_IDX_HEADS, tq, 2 * LANE), BF16),
                 heads(F32), heads(F32, 2 * LANE), heads(F32, kb), heads(BF16, kb)],
        sem=("arbitrary",), name="dsa_prompt",
        args=(z, z, z, zb, zb, _indexer_keys3(z, t)))


def _feature_major(cache, n_feat):
    nd = cache.ndim
    perm = (0, 1) + tuple(range(3, nd)) + (2,)
    return jnp.transpose(cache, perm).reshape(cache.shape[0], cache.shape[1], n_feat, cache.shape[2])


def _dsa_sample_kernel(q_ref, iq_ref, new_ik_ref, new_k_ref, new_v_ref, ckt_ref, cvt_ref, cikt_ref, o_ref,
                       key_ref, qh_ref, *, ts, n_seq, past, topk, q_pos0):
    rows_all = n_seq * ts
    width = key_ref.shape[1]
    iq_all, iw_all = iq_ref[...], new_ik_ref[...]
    row_c = lax.broadcasted_iota(jnp.int32, (ts, past), 0)
    limit_c = (((q_pos0 + row_c) >> 6) + 1) << 6
    kpos_c = lax.broadcasted_iota(jnp.int32, (ts, past), 1)
    kpos_n = past + lax.broadcasted_iota(jnp.int32, (ts, ts), 1)
    limit_n = (((q_pos0 + lax.broadcasted_iota(jnp.int32, (ts, ts), 0)) >> 6) + 1) << 6
    key_ref[:, past:] = jnp.full((rows_all, width - past), NEG_KEY, jnp.int32)
    for g in range(n_seq):
        rows = slice(g * ts, (g + 1) * ts)
        iq, iw = iq_all[rows], iw_all[rows]
        ikt = cikt_ref[g]
        ik_new = iw[:, :HEAD_DIM]
        sc_c = sc_n = None
        for hd in range(A_IDX_HEADS):
            qh = iq[:, hd * HEAD_DIM:(hd + 1) * HEAD_DIM]
            wgt = iw[:, HEAD_DIM + hd:HEAD_DIM + hd + 1] * (A_IDX_HEADS ** -0.5 * HEAD_DIM ** -0.5)
            t_c = jnp.maximum(_dot3(qh, ikt), 0.0) * wgt
            t_n = jnp.maximum(_dot_nt3(qh, ik_new), 0.0) * wgt
            sc_c = t_c if sc_c is None else sc_c + t_c
            sc_n = t_n if sc_n is None else sc_n + t_n
        key_ref[rows, :past] = jnp.where(kpos_c < limit_c, _to_key(sc_c), NEG_KEY)
        key_ref[rows, past:past + ts] = jnp.where(kpos_n < limit_n, _to_key(sc_n), NEG_KEY)

    thr, cut = _topk_threshold(key_ref, width // LANE, LANE, topk)
    _dsa_query_heads(q_ref, qh_ref)
    for g in range(n_seq):
        rows = slice(g * ts, (g + 1) * ts)
        sel_c = _selected(key_ref[rows, :past], 0, thr[rows], cut[rows])
        sel_n = _selected(key_ref[rows, past:past + ts], past, thr[rows], cut[rows])
        kct, vct = ckt_ref[g].astype(BF16), cvt_ref[g].astype(BF16)
        kn, vn = new_k_ref[rows, :], new_v_ref[rows, :]
        outs = []
        for hd in range(A_HEADS):
            qh = qh_ref[hd, rows, :]
            s_c = jnp.where(sel_c, jnp.dot(qh, kct, preferred_element_type=F32), -jnp.inf)
            s_n = jnp.where(sel_n, _dot_nt(qh, kn), -jnp.inf)
            m = jnp.maximum(jnp.max(s_c, axis=-1, keepdims=True), jnp.max(s_n, axis=-1, keepdims=True))
            p_c, p_n = jnp.exp(s_c - m), jnp.exp(s_n - m)
            l = jnp.sum(p_c, axis=-1, keepdims=True) + jnp.sum(p_n, axis=-1, keepdims=True)
            o = _dot_nt(p_c.astype(BF16), vct) + jnp.dot(p_n.astype(BF16), vn, preferred_element_type=F32)
            outs.append(o / l)
        for j in range(A_HEADS // 2):
            o_ref[rows, j * LANE:(j + 1) * LANE] = _pair_heads(
                outs[2 * j], outs[2 * j + 1], (2 * j) // (A_HEADS // A_KV_HEADS)).astype(o_ref.dtype)


def _dsa_sample(z, zb, cache_k, cache_v, cache_ik, l, t, nb, ts, o_all, m):
    past = cache_k.shape[2]
    topk = min(A_TOPK_MAX, (past + ts) // 4)
    width = ((past + ts + LANE - 1) // LANE) * LANE
    n_seq = LANE // ts
    rows = n_seq * ts
    assert LANE % ts == 0 and nb % n_seq == 0 and t % rows == 0
    rb = t // rows
    new = lambda s: pl.BlockSpec((rows, LANE), lambda i: (rb + i, s))
    cache = lambda w: pl.BlockSpec((None, n_seq, w, past), lambda i: (l, i, 0, 0))
    return _mixer_call(
        functools.partial(_dsa_sample_kernel, ts=ts, n_seq=n_seq, past=past, topk=topk, q_pos0=past), o_all, m,
        grid=(nb // n_seq,),
        in_specs=[pl.BlockSpec((rows, 4 * LANE), lambda i: (rb + i, S_AQ // 4)),
                  pl.BlockSpec((rows, 2 * LANE), lambda i: (rb + i, S_AIQ // 2)),
                  new(S_AIK), new(S_AK), new(S_AV), cache(LANE), cache(LANE), cache(HEAD_DIM)],
        out_block=(rows, BRANCH_WIDTH), out_index=lambda i: (rb + i, 0),
        scratch=[pltpu.VMEM((rows, width), jnp.int32), pltpu.VMEM((A_HEADS, rows, LANE), BF16)],
        sem=("arbitrary",), name="dsa_sample",
        args=(z, z, z, zb, zb, _feature_major(cache_k, LANE), _feature_major(cache_v, LANE),
              _feature_major(cache_ik, HEAD_DIM)))


def _band_prompt_kernel(q_ref, k_ref, v_ref, bias0_ref, bias1_ref, o_ref, s_ref, p_ref, *, tq, win):
    i = pl.program_id(1)
    units = [(r, e) for r in range(2) for e in range(2)]
    biases = (bias0_ref, bias1_ref)
    w0s, oks = [], []
    for r in range(2):
        q0 = (2 * i + r) * tq
        w0 = pl.multiple_of(jnp.maximum(q0 - B_WINDOW, 0), tq)
        kc = (w0 + lax.broadcasted_iota(jnp.int32, (tq, win), 1)) >> 6
        qc = (q0 + lax.broadcasted_iota(jnp.int32, (tq, win), 0)) >> 6
        w0s.append(w0)
        oks.append((kc <= qc) & (kc >= qc - B_PAST_CHUNKS))
    for u, (r, e) in enumerate(units):
        qs = q_ref[r * tq:(r + 1) * tq, :] * (HEAD_DIM ** -0.5)
        qe = jnp.where(_half_mask(qs.shape, e), qs, 0.0).astype(BF16)
        s_ref[u] = _dot_nt(qe, k_ref[pl.ds(w0s[r], win), :])
    for u, (r, e) in enumerate(units):
        s = jnp.where(oks[r], s_ref[u] + biases[r][e], -jnp.inf)
        p_ref[u] = jnp.exp(s - jnp.max(s, axis=-1, keepdims=True)).astype(BF16)
    outs = []
    for u, (r, e) in enumerate(units):
        vw = v_ref[pl.ds(w0s[r], win), :]
        acc = jnp.dot(p_ref[u], jnp.concatenate([vw, jnp.ones_like(vw)], axis=1), preferred_element_type=F32)
        outs.append(acc[:, :LANE] / acc[:, LANE:])
    for r in range(2):
        o_ref[r * tq:(r + 1) * tq, :] = jnp.where(_half_mask(outs[0].shape, 0), outs[2 * r],
                                                   outs[2 * r + 1]).astype(o_ref.dtype)


def _band_prompt(z, zb, bias, t, tq, o_all, m):
    win = B_WINDOW + tq
    n_case = B_WINDOW // tq
    assert t % (2 * tq) == 0
    case = lambda r: pl.BlockSpec((None, 2, tq, win), lambda j, i: (jnp.minimum(2 * i + r, n_case), j, 0, 0))
    return _mixer_call(
        functools.partial(_band_prompt_kernel, tq=tq, win=win), o_all, m,
        grid=(B_HEADS // 2, t // (2 * tq)),
        in_specs=[pl.BlockSpec((2 * tq, LANE), lambda j, i: (i, S_BQ + j)),
                  pl.BlockSpec((t, LANE), lambda j, i: (0, S_BK + j)),
                  pl.BlockSpec((t, LANE), lambda j, i: (0, S_BV + j)),
                  case(0), case(1)],
        out_block=(2 * tq, LANE), out_index=lambda j, i: (i, BRANCH_WIDTH // LANE + j),
        scratch=[pltpu.VMEM((4, tq, win), F32), pltpu.VMEM((4, tq, win), BF16)],
        sem=("parallel", "arbitrary"), name="band_prompt", args=(z, zb, zb, bias, bias))


def _band_sample_kernel(z_ref, zb_ref, kct_ref, vct_ref, bc_ref, bn_ref, o_ref):
    for j in range(B_HEADS // 2):
        slot = lambda ref, s: ref[:, (s + j) * LANE:(s + j + 1) * LANE]
        feat = slice(j * LANE, (j + 1) * LANE)
        kct, vct = kct_ref[feat, :].astype(BF16), vct_ref[feat, :].astype(BF16)
        kn, vn = slot(zb_ref, S_BK), slot(zb_ref, S_BV)
        qs = slot(z_ref, S_BQ) * (HEAD_DIM ** -0.5)
        outs = []
        for e in range(2):
            qe = jnp.where(_half_mask(qs.shape, e), qs, 0.0).astype(BF16)
            s_c = jnp.dot(qe, kct, preferred_element_type=F32) + bc_ref[2 * j + e]
            s_n = _dot_nt(qe, kn) + bn_ref[2 * j + e]
            m = jnp.maximum(jnp.max(s_c, axis=-1, keepdims=True), jnp.max(s_n, axis=-1, keepdims=True))
            p_c, p_n = jnp.exp(s_c - m), jnp.exp(s_n - m)
            l = jnp.sum(p_c, axis=-1, keepdims=True) + jnp.sum(p_n, axis=-1, keepdims=True)
            o = _dot_nt(p_c.astype(BF16), vct) + jnp.dot(p_n.astype(BF16), vn, preferred_element_type=F32)
            outs.append(o / l)
        o_ref[:, feat] = jnp.where(_half_mask(outs[0].shape, 0), outs[0], outs[1]).astype(o_ref.dtype)


def _band_sample(z, zb, cache_k, cache_v, bias_c, bias_n, l, t, nb, ts, o_all, m):
    pb = cache_k.shape[2]
    rb = t // ts
    bw = BRANCH_WIDTH
    cache = pl.BlockSpec((None, None, bw, pb), lambda b: (l, b, 0, 0))
    full = lambda a: pl.BlockSpec(a.shape, lambda b: (0,) * a.ndim)
    return _mixer_call(
        _band_sample_kernel, o_all, m,
        grid=(nb,),
        in_specs=[pl.BlockSpec((ts, z.shape[1]), lambda b: (rb + b, 0)),
                  pl.BlockSpec((ts, z.shape[1]), lambda b: (rb + b, 0)),
                  cache, cache, full(bias_c), full(bias_n)],
        out_block=(ts, bw), out_index=lambda b: (rb + b, 1),
        sem=("parallel",), name="band_sample",
        args=(z, zb, _feature_major(cache_k, bw), _feature_major(cache_v, bw), bias_c, bias_n))


def _band_bias(table, tq):
    n_case = B_WINDOW // tq
    win = B_WINDOW + tq
    width = win + B_WINDOW
    j = np.arange(width + tq - 1) - (tq - 1) - B_WINDOW
    ext = table.astype(F32)[:, np.clip(j, -B_REL_CLIP, B_REL_CLIP) + B_REL_CLIP]
    n = ext.shape[1]
    x = jnp.roll(ext, -(tq - 1), axis=1)
    toe = jnp.tile(x, (1, tq))[:, :tq * (n - 1)].reshape(ext.shape[0], tq, n - 1)[:, :, :width]
    return jnp.stack([toe[:, :, B_WINDOW - c * tq:B_WINDOW - c * tq + win] for c in range(n_case + 1)], axis=0)


def _lambda(lam_ref, lam_init):
    lv = lam_ref[...]
    return (jnp.exp(jnp.sum(lv[0:1] * lv[1:2], axis=-1, keepdims=True))
            - jnp.exp(jnp.sum(lv[2:3] * lv[3:4], axis=-1, keepdims=True)) + lam_init)


def _diff_finish(o0, o1, lam, on_ref, lam_init):
    attn = o0 - lam * o1
    ms = jnp.mean(attn * attn, axis=-1, keepdims=True)
    return (attn * lax.rsqrt(ms + NORM_EPS) * on_ref[...]) * (1.0 - lam_init)


def _diff_prompt_kernel(q_ref, k_ref, v_ref, lam_ref, on_ref, o_ref, qh_ref, m_ref, acc_ref, s_ref, p_ref,
                        *, tq, kb, tg, lam_init):
    i = pl.program_id(1)
    q0 = i * tq
    n_grp = tq // tg
    n_blk = (q0 + tq + kb - 1) // kb
    row = lax.broadcasted_iota(jnp.int32, (tg, kb), 0)
    limits = [(((q0 + g * tg + row) >> 6) + 1) << 6 for g in range(n_grp)]
    for g in range(n_grp):
        qs = q_ref[g * tg:(g + 1) * tg, :] * (HEAD_DIM ** -0.5)
        for c in range(2):
            qh_ref[2 * g + c] = jnp.where(_half_mask(qs.shape, c), qs, 0.0).astype(BF16)
    _softmax_init(m_ref, acc_ref)
    n_full = (q0 + CHUNK) // kb

    def attn_blk(b, carry, masked):
        start = pl.multiple_of(b * kb, kb)
        ok = None
        if masked:
            kpos = start + lax.broadcasted_iota(jnp.int32, (tg, kb), 1)
            ok = [kpos < limits[u // 2] for u in range(2 * n_grp)]
        _softmax_block(ok, k_ref[pl.ds(start, kb), :], v_ref[pl.ds(start, kb), :],
                       [qh_ref[u] for u in range(2 * n_grp)], m_ref, acc_ref, s_ref, p_ref)
        return carry

    lax.fori_loop(0, n_full, functools.partial(attn_blk, masked=False), 0)
    lax.fori_loop(n_full, n_blk, functools.partial(attn_blk, masked=True), 0)
    lam = _lambda(lam_ref, lam_init)
    for g in range(n_grp):
        outs = [_softmax_result(acc_ref, 2 * g + c) for c in range(2)]
        o_ref[g * tg:(g + 1) * tg, :] = _diff_finish(outs[0], outs[1], lam, on_ref, lam_init).astype(o_ref.dtype)


def _diff_prompt(z, zb, c_lam_all, c_on_all, l, t, lam_init, o_all, m):
    tq, kb, tg = _tile(t, 512), 512, 128
    maps = lambda dt, w=LANE: pltpu.VMEM((2 * tq // tg, tg, w), dt)
    return _mixer_call(
        functools.partial(_diff_prompt_kernel, tq=tq, kb=kb, tg=tg, lam_init=lam_init), o_all, m,
        grid=(C_HEADS, t // tq),
        in_specs=[pl.BlockSpec((tq, LANE), lambda h, i: (i, S_CQ + h)),
                  pl.BlockSpec((t, LANE), lambda h, i: (0, S_CK + h)),
                  pl.BlockSpec((t, LANE), lambda h, i: (0, S_CV + h)),
                  pl.BlockSpec((None, 4, HEAD_DIM), lambda h, i: (l, 0, 0)),
                  _layer_vec(l, LANE)],
        out_block=(tq, LANE), out_index=lambda h, i: (i, 2 * BRANCH_WIDTH // LANE + h),
        scratch=[maps(BF16), maps(F32), maps(F32, 2 * LANE), maps(F32, kb), maps(BF16, kb)],
        sem=("parallel", "arbitrary"), name="diff_prompt",
        args=(z, zb, zb, c_lam_all, _vec3(c_on_all)))


def _diff_sample_kernel(z_ref, zb_ref, kct_ref, vc_ref, lam_ref, on_ref, o_ref, *, past, lam_init):
    lam = _lambda(lam_ref, lam_init)
    for h in range(C_HEADS):
        slot = lambda ref, s: ref[:, (s + h) * LANE:(s + h + 1) * LANE]
        feat = slice(h * LANE, (h + 1) * LANE)
        kct = kct_ref[feat, :].astype(BF16)
        vc = vc_ref[pl.ds(h, past, stride=C_HEADS), :].astype(BF16)
        kn, vn = slot(zb_ref, S_CK), slot(zb_ref, S_CV)
        qs = slot(z_ref, S_CQ) * (HEAD_DIM ** -0.5)
        outs = []
        for c in range(2):
            qc = jnp.where(_half_mask(qs.shape, c), qs, 0.0).astype(BF16)
            s_c = jnp.dot(qc, kct, preferred_element_type=F32)
            s_n = _dot_nt(qc, kn)
            m = jnp.maximum(jnp.max(s_c, axis=-1, keepdims=True), jnp.max(s_n, axis=-1, keepdims=True))
            p_c, p_n = jnp.exp(s_c - m), jnp.exp(s_n - m)
            l = jnp.sum(p_c, axis=-1, keepdims=True) + jnp.sum(p_n, axis=-1, keepdims=True)
            o = (jnp.dot(p_c.astype(BF16), vc, preferred_element_type=F32)
                 + jnp.dot(p_n.astype(BF16), vn, preferred_element_type=F32))
            outs.append(o / l)
        o_ref[:, feat] = _diff_finish(outs[0], outs[1], lam, on_ref, lam_init).astype(o_ref.dtype)


def _diff_sample(z, zb, cache_k, cache_v, c_lam_all, c_on_all, l, t, nb, ts, lam_init, o_all, m):
    depth, _, past = cache_k.shape[:3]
    rb = t // ts
    bw = BRANCH_WIDTH
    return _mixer_call(
        functools.partial(_diff_sample_kernel, past=past, lam_init=lam_init), o_all, m,
        grid=(nb,),
        in_specs=[pl.BlockSpec((ts, z.shape[1]), lambda b: (rb + b, 0)),
                  pl.BlockSpec((ts, z.shape[1]), lambda b: (rb + b, 0)),
                  pl.BlockSpec((None, None, bw, past), lambda b: (l, b, 0, 0)),
                  pl.BlockSpec((None, None, past * C_HEADS, LANE), lambda b: (l, b, 0, 0)),
                  pl.BlockSpec((None, 4, HEAD_DIM), lambda b: (l, 0, 0)),
                  _layer_vec(l, LANE)],
        out_block=(ts, bw), out_index=lambda b: (rb + b, 2),
        sem=("parallel",), name="diff_sample",
        args=(z, zb, _feature_major(cache_k, bw), cache_v.reshape(depth, nb, past * C_HEADS, LANE),
              c_lam_all, _vec3(c_on_all)))


def _rwkv_pre_kernel(zd_ref, prev_ref, first_ref, mu_ref, w0_ref, w2_ref, a0_ref, a2_ref, g2_ref, kkw_ref,
                     ka_ref, bd_ref, r_ref, w_ref, k_ref, v_ref, kk_ref, b_ref, g_ref, *, tm, t, ts):
    zf = zd_ref[...]
    grow = pl.program_id(0) * tm + lax.broadcasted_iota(jnp.int32, (tm, 1), 0)
    zs = jnp.where(grow % tm == 0, prev_ref[7:8, :], pltpu.roll(zf, 1, 0))
    seq_start = (grow >= t) & ((grow - t) % ts == 0)
    zs = jnp.where(seq_start, first_ref[...], zs)
    zs = jnp.where(grow == 0, 0.0, zs)
    zm = zf + (zs - zf) * mu_ref[...]
    bw = BRANCH_WIDTH
    r, k, v = zm[:, :bw], zm[:, bw:2 * bw], zm[:, 2 * bw:3 * bw]
    wl, al, gl = zm[:, 3 * bw:3 * bw + 64], zm[:, 3 * bw + 64:3 * bw + 128], zm[:, 3 * bw + 128:]
    dot = lambda a, b: jnp.dot(a.astype(BF16), b.astype(BF16), preferred_element_type=F32)
    u = -(w0_ref[...] + dot(jnp.tanh(wl), w2_ref[...]))
    softplus = jnp.maximum(u, 0.0) + jnp.log(1.0 + jnp.exp(-jnp.abs(u)))
    w = -softplus - 0.5
    a = jax.nn.sigmoid(a0_ref[...] + dot(al, a2_ref[...]))
    kk = k * kkw_ref[...]
    nrm = jnp.sqrt(_seg_sum(kk * kk, bd_ref[...]))
    kk = kk / jnp.maximum(nrm, 1e-12)
    r_ref[...] = r
    w_ref[...] = -jnp.exp(w)
    k_ref[...] = k * (1.0 + (a - 1.0) * ka_ref[...])
    v_ref[...] = v
    kk_ref[...] = kk
    b_ref[...] = kk * a
    g_ref[...] = dot(jax.nn.sigmoid(gl), g2_ref[...])


def _rwkv_pre(zd, zfirst, p, l, t, ts):
    m, dc = zd.shape
    tm = _tile(math.gcd(t, m - t), 512)
    bw = BRANCH_WIDTH
    npt = t // tm
    rows = pl.BlockSpec((tm, dc), lambda i: (i, 0))
    vec = lambda n: _layer_vec(l, n)
    mat = lambda k: pl.BlockSpec((None, k, bw), lambda i: (l, 0, 0))
    out = jax.ShapeDtypeStruct((m, bw), F32)
    return pl.pallas_call(
        functools.partial(_rwkv_pre_kernel, tm=tm, t=t, ts=ts),
        out_shape=[out] * 7,
        grid=(m // tm,),
        in_specs=[rows,
                  pl.BlockSpec((8, dc), lambda i: (jnp.maximum(i * (tm // 8) - 1, 0), 0)),
                  pl.BlockSpec((tm, dc), lambda i: (jnp.maximum(i - npt, 0), 0)),
                  vec(dc), vec(bw), mat(64), vec(bw), mat(64), mat(128), vec(bw), vec(bw),
                  pl.BlockSpec((bw, bw), lambda i: (0, 0))],
        out_specs=[pl.BlockSpec((tm, bw), lambda i: (i, 0))] * 7,
        compiler_params=_cparams("parallel"),
        name="rwkv_pre",
    )(zd, zd, zfirst, _vec3(p["d_mu"]), _vec3(p["d_w0"]), p["d_w2"], _vec3(p["d_a0"]), p["d_a2"], p["d_g2"],
      _vec3(p["d_k_k"]), _vec3(p["d_k_a"]), _ones_blockdiag(bw))


RW_CHUNK_MAX = 64
RW_BLOCK = 128
RW_SLOTS = BRANCH_WIDTH // LANE


def _dot3(a, b):
    ah, al = _split2(a)
    bh, bl = _split2(b)
    d = lambda x, y: jnp.dot(x, y, preferred_element_type=F32)
    return d(ah, bh) + d(al, bh) + d(ah, bl)


def _rwkv_chunk_kernel(*refs, seq_chunks, RW_CHUNK):
    if seq_chunks:
        (r_ref, lw_ref, k_ref, v_ref, kk_ref, b_ref, s0_ref, y_ref, sf_ref,
         h_ref, u_ref, ab_ref, rb_ref, u0_ref, y0_ref, bt_ref, kt_ref, eg_ref) = refs
    else:
        (r_ref, lw_ref, k_ref, v_ref, kk_ref, b_ref, y_ref, sf_ref,
         h_ref, u_ref, ab_ref, rb_ref, u0_ref, y0_ref, bt_ref, kt_ref, eg_ref) = refs
        s0_ref = None

        @pl.when(pl.program_id(0) == 0)
        def _():
            h_ref[...] = jnp.zeros_like(h_ref)

    n = RW_BLOCK
    n_chunks = n // RW_CHUNK
    row = lax.broadcasted_iota(jnp.int32, (n, n), 0)
    col = lax.broadcasted_iota(jnp.int32, (n, n), 1)
    same = (row // RW_CHUNK) == (col // RW_CHUNK)
    strict, incl = same & (col < row), same & (col <= row)
    eye = jnp.where(row == col, 1.0, 0.0)
    head_diag = (row // HEAD_DIM) == (col // HEAD_DIM)
    in_chunk = row % RW_CHUNK
    k_pick = jnp.where(lax.broadcasted_iota(jnp.int32, (n, HEAD_DIM), 0) % HEAD_DIM
                       == lax.broadcasted_iota(jnp.int32, (n, HEAD_DIM), 1), 1.0, 0.0)
    bf = lambda x: x.astype(BF16)
    mm = lambda x, y: jnp.dot(bf(x), bf(y), preferred_element_type=F32)
    u_ref[...] = jnp.zeros_like(u_ref)

    halves = [_half_mask((n, LANE), e) for e in range(2)]
    group = 2
    for j0 in range(0, RW_SLOTS, group):
        slot_data = []
        for j in range(j0, j0 + group):
            sl = slice(j * LANE, (j + 1) * LANE)
            lw, v = lw_ref[:, sl], v_ref[:, sl]
            g = lw
            for d in [1 << s for s in range(RW_CHUNK.bit_length() - 1)]:
                g = g + jnp.where(in_chunk >= d, pltpu.roll(g, d, 0), 0.0)
            inv_g = jnp.exp(-g)
            a_t = -kk_ref[:, sl] * jnp.exp(g - lw)
            b_t, k_t = b_ref[:, sl] * inv_g, k_ref[:, sl] * inv_g
            r_t = r_ref[:, sl] * jnp.exp(g)
            bt_ref[j], kt_ref[j], eg_ref[j] = b_t.T, k_t.T, jnp.exp(g).T
            slot_data.append((a_t, r_t, bf(b_t), bf(k_t), bf(v)))
        chains = [(s, e) for s in range(group) for e in range(2)]
        each = lambda f: [f(s, e, i) for i, (s, e) in enumerate(chains)]
        a_e = each(lambda s, e, i: bf(jnp.where(halves[e], slot_data[s][0], 0.0)))
        r_e = each(lambda s, e, i: bf(jnp.where(halves[e], slot_data[s][1], 0.0)))
        n_ab = each(lambda s, e, i: jnp.where(strict, _dot_nt(a_e[i], slot_data[s][2]), 0.0))
        n_ak = each(lambda s, e, i: jnp.where(strict, _dot_nt(a_e[i], slot_data[s][3]), 0.0))
        m_rb = each(lambda s, e, i: bf(jnp.where(incl, _dot_nt(r_e[i], slot_data[s][2]), 0.0)))
        m_rk = each(lambda s, e, i: bf(jnp.where(incl, _dot_nt(r_e[i], slot_data[s][3]), 0.0)))
        w_e = each(lambda s, e, i: mm(n_ak[i], slot_data[s][4]))
        tinv = each(lambda s, e, i: eye + n_ab[i])
        pw = n_ab
        for _ in range(RW_CHUNK.bit_length() - 2):
            pw = each(lambda s, e, i: mm(pw[i], pw[i]))
            tinv = each(lambda s, e, i: tinv[i] + mm(tinv[i], pw[i]))
        t16 = each(lambda s, e, i: bf(tinv[i]))
        a_bar = each(lambda s, e, i: mm(t16[i], slot_data[s][0]))
        u0 = each(lambda s, e, i: mm(t16[i], w_e[i]))
        r_bar = each(lambda s, e, i: slot_data[s][1] + mm(m_rb[i], a_bar[i]))
        y0 = each(lambda s, e, i: mm(m_rb[i], u0[i]) + mm(m_rk[i], slot_data[s][4]))
        for s in range(group):
            pick = lambda vals: jnp.where(halves[0], vals[2 * s], vals[2 * s + 1])
            j = j0 + s
            ab_ref[j], rb_ref[j], u0_ref[j], y0_ref[j] = pick(a_bar), pick(r_bar), pick(u0), pick(y0)

    for c in range(n_chunks):
        rows = slice(c * RW_CHUNK, (c + 1) * RW_CHUNK)
        col_c = (col // RW_CHUNK) == c
        for j in range(RW_SLOTS):
            sl = slice(j * LANE, (j + 1) * LANE)
            if seq_chunks and c % seq_chunks == 0:
                x = s0_ref[c // seq_chunks, sl, :]
                h = jnp.where(head_diag, _dot_nt3(k_pick, x), 0.0)
            else:
                h = h_ref[j]
            res = _dot3(jnp.concatenate([ab_ref[j, rows, :], rb_ref[j, rows, :]], axis=0), h)
            u_c = res[:RW_CHUNK] + u0_ref[j, rows, :]
            y_ref[rows, sl] = res[RW_CHUNK:] + y0_ref[j, rows, :]
            u_ref[j, rows, :] = u_c
            bk = jnp.concatenate([jnp.where(col_c, bt_ref[j], 0.0), jnp.where(col_c, kt_ref[j], 0.0)], axis=1)
            uv = jnp.concatenate([u_ref[j], v_ref[:, sl]], axis=0)
            inc = jnp.where(head_diag, mm(bk, uv), 0.0)
            g_end = eg_ref[j, :, (c + 1) * RW_CHUNK - 1:(c + 1) * RW_CHUNK]
            h = g_end * (h + inc)
            h_ref[j] = h
            if seq_chunks and (c + 1) % seq_chunks == 0:
                ht = h.T
                sf_ref[c // seq_chunks, sl, :] = (ht + pltpu.roll(ht, HEAD_DIM, 1))[:, :HEAD_DIM]

    if not seq_chunks:
        @pl.when(pl.program_id(0) == pl.num_programs(0) - 1)
        def _():
            for j in range(RW_SLOTS):
                ht = h_ref[j].T
                sf_ref[0, j * LANE:(j + 1) * LANE, :] = (ht + pltpu.roll(ht, HEAD_DIM, 1))[:, :HEAD_DIM]


def _rwkv_chunked(ops, s0, row0, n_seq, t):
    n = RW_BLOCK
    bw = BRANCH_WIDTH
    chunk = math.gcd(t, RW_CHUNK_MAX)
    assert row0 % n == 0 and (n_seq * t) % n == 0 and chunk >= 8 and chunk & (chunk - 1) == 0
    rb = row0 // n
    rows = pl.BlockSpec((n, bw), lambda i: (rb + i, 0))
    if s0 is None:
        assert n_seq == 1
        seq_chunks, per_blk, extra, extra_specs = 0, 1, [], []
        sf_spec = pl.BlockSpec((1, bw, HEAD_DIM), lambda i: (0, 0, 0))
    else:
        assert n % t == 0
        seq_chunks, per_blk = t // chunk, n // t
        extra = [s0.reshape(n_seq, bw, HEAD_DIM)]
        extra_specs = [pl.BlockSpec((per_blk, bw, HEAD_DIM), lambda i: (i, 0, 0))]
        sf_spec = pl.BlockSpec((per_blk, bw, HEAD_DIM), lambda i: (i, 0, 0))
    slot = lambda dt=F32: pltpu.VMEM((RW_SLOTS, n, LANE), dt)
    y, sf = pl.pallas_call(
        functools.partial(_rwkv_chunk_kernel, seq_chunks=seq_chunks, RW_CHUNK=chunk),
        out_shape=[jax.ShapeDtypeStruct((n_seq * t, bw), F32), jax.ShapeDtypeStruct((n_seq, bw, HEAD_DIM), F32)],
        grid=(n_seq * t // n,),
        in_specs=[rows] * 6 + extra_specs,
        out_specs=[pl.BlockSpec((n, bw), lambda i: (i, 0)), sf_spec],
        scratch_shapes=[slot() for _ in range(9)],
        compiler_params=_cparams("arbitrary"),
        name="rwkv_chunked",
    )(*ops, *extra)
    return y, sf.reshape(n_seq, D_HEADS, HEAD_DIM, HEAD_DIM)


def _rwkv_post_kernel(y_ref, r_ref, k_ref, v_ref, g_ref, lnw_ref, lnb_ref, rk_ref, bd_ref, o_ref):
    bd = bd_ref[...]
    y = y_ref[...]
    mean = _seg_sum(y, bd) * (1.0 / HEAD_DIM)
    yc = y - mean
    var = _seg_sum(yc * yc, bd) * (1.0 / HEAD_DIM)
    yn = yc * lax.rsqrt(var + D_GN_EPS) * lnw_ref[...] + lnb_ref[...]
    bonus = _seg_sum(r_ref[...] * k_ref[...] * rk_ref[...], bd) * v_ref[...]
    o_ref[...] = ((yn + bonus) * g_ref[...]).astype(o_ref.dtype)


def _rwkv_post(y, r, k, v, g, p, l, o_all):
    m, bw = y.shape
    tm = _tile(m, 512)
    rows = pl.BlockSpec((tm, bw), lambda i: (i, 0))
    vec = _layer_vec(l, bw)
    return _mixer_call(
        _rwkv_post_kernel, o_all, m,
        grid=(m // tm,),
        in_specs=[rows] * 5 + [vec] * 3 + [pl.BlockSpec((bw, bw), lambda i: (0, 0))],
        out_block=(tm, bw), out_index=lambda i: (i, N_BRANCH - 1),
        sem=("parallel",), name="rwkv_post",
        args=(y, r, k, v, g, _vec3(p["d_ln_w"]), _vec3(p["d_ln_b"]), _vec3(p["d_r_k"]), _ones_blockdiag(bw)))


def _merge_kernel(o_ref, wbr_ref, h_ref, wg_ref, out_ref, acc_ref):
    n = pl.program_id(2)

    @pl.when(n == 0)
    def _():
        acc_ref[...] = jnp.zeros_like(acc_ref)

    u = jnp.dot(o_ref[...], wbr_ref[...].astype(BF16), preferred_element_type=F32)
    gate = jax.nn.sigmoid(jnp.dot(h_ref[...], wg_ref[...], preferred_element_type=F32))
    acc_ref[...] += u * gate

    @pl.when(n == N_BRANCH - 1)
    def _():
        out_ref[...] = acc_ref[...].astype(out_ref.dtype)


def _merge(o_all, w_br_all, h, w_pack, l):
    m, d = h.shape
    tm, tn = _tile(m, 2304), 512
    nj = d // tn
    g0 = S_GATE * LANE // tn
    return pl.pallas_call(
        _merge_kernel,
        out_shape=jax.ShapeDtypeStruct((m, d), BF16),
        grid=(m // tm, nj, N_BRANCH),
        in_specs=[pl.BlockSpec((tm, BRANCH_WIDTH), lambda i, j, n: (i, n)),
                  pl.BlockSpec((None, None, BRANCH_WIDTH, tn), lambda i, j, n: (l, n, 0, j)),
                  pl.BlockSpec((tm, d), lambda i, j, n: (i, 0)),
                  pl.BlockSpec((None, d, tn), lambda i, j, n: (l, 0, g0 + n * nj + j))],
        out_specs=pl.BlockSpec((tm, tn), lambda i, j, n: (i, j)),
        scratch_shapes=[pltpu.VMEM((tm, tn), F32)],
        compiler_params=_cparams("parallel", "parallel", "arbitrary"),
        name="branch_merge",
    )(o_all, w_br_all, h, w_pack)


def _ple_kernel(x_ref, pe_ref, wple_ref, h_ref, wg_ref, o_ref):
    emb = jnp.dot(pe_ref[...].astype(BF16), wple_ref[...].astype(BF16), preferred_element_type=F32)
    gate = jax.nn.sigmoid(jnp.dot(h_ref[...], wg_ref[...].astype(BF16), preferred_element_type=F32))
    o_ref[...] = x_ref[...] + emb * gate


def _ple(x, pe, w_ple_all, h, w_gate_all, l):
    m, d = x.shape
    pd = pe.shape[1]
    tm, tn = _tile(m, 1536), 512
    return pl.pallas_call(
        _ple_kernel,
        out_shape=jax.ShapeDtypeStruct((m, d), F32),
        grid=(m // tm, d // tn),
        in_specs=[pl.BlockSpec((tm, tn), lambda i, j: (i, j)),
                  pl.BlockSpec((tm, pd), lambda i, j: (i, 0)),
                  pl.BlockSpec((None, pd, tn), lambda i, j: (l, 0, j)),
                  pl.BlockSpec((tm, d), lambda i, j: (i, 0)),
                  pl.BlockSpec((None, d, tn), lambda i, j: (l, 0, j))],
        out_specs=pl.BlockSpec((tm, tn), lambda i, j: (i, j)),
        compiler_params=_cparams("parallel", "parallel"),
        name="ple",
    )(x, pe, w_ple_all, h, w_gate_all)


def _layer(l, x, pe, p, t, nb, ts, tabs):
    m = x.shape[0]
    lam_init = 0.8 - 0.6 * math.exp(-0.3 * l)
    dc = p["d_mu"].shape[1]
    d = x.shape[1]
    lw = {"a_qn": p["a_q_norm"][l], "a_kn": p["a_k_norm"][l], "b_qn": p["b_q_norm"][l], "b_kn": p["b_k_norm"][l],
          "c_qn": p["c_q_norm"][l], "c_kn": p["c_k_norm"][l]}

    h = _rmsnorm(x, p["norm1_g"], l)
    w_pack = p["w_pack"]
    z, zb = _qkv_projection(h, w_pack, l, *_column_vectors(lw), *tabs)
    zd = _matmul(h, w_pack, lambda acc: acc, F32, n=dc, layer=l, col0=S_D * LANE, tn=256, name="rwkv_projection")

    o_all = _dsa_prompt(z, zb, t, None, m)
    o_all = _dsa_sample(z, zb, p["cache_a_k"], p["cache_a_v"], p["cache_a_kidx"], l, t, nb, ts, o_all, m)
    tq_b = 128
    bias = _band_bias(p["b_rel_bias"][l], tq_b)
    pb = p["cache_b_k"].shape[2]
    generic = bias[B_WINDOW // tq_b]
    o_all = _band_prompt(z, zb, bias, t, tq_b, o_all, m)
    o_all = _band_sample(z, zb, p["cache_b_k"], p["cache_b_v"], generic[:, :ts, B_WINDOW - pb:B_WINDOW],
                         generic[:, :ts, B_WINDOW:B_WINDOW + ts], l, t, nb, ts, o_all, m)
    c_on = p["c_out_norm"]
    o_all = _diff_prompt(z, zb, p["c_lambda"], c_on, l, t, lam_init, o_all, m)
    o_all = _diff_sample(z, zb, p["cache_c_k"], p["cache_c_v"], p["c_lambda"], c_on, l, t, nb, ts, lam_init,
                         o_all, m)

    zfirst = jnp.broadcast_to(p["state_d_shift"][l], (nb, ts, dc)).reshape(nb * ts, dc)
    r, w, k, v, kk, b, g = _rwkv_pre(zd, zfirst, p, l, t, ts)
    ops = (r, w, k, v, kk, b)
    y_p, wkv_p = _rwkv_chunked(ops, None, 0, 1, t)
    y_s, wkv_s = _rwkv_chunked(ops, p["state_d_wkv"][l], t, nb, ts)
    o_all = _rwkv_post(jnp.concatenate([y_p, y_s], axis=0), r, k, v, g, p, l, o_all)

    ug = _merge(o_all, p["w_branch"], h, w_pack, l)
    res = lambda acc, r_: r_ + acc
    x = _matmul(ug, p["w_out"], res, F32, n=d, layer=l, residual=x, name="out_proj")
    h2 = _rmsnorm(x, p["norm2_g"], l)
    ffn = p["w_up"].shape[2]
    up = _matmul(h2, p["w_up"], lambda acc: jnp.square(jnp.maximum(acc, 0.0)), BF16, n=ffn, layer=l, tm=2304,
                 name="mlp_up")
    x = _matmul(up, _cast_bf16(p["w_down"], l), res, F32, n=d, residual=x, tm=1024, tn=1024, name="mlp_down")
    h3 = _rmsnorm(x, p["norm3_g"], l)
    x = _ple(x, pe, p["w_ple"], h3, p["w_ple_gate"], l)

    slot = lambda rows, s, n=1: z[rows, s * LANE:(s + n) * LANE]

    def rows_of(sl, lead):
        a = lambda s, n, shape: slot(sl, s, n).reshape(lead + shape)
        ak = a(S_AK, 1, (A_KV_HEADS, HEAD_DIM))
        av = a(S_AV, 1, (A_KV_HEADS, HEAD_DIM))
        aik = slot(sl, S_AIK)[:, :HEAD_DIM].reshape(lead + (HEAD_DIM,))
        bk = a(S_BK, 4, (B_HEADS, HEAD_DIM))
        bv = a(S_BV, 4, (B_HEADS, HEAD_DIM))
        ck = a(S_CK, 4, (C_HEADS, 2, HEAD_DIM))
        cv = a(S_CV, 4, (C_HEADS, 2 * HEAD_DIM))
        return ak, av, aik, bk, bv, ck, cv

    keep = min(B_WINDOW, t)
    pak, pav, paik, pbk, pbv, pck, pcv = rows_of(slice(0, t), (1, t))
    new_p = (pak, pav, paik, pbk[:, t - keep:], pbv[:, t - keep:], pck, pcv, wkv_p, zd[t - 1:t].reshape(1, 1, dc))
    new_s = rows_of(slice(t, m), (nb, ts)) + (wkv_s, zd[t:].reshape(nb, ts, dc)[:, -1:])
    return x, new_p, new_s


def kernel(x_prompt, x_sample, cache_a_k, cache_a_v, cache_a_kidx, cache_b_k, cache_b_v, cache_c_k, cache_c_v, state_d_wkv, state_d_shift, p_prompt, p_sample, norm1_g, w_in, a_q_norm, a_k_norm, b_q_norm, b_k_norm, b_rel_bias, c_q_norm, c_k_norm, c_lambda, c_out_norm, d_mu, d_w0, d_w2, d_a0, d_a2, d_g2, d_k_k, d_k_a, d_r_k, d_ln_w, d_ln_b, w_branch, w_out, norm2_g, w_up, w_down, norm3_g, w_ple, w_ple_gate):
    batch, t, d = x_prompt.shape
    nb, ts, _ = x_sample.shape
    past = cache_a_k.shape[2]
    depth = w_in.shape[0]
    assert batch == 1 and t % 512 == 0 and past % CHUNK == 0 and ts <= CHUNK and (nb * ts) % 8 == 0
    p = dict(cache_a_k=cache_a_k, cache_a_v=cache_a_v, cache_a_kidx=cache_a_kidx, cache_b_k=cache_b_k,
             cache_b_v=cache_b_v, cache_c_k=cache_c_k, cache_c_v=cache_c_v, state_d_wkv=state_d_wkv,
             state_d_shift=state_d_shift, norm1_g=norm1_g, w_in=w_in, a_q_norm=a_q_norm, a_k_norm=a_k_norm,
             b_q_norm=b_q_norm, b_k_norm=b_k_norm, b_rel_bias=b_rel_bias, c_q_norm=c_q_norm, c_k_norm=c_k_norm,
             c_lambda=c_lambda, c_out_norm=c_out_norm, d_mu=d_mu, d_w0=d_w0, d_w2=d_w2, d_a0=d_a0, d_a2=d_a2,
             d_g2=d_g2, d_k_k=d_k_k, d_k_a=d_k_a, d_r_k=d_r_k, d_ln_w=d_ln_w, d_ln_b=d_ln_b, w_branch=w_branch,
             w_out=w_out, norm2_g=norm2_g, w_up=w_up, w_down=w_down, norm3_g=norm3_g, w_ple=w_ple,
             w_ple_gate=w_ple_gate)
    p["w_pack"] = _pack_w_in(w_in)
    x = jnp.concatenate([x_prompt[0], x_sample.reshape(nb * ts, d)], axis=0)
    pos = jnp.concatenate([jnp.arange(t, dtype=jnp.int32),
                           jnp.tile(past + jnp.arange(ts, dtype=jnp.int32), nb)])
    tabs = _rope_tables(pos)
    st_p = [[] for _ in range(9)]
    st_s = [[] for _ in range(9)]
    for l in range(depth):
        pe = jnp.concatenate([p_prompt[l, 0], p_sample[l].reshape(nb * ts, -1)], axis=0)
        x, new_p, new_s = _layer(l, x, pe, p, t, nb, ts, tabs)
        for lst, arr in zip(st_p, new_p):
            lst.append(arr)
        for lst, arr in zip(st_s, new_s):
            lst.append(arr)
    outs_p = [jnp.stack(s, axis=0) for s in st_p]
    outs_s = [jnp.stack(s, axis=0) for s in st_s]
    return (x[:t].reshape(1, t, d), x[t:].reshape(nb, ts, d), *outs_p, *outs_s)
```
